```python
import jax, jax.numpy as jnp
from jax import lax
import numpy as np

D_MODEL = 1024
BATCH = 8
SEQ = 8192
DEPTH = 1

N_ATTN_HEADS = 8
ATTN_HEAD_DIM = 64
ATTN_WIDTH = N_ATTN_HEADS * ATTN_HEAD_DIM
N_SGU_GROUPS = 8
SGU_WIDTH = D_MODEL // 2
SGU_GROUP_DIM = SGU_WIDTH // N_SGU_GROUPS
CHUNK = 128
Q_BLOCK = 128
D_FF = 4 * D_MODEL
N_BRANCH = 2
EPS = 1e-6
IN_SPLITS = (2 * SGU_WIDTH, ATTN_WIDTH, ATTN_WIDTH, ATTN_WIDTH, N_ATTN_HEADS, N_BRANCH * D_MODEL)
IN_WIDTH = sum(IN_SPLITS)
IN_OFFSETS = tuple(int(o) for o in np.cumsum(IN_SPLITS)[:-1])

kernel_name = "hybrid_gmlp_fox_gated_block"


def rmsnorm(x, g):
    x32 = x.astype(jnp.float32)
    y = x32 * lax.rsqrt(jnp.mean(x32 * x32, axis=-1, keepdims=True) + EPS)
    return y.astype(x.dtype) * g


def layernorm(x, g, b):
    x32 = x.astype(jnp.float32)
    mu = jnp.mean(x32, axis=-1, keepdims=True)
    xc = x32 - mu
    y = xc * lax.rsqrt(jnp.mean(xc * xc, axis=-1, keepdims=True) + EPS)
    return y.astype(x.dtype) * g + b


def chunked_sgu(z, g_sgu, b_sgu, w_spatial, b_spatial):
    B, S, _ = z.shape
    u, v = z[..., :SGU_WIDTH], z[..., SGU_WIDTH:]
    v = layernorm(v, g_sgu, b_sgu)
    vc = v.reshape(B, S // CHUNK, CHUNK, N_SGU_GROUPS, SGU_GROUP_DIM)
    causal = jnp.tril(jnp.ones((CHUNK, CHUNK), dtype=bool))
    ws = jnp.where(causal[None], w_spatial, jnp.zeros_like(w_spatial))
    s = jnp.einsum('gts,bcsgd->bctgd', ws, vc)
    s = s + b_spatial.T[None, None, :, :, None]
    return u * s.reshape(B, S, SGU_WIDTH)


def forgetting_attention(q, k, v, cum):
    B, H, S, d = q.shape
    nb = S // Q_BLOCK
    qb = q.reshape(B, H, nb, Q_BLOCK, d).transpose(2, 0, 1, 3, 4)
    cb = cum.reshape(B, H, nb, Q_BLOCK).transpose(2, 0, 1, 3)
    key_pos = jnp.arange(S)
    scale = d ** -0.5

    def one_block(args):
        q_blk, c_blk, i = args
        s = jnp.einsum('bhqd,bhkd->bhqk', q_blk, k).astype(jnp.float32) * scale
        s = s + c_blk[..., :, None] - cum[..., None, :]
        q_pos = i * Q_BLOCK + jnp.arange(Q_BLOCK)
        mask = key_pos[None, :] <= q_pos[:, None]
        s = jnp.where(mask, s, -jnp.inf)
        p = jax.nn.softmax(s, axis=-1)
        return jnp.einsum('bhqk,bhkd->bhqd', p.astype(v.dtype), v)

    out = lax.map(one_block, (qb, cb, jnp.arange(nb)))
    return out.transpose(1, 2, 0, 3, 4).reshape(B, H, S, d)


def _fwd_setup_inputs(seed: int = 0) -> dict:
    key = jax.random.key(seed)
    ks = jax.random.split(key, 20)
    L = DEPTH

    def nrm(k, shape, scale):
        return jax.random.normal(k, shape, jnp.float32) * scale

    def gain(k, shape):
        return 1.0 + 0.05 * jax.random.normal(k, shape, jnp.float32)

    return {
        "x": jax.random.normal(ks[0], (BATCH, SEQ, D_MODEL), jnp.float32),
        "g_mix_pre": gain(ks[1], (L, D_MODEL)),
        "w_in": nrm(ks[2], (L, D_MODEL, IN_WIDTH), D_MODEL ** -0.5),
        "b_forget": 2.0 + 0.5 * jax.random.normal(ks[3], (L, N_ATTN_HEADS), jnp.float32),
        "g_sgu": gain(ks[4], (L, SGU_WIDTH)),
        "b_sgu": nrm(ks[5], (L, SGU_WIDTH), 0.02),
        "w_spatial": nrm(ks[6], (L, N_SGU_GROUPS, CHUNK, CHUNK), CHUNK ** -0.5),
        "b_spatial": 1.0 + 0.1 * jax.random.normal(ks[7], (L, N_SGU_GROUPS, CHUNK), jnp.float32),
        "w_branch_sgu": nrm(ks[8], (L, SGU_WIDTH, D_MODEL), SGU_WIDTH ** -0.5),
        "w_branch_attn": nrm(ks[9], (L, ATTN_WIDTH, D_MODEL), ATTN_WIDTH ** -0.5),
        "w_out": nrm(ks[10], (L, D_MODEL, D_MODEL), D_MODEL ** -0.5),
        "g_mix_post": gain(ks[11], (L, D_MODEL)),
        "g_ffn_pre": gain(ks[12], (L, D_MODEL)),
        "w_up": nrm(ks[13], (L, D_MODEL, D_FF), D_MODEL ** -0.5),
        "w_down": nrm(ks[14], (L, D_FF, D_MODEL), D_FF ** -0.5),
        "g_ffn_post": gain(ks[15], (L, D_MODEL)),
    }


def _fwd_reference(x, g_mix_pre, w_in, b_forget, g_sgu, b_sgu, w_spatial, b_spatial,
              w_branch_sgu, w_branch_attn, w_out, g_mix_post, g_ffn_pre, w_up, w_down,
              g_ffn_post):
    B, S, _ = x.shape
    h = x
    for l in range(DEPTH):
        xn = rmsnorm(h, g_mix_pre[l])
        proj = xn @ w_in[l]
        z_sgu, q, k, v, f_logit, gate_logit = jnp.split(proj, IN_OFFSETS, axis=-1)

        y_sgu = chunked_sgu(jax.nn.gelu(z_sgu), g_sgu[l], b_sgu[l], w_spatial[l], b_spatial[l])

        def heads(t):
            return t.reshape(B, S, N_ATTN_HEADS, ATTN_HEAD_DIM).transpose(0, 2, 1, 3)
        log_f = jax.nn.log_sigmoid((f_logit + b_forget[l]).astype(jnp.float32))
        cum = jnp.cumsum(log_f, axis=1).transpose(0, 2, 1)
        y_attn = forgetting_attention(heads(q), heads(k), heads(v), cum)
        y_attn = y_attn.transpose(0, 2, 1, 3).reshape(B, S, ATTN_WIDTH)

        gates = jax.nn.sigmoid(gate_logit)
        merged = (gates[..., :D_MODEL] * (y_sgu @ w_branch_sgu[l])
                  + gates[..., D_MODEL:] * (y_attn @ w_branch_attn[l]))
        h = h + rmsnorm(merged @ w_out[l], g_mix_post[l])

        xn2 = rmsnorm(h, g_ffn_pre[l])
        hid = jnp.square(jax.nn.relu(xn2 @ w_up[l]))
        h = h + rmsnorm(hid @ w_down[l], g_ffn_post[l])
    return h


import jax as _jax
import jax.numpy as _jnp

TWIN_FORMAT = 'train_step'
FWD_PARAMS = ['x', 'g_mix_pre', 'w_in', 'b_forget', 'g_sgu', 'b_sgu', 'w_spatial', 'b_spatial', 'w_branch_sgu', 'w_branch_attn', 'w_out', 'g_mix_post', 'g_ffn_pre', 'w_up', 'w_down', 'g_ffn_post']
TWIN_WEIGHTS = ['g_mix_pre', 'w_in', 'b_forget', 'g_sgu', 'b_sgu', 'w_spatial', 'b_spatial', 'w_branch_sgu', 'w_branch_attn', 'w_out', 'g_mix_post', 'g_ffn_pre', 'w_up', 'w_down', 'g_ffn_post']
TWIN_DIFF_INPUT = 'x'
TWIN_INPUTS = ['x', 'g_mix_pre', 'w_in', 'b_forget', 'g_sgu', 'b_sgu', 'w_spatial', 'b_spatial', 'w_branch_sgu', 'w_branch_attn', 'w_out', 'g_mix_post', 'g_ffn_pre', 'w_up', 'w_down', 'g_ffn_post', 'loss_target', 'm_g_mix_pre', 'm_w_in', 'm_b_forget', 'm_g_sgu', 'm_b_sgu', 'm_w_spatial', 'm_b_spatial', 'm_w_branch_sgu', 'm_w_branch_attn', 'm_w_out', 'm_g_mix_post', 'm_g_ffn_pre', 'm_w_up', 'm_w_down', 'm_g_ffn_post', 'v_g_mix_pre', 'v_w_in', 'v_b_forget', 'v_g_sgu', 'v_b_sgu', 'v_w_spatial', 'v_b_spatial', 'v_w_branch_sgu', 'v_w_branch_attn', 'v_w_out', 'v_g_mix_post', 'v_g_ffn_pre', 'v_w_up', 'v_w_down', 'v_g_ffn_post']
TWIN_OUTPUTS = ['loss', 'grad_x', 'grad_g_mix_pre', 'grad_w_in', 'grad_b_forget', 'grad_g_sgu', 'grad_b_sgu', 'grad_w_spatial', 'grad_b_spatial', 'grad_w_branch_sgu', 'grad_w_branch_attn', 'grad_w_out', 'grad_g_mix_post', 'grad_g_ffn_pre', 'grad_w_up', 'grad_w_down', 'grad_g_ffn_post', 'delta_g_mix_pre', 'delta_w_in', 'delta_b_forget', 'delta_g_sgu', 'delta_b_sgu', 'delta_w_spatial', 'delta_b_spatial', 'delta_w_branch_sgu', 'delta_w_branch_attn', 'delta_w_out', 'delta_g_mix_post', 'delta_g_ffn_pre', 'delta_w_up', 'delta_w_down', 'delta_g_ffn_post', 'new_m_g_mix_pre', 'new_m_w_in', 'new_m_b_forget', 'new_m_g_sgu', 'new_m_b_sgu', 'new_m_w_spatial', 'new_m_b_spatial', 'new_m_w_branch_sgu', 'new_m_w_branch_attn', 'new_m_w_out', 'new_m_g_mix_post', 'new_m_g_ffn_pre', 'new_m_w_up', 'new_m_w_down', 'new_m_g_ffn_post', 'new_v_g_mix_pre', 'new_v_w_in', 'new_v_b_forget', 'new_v_g_sgu', 'new_v_b_sgu', 'new_v_w_spatial', 'new_v_b_spatial', 'new_v_w_branch_sgu', 'new_v_w_branch_attn', 'new_v_w_out', 'new_v_g_mix_post', 'new_v_g_ffn_pre', 'new_v_w_up', 'new_v_w_down', 'new_v_g_ffn_post']
TWIN_LEAF_KINDS = {'loss': 'loss', 'grad_x': 'grad_x', 'grad_g_mix_pre': 'grad_w', 'grad_w_in': 'grad_w', 'grad_b_forget': 'grad_w', 'grad_g_sgu': 'grad_w', 'grad_b_sgu': 'grad_w', 'grad_w_spatial': 'grad_w', 'grad_b_spatial': 'grad_w', 'grad_w_branch_sgu': 'grad_w', 'grad_w_branch_attn': 'grad_w', 'grad_w_out': 'grad_w', 'grad_g_mix_post': 'grad_w', 'grad_g_ffn_pre': 'grad_w', 'grad_w_up': 'grad_w', 'grad_w_down': 'grad_w', 'grad_g_ffn_post': 'grad_w', 'delta_g_mix_pre': 'delta_w', 'delta_w_in': 'delta_w', 'delta_b_forget': 'delta_w', 'delta_g_sgu': 'delta_w', 'delta_b_sgu': 'delta_w', 'delta_w_spatial': 'delta_w', 'delta_b_spatial': 'delta_w', 'delta_w_branch_sgu': 'delta_w', 'delta_w_branch_attn': 'delta_w', 'delta_w_out': 'delta_w', 'delta_g_mix_post': 'delta_w', 'delta_g_ffn_pre': 'delta_w', 'delta_w_up': 'delta_w', 'delta_w_down': 'delta_w', 'delta_g_ffn_post': 'delta_w', 'new_m_g_mix_pre': 'new_m', 'new_m_w_in': 'new_m', 'new_m_b_forget': 'new_m', 'new_m_g_sgu': 'new_m', 'new_m_b_sgu': 'new_m', 'new_m_w_spatial': 'new_m', 'new_m_b_spatial': 'new_m', 'new_m_w_branch_sgu': 'new_m', 'new_m_w_branch_attn': 'new_m', 'new_m_w_out': 'new_m', 'new_m_g_mix_post': 'new_m', 'new_m_g_ffn_pre': 'new_m', 'new_m_w_up': 'new_m', 'new_m_w_down': 'new_m', 'new_m_g_ffn_post': 'new_m', 'new_v_g_mix_pre': 'new_v', 'new_v_w_in': 'new_v', 'new_v_b_forget': 'new_v', 'new_v_g_sgu': 'new_v', 'new_v_b_sgu': 'new_v', 'new_v_w_spatial': 'new_v', 'new_v_b_spatial': 'new_v', 'new_v_w_branch_sgu': 'new_v', 'new_v_w_branch_attn': 'new_v', 'new_v_w_out': 'new_v', 'new_v_g_mix_post': 'new_v', 'new_v_g_ffn_pre': 'new_v', 'new_v_w_up': 'new_v', 'new_v_w_down': 'new_v', 'new_v_g_ffn_post': 'new_v'}


def _forward(args):
    return _fwd_reference(*[args[k] for k in FWD_PARAMS])


def _output_shape():
    def fwd():
        inp = _fwd_setup_inputs(0)
        return _fwd_reference(*[inp[k] for k in FWD_PARAMS])
    out = _jax.eval_shape(fwd)
    return out.shape, out.dtype

N_MICROBATCH = 1
ADAM_LR = 0.001
ADAM_B1 = 0.9
ADAM_B2 = 0.999
ADAM_EPS = 1e-08
ADAM_WD = 0.01
ADAM_STEP = 10
PER_EXAMPLE_BATCH_AXIS = {'x': 0, 'loss_target': 0}
SHARED_INPUTS = []
_WEIGHT_DTYPES = {'g_mix_pre': _jnp.float32, 'w_in': _jnp.float32, 'b_forget': _jnp.float32, 'g_sgu': _jnp.float32, 'b_sgu': _jnp.float32, 'w_spatial': _jnp.float32, 'b_spatial': _jnp.float32, 'w_branch_sgu': _jnp.float32, 'w_branch_attn': _jnp.float32, 'w_out': _jnp.float32, 'g_mix_post': _jnp.float32, 'g_ffn_pre': _jnp.float32, 'w_up': _jnp.float32, 'w_down': _jnp.float32, 'g_ffn_post': _jnp.float32}
MOMENT_SCALE = {'g_mix_pre': 1.252545e+00, 'w_in': 5.465075e-01, 'b_forget': 5.100056e+00, 'g_sgu': 7.114826e-01, 'b_sgu': 5.212439e-01, 'w_spatial': 2.800294e-01, 'b_spatial': 5.293327e-01, 'w_branch_sgu': 8.002181e+00, 'w_branch_attn': 4.605937e-01, 'w_out': 8.174414e+00, 'g_mix_post': 6.470121e+01, 'g_ffn_pre': 2.842986e+00, 'w_up': 1.481120e+00, 'w_down': 8.000404e+00, 'g_ffn_post': 6.625816e+01}


def _to_microbatches(a, axis):
    t = _jnp.moveaxis(a, axis, 0)
    t = t.reshape((N_MICROBATCH, t.shape[0] // N_MICROBATCH) + t.shape[1:])
    return _jnp.moveaxis(t, 1, axis + 1)


def setup_inputs(seed: int = 0) -> dict:
    inp = _fwd_setup_inputs(seed)
    key = _jax.random.fold_in(_jax.random.key(seed), 7919)
    shape, _ = _output_shape()
    out = dict(inp)
    out["loss_target"] = _jax.random.normal(_jax.random.fold_in(key, 0), shape, _jnp.float32)
    for i, name in enumerate(TWIN_WEIGHTS):
        w = inp[name].astype(_jnp.float32)
        if MOMENT_SCALE is None:
            s = _jnp.sqrt(_jnp.mean(_jnp.square(w)) + 1e-30)
        else:
            s = MOMENT_SCALE[name]
        km, kv = _jax.random.split(_jax.random.fold_in(key, i + 1))
        out[name] = w
        out["m_" + name] = s * _jax.random.normal(km, w.shape, _jnp.float32)
        out["v_" + name] = (s * s) * _jax.random.uniform(kv, w.shape, _jnp.float32, 0.5, 1.5)
    if N_MICROBATCH > 1:
        for name, axis in PER_EXAMPLE_BATCH_AXIS.items():
            out[name] = _to_microbatches(out[name], axis)
    return {'x': out['x'], 'g_mix_pre': out['g_mix_pre'], 'w_in': out['w_in'], 'b_forget': out['b_forget'], 'g_sgu': out['g_sgu'], 'b_sgu': out['b_sgu'], 'w_spatial': out['w_spatial'], 'b_spatial': out['b_spatial'], 'w_branch_sgu': out['w_branch_sgu'], 'w_branch_attn': out['w_branch_attn'], 'w_out': out['w_out'], 'g_mix_post': out['g_mix_post'], 'g_ffn_pre': out['g_ffn_pre'], 'w_up': out['w_up'], 'w_down': out['w_down'], 'g_ffn_post': out['g_ffn_post'], 'loss_target': out['loss_target'], 'm_g_mix_pre': out['m_g_mix_pre'], 'm_w_in': out['m_w_in'], 'm_b_forget': out['m_b_forget'], 'm_g_sgu': out['m_g_sgu'], 'm_b_sgu': out['m_b_sgu'], 'm_w_spatial': out['m_w_spatial'], 'm_b_spatial': out['m_b_spatial'], 'm_w_branch_sgu': out['m_w_branch_sgu'], 'm_w_branch_attn': out['m_w_branch_attn'], 'm_w_out': out['m_w_out'], 'm_g_mix_post': out['m_g_mix_post'], 'm_g_ffn_pre': out['m_g_ffn_pre'], 'm_w_up': out['m_w_up'], 'm_w_down': out['m_w_down'], 'm_g_ffn_post': out['m_g_ffn_post'], 'v_g_mix_pre': out['v_g_mix_pre'], 'v_w_in': out['v_w_in'], 'v_b_forget': out['v_b_forget'], 'v_g_sgu': out['v_g_sgu'], 'v_b_sgu': out['v_b_sgu'], 'v_w_spatial': out['v_w_spatial'], 'v_b_spatial': out['v_b_spatial'], 'v_w_branch_sgu': out['v_w_branch_sgu'], 'v_w_branch_attn': out['v_w_branch_attn'], 'v_w_out': out['v_w_out'], 'v_g_mix_post': out['v_g_mix_post'], 'v_g_ffn_pre': out['v_g_ffn_pre'], 'v_w_up': out['v_w_up'], 'v_w_down': out['v_w_down'], 'v_g_ffn_post': out['v_g_ffn_post']}


def _loss(weights, diff, rest, loss_target):
    with _jax.named_scope("forward"):
        args = {**rest, TWIN_DIFF_INPUT: diff, **{k: w.astype(_WEIGHT_DTYPES[k]) for k, w in weights.items()}}
        y = _forward(args)
    with _jax.named_scope("loss_head"):
        err = _jnp.square(y.astype(_jnp.float32) - loss_target)
        return 0.5 * _jnp.sum(_jnp.mean(err, axis=-1)) if err.ndim else 0.5 * err


def _adamw(w, g, m, v):
    m = ADAM_B1 * m + (1.0 - ADAM_B1) * g
    v = ADAM_B2 * v + (1.0 - ADAM_B2) * _jnp.square(g)
    m_hat = m / (1.0 - ADAM_B1 ** ADAM_STEP)
    v_hat = v / (1.0 - ADAM_B2 ** ADAM_STEP)
    delta = -ADAM_LR * (m_hat / (_jnp.sqrt(v_hat) + ADAM_EPS) + ADAM_WD * w)
    return delta, m, v


def reference(x, g_mix_pre, w_in, b_forget, g_sgu, b_sgu, w_spatial, b_spatial, w_branch_sgu, w_branch_attn, w_out, g_mix_post, g_ffn_pre, w_up, w_down, g_ffn_post, loss_target, m_g_mix_pre, m_w_in, m_b_forget, m_g_sgu, m_b_sgu, m_w_spatial, m_b_spatial, m_w_branch_sgu, m_w_branch_attn, m_w_out, m_g_mix_post, m_g_ffn_pre, m_w_up, m_w_down, m_g_ffn_post, v_g_mix_pre, v_w_in, v_b_forget, v_g_sgu, v_b_sgu, v_w_spatial, v_b_spatial, v_w_branch_sgu, v_w_branch_attn, v_w_out, v_g_mix_post, v_g_ffn_pre, v_w_up, v_w_down, v_g_ffn_post):
    given = dict(x=x, g_mix_pre=g_mix_pre, w_in=w_in, b_forget=b_forget, g_sgu=g_sgu, b_sgu=b_sgu, w_spatial=w_spatial, b_spatial=b_spatial, w_branch_sgu=w_branch_sgu, w_branch_attn=w_branch_attn, w_out=w_out, g_mix_post=g_mix_post, g_ffn_pre=g_ffn_pre, w_up=w_up, w_down=w_down, g_ffn_post=g_ffn_post, loss_target=loss_target, m_g_mix_pre=m_g_mix_pre, m_w_in=m_w_in, m_b_forget=m_b_forget, m_g_sgu=m_g_sgu, m_b_sgu=m_b_sgu, m_w_spatial=m_w_spatial, m_b_spatial=m_b_spatial, m_w_branch_sgu=m_w_branch_sgu, m_w_branch_attn=m_w_branch_attn, m_w_out=m_w_out, m_g_mix_post=m_g_mix_post, m_g_ffn_pre=m_g_ffn_pre, m_w_up=m_w_up, m_w_down=m_w_down, m_g_ffn_post=m_g_ffn_post, v_g_mix_pre=v_g_mix_pre, v_w_in=v_w_in, v_b_forget=v_b_forget, v_g_sgu=v_g_sgu, v_b_sgu=v_b_sgu, v_w_spatial=v_w_spatial, v_b_spatial=v_b_spatial, v_w_branch_sgu=v_w_branch_sgu, v_w_branch_attn=v_w_branch_attn, v_w_out=v_w_out, v_g_mix_post=v_g_mix_post, v_g_ffn_pre=v_g_ffn_pre, v_w_up=v_w_up, v_w_down=v_w_down, v_g_ffn_post=v_g_ffn_post)
    weights = {n: given[n] for n in TWIN_WEIGHTS}
    shared = {n: given[n] for n in SHARED_INPUTS}
    per_example = {n: given[n] for n in ['x']}
    grad_fn = _jax.value_and_grad(_loss, argnums=(0, 1))

    def one_microbatch(ex, loss_target):
        ex = dict(ex)
        diff = ex.pop(TWIN_DIFF_INPUT)
        return grad_fn(weights, diff, {**shared, **ex}, loss_target)

    if N_MICROBATCH == 1:
        loss, (grad_w, grad_x) = one_microbatch(per_example, given["loss_target"])
    else:
        def body(carry, xs):
            loss_sum, grad_sum = carry
            l_k, (gw_k, gx_k) = one_microbatch(xs[0], xs[1])
            with _jax.named_scope("update"):
                return (loss_sum + l_k, _jax.tree.map(_jnp.add, grad_sum, gw_k)), gx_k

        init = (_jnp.zeros((), _jnp.float32), _jax.tree.map(_jnp.zeros_like, weights))
        (loss, grad_w), grad_x = _jax.lax.scan(body, init, (per_example, given["loss_target"]))
    with _jax.named_scope("update"):
        delta_w, new_m, new_v = {}, {}, {}
        for n in TWIN_WEIGHTS:
            delta_w[n], new_m[n], new_v[n] = _adamw(weights[n], grad_w[n], given["m_" + n], given["v_" + n])
    return (loss, grad_x, *[grad_w[n] for n in TWIN_WEIGHTS], *[delta_w[n] for n in TWIN_WEIGHTS],
            *[new_m[n] for n in TWIN_WEIGHTS], *[new_v[n] for n in TWIN_WEIGHTS])
```

```python
import jax
import jax.numpy as jnp
from jax import lax
from jax.experimental import pallas as pl
from jax.experimental.pallas import tpu as pltpu

F32 = jnp.float32
BF16 = jnp.bfloat16
HIGHEST = lax.Precision.HIGHEST
MESH = pl.DeviceIdType.MESH

D_MODEL = 1024
SGU_WIDTH = 512
ATTN_WIDTH = 512
N_HEADS = 8
CHUNK = 128
D_FF = 4096
IN_WIDTH = 4616
N_DEV = 8
IN_SHARD = IN_WIDTH // N_DEV
IN_SHARD_PAD = 640
ZQKV_WIDTH = 2 * SGU_WIDTH + 3 * ATTN_WIDTH
GATE_OFFSET = ZQKV_WIDTH + N_HEADS
EPS = 1e-6
LANES = 128
SUBLANES = 8
VMEM_BYTES = 64 * 1024 * 1024
MIB = 1024 * 1024

ADAM_LR = 0.001
ADAM_B1 = 0.9
ADAM_B2 = 0.999
ADAM_EPS = 1e-08
ADAM_WD = 0.01
ADAM_STEP = 10

TOKEN_TILE = 256
ATTN_TILE = 256
CUM_TILE = 256
SGU_TILE = 512
WGRAD_TILE = 512
NEG = -1e30

NT_DIMS = (((1,), (1,)), ((), ()))
TN_DIMS = (((0,), (0,)), ((), ()))


def _params(vmem_mb, n_grid=1):
    return pltpu.CompilerParams(
        dimension_semantics=("arbitrary",) * n_grid,
        vmem_limit_bytes=min(vmem_mb * MIB, VMEM_BYTES - 6 * MIB),
    )


def _dot(a, b):
    return jnp.dot(a, b, preferred_element_type=F32)


def _dot_nt(a, b):
    return lax.dot_general(a, b, NT_DIMS, preferred_element_type=F32)


def _dot_tn(a, b):
    return lax.dot_general(a, b, TN_DIMS, preferred_element_type=F32)


def _const_spec(shape):
    nd = len(shape)
    return pl.BlockSpec(shape, lambda *_: (0,) * nd, pipeline_mode=pl.Buffered(1))


def _row_spec(tm, n, col=0):
    return pl.BlockSpec((tm, n), lambda i: (i, col))


def _fold8(v):
    return v.reshape(v.shape[0] // SUBLANES, SUBLANES, v.shape[1]).sum(axis=0)


def _pick(v, lane_iota, k):
    return jnp.sum(jnp.where(lane_iota == k, v, 0.0), axis=1, keepdims=True)


def _iota(shape, dim):
    return lax.broadcasted_iota(jnp.int32, shape, dim)


def _gelu(x):
    c = 0.7978845608028654
    return 0.5 * x * (1.0 + jnp.tanh(c * (x + 0.044715 * x * x * x)))


def _gelu_grad(x):
    c = 0.7978845608028654
    t = jnp.tanh(c * (x + 0.044715 * x * x * x))
    return 0.5 * (1.0 + t) + 0.5 * x * (1.0 - t * t) * (c * (1.0 + 3.0 * 0.044715 * x * x))


def _rms_stats(v):
    r = lax.rsqrt(jnp.mean(v * v, axis=1, keepdims=True) + EPS)
    return r, v * r


def _rms_bwd(dout, vhat, r, g):
    a = dout * g
    return r * (a - vhat * jnp.mean(a * vhat, axis=1, keepdims=True))


def _mesh_pos():
    return lax.axis_index("x"), lax.axis_index("y"), lax.axis_index("c")


def _dev_index(px, py, pc):
    return 4 * px + 2 * py + pc


def _other_chips(x, y):
    return [(1 - x, y), (x, 1 - y), (1 - x, 1 - y)]


def _gather_weights(w_in_p, w_bs, w_ba, w_out, w_up, w_down):
    shards = (w_in_p, w_bs, w_ba, w_out, w_up, w_down)
    n = len(shards)
    out_shapes = (
        jax.ShapeDtypeStruct((N_DEV, D_MODEL, IN_SHARD_PAD), BF16),
        jax.ShapeDtypeStruct((SGU_WIDTH, D_MODEL), BF16),
        jax.ShapeDtypeStruct((ATTN_WIDTH, D_MODEL), BF16),
        jax.ShapeDtypeStruct((D_MODEL, D_MODEL), BF16),
        jax.ShapeDtypeStruct((D_MODEL, D_FF), BF16),
        jax.ShapeDtypeStruct((D_FF, D_MODEL), BF16),
    )

    def view(a, ref, j):
        if a == 0:
            return ref.at[j]
        rows, cols = shards[a].shape
        if cols == D_MODEL:
            return ref.at[pl.ds(pl.multiple_of(j * rows, rows), rows), :]
        return ref.at[:, pl.ds(pl.multiple_of(j * cols, cols), cols)]

    def body(*refs):
        ins, outs, stage = refs[:n], refs[n:2 * n], refs[2 * n:3 * n]
        send_sems, recv_sems, local_sems = refs[3 * n:]
        x, y, c = _mesh_pos()
        me, sibling = (x, y, c), (x, y, 1 - c)
        chips = _other_chips(x, y)

        def copy(a, k, block, to, from_stage=False):
            dst = view(a, outs[a], _dev_index(*block))
            return pltpu.make_async_remote_copy(
                src_ref=stage[a] if from_stage else dst, dst_ref=dst,
                send_sem=send_sems.at[a, k], recv_sem=recv_sems.at[a, k],
                device_id=to, device_id_type=MESH)

        local = []
        for a in range(n):
            stage[a][...] = ins[a][...].astype(BF16)
            mine = pltpu.make_async_copy(stage[a], view(a, outs[a], _dev_index(*me)), local_sems.at[a])
            mine.start()
            local.append(mine)
        sends = []
        for a in range(n):
            first = [copy(a, 0, me, sibling, True)]
            first += [copy(a, 1 + j, me, (*chip, c), True) for j, chip in enumerate(chips)]
            for cp in first:
                cp.start()
            sends += first
        for a in range(n):
            for j, chip in enumerate(chips):
                copy(a, 1 + j, (*chip, c), me).wait_recv()
                fwd = copy(a, 4 + j, (*chip, c), sibling)
                fwd.start()
                sends.append(fwd)
        for a in range(n):
            copy(a, 0, sibling, me).wait_recv()
            for j, chip in enumerate(chips):
                copy(a, 4 + j, (*chip, 1 - c), me).wait_recv()
        for cp in sends:
            cp.wait_send()
        for mine in local:
            mine.wait()

    return pl.pallas_call(
        body,
        out_shape=out_shapes,
        in_specs=[pl.BlockSpec(memory_space=pltpu.VMEM)] * n,
        out_specs=[pl.BlockSpec(memory_space=pl.ANY)] * n,
        scratch_shapes=[pltpu.VMEM(s.shape, BF16) for s in shards]
        + [pltpu.SemaphoreType.DMA((n, 7)), pltpu.SemaphoreType.DMA((n, 7)), pltpu.SemaphoreType.DMA((n,))],
        compiler_params=pltpu.CompilerParams(vmem_limit_bytes=32 * MIB),
        name="gather_weights",
    )(*shards)


def _reduce_scatter_grads(grads):
    n = len(grads)
    shapes = [tuple(g.shape[1:]) for g in grads]
    max_rows = max(s[0] for s in shapes)
    max_cols = max(s[1] for s in shapes)

    def body(*refs):
        ins = refs[:n]
        outs = refs[n:2 * n]
        land1 = refs[2 * n:3 * n]
        stage = refs[3 * n:4 * n]
        land2 = refs[4 * n:5 * n]
        buf_a, buf_b, s1_send, s1_recv, s2_send, s2_recv, lsem = refs[5 * n:]
        x, y, c = _mesh_pos()
        sibling = (x, y, 1 - c)
        chips = _other_chips(x, y) + [(x, y)]

        def s1_copy(a, k):
            cx, cy = chips[k]
            return pltpu.make_async_remote_copy(
                src_ref=ins[a].at[_dev_index(cx, cy, 1 - c)], dst_ref=land1[a].at[k],
                send_sem=s1_send.at[a, k], recv_sem=s1_recv.at[a, k],
                device_id=sibling, device_id_type=MESH)

        def s2_copy(a, k):
            cx, cy = chips[k]
            return pltpu.make_async_remote_copy(
                src_ref=stage[a].at[k], dst_ref=land2[a].at[k],
                send_sem=s2_send.at[a, k], recv_sem=s2_recv.at[a, k],
                device_id=(cx, cy, c), device_id_type=MESH)

        def chip_partial(a, k):
            rows, cols = shapes[a]
            va = buf_a.at[pl.ds(0, rows), pl.ds(0, cols)]
            vb = buf_b.at[pl.ds(0, rows), pl.ds(0, cols)]
            cx, cy = chips[k]
            la = pltpu.make_async_copy(ins[a].at[_dev_index(cx, cy, c)], va, lsem.at[0])
            lb = pltpu.make_async_copy(land1[a].at[k], vb, lsem.at[1])
            la.start()
            lb.start()
            la.wait()
            lb.wait()
            return va[...] + vb[...]

        sends = []
        for a in range(n):
            for k in range(4):
                cp = s1_copy(a, k)
                cp.start()
                sends.append(cp)
        for a in range(n):
            for k in range(3):
                s1_copy(a, k).wait_recv()
                stage[a][k] = chip_partial(a, k).astype(BF16)
                cp = s2_copy(a, k)
                cp.start()
                sends.append(cp)
        for a in range(n):
            s1_copy(a, 3).wait_recv()
            acc = chip_partial(a, 3)
            for k in range(3):
                s2_copy(a, k).wait_recv()
                acc = acc + land2[a][k].astype(F32)
            outs[a][...] = acc
        for cp in sends:
            cp.wait_send()

    out_shape = tuple(jax.ShapeDtypeStruct(s, F32) for s in shapes)
    out_shape += tuple(jax.ShapeDtypeStruct((4,) + s, F32) for s in shapes)
    res = pl.pallas_call(
        body,
        out_shape=out_shape,
        in_specs=[pl.BlockSpec(memory_space=pl.ANY)] * n,
        out_specs=[pl.BlockSpec(memory_space=pltpu.VMEM)] * n + [pl.BlockSpec(memory_space=pl.ANY)] * n,
        scratch_shapes=[pltpu.VMEM((3,) + s, BF16) for s in shapes]
        + [pltpu.VMEM((3,) + s, BF16) for s in shapes]
        + [pltpu.VMEM((max_rows, max_cols), F32), pltpu.VMEM((max_rows, max_cols), F32),
           pltpu.SemaphoreType.DMA((n, 4)), pltpu.SemaphoreType.DMA((n, 4)),
           pltpu.SemaphoreType.DMA((n, 3)), pltpu.SemaphoreType.DMA((n, 3)),
           pltpu.SemaphoreType.DMA((2,))],
        compiler_params=pltpu.CompilerParams(vmem_limit_bytes=56 * MIB),
        name="reduce_scatter_grads",
    )(*grads)
    return res[:n]


def _adamw_math(w, g, m, v):
    m = ADAM_B1 * m + (1.0 - ADAM_B1) * g
    v = ADAM_B2 * v + (1.0 - ADAM_B2) * (g * g)
    m_hat = m / (1.0 - ADAM_B1 ** ADAM_STEP)
    v_hat = v / (1.0 - ADAM_B2 ** ADAM_STEP)
    delta = -ADAM_LR * (m_hat / (jnp.sqrt(v_hat) + ADAM_EPS) + ADAM_WD * w)
    return delta, m, v


def _allreduce_small_adamw(g, w, m, v):
    shape = g.shape

    def body(g_ref, w_ref, m_ref, v_ref, gs_ref, d_ref, nm_ref, nv_ref, sib, psum, land, send_sems, recv_sems):
        x, y, c = _mesh_pos()
        sibling = (x, y, 1 - c)
        chips = _other_chips(x, y)
        first = pltpu.make_async_remote_copy(
            src_ref=g_ref, dst_ref=sib, send_sem=send_sems.at[0], recv_sem=recv_sems.at[0],
            device_id=sibling, device_id_type=MESH)
        first.start()
        first.wait_recv()
        psum[...] = g_ref[...] + sib[...]
        second = []
        for k, (cx, cy) in enumerate(chips):
            cp = pltpu.make_async_remote_copy(
                src_ref=psum, dst_ref=land.at[k], send_sem=send_sems.at[1 + k], recv_sem=recv_sems.at[1 + k],
                device_id=(cx, cy, c), device_id_type=MESH)
            cp.start()
            second.append(cp)
        for cp in second:
            cp.wait_recv()
        total = (psum[...] + land[0]) + (land[1] + land[2])
        gs_ref[...] = total
        delta, nm, nv = _adamw_math(w_ref[...], total, m_ref[...], v_ref[...])
        d_ref[...] = delta
        nm_ref[...] = nm
        nv_ref[...] = nv
        first.wait_send()
        for cp in second:
            cp.wait_send()

    sd = jax.ShapeDtypeStruct(shape, F32)
    return pl.pallas_call(
        body,
        out_shape=(sd, sd, sd, sd),
        in_specs=[pl.BlockSpec(memory_space=pltpu.VMEM)] * 4,
        out_specs=[pl.BlockSpec(memory_space=pltpu.VMEM)] * 4,
        scratch_shapes=[pltpu.VMEM(shape, F32), pltpu.VMEM(shape, F32), pltpu.VMEM((3,) + shape, F32),
                        pltpu.SemaphoreType.DMA((4,)), pltpu.SemaphoreType.DMA((4,))],
        compiler_params=pltpu.CompilerParams(vmem_limit_bytes=32 * MIB),
        name="allreduce_small_adamw",
    )(g, w, m, v)


def _adamw(w, g, m, v, name):
    rows, cols = w.shape
    tm = 256 if rows % 256 == 0 else rows

    def body(w_ref, g_ref, m_ref, v_ref, d_ref, nm_ref, nv_ref):
        delta, nm, nv = _adamw_math(w_ref[...], g_ref[...], m_ref[...], v_ref[...])
        d_ref[...] = delta
        nm_ref[...] = nm
        nv_ref[...] = nv

    sd = jax.ShapeDtypeStruct((rows, cols), F32)
    spec = _row_spec(tm, cols)
    return pl.pallas_call(
        body, grid=(rows // tm,), out_shape=(sd, sd, sd), in_specs=[spec] * 4, out_specs=[spec] * 3,
        compiler_params=_params(32), name=name,
    )(w, g, m, v)


def _fwd_in(x2, g1, wz, wf, wg):
    T = x2.shape[0]
    tm = TOKEN_TILE

    def body(x_ref, g_ref, wz_ref, wf_ref, wg_ref, xn_ref, zuv_ref, qkv_ref, fl_ref, gt_ref):
        x = x_ref[...]
        r, xh = _rms_stats(x)
        xn = (xh * g_ref[...]).astype(BF16)
        xn_ref[...] = xn
        zuv_ref[...] = _dot(xn, wz_ref[:, 0:1024]).astype(BF16)
        qkv_ref[:, 0:512] = (_dot(xn, wz_ref[:, 1024:1536]) * 0.125).astype(BF16)
        qkv_ref[:, 512:1536] = _dot(xn, wz_ref[:, 1536:2560]).astype(BF16)
        fl_ref[...] = _dot(xn, wf_ref[...])
        gt_ref[...] = jax.nn.sigmoid(_dot(xn, wg_ref[...])).astype(BF16)

    return pl.pallas_call(
        body, grid=(T // tm,),
        out_shape=(jax.ShapeDtypeStruct((T, D_MODEL), BF16), jax.ShapeDtypeStruct((T, 1024), BF16),
                   jax.ShapeDtypeStruct((T, 1536), BF16), jax.ShapeDtypeStruct((T, LANES), F32),
                   jax.ShapeDtypeStruct((T, 2048), BF16)),
        in_specs=[_row_spec(tm, D_MODEL), _const_spec((1, D_MODEL)), _const_spec((D_MODEL, ZQKV_WIDTH)),
                  _const_spec((D_MODEL, LANES)), _const_spec((D_MODEL, 2048))],
        out_specs=[_row_spec(tm, D_MODEL), _row_spec(tm, 1024), _row_spec(tm, 1536), _row_spec(tm, LANES),
                   _row_spec(tm, 2048)],
        compiler_params=_params(48), name="fwd_in",
    )(x2, g1, wz, wf, wg)


def _log_sigmoid(f):
    return jnp.minimum(f, 0.0) - jnp.log1p(jnp.exp(-jnp.abs(f)))


def _fwd_cum(fl, bfp):
    T = fl.shape[0]
    tb = CUM_TILE

    def body(fl_ref, b_ref, cc_ref, ct_ref):
        tri = (_iota((tb, tb), 0) >= _iota((tb, tb), 1)).astype(F32)
        eye = (_iota((SUBLANES, LANES), 0) == _iota((SUBLANES, LANES), 1)).astype(F32)
        carry = jnp.zeros((1, LANES), F32)
        for i in range(T // tb):
            lf = _log_sigmoid(fl_ref[i * tb:(i + 1) * tb, :] + b_ref[...])
            cs = jnp.dot(tri, lf, precision=HIGHEST, preferred_element_type=F32) + carry
            cc_ref[i * tb:(i + 1) * tb, :] = cs
            carry = cs[tb - 1:tb, :]
            ct_ref[:, i * tb:(i + 1) * tb] = lax.dot_general(
                eye, cs, NT_DIMS, precision=HIGHEST, preferred_element_type=F32)

    return pl.pallas_call(
        body,
        out_shape=(jax.ShapeDtypeStruct((T, LANES), F32), jax.ShapeDtypeStruct((N_HEADS, T), F32)),
        compiler_params=pltpu.CompilerParams(vmem_limit_bytes=32 * MIB), name="fwd_cum",
    )(fl, bfp)


def _sgu_forward_parts(z, gs, bs):
    u = _gelu(z[:, :SGU_WIDTH])
    vv = _gelu(z[:, SGU_WIDTH:])
    vc = vv - jnp.mean(vv, axis=1, keepdims=True)
    rs = lax.rsqrt(jnp.mean(vc * vc, axis=1, keepdims=True) + EPS)
    vhat = vc * rs
    return u, vhat, rs, vhat * gs + bs


def _sgu_pair_weights(w_ref, bT, p):
    tril = _iota((CHUNK, CHUNK), 0) >= _iota((CHUNK, CHUNK), 1)
    we = jnp.where(tril, w_ref[2 * p], 0.0).astype(BF16)
    wo = jnp.where(tril, w_ref[2 * p + 1], 0.0).astype(BF16)
    lane8 = _iota(bT.shape, 1)
    low = _iota((CHUNK, LANES), 1) < 64
    b2 = jnp.where(low, _pick(bT, lane8, 2 * p), _pick(bT, lane8, 2 * p + 1))
    return we, wo, b2


def _chunks_on_lanes(v, p, nc):
    return jnp.concatenate([v[c * CHUNK:(c + 1) * CHUNK, LANES * p:LANES * (p + 1)] for c in range(nc)], axis=1)


def _sgu_mix(we, wo, b2, vcat, nc):
    low = (_iota((CHUNK, nc * LANES), 1) % LANES) < 64
    return jnp.where(low, _dot(we, vcat), _dot(wo, vcat)) + jnp.concatenate([b2] * nc, axis=1)


def _fwd_sgu(zuv, gs, bs, wsp, bT):
    T = zuv.shape[0]
    tc = SGU_TILE
    nc = tc // CHUNK

    def body(z_ref, gs_ref, bs_ref, w_ref, bT_ref, y_ref):
        u, _, _, vln = _sgu_forward_parts(z_ref[...].astype(F32), gs_ref[...], bs_ref[...])
        vb = vln.astype(BF16)
        for p in range(4):
            we, wo, b2 = _sgu_pair_weights(w_ref, bT_ref[...], p)
            s = _sgu_mix(we, wo, b2, _chunks_on_lanes(vb, p, nc), nc)
            for c in range(nc):
                rows, cols = slice(c * CHUNK, (c + 1) * CHUNK), slice(LANES * p, LANES * (p + 1))
                y_ref[rows, cols] = (u[rows, cols] * s[:, c * LANES:(c + 1) * LANES]).astype(BF16)

    return pl.pallas_call(
        body, grid=(T // tc,), out_shape=jax.ShapeDtypeStruct((T, SGU_WIDTH), BF16),
        in_specs=[_row_spec(tc, 1024), _const_spec((1, SGU_WIDTH)), _const_spec((1, SGU_WIDTH)),
                  _const_spec((8, CHUNK, CHUNK)), _const_spec((CHUNK, 8))],
        out_specs=_row_spec(tc, SGU_WIDTH),
        compiler_params=_params(40), name="fwd_sgu",
    )(zuv, gs, bs, wsp, bT)


def _stack_heads(t2):
    low = _iota(t2.shape, 1) < 64
    zero = jnp.zeros_like(t2)
    return jnp.concatenate([jnp.where(low, t2, zero), jnp.where(low, zero, t2)], axis=0)


def _fwd_attn(qkv, cc, cr):
    T = qkv.shape[0]
    tq = tk = ATTN_TILE
    nk = T // tk

    def body(q_ref, k_ref, v_ref, cc_ref, cr_ref, o_ref, lse_ref):
        i = pl.program_id(0)
        lane = _iota((tq, LANES), 1)
        low = lane < 64
        cc = cc_ref[...]
        lse_blk = jnp.zeros((tq, LANES), F32)
        row = _iota((2 * tq, tk), 0) % tq
        col = _iota((2 * tq, tk), 1)
        for p in range(4):
            cols = slice(LANES * p, LANES * (p + 1))
            qs = _stack_heads(q_ref[:, cols])
            cq = jnp.concatenate([_pick(cc, lane, 2 * p), _pick(cc, lane, 2 * p + 1)], axis=0)

            def step(j, carry, masked):
                m, l, acc = carry
                ks = pl.ds(pl.multiple_of(j * tk, tk), tk)
                k2 = k_ref[ks, cols]
                v2 = v_ref[ks, cols]
                cke = cr_ref[pl.ds(2 * p * nk + j, 1), :]
                cko = cr_ref[pl.ds((2 * p + 1) * nk + j, 1), :]
                ck = jnp.concatenate([jnp.broadcast_to(cke, (tq, tk)), jnp.broadcast_to(cko, (tq, tk))], axis=0)
                s = _dot_nt(qs, k2) + (cq - ck)
                if masked:
                    s = jnp.where(col <= row, s, NEG)
                mn = jnp.maximum(m, jnp.max(s, axis=1, keepdims=True))
                al = jnp.exp(m - mn)
                pm = jnp.exp(s - mn)
                l = al * l + jnp.sum(pm, axis=1, keepdims=True)
                acc = al * acc + _dot(pm.astype(BF16), v2)
                return mn, l, acc

            init = (jnp.full((2 * tq, 1), NEG, F32), jnp.zeros((2 * tq, 1), F32), jnp.zeros((2 * tq, LANES), F32))
            carry = lax.fori_loop(0, i, lambda j, cr_: step(j, cr_, False), init)
            m, l, acc = step(i, carry, True)
            o = acc / l
            o_ref[:, cols] = jnp.where(low, o[:tq], o[tq:]).astype(BF16)
            lse = m + jnp.log(l)
            lse_blk = jnp.where(lane == 2 * p, lse[:tq], lse_blk)
            lse_blk = jnp.where(lane == 2 * p + 1, lse[tq:], lse_blk)
        lse_ref[...] = lse_blk

    return pl.pallas_call(
        body, grid=(T // tq,),
        out_shape=(jax.ShapeDtypeStruct((T, ATTN_WIDTH), BF16), jax.ShapeDtypeStruct((T, LANES), F32)),
        in_specs=[pl.BlockSpec((tq, 512), lambda i: (i, 0)),
                  pl.BlockSpec((T, 512), lambda i: (0, 1), pipeline_mode=pl.Buffered(1)),
                  pl.BlockSpec((T, 512), lambda i: (0, 2), pipeline_mode=pl.Buffered(1)),
                  _row_spec(tq, LANES), _const_spec((N_HEADS * nk, tk))],
        out_specs=[_row_spec(tq, ATTN_WIDTH), _row_spec(tq, LANES)],
        compiler_params=_params(48), name="fwd_attn",
    )(qkv, qkv, qkv, cc, cr)


def _fwd_merge(ys, ya, gt, x2, wbs, wba, wo, g2):
    T = x2.shape[0]
    tm = TOKEN_TILE

    def body(ys_ref, ya_ref, gt_ref, x_ref, wbs_ref, wba_ref, wo_ref, g2_ref, a_ref, b_ref, mg_ref, o_ref, h1_ref):
        A = _dot(ys_ref[...], wbs_ref[...])
        B = _dot(ya_ref[...], wba_ref[...])
        mg = (gt_ref[:, :D_MODEL].astype(F32) * A + gt_ref[:, D_MODEL:].astype(F32) * B).astype(BF16)
        o = _dot(mg, wo_ref[...])
        r2, oh = _rms_stats(o)
        a_ref[...] = A.astype(BF16)
        b_ref[...] = B.astype(BF16)
        mg_ref[...] = mg
        o_ref[...] = o.astype(BF16)
        h1_ref[...] = x_ref[...] + oh * g2_ref[...]

    sd = jax.ShapeDtypeStruct((T, D_MODEL), BF16)
    return pl.pallas_call(
        body, grid=(T // tm,),
        out_shape=(sd, sd, sd, sd, jax.ShapeDtypeStruct((T, D_MODEL), F32)),
        in_specs=[_row_spec(tm, 512), _row_spec(tm, 512), _row_spec(tm, 2048), _row_spec(tm, D_MODEL),
                  _const_spec((512, D_MODEL)), _const_spec((512, D_MODEL)), _const_spec((D_MODEL, D_MODEL)),
                  _const_spec((1, D_MODEL))],
        out_specs=[_row_spec(tm, D_MODEL)] * 5,
        compiler_params=_params(40), name="fwd_merge",
    )(ys, ya, gt, x2, wbs, wba, wo, g2)


def _fwd_ffn_loss(h1, tgt, wup, wdn, g3, g4):
    T = h1.shape[0]
    tm = TOKEN_TILE
    nsteps = T // tm

    def body(h1_ref, tg_ref, wup_ref, wdn_ref, g3_ref, g4_ref, xn2_ref, a_ref, ddn_ref, dy_ref, loss_ref,
             dg4_ref, acc_l, acc_g):
        i = pl.program_id(0)

        @pl.when(i == 0)
        def _():
            acc_l[...] = jnp.zeros_like(acc_l)
            acc_g[...] = jnp.zeros_like(acc_g)

        h1v = h1_ref[...]
        r3, h1h = _rms_stats(h1v)
        xn2 = (h1h * g3_ref[...]).astype(BF16)
        xn2_ref[...] = xn2
        dn = jnp.zeros((tm, D_MODEL), F32)
        for j in range(D_FF // 1024):
            cols = slice(1024 * j, 1024 * (j + 1))
            a = _dot(xn2, wup_ref[:, cols])
            a_ref[:, cols] = a.astype(BF16)
            hid = jnp.square(jnp.maximum(a, 0.0)).astype(BF16)
            dn = dn + _dot(hid, wdn_ref[cols, :])
        r4, dnh = _rms_stats(dn)
        g4v = g4_ref[...]
        e = (h1v + dnh * g4v) - tg_ref[...]
        sq = e * e
        s1 = sq[:, 0:LANES]
        for j in range(1, D_MODEL // LANES):
            s1 = s1 + sq[:, LANES * j:LANES * (j + 1)]
        acc_l[...] += _fold8(s1)
        dy = e * (1.0 / D_MODEL)
        dy_ref[...] = dy
        acc_g[...] += _fold8(dy * dnh)
        ddn_ref[...] = _rms_bwd(dy, dnh, r4, g4v).astype(BF16)

        @pl.when(i == nsteps - 1)
        def _():
            loss_ref[...] = acc_l[...] * (0.5 / D_MODEL)
            dg4_ref[...] = jnp.sum(acc_g[...], axis=0, keepdims=True)

    return pl.pallas_call(
        body, grid=(nsteps,),
        out_shape=(jax.ShapeDtypeStruct((T, D_MODEL), BF16), jax.ShapeDtypeStruct((T, D_FF), BF16),
                   jax.ShapeDtypeStruct((T, D_MODEL), BF16), jax.ShapeDtypeStruct((T, D_MODEL), F32),
                   jax.ShapeDtypeStruct((SUBLANES, LANES), F32), jax.ShapeDtypeStruct((1, D_MODEL), F32)),
        in_specs=[_row_spec(tm, D_MODEL), _row_spec(tm, D_MODEL), _const_spec((D_MODEL, D_FF)),
                  _const_spec((D_FF, D_MODEL)), _const_spec((1, D_MODEL)), _const_spec((1, D_MODEL))],
        out_specs=[_row_spec(tm, D_MODEL), _row_spec(tm, D_FF), _row_spec(tm, D_MODEL), _row_spec(tm, D_MODEL),
                   pl.BlockSpec((SUBLANES, LANES), lambda i: (0, 0)), pl.BlockSpec((1, D_MODEL), lambda i: (0, 0))],
        scratch_shapes=[pltpu.VMEM((SUBLANES, LANES), F32), pltpu.VMEM((SUBLANES, D_MODEL), F32)],
        compiler_params=_params(52), name="fwd_ffn_loss",
    )(h1, tgt, wup, wdn, g3, g4)


def _bwd_ffn(ddn, a, dy, h1, wup, wdn, g3):
    T = h1.shape[0]
    tm = TOKEN_TILE
    nsteps = T // tm

    def body(ddn_ref, a_ref, dy_ref, h1_ref, wup_ref, wdn_ref, g3_ref, da_ref, dh1_ref, dg3_ref, acc_g):
        i = pl.program_id(0)

        @pl.when(i == 0)
        def _():
            acc_g[...] = jnp.zeros_like(acc_g)

        ddnv = ddn_ref[...]
        dxn2 = jnp.zeros((tm, D_MODEL), F32)
        for j in range(D_FF // 1024):
            cols = slice(1024 * j, 1024 * (j + 1))
            dhid = _dot_nt(ddnv, wdn_ref[cols, :])
            da = (dhid * (2.0 * jnp.maximum(a_ref[:, cols].astype(F32), 0.0))).astype(BF16)
            da_ref[:, cols] = da
            dxn2 = dxn2 + _dot_nt(da, wup_ref[:, cols])
        r3, h1h = _rms_stats(h1_ref[...])
        acc_g[...] += _fold8(dxn2 * h1h)
        dh1_ref[...] = dy_ref[...] + _rms_bwd(dxn2, h1h, r3, g3_ref[...])

        @pl.when(i == nsteps - 1)
        def _():
            dg3_ref[...] = jnp.sum(acc_g[...], axis=0, keepdims=True)

    return pl.pallas_call(
        body, grid=(nsteps,),
        out_shape=(jax.ShapeDtypeStruct((T, D_FF), BF16), jax.ShapeDtypeStruct((T, D_MODEL), F32),
                   jax.ShapeDtypeStruct((1, D_MODEL), F32)),
        in_specs=[_row_spec(tm, D_MODEL), _row_spec(tm, D_FF), _row_spec(tm, D_MODEL), _row_spec(tm, D_MODEL),
                  _const_spec((D_MODEL, D_FF)), _const_spec((D_FF, D_MODEL)), _const_spec((1, D_MODEL))],
        out_specs=[_row_spec(tm, D_FF), _row_spec(tm, D_MODEL), pl.BlockSpec((1, D_MODEL), lambda i: (0, 0))],
        scratch_shapes=[pltpu.VMEM((SUBLANES, D_MODEL), F32)],
        compiler_params=_params(52), name="bwd_ffn",
    )(ddn, a, dy, h1, wup, wdn, g3)


def _wgrad(xa, dy, name, relu2=False, tn=None):
    T, K = xa.shape
    N = dy.shape[1]
    tn = N if tn is None else tn
    tt = WGRAD_TILE

    def body(x_ref, dy_ref, o_ref):
        @pl.when(pl.program_id(1) == 0)
        def _():
            o_ref[...] = jnp.zeros_like(o_ref)

        xv = x_ref[...]
        if relu2:
            xv = jnp.square(jnp.maximum(xv.astype(F32), 0.0)).astype(BF16)
        o_ref[...] += _dot_tn(xv, dy_ref[...])

    return pl.pallas_call(
        body, grid=(N // tn, T // tt), out_shape=jax.ShapeDtypeStruct((K, N), F32),
        in_specs=[pl.BlockSpec((tt, K), lambda n, t: (t, 0)), pl.BlockSpec((tt, tn), lambda n, t: (t, n))],
        out_specs=pl.BlockSpec((K, tn), lambda n, t: (0, n)),
        compiler_params=_params(52, 2), name=name,
    )(xa, dy)


def _bwd_merge(dh1, o, A, B, gt, ya, wbs, wba, wo, g2):
    T = dh1.shape[0]
    tm = TOKEN_TILE
    nsteps = T // tm

    def body(dh1_ref, o_ref, a_ref, b_ref, gt_ref, ya_ref, wbs_ref, wba_ref, wo_ref, g2_ref,
             do_ref, da_ref, db_ref, dgl_ref, dys_ref, dya_ref, ds_ref, dg2_ref, acc_g):
        i = pl.program_id(0)

        @pl.when(i == 0)
        def _():
            acc_g[...] = jnp.zeros_like(acc_g)

        dh1v = dh1_ref[...]
        r2, oh = _rms_stats(o_ref[...].astype(F32))
        acc_g[...] += _fold8(dh1v * oh)
        do = _rms_bwd(dh1v, oh, r2, g2_ref[...]).astype(BF16)
        do_ref[...] = do
        dmg = _dot_nt(do, wo_ref[...])
        ga = gt_ref[:, :D_MODEL].astype(F32)
        gb = gt_ref[:, D_MODEL:].astype(F32)
        dgl_ref[:, :D_MODEL] = (dmg * a_ref[...].astype(F32) * ga * (1.0 - ga)).astype(BF16)
        dgl_ref[:, D_MODEL:] = (dmg * b_ref[...].astype(F32) * gb * (1.0 - gb)).astype(BF16)
        dA = (dmg * ga).astype(BF16)
        dB = (dmg * gb).astype(BF16)
        da_ref[...] = dA
        db_ref[...] = dB
        dys_ref[...] = _dot_nt(dA, wbs_ref[...]).astype(BF16)
        dya = _dot_nt(dB, wba_ref[...]).astype(BF16)
        dya_ref[...] = dya
        prod = dya.astype(F32) * ya_ref[...].astype(F32)
        lane = _iota((tm, LANES), 1)
        low = lane < 64
        blk = jnp.zeros((tm, LANES), F32)
        for p in range(4):
            pp = prod[:, LANES * p:LANES * (p + 1)]
            blk = jnp.where(lane == 2 * p, jnp.sum(jnp.where(low, pp, 0.0), axis=1, keepdims=True), blk)
            blk = jnp.where(lane == 2 * p + 1, jnp.sum(jnp.where(low, 0.0, pp), axis=1, keepdims=True), blk)
        ds_ref[...] = blk

        @pl.when(i == nsteps - 1)
        def _():
            dg2_ref[...] = jnp.sum(acc_g[...], axis=0, keepdims=True)

    sd = jax.ShapeDtypeStruct((T, D_MODEL), BF16)
    sh = jax.ShapeDtypeStruct((T, 512), BF16)
    return pl.pallas_call(
        body, grid=(nsteps,),
        out_shape=(sd, sd, sd, jax.ShapeDtypeStruct((T, 2048), BF16), sh, sh,
                   jax.ShapeDtypeStruct((T, LANES), F32), jax.ShapeDtypeStruct((1, D_MODEL), F32)),
        in_specs=[_row_spec(tm, D_MODEL)] * 4 + [_row_spec(tm, 2048), _row_spec(tm, 512),
                  _const_spec((512, D_MODEL)), _const_spec((512, D_MODEL)), _const_spec((D_MODEL, D_MODEL)),
                  _const_spec((1, D_MODEL))],
        out_specs=[_row_spec(tm, D_MODEL)] * 3 + [_row_spec(tm, 2048), _row_spec(tm, 512), _row_spec(tm, 512),
                   _row_spec(tm, LANES), pl.BlockSpec((1, D_MODEL), lambda i: (0, 0))],
        scratch_shapes=[pltpu.VMEM((SUBLANES, D_MODEL), F32)],
        compiler_params=_params(40), name="bwd_merge",
    )(dh1, o, A, B, gt, ya, wbs, wba, wo, g2)


def _bwd_sgu(zuv, dys, gs, bs, wsp, bT):
    T = zuv.shape[0]
    tc = SGU_TILE
    nc = tc // CHUNK
    nsteps = T // tc

    def body(z_ref, dy_ref, gs_ref, bs_ref, w_ref, bT_ref, dz_ref, dw_ref, dbT_ref, dgs_ref, dbs_ref,
             acc_w, acc_b, acc_gs, acc_bs, dvln_s):
        i = pl.program_id(0)

        @pl.when(i == 0)
        def _():
            acc_w[...] = jnp.zeros_like(acc_w)
            acc_b[...] = jnp.zeros_like(acc_b)
            acc_gs[...] = jnp.zeros_like(acc_gs)
            acc_bs[...] = jnp.zeros_like(acc_bs)

        z = z_ref[...].astype(F32)
        gsv = gs_ref[...]
        u, vhat, rs, vln = _sgu_forward_parts(z, gsv, bs_ref[...])
        vb = vln.astype(BF16)
        dy = dy_ref[...].astype(F32)
        low_w = (_iota((CHUNK, nc * LANES), 1) % LANES) < 64
        for p in range(4):
            we, wo, b2 = _sgu_pair_weights(w_ref, bT_ref[...], p)
            vcat = _chunks_on_lanes(vb, p, nc)
            s = _sgu_mix(we, wo, b2, vcat, nc)
            dyc = _chunks_on_lanes(dy, p, nc)
            ds = dyc * _chunks_on_lanes(u, p, nc)
            dsb = ds.astype(BF16)
            zero = jnp.zeros_like(dsb)
            dse = jnp.where(low_w, dsb, zero)
            dso = jnp.where(low_w, zero, dsb)
            acc_w[2 * p] += _dot_nt(dse, vcat)
            acc_w[2 * p + 1] += _dot_nt(dso, vcat)
            acc_b[p] += ds
            dvl = jnp.where(low_w, _dot_tn(we, dsb), _dot_tn(wo, dsb))
            for c in range(nc):
                rows, cols = slice(c * CHUNK, (c + 1) * CHUNK), slice(LANES * p, LANES * (p + 1))
                dvln_s[rows, cols] = dvl[:, c * LANES:(c + 1) * LANES]
                du = dy[rows, cols] * s[:, c * LANES:(c + 1) * LANES]
                dz_ref[rows, cols] = (du * _gelu_grad(z[rows, cols])).astype(BF16)
        dvln = dvln_s[...]
        acc_gs[...] += _fold8(dvln * vhat)
        acc_bs[...] += _fold8(dvln)
        al = dvln * gsv
        dvv = rs * (al - jnp.mean(al, axis=1, keepdims=True) - vhat * jnp.mean(al * vhat, axis=1, keepdims=True))
        dz_ref[:, SGU_WIDTH:] = (dvv * _gelu_grad(z[:, SGU_WIDTH:])).astype(BF16)

        @pl.when(i == nsteps - 1)
        def _():
            tril = _iota((CHUNK, CHUNK), 0) >= _iota((CHUNK, CHUNK), 1)
            lane = _iota((CHUNK, LANES), 1)
            low = lane < 64
            blk = jnp.zeros((CHUNK, LANES), F32)
            for g in range(8):
                dw_ref[g] = jnp.where(tril, acc_w[g], 0.0)
            for p in range(4):
                t = acc_b[p]
                tot = t[:, 0:LANES]
                for c in range(1, nc):
                    tot = tot + t[:, c * LANES:(c + 1) * LANES]
                blk = jnp.where(lane == 2 * p, jnp.sum(jnp.where(low, tot, 0.0), axis=1, keepdims=True), blk)
                blk = jnp.where(lane == 2 * p + 1, jnp.sum(jnp.where(low, 0.0, tot), axis=1, keepdims=True), blk)
            dbT_ref[...] = blk
            dgs_ref[...] = jnp.sum(acc_gs[...], axis=0, keepdims=True)
            dbs_ref[...] = jnp.sum(acc_bs[...], axis=0, keepdims=True)

    whole = lambda shape: pl.BlockSpec(shape, lambda i: (0,) * len(shape))
    return pl.pallas_call(
        body, grid=(nsteps,),
        out_shape=(jax.ShapeDtypeStruct((T, 1024), BF16), jax.ShapeDtypeStruct((8, CHUNK, CHUNK), F32),
                   jax.ShapeDtypeStruct((CHUNK, LANES), F32), jax.ShapeDtypeStruct((1, SGU_WIDTH), F32),
                   jax.ShapeDtypeStruct((1, SGU_WIDTH), F32)),
        in_specs=[_row_spec(tc, 1024), _row_spec(tc, SGU_WIDTH), _const_spec((1, SGU_WIDTH)),
                  _const_spec((1, SGU_WIDTH)), _const_spec((8, CHUNK, CHUNK)), _const_spec((CHUNK, 8))],
        out_specs=[_row_spec(tc, 1024), whole((8, CHUNK, CHUNK)), whole((CHUNK, LANES)),
                   whole((1, SGU_WIDTH)), whole((1, SGU_WIDTH))],
        scratch_shapes=[pltpu.VMEM((8, CHUNK, CHUNK), F32), pltpu.VMEM((4, CHUNK, nc * LANES), F32),
                        pltpu.VMEM((SUBLANES, SGU_WIDTH), F32), pltpu.VMEM((SUBLANES, SGU_WIDTH), F32),
                        pltpu.VMEM((tc, SGU_WIDTH), F32)],
        compiler_params=_params(48), name="bwd_sgu",
    )(zuv, dys, gs, bs, wsp, bT)


def _bwd_attn(qkv, dya, stats, cr):
    T = qkv.shape[0]
    tq = tk = ATTN_TILE
    nq = T // tq
    nk = T // tk

    def body(q_ref, k_ref, v_ref, do_ref, st_ref, cr_ref, dq_ref, dk_ref, dv_ref, dcr_ref, dcc_ref, dq_acc):
        p = pl.program_id(0)

        @pl.when(p == 0)
        def _():
            dcc_ref[...] = jnp.zeros_like(dcc_ref)

        lane = _iota((tq, LANES), 1)
        low = lane < 64
        row = _iota((2 * tq, tk), 0) % tq
        col = _iota((2 * tq, tk), 1)
        dq_acc[...] = jnp.zeros_like(dq_acc)

        def kv_block(j, _):
            ks = pl.ds(pl.multiple_of(j * tk, tk), tk)
            k2 = k_ref[ks, :]
            v2 = v_ref[ks, :]
            cke = cr_ref[pl.ds(j, 1), :]
            cko = cr_ref[pl.ds(nk + j, 1), :]
            ck = jnp.concatenate([jnp.broadcast_to(cke, (tq, tk)), jnp.broadcast_to(cko, (tq, tk))], axis=0)

            def q_block(i, carry):
                dk_a, dv_a, dce, dco = carry
                qsl = pl.ds(pl.multiple_of(i * tq, tq), tq)
                qs = _stack_heads(q_ref[qsl, :])
                dos = _stack_heads(do_ref[qsl, :])
                st = st_ref[qsl, :]
                lse = jnp.concatenate([_pick(st, lane, 2 * p), _pick(st, lane, 2 * p + 1)], axis=0)
                dsum = jnp.concatenate([_pick(st, lane, 8 + 2 * p), _pick(st, lane, 9 + 2 * p)], axis=0)
                cq = jnp.concatenate([_pick(st, lane, 16 + 2 * p), _pick(st, lane, 17 + 2 * p)], axis=0)
                s = _dot_nt(qs, k2) + (cq - ck)
                s = jnp.where((col <= row) | (i != j), s, NEG)
                pm = jnp.exp(s - lse)
                dp = _dot_nt(dos, v2)
                ds = pm * (dp - dsum)
                dsb = ds.astype(BF16)
                dv_a = dv_a + _dot_tn(pm.astype(BF16), dos)
                dk_a = dk_a + _dot_tn(dsb, qs)
                dqs = _dot(dsb, k2)
                dq_acc[qsl, :] += jnp.where(low, dqs[:tq], dqs[tq:])
                rse = jnp.sum(ds[:tq], axis=1, keepdims=True)
                rso = jnp.sum(ds[tq:], axis=1, keepdims=True)
                dcc_ref[qsl, :] += jnp.where(lane == 2 * p, rse, 0.0) + jnp.where(lane == 2 * p + 1, rso, 0.0)
                dce = dce - jnp.sum(ds[:tq], axis=0, keepdims=True)
                dco = dco - jnp.sum(ds[tq:], axis=0, keepdims=True)
                return dk_a, dv_a, dce, dco

            init = (jnp.zeros((tk, LANES), F32), jnp.zeros((tk, LANES), F32),
                    jnp.zeros((1, tk), F32), jnp.zeros((1, tk), F32))
            dk_a, dv_a, dce, dco = lax.fori_loop(j, nq, q_block, init)
            dk_ref[ks, :] = dk_a.astype(BF16)
            dv_ref[ks, :] = dv_a.astype(BF16)
            dcr_ref[pl.ds(j, 1), :] = dce
            dcr_ref[pl.ds(nk + j, 1), :] = dco
            return 0

        lax.fori_loop(0, nk, kv_block, 0)
        dq_ref[...] = (dq_acc[...] * 0.125).astype(BF16)

    sh = jax.ShapeDtypeStruct((T, ATTN_WIDTH), BF16)
    col_spec = lambda cb: pl.BlockSpec((T, LANES), lambda p: (0, cb + p))
    return pl.pallas_call(
        body, grid=(4,),
        out_shape=(sh, sh, sh, jax.ShapeDtypeStruct((N_HEADS * nk, tk), F32), jax.ShapeDtypeStruct((T, LANES), F32)),
        in_specs=[col_spec(0), col_spec(4), col_spec(8), col_spec(0), _const_spec((T, LANES)),
                  pl.BlockSpec((2 * nk, tk), lambda p: (p, 0))],
        out_specs=[col_spec(0), col_spec(0), col_spec(0), pl.BlockSpec((2 * nk, tk), lambda p: (p, 0)),
                   pl.BlockSpec((T, LANES), lambda p: (0, 0))],
        scratch_shapes=[pltpu.VMEM((T, LANES), F32)],
        compiler_params=_params(56), name="bwd_attn",
    )(qkv, qkv, qkv, dya, stats, cr)


def _bwd_cum(dct, dcc, fl, bfp):
    T = fl.shape[0]
    tb = CUM_TILE

    def body(dct_ref, dcc_ref, fl_ref, b_ref, dfl_ref, dbf_ref):
        triu = (_iota((tb, tb), 0) <= _iota((tb, tb), 1)).astype(F32)
        eye = (_iota((SUBLANES, LANES), 0) == _iota((SUBLANES, LANES), 1)).astype(F32)
        carry = jnp.zeros((1, LANES), F32)
        dbf = jnp.zeros((1, LANES), F32)
        for i in reversed(range(T // tb)):
            blk = dct_ref[:, i * tb:(i + 1) * tb]
            colblk = lax.dot_general(blk, eye, TN_DIMS, precision=HIGHEST, preferred_element_type=F32)
            colblk = colblk + dcc_ref[i * tb:(i + 1) * tb, :]
            rc = jnp.dot(triu, colblk, precision=HIGHEST, preferred_element_type=F32) + carry
            carry = rc[0:1, :]
            sig = jax.nn.sigmoid(fl_ref[i * tb:(i + 1) * tb, :] + b_ref[...])
            dfl = rc * (1.0 - sig)
            dfl_ref[i * tb:(i + 1) * tb, :] = dfl.astype(BF16)
            dbf = dbf + jnp.sum(dfl, axis=0, keepdims=True)
        dbf_ref[...] = dbf

    return pl.pallas_call(
        body,
        out_shape=(jax.ShapeDtypeStruct((T, LANES), BF16), jax.ShapeDtypeStruct((1, LANES), F32)),
        compiler_params=pltpu.CompilerParams(vmem_limit_bytes=32 * MIB), name="bwd_cum",
    )(dct, dcc, fl, bfp)


def _bwd_in(dz, dq, dk, dv, dfl, dgl, dh1, x2, g1, wz, wf, wg):
    T = x2.shape[0]
    tm = TOKEN_TILE
    nsteps = T // tm

    def body(dz_ref, dq_ref, dk_ref, dv_ref, dfl_ref, dgl_ref, dh1_ref, x_ref, g_ref, wz_ref, wf_ref, wg_ref,
             dx_ref, dg1_ref, acc_g):
        i = pl.program_id(0)

        @pl.when(i == 0)
        def _():
            acc_g[...] = jnp.zeros_like(acc_g)

        dxn = _dot_nt(dz_ref[...], wz_ref[:, 0:1024])
        dxn = dxn + _dot_nt(dq_ref[...], wz_ref[:, 1024:1536])
        dxn = dxn + _dot_nt(dk_ref[...], wz_ref[:, 1536:2048])
        dxn = dxn + _dot_nt(dv_ref[...], wz_ref[:, 2048:2560])
        dxn = dxn + _dot_nt(dfl_ref[...], wf_ref[...])
        dxn = dxn + _dot_nt(dgl_ref[...], wg_ref[...])
        r1, xh = _rms_stats(x_ref[...])
        acc_g[...] += _fold8(dxn * xh)
        dx_ref[...] = dh1_ref[...] + _rms_bwd(dxn, xh, r1, g_ref[...])

        @pl.when(i == nsteps - 1)
        def _():
            dg1_ref[...] = jnp.sum(acc_g[...], axis=0, keepdims=True)

    return pl.pallas_call(
        body, grid=(nsteps,),
        out_shape=(jax.ShapeDtypeStruct((T, D_MODEL), F32), jax.ShapeDtypeStruct((1, D_MODEL), F32)),
        in_specs=[_row_spec(tm, 1024), _row_spec(tm, 512), _row_spec(tm, 512), _row_spec(tm, 512),
                  _row_spec(tm, LANES), _row_spec(tm, 2048), _row_spec(tm, D_MODEL), _row_spec(tm, D_MODEL),
                  _const_spec((1, D_MODEL)), _const_spec((D_MODEL, ZQKV_WIDTH)), _const_spec((D_MODEL, LANES)),
                  _const_spec((D_MODEL, 2048))],
        out_specs=[_row_spec(tm, D_MODEL), pl.BlockSpec((1, D_MODEL), lambda i: (0, 0))],
        scratch_shapes=[pltpu.VMEM((SUBLANES, D_MODEL), F32)],
        compiler_params=_params(48), name="bwd_in",
    )(dz, dq, dk, dv, dfl, dgl, dh1, x2, g1, wz, wf, wg)


def _pad_rows(v, rows):
    return jnp.pad(v, ((0, rows - v.shape[0]), (0, 0)))


def _pack_small(g_mix_pre, b_forget, g_sgu, b_sgu, w_spatial, b_spatial, g_mix_post, g_ffn_pre, g_ffn_post):
    vec = lambda v: _pad_rows(v.reshape(-1, LANES), SUBLANES)
    return jnp.concatenate([
        w_spatial.reshape(-1, LANES), vec(g_mix_pre), _pad_rows(jnp.pad(b_forget, ((0, 0), (0, LANES - N_HEADS))), SUBLANES),
        vec(g_sgu), vec(b_sgu), vec(b_spatial), vec(g_mix_post), vec(g_ffn_pre), vec(g_ffn_post)], axis=0)


def _unpack_small(p):
    nw = N_HEADS * CHUNK * CHUNK // LANES
    blk = lambda k: p[nw + SUBLANES * k: nw + SUBLANES * (k + 1)]
    return dict(
        w_spatial=p[:nw].reshape(1, N_HEADS, CHUNK, CHUNK),
        g_mix_pre=blk(0).reshape(1, D_MODEL), b_forget=blk(1)[0:1, :N_HEADS],
        g_sgu=blk(2)[:4].reshape(1, SGU_WIDTH), b_sgu=blk(3)[:4].reshape(1, SGU_WIDTH),
        b_spatial=blk(4).reshape(1, N_HEADS, CHUNK), g_mix_post=blk(5).reshape(1, D_MODEL),
        g_ffn_pre=blk(6).reshape(1, D_MODEL), g_ffn_post=blk(7).reshape(1, D_MODEL))


def kernel(x, g_mix_pre, w_in, b_forget, g_sgu, b_sgu, w_spatial, b_spatial, w_branch_sgu, w_branch_attn, w_out, g_mix_post, g_ffn_pre, w_up, w_down, g_ffn_post, loss_target, m_g_mix_pre, m_w_in, m_b_forget, m_g_sgu, m_b_sgu, m_w_spatial, m_b_spatial, m_w_branch_sgu, m_w_branch_attn, m_w_out, m_g_mix_post, m_g_ffn_pre, m_w_up, m_w_down, m_g_ffn_post, v_g_mix_pre, v_w_in, v_b_forget, v_g_sgu, v_b_sgu, v_w_spatial, v_b_spatial, v_w_branch_sgu, v_w_branch_attn, v_w_out, v_g_mix_post, v_g_ffn_pre, v_w_up, v_w_down, v_g_ffn_post):
    T = x.shape[1]
    tk = ATTN_TILE
    x2 = x.reshape(T, D_MODEL)
    tgt = loss_target.reshape(T, D_MODEL)
    pad_in = lambda w: jnp.pad(w, ((0, 0), (0, IN_SHARD_PAD - IN_SHARD)))

    wg_in, wbs, wba, wo, wup, wdn = _gather_weights(
        pad_in(w_in[0]), w_branch_sgu[0], w_branch_attn[0], w_out[0], w_up[0], w_down[0])
    w_in_full = jnp.concatenate([wg_in[j, :, :IN_SHARD] for j in range(N_DEV)], axis=1)
    wz = w_in_full[:, :ZQKV_WIDTH]
    wf = jnp.pad(w_in_full[:, ZQKV_WIDTH:GATE_OFFSET], ((0, 0), (0, LANES - N_HEADS)))
    wgt = w_in_full[:, GATE_OFFSET:]
    bfp = jnp.pad(b_forget, ((0, 0), (0, LANES - N_HEADS)))
    wsp = w_spatial[0]
    bT = b_spatial[0].T

    xn, zuv, qkv, fl, gt = _fwd_in(x2, g_mix_pre, wz, wf, wgt)
    cc, ct = _fwd_cum(fl, bfp)
    cr = ct.reshape(N_HEADS * T // tk, tk)
    ys = _fwd_sgu(zuv, g_sgu, b_sgu, wsp, bT)
    ya, lse = _fwd_attn(qkv, cc, cr)
    A, B, mg, o, h1 = _fwd_merge(ys, ya, gt, x2, wbs, wba, wo, g_mix_post)
    xn2, a, ddn, dy, loss_part, dg4 = _fwd_ffn_loss(h1, tgt, wup, wdn, g_ffn_pre, g_ffn_post)
    loss = lax.psum(jnp.sum(loss_part), ("x", "y", "c"))

    da, dh1, dg3 = _bwd_ffn(ddn, a, dy, h1, wup, wdn, g_ffn_pre)
    dw_up = _wgrad(xn2, da, "wgrad_up", tn=2048)
    dw_down = _wgrad(a, ddn, "wgrad_down", relu2=True)
    do, dA, dB, dgl, dys, dya, dsum, dg2 = _bwd_merge(dh1, o, A, B, gt, ya, wbs, wba, wo, g_mix_post)
    dw_out = _wgrad(mg, do, "wgrad_out")
    dw_bs = _wgrad(ys, dA, "wgrad_branch_sgu")
    dw_ba = _wgrad(ya, dB, "wgrad_branch_attn")
    dzuv, dwsp, dbT, dgs, dbs = _bwd_sgu(zuv, dys, g_sgu, b_sgu, wsp, bT)
    zeros = jnp.zeros((T, LANES - 24), F32)
    stats = jnp.concatenate([lse[:, :8], dsum[:, :8], cc[:, :8], zeros], axis=1)
    dq, dk, dv, dcr, dcc = _bwd_attn(qkv, dya, stats, cr)
    dfl, dbf = _bwd_cum(dcr.reshape(N_HEADS, T), dcc, fl, bfp)
    dx, dg1 = _bwd_in(dzuv, dq, dk, dv, dfl, dgl, dh1, x2, g_mix_pre, wz, wf, wgt)
    dw_z = _wgrad(xn, dzuv, "wgrad_in_z")
    dw_q = _wgrad(xn, dq, "wgrad_in_q")
    dw_k = _wgrad(xn, dk, "wgrad_in_k")
    dw_v = _wgrad(xn, dv, "wgrad_in_v")
    dw_f = _wgrad(xn, dfl, "wgrad_in_f")
    dw_g = _wgrad(xn, dgl, "wgrad_in_gate")
    dw_in = jnp.concatenate([dw_z, dw_q, dw_k, dw_v, dw_f[:, :N_HEADS], dw_g], axis=1)

    blocks_in = jnp.pad(dw_in.reshape(D_MODEL, N_DEV, IN_SHARD).transpose(1, 0, 2),
                        ((0, 0), (0, 0), (0, IN_SHARD_PAD - IN_SHARD)))
    col_blocks = lambda g, w: g.reshape(g.shape[0], N_DEV, w).transpose(1, 0, 2)
    row_blocks = lambda g, r: g.reshape(N_DEV, r, g.shape[1])
    g_in, g_bs, g_ba, g_out, g_up, g_down = _reduce_scatter_grads([
        blocks_in, col_blocks(dw_bs, 128), col_blocks(dw_ba, 128), row_blocks(dw_out, 128),
        col_blocks(dw_up, 512), row_blocks(dw_down, 512)])
    g_in = g_in[:, :IN_SHARD]

    small_g = _pack_small(dg1, dbf[:, :N_HEADS], dgs, dbs, dwsp[None], dbT[:, :N_HEADS].T[None], dg2, dg3, dg4)
    small_w = _pack_small(g_mix_pre, b_forget, g_sgu, b_sgu, w_spatial, b_spatial, g_mix_post, g_ffn_pre, g_ffn_post)
    small_m = _pack_small(m_g_mix_pre, m_b_forget, m_g_sgu, m_b_sgu, m_w_spatial, m_b_spatial, m_g_mix_post,
                          m_g_ffn_pre, m_g_ffn_post)
    small_v = _pack_small(v_g_mix_pre, v_b_forget, v_g_sgu, v_b_sgu, v_w_spatial, v_b_spatial, v_g_mix_post,
                          v_g_ffn_pre, v_g_ffn_post)
    sg, sd, sm, sv = (_unpack_small(t) for t in _allreduce_small_adamw(small_g, small_w, small_m, small_v))

    big = {}
    for name, w, g, m, v in (
            ("w_in", w_in, g_in, m_w_in, v_w_in), ("w_branch_sgu", w_branch_sgu, g_bs, m_w_branch_sgu, v_w_branch_sgu),
            ("w_branch_attn", w_branch_attn, g_ba, m_w_branch_attn, v_w_branch_attn),
            ("w_out", w_out, g_out, m_w_out, v_w_out), ("w_up", w_up, g_up, m_w_up, v_w_up),
            ("w_down", w_down, g_down, m_w_down, v_w_down)):
        d_, m_, v_ = _adamw(w[0], g, m[0], v[0], "adamw_" + name)
        big[name] = (g[None], d_[None], m_[None], v_[None])

    order = ["g_mix_pre", "w_in", "b_forget", "g_sgu", "b_sgu", "w_spatial", "b_spatial", "w_branch_sgu",
             "w_branch_attn", "w_out", "g_mix_post", "g_ffn_pre", "w_up", "w_down", "g_ffn_post"]
    outs = [loss, dx.reshape(1, T, D_MODEL)]
    for kind, small in enumerate((sg, sd, sm, sv)):
        outs += [big[nm][kind] if nm in big else small[nm] for nm in order]
    return tuple(outs)
```

```python
import jax
import jax.numpy as jnp
from jax import lax
from jax.experimental import pallas as pl
from jax.experimental.pallas import tpu as pltpu

F32 = jnp.float32
BF16 = jnp.bfloat16
HIGHEST = lax.Precision.HIGHEST
MESH = pl.DeviceIdType.MESH

D_MODEL = 1024
SGU_WIDTH = 512
ATTN_WIDTH = 512
N_HEADS = 8
CHUNK = 128
D_FF = 4096
IN_WIDTH = 4616
N_DEV = 8
IN_SHARD = IN_WIDTH // N_DEV
IN_SHARD_PAD = 640
ZQKV_WIDTH = 2 * SGU_WIDTH + 3 * ATTN_WIDTH
GATE_OFFSET = ZQKV_WIDTH + N_HEADS
EPS = 1e-6
LANES = 128
SUBLANES = 8
VMEM_BYTES = 64 * 1024 * 1024
MIB = 1024 * 1024

ADAM_LR = 0.001
ADAM_B1 = 0.9
ADAM_B2 = 0.999
ADAM_EPS = 1e-08
ADAM_WD = 0.01
ADAM_STEP = 10

TOKEN_TILE = 256
ATTN_TILE = 512
CUM_TILE = 256
SGU_TILE = 512
WGRAD_TILE = 512
NEG = -1e30

NT_DIMS = (((1,), (1,)), ((), ()))
TN_DIMS = (((0,), (0,)), ((), ()))


def _params(vmem_mb, n_grid=1):
    return pltpu.CompilerParams(
        dimension_semantics=("arbitrary",) * n_grid,
        vmem_limit_bytes=min(vmem_mb * MIB, VMEM_BYTES - 6 * MIB),
    )


def _dot(a, b):
    return jnp.dot(a, b, preferred_element_type=F32)


def _dot_nt(a, b):
    return lax.dot_general(a, b, NT_DIMS, preferred_element_type=F32)


def _dot_tn(a, b):
    return lax.dot_general(a, b, TN_DIMS, preferred_element_type=F32)


def _const_spec(shape):
    nd = len(shape)
    return pl.BlockSpec(shape, lambda *_: (0,) * nd, pipeline_mode=pl.Buffered(1))


def _row_spec(tm, n, col=0):
    return pl.BlockSpec((tm, n), lambda i: (i, col))


def _fold8(v):
    return v.reshape(v.shape[0] // SUBLANES, SUBLANES, v.shape[1]).sum(axis=0)


def _pick(v, lane_iota, k):
    return jnp.sum(jnp.where(lane_iota == k, v, 0.0), axis=1, keepdims=True)


def _iota(shape, dim):
    return lax.broadcasted_iota(jnp.int32, shape, dim)


def _gelu(x):
    c = 0.7978845608028654
    return 0.5 * x * (1.0 + jnp.tanh(c * (x + 0.044715 * x * x * x)))


def _gelu_grad(x):
    c = 0.7978845608028654
    t = jnp.tanh(c * (x + 0.044715 * x * x * x))
    return 0.5 * (1.0 + t) + 0.5 * x * (1.0 - t * t) * (c * (1.0 + 3.0 * 0.044715 * x * x))


def _rms_stats(v):
    r = lax.rsqrt(jnp.mean(v * v, axis=1, keepdims=True) + EPS)
    return r, v * r


def _rms_bwd(dout, vhat, r, g):
    a = dout * g
    return r * (a - vhat * jnp.mean(a * vhat, axis=1, keepdims=True))


def _mesh_pos():
    return lax.axis_index("x"), lax.axis_index("y"), lax.axis_index("c")


def _dev_index(px, py, pc):
    return 4 * px + 2 * py + pc


def _other_chips(x, y):
    return [(1 - x, y), (x, 1 - y), (1 - x, 1 - y)]


def _gather_weights(w_in_p, w_bs, w_ba, w_out, w_up, w_down):
    shards = (w_in_p, w_bs, w_ba, w_out, w_up, w_down)
    n = len(shards)
    out_shapes = (
        jax.ShapeDtypeStruct((N_DEV, D_MODEL, IN_SHARD_PAD), BF16),
        jax.ShapeDtypeStruct((SGU_WIDTH, D_MODEL), BF16),
        jax.ShapeDtypeStruct((ATTN_WIDTH, D_MODEL), BF16),
        jax.ShapeDtypeStruct((D_MODEL, D_MODEL), BF16),
        jax.ShapeDtypeStruct((D_MODEL, D_FF), BF16),
        jax.ShapeDtypeStruct((D_FF, D_MODEL), BF16),
    )

    def view(a, ref, j):
        if a == 0:
            return ref.at[j]
        rows, cols = shards[a].shape
        if cols == D_MODEL:
            return ref.at[pl.ds(pl.multiple_of(j * rows, rows), rows), :]
        return ref.at[:, pl.ds(pl.multiple_of(j * cols, cols), cols)]

    def body(*refs):
        ins, outs, stage = refs[:n], refs[n:2 * n], refs[2 * n:3 * n]
        send_sems, recv_sems, local_sems = refs[3 * n:]
        x, y, c = _mesh_pos()
        me, sibling = (x, y, c), (x, y, 1 - c)
        chips = _other_chips(x, y)

        def copy(a, k, block, to, from_stage=False):
            dst = view(a, outs[a], _dev_index(*block))
            return pltpu.make_async_remote_copy(
                src_ref=stage[a] if from_stage else dst, dst_ref=dst,
                send_sem=send_sems.at[a, k], recv_sem=recv_sems.at[a, k],
                device_id=to, device_id_type=MESH)

        local = []
        for a in range(n):
            stage[a][...] = ins[a][...].astype(BF16)
            mine = pltpu.make_async_copy(stage[a], view(a, outs[a], _dev_index(*me)), local_sems.at[a])
            mine.start()
            local.append(mine)
        sends = []
        for a in range(n):
            first = [copy(a, 0, me, sibling, True)]
            first += [copy(a, 1 + j, me, (*chip, c), True) for j, chip in enumerate(chips)]
            for cp in first:
                cp.start()
            sends += first
        for a in range(n):
            for j, chip in enumerate(chips):
                copy(a, 1 + j, (*chip, c), me).wait_recv()
                fwd = copy(a, 4 + j, (*chip, c), sibling)
                fwd.start()
                sends.append(fwd)
        for a in range(n):
            copy(a, 0, sibling, me).wait_recv()
            for j, chip in enumerate(chips):
                copy(a, 4 + j, (*chip, 1 - c), me).wait_recv()
        for cp in sends:
            cp.wait_send()
        for mine in local:
            mine.wait()

    return pl.pallas_call(
        body,
        out_shape=out_shapes,
        in_specs=[pl.BlockSpec(memory_space=pltpu.VMEM)] * n,
        out_specs=[pl.BlockSpec(memory_space=pl.ANY)] * n,
        scratch_shapes=[pltpu.VMEM(s.shape, BF16) for s in shards]
        + [pltpu.SemaphoreType.DMA((n, 7)), pltpu.SemaphoreType.DMA((n, 7)), pltpu.SemaphoreType.DMA((n,))],
        compiler_params=pltpu.CompilerParams(vmem_limit_bytes=32 * MIB),
        name="gather_weights",
    )(*shards)


def _reduce_scatter_grads(grads):
    n = len(grads)
    shapes = [tuple(g.shape[1:]) for g in grads]
    max_rows = max(s[0] for s in shapes)
    max_cols = max(s[1] for s in shapes)

    def body(*refs):
        ins = refs[:n]
        outs = refs[n:2 * n]
        land1 = refs[2 * n:3 * n]
        stage = refs[3 * n:4 * n]
        land2 = refs[4 * n:5 * n]
        buf_a, buf_b, s1_send, s1_recv, s2_send, s2_recv, lsem = refs[5 * n:]
        x, y, c = _mesh_pos()
        sibling = (x, y, 1 - c)
        chips = _other_chips(x, y) + [(x, y)]

        def s1_copy(a, k):
            cx, cy = chips[k]
            return pltpu.make_async_remote_copy(
                src_ref=ins[a].at[_dev_index(cx, cy, 1 - c)], dst_ref=land1[a].at[k],
                send_sem=s1_send.at[a, k], recv_sem=s1_recv.at[a, k],
                device_id=sibling, device_id_type=MESH)

        def s2_copy(a, k):
            cx, cy = chips[k]
            return pltpu.make_async_remote_copy(
                src_ref=stage[a].at[k], dst_ref=land2[a].at[k],
                send_sem=s2_send.at[a, k], recv_sem=s2_recv.at[a, k],
                device_id=(cx, cy, c), device_id_type=MESH)

        def chip_partial(a, k):
            rows, cols = shapes[a]
            va = buf_a.at[pl.ds(0, rows), pl.ds(0, cols)]
            vb = buf_b.at[pl.ds(0, rows), pl.ds(0, cols)]
            cx, cy = chips[k]
            la = pltpu.make_async_copy(ins[a].at[_dev_index(cx, cy, c)], va, lsem.at[0])
            lb = pltpu.make_async_copy(land1[a].at[k], vb, lsem.at[1])
            la.start()
            lb.start()
            la.wait()
            lb.wait()
            return va[...] + vb[...]

        sends = []
        for a in range(n):
            for k in range(4):
                cp = s1_copy(a, k)
                cp.start()
                sends.append(cp)
        for a in range(n):
            for k in range(3):
                s1_copy(a, k).wait_recv()
                stage[a][k] = chip_partial(a, k).astype(BF16)
                cp = s2_copy(a, k)
                cp.start()
                sends.append(cp)
        for a in range(n):
            s1_copy(a, 3).wait_recv()
            acc = chip_partial(a, 3)
            for k in range(3):
                s2_copy(a, k).wait_recv()
                acc = acc + land2[a][k].astype(F32)
            outs[a][...] = acc
        for cp in sends:
            cp.wait_send()

    out_shape = tuple(jax.ShapeDtypeStruct(s, F32) for s in shapes)
    out_shape += tuple(jax.ShapeDtypeStruct((4,) + s, F32) for s in shapes)
    res = pl.pallas_call(
        body,
        out_shape=out_shape,
        in_specs=[pl.BlockSpec(memory_space=pl.ANY)] * n,
        out_specs=[pl.BlockSpec(memory_space=pltpu.VMEM)] * n + [pl.BlockSpec(memory_space=pl.ANY)] * n,
        scratch_shapes=[pltpu.VMEM((3,) + s, BF16) for s in shapes]
        + [pltpu.VMEM((3,) + s, BF16) for s in shapes]
        + [pltpu.VMEM((max_rows, max_cols), F32), pltpu.VMEM((max_rows, max_cols), F32),
           pltpu.SemaphoreType.DMA((n, 4)), pltpu.SemaphoreType.DMA((n, 4)),
           pltpu.SemaphoreType.DMA((n, 3)), pltpu.SemaphoreType.DMA((n, 3)),
           pltpu.SemaphoreType.DMA((2,))],
        compiler_params=pltpu.CompilerParams(vmem_limit_bytes=56 * MIB),
        name="reduce_scatter_grads",
    )(*grads)
    return res[:n]


def _adamw_math(w, g, m, v):
    m = ADAM_B1 * m + (1.0 - ADAM_B1) * g
    v = ADAM_B2 * v + (1.0 - ADAM_B2) * (g * g)
    m_hat = m / (1.0 - ADAM_B1 ** ADAM_STEP)
    v_hat = v / (1.0 - ADAM_B2 ** ADAM_STEP)
    delta = -ADAM_LR * (m_hat / (jnp.sqrt(v_hat) + ADAM_EPS) + ADAM_WD * w)
    return delta, m, v


def _allreduce_small_adamw(g, w, m, v):
    shape = g.shape

    def body(g_ref, w_ref, m_ref, v_ref, gs_ref, d_ref, nm_ref, nv_ref, sib, psum, land, send_sems, recv_sems):
        x, y, c = _mesh_pos()
        sibling = (x, y, 1 - c)
        chips = _other_chips(x, y)
        first = pltpu.make_async_remote_copy(
            src_ref=g_ref, dst_ref=sib, send_sem=send_sems.at[0], recv_sem=recv_sems.at[0],
            device_id=sibling, device_id_type=MESH)
        first.start()
        first.wait_recv()
        psum[...] = g_ref[...] + sib[...]
        second = []
        for k, (cx, cy) in enumerate(chips):
            cp = pltpu.make_async_remote_copy(
                src_ref=psum, dst_ref=land.at[k], send_sem=send_sems.at[1 + k], recv_sem=recv_sems.at[1 + k],
                device_id=(cx, cy, c), device_id_type=MESH)
            cp.start()
            second.append(cp)
        for cp in second:
            cp.wait_recv()
        total = (psum[...] + land[0]) + (land[1] + land[2])
        gs_ref[...] = total
        delta, nm, nv = _adamw_math(w_ref[...], total, m_ref[...], v_ref[...])
        d_ref[...] = delta
        nm_ref[...] = nm
        nv_ref[...] = nv
        first.wait_send()
        for cp in second:
            cp.wait_send()

    sd = jax.ShapeDtypeStruct(shape, F32)
    return pl.pallas_call(
        body,
        out_shape=(sd, sd, sd, sd),
        in_specs=[pl.BlockSpec(memory_space=pltpu.VMEM)] * 4,
        out_specs=[pl.BlockSpec(memory_space=pltpu.VMEM)] * 4,
        scratch_shapes=[pltpu.VMEM(shape, F32), pltpu.VMEM(shape, F32), pltpu.VMEM((3,) + shape, F32),
                        pltpu.SemaphoreType.DMA((4,)), pltpu.SemaphoreType.DMA((4,))],
        compiler_params=pltpu.CompilerParams(vmem_limit_bytes=32 * MIB),
        name="allreduce_small_adamw",
    )(g, w, m, v)


def _adamw(w, g, m, v, name):
    rows, cols = w.shape
    tm = 256 if rows % 256 == 0 else rows

    def body(w_ref, g_ref, m_ref, v_ref, d_ref, nm_ref, nv_ref):
        delta, nm, nv = _adamw_math(w_ref[...], g_ref[...], m_ref[...], v_ref[...])
        d_ref[...] = delta
        nm_ref[...] = nm
        nv_ref[...] = nv

    sd = jax.ShapeDtypeStruct((rows, cols), F32)
    spec = _row_spec(tm, cols)
    return pl.pallas_call(
        body, grid=(rows // tm,), out_shape=(sd, sd, sd), in_specs=[spec] * 4, out_specs=[spec] * 3,
        compiler_params=_params(32), name=name,
    )(w, g, m, v)


def _fwd_in(x2, g1, wz, wf, wg):
    T = x2.shape[0]
    tm = TOKEN_TILE

    def body(x_ref, g_ref, wz_ref, wf_ref, wg_ref, xn_ref, zuv_ref, qkv_ref, fl_ref, gt_ref):
        x = x_ref[...]
        r, xh = _rms_stats(x)
        xn = (xh * g_ref[...]).astype(BF16)
        xn_ref[...] = xn
        zuv_ref[...] = _dot(xn, wz_ref[:, 0:1024]).astype(BF16)
        qkv_ref[:, 0:512] = (_dot(xn, wz_ref[:, 1024:1536]) * 0.125).astype(BF16)
        qkv_ref[:, 512:1536] = _dot(xn, wz_ref[:, 1536:2560]).astype(BF16)
        fl_ref[...] = _dot(xn, wf_ref[...])
        gt_ref[...] = jax.nn.sigmoid(_dot(xn, wg_ref[...])).astype(BF16)

    return pl.pallas_call(
        body, grid=(T // tm,),
        out_shape=(jax.ShapeDtypeStruct((T, D_MODEL), BF16), jax.ShapeDtypeStruct((T, 1024), BF16),
                   jax.ShapeDtypeStruct((T, 1536), BF16), jax.ShapeDtypeStruct((T, LANES), F32),
                   jax.ShapeDtypeStruct((T, 2048), BF16)),
        in_specs=[_row_spec(tm, D_MODEL), _const_spec((1, D_MODEL)), _const_spec((D_MODEL, ZQKV_WIDTH)),
                  _const_spec((D_MODEL, LANES)), _const_spec((D_MODEL, 2048))],
        out_specs=[_row_spec(tm, D_MODEL), _row_spec(tm, 1024), _row_spec(tm, 1536), _row_spec(tm, LANES),
                   _row_spec(tm, 2048)],
        compiler_params=_params(48), name="fwd_in",
    )(x2, g1, wz, wf, wg)


def _log_sigmoid(f):
    return jnp.minimum(f, 0.0) - jnp.log1p(jnp.exp(-jnp.abs(f)))


AUG_LANES = 6


def _split3(v):
    hi = v.astype(BF16)
    r1 = v - hi.astype(F32)
    mid = r1.astype(BF16)
    lo = (r1 - mid.astype(F32)).astype(BF16)
    return hi, mid, lo


def _spread(parts, k0):
    r, c = _iota((LANES, LANES), 0), _iota((LANES, LANES), 1)
    out = None
    for i, part in enumerate(parts):
        e = ((c == AUG_LANES * r + (k0 + i)) & (r < N_HEADS)).astype(BF16)
        term = _dot(part, e)
        out = term if out is None else out + term
    return out


def _aug_query(v):
    ones = (_iota(v.shape, 1) < N_HEADS).astype(BF16)
    return (_spread(_split3(v), 0) + _spread((ones, ones, ones), 3)).astype(BF16)


def _aug_key(v):
    ones = (_iota(v.shape, 1) < N_HEADS).astype(BF16)
    return (_spread((ones, ones, ones), 0) - _spread(_split3(v), 3)).astype(BF16)


def _aug_stack(t2, aug, p):
    lane = _iota(t2.shape, 1)
    low = lane < 64
    zero = jnp.zeros_like(t2)
    first = 2 * AUG_LANES * p
    a_e = jnp.where((lane >= first) & (lane < first + AUG_LANES), aug, zero)
    a_o = jnp.where((lane >= first + AUG_LANES) & (lane < first + 2 * AUG_LANES), aug, zero)
    top = jnp.concatenate([jnp.where(low, t2, zero), a_e], axis=1)
    bot = jnp.concatenate([jnp.where(low, zero, t2), a_o], axis=1)
    return jnp.concatenate([top, bot], axis=0)


def _fwd_cum(fl, bfp):
    T = fl.shape[0]
    tb = CUM_TILE

    def body(fl_ref, b_ref, cc_ref, qa_ref, ka_ref):
        tri = (_iota((tb, tb), 0) >= _iota((tb, tb), 1)).astype(F32)
        carry = jnp.zeros((1, LANES), F32)
        for i in range(T // tb):
            rows = slice(i * tb, (i + 1) * tb)
            lf = _log_sigmoid(fl_ref[rows, :] + b_ref[...])
            cs = jnp.dot(tri, lf, precision=HIGHEST, preferred_element_type=F32) + carry
            cc_ref[rows, :] = cs
            carry = cs[tb - 1:tb, :]
            qa_ref[rows, :] = _aug_query(cs)
            ka_ref[rows, :] = _aug_key(cs)

    return pl.pallas_call(
        body,
        out_shape=(jax.ShapeDtypeStruct((T, LANES), F32), jax.ShapeDtypeStruct((T, LANES), BF16),
                   jax.ShapeDtypeStruct((T, LANES), BF16)),
        compiler_params=pltpu.CompilerParams(vmem_limit_bytes=32 * MIB), name="fwd_cum",
    )(fl, bfp)


def _sgu_forward_parts(z, gs, bs):
    u = _gelu(z[:, :SGU_WIDTH])
    vv = _gelu(z[:, SGU_WIDTH:])
    vc = vv - jnp.mean(vv, axis=1, keepdims=True)
    rs = lax.rsqrt(jnp.mean(vc * vc, axis=1, keepdims=True) + EPS)
    vhat = vc * rs
    return u, vhat, rs, vhat * gs + bs


def _sgu_pair_weights(w_ref, bT, p):
    tril = _iota((CHUNK, CHUNK), 0) >= _iota((CHUNK, CHUNK), 1)
    we = jnp.where(tril, w_ref[2 * p], 0.0).astype(BF16)
    wo = jnp.where(tril, w_ref[2 * p + 1], 0.0).astype(BF16)
    lane8 = _iota(bT.shape, 1)
    low = _iota((CHUNK, LANES), 1) < 64
    b2 = jnp.where(low, _pick(bT, lane8, 2 * p), _pick(bT, lane8, 2 * p + 1))
    return we, wo, b2


def _chunks_on_lanes(v, p, nc):
    return jnp.concatenate([v[c * CHUNK:(c + 1) * CHUNK, LANES * p:LANES * (p + 1)] for c in range(nc)], axis=1)


def _sgu_mix(we, wo, b2, vcat, nc):
    low = (_iota((CHUNK, nc * LANES), 1) % LANES) < 64
    return jnp.where(low, _dot(we, vcat), _dot(wo, vcat)) + jnp.concatenate([b2] * nc, axis=1)


def _fwd_sgu(zuv, gs, bs, wsp, bT):
    T = zuv.shape[0]
    tc = SGU_TILE
    nc = tc // CHUNK

    def body(z_ref, gs_ref, bs_ref, w_ref, bT_ref, y_ref):
        u, _, _, vln = _sgu_forward_parts(z_ref[...].astype(F32), gs_ref[...], bs_ref[...])
        vb = vln.astype(BF16)
        for p in range(4):
            we, wo, b2 = _sgu_pair_weights(w_ref, bT_ref[...], p)
            s = _sgu_mix(we, wo, b2, _chunks_on_lanes(vb, p, nc), nc)
            for c in range(nc):
                rows, cols = slice(c * CHUNK, (c + 1) * CHUNK), slice(LANES * p, LANES * (p + 1))
                y_ref[rows, cols] = (u[rows, cols] * s[:, c * LANES:(c + 1) * LANES]).astype(BF16)

    return pl.pallas_call(
        body, grid=(T // tc,), out_shape=jax.ShapeDtypeStruct((T, SGU_WIDTH), BF16),
        in_specs=[_row_spec(tc, 1024), _const_spec((1, SGU_WIDTH)), _const_spec((1, SGU_WIDTH)),
                  _const_spec((8, CHUNK, CHUNK)), _const_spec((CHUNK, 8))],
        out_specs=_row_spec(tc, SGU_WIDTH),
        compiler_params=_params(40), name="fwd_sgu",
    )(zuv, gs, bs, wsp, bT)


def _fwd_attn(qkv, qaug, kaug):
    T = qkv.shape[0]
    tq = tk = ATTN_TILE

    def body(q_ref, qa_ref, k_ref, v_ref, ka_ref, o_ref, lse_ref):
        i = pl.program_id(0)
        lane = _iota((tq, LANES), 1)
        low = lane < 64
        lowk = _iota((tk, LANES), 1) < 64
        one = jnp.ones((tk, LANES), BF16)
        row = _iota((2 * tq, tk), 0) % tq
        col = _iota((2 * tq, tk), 1)
        cols = [slice(LANES * p, LANES * (p + 1)) for p in range(4)]
        qa = qa_ref[...]
        qs = [_aug_stack(q_ref[:, cols[p]], qa, p) for p in range(4)]

        def step(j, carry, masked):
            ks = pl.ds(pl.multiple_of(j * tk, tk), tk)
            ka = ka_ref[ks, :]
            new = []
            for p in range(4):
                m, acc_e, acc_o = carry[p]
                v2 = v_ref[ks, cols[p]]
                s = _dot_nt(qs[p], jnp.concatenate([k_ref[ks, cols[p]], ka], axis=1))
                if masked:
                    s = jnp.where(col <= row, s, NEG)
                mn = jnp.maximum(m, jnp.max(s, axis=1, keepdims=True))
                al = jnp.exp(m - mn)
                pm = jnp.exp(s - mn).astype(BF16)
                acc_e = al[:tq] * acc_e + _dot(pm[:tq], jnp.where(lowk, v2, one))
                acc_o = al[tq:] * acc_o + _dot(pm[tq:], jnp.where(lowk, one, v2))
                new.append((mn, acc_e, acc_o))
            return tuple(new)

        init = tuple((jnp.full((2 * tq, 1), NEG, F32), jnp.zeros((tq, LANES), F32), jnp.zeros((tq, LANES), F32))
                     for _ in range(4))
        carry = lax.fori_loop(0, i, lambda j, c: step(j, c, False), init)
        carry = step(i, carry, True)
        lse_blk = jnp.zeros((tq, LANES), F32)
        for p in range(4):
            m, acc_e, acc_o = carry[p]
            l_e = pltpu.roll(acc_e, 64, 1)
            l_o = pltpu.roll(acc_o, 64, 1)
            o_ref[:, cols[p]] = jnp.where(low, acc_e / l_e, acc_o / l_o).astype(BF16)
            lse_blk = jnp.where(lane == 2 * p, m[:tq] + jnp.log(l_e), lse_blk)
            lse_blk = jnp.where(lane == 2 * p + 1, m[tq:] + jnp.log(acc_o), lse_blk)
        lse_ref[...] = lse_blk

    return pl.pallas_call(
        body, grid=(T // tq,),
        out_shape=(jax.ShapeDtypeStruct((T, ATTN_WIDTH), BF16), jax.ShapeDtypeStruct((T, LANES), F32)),
        in_specs=[_row_spec(tq, 512), _row_spec(tq, LANES),
                  pl.BlockSpec((T, 512), lambda i: (0, 1), pipeline_mode=pl.Buffered(1)),
                  pl.BlockSpec((T, 512), lambda i: (0, 2), pipeline_mode=pl.Buffered(1)),
                  _const_spec((T, LANES))],
        out_specs=[_row_spec(tq, ATTN_WIDTH), _row_spec(tq, LANES)],
        compiler_params=_params(56), name="fwd_attn",
    )(qkv, qaug, qkv, qkv, kaug)


def _fwd_merge(ys, ya, gt, x2, wbs, wba, wo, g2):
    T = x2.shape[0]
    tm = TOKEN_TILE

    def body(ys_ref, ya_ref, gt_ref, x_ref, wbs_ref, wba_ref, wo_ref, g2_ref, a_ref, b_ref, mg_ref, o_ref, h1_ref):
        A = _dot(ys_ref[...], wbs_ref[...])
        B = _dot(ya_ref[...], wba_ref[...])
        mg = (gt_ref[:, :D_MODEL].astype(F32) * A + gt_ref[:, D_MODEL:].astype(F32) * B).astype(BF16)
        o = _dot(mg, wo_ref[...])
        r2, oh = _rms_stats(o)
        a_ref[...] = A.astype(BF16)
        b_ref[...] = B.astype(BF16)
        mg_ref[...] = mg
        o_ref[...] = o.astype(BF16)
        h1_ref[...] = x_ref[...] + oh * g2_ref[...]

    sd = jax.ShapeDtypeStruct((T, D_MODEL), BF16)
    return pl.pallas_call(
        body, grid=(T // tm,),
        out_shape=(sd, sd, sd, sd, jax.ShapeDtypeStruct((T, D_MODEL), F32)),
        in_specs=[_row_spec(tm, 512), _row_spec(tm, 512), _row_spec(tm, 2048), _row_spec(tm, D_MODEL),
                  _const_spec((512, D_MODEL)), _const_spec((512, D_MODEL)), _const_spec((D_MODEL, D_MODEL)),
                  _const_spec((1, D_MODEL))],
        out_specs=[_row_spec(tm, D_MODEL)] * 5,
        compiler_params=_params(40), name="fwd_merge",
    )(ys, ya, gt, x2, wbs, wba, wo, g2)


def _fwd_ffn_loss(h1, tgt, wup, wdn, g3, g4):
    T = h1.shape[0]
    tm = TOKEN_TILE
    nsteps = T // tm

    def body(h1_ref, tg_ref, wup_ref, wdn_ref, g3_ref, g4_ref, xn2_ref, a_ref, ddn_ref, dy_ref, loss_ref,
             dg4_ref, acc_l, acc_g):
        i = pl.program_id(0)

        @pl.when(i == 0)
        def _():
            acc_l[...] = jnp.zeros_like(acc_l)
            acc_g[...] = jnp.zeros_like(acc_g)

        h1v = h1_ref[...]
        r3, h1h = _rms_stats(h1v)
        xn2 = (h1h * g3_ref[...]).astype(BF16)
        xn2_ref[...] = xn2
        dn = jnp.zeros((tm, D_MODEL), F32)
        for j in range(D_FF // 1024):
            cols = slice(1024 * j, 1024 * (j + 1))
            a = _dot(xn2, wup_ref[:, cols])
            a_ref[:, cols] = a.astype(BF16)
            hid = jnp.square(jnp.maximum(a, 0.0)).astype(BF16)
            dn = dn + _dot(hid, wdn_ref[cols, :])
        r4, dnh = _rms_stats(dn)
        g4v = g4_ref[...]
        e = (h1v + dnh * g4v) - tg_ref[...]
        sq = e * e
        s1 = sq[:, 0:LANES]
        for j in range(1, D_MODEL // LANES):
            s1 = s1 + sq[:, LANES * j:LANES * (j + 1)]
        acc_l[...] += _fold8(s1)
        dy = e * (1.0 / D_MODEL)
        dy_ref[...] = dy
        acc_g[...] += _fold8(dy * dnh)
        ddn_ref[...] = _rms_bwd(dy, dnh, r4, g4v).astype(BF16)

        @pl.when(i == nsteps - 1)
        def _():
            loss_ref[...] = acc_l[...] * (0.5 / D_MODEL)
            dg4_ref[...] = jnp.sum(acc_g[...], axis=0, keepdims=True)

    return pl.pallas_call(
        body, grid=(nsteps,),
        out_shape=(jax.ShapeDtypeStruct((T, D_MODEL), BF16), jax.ShapeDtypeStruct((T, D_FF), BF16),
                   jax.ShapeDtypeStruct((T, D_MODEL), BF16), jax.ShapeDtypeStruct((T, D_MODEL), F32),
                   jax.ShapeDtypeStruct((SUBLANES, LANES), F32), jax.ShapeDtypeStruct((1, D_MODEL), F32)),
        in_specs=[_row_spec(tm, D_MODEL), _row_spec(tm, D_MODEL), _const_spec((D_MODEL, D_FF)),
                  _const_spec((D_FF, D_MODEL)), _const_spec((1, D_MODEL)), _const_spec((1, D_MODEL))],
        out_specs=[_row_spec(tm, D_MODEL), _row_spec(tm, D_FF), _row_spec(tm, D_MODEL), _row_spec(tm, D_MODEL),
                   pl.BlockSpec((SUBLANES, LANES), lambda i: (0, 0)), pl.BlockSpec((1, D_MODEL), lambda i: (0, 0))],
        scratch_shapes=[pltpu.VMEM((SUBLANES, LANES), F32), pltpu.VMEM((SUBLANES, D_MODEL), F32)],
        compiler_params=_params(52), name="fwd_ffn_loss",
    )(h1, tgt, wup, wdn, g3, g4)


def _bwd_ffn(ddn, a, dy, h1, wup, wdn, g3):
    T = h1.shape[0]
    tm = TOKEN_TILE
    nsteps = T // tm

    def body(ddn_ref, a_ref, dy_ref, h1_ref, wup_ref, wdn_ref, g3_ref, da_ref, dh1_ref, dg3_ref, acc_g):
        i = pl.program_id(0)

        @pl.when(i == 0)
        def _():
            acc_g[...] = jnp.zeros_like(acc_g)

        ddnv = ddn_ref[...]
        dxn2 = jnp.zeros((tm, D_MODEL), F32)
        for j in range(D_FF // 1024):
            cols = slice(1024 * j, 1024 * (j + 1))
            dhid = _dot_nt(ddnv, wdn_ref[cols, :])
            da = (dhid * (2.0 * jnp.maximum(a_ref[:, cols].astype(F32), 0.0))).astype(BF16)
            da_ref[:, cols] = da
            dxn2 = dxn2 + _dot_nt(da, wup_ref[:, cols])
        r3, h1h = _rms_stats(h1_ref[...])
        acc_g[...] += _fold8(dxn2 * h1h)
        dh1_ref[...] = dy_ref[...] + _rms_bwd(dxn2, h1h, r3, g3_ref[...])

        @pl.when(i == nsteps - 1)
        def _():
            dg3_ref[...] = jnp.sum(acc_g[...], axis=0, keepdims=True)

    return pl.pallas_call(
        body, grid=(nsteps,),
        out_shape=(jax.ShapeDtypeStruct((T, D_FF), BF16), jax.ShapeDtypeStruct((T, D_MODEL), F32),
                   jax.ShapeDtypeStruct((1, D_MODEL), F32)),
        in_specs=[_row_spec(tm, D_MODEL), _row_spec(tm, D_FF), _row_spec(tm, D_MODEL), _row_spec(tm, D_MODEL),
                  _const_spec((D_MODEL, D_FF)), _const_spec((D_FF, D_MODEL)), _const_spec((1, D_MODEL))],
        out_specs=[_row_spec(tm, D_FF), _row_spec(tm, D_MODEL), pl.BlockSpec((1, D_MODEL), lambda i: (0, 0))],
        scratch_shapes=[pltpu.VMEM((SUBLANES, D_MODEL), F32)],
        compiler_params=_params(52), name="bwd_ffn",
    )(ddn, a, dy, h1, wup, wdn, g3)


def _wgrad(xa, dy, name, relu2=False, tn=None):
    T, K = xa.shape
    N = dy.shape[1]
    tn = N if tn is None else tn
    tt = WGRAD_TILE

    def body(x_ref, dy_ref, o_ref):
        @pl.when(pl.program_id(1) == 0)
        def _():
            o_ref[...] = jnp.zeros_like(o_ref)

        xv = x_ref[...]
        if relu2:
            xv = jnp.square(jnp.maximum(xv.astype(F32), 0.0)).astype(BF16)
        o_ref[...] += _dot_tn(xv, dy_ref[...])

    return pl.pallas_call(
        body, grid=(N // tn, T // tt), out_shape=jax.ShapeDtypeStruct((K, N), F32),
        in_specs=[pl.BlockSpec((tt, K), lambda n, t: (t, 0)), pl.BlockSpec((tt, tn), lambda n, t: (t, n))],
        out_specs=pl.BlockSpec((K, tn), lambda n, t: (0, n)),
        compiler_params=_params(52, 2), name=name,
    )(xa, dy)


def _bwd_merge(dh1, o, A, B, gt, ya, lse, cc, wbs, wba, wo, g2):
    T = dh1.shape[0]
    tm = TOKEN_TILE
    nsteps = T // tm

    def body(dh1_ref, o_ref, a_ref, b_ref, gt_ref, ya_ref, lse_ref, cc_ref, wbs_ref, wba_ref, wo_ref, g2_ref,
             do_ref, da_ref, db_ref, dgl_ref, dys_ref, dya_ref, qab_ref, dab_ref, dg2_ref, acc_g):
        i = pl.program_id(0)

        @pl.when(i == 0)
        def _():
            acc_g[...] = jnp.zeros_like(acc_g)

        dh1v = dh1_ref[...]
        r2, oh = _rms_stats(o_ref[...].astype(F32))
        acc_g[...] += _fold8(dh1v * oh)
        do = _rms_bwd(dh1v, oh, r2, g2_ref[...]).astype(BF16)
        do_ref[...] = do
        dmg = _dot_nt(do, wo_ref[...])
        ga = gt_ref[:, :D_MODEL].astype(F32)
        gb = gt_ref[:, D_MODEL:].astype(F32)
        dgl_ref[:, :D_MODEL] = (dmg * a_ref[...].astype(F32) * ga * (1.0 - ga)).astype(BF16)
        dgl_ref[:, D_MODEL:] = (dmg * b_ref[...].astype(F32) * gb * (1.0 - gb)).astype(BF16)
        dA = (dmg * ga).astype(BF16)
        dB = (dmg * gb).astype(BF16)
        da_ref[...] = dA
        db_ref[...] = dB
        dys_ref[...] = _dot_nt(dA, wbs_ref[...]).astype(BF16)
        dya = _dot_nt(dB, wba_ref[...]).astype(BF16)
        dya_ref[...] = dya
        prod = dya.astype(F32) * ya_ref[...].astype(F32)
        lane = _iota((tm, LANES), 1)
        low = lane < 64
        blk = jnp.zeros((tm, LANES), F32)
        for p in range(4):
            pp = prod[:, LANES * p:LANES * (p + 1)]
            blk = jnp.where(lane == 2 * p, jnp.sum(jnp.where(low, pp, 0.0), axis=1, keepdims=True), blk)
            blk = jnp.where(lane == 2 * p + 1, jnp.sum(jnp.where(low, 0.0, pp), axis=1, keepdims=True), blk)
        qab_ref[...] = _aug_query(cc_ref[...] - lse_ref[...])
        dab_ref[...] = _spread(_split3(-blk), 0).astype(BF16)

        @pl.when(i == nsteps - 1)
        def _():
            dg2_ref[...] = jnp.sum(acc_g[...], axis=0, keepdims=True)

    sd = jax.ShapeDtypeStruct((T, D_MODEL), BF16)
    sh = jax.ShapeDtypeStruct((T, 512), BF16)
    sa = jax.ShapeDtypeStruct((T, LANES), BF16)
    return pl.pallas_call(
        body, grid=(nsteps,),
        out_shape=(sd, sd, sd, jax.ShapeDtypeStruct((T, 2048), BF16), sh, sh, sa, sa,
                   jax.ShapeDtypeStruct((1, D_MODEL), F32)),
        in_specs=[_row_spec(tm, D_MODEL)] * 4 + [_row_spec(tm, 2048), _row_spec(tm, 512), _row_spec(tm, LANES),
                  _row_spec(tm, LANES), _const_spec((512, D_MODEL)), _const_spec((512, D_MODEL)),
                  _const_spec((D_MODEL, D_MODEL)), _const_spec((1, D_MODEL))],
        out_specs=[_row_spec(tm, D_MODEL)] * 3 + [_row_spec(tm, 2048), _row_spec(tm, 512), _row_spec(tm, 512),
                   _row_spec(tm, LANES), _row_spec(tm, LANES), pl.BlockSpec((1, D_MODEL), lambda i: (0, 0))],
        scratch_shapes=[pltpu.VMEM((SUBLANES, D_MODEL), F32)],
        compiler_params=_params(40), name="bwd_merge",
    )(dh1, o, A, B, gt, ya, lse, cc, wbs, wba, wo, g2)


def _bwd_sgu(zuv, dys, gs, bs, wsp, bT):
    T = zuv.shape[0]
    tc = SGU_TILE
    nc = tc // CHUNK
    nsteps = T // tc

    def body(z_ref, dy_ref, gs_ref, bs_ref, w_ref, bT_ref, dz_ref, dw_ref, dbT_ref, dgs_ref, dbs_ref,
             acc_w, acc_b, acc_gs, acc_bs, dvln_s):
        i = pl.program_id(0)

        @pl.when(i == 0)
        def _():
            acc_w[...] = jnp.zeros_like(acc_w)
            acc_b[...] = jnp.zeros_like(acc_b)
            acc_gs[...] = jnp.zeros_like(acc_gs)
            acc_bs[...] = jnp.zeros_like(acc_bs)

        z = z_ref[...].astype(F32)
        gsv = gs_ref[...]
        u, vhat, rs, vln = _sgu_forward_parts(z, gsv, bs_ref[...])
        vb = vln.astype(BF16)
        dy = dy_ref[...].astype(F32)
        low_w = (_iota((CHUNK, nc * LANES), 1) % LANES) < 64
        for p in range(4):
            we, wo, b2 = _sgu_pair_weights(w_ref, bT_ref[...], p)
            vcat = _chunks_on_lanes(vb, p, nc)
            s = _sgu_mix(we, wo, b2, vcat, nc)
            dyc = _chunks_on_lanes(dy, p, nc)
            ds = dyc * _chunks_on_lanes(u, p, nc)
            dsb = ds.astype(BF16)
            zero = jnp.zeros_like(dsb)
            dse = jnp.where(low_w, dsb, zero)
            dso = jnp.where(low_w, zero, dsb)
            acc_w[2 * p] += _dot_nt(dse, vcat)
            acc_w[2 * p + 1] += _dot_nt(dso, vcat)
            acc_b[p] += ds
            dvl = jnp.where(low_w, _dot_tn(we, dsb), _dot_tn(wo, dsb))
            for c in range(nc):
                rows, cols = slice(c * CHUNK, (c + 1) * CHUNK), slice(LANES * p, LANES * (p + 1))
                dvln_s[rows, cols] = dvl[:, c * LANES:(c + 1) * LANES]
                du = dy[rows, cols] * s[:, c * LANES:(c + 1) * LANES]
                dz_ref[rows, cols] = (du * _gelu_grad(z[rows, cols])).astype(BF16)
        dvln = dvln_s[...]
        acc_gs[...] += _fold8(dvln * vhat)
        acc_bs[...] += _fold8(dvln)
        al = dvln * gsv
        dvv = rs * (al - jnp.mean(al, axis=1, keepdims=True) - vhat * jnp.mean(al * vhat, axis=1, keepdims=True))
        dz_ref[:, SGU_WIDTH:] = (dvv * _gelu_grad(z[:, SGU_WIDTH:])).astype(BF16)

        @pl.when(i == nsteps - 1)
        def _():
            tril = _iota((CHUNK, CHUNK), 0) >= _iota((CHUNK, CHUNK), 1)
            lane = _iota((CHUNK, LANES), 1)
            low = lane < 64
            blk = jnp.zeros((CHUNK, LANES), F32)
            for g in range(8):
                dw_ref[g] = jnp.where(tril, acc_w[g], 0.0)
            for p in range(4):
                t = acc_b[p]
                tot = t[:, 0:LANES]
                for c in range(1, nc):
                    tot = tot + t[:, c * LANES:(c + 1) * LANES]
                blk = jnp.where(lane == 2 * p, jnp.sum(jnp.where(low, tot, 0.0), axis=1, keepdims=True), blk)
                blk = jnp.where(lane == 2 * p + 1, jnp.sum(jnp.where(low, 0.0, tot), axis=1, keepdims=True), blk)
            dbT_ref[...] = blk
            dgs_ref[...] = jnp.sum(acc_gs[...], axis=0, keepdims=True)
            dbs_ref[...] = jnp.sum(acc_bs[...], axis=0, keepdims=True)

    whole = lambda shape: pl.BlockSpec(shape, lambda i: (0,) * len(shape))
    return pl.pallas_call(
        body, grid=(nsteps,),
        out_shape=(jax.ShapeDtypeStruct((T, 1024), BF16), jax.ShapeDtypeStruct((8, CHUNK, CHUNK), F32),
                   jax.ShapeDtypeStruct((CHUNK, LANES), F32), jax.ShapeDtypeStruct((1, SGU_WIDTH), F32),
                   jax.ShapeDtypeStruct((1, SGU_WIDTH), F32)),
        in_specs=[_row_spec(tc, 1024), _row_spec(tc, SGU_WIDTH), _const_spec((1, SGU_WIDTH)),
                  _const_spec((1, SGU_WIDTH)), _const_spec((8, CHUNK, CHUNK)), _const_spec((CHUNK, 8))],
        out_specs=[_row_spec(tc, 1024), whole((8, CHUNK, CHUNK)), whole((CHUNK, LANES)),
                   whole((1, SGU_WIDTH)), whole((1, SGU_WIDTH))],
        scratch_shapes=[pltpu.VMEM((8, CHUNK, CHUNK), F32), pltpu.VMEM((4, CHUNK, nc * LANES), F32),
                        pltpu.VMEM((SUBLANES, SGU_WIDTH), F32), pltpu.VMEM((SUBLANES, SGU_WIDTH), F32),
                        pltpu.VMEM((tc, SGU_WIDTH), F32)],
        compiler_params=_params(48), name="bwd_sgu",
    )(zuv, dys, gs, bs, wsp, bT)


def _bwd_attn(qkv, dya, qab, dab, kaug):
    T = qkv.shape[0]
    tq = tk = ATTN_TILE
    nq = T // tq
    nk = T // tk

    def body(q_ref, do_ref, qa_ref, da_ref, k_ref, v_ref, ka_ref, dq_ref, dk_ref, dv_ref, dcx_ref, dq_acc):
        p = pl.program_id(0)
        j = pl.program_id(1)
        lane = _iota((tq, LANES), 1)
        low = lane < 64
        row = _iota((2 * tq, tk), 0) % tq
        col = _iota((2 * tq, tk), 1)
        first = 2 * AUG_LANES * p

        @pl.when(j == 0)
        def _():
            dq_acc[...] = jnp.zeros_like(dq_acc)

        @pl.when((j == 0) & (p == 0))
        def _():
            dcx_ref[...] = jnp.zeros_like(dcx_ref)

        ka = ka_ref[...]
        kk = jnp.concatenate([k_ref[...], ka], axis=1)
        vv = jnp.concatenate([v_ref[...], ka], axis=1)

        def q_block(i, carry, masked):
            dk_a, dv_a = carry
            qsl = pl.ds(pl.multiple_of(i * tq, tq), tq)
            qs = _aug_stack(q_ref[qsl, :], qa_ref[qsl, :], p)
            dos = _aug_stack(do_ref[qsl, :], da_ref[qsl, :], p)
            s = _dot_nt(qs, kk)
            if masked:
                s = jnp.where(col <= row, s, NEG)
            pm = jnp.exp(s)
            ds = pm * _dot_nt(dos, vv)
            dsb = ds.astype(BF16)
            dv_a = dv_a + _dot_tn(pm.astype(BF16), dos[:, :LANES])
            dk_a = dk_a + _dot_tn(dsb, qs)
            dqx = _dot(dsb, kk)
            dq_acc[qsl, :] += jnp.where(low, dqx[:tq, :LANES], dqx[tq:, :LANES])
            dcx_ref[qsl, :] += (jnp.where(lane == first, dqx[:tq, LANES:], 0.0)
                                + jnp.where(lane == first + AUG_LANES, dqx[tq:, LANES:], 0.0))
            return dk_a, dv_a

        init = (jnp.zeros((tk, 2 * LANES), F32), jnp.zeros((tk, LANES), F32))
        carry = q_block(j, init, True)
        dk_a, dv_a = lax.fori_loop(j + 1, nq, lambda i, c: q_block(i, c, False), carry)
        dk_ref[...] = dk_a[:, :LANES].astype(BF16)
        dv_ref[...] = dv_a.astype(BF16)
        ksl = pl.ds(pl.multiple_of(j * tk, tk), tk)
        lk = _iota((tk, LANES), 1)
        dcx_ref[ksl, :] += jnp.where((lk == first + 3) | (lk == first + AUG_LANES + 3), dk_a[:, LANES:], 0.0)

        @pl.when(j == nk - 1)
        def _():
            dq_ref[...] = (dq_acc[...] * 0.125).astype(BF16)

    sh = jax.ShapeDtypeStruct((T, ATTN_WIDTH), BF16)
    full = lambda cb: pl.BlockSpec((T, LANES), lambda p, j: (0, cb + p))
    blk = lambda cb: pl.BlockSpec((tk, LANES), lambda p, j: (j, cb + p))
    return pl.pallas_call(
        body, grid=(4, nk),
        out_shape=(sh, sh, sh, jax.ShapeDtypeStruct((T, LANES), F32)),
        in_specs=[full(0), full(0), _const_spec((T, LANES)), _const_spec((T, LANES)), blk(4), blk(8),
                  pl.BlockSpec((tk, LANES), lambda p, j: (j, 0))],
        out_specs=[full(0), blk(0), blk(0), pl.BlockSpec((T, LANES), lambda p, j: (0, 0))],
        scratch_shapes=[pltpu.VMEM((T, LANES), F32)],
        compiler_params=_params(56, 2), name="bwd_attn",
    )(qkv, dya, qab, dab, qkv, qkv, kaug)


def _bwd_cum(dcx, fl, bfp):
    T = fl.shape[0]
    tb = CUM_TILE

    def body(dcx_ref, fl_ref, b_ref, dfl_ref, dbf_ref):
        triu = (_iota((tb, tb), 0) <= _iota((tb, tb), 1)).astype(F32)
        r, c = _iota((LANES, LANES), 0), _iota((LANES, LANES), 1)
        sel = (((r == AUG_LANES * c) & (c < N_HEADS)).astype(F32)
               - ((r == AUG_LANES * c + 3) & (c < N_HEADS)).astype(F32))
        carry = jnp.zeros((1, LANES), F32)
        dbf = jnp.zeros((1, LANES), F32)
        for i in reversed(range(T // tb)):
            colblk = jnp.dot(dcx_ref[i * tb:(i + 1) * tb, :], sel, precision=HIGHEST, preferred_element_type=F32)
            rc = jnp.dot(triu, colblk, precision=HIGHEST, preferred_element_type=F32) + carry
            carry = rc[0:1, :]
            sig = jax.nn.sigmoid(fl_ref[i * tb:(i + 1) * tb, :] + b_ref[...])
            dfl = rc * (1.0 - sig)
            dfl_ref[i * tb:(i + 1) * tb, :] = dfl.astype(BF16)
            dbf = dbf + jnp.sum(dfl, axis=0, keepdims=True)
        dbf_ref[...] = dbf

    return pl.pallas_call(
        body,
        out_shape=(jax.ShapeDtypeStruct((T, LANES), BF16), jax.ShapeDtypeStruct((1, LANES), F32)),
        compiler_params=pltpu.CompilerParams(vmem_limit_bytes=32 * MIB), name="bwd_cum",
    )(dcx, fl, bfp)


def _bwd_in(dz, dq, dk, dv, dfl, dgl, dh1, x2, g1, wz, wf, wg):
    T = x2.shape[0]
    tm = TOKEN_TILE
    nsteps = T // tm

    def body(dz_ref, dq_ref, dk_ref, dv_ref, dfl_ref, dgl_ref, dh1_ref, x_ref, g_ref, wz_ref, wf_ref, wg_ref,
             dx_ref, dg1_ref, acc_g):
        i = pl.program_id(0)

        @pl.when(i == 0)
        def _():
            acc_g[...] = jnp.zeros_like(acc_g)

        dxn = _dot_nt(dz_ref[...], wz_ref[:, 0:1024])
        dxn = dxn + _dot_nt(dq_ref[...], wz_ref[:, 1024:1536])
        dxn = dxn + _dot_nt(dk_ref[...], wz_ref[:, 1536:2048])
        dxn = dxn + _dot_nt(dv_ref[...], wz_ref[:, 2048:2560])
        dxn = dxn + _dot_nt(dfl_ref[...], wf_ref[...])
        dxn = dxn + _dot_nt(dgl_ref[...], wg_ref[...])
        r1, xh = _rms_stats(x_ref[...])
        acc_g[...] += _fold8(dxn * xh)
        dx_ref[...] = dh1_ref[...] + _rms_bwd(dxn, xh, r1, g_ref[...])

        @pl.when(i == nsteps - 1)
        def _():
            dg1_ref[...] = jnp.sum(acc_g[...], axis=0, keepdims=True)

    return pl.pallas_call(
        body, grid=(nsteps,),
        out_shape=(jax.ShapeDtypeStruct((T, D_MODEL), F32), jax.ShapeDtypeStruct((1, D_MODEL), F32)),
        in_specs=[_row_spec(tm, 1024), _row_spec(tm, 512), _row_spec(tm, 512), _row_spec(tm, 512),
                  _row_spec(tm, LANES), _row_spec(tm, 2048), _row_spec(tm, D_MODEL), _row_spec(tm, D_MODEL),
                  _const_spec((1, D_MODEL)), _const_spec((D_MODEL, ZQKV_WIDTH)), _const_spec((D_MODEL, LANES)),
                  _const_spec((D_MODEL, 2048))],
        out_specs=[_row_spec(tm, D_MODEL), pl.BlockSpec((1, D_MODEL), lambda i: (0, 0))],
        scratch_shapes=[pltpu.VMEM((SUBLANES, D_MODEL), F32)],
        compiler_params=_params(48), name="bwd_in",
    )(dz, dq, dk, dv, dfl, dgl, dh1, x2, g1, wz, wf, wg)


def _pad_rows(v, rows):
    return jnp.pad(v, ((0, rows - v.shape[0]), (0, 0)))


def _pack_small(g_mix_pre, b_forget, g_sgu, b_sgu, w_spatial, b_spatial, g_mix_post, g_ffn_pre, g_ffn_post):
    vec = lambda v: _pad_rows(v.reshape(-1, LANES), SUBLANES)
    return jnp.concatenate([
        w_spatial.reshape(-1, LANES), vec(g_mix_pre), _pad_rows(jnp.pad(b_forget, ((0, 0), (0, LANES - N_HEADS))), SUBLANES),
        vec(g_sgu), vec(b_sgu), vec(b_spatial), vec(g_mix_post), vec(g_ffn_pre), vec(g_ffn_post)], axis=0)


def _unpack_small(p):
    nw = N_HEADS * CHUNK * CHUNK // LANES
    blk = lambda k: p[nw + SUBLANES * k: nw + SUBLANES * (k + 1)]
    return dict(
        w_spatial=p[:nw].reshape(1, N_HEADS, CHUNK, CHUNK),
        g_mix_pre=blk(0).reshape(1, D_MODEL), b_forget=blk(1)[0:1, :N_HEADS],
        g_sgu=blk(2)[:4].reshape(1, SGU_WIDTH), b_sgu=blk(3)[:4].reshape(1, SGU_WIDTH),
        b_spatial=blk(4).reshape(1, N_HEADS, CHUNK), g_mix_post=blk(5).reshape(1, D_MODEL),
        g_ffn_pre=blk(6).reshape(1, D_MODEL), g_ffn_post=blk(7).reshape(1, D_MODEL))


def kernel(x, g_mix_pre, w_in, b_forget, g_sgu, b_sgu, w_spatial, b_spatial, w_branch_sgu, w_branch_attn, w_out, g_mix_post, g_ffn_pre, w_up, w_down, g_ffn_post, loss_target, m_g_mix_pre, m_w_in, m_b_forget, m_g_sgu, m_b_sgu, m_w_spatial, m_b_spatial, m_w_branch_sgu, m_w_branch_attn, m_w_out, m_g_mix_post, m_g_ffn_pre, m_w_up, m_w_down, m_g_ffn_post, v_g_mix_pre, v_w_in, v_b_forget, v_g_sgu, v_b_sgu, v_w_spatial, v_b_spatial, v_w_branch_sgu, v_w_branch_attn, v_w_out, v_g_mix_post, v_g_ffn_pre, v_w_up, v_w_down, v_g_ffn_post):
    T = x.shape[1]
    x2 =x.reshape(T, D_MODEL)
    tgt = loss_target.reshape(T, D_MODEL)
    pad_in = lambda w: jnp.pad(w, ((0, 0), (0, IN_SHARD_PAD - IN_SHARD)))

    wg_in, wbs, wba, wo, wup, wdn = _gather_weights(
        pad_in(w_in[0]), w_branch_sgu[0], w_branch_attn[0], w_out[0], w_up[0], w_down[0])
    w_in_full = jnp.concatenate([wg_in[j, :, :IN_SHARD] for j in range(N_DEV)], axis=1)
    wz = w_in_full[:, :ZQKV_WIDTH]
    wf = jnp.pad(w_in_full[:, ZQKV_WIDTH:GATE_OFFSET], ((0, 0), (0, LANES - N_HEADS)))
    wgt = w_in_full[:, GATE_OFFSET:]
    bfp = jnp.pad(b_forget, ((0, 0), (0, LANES - N_HEADS)))
    wsp = w_spatial[0]
    bT = b_spatial[0].T

    xn, zuv, qkv, fl, gt = _fwd_in(x2, g_mix_pre, wz, wf, wgt)
    cc, qaug, kaug = _fwd_cum(fl, bfp)
    ys = _fwd_sgu(zuv, g_sgu, b_sgu, wsp, bT)
    ya, lse = _fwd_attn(qkv, qaug, kaug)
    A, B, mg, o, h1 = _fwd_merge(ys, ya, gt, x2, wbs, wba, wo, g_mix_post)
    xn2, a, ddn, dy, loss_part, dg4 = _fwd_ffn_loss(h1, tgt, wup, wdn, g_ffn_pre, g_ffn_post)
    loss = lax.psum(jnp.sum(loss_part), ("x", "y", "c"))

    da, dh1, dg3 = _bwd_ffn(ddn, a, dy, h1, wup, wdn, g_ffn_pre)
    dw_up = _wgrad(xn2, da, "wgrad_up", tn=2048)
    dw_down = _wgrad(a, ddn, "wgrad_down", relu2=True)
    do, dA, dB, dgl, dys, dya, qab, dab, dg2 = _bwd_merge(dh1, o, A, B, gt, ya, lse, cc, wbs, wba, wo, g_mix_post)
    dw_out = _wgrad(mg, do, "wgrad_out")
    dw_bs = _wgrad(ys, dA, "wgrad_branch_sgu")
    dw_ba = _wgrad(ya, dB, "wgrad_branch_attn")
    dzuv, dwsp, dbT, dgs, dbs = _bwd_sgu(zuv, dys, g_sgu, b_sgu, wsp, bT)
    dq, dk, dv, dcx = _bwd_attn(qkv, dya, qab, dab, kaug)
    dfl, dbf = _bwd_cum(dcx, fl, bfp)
    dx, dg1 = _bwd_in(dzuv, dq, dk, dv, dfl, dgl, dh1, x2, g_mix_pre, wz, wf, wgt)
    dw_z = _wgrad(xn, dzuv, "wgrad_in_z")
    dw_q = _wgrad(xn, dq, "wgrad_in_q")
    dw_k = _wgrad(xn, dk, "wgrad_in_k")
    dw_v = _wgrad(xn, dv, "wgrad_in_v")
    dw_f = _wgrad(xn, dfl, "wgrad_in_f")
    dw_g = _wgrad(xn, dgl, "wgrad_in_gate")
    dw_in = jnp.concatenate([dw_z, dw_q, dw_k, dw_v, dw_f[:, :N_HEADS], dw_g], axis=1)

    blocks_in = jnp.pad(dw_in.reshape(D_MODEL, N_DEV, IN_SHARD).transpose(1, 0, 2),
                        ((0, 0), (0, 0), (0, IN_SHARD_PAD - IN_SHARD)))
    col_blocks = lambda g, w: g.reshape(g.shape[0], N_DEV, w).transpose(1, 0, 2)
    row_blocks = lambda g, r: g.reshape(N_DEV, r, g.shape[1])
    g_in, g_bs, g_ba, g_out, g_up, g_down = _reduce_scatter_grads([
        blocks_in, col_blocks(dw_bs, 128), col_blocks(dw_ba, 128), row_blocks(dw_out, 128),
        col_blocks(dw_up, 512), row_blocks(dw_down, 512)])
    g_in = g_in[:, :IN_SHARD]

    small_g = _pack_small(dg1, dbf[:, :N_HEADS], dgs, dbs, dwsp[None], dbT[:, :N_HEADS].T[None], dg2, dg3, dg4)
    small_w = _pack_small(g_mix_pre, b_forget, g_sgu, b_sgu, w_spatial, b_spatial, g_mix_post, g_ffn_pre, g_ffn_post)
    small_m = _pack_small(m_g_mix_pre, m_b_forget, m_g_sgu, m_b_sgu, m_w_spatial, m_b_spatial, m_g_mix_post,
                          m_g_ffn_pre, m_g_ffn_post)
    small_v = _pack_small(v_g_mix_pre, v_b_forget, v_g_sgu, v_b_sgu, v_w_spatial, v_b_spatial, v_g_mix_post,
                          v_g_ffn_pre, v_g_ffn_post)
    sg, sd, sm, sv = (_unpack_small(t) for t in _allreduce_small_adamw(small_g, small_w, small_m, small_v))

    big = {}
    for name, w, g, m, v in (
            ("w_in", w_in, g_in, m_w_in, v_w_in), ("w_branch_sgu", w_branch_sgu, g_bs, m_w_branch_sgu, v_w_branch_sgu),
            ("w_branch_attn", w_branch_attn, g_ba, m_w_branch_attn, v_w_branch_attn),
            ("w_out", w_out, g_out, m_w_out, v_w_out), ("w_up", w_up, g_up, m_w_up, v_w_up),
            ("w_down", w_down, g_down, m_w_down, v_w_down)):
        d_, m_, v_ = _adamw(w[0], g, m[0], v[0], "adamw_" + name)
        big[name] = (g[None], d_[None], m_[None], v_[None])

    order = ["g_mix_pre", "w_in", "b_forget", "g_sgu", "b_sgu", "w_spatial", "b_spatial", "w_branch_sgu",
             "w_branch_attn", "w_out", "g_mix_post", "g_ffn_pre", "w_up", "w_down", "g_ffn_post"]
    outs = [loss, dx.reshape(1, T, D_MODEL)]
    for kind, small in enumerate((sg, sd, sm, sv)):
        outs += [big[nm][kind] if nm in big else small[nm] for nm in order]
    return tuple(outs)
```

```python
import jax
import jax.numpy as jnp
from jax import lax
from jax.experimental import pallas as pl
from jax.experimental.pallas import tpu as pltpu

F32 = jnp.float32
BF16 = jnp.bfloat16
HIGHEST = lax.Precision.HIGHEST
MESH = pl.DeviceIdType.MESH

D_MODEL = 1024
SGU_WIDTH = 512
ATTN_WIDTH = 512
N_HEADS = 8
CHUNK = 128
D_FF = 4096
IN_WIDTH = 4616
N_DEV = 8
IN_SHARD = IN_WIDTH // N_DEV
IN_SHARD_PAD = 640
ZQKV_WIDTH = 2 * SGU_WIDTH + 3 * ATTN_WIDTH
GATE_OFFSET = ZQKV_WIDTH + N_HEADS
EPS = 1e-6
LANES = 128
SUBLANES = 8
VMEM_BYTES = 64 * 1024 * 1024
MIB = 1024 * 1024

ADAM_LR = 0.001
ADAM_B1 = 0.9
ADAM_B2 = 0.999
ADAM_EPS = 1e-08
ADAM_WD = 0.01
ADAM_STEP = 10

TOKEN_TILE = 256
ATTN_TILE = 512
CUM_TILE = 256
SGU_TILE = 512
WGRAD_TILE = 512
NEG = -1e30

NT_DIMS = (((1,), (1,)), ((), ()))
TN_DIMS = (((0,), (0,)), ((), ()))


def _params(vmem_mb, n_grid=1):
    return pltpu.CompilerParams(
        dimension_semantics=("arbitrary",) * n_grid,
        vmem_limit_bytes=min(vmem_mb * MIB, VMEM_BYTES - 6 * MIB),
    )


def _dot(a, b):
    return jnp.dot(a, b, preferred_element_type=F32)


def _dot_nt(a, b):
    return lax.dot_general(a, b, NT_DIMS, preferred_element_type=F32)


def _dot_tn(a, b):
    return lax.dot_general(a, b, TN_DIMS, preferred_element_type=F32)


def _const_spec(shape):
    nd = len(shape)
    return pl.BlockSpec(shape, lambda *_: (0,) * nd, pipeline_mode=pl.Buffered(1))


def _row_spec(tm, n, col=0):
    return pl.BlockSpec((tm, n), lambda i: (i, col))


def _fold8(v):
    return v.reshape(v.shape[0] // SUBLANES, SUBLANES, v.shape[1]).sum(axis=0)


def _pick(v, lane_iota, k):
    return jnp.sum(jnp.where(lane_iota == k, v, 0.0), axis=1, keepdims=True)


def _iota(shape, dim):
    return lax.broadcasted_iota(jnp.int32, shape, dim)


def _gelu(x):
    c = 0.7978845608028654
    return 0.5 * x * (1.0 + jnp.tanh(c * (x + 0.044715 * x * x * x)))


def _gelu_grad(x):
    c = 0.7978845608028654
    t = jnp.tanh(c * (x + 0.044715 * x * x * x))
    return 0.5 * (1.0 + t) + 0.5 * x * (1.0 - t * t) * (c * (1.0 + 3.0 * 0.044715 * x * x))


def _rms_stats(v):
    r = lax.rsqrt(jnp.mean(v * v, axis=1, keepdims=True) + EPS)
    return r, v * r


def _rms_bwd(dout, vhat, r, g):
    a = dout * g
    return r * (a - vhat * jnp.mean(a * vhat, axis=1, keepdims=True))


def _mesh_pos():
    return lax.axis_index("x"), lax.axis_index("y"), lax.axis_index("c")


def _dev_index(px, py, pc):
    return 4 * px + 2 * py + pc


def _other_chips(x, y):
    return [(1 - x, y), (x, 1 - y), (1 - x, 1 - y)]


class _WeightGather:
    def __init__(self, shard_shapes, kinds, stage_shapes=None):
        self.shard_shapes = list(shard_shapes)
        self.kinds = list(kinds)
        self.stage_shapes = list(stage_shapes or shard_shapes)
        self.n = len(self.kinds)

    def out_shapes(self):
        shapes = []
        for (rows, cols), kind in zip(self.stage_shapes, self.kinds):
            full = {"block": (N_DEV, rows, cols), "rows": (N_DEV * rows, cols), "cols": (rows, N_DEV * cols)}[kind]
            shapes.append(jax.ShapeDtypeStruct(full, BF16))
        return shapes

    def scratch_shapes(self):
        return ([pltpu.VMEM(s, BF16) for s in self.stage_shapes]
                + [pltpu.SemaphoreType.DMA((self.n, 7)), pltpu.SemaphoreType.DMA((self.n, 7)),
                   pltpu.SemaphoreType.DMA((self.n,))])

    def _view(self, a, ref, j):
        rows, cols = self.stage_shapes[a]
        if self.kinds[a] == "block":
            return ref.at[j]
        if self.kinds[a] == "rows":
            return ref.at[pl.ds(pl.multiple_of(j * rows, rows), rows), :]
        return ref.at[:, pl.ds(pl.multiple_of(j * cols, cols), cols)]

    def _copy(self, outs, scratch, a, k, block, to, from_stage=False):
        stage, (send_sems, recv_sems, _) = scratch[:self.n], scratch[self.n:]
        dst = self._view(a, outs[a], _dev_index(*block))
        return pltpu.make_async_remote_copy(
            src_ref=stage[a] if from_stage else dst, dst_ref=dst,
            send_sem=send_sems.at[a, k], recv_sem=recv_sems.at[a, k],
            device_id=to, device_id_type=MESH)

    def _local(self, outs, scratch, a, me):
        return pltpu.make_async_copy(scratch[a], self._view(a, outs[a], _dev_index(*me)), scratch[-1].at[a])

    def start(self, ins, outs, scratch):
        x, y, c = _mesh_pos()
        me, sibling = (x, y, c), (x, y, 1 - c)
        for a in range(self.n):
            rows, cols = self.shard_shapes[a]
            if self.stage_shapes[a] != self.shard_shapes[a]:
                scratch[a][...] = jnp.zeros(self.stage_shapes[a], BF16)
            scratch[a][0:rows, 0:cols] = ins[a][...].astype(BF16)
            self._local(outs, scratch, a, me).start()
        for a in range(self.n):
            self._copy(outs, scratch, a, 0, me, sibling, True).start()
            for j, chip in enumerate(_other_chips(x, y)):
                self._copy(outs, scratch, a, 1 + j, me, (*chip, c), True).start()

    def forward(self, outs, scratch):
        x, y, c = _mesh_pos()
        me, sibling = (x, y, c), (x, y, 1 - c)
        for a in range(self.n):
            for j, chip in enumerate(_other_chips(x, y)):
                self._copy(outs, scratch, a, 1 + j, (*chip, c), me).wait_recv()
                self._copy(outs, scratch, a, 4 + j, (*chip, c), sibling).start()

    def finish(self, outs, scratch):
        x, y, c = _mesh_pos()
        me, sibling = (x, y, c), (x, y, 1 - c)
        chips = _other_chips(x, y)
        for a in range(self.n):
            self._copy(outs, scratch, a, 0, sibling, me).wait_recv()
            for j, chip in enumerate(chips):
                self._copy(outs, scratch, a, 4 + j, (*chip, 1 - c), me).wait_recv()
        for a in range(self.n):
            self._copy(outs, scratch, a, 0, me, sibling, True).wait_send()
            for j, chip in enumerate(chips):
                self._copy(outs, scratch, a, 1 + j, me, (*chip, c), True).wait_send()
                self._copy(outs, scratch, a, 4 + j, (*chip, c), sibling).wait_send()
            self._local(outs, scratch, a, me).wait()


def _gather_w_in(w_in_local):
    g = _WeightGather([(D_MODEL, IN_SHARD)], ["block"], [(D_MODEL, IN_SHARD_PAD)])

    def body(w_ref, out_ref, *scratch):
        g.start([w_ref], [out_ref], scratch)
        g.forward([out_ref], scratch)
        g.finish([out_ref], scratch)

    return pl.pallas_call(
        body,
        out_shape=g.out_shapes()[0],
        in_specs=[pl.BlockSpec(memory_space=pltpu.VMEM)],
        out_specs=pl.BlockSpec(memory_space=pl.ANY),
        scratch_shapes=g.scratch_shapes(),
        compiler_params=pltpu.CompilerParams(vmem_limit_bytes=32 * MIB),
        name="gather_w_in",
    )(w_in_local)


def _reduce_scatter_grads(grads):
    red = _GradReduce([tuple(g.shape[1:]) for g in grads])
    n = red.n

    def body(*refs):
        ins, hbm, scratch = refs[:n], refs[n:5 * n], refs[5 * n:]
        red.start(ins, hbm, scratch)
        red.partials(ins, hbm, scratch)
        red.finish(ins, hbm, scratch)

    res = pl.pallas_call(
        body,
        out_shape=red.out_shapes(),
        in_specs=[pl.BlockSpec(memory_space=pl.ANY)] * n,
        out_specs=[pl.BlockSpec(memory_space=pl.ANY)] * (4 * n),
        scratch_shapes=red.scratch_shapes(),
        compiler_params=pltpu.CompilerParams(vmem_limit_bytes=32 * MIB),
        name="reduce_scatter_grads",
    )(*grads)
    return res[:n]


REDUCE_CHUNK_ROWS = 256


class _GradReduce:
    def __init__(self, shapes):
        self.shapes = [tuple(s) for s in shapes]
        self.n = len(self.shapes)
        self.cols = max(s[1] for s in self.shapes)

    def out_shapes(self):
        sd = jax.ShapeDtypeStruct
        return ([sd(s, F32) for s in self.shapes] + [sd((4,) + s, F32) for s in self.shapes]
                + [sd((3,) + s, BF16) for s in self.shapes] + [sd((3,) + s, BF16) for s in self.shapes])

    def scratch_shapes(self):
        n = self.n
        return [pltpu.VMEM((REDUCE_CHUNK_ROWS, self.cols), F32), pltpu.VMEM((REDUCE_CHUNK_ROWS, self.cols), F32),
                pltpu.VMEM((REDUCE_CHUNK_ROWS, self.cols), BF16),
                pltpu.SemaphoreType.DMA((n, 4)), pltpu.SemaphoreType.DMA((n, 4)),
                pltpu.SemaphoreType.DMA((n, 3)), pltpu.SemaphoreType.DMA((n, 3)),
                pltpu.SemaphoreType.DMA((3,))]

    def _parts(self, hbm):
        n = self.n
        return hbm[:n], hbm[n:2 * n], hbm[2 * n:3 * n], hbm[3 * n:4 * n]

    def _chips(self):
        x, y, c = _mesh_pos()
        return _other_chips(x, y) + [(x, y)], c, (x, y, 1 - c)

    def _s1(self, ins, hbm, scratch, a, k):
        _, land1, _, _ = self._parts(hbm)
        chips, c, sibling = self._chips()
        return pltpu.make_async_remote_copy(
            src_ref=ins[a].at[_dev_index(*chips[k], 1 - c)], dst_ref=land1[a].at[k],
            send_sem=scratch[3].at[a, k], recv_sem=scratch[4].at[a, k],
            device_id=sibling, device_id_type=MESH)

    def _s2(self, hbm, scratch, a, k):
        _, _, part, land2 = self._parts(hbm)
        chips, c, _ = self._chips()
        return pltpu.make_async_remote_copy(
            src_ref=part[a].at[k], dst_ref=land2[a].at[k],
            send_sem=scratch[5].at[a, k], recv_sem=scratch[6].at[a, k],
            device_id=(*chips[k], c), device_id_type=MESH)

    def _chunks(self, a):
        rows, cols = self.shapes[a]
        step = min(rows, REDUCE_CHUNK_ROWS)
        return [(r, step, cols) for r in range(0, rows, step)]

    def _sum_two(self, src_a, src_b, scratch, r0, nr, cols):
        va = scratch[0].at[pl.ds(0, nr), pl.ds(0, cols)]
        vb = scratch[1].at[pl.ds(0, nr), pl.ds(0, cols)]
        la = pltpu.make_async_copy(src_a.at[pl.ds(r0, nr), :], va, scratch[7].at[0])
        lb = pltpu.make_async_copy(src_b.at[pl.ds(r0, nr), :], vb, scratch[7].at[1])
        la.start()
        lb.start()
        la.wait()
        lb.wait()
        return va[...] + vb[...]

    def start(self, ins, hbm, scratch):
        for a in range(self.n):
            for k in range(4):
                self._s1(ins, hbm, scratch, a, k).start()

    def partials(self, ins, hbm, scratch):
        _, land1, part, _ = self._parts(hbm)
        chips, c, _ = self._chips()
        for a in range(self.n):
            for k in range(3):
                self._s1(ins, hbm, scratch, a, k).wait_recv()
                for r0, nr, cols in self._chunks(a):
                    vs = scratch[2].at[pl.ds(0, nr), pl.ds(0, cols)]
                    vs[...] = self._sum_two(ins[a].at[_dev_index(*chips[k], c)], land1[a].at[k],
                                            scratch, r0, nr, cols).astype(BF16)
                    st = pltpu.make_async_copy(vs, part[a].at[k, pl.ds(r0, nr), :], scratch[7].at[2])
                    st.start()
                    st.wait()
                self._s2(hbm, scratch, a, k).start()

    def finish(self, ins, hbm, scratch):
        outs, land1, _, land2 = self._parts(hbm)
        chips, c, _ = self._chips()
        for a in range(self.n):
            self._s1(ins, hbm, scratch, a, 3).wait_recv()
            for k in range(3):
                self._s2(hbm, scratch, a, k).wait_recv()
            for r0, nr, cols in self._chunks(a):
                acc = self._sum_two(ins[a].at[_dev_index(*chips[3], c)], land1[a].at[3], scratch, r0, nr, cols)
                vs = scratch[2].at[pl.ds(0, nr), pl.ds(0, cols)]
                for k in range(3):
                    ld = pltpu.make_async_copy(land2[a].at[k, pl.ds(r0, nr), :], vs, scratch[7].at[2])
                    ld.start()
                    ld.wait()
                    acc = acc + vs[...].astype(F32)
                va = scratch[0].at[pl.ds(0, nr), pl.ds(0, cols)]
                va[...] = acc
                st = pltpu.make_async_copy(va, outs[a].at[pl.ds(r0, nr), :], scratch[7].at[0])
                st.start()
                st.wait()
        for a in range(self.n):
            for k in range(4):
                self._s1(ins, hbm, scratch, a, k).wait_send()
            for k in range(3):
                self._s2(hbm, scratch, a, k).wait_send()


def _adamw_math(w, g, m, v):
    m = ADAM_B1 * m + (1.0 - ADAM_B1) * g
    v = ADAM_B2 * v + (1.0 - ADAM_B2) * (g * g)
    m_hat = m / (1.0 - ADAM_B1 ** ADAM_STEP)
    v_hat = v / (1.0 - ADAM_B2 ** ADAM_STEP)
    delta = -ADAM_LR * (m_hat / (jnp.sqrt(v_hat) + ADAM_EPS) + ADAM_WD * w)
    return delta, m, v


def _allreduce_small_adamw(g, w, m, v):
    shape = g.shape

    def body(g_ref, w_ref, m_ref, v_ref, gs_ref, d_ref, nm_ref, nv_ref, sib, psum, land, send_sems, recv_sems):
        x, y, c = _mesh_pos()
        sibling = (x, y, 1 - c)
        chips = _other_chips(x, y)
        first = pltpu.make_async_remote_copy(
            src_ref=g_ref, dst_ref=sib, send_sem=send_sems.at[0], recv_sem=recv_sems.at[0],
            device_id=sibling, device_id_type=MESH)
        first.start()
        first.wait_recv()
        psum[...] = g_ref[...] + sib[...]
        second = []
        for k, (cx, cy) in enumerate(chips):
            cp = pltpu.make_async_remote_copy(
                src_ref=psum, dst_ref=land.at[k], send_sem=send_sems.at[1 + k], recv_sem=recv_sems.at[1 + k],
                device_id=(cx, cy, c), device_id_type=MESH)
            cp.start()
            second.append(cp)
        for cp in second:
            cp.wait_recv()
        total = (psum[...] + land[0]) + (land[1] + land[2])
        gs_ref[...] = total
        delta, nm, nv = _adamw_math(w_ref[...], total, m_ref[...], v_ref[...])
        d_ref[...] = delta
        nm_ref[...] = nm
        nv_ref[...] = nv
        first.wait_send()
        for cp in second:
            cp.wait_send()

    sd = jax.ShapeDtypeStruct(shape, F32)
    return pl.pallas_call(
        body,
        out_shape=(sd, sd, sd, sd),
        in_specs=[pl.BlockSpec(memory_space=pltpu.VMEM)] * 4,
        out_specs=[pl.BlockSpec(memory_space=pltpu.VMEM)] * 4,
        scratch_shapes=[pltpu.VMEM(shape, F32), pltpu.VMEM(shape, F32), pltpu.VMEM((3,) + shape, F32),
                        pltpu.SemaphoreType.DMA((4,)), pltpu.SemaphoreType.DMA((4,))],
        compiler_params=pltpu.CompilerParams(vmem_limit_bytes=32 * MIB),
        name="allreduce_small_adamw",
    )(g, w, m, v)


def _adamw(w, g, m, v, name):
    rows, cols = w.shape
    tm = 256 if rows % 256 == 0 else rows

    def body(w_ref, g_ref, m_ref, v_ref, d_ref, nm_ref, nv_ref):
        delta, nm, nv = _adamw_math(w_ref[...], g_ref[...], m_ref[...], v_ref[...])
        d_ref[...] = delta
        nm_ref[...] = nm
        nv_ref[...] = nv

    sd = jax.ShapeDtypeStruct((rows, cols), F32)
    spec = _row_spec(tm, cols)
    return pl.pallas_call(
        body, grid=(rows // tm,), out_shape=(sd, sd, sd), in_specs=[spec] * 4, out_specs=[spec] * 3,
        compiler_params=_params(32), name=name,
    )(w, g, m, v)


def _fwd_in(x2, g1, wz, wf, wg):
    T = x2.shape[0]
    tm = TOKEN_TILE

    def body(x_ref, g_ref, wz_ref, wf_ref, wg_ref, xn_ref, zuv_ref, qkv_ref, fl_ref, gt_ref):
        x = x_ref[...]
        r, xh = _rms_stats(x)
        xn = (xh * g_ref[...]).astype(BF16)
        xn_ref[...] = xn
        zuv_ref[...] = _dot(xn, wz_ref[:, 0:1024]).astype(BF16)
        qkv_ref[:, 0:512] = (_dot(xn, wz_ref[:, 1024:1536]) * 0.125).astype(BF16)
        qkv_ref[:, 512:1536] = _dot(xn, wz_ref[:, 1536:2560]).astype(BF16)
        fl_ref[...] = _dot(xn, wf_ref[...])
        gt_ref[...] = jax.nn.sigmoid(_dot(xn, wg_ref[...])).astype(BF16)

    return pl.pallas_call(
        body, grid=(T // tm,),
        out_shape=(jax.ShapeDtypeStruct((T, D_MODEL), BF16), jax.ShapeDtypeStruct((T, 1024), BF16),
                   jax.ShapeDtypeStruct((T, 1536), BF16), jax.ShapeDtypeStruct((T, LANES), F32),
                   jax.ShapeDtypeStruct((T, 2048), BF16)),
        in_specs=[_row_spec(tm, D_MODEL), _const_spec((1, D_MODEL)), _const_spec((D_MODEL, ZQKV_WIDTH)),
                  _const_spec((D_MODEL, LANES)), _const_spec((D_MODEL, 2048))],
        out_specs=[_row_spec(tm, D_MODEL), _row_spec(tm, 1024), _row_spec(tm, 1536), _row_spec(tm, LANES),
                   _row_spec(tm, 2048)],
        compiler_params=_params(48), name="fwd_in",
    )(x2, g1, wz, wf, wg)


def _log_sigmoid(f):
    return jnp.minimum(f, 0.0) - jnp.log1p(jnp.exp(-jnp.abs(f)))


AUG_LANES = 6


def _split3(v):
    hi = v.astype(BF16)
    r1 = v - hi.astype(F32)
    mid = r1.astype(BF16)
    lo = (r1 - mid.astype(F32)).astype(BF16)
    return hi, mid, lo


def _spread(parts, k0):
    r, c = _iota((LANES, LANES), 0), _iota((LANES, LANES), 1)
    out = None
    for i, part in enumerate(parts):
        e = ((c == AUG_LANES * r + (k0 + i)) & (r < N_HEADS)).astype(BF16)
        term = _dot(part, e)
        out = term if out is None else out + term
    return out


def _aug_query(v):
    ones = (_iota(v.shape, 1) < N_HEADS).astype(BF16)
    return (_spread(_split3(v), 0) + _spread((ones, ones, ones), 3)).astype(BF16)


def _aug_key(v):
    ones = (_iota(v.shape, 1) < N_HEADS).astype(BF16)
    return (_spread((ones, ones, ones), 0) - _spread(_split3(v), 3)).astype(BF16)


def _aug_stack(t2, aug, p):
    lane = _iota(t2.shape, 1)
    low = lane < 64
    zero = jnp.zeros_like(t2)
    first = 2 * AUG_LANES * p
    a_e = jnp.where((lane >= first) & (lane < first + AUG_LANES), aug, zero)
    a_o = jnp.where((lane >= first + AUG_LANES) & (lane < first + 2 * AUG_LANES), aug, zero)
    top = jnp.concatenate([jnp.where(low, t2, zero), a_e], axis=1)
    bot = jnp.concatenate([jnp.where(low, zero, t2), a_o], axis=1)
    return jnp.concatenate([top, bot], axis=0)


def _fwd_cum(fl, bfp):
    T = fl.shape[0]
    tb = CUM_TILE

    def body(fl_ref, b_ref, cc_ref, qa_ref, ka_ref):
        tri = (_iota((tb, tb), 0) >= _iota((tb, tb), 1)).astype(F32)
        carry = jnp.zeros((1, LANES), F32)
        for i in range(T // tb):
            rows = slice(i * tb, (i + 1) * tb)
            lf = _log_sigmoid(fl_ref[rows, :] + b_ref[...])
            cs = jnp.dot(tri, lf, precision=HIGHEST, preferred_element_type=F32) + carry
            cc_ref[rows, :] = cs
            carry = cs[tb - 1:tb, :]
            qa_ref[rows, :] = _aug_query(cs)
            ka_ref[rows, :] = _aug_key(cs)

    return pl.pallas_call(
        body,
        out_shape=(jax.ShapeDtypeStruct((T, LANES), F32), jax.ShapeDtypeStruct((T, LANES), BF16),
                   jax.ShapeDtypeStruct((T, LANES), BF16)),
        compiler_params=pltpu.CompilerParams(vmem_limit_bytes=32 * MIB), name="fwd_cum",
    )(fl, bfp)


def _sgu_forward_parts(z, gs, bs):
    u = _gelu(z[:, :SGU_WIDTH])
    vv = _gelu(z[:, SGU_WIDTH:])
    vc = vv - jnp.mean(vv, axis=1, keepdims=True)
    rs = lax.rsqrt(jnp.mean(vc * vc, axis=1, keepdims=True) + EPS)
    vhat = vc * rs
    return u, vhat, rs, vhat * gs + bs


def _sgu_pair_weights(w_ref, bT, p):
    tril = _iota((CHUNK, CHUNK), 0) >= _iota((CHUNK, CHUNK), 1)
    we = jnp.where(tril, w_ref[2 * p], 0.0).astype(BF16)
    wo = jnp.where(tril, w_ref[2 * p + 1], 0.0).astype(BF16)
    lane8 = _iota(bT.shape, 1)
    low = _iota((CHUNK, LANES), 1) < 64
    b2 = jnp.where(low, _pick(bT, lane8, 2 * p), _pick(bT, lane8, 2 * p + 1))
    return we, wo, b2


def _chunks_on_lanes(v, p, nc):
    return jnp.concatenate([v[c * CHUNK:(c + 1) * CHUNK, LANES * p:LANES * (p + 1)] for c in range(nc)], axis=1)


def _sgu_mix(we, wo, b2, vcat, nc):
    low = (_iota((CHUNK, nc * LANES), 1) % LANES) < 64
    return jnp.where(low, _dot(we, vcat), _dot(wo, vcat)) + jnp.concatenate([b2] * nc, axis=1)


def _fwd_sgu(zuv, gs, bs, wsp, bT):
    T = zuv.shape[0]
    tc = SGU_TILE
    nc = tc // CHUNK

    def body(z_ref, gs_ref, bs_ref, w_ref, bT_ref, y_ref):
        u, _, _, vln = _sgu_forward_parts(z_ref[...].astype(F32), gs_ref[...], bs_ref[...])
        vb = vln.astype(BF16)
        for p in range(4):
            we, wo, b2 = _sgu_pair_weights(w_ref, bT_ref[...], p)
            s = _sgu_mix(we, wo, b2, _chunks_on_lanes(vb, p, nc), nc)
            for c in range(nc):
                rows, cols = slice(c * CHUNK, (c + 1) * CHUNK), slice(LANES * p, LANES * (p + 1))
                y_ref[rows, cols] = (u[rows, cols] * s[:, c * LANES:(c + 1) * LANES]).astype(BF16)

    return pl.pallas_call(
        body, grid=(T // tc,), out_shape=jax.ShapeDtypeStruct((T, SGU_WIDTH), BF16),
        in_specs=[_row_spec(tc, 1024), _const_spec((1, SGU_WIDTH)), _const_spec((1, SGU_WIDTH)),
                  _const_spec((8, CHUNK, CHUNK)), _const_spec((CHUNK, 8))],
        out_specs=_row_spec(tc, SGU_WIDTH),
        compiler_params=_params(40), name="fwd_sgu",
    )(zuv, gs, bs, wsp, bT)


def _fwd_attn(qkv, qaug, kaug, w_shards):
    T = qkv.shape[0]
    tq = tk = ATTN_TILE
    nq = T // tq
    gather = _WeightGather([w.shape for w in w_shards], ["cols", "cols", "rows", "cols", "rows"])
    nw = gather.n

    def body(q_ref, qa_ref, k_ref, v_ref, ka_ref, *rest):
        w_refs, (o_ref, lse_ref), wg_refs, scratch = rest[:nw], rest[nw:nw + 2], rest[nw + 2:2 * nw + 2], rest[2 * nw + 2:]
        i = pl.program_id(0)

        @pl.when(i == 0)
        def _():
            gather.start(w_refs, wg_refs, scratch)

        @pl.when(i == nq // 2)
        def _():
            gather.forward(wg_refs, scratch)

        lane = _iota((tq, LANES), 1)
        low = lane < 64
        lowk = _iota((tk, LANES), 1) < 64
        one = jnp.ones((tk, LANES), BF16)
        row = _iota((2 * tq, tk), 0) % tq
        col = _iota((2 * tq, tk), 1)
        cols = [slice(LANES * p, LANES * (p + 1)) for p in range(4)]
        qa = qa_ref[...]
        qs = [_aug_stack(q_ref[:, cols[p]], qa, p) for p in range(4)]

        def step(j, carry, masked):
            ks = pl.ds(pl.multiple_of(j * tk, tk), tk)
            ka = ka_ref[ks, :]
            new = []
            for p in range(4):
                m, acc_e, acc_o = carry[p]
                v2 = v_ref[ks, cols[p]]
                s = _dot_nt(qs[p], jnp.concatenate([k_ref[ks, cols[p]], ka], axis=1))
                if masked:
                    s = jnp.where(col <= row, s, NEG)
                mn = jnp.maximum(m, jnp.max(s, axis=1, keepdims=True))
                al = jnp.exp(m - mn)
                pm = jnp.exp(s - mn).astype(BF16)
                acc_e = al[:tq] * acc_e + _dot(pm[:tq], jnp.where(lowk, v2, one))
                acc_o = al[tq:] * acc_o + _dot(pm[tq:], jnp.where(lowk, one, v2))
                new.append((mn, acc_e, acc_o))
            return tuple(new)

        init = tuple((jnp.full((2 * tq, 1), NEG, F32), jnp.zeros((tq, LANES), F32), jnp.zeros((tq, LANES), F32))
                     for _ in range(4))
        carry = lax.fori_loop(0, i, lambda j, c: step(j, c, False), init)
        carry = step(i, carry, True)
        lse_blk = jnp.zeros((tq, LANES), F32)
        for p in range(4):
            m, acc_e, acc_o = carry[p]
            l_e = pltpu.roll(acc_e, 64, 1)
            l_o = pltpu.roll(acc_o, 64, 1)
            o_ref[:, cols[p]] = jnp.where(low, acc_e / l_e, acc_o / l_o).astype(BF16)
            lse_blk = jnp.where(lane == 2 * p, m[:tq] + jnp.log(l_e), lse_blk)
            lse_blk = jnp.where(lane == 2 * p + 1, m[tq:] + jnp.log(acc_o), lse_blk)
        lse_ref[...] = lse_blk

        @pl.when(i == nq - 1)
        def _():
            gather.finish(wg_refs, scratch)

    return pl.pallas_call(
        body, grid=(nq,),
        out_shape=[jax.ShapeDtypeStruct((T, ATTN_WIDTH), BF16), jax.ShapeDtypeStruct((T, LANES), F32)]
        + gather.out_shapes(),
        in_specs=[_row_spec(tq, 512), _row_spec(tq, LANES),
                  pl.BlockSpec((T, 512), lambda i: (0, 1), pipeline_mode=pl.Buffered(1)),
                  pl.BlockSpec((T, 512), lambda i: (0, 2), pipeline_mode=pl.Buffered(1)),
                  _const_spec((T, LANES))] + [_const_spec(w.shape) for w in w_shards],
        out_specs=[_row_spec(tq, ATTN_WIDTH), _row_spec(tq, LANES)] + [pl.BlockSpec(memory_space=pl.ANY)] * nw,
        scratch_shapes=gather.scratch_shapes(),
        compiler_params=_params(58), name="fwd_attn",
    )(qkv, qaug, qkv, qkv, kaug, *w_shards)


def _fwd_merge(ys, ya, gt, x2, wbs, wba, wo, g2):
    T = x2.shape[0]
    tm = TOKEN_TILE

    def body(ys_ref, ya_ref, gt_ref, x_ref, wbs_ref, wba_ref, wo_ref, g2_ref, a_ref, b_ref, mg_ref, o_ref, h1_ref):
        A = _dot(ys_ref[...], wbs_ref[...])
        B = _dot(ya_ref[...], wba_ref[...])
        mg = (gt_ref[:, :D_MODEL].astype(F32) * A + gt_ref[:, D_MODEL:].astype(F32) * B).astype(BF16)
        o = _dot(mg, wo_ref[...])
        r2, oh = _rms_stats(o)
        a_ref[...] = A.astype(BF16)
        b_ref[...] = B.astype(BF16)
        mg_ref[...] = mg
        o_ref[...] = o.astype(BF16)
        h1_ref[...] = x_ref[...] + oh * g2_ref[...]

    sd = jax.ShapeDtypeStruct((T, D_MODEL), BF16)
    return pl.pallas_call(
        body, grid=(T // tm,),
        out_shape=(sd, sd, sd, sd, jax.ShapeDtypeStruct((T, D_MODEL), F32)),
        in_specs=[_row_spec(tm, 512), _row_spec(tm, 512), _row_spec(tm, 2048), _row_spec(tm, D_MODEL),
                  _const_spec((512, D_MODEL)), _const_spec((512, D_MODEL)), _const_spec((D_MODEL, D_MODEL)),
                  _const_spec((1, D_MODEL))],
        out_specs=[_row_spec(tm, D_MODEL)] * 5,
        compiler_params=_params(40), name="fwd_merge",
    )(ys, ya, gt, x2, wbs, wba, wo, g2)


def _fwd_ffn_loss(h1, tgt, wup, wdn, g3, g4):
    T = h1.shape[0]
    tm = TOKEN_TILE
    nsteps = T // tm

    def body(h1_ref, tg_ref, wup_ref, wdn_ref, g3_ref, g4_ref, xn2_ref, a_ref, ddn_ref, dy_ref, loss_ref,
             dg4_ref, acc_l, acc_g):
        i = pl.program_id(0)

        @pl.when(i == 0)
        def _():
            acc_l[...] = jnp.zeros_like(acc_l)
            acc_g[...] = jnp.zeros_like(acc_g)

        h1v = h1_ref[...]
        r3, h1h = _rms_stats(h1v)
        xn2 = (h1h * g3_ref[...]).astype(BF16)
        xn2_ref[...] = xn2
        dn = jnp.zeros((tm, D_MODEL), F32)
        for j in range(D_FF // 1024):
            cols = slice(1024 * j, 1024 * (j + 1))
            a = _dot(xn2, wup_ref[:, cols])
            a_ref[:, cols] = a.astype(BF16)
            hid = jnp.square(jnp.maximum(a, 0.0)).astype(BF16)
            dn = dn + _dot(hid, wdn_ref[cols, :])
        r4, dnh = _rms_stats(dn)
        g4v = g4_ref[...]
        e = (h1v + dnh * g4v) - tg_ref[...]
        sq = e * e
        s1 = sq[:, 0:LANES]
        for j in range(1, D_MODEL // LANES):
            s1 = s1 + sq[:, LANES * j:LANES * (j + 1)]
        acc_l[...] += _fold8(s1)
        dy = e * (1.0 / D_MODEL)
        dy_ref[...] = dy
        acc_g[...] += _fold8(dy * dnh)
        ddn_ref[...] = _rms_bwd(dy, dnh, r4, g4v).astype(BF16)

        @pl.when(i == nsteps - 1)
        def _():
            loss_ref[...] = acc_l[...] * (0.5 / D_MODEL)
            dg4_ref[...] = jnp.sum(acc_g[...], axis=0, keepdims=True)

    return pl.pallas_call(
        body, grid=(nsteps,),
        out_shape=(jax.ShapeDtypeStruct((T, D_MODEL), BF16), jax.ShapeDtypeStruct((T, D_FF), BF16),
                   jax.ShapeDtypeStruct((T, D_MODEL), BF16), jax.ShapeDtypeStruct((T, D_MODEL), F32),
                   jax.ShapeDtypeStruct((SUBLANES, LANES), F32), jax.ShapeDtypeStruct((1, D_MODEL), F32)),
        in_specs=[_row_spec(tm, D_MODEL), _row_spec(tm, D_MODEL), _const_spec((D_MODEL, D_FF)),
                  _const_spec((D_FF, D_MODEL)), _const_spec((1, D_MODEL)), _const_spec((1, D_MODEL))],
        out_specs=[_row_spec(tm, D_MODEL), _row_spec(tm, D_FF), _row_spec(tm, D_MODEL), _row_spec(tm, D_MODEL),
                   pl.BlockSpec((SUBLANES, LANES), lambda i: (0, 0)), pl.BlockSpec((1, D_MODEL), lambda i: (0, 0))],
        scratch_shapes=[pltpu.VMEM((SUBLANES, LANES), F32), pltpu.VMEM((SUBLANES, D_MODEL), F32)],
        compiler_params=_params(52), name="fwd_ffn_loss",
    )(h1, tgt, wup, wdn, g3, g4)


def _bwd_ffn(ddn, a, dy, h1, wup, wdn, g3):
    T = h1.shape[0]
    tm = TOKEN_TILE
    nsteps = T // tm

    def body(ddn_ref, a_ref, dy_ref, h1_ref, wup_ref, wdn_ref, g3_ref, da_ref, dh1_ref, dg3_ref, acc_g):
        i = pl.program_id(0)

        @pl.when(i == 0)
        def _():
            acc_g[...] = jnp.zeros_like(acc_g)

        ddnv = ddn_ref[...]
        dxn2 = jnp.zeros((tm, D_MODEL), F32)
        for j in range(D_FF // 1024):
            cols = slice(1024 * j, 1024 * (j + 1))
            dhid = _dot_nt(ddnv, wdn_ref[cols, :])
            da = (dhid * (2.0 * jnp.maximum(a_ref[:, cols].astype(F32), 0.0))).astype(BF16)
            da_ref[:, cols] = da
            dxn2 = dxn2 + _dot_nt(da, wup_ref[:, cols])
        r3, h1h = _rms_stats(h1_ref[...])
        acc_g[...] += _fold8(dxn2 * h1h)
        dh1_ref[...] = dy_ref[...] + _rms_bwd(dxn2, h1h, r3, g3_ref[...])

        @pl.when(i == nsteps - 1)
        def _():
            dg3_ref[...] = jnp.sum(acc_g[...], axis=0, keepdims=True)

    return pl.pallas_call(
        body, grid=(nsteps,),
        out_shape=(jax.ShapeDtypeStruct((T, D_FF), BF16), jax.ShapeDtypeStruct((T, D_MODEL), F32),
                   jax.ShapeDtypeStruct((1, D_MODEL), F32)),
        in_specs=[_row_spec(tm, D_MODEL), _row_spec(tm, D_FF), _row_spec(tm, D_MODEL), _row_spec(tm, D_MODEL),
                  _const_spec((D_MODEL, D_FF)), _const_spec((D_FF, D_MODEL)), _const_spec((1, D_MODEL))],
        out_specs=[_row_spec(tm, D_FF), _row_spec(tm, D_MODEL), pl.BlockSpec((1, D_MODEL), lambda i: (0, 0))],
        scratch_shapes=[pltpu.VMEM((SUBLANES, D_MODEL), F32)],
        compiler_params=_params(52), name="bwd_ffn",
    )(ddn, a, dy, h1, wup, wdn, g3)


def _wgrad(xa, dy, name, relu2=False, tn=None):
    T, K = xa.shape
    N = dy.shape[1]
    tn = N if tn is None else tn
    tt = WGRAD_TILE

    def body(x_ref, dy_ref, o_ref):
        @pl.when(pl.program_id(1) == 0)
        def _():
            o_ref[...] = jnp.zeros_like(o_ref)

        xv = x_ref[...]
        if relu2:
            xv = jnp.square(jnp.maximum(xv.astype(F32), 0.0)).astype(BF16)
        o_ref[...] += _dot_tn(xv, dy_ref[...])

    return pl.pallas_call(
        body, grid=(N // tn, T // tt), out_shape=jax.ShapeDtypeStruct((K, N), F32),
        in_specs=[pl.BlockSpec((tt, K), lambda n, t: (t, 0)), pl.BlockSpec((tt, tn), lambda n, t: (t, n))],
        out_specs=pl.BlockSpec((K, tn), lambda n, t: (0, n)),
        compiler_params=_params(52, 2), name=name,
    )(xa, dy)


def _bwd_merge(dh1, o, A, B, gt, ya, lse, cc, wbs, wba, wo, g2):
    T = dh1.shape[0]
    tm = TOKEN_TILE
    nsteps = T // tm

    def body(dh1_ref, o_ref, a_ref, b_ref, gt_ref, ya_ref, lse_ref, cc_ref, wbs_ref, wba_ref, wo_ref, g2_ref,
             do_ref, da_ref, db_ref, dgl_ref, dys_ref, dya_ref, qab_ref, dab_ref, dg2_ref, acc_g):
        i = pl.program_id(0)

        @pl.when(i == 0)
        def _():
            acc_g[...] = jnp.zeros_like(acc_g)

        dh1v = dh1_ref[...]
        r2, oh = _rms_stats(o_ref[...].astype(F32))
        acc_g[...] += _fold8(dh1v * oh)
        do = _rms_bwd(dh1v, oh, r2, g2_ref[...]).astype(BF16)
        do_ref[...] = do
        dmg = _dot_nt(do, wo_ref[...])
        ga = gt_ref[:, :D_MODEL].astype(F32)
        gb = gt_ref[:, D_MODEL:].astype(F32)
        dgl_ref[:, :D_MODEL] = (dmg * a_ref[...].astype(F32) * ga * (1.0 - ga)).astype(BF16)
        dgl_ref[:, D_MODEL:] = (dmg * b_ref[...].astype(F32) * gb * (1.0 - gb)).astype(BF16)
        dA = (dmg * ga).astype(BF16)
        dB = (dmg * gb).astype(BF16)
        da_ref[...] = dA
        db_ref[...] = dB
        dys_ref[...] = _dot_nt(dA, wbs_ref[...]).astype(BF16)
        dya = _dot_nt(dB, wba_ref[...]).astype(BF16)
        dya_ref[...] = dya
        prod = dya.astype(F32) * ya_ref[...].astype(F32)
        lane = _iota((tm, LANES), 1)
        low = lane < 64
        blk = jnp.zeros((tm, LANES), F32)
        for p in range(4):
            pp = prod[:, LANES * p:LANES * (p + 1)]
            blk = jnp.where(lane == 2 * p, jnp.sum(jnp.where(low, pp, 0.0), axis=1, keepdims=True), blk)
            blk = jnp.where(lane == 2 * p + 1, jnp.sum(jnp.where(low, 0.0, pp), axis=1, keepdims=True), blk)
        qab_ref[...] = _aug_query(cc_ref[...] - lse_ref[...])
        dab_ref[...] = _spread(_split3(-blk), 0).astype(BF16)

        @pl.when(i == nsteps - 1)
        def _():
            dg2_ref[...] = jnp.sum(acc_g[...], axis=0, keepdims=True)

    sd = jax.ShapeDtypeStruct((T, D_MODEL), BF16)
    sh = jax.ShapeDtypeStruct((T, 512), BF16)
    sa = jax.ShapeDtypeStruct((T, LANES), BF16)
    return pl.pallas_call(
        body, grid=(nsteps,),
        out_shape=(sd, sd, sd, jax.ShapeDtypeStruct((T, 2048), BF16), sh, sh, sa, sa,
                   jax.ShapeDtypeStruct((1, D_MODEL), F32)),
        in_specs=[_row_spec(tm, D_MODEL)] * 4 + [_row_spec(tm, 2048), _row_spec(tm, 512), _row_spec(tm, LANES),
                  _row_spec(tm, LANES), _const_spec((512, D_MODEL)), _const_spec((512, D_MODEL)),
                  _const_spec((D_MODEL, D_MODEL)), _const_spec((1, D_MODEL))],
        out_specs=[_row_spec(tm, D_MODEL)] * 3 + [_row_spec(tm, 2048), _row_spec(tm, 512), _row_spec(tm, 512),
                   _row_spec(tm, LANES), _row_spec(tm, LANES), pl.BlockSpec((1, D_MODEL), lambda i: (0, 0))],
        scratch_shapes=[pltpu.VMEM((SUBLANES, D_MODEL), F32)],
        compiler_params=_params(40), name="bwd_merge",
    )(dh1, o, A, B, gt, ya, lse, cc, wbs, wba, wo, g2)


def _bwd_sgu(zuv, dys, gs, bs, wsp, bT):
    T = zuv.shape[0]
    tc = SGU_TILE
    nc = tc // CHUNK
    nsteps = T // tc

    def body(z_ref, dy_ref, gs_ref, bs_ref, w_ref, bT_ref, dz_ref, dw_ref, dbT_ref, dgs_ref, dbs_ref,
             acc_w, acc_b, acc_gs, acc_bs, dvln_s):
        i = pl.program_id(0)

        @pl.when(i == 0)
        def _():
            acc_w[...] = jnp.zeros_like(acc_w)
            acc_b[...] = jnp.zeros_like(acc_b)
            acc_gs[...] = jnp.zeros_like(acc_gs)
            acc_bs[...] = jnp.zeros_like(acc_bs)

        z = z_ref[...].astype(F32)
        gsv = gs_ref[...]
        u, vhat, rs, vln = _sgu_forward_parts(z, gsv, bs_ref[...])
        vb = vln.astype(BF16)
        dy = dy_ref[...].astype(F32)
        low_w = (_iota((CHUNK, nc * LANES), 1) % LANES) < 64
        for p in range(4):
            we, wo, b2 = _sgu_pair_weights(w_ref, bT_ref[...], p)
            vcat = _chunks_on_lanes(vb, p, nc)
            s = _sgu_mix(we, wo, b2, vcat, nc)
            dyc = _chunks_on_lanes(dy, p, nc)
            ds = dyc * _chunks_on_lanes(u, p, nc)
            dsb = ds.astype(BF16)
            zero = jnp.zeros_like(dsb)
            dse = jnp.where(low_w, dsb, zero)
            dso = jnp.where(low_w, zero, dsb)
            acc_w[2 * p] += _dot_nt(dse, vcat)
            acc_w[2 * p + 1] += _dot_nt(dso, vcat)
            acc_b[p] += ds
            dvl = jnp.where(low_w, _dot_tn(we, dsb), _dot_tn(wo, dsb))
            for c in range(nc):
                rows, cols = slice(c * CHUNK, (c + 1) * CHUNK), slice(LANES * p, LANES * (p + 1))
                dvln_s[rows, cols] = dvl[:, c * LANES:(c + 1) * LANES]
                du = dy[rows, cols] * s[:, c * LANES:(c + 1) * LANES]
                dz_ref[rows, cols] = (du * _gelu_grad(z[rows, cols])).astype(BF16)
        dvln = dvln_s[...]
        acc_gs[...] += _fold8(dvln * vhat)
        acc_bs[...] += _fold8(dvln)
        al = dvln * gsv
        dvv = rs * (al - jnp.mean(al, axis=1, keepdims=True) - vhat * jnp.mean(al * vhat, axis=1, keepdims=True))
        dz_ref[:, SGU_WIDTH:] = (dvv * _gelu_grad(z[:, SGU_WIDTH:])).astype(BF16)

        @pl.when(i == nsteps - 1)
        def _():
            tril = _iota((CHUNK, CHUNK), 0) >= _iota((CHUNK, CHUNK), 1)
            lane = _iota((CHUNK, LANES), 1)
            low = lane < 64
            blk = jnp.zeros((CHUNK, LANES), F32)
            for g in range(8):
                dw_ref[g] = jnp.where(tril, acc_w[g], 0.0)
            for p in range(4):
                t = acc_b[p]
                tot = t[:, 0:LANES]
                for c in range(1, nc):
                    tot = tot + t[:, c * LANES:(c + 1) * LANES]
                blk = jnp.where(lane == 2 * p, jnp.sum(jnp.where(low, tot, 0.0), axis=1, keepdims=True), blk)
                blk = jnp.where(lane == 2 * p + 1, jnp.sum(jnp.where(low, 0.0, tot), axis=1, keepdims=True), blk)
            dbT_ref[...] = blk
            dgs_ref[...] = jnp.sum(acc_gs[...], axis=0, keepdims=True)
            dbs_ref[...] = jnp.sum(acc_bs[...], axis=0, keepdims=True)

    whole = lambda shape: pl.BlockSpec(shape, lambda i: (0,) * len(shape))
    return pl.pallas_call(
        body, grid=(nsteps,),
        out_shape=(jax.ShapeDtypeStruct((T, 1024), BF16), jax.ShapeDtypeStruct((8, CHUNK, CHUNK), F32),
                   jax.ShapeDtypeStruct((CHUNK, LANES), F32), jax.ShapeDtypeStruct((1, SGU_WIDTH), F32),
                   jax.ShapeDtypeStruct((1, SGU_WIDTH), F32)),
        in_specs=[_row_spec(tc, 1024), _row_spec(tc, SGU_WIDTH), _const_spec((1, SGU_WIDTH)),
                  _const_spec((1, SGU_WIDTH)), _const_spec((8, CHUNK, CHUNK)), _const_spec((CHUNK, 8))],
        out_specs=[_row_spec(tc, 1024), whole((8, CHUNK, CHUNK)), whole((CHUNK, LANES)),
                   whole((1, SGU_WIDTH)), whole((1, SGU_WIDTH))],
        scratch_shapes=[pltpu.VMEM((8, CHUNK, CHUNK), F32), pltpu.VMEM((4, CHUNK, nc * LANES), F32),
                        pltpu.VMEM((SUBLANES, SGU_WIDTH), F32), pltpu.VMEM((SUBLANES, SGU_WIDTH), F32),
                        pltpu.VMEM((tc, SGU_WIDTH), F32)],
        compiler_params=_params(48), name="bwd_sgu",
    )(zuv, dys, gs, bs, wsp, bT)


def _bwd_attn(qkv, dya, qab, dab, kaug, grads):
    T = qkv.shape[0]
    tq = tk = ATTN_TILE
    nq = T // tq
    nk = T // tk
    red = _GradReduce([tuple(g.shape[1:]) for g in grads])
    nr = red.n

    def body(q_ref, do_ref, qa_ref, da_ref, k_ref, v_ref, ka_ref, *rest):
        g_refs, (dq_ref, dk_ref, dv_ref, dcx_ref) = rest[:nr], rest[nr:nr + 4]
        hbm, dq_acc, scratch = rest[nr + 4:5 * nr + 4], rest[5 * nr + 4], rest[5 * nr + 5:]
        p = pl.program_id(0)
        j = pl.program_id(1)

        @pl.when((p == 0) & (j == 0))
        def _():
            red.start(g_refs, hbm, scratch)

        @pl.when((p == 0) & (j == min(3, nk - 1)))
        def _():
            red.partials(g_refs, hbm, scratch)

        @pl.when((p == 2) & (j == 0))
        def _():
            red.finish(g_refs, hbm, scratch)

        lane = _iota((tq, LANES), 1)
        low = lane < 64
        row = _iota((2 * tq, tk), 0) % tq
        col = _iota((2 * tq, tk), 1)
        first = 2 * AUG_LANES * p

        @pl.when(j == 0)
        def _():
            dq_acc[...] = jnp.zeros_like(dq_acc)

        @pl.when((j == 0) & (p == 0))
        def _():
            dcx_ref[...] = jnp.zeros_like(dcx_ref)

        ka = ka_ref[...]
        kk = jnp.concatenate([k_ref[...], ka], axis=1)
        vv = jnp.concatenate([v_ref[...], ka], axis=1)

        def q_block(i, carry, masked):
            dk_a, dv_a = carry
            qsl = pl.ds(pl.multiple_of(i * tq, tq), tq)
            qs = _aug_stack(q_ref[qsl, :], qa_ref[qsl, :], p)
            dos = _aug_stack(do_ref[qsl, :], da_ref[qsl, :], p)
            s = _dot_nt(qs, kk)
            if masked:
                s = jnp.where(col <= row, s, NEG)
            pm = jnp.exp(s)
            ds = pm * _dot_nt(dos, vv)
            dsb = ds.astype(BF16)
            dv_a = dv_a + _dot_tn(pm.astype(BF16), dos[:, :LANES])
            dk_a = dk_a + _dot_tn(dsb, qs)
            dqx = _dot(dsb, kk)
            dq_acc[qsl, :] += jnp.where(low, dqx[:tq, :LANES], dqx[tq:, :LANES])
            dcx_ref[qsl, :] += (jnp.where(lane == first, dqx[:tq, LANES:], 0.0)
                                + jnp.where(lane == first + AUG_LANES, dqx[tq:, LANES:], 0.0))
            return dk_a, dv_a

        init = (jnp.zeros((tk, 2 * LANES), F32), jnp.zeros((tk, LANES), F32))
        carry = q_block(j, init, True)
        dk_a, dv_a = lax.fori_loop(j + 1, nq, lambda i, c: q_block(i, c, False), carry)
        dk_ref[...] = dk_a[:, :LANES].astype(BF16)
        dv_ref[...] = dv_a.astype(BF16)
        ksl = pl.ds(pl.multiple_of(j * tk, tk), tk)
        lk = _iota((tk, LANES), 1)
        dcx_ref[ksl, :] += jnp.where((lk == first + 3) | (lk == first + AUG_LANES + 3), dk_a[:, LANES:], 0.0)

        @pl.when(j == nk - 1)
        def _():
            dq_ref[...] = (dq_acc[...] * 0.125).astype(BF16)

    sh = jax.ShapeDtypeStruct((T, ATTN_WIDTH), BF16)
    full = lambda cb: pl.BlockSpec((T, LANES), lambda p, j: (0, cb + p))
    blk = lambda cb: pl.BlockSpec((tk, LANES), lambda p, j: (j, cb + p))
    hbm_spec = pl.BlockSpec(memory_space=pl.ANY)
    res = pl.pallas_call(
        body, grid=(4, nk),
        out_shape=[sh, sh, sh, jax.ShapeDtypeStruct((T, LANES), F32)] + red.out_shapes(),
        in_specs=[full(0), full(0), _const_spec((T, LANES)), _const_spec((T, LANES)), blk(4), blk(8),
                  pl.BlockSpec((tk, LANES), lambda p, j: (j, 0))] + [hbm_spec] * nr,
        out_specs=[full(0), blk(0), blk(0), pl.BlockSpec((T, LANES), lambda p, j: (0, 0))] + [hbm_spec] * (4 * nr),
        scratch_shapes=[pltpu.VMEM((T, LANES), F32)] + red.scratch_shapes(),
        compiler_params=_params(58, 2), name="bwd_attn",
    )(qkv, dya, qab, dab, qkv, qkv, kaug, *grads)
    return res[:4 + nr]


def _bwd_cum(dcx, fl, bfp):
    T = fl.shape[0]
    tb = CUM_TILE

    def body(dcx_ref, fl_ref, b_ref, dfl_ref, dbf_ref):
        triu = (_iota((tb, tb), 0) <= _iota((tb, tb), 1)).astype(F32)
        r, c = _iota((LANES, LANES), 0), _iota((LANES, LANES), 1)
        sel = (((r == AUG_LANES * c) & (c < N_HEADS)).astype(F32)
               - ((r == AUG_LANES * c + 3) & (c < N_HEADS)).astype(F32))
        carry = jnp.zeros((1, LANES), F32)
        dbf = jnp.zeros((1, LANES), F32)
        for i in reversed(range(T // tb)):
            colblk = jnp.dot(dcx_ref[i * tb:(i + 1) * tb, :], sel, precision=HIGHEST, preferred_element_type=F32)
            rc = jnp.dot(triu, colblk, precision=HIGHEST, preferred_element_type=F32) + carry
            carry = rc[0:1, :]
            sig = jax.nn.sigmoid(fl_ref[i * tb:(i + 1) * tb, :] + b_ref[...])
            dfl = rc * (1.0 - sig)
            dfl_ref[i * tb:(i + 1) * tb, :] = dfl.astype(BF16)
            dbf = dbf + jnp.sum(dfl, axis=0, keepdims=True)
        dbf_ref[...] = dbf

    return pl.pallas_call(
        body,
        out_shape=(jax.ShapeDtypeStruct((T, LANES), BF16), jax.ShapeDtypeStruct((1, LANES), F32)),
        compiler_params=pltpu.CompilerParams(vmem_limit_bytes=32 * MIB), name="bwd_cum",
    )(dcx, fl, bfp)


def _bwd_in(dz, dq, dk, dv, dfl, dgl, dh1, x2, g1, wz, wf, wg):
    T = x2.shape[0]
    tm = TOKEN_TILE
    nsteps = T // tm

    def body(dz_ref, dq_ref, dk_ref, dv_ref, dfl_ref, dgl_ref, dh1_ref, x_ref, g_ref, wz_ref, wf_ref, wg_ref,
             dx_ref, dg1_ref, acc_g):
        i = pl.program_id(0)

        @pl.when(i == 0)
        def _():
            acc_g[...] = jnp.zeros_like(acc_g)

        dxn = _dot_nt(dz_ref[...], wz_ref[:, 0:1024])
        dxn = dxn + _dot_nt(dq_ref[...], wz_ref[:, 1024:1536])
        dxn = dxn + _dot_nt(dk_ref[...], wz_ref[:, 1536:2048])
        dxn = dxn + _dot_nt(dv_ref[...], wz_ref[:, 2048:2560])
        dxn = dxn + _dot_nt(dfl_ref[...], wf_ref[...])
        dxn = dxn + _dot_nt(dgl_ref[...], wg_ref[...])
        r1, xh = _rms_stats(x_ref[...])
        acc_g[...] += _fold8(dxn * xh)
        dx_ref[...] = dh1_ref[...] + _rms_bwd(dxn, xh, r1, g_ref[...])

        @pl.when(i == nsteps - 1)
        def _():
            dg1_ref[...] = jnp.sum(acc_g[...], axis=0, keepdims=True)

    return pl.pallas_call(
        body, grid=(nsteps,),
        out_shape=(jax.ShapeDtypeStruct((T, D_MODEL), F32), jax.ShapeDtypeStruct((1, D_MODEL), F32)),
        in_specs=[_row_spec(tm, 1024), _row_spec(tm, 512), _row_spec(tm, 512), _row_spec(tm, 512),
                  _row_spec(tm, LANES), _row_spec(tm, 2048), _row_spec(tm, D_MODEL), _row_spec(tm, D_MODEL),
                  _const_spec((1, D_MODEL)), _const_spec((D_MODEL, ZQKV_WIDTH)), _const_spec((D_MODEL, LANES)),
                  _const_spec((D_MODEL, 2048))],
        out_specs=[_row_spec(tm, D_MODEL), pl.BlockSpec((1, D_MODEL), lambda i: (0, 0))],
        scratch_shapes=[pltpu.VMEM((SUBLANES, D_MODEL), F32)],
        compiler_params=_params(48), name="bwd_in",
    )(dz, dq, dk, dv, dfl, dgl, dh1, x2, g1, wz, wf, wg)


def _pad_rows(v, rows):
    return jnp.pad(v, ((0, rows - v.shape[0]), (0, 0)))


def _pack_small(g_mix_pre, b_forget, g_sgu, b_sgu, w_spatial, b_spatial, g_mix_post, g_ffn_pre, g_ffn_post):
    vec = lambda v: _pad_rows(v.reshape(-1, LANES), SUBLANES)
    return jnp.concatenate([
        w_spatial.reshape(-1, LANES), vec(g_mix_pre), _pad_rows(jnp.pad(b_forget, ((0, 0), (0, LANES - N_HEADS))), SUBLANES),
        vec(g_sgu), vec(b_sgu), vec(b_spatial), vec(g_mix_post), vec(g_ffn_pre), vec(g_ffn_post)], axis=0)


def _unpack_small(p):
    nw = N_HEADS * CHUNK * CHUNK // LANES
    blk = lambda k: p[nw + SUBLANES * k: nw + SUBLANES * (k + 1)]
    return dict(
        w_spatial=p[:nw].reshape(1, N_HEADS, CHUNK, CHUNK),
        g_mix_pre=blk(0).reshape(1, D_MODEL), b_forget=blk(1)[0:1, :N_HEADS],
        g_sgu=blk(2)[:4].reshape(1, SGU_WIDTH), b_sgu=blk(3)[:4].reshape(1, SGU_WIDTH),
        b_spatial=blk(4).reshape(1, N_HEADS, CHUNK), g_mix_post=blk(5).reshape(1, D_MODEL),
        g_ffn_pre=blk(6).reshape(1, D_MODEL), g_ffn_post=blk(7).reshape(1, D_MODEL))


def kernel(x, g_mix_pre, w_in, b_forget, g_sgu, b_sgu, w_spatial, b_spatial, w_branch_sgu, w_branch_attn, w_out, g_mix_post, g_ffn_pre, w_up, w_down, g_ffn_post, loss_target, m_g_mix_pre, m_w_in, m_b_forget, m_g_sgu, m_b_sgu, m_w_spatial, m_b_spatial, m_w_branch_sgu, m_w_branch_attn, m_w_out, m_g_mix_post, m_g_ffn_pre, m_w_up, m_w_down, m_g_ffn_post, v_g_mix_pre, v_w_in, v_b_forget, v_g_sgu, v_b_sgu, v_w_spatial, v_b_spatial, v_w_branch_sgu, v_w_branch_attn, v_w_out, v_g_mix_post, v_g_ffn_pre, v_w_up, v_w_down, v_g_ffn_post):
    T = x.shape[1]
    x2 = x.reshape(T, D_MODEL)
    tgt = loss_target.reshape(T, D_MODEL)

    wg_in = _gather_w_in(w_in[0])
    w_in_full = jnp.concatenate([wg_in[j, :, :IN_SHARD] for j in range(N_DEV)], axis=1)
    wz = w_in_full[:, :ZQKV_WIDTH]
    wf = jnp.pad(w_in_full[:, ZQKV_WIDTH:GATE_OFFSET], ((0, 0), (0, LANES - N_HEADS)))
    wgt = w_in_full[:, GATE_OFFSET:]
    bfp = jnp.pad(b_forget, ((0, 0), (0, LANES - N_HEADS)))
    wsp = w_spatial[0]
    bT = b_spatial[0].T

    xn, zuv, qkv, fl, gt = _fwd_in(x2, g_mix_pre, wz, wf, wgt)
    cc, qaug, kaug = _fwd_cum(fl, bfp)
    ys = _fwd_sgu(zuv, g_sgu, b_sgu, wsp, bT)
    ya, lse, wbs, wba, wo, wup, wdn = _fwd_attn(
        qkv, qaug, kaug, (w_branch_sgu[0], w_branch_attn[0], w_out[0], w_up[0], w_down[0]))
    A, B, mg, o, h1 = _fwd_merge(ys, ya, gt, x2, wbs, wba, wo, g_mix_post)
    xn2, a, ddn, dy, loss_part, dg4 = _fwd_ffn_loss(h1, tgt, wup, wdn, g_ffn_pre, g_ffn_post)
    loss = lax.psum(jnp.sum(loss_part), ("x", "y", "c"))

    da, dh1, dg3 = _bwd_ffn(ddn, a, dy, h1, wup, wdn, g_ffn_pre)
    dw_up = _wgrad(xn2, da, "wgrad_up", tn=2048)
    dw_down = _wgrad(a, ddn, "wgrad_down", relu2=True)
    do, dA, dB, dgl, dys, dya, qab, dab, dg2 = _bwd_merge(dh1, o, A, B, gt, ya, lse, cc, wbs, wba, wo, g_mix_post)
    dw_out = _wgrad(mg, do, "wgrad_out")
    dw_bs = _wgrad(ys, dA, "wgrad_branch_sgu")
    dw_ba = _wgrad(ya, dB, "wgrad_branch_attn")
    dzuv, dwsp, dbT, dgs, dbs = _bwd_sgu(zuv, dys, g_sgu, b_sgu, wsp, bT)
    col_blocks = lambda g, w: g.reshape(g.shape[0], N_DEV, w).transpose(1, 0, 2)
    row_blocks = lambda g, r: g.reshape(N_DEV, r, g.shape[1])
    dq, dk, dv, dcx, g_bs, g_ba, g_out, g_up, g_down = _bwd_attn(
        qkv, dya, qab, dab, kaug,
        [col_blocks(dw_bs, 128), col_blocks(dw_ba, 128), row_blocks(dw_out, 128), col_blocks(dw_up, 512),
         row_blocks(dw_down, 512)])
    dfl, dbf = _bwd_cum(dcx, fl, bfp)
    dx, dg1 = _bwd_in(dzuv, dq, dk, dv, dfl, dgl, dh1, x2, g_mix_pre, wz, wf, wgt)
    dw_z = _wgrad(xn, dzuv, "wgrad_in_z")
    dw_q = _wgrad(xn, dq, "wgrad_in_q")
    dw_k = _wgrad(xn, dk, "wgrad_in_k")
    dw_v = _wgrad(xn, dv, "wgrad_in_v")
    dw_f = _wgrad(xn, dfl, "wgrad_in_f")
    dw_g = _wgrad(xn, dgl, "wgrad_in_gate")
    dw_in = jnp.concatenate([dw_z, dw_q, dw_k, dw_v, dw_f[:, :N_HEADS], dw_g], axis=1)

    blocks_in = jnp.pad(dw_in.reshape(D_MODEL, N_DEV, IN_SHARD).transpose(1, 0, 2),
                        ((0, 0), (0, 0), (0, IN_SHARD_PAD - IN_SHARD)))
    g_in = _reduce_scatter_grads([blocks_in])[0][:, :IN_SHARD]

    small_g = _pack_small(dg1, dbf[:, :N_HEADS], dgs, dbs, dwsp[None], dbT[:, :N_HEADS].T[None], dg2, dg3, dg4)
    small_w = _pack_small(g_mix_pre, b_forget, g_sgu, b_sgu, w_spatial, b_spatial, g_mix_post, g_ffn_pre, g_ffn_post)
    small_m = _pack_small(m_g_mix_pre, m_b_forget, m_g_sgu, m_b_sgu, m_w_spatial, m_b_spatial, m_g_mix_post,
                          m_g_ffn_pre, m_g_ffn_post)
    small_v = _pack_small(v_g_mix_pre, v_b_forget, v_g_sgu, v_b_sgu, v_w_spatial, v_b_spatial, v_g_mix_post,
                          v_g_ffn_pre, v_g_ffn_post)
    sg, sd, sm, sv = (_unpack_small(t) for t in _allreduce_small_adamw(small_g, small_w, small_m, small_v))

    big = {}
    for name, w, g, m, v in (
            ("w_in", w_in, g_in, m_w_in, v_w_in), ("w_branch_sgu", w_branch_sgu, g_bs, m_w_branch_sgu, v_w_branch_sgu),
            ("w_branch_attn", w_branch_attn, g_ba, m_w_branch_attn, v_w_branch_attn),
            ("w_out", w_out, g_out, m_w_out, v_w_out), ("w_up", w_up, g_up, m_w_up, v_w_up),
            ("w_down", w_down, g_down, m_w_down, v_w_down)):
        d_, m_, v_ = _adamw(w[0], g, m[0], v[0], "adamw_" + name)
        big[name] = (g[None], d_[None], m_[None], v_[None])

    order = ["g_mix_pre", "w_in", "b_forget", "g_sgu", "b_sgu", "w_spatial", "b_spatial", "w_branch_sgu",
             "w_branch_attn", "w_out", "g_mix_post", "g_ffn_pre", "w_up", "w_down", "g_ffn_post"]
    outs = [loss, dx.reshape(1, T, D_MODEL)]
    for kind, small in enumerate((sg, sd, sm, sv)):
        outs += [big[nm][kind] if nm in big else small[nm] for nm in order]
    return tuple(outs)
```

```python
import jax
import jax.numpy as jnp
from jax import lax
from jax.experimental import pallas as pl
from jax.experimental.pallas import tpu as pltpu

F32 = jnp.float32
BF16 = jnp.bfloat16
HIGHEST = lax.Precision.HIGHEST
MESH = pl.DeviceIdType.MESH

D_MODEL = 1024
SGU_WIDTH = 512
ATTN_WIDTH = 512
N_HEADS = 8
CHUNK = 128
D_FF = 4096
IN_WIDTH = 4616
N_DEV = 8
IN_SHARD = IN_WIDTH // N_DEV
IN_SHARD_PAD = 640
ZQKV_WIDTH = 2 * SGU_WIDTH + 3 * ATTN_WIDTH
GATE_OFFSET = ZQKV_WIDTH + N_HEADS
EPS = 1e-6
LANES = 128
SUBLANES = 8
VMEM_BYTES = 64 * 1024 * 1024
MIB = 1024 * 1024

ADAM_LR = 0.001
ADAM_B1 = 0.9
ADAM_B2 = 0.999
ADAM_EPS = 1e-08
ADAM_WD = 0.01
ADAM_STEP = 10

TOKEN_TILE = 256
ATTN_TILE = 512
CUM_TILE = 256
SGU_TILE = 512
WGRAD_TILE = 512
NEG = -1e30

NT_DIMS = (((1,), (1,)), ((), ()))
TN_DIMS = (((0,), (0,)), ((), ()))


def _params(vmem_mb, n_grid=1):
    return pltpu.CompilerParams(
        dimension_semantics=("arbitrary",) * n_grid,
        vmem_limit_bytes=min(vmem_mb * MIB, VMEM_BYTES - 6 * MIB),
    )


def _dot(a, b):
    return jnp.dot(a, b, preferred_element_type=F32)


def _dot_nt(a, b):
    return lax.dot_general(a, b, NT_DIMS, preferred_element_type=F32)


def _dot_tn(a, b):
    return lax.dot_general(a, b, TN_DIMS, preferred_element_type=F32)


def _const_spec(shape):
    nd = len(shape)
    return pl.BlockSpec(shape, lambda *_: (0,) * nd, pipeline_mode=pl.Buffered(1))


def _row_spec(tm, n, col=0):
    return pl.BlockSpec((tm, n), lambda i: (i, col))


def _fold8(v):
    return v.reshape(v.shape[0] // SUBLANES, SUBLANES, v.shape[1]).sum(axis=0)


def _pick(v, lane_iota, k):
    return jnp.sum(jnp.where(lane_iota == k, v, 0.0), axis=1, keepdims=True)


def _iota(shape, dim):
    return lax.broadcasted_iota(jnp.int32, shape, dim)


def _gelu(x):
    c = 0.7978845608028654
    return 0.5 * x * (1.0 + jnp.tanh(c * (x + 0.044715 * x * x * x)))


def _gelu_grad(x):
    c = 0.7978845608028654
    t = jnp.tanh(c * (x + 0.044715 * x * x * x))
    return 0.5 * (1.0 + t) + 0.5 * x * (1.0 - t * t) * (c * (1.0 + 3.0 * 0.044715 * x * x))


def _rms_stats(v):
    r = lax.rsqrt(jnp.mean(v * v, axis=1, keepdims=True) + EPS)
    return r, v * r


def _rms_bwd(dout, vhat, r, g):
    a = dout * g
    return r * (a - vhat * jnp.mean(a * vhat, axis=1, keepdims=True))


def _mesh_pos():
    return lax.axis_index("x"), lax.axis_index("y"), lax.axis_index("c")


def _dev_index(px, py, pc):
    return 4 * px + 2 * py + pc


def _other_chips(x, y):
    return [(1 - x, y), (x, 1 - y), (1 - x, 1 - y)]


class _WeightGather:
    def __init__(self, shard_shapes, kinds, stage_shapes=None):
        self.shard_shapes = list(shard_shapes)
        self.kinds = list(kinds)
        self.stage_shapes = list(stage_shapes or shard_shapes)
        self.n = len(self.kinds)

    def out_shapes(self):
        shapes = []
        for (rows, cols), kind in zip(self.stage_shapes, self.kinds):
            full = {"block": (N_DEV, rows, cols), "rows": (N_DEV * rows, cols), "cols": (rows, N_DEV * cols)}[kind]
            shapes.append(jax.ShapeDtypeStruct(full, BF16))
        return shapes

    def scratch_shapes(self):
        return ([pltpu.VMEM(s, BF16) for s in self.stage_shapes]
                + [pltpu.SemaphoreType.DMA((self.n, 7)), pltpu.SemaphoreType.DMA((self.n, 7)),
                   pltpu.SemaphoreType.DMA((self.n,))])

    def _view(self, a, ref, j):
        rows, cols = self.stage_shapes[a]
        if self.kinds[a] == "block":
            return ref.at[j]
        if self.kinds[a] == "rows":
            return ref.at[pl.ds(pl.multiple_of(j * rows, rows), rows), :]
        return ref.at[:, pl.ds(pl.multiple_of(j * cols, cols), cols)]

    def _copy(self, outs, scratch, a, k, block, to, from_stage=False):
        stage, (send_sems, recv_sems, _) = scratch[:self.n], scratch[self.n:]
        dst = self._view(a, outs[a], _dev_index(*block))
        return pltpu.make_async_remote_copy(
            src_ref=stage[a] if from_stage else dst, dst_ref=dst,
            send_sem=send_sems.at[a, k], recv_sem=recv_sems.at[a, k],
            device_id=to, device_id_type=MESH)

    def _local(self, outs, scratch, a, me):
        return pltpu.make_async_copy(scratch[a], self._view(a, outs[a], _dev_index(*me)), scratch[-1].at[a])

    def start(self, ins, outs, scratch):
        x, y, c = _mesh_pos()
        me, sibling = (x, y, c), (x, y, 1 - c)
        for a in range(self.n):
            rows, cols = self.shard_shapes[a]
            if self.stage_shapes[a] != self.shard_shapes[a]:
                scratch[a][...] = jnp.zeros(self.stage_shapes[a], BF16)
            scratch[a][0:rows, 0:cols] = ins[a][...].astype(BF16)
            self._local(outs, scratch, a, me).start()
        for a in range(self.n):
            self._copy(outs, scratch, a, 0, me, sibling, True).start()
            for j, chip in enumerate(_other_chips(x, y)):
                self._copy(outs, scratch, a, 1 + j, me, (*chip, c), True).start()

    def forward(self, outs, scratch):
        x, y, c = _mesh_pos()
        me, sibling = (x, y, c), (x, y, 1 - c)
        for a in range(self.n):
            for j, chip in enumerate(_other_chips(x, y)):
                self._copy(outs, scratch, a, 1 + j, (*chip, c), me).wait_recv()
                self._copy(outs, scratch, a, 4 + j, (*chip, c), sibling).start()

    def finish(self, outs, scratch):
        x, y, c = _mesh_pos()
        me, sibling = (x, y, c), (x, y, 1 - c)
        chips = _other_chips(x, y)
        for a in range(self.n):
            self._copy(outs, scratch, a, 0, sibling, me).wait_recv()
            for j, chip in enumerate(chips):
                self._copy(outs, scratch, a, 4 + j, (*chip, 1 - c), me).wait_recv()
        for a in range(self.n):
            self._copy(outs, scratch, a, 0, me, sibling, True).wait_send()
            for j, chip in enumerate(chips):
                self._copy(outs, scratch, a, 1 + j, me, (*chip, c), True).wait_send()
                self._copy(outs, scratch, a, 4 + j, (*chip, c), sibling).wait_send()
            self._local(outs, scratch, a, me).wait()


def _gather_w_in(w_in_local):
    g = _WeightGather([(D_MODEL, IN_SHARD)], ["block"], [(D_MODEL, IN_SHARD_PAD)])

    def body(w_ref, out_ref, *scratch):
        g.start([w_ref], [out_ref], scratch)
        g.forward([out_ref], scratch)
        g.finish([out_ref], scratch)

    return pl.pallas_call(
        body,
        out_shape=g.out_shapes()[0],
        in_specs=[pl.BlockSpec(memory_space=pltpu.VMEM)],
        out_specs=pl.BlockSpec(memory_space=pl.ANY),
        scratch_shapes=g.scratch_shapes(),
        compiler_params=pltpu.CompilerParams(vmem_limit_bytes=32 * MIB),
        name="gather_w_in",
    )(w_in_local)


class _GradExchange:
    def __init__(self, shapes):
        self.shapes = [tuple(s) for s in shapes]
        self.n = len(self.shapes)

    def land_shapes(self, stage):
        slots, dtype = (4, F32) if stage == 1 else (3, BF16)
        return [jax.ShapeDtypeStruct((slots,) + s, dtype) for s in self.shapes]

    def sem_shapes(self, stage):
        slots = 4 if stage == 1 else 3
        return [pltpu.SemaphoreType.DMA((self.n, slots)), pltpu.SemaphoreType.DMA((self.n, slots))]

    def _copy(self, stage, srcs, lands, sems, a, k):
        x, y, c = _mesh_pos()
        cx, cy = (_other_chips(x, y) + [(x, y)])[k]
        if stage == 1:
            src, to = srcs[a].at[_dev_index(cx, cy, 1 - c)], (x, y, 1 - c)
        else:
            src, to = srcs[a].at[k], (cx, cy, c)
        return pltpu.make_async_remote_copy(
            src_ref=src, dst_ref=lands[a].at[k], send_sem=sems[0].at[a, k], recv_sem=sems[1].at[a, k],
            device_id=to, device_id_type=MESH)

    def start(self, stage, srcs, lands, sems):
        for a in range(self.n):
            for k in range(4 if stage == 1 else 3):
                self._copy(stage, srcs, lands, sems, a, k).start()

    def wait(self, stage, srcs, lands, sems):
        for a in range(self.n):
            for k in range(4 if stage == 1 else 3):
                cp = self._copy(stage, srcs, lands, sems, a, k)
                cp.wait_recv()
                cp.wait_send()


def _owner_indices():
    x, y, c = _mesh_pos()
    return jnp.stack([_dev_index(cx, cy, c) for cx, cy in _other_chips(x, y) + [(x, y)]]).astype(jnp.int32)


def _chip_partials(g, land1, idx, name):
    _, rows, cols = g.shape
    tr = min(rows, 256)

    def body(idx_ref, g_ref, l_ref, o_ref):
        o_ref[...] = (g_ref[...] + l_ref[...]).astype(BF16)

    return pl.pallas_call(
        body,
        grid_spec=pltpu.PrefetchScalarGridSpec(
            num_scalar_prefetch=1, grid=(3, rows // tr),
            in_specs=[pl.BlockSpec((None, tr, cols), lambda k, r, idx: (idx[k], r, 0)),
                      pl.BlockSpec((None, tr, cols), lambda k, r, idx: (k, r, 0))],
            out_specs=pl.BlockSpec((None, tr, cols), lambda k, r, idx: (k, r, 0))),
        out_shape=jax.ShapeDtypeStruct((3, rows, cols), BF16),
        compiler_params=_params(32, 2), name=name,
    )(idx, g, land1)


def _reduced_block(g_ref, l1_ref, a_ref, b_ref, c_ref):
    return ((g_ref[...] + l1_ref[...]) + a_ref[...].astype(F32)) + b_ref[...].astype(F32) + c_ref[...].astype(F32)


def _reduced_specs(tm, cols):
    return [pl.BlockSpec((None, tm, cols), lambda i, idx: (idx[3], i, 0)),
            pl.BlockSpec((None, tm, cols), lambda i, idx: (3, i, 0)),
            pl.BlockSpec((None, tm, cols), lambda i, idx: (0, i, 0)),
            pl.BlockSpec((None, tm, cols), lambda i, idx: (1, i, 0)),
            pl.BlockSpec((None, tm, cols), lambda i, idx: (2, i, 0))]


def _reduced_grad(g, land1, land2, idx, name):
    _, rows, cols = g.shape
    tm = min(rows, 256)

    def body(idx_ref, g_ref, l1_ref, a_ref, b_ref, c_ref, o_ref):
        o_ref[...] = _reduced_block(g_ref, l1_ref, a_ref, b_ref, c_ref)

    return pl.pallas_call(
        body,
        grid_spec=pltpu.PrefetchScalarGridSpec(
            num_scalar_prefetch=1, grid=(rows // tm,), in_specs=_reduced_specs(tm, cols),
            out_specs=pl.BlockSpec((tm, cols), lambda i, idx: (i, 0))),
        out_shape=jax.ShapeDtypeStruct((rows, cols), F32),
        compiler_params=_params(32), name=name,
    )(idx, g, land1, land2, land2, land2)


def _adamw_math(w, g, m, v):
    m = ADAM_B1 * m + (1.0 - ADAM_B1) * g
    v = ADAM_B2 * v + (1.0 - ADAM_B2) * (g * g)
    m_hat = m / (1.0 - ADAM_B1 ** ADAM_STEP)
    v_hat = v / (1.0 - ADAM_B2 ** ADAM_STEP)
    delta = -ADAM_LR * (m_hat / (jnp.sqrt(v_hat) + ADAM_EPS) + ADAM_WD * w)
    return delta, m, v


def _allreduce_small_adamw(g, w, m, v, part_in):
    shape = g.shape
    ex = _GradExchange([tuple(part_in.shape[1:])])

    def body(g_ref, w_ref, m_ref, v_ref, part_ref, gs_ref, d_ref, nm_ref, nv_ref, land2_ref, sib, psum, land,
             send_sems, recv_sems, *ex_sems):
        ex.start(2, [part_ref], [land2_ref], ex_sems)
        x, y, c = _mesh_pos()
        sibling = (x, y, 1 - c)
        chips = _other_chips(x, y)
        first = pltpu.make_async_remote_copy(
            src_ref=g_ref, dst_ref=sib, send_sem=send_sems.at[0], recv_sem=recv_sems.at[0],
            device_id=sibling, device_id_type=MESH)
        first.start()
        first.wait_recv()
        psum[...] = g_ref[...] + sib[...]
        second = []
        for k, (cx, cy) in enumerate(chips):
            cp = pltpu.make_async_remote_copy(
                src_ref=psum, dst_ref=land.at[k], send_sem=send_sems.at[1 + k], recv_sem=recv_sems.at[1 + k],
                device_id=(cx, cy, c), device_id_type=MESH)
            cp.start()
            second.append(cp)
        for cp in second:
            cp.wait_recv()
        total = (psum[...] + land[0]) + (land[1] + land[2])
        gs_ref[...] = total
        delta, nm, nv = _adamw_math(w_ref[...], total, m_ref[...], v_ref[...])
        d_ref[...] = delta
        nm_ref[...] = nm
        nv_ref[...] = nv
        first.wait_send()
        for cp in second:
            cp.wait_send()
        ex.wait(2, [part_ref], [land2_ref], ex_sems)

    sd = jax.ShapeDtypeStruct(shape, F32)
    hbm_spec = pl.BlockSpec(memory_space=pl.ANY)
    return pl.pallas_call(
        body,
        out_shape=(sd, sd, sd, sd, ex.land_shapes(2)[0]),
        in_specs=[pl.BlockSpec(memory_space=pltpu.VMEM)] * 4 + [hbm_spec],
        out_specs=[pl.BlockSpec(memory_space=pltpu.VMEM)] * 4 + [hbm_spec],
        scratch_shapes=[pltpu.VMEM(shape, F32), pltpu.VMEM(shape, F32), pltpu.VMEM((3,) + shape, F32),
                        pltpu.SemaphoreType.DMA((4,)), pltpu.SemaphoreType.DMA((4,))] + ex.sem_shapes(2),
        compiler_params=pltpu.CompilerParams(vmem_limit_bytes=32 * MIB),
        name="allreduce_small_adamw",
    )(g, w, m, v, part_in)


def _adamw_reduced(w, m, v, g, land1, land2, idx, name):
    rows, cols = w.shape
    tm = min(rows, 256)

    def body(idx_ref, w_ref, m_ref, v_ref, g_ref, l1_ref, a_ref, b_ref, c_ref, go_ref, d_ref, nm_ref, nv_ref):
        gsum = _reduced_block(g_ref, l1_ref, a_ref, b_ref, c_ref)
        go_ref[...] = gsum
        delta, nm, nv = _adamw_math(w_ref[...], gsum, m_ref[...], v_ref[...])
        d_ref[...] = delta
        nm_ref[...] = nm
        nv_ref[...] = nv

    sd = jax.ShapeDtypeStruct((rows, cols), F32)
    spec = pl.BlockSpec((tm, cols), lambda i, idx: (i, 0))
    return pl.pallas_call(
        body,
        grid_spec=pltpu.PrefetchScalarGridSpec(
            num_scalar_prefetch=1, grid=(rows // tm,), in_specs=[spec] * 3 + _reduced_specs(tm, cols),
            out_specs=[spec] * 4),
        out_shape=(sd, sd, sd, sd),
        compiler_params=_params(32), name=name,
    )(idx, w, m, v, g, land1, land2, land2, land2)


def _adamw(w, g, m, v, name):
    rows, cols = w.shape
    tm = 256 if rows % 256 == 0 else rows

    def body(w_ref, g_ref, m_ref, v_ref, d_ref, nm_ref, nv_ref):
        delta, nm, nv = _adamw_math(w_ref[...], g_ref[...], m_ref[...], v_ref[...])
        d_ref[...] = delta
        nm_ref[...] = nm
        nv_ref[...] = nv

    sd = jax.ShapeDtypeStruct((rows, cols), F32)
    spec = _row_spec(tm, cols)
    return pl.pallas_call(
        body, grid=(rows // tm,), out_shape=(sd, sd, sd), in_specs=[spec] * 4, out_specs=[spec] * 3,
        compiler_params=_params(32), name=name,
    )(w, g, m, v)


def _fwd_in(x2, g1, wz, wf, wg):
    T = x2.shape[0]
    tm = TOKEN_TILE

    def body(x_ref, g_ref, wz_ref, wf_ref, wg_ref, xn_ref, zuv_ref, qkv_ref, fl_ref, gt_ref):
        x = x_ref[...]
        r, xh = _rms_stats(x)
        xn = (xh * g_ref[...]).astype(BF16)
        xn_ref[...] = xn
        zuv_ref[...] = _dot(xn, wz_ref[:, 0:1024]).astype(BF16)
        qkv_ref[:, 0:512] = (_dot(xn, wz_ref[:, 1024:1536]) * 0.125).astype(BF16)
        qkv_ref[:, 512:1536] = _dot(xn, wz_ref[:, 1536:2560]).astype(BF16)
        fl_ref[...] = _dot(xn, wf_ref[...])
        gt_ref[...] = jax.nn.sigmoid(_dot(xn, wg_ref[...])).astype(BF16)

    return pl.pallas_call(
        body, grid=(T // tm,),
        out_shape=(jax.ShapeDtypeStruct((T, D_MODEL), BF16), jax.ShapeDtypeStruct((T, 1024), BF16),
                   jax.ShapeDtypeStruct((T, 1536), BF16), jax.ShapeDtypeStruct((T, LANES), F32),
                   jax.ShapeDtypeStruct((T, 2048), BF16)),
        in_specs=[_row_spec(tm, D_MODEL), _const_spec((1, D_MODEL)), _const_spec((D_MODEL, ZQKV_WIDTH)),
                  _const_spec((D_MODEL, LANES)), _const_spec((D_MODEL, 2048))],
        out_specs=[_row_spec(tm, D_MODEL), _row_spec(tm, 1024), _row_spec(tm, 1536), _row_spec(tm, LANES),
                   _row_spec(tm, 2048)],
        compiler_params=_params(48), name="fwd_in",
    )(x2, g1, wz, wf, wg)


def _log_sigmoid(f):
    return jnp.minimum(f, 0.0) - jnp.log1p(jnp.exp(-jnp.abs(f)))


AUG_LANES = 6


def _split3(v):
    hi = v.astype(BF16)
    r1 = v - hi.astype(F32)
    mid = r1.astype(BF16)
    lo = (r1 - mid.astype(F32)).astype(BF16)
    return hi, mid, lo


def _spread(parts, k0):
    r, c = _iota((LANES, LANES), 0), _iota((LANES, LANES), 1)
    out = None
    for i, part in enumerate(parts):
        e = ((c == AUG_LANES * r + (k0 + i)) & (r < N_HEADS)).astype(BF16)
        term = _dot(part, e)
        out = term if out is None else out + term
    return out


def _aug_query(v):
    ones = (_iota(v.shape, 1) < N_HEADS).astype(BF16)
    return (_spread(_split3(v), 0) + _spread((ones, ones, ones), 3)).astype(BF16)


def _aug_key(v):
    ones = (_iota(v.shape, 1) < N_HEADS).astype(BF16)
    return (_spread((ones, ones, ones), 0) - _spread(_split3(v), 3)).astype(BF16)


def _aug_stack(t2, aug, p):
    lane = _iota(t2.shape, 1)
    low = lane < 64
    zero = jnp.zeros_like(t2)
    first = 2 * AUG_LANES * p
    a_e = jnp.where((lane >= first) & (lane < first + AUG_LANES), aug, zero)
    a_o = jnp.where((lane >= first + AUG_LANES) & (lane < first + 2 * AUG_LANES), aug, zero)
    top = jnp.concatenate([jnp.where(low, t2, zero), a_e], axis=1)
    bot = jnp.concatenate([jnp.where(low, zero, t2), a_o], axis=1)
    return jnp.concatenate([top, bot], axis=0)


def _fwd_cum(fl, bfp):
    T = fl.shape[0]
    tb = CUM_TILE

    def body(fl_ref, b_ref, cc_ref, qa_ref, ka_ref):
        tri = (_iota((tb, tb), 0) >= _iota((tb, tb), 1)).astype(F32)
        carry = jnp.zeros((1, LANES), F32)
        for i in range(T // tb):
            rows = slice(i * tb, (i + 1) * tb)
            lf = _log_sigmoid(fl_ref[rows, :] + b_ref[...])
            cs = jnp.dot(tri, lf, precision=HIGHEST, preferred_element_type=F32) + carry
            cc_ref[rows, :] = cs
            carry = cs[tb - 1:tb, :]
            qa_ref[rows, :] = _aug_query(cs)
            ka_ref[rows, :] = _aug_key(cs)

    return pl.pallas_call(
        body,
        out_shape=(jax.ShapeDtypeStruct((T, LANES), F32), jax.ShapeDtypeStruct((T, LANES), BF16),
                   jax.ShapeDtypeStruct((T, LANES), BF16)),
        compiler_params=pltpu.CompilerParams(vmem_limit_bytes=32 * MIB), name="fwd_cum",
    )(fl, bfp)


def _sgu_forward_parts(z, gs, bs):
    u = _gelu(z[:, :SGU_WIDTH])
    vv = _gelu(z[:, SGU_WIDTH:])
    vc = vv - jnp.mean(vv, axis=1, keepdims=True)
    rs = lax.rsqrt(jnp.mean(vc * vc, axis=1, keepdims=True) + EPS)
    vhat = vc * rs
    return u, vhat, rs, vhat * gs + bs


def _sgu_pair_weights(w_ref, bT, p):
    tril = _iota((CHUNK, CHUNK), 0) >= _iota((CHUNK, CHUNK), 1)
    we = jnp.where(tril, w_ref[2 * p], 0.0).astype(BF16)
    wo = jnp.where(tril, w_ref[2 * p + 1], 0.0).astype(BF16)
    lane8 = _iota(bT.shape, 1)
    low = _iota((CHUNK, LANES), 1) < 64
    b2 = jnp.where(low, _pick(bT, lane8, 2 * p), _pick(bT, lane8, 2 * p + 1))
    return we, wo, b2


def _chunks_on_lanes(v, p, nc):
    return jnp.concatenate([v[c * CHUNK:(c + 1) * CHUNK, LANES * p:LANES * (p + 1)] for c in range(nc)], axis=1)


def _sgu_mix(we, wo, b2, vcat, nc):
    low = (_iota((CHUNK, nc * LANES), 1) % LANES) < 64
    return jnp.where(low, _dot(we, vcat), _dot(wo, vcat)) + jnp.concatenate([b2] * nc, axis=1)


def _fwd_sgu(zuv, gs, bs, wsp, bT):
    T = zuv.shape[0]
    tc = SGU_TILE
    nc = tc // CHUNK

    def body(z_ref, gs_ref, bs_ref, w_ref, bT_ref, y_ref):
        u, _, _, vln = _sgu_forward_parts(z_ref[...].astype(F32), gs_ref[...], bs_ref[...])
        vb = vln.astype(BF16)
        for p in range(4):
            we, wo, b2 = _sgu_pair_weights(w_ref, bT_ref[...], p)
            s = _sgu_mix(we, wo, b2, _chunks_on_lanes(vb, p, nc), nc)
            for c in range(nc):
                rows, cols = slice(c * CHUNK, (c + 1) * CHUNK), slice(LANES * p, LANES * (p + 1))
                y_ref[rows, cols] = (u[rows, cols] * s[:, c * LANES:(c + 1) * LANES]).astype(BF16)

    return pl.pallas_call(
        body, grid=(T // tc,), out_shape=jax.ShapeDtypeStruct((T, SGU_WIDTH), BF16),
        in_specs=[_row_spec(tc, 1024), _const_spec((1, SGU_WIDTH)), _const_spec((1, SGU_WIDTH)),
                  _const_spec((8, CHUNK, CHUNK)), _const_spec((CHUNK, 8))],
        out_specs=_row_spec(tc, SGU_WIDTH),
        compiler_params=_params(40), name="fwd_sgu",
    )(zuv, gs, bs, wsp, bT)


def _fwd_attn(qkv, qaug, kaug, w_shards):
    T = qkv.shape[0]
    tq = tk = ATTN_TILE
    nq = T // tq
    gather = _WeightGather([w.shape for w in w_shards], ["cols", "cols", "rows", "cols", "rows"])
    nw = gather.n

    def body(q_ref, qa_ref, k_ref, v_ref, ka_ref, *rest):
        w_refs, (o_ref, lse_ref), wg_refs, scratch = rest[:nw], rest[nw:nw + 2], rest[nw + 2:2 * nw + 2], rest[2 * nw + 2:]
        i = pl.program_id(0)

        @pl.when(i == 0)
        def _():
            gather.start(w_refs, wg_refs, scratch)

        @pl.when(i == nq // 2)
        def _():
            gather.forward(wg_refs, scratch)

        lane = _iota((tq, LANES), 1)
        low = lane < 64
        lowk = _iota((tk, LANES), 1) < 64
        one = jnp.ones((tk, LANES), BF16)
        row = _iota((2 * tq, tk), 0) % tq
        col = _iota((2 * tq, tk), 1)
        cols = [slice(LANES * p, LANES * (p + 1)) for p in range(4)]
        qa = qa_ref[...]
        qs = [_aug_stack(q_ref[:, cols[p]], qa, p) for p in range(4)]

        def step(j, carry, masked):
            ks = pl.ds(pl.multiple_of(j * tk, tk), tk)
            ka = ka_ref[ks, :]
            new = []
            for p in range(4):
                m, acc_e, acc_o = carry[p]
                v2 = v_ref[ks, cols[p]]
                s = _dot_nt(qs[p], jnp.concatenate([k_ref[ks, cols[p]], ka], axis=1))
                if masked:
                    s = jnp.where(col <= row, s, NEG)
                mn = jnp.maximum(m, jnp.max(s, axis=1, keepdims=True))
                al = jnp.exp(m - mn)
                pm = jnp.exp(s - mn).astype(BF16)
                acc_e = al[:tq] * acc_e + _dot(pm[:tq], jnp.where(lowk, v2, one))
                acc_o = al[tq:] * acc_o + _dot(pm[tq:], jnp.where(lowk, one, v2))
                new.append((mn, acc_e, acc_o))
            return tuple(new)

        init = tuple((jnp.full((2 * tq, 1), NEG, F32), jnp.zeros((tq, LANES), F32), jnp.zeros((tq, LANES), F32))
                     for _ in range(4))
        carry = lax.fori_loop(0, i, lambda j, c: step(j, c, False), init)
        carry = step(i, carry, True)
        lse_blk = jnp.zeros((tq, LANES), F32)
        for p in range(4):
            m, acc_e, acc_o = carry[p]
            l_e = pltpu.roll(acc_e, 64, 1)
            l_o = pltpu.roll(acc_o, 64, 1)
            o_ref[:, cols[p]] = jnp.where(low, acc_e / l_e, acc_o / l_o).astype(BF16)
            lse_blk = jnp.where(lane == 2 * p, m[:tq] + jnp.log(l_e), lse_blk)
            lse_blk = jnp.where(lane == 2 * p + 1, m[tq:] + jnp.log(acc_o), lse_blk)
        lse_ref[...] = lse_blk

        @pl.when(i == nq - 1)
        def _():
            gather.finish(wg_refs, scratch)

    return pl.pallas_call(
        body, grid=(nq,),
        out_shape=[jax.ShapeDtypeStruct((T, ATTN_WIDTH), BF16), jax.ShapeDtypeStruct((T, LANES), F32)]
        + gather.out_shapes(),
        in_specs=[_row_spec(tq, 512), _row_spec(tq, LANES),
                  pl.BlockSpec((T, 512), lambda i: (0, 1), pipeline_mode=pl.Buffered(1)),
                  pl.BlockSpec((T, 512), lambda i: (0, 2), pipeline_mode=pl.Buffered(1)),
                  _const_spec((T, LANES))] + [_const_spec(w.shape) for w in w_shards],
        out_specs=[_row_spec(tq, ATTN_WIDTH), _row_spec(tq, LANES)] + [pl.BlockSpec(memory_space=pl.ANY)] * nw,
        scratch_shapes=gather.scratch_shapes(),
        compiler_params=_params(58), name="fwd_attn",
    )(qkv, qaug, qkv, qkv, kaug, *w_shards)


def _fwd_merge(ys, ya, gt, x2, wbs, wba, wo, g2):
    T = x2.shape[0]
    tm = TOKEN_TILE

    def body(ys_ref, ya_ref, gt_ref, x_ref, wbs_ref, wba_ref, wo_ref, g2_ref, a_ref, b_ref, mg_ref, o_ref, h1_ref):
        A = _dot(ys_ref[...], wbs_ref[...])
        B = _dot(ya_ref[...], wba_ref[...])
        mg = (gt_ref[:, :D_MODEL].astype(F32) * A + gt_ref[:, D_MODEL:].astype(F32) * B).astype(BF16)
        o = _dot(mg, wo_ref[...])
        r2, oh = _rms_stats(o)
        a_ref[...] = A.astype(BF16)
        b_ref[...] = B.astype(BF16)
        mg_ref[...] = mg
        o_ref[...] = o.astype(BF16)
        h1_ref[...] = x_ref[...] + oh * g2_ref[...]

    sd = jax.ShapeDtypeStruct((T, D_MODEL), BF16)
    return pl.pallas_call(
        body, grid=(T // tm,),
        out_shape=(sd, sd, sd, sd, jax.ShapeDtypeStruct((T, D_MODEL), F32)),
        in_specs=[_row_spec(tm, 512), _row_spec(tm, 512), _row_spec(tm, 2048), _row_spec(tm, D_MODEL),
                  _const_spec((512, D_MODEL)), _const_spec((512, D_MODEL)), _const_spec((D_MODEL, D_MODEL)),
                  _const_spec((1, D_MODEL))],
        out_specs=[_row_spec(tm, D_MODEL)] * 5,
        compiler_params=_params(40), name="fwd_merge",
    )(ys, ya, gt, x2, wbs, wba, wo, g2)


def _fwd_ffn_loss(h1, tgt, wup, wdn, g3, g4):
    T = h1.shape[0]
    tm = TOKEN_TILE
    nsteps = T // tm

    def body(h1_ref, tg_ref, wup_ref, wdn_ref, g3_ref, g4_ref, xn2_ref, a_ref, ddn_ref, dy_ref, loss_ref,
             dg4_ref, acc_l, acc_g):
        i = pl.program_id(0)

        @pl.when(i == 0)
        def _():
            acc_l[...] = jnp.zeros_like(acc_l)
            acc_g[...] = jnp.zeros_like(acc_g)

        h1v = h1_ref[...]
        r3, h1h = _rms_stats(h1v)
        xn2 = (h1h * g3_ref[...]).astype(BF16)
        xn2_ref[...] = xn2
        dn = jnp.zeros((tm, D_MODEL), F32)
        for j in range(D_FF // 1024):
            cols = slice(1024 * j, 1024 * (j + 1))
            a = _dot(xn2, wup_ref[:, cols])
            a_ref[:, cols] = a.astype(BF16)
            hid = jnp.square(jnp.maximum(a, 0.0)).astype(BF16)
            dn = dn + _dot(hid, wdn_ref[cols, :])
        r4, dnh = _rms_stats(dn)
        g4v = g4_ref[...]
        e = (h1v + dnh * g4v) - tg_ref[...]
        sq = e * e
        s1 = sq[:, 0:LANES]
        for j in range(1, D_MODEL // LANES):
            s1 = s1 + sq[:, LANES * j:LANES * (j + 1)]
        acc_l[...] += _fold8(s1)
        dy = e * (1.0 / D_MODEL)
        dy_ref[...] = dy
        acc_g[...] += _fold8(dy * dnh)
        ddn_ref[...] = _rms_bwd(dy, dnh, r4, g4v).astype(BF16)

        @pl.when(i == nsteps - 1)
        def _():
            loss_ref[...] = acc_l[...] * (0.5 / D_MODEL)
            dg4_ref[...] = jnp.sum(acc_g[...], axis=0, keepdims=True)

    return pl.pallas_call(
        body, grid=(nsteps,),
        out_shape=(jax.ShapeDtypeStruct((T, D_MODEL), BF16), jax.ShapeDtypeStruct((T, D_FF), BF16),
                   jax.ShapeDtypeStruct((T, D_MODEL), BF16), jax.ShapeDtypeStruct((T, D_MODEL), F32),
                   jax.ShapeDtypeStruct((SUBLANES, LANES), F32), jax.ShapeDtypeStruct((1, D_MODEL), F32)),
        in_specs=[_row_spec(tm, D_MODEL), _row_spec(tm, D_MODEL), _const_spec((D_MODEL, D_FF)),
                  _const_spec((D_FF, D_MODEL)), _const_spec((1, D_MODEL)), _const_spec((1, D_MODEL))],
        out_specs=[_row_spec(tm, D_MODEL), _row_spec(tm, D_FF), _row_spec(tm, D_MODEL), _row_spec(tm, D_MODEL),
                   pl.BlockSpec((SUBLANES, LANES), lambda i: (0, 0)), pl.BlockSpec((1, D_MODEL), lambda i: (0, 0))],
        scratch_shapes=[pltpu.VMEM((SUBLANES, LANES), F32), pltpu.VMEM((SUBLANES, D_MODEL), F32)],
        compiler_params=_params(52), name="fwd_ffn_loss",
    )(h1, tgt, wup, wdn, g3, g4)


def _bwd_ffn(ddn, a, dy, h1, wup, wdn, g3):
    T = h1.shape[0]
    tm = TOKEN_TILE
    nsteps = T // tm

    def body(ddn_ref, a_ref, dy_ref, h1_ref, wup_ref, wdn_ref, g3_ref, da_ref, dh1_ref, dg3_ref, acc_g):
        i = pl.program_id(0)

        @pl.when(i == 0)
        def _():
            acc_g[...] = jnp.zeros_like(acc_g)

        ddnv = ddn_ref[...]
        dxn2 = jnp.zeros((tm, D_MODEL), F32)
        for j in range(D_FF // 1024):
            cols = slice(1024 * j, 1024 * (j + 1))
            dhid = _dot_nt(ddnv, wdn_ref[cols, :])
            da = (dhid * (2.0 * jnp.maximum(a_ref[:, cols].astype(F32), 0.0))).astype(BF16)
            da_ref[:, cols] = da
            dxn2 = dxn2 + _dot_nt(da, wup_ref[:, cols])
        r3, h1h = _rms_stats(h1_ref[...])
        acc_g[...] += _fold8(dxn2 * h1h)
        dh1_ref[...] = dy_ref[...] + _rms_bwd(dxn2, h1h, r3, g3_ref[...])

        @pl.when(i == nsteps - 1)
        def _():
            dg3_ref[...] = jnp.sum(acc_g[...], axis=0, keepdims=True)

    return pl.pallas_call(
        body, grid=(nsteps,),
        out_shape=(jax.ShapeDtypeStruct((T, D_FF), BF16), jax.ShapeDtypeStruct((T, D_MODEL), F32),
                   jax.ShapeDtypeStruct((1, D_MODEL), F32)),
        in_specs=[_row_spec(tm, D_MODEL), _row_spec(tm, D_FF), _row_spec(tm, D_MODEL), _row_spec(tm, D_MODEL),
                  _const_spec((D_MODEL, D_FF)), _const_spec((D_FF, D_MODEL)), _const_spec((1, D_MODEL))],
        out_specs=[_row_spec(tm, D_FF), _row_spec(tm, D_MODEL), pl.BlockSpec((1, D_MODEL), lambda i: (0, 0))],
        scratch_shapes=[pltpu.VMEM((SUBLANES, D_MODEL), F32)],
        compiler_params=_params(52), name="bwd_ffn",
    )(ddn, a, dy, h1, wup, wdn, g3)


def _wgrad(xa, dy, name, relu2=False, tn=None, blocked=False):
    T, K = xa.shape
    N = dy.shape[1]
    tn = N if tn is None else tn
    tt = WGRAD_TILE
    if blocked:
        out_shape = jax.ShapeDtypeStruct((N // tn, K, tn), F32)
        out_spec = pl.BlockSpec((None, K, tn), lambda n, t: (n, 0, 0))
    else:
        out_shape = jax.ShapeDtypeStruct((K, N), F32)
        out_spec = pl.BlockSpec((K, tn), lambda n, t: (0, n))

    def body(x_ref, dy_ref, o_ref):
        @pl.when(pl.program_id(1) == 0)
        def _():
            o_ref[...] = jnp.zeros_like(o_ref)

        xv = x_ref[...]
        if relu2:
            xv = jnp.square(jnp.maximum(xv.astype(F32), 0.0)).astype(BF16)
        o_ref[...] += _dot_tn(xv, dy_ref[...])

    return pl.pallas_call(
        body, grid=(N // tn, T // tt), out_shape=out_shape,
        in_specs=[pl.BlockSpec((tt, K), lambda n, t: (t, 0)), pl.BlockSpec((tt, tn), lambda n, t: (t, n))],
        out_specs=out_spec,
        compiler_params=_params(52, 2), name=name,
    )(xa, dy)


def _bwd_merge(dh1, o, A, B, gt, ya, lse, cc, wbs, wba, wo, g2):
    T = dh1.shape[0]
    tm = TOKEN_TILE
    nsteps = T // tm

    def body(dh1_ref, o_ref, a_ref, b_ref, gt_ref, ya_ref, lse_ref, cc_ref, wbs_ref, wba_ref, wo_ref, g2_ref,
             do_ref, da_ref, db_ref, dgl_ref, dys_ref, dya_ref, qab_ref, dab_ref, dg2_ref, acc_g):
        i = pl.program_id(0)

        @pl.when(i == 0)
        def _():
            acc_g[...] = jnp.zeros_like(acc_g)

        dh1v = dh1_ref[...]
        r2, oh = _rms_stats(o_ref[...].astype(F32))
        acc_g[...] += _fold8(dh1v * oh)
        do = _rms_bwd(dh1v, oh, r2, g2_ref[...]).astype(BF16)
        do_ref[...] = do
        dmg = _dot_nt(do, wo_ref[...])
        ga = gt_ref[:, :D_MODEL].astype(F32)
        gb = gt_ref[:, D_MODEL:].astype(F32)
        dgl_ref[:, :D_MODEL] = (dmg * a_ref[...].astype(F32) * ga * (1.0 - ga)).astype(BF16)
        dgl_ref[:, D_MODEL:] = (dmg * b_ref[...].astype(F32) * gb * (1.0 - gb)).astype(BF16)
        dA = (dmg * ga).astype(BF16)
        dB = (dmg * gb).astype(BF16)
        da_ref[...] = dA
        db_ref[...] = dB
        dys_ref[...] = _dot_nt(dA, wbs_ref[...]).astype(BF16)
        dya = _dot_nt(dB, wba_ref[...]).astype(BF16)
        dya_ref[...] = dya
        prod = dya.astype(F32) * ya_ref[...].astype(F32)
        lane = _iota((tm, LANES), 1)
        low = lane < 64
        blk = jnp.zeros((tm, LANES), F32)
        for p in range(4):
            pp = prod[:, LANES * p:LANES * (p + 1)]
            blk = jnp.where(lane == 2 * p, jnp.sum(jnp.where(low, pp, 0.0), axis=1, keepdims=True), blk)
            blk = jnp.where(lane == 2 * p + 1, jnp.sum(jnp.where(low, 0.0, pp), axis=1, keepdims=True), blk)
        qab_ref[...] = _aug_query(cc_ref[...] - lse_ref[...])
        dab_ref[...] = _spread(_split3(-blk), 0).astype(BF16)

        @pl.when(i == nsteps - 1)
        def _():
            dg2_ref[...] = jnp.sum(acc_g[...], axis=0, keepdims=True)

    sd = jax.ShapeDtypeStruct((T, D_MODEL), BF16)
    sh = jax.ShapeDtypeStruct((T, 512), BF16)
    sa = jax.ShapeDtypeStruct((T, LANES), BF16)
    return pl.pallas_call(
        body, grid=(nsteps,),
        out_shape=(sd, sd, sd, jax.ShapeDtypeStruct((T, 2048), BF16), sh, sh, sa, sa,
                   jax.ShapeDtypeStruct((1, D_MODEL), F32)),
        in_specs=[_row_spec(tm, D_MODEL)] * 4 + [_row_spec(tm, 2048), _row_spec(tm, 512), _row_spec(tm, LANES),
                  _row_spec(tm, LANES), _const_spec((512, D_MODEL)), _const_spec((512, D_MODEL)),
                  _const_spec((D_MODEL, D_MODEL)), _const_spec((1, D_MODEL))],
        out_specs=[_row_spec(tm, D_MODEL)] * 3 + [_row_spec(tm, 2048), _row_spec(tm, 512), _row_spec(tm, 512),
                   _row_spec(tm, LANES), _row_spec(tm, LANES), pl.BlockSpec((1, D_MODEL), lambda i: (0, 0))],
        scratch_shapes=[pltpu.VMEM((SUBLANES, D_MODEL), F32)],
        compiler_params=_params(40), name="bwd_merge",
    )(dh1, o, A, B, gt, ya, lse, cc, wbs, wba, wo, g2)


def _bwd_sgu(zuv, dys, gs, bs, wsp, bT, grads):
    T = zuv.shape[0]
    tc = SGU_TILE
    nc = tc // CHUNK
    nsteps = T // tc
    ex = _GradExchange([tuple(g.shape[1:]) for g in grads])
    ng = ex.n

    def body(z_ref, dy_ref, gs_ref, bs_ref, w_ref, bT_ref, *rest):
        g_refs, (dz_ref, dw_ref, dbT_ref, dgs_ref, dbs_ref) = rest[:ng], rest[ng:ng + 5]
        land1 = rest[ng + 5:2 * ng + 5]
        acc_w, acc_b, acc_gs, acc_bs, dvln_s = rest[2 * ng + 5:2 * ng + 10]
        ex_sems = rest[2 * ng + 10:]
        i = pl.program_id(0)

        @pl.when(i == 0)
        def _():
            ex.start(1, g_refs, land1, ex_sems)
            acc_w[...] = jnp.zeros_like(acc_w)
            acc_b[...] = jnp.zeros_like(acc_b)
            acc_gs[...] = jnp.zeros_like(acc_gs)
            acc_bs[...] = jnp.zeros_like(acc_bs)

        z = z_ref[...].astype(F32)
        gsv = gs_ref[...]
        u, vhat, rs, vln = _sgu_forward_parts(z, gsv, bs_ref[...])
        vb = vln.astype(BF16)
        dy = dy_ref[...].astype(F32)
        low_w = (_iota((CHUNK, nc * LANES), 1) % LANES) < 64
        for p in range(4):
            we, wo, b2 = _sgu_pair_weights(w_ref, bT_ref[...], p)
            vcat = _chunks_on_lanes(vb, p, nc)
            s = _sgu_mix(we, wo, b2, vcat, nc)
            dyc = _chunks_on_lanes(dy, p, nc)
            ds = dyc * _chunks_on_lanes(u, p, nc)
            dsb = ds.astype(BF16)
            zero = jnp.zeros_like(dsb)
            dse = jnp.where(low_w, dsb, zero)
            dso = jnp.where(low_w, zero, dsb)
            acc_w[2 * p] += _dot_nt(dse, vcat)
            acc_w[2 * p + 1] += _dot_nt(dso, vcat)
            acc_b[p] += ds
            dvl = jnp.where(low_w, _dot_tn(we, dsb), _dot_tn(wo, dsb))
            for c in range(nc):
                rows, cols = slice(c * CHUNK, (c + 1) * CHUNK), slice(LANES * p, LANES * (p + 1))
                dvln_s[rows, cols] = dvl[:, c * LANES:(c + 1) * LANES]
                du = dy[rows, cols] * s[:, c * LANES:(c + 1) * LANES]
                dz_ref[rows, cols] = (du * _gelu_grad(z[rows, cols])).astype(BF16)
        dvln = dvln_s[...]
        acc_gs[...] += _fold8(dvln * vhat)
        acc_bs[...] += _fold8(dvln)
        al = dvln * gsv
        dvv = rs * (al - jnp.mean(al, axis=1, keepdims=True) - vhat * jnp.mean(al * vhat, axis=1, keepdims=True))
        dz_ref[:, SGU_WIDTH:] = (dvv * _gelu_grad(z[:, SGU_WIDTH:])).astype(BF16)

        @pl.when(i == nsteps - 1)
        def _():
            tril = _iota((CHUNK, CHUNK), 0) >= _iota((CHUNK, CHUNK), 1)
            lane = _iota((CHUNK, LANES), 1)
            low = lane < 64
            blk = jnp.zeros((CHUNK, LANES), F32)
            for g in range(8):
                dw_ref[g] = jnp.where(tril, acc_w[g], 0.0)
            for p in range(4):
                t = acc_b[p]
                tot = t[:, 0:LANES]
                for c in range(1, nc):
                    tot = tot + t[:, c * LANES:(c + 1) * LANES]
                blk = jnp.where(lane == 2 * p, jnp.sum(jnp.where(low, tot, 0.0), axis=1, keepdims=True), blk)
                blk = jnp.where(lane == 2 * p + 1, jnp.sum(jnp.where(low, 0.0, tot), axis=1, keepdims=True), blk)
            dbT_ref[...] = blk
            dgs_ref[...] = jnp.sum(acc_gs[...], axis=0, keepdims=True)
            dbs_ref[...] = jnp.sum(acc_bs[...], axis=0, keepdims=True)
            ex.wait(1, g_refs, land1, ex_sems)

    whole = lambda shape: pl.BlockSpec(shape, lambda i: (0,) * len(shape))
    hbm_spec = pl.BlockSpec(memory_space=pl.ANY)
    return pl.pallas_call(
        body, grid=(nsteps,),
        out_shape=[jax.ShapeDtypeStruct((T, 1024), BF16), jax.ShapeDtypeStruct((8, CHUNK, CHUNK), F32),
                   jax.ShapeDtypeStruct((CHUNK, LANES), F32), jax.ShapeDtypeStruct((1, SGU_WIDTH), F32),
                   jax.ShapeDtypeStruct((1, SGU_WIDTH), F32)] + ex.land_shapes(1),
        in_specs=[_row_spec(tc, 1024), _row_spec(tc, SGU_WIDTH), _const_spec((1, SGU_WIDTH)),
                  _const_spec((1, SGU_WIDTH)), _const_spec((8, CHUNK, CHUNK)), _const_spec((CHUNK, 8))]
        + [hbm_spec] * ng,
        out_specs=[_row_spec(tc, 1024), whole((8, CHUNK, CHUNK)), whole((CHUNK, LANES)),
                   whole((1, SGU_WIDTH)), whole((1, SGU_WIDTH))] + [hbm_spec] * ng,
        scratch_shapes=[pltpu.VMEM((8, CHUNK, CHUNK), F32), pltpu.VMEM((4, CHUNK, nc * LANES), F32),
                        pltpu.VMEM((SUBLANES, SGU_WIDTH), F32), pltpu.VMEM((SUBLANES, SGU_WIDTH), F32),
                        pltpu.VMEM((tc, SGU_WIDTH), F32)] + ex.sem_shapes(1),
        compiler_params=_params(48), name="bwd_sgu",
    )(zuv, dys, gs, bs, wsp, bT, *grads)


def _bwd_attn(qkv, dya, qab, dab, kaug, parts):
    T = qkv.shape[0]
    tq = tk = ATTN_TILE
    nq = T // tq
    nk = T // tk
    ex = _GradExchange([tuple(g.shape[1:]) for g in parts])
    nr = ex.n

    def body(q_ref, do_ref, qa_ref, da_ref, k_ref, v_ref, ka_ref, *rest):
        part_refs, (dq_ref, dk_ref, dv_ref, dcx_ref) = rest[:nr], rest[nr:nr + 4]
        land2, dq_acc, ex_sems = rest[nr + 4:2 * nr + 4], rest[2 * nr + 4], rest[2 * nr + 5:]
        p = pl.program_id(0)
        j = pl.program_id(1)

        @pl.when((p == 0) & (j == 0))
        def _():
            ex.start(2, part_refs, land2, ex_sems)

        lane = _iota((tq, LANES), 1)
        low = lane < 64
        row = _iota((2 * tq, tk), 0) % tq
        col = _iota((2 * tq, tk), 1)
        first = 2 * AUG_LANES * p

        @pl.when(j == 0)
        def _():
            dq_acc[...] = jnp.zeros_like(dq_acc)

        @pl.when((j == 0) & (p == 0))
        def _():
            dcx_ref[...] = jnp.zeros_like(dcx_ref)

        ka = ka_ref[...]
        kk = jnp.concatenate([k_ref[...], ka], axis=1)
        vv = jnp.concatenate([v_ref[...], ka], axis=1)

        def q_block(i, carry, masked):
            dk_a, dv_a = carry
            qsl = pl.ds(pl.multiple_of(i * tq, tq), tq)
            qs = _aug_stack(q_ref[qsl, :], qa_ref[qsl, :], p)
            dos = _aug_stack(do_ref[qsl, :], da_ref[qsl, :], p)
            s = _dot_nt(qs, kk)
            if masked:
                s = jnp.where(col <= row, s, NEG)
            pm = jnp.exp(s)
            ds = pm * _dot_nt(dos, vv)
            dsb = ds.astype(BF16)
            dv_a = dv_a + _dot_tn(pm.astype(BF16), dos[:, :LANES])
            dk_a = dk_a + _dot_tn(dsb, qs)
            dqx = _dot(dsb, kk)
            dq_acc[qsl, :] += jnp.where(low, dqx[:tq, :LANES], dqx[tq:, :LANES])
            dcx_ref[qsl, :] += (jnp.where(lane == first, dqx[:tq, LANES:], 0.0)
                                + jnp.where(lane == first + AUG_LANES, dqx[tq:, LANES:], 0.0))
            return dk_a, dv_a

        init = (jnp.zeros((tk, 2 * LANES), F32), jnp.zeros((tk, LANES), F32))
        carry = q_block(j, init, True)
        dk_a, dv_a = lax.fori_loop(j + 1, nq, lambda i, c: q_block(i, c, False), carry)
        dk_ref[...] = dk_a[:, :LANES].astype(BF16)
        dv_ref[...] = dv_a.astype(BF16)
        ksl = pl.ds(pl.multiple_of(j * tk, tk), tk)
        lk = _iota((tk, LANES), 1)
        dcx_ref[ksl, :] += jnp.where((lk == first + 3) | (lk == first + AUG_LANES + 3), dk_a[:, LANES:], 0.0)

        @pl.when(j == nk - 1)
        def _():
            dq_ref[...] = (dq_acc[...] * 0.125).astype(BF16)

        @pl.when((p == 3) & (j == nk - 1))
        def _():
            ex.wait(2, part_refs, land2, ex_sems)

    sh = jax.ShapeDtypeStruct((T, ATTN_WIDTH), BF16)
    full = lambda cb: pl.BlockSpec((T, LANES), lambda p, j: (0, cb + p))
    blk = lambda cb: pl.BlockSpec((tk, LANES), lambda p, j: (j, cb + p))
    hbm_spec = pl.BlockSpec(memory_space=pl.ANY)
    return pl.pallas_call(
        body, grid=(4, nk),
        out_shape=[sh, sh, sh, jax.ShapeDtypeStruct((T, LANES), F32)] + ex.land_shapes(2),
        in_specs=[full(0), full(0), _const_spec((T, LANES)), _const_spec((T, LANES)), blk(4), blk(8),
                  pl.BlockSpec((tk, LANES), lambda p, j: (j, 0))] + [hbm_spec] * nr,
        out_specs=[full(0), blk(0), blk(0), pl.BlockSpec((T, LANES), lambda p, j: (0, 0))] + [hbm_spec] * nr,
        scratch_shapes=[pltpu.VMEM((T, LANES), F32)] + ex.sem_shapes(2),
        compiler_params=_params(58, 2), name="bwd_attn",
    )(qkv, dya, qab, dab, qkv, qkv, kaug, *parts)


def _bwd_cum(dcx, fl, bfp):
    T = fl.shape[0]
    tb = CUM_TILE

    def body(dcx_ref, fl_ref, b_ref, dfl_ref, dbf_ref):
        triu = (_iota((tb, tb), 0) <= _iota((tb, tb), 1)).astype(F32)
        r, c = _iota((LANES, LANES), 0), _iota((LANES, LANES), 1)
        sel = (((r == AUG_LANES * c) & (c < N_HEADS)).astype(F32)
               - ((r == AUG_LANES * c + 3) & (c < N_HEADS)).astype(F32))
        carry = jnp.zeros((1, LANES), F32)
        dbf = jnp.zeros((1, LANES), F32)
        for i in reversed(range(T // tb)):
            colblk = jnp.dot(dcx_ref[i * tb:(i + 1) * tb, :], sel, precision=HIGHEST, preferred_element_type=F32)
            rc = jnp.dot(triu, colblk, precision=HIGHEST, preferred_element_type=F32) + carry
            carry = rc[0:1, :]
            sig = jax.nn.sigmoid(fl_ref[i * tb:(i + 1) * tb, :] + b_ref[...])
            dfl = rc * (1.0 - sig)
            dfl_ref[i * tb:(i + 1) * tb, :] = dfl.astype(BF16)
            dbf = dbf + jnp.sum(dfl, axis=0, keepdims=True)
        dbf_ref[...] = dbf

    return pl.pallas_call(
        body,
        out_shape=(jax.ShapeDtypeStruct((T, LANES), BF16), jax.ShapeDtypeStruct((1, LANES), F32)),
        compiler_params=pltpu.CompilerParams(vmem_limit_bytes=32 * MIB), name="bwd_cum",
    )(dcx, fl, bfp)


def _bwd_in(dz, dq, dk, dv, dfl, dgl, dh1, x2, g1, wz, wf, wg, grad_in):
    T = x2.shape[0]
    tm = TOKEN_TILE
    nsteps = T // tm
    ex = _GradExchange([tuple(grad_in.shape[1:])])

    def body(dz_ref, dq_ref, dk_ref, dv_ref, dfl_ref, dgl_ref, dh1_ref, x_ref, g_ref, wz_ref, wf_ref, wg_ref,
             gin_ref, dx_ref, dg1_ref, land1_ref, acc_g, *ex_sems):
        i = pl.program_id(0)

        @pl.when(i == 0)
        def _():
            ex.start(1, [gin_ref], [land1_ref], ex_sems)
            acc_g[...] = jnp.zeros_like(acc_g)

        dxn = _dot_nt(dz_ref[...], wz_ref[:, 0:1024])
        dxn = dxn + _dot_nt(dq_ref[...], wz_ref[:, 1024:1536])
        dxn = dxn + _dot_nt(dk_ref[...], wz_ref[:, 1536:2048])
        dxn = dxn + _dot_nt(dv_ref[...], wz_ref[:, 2048:2560])
        dxn = dxn + _dot_nt(dfl_ref[...], wf_ref[...])
        dxn = dxn + _dot_nt(dgl_ref[...], wg_ref[...])
        r1, xh = _rms_stats(x_ref[...])
        acc_g[...] += _fold8(dxn * xh)
        dx_ref[...] = dh1_ref[...] + _rms_bwd(dxn, xh, r1, g_ref[...])

        @pl.when(i == nsteps - 1)
        def _():
            dg1_ref[...] = jnp.sum(acc_g[...], axis=0, keepdims=True)
            ex.wait(1, [gin_ref], [land1_ref], ex_sems)

    hbm_spec = pl.BlockSpec(memory_space=pl.ANY)
    return pl.pallas_call(
        body, grid=(nsteps,),
        out_shape=[jax.ShapeDtypeStruct((T, D_MODEL), F32), jax.ShapeDtypeStruct((1, D_MODEL), F32)]
        + ex.land_shapes(1),
        in_specs=[_row_spec(tm, 1024), _row_spec(tm, 512), _row_spec(tm, 512), _row_spec(tm, 512),
                  _row_spec(tm, LANES), _row_spec(tm, 2048), _row_spec(tm, D_MODEL), _row_spec(tm, D_MODEL),
                  _const_spec((1, D_MODEL)), _const_spec((D_MODEL, ZQKV_WIDTH)), _const_spec((D_MODEL, LANES)),
                  _const_spec((D_MODEL, 2048)), hbm_spec],
        out_specs=[_row_spec(tm, D_MODEL), pl.BlockSpec((1, D_MODEL), lambda i: (0, 0)), hbm_spec],
        scratch_shapes=[pltpu.VMEM((SUBLANES, D_MODEL), F32)] + ex.sem_shapes(1),
        compiler_params=_params(48), name="bwd_in",
    )(dz, dq, dk, dv, dfl, dgl, dh1, x2, g1, wz, wf, wg, grad_in)


def _pad_rows(v, rows):
    return jnp.pad(v, ((0, rows - v.shape[0]), (0, 0)))


def _pack_small(g_mix_pre, b_forget, g_sgu, b_sgu, w_spatial, b_spatial, g_mix_post, g_ffn_pre, g_ffn_post):
    vec = lambda v: _pad_rows(v.reshape(-1, LANES), SUBLANES)
    return jnp.concatenate([
        w_spatial.reshape(-1, LANES), vec(g_mix_pre), _pad_rows(jnp.pad(b_forget, ((0, 0), (0, LANES - N_HEADS))), SUBLANES),
        vec(g_sgu), vec(b_sgu), vec(b_spatial), vec(g_mix_post), vec(g_ffn_pre), vec(g_ffn_post)], axis=0)


def _unpack_small(p):
    nw = N_HEADS * CHUNK * CHUNK // LANES
    blk = lambda k: p[nw + SUBLANES * k: nw + SUBLANES * (k + 1)]
    return dict(
        w_spatial=p[:nw].reshape(1, N_HEADS, CHUNK, CHUNK),
        g_mix_pre=blk(0).reshape(1, D_MODEL), b_forget=blk(1)[0:1, :N_HEADS],
        g_sgu=blk(2)[:4].reshape(1, SGU_WIDTH), b_sgu=blk(3)[:4].reshape(1, SGU_WIDTH),
        b_spatial=blk(4).reshape(1, N_HEADS, CHUNK), g_mix_post=blk(5).reshape(1, D_MODEL),
        g_ffn_pre=blk(6).reshape(1, D_MODEL), g_ffn_post=blk(7).reshape(1, D_MODEL))


def kernel(x, g_mix_pre, w_in, b_forget, g_sgu, b_sgu, w_spatial, b_spatial, w_branch_sgu, w_branch_attn, w_out, g_mix_post, g_ffn_pre, w_up, w_down, g_ffn_post, loss_target, m_g_mix_pre, m_w_in, m_b_forget, m_g_sgu, m_b_sgu, m_w_spatial, m_b_spatial, m_w_branch_sgu, m_w_branch_attn, m_w_out, m_g_mix_post, m_g_ffn_pre, m_w_up, m_w_down, m_g_ffn_post, v_g_mix_pre, v_w_in, v_b_forget, v_g_sgu, v_b_sgu, v_w_spatial, v_b_spatial, v_w_branch_sgu, v_w_branch_attn, v_w_out, v_g_mix_post, v_g_ffn_pre, v_w_up, v_w_down, v_g_ffn_post):
    T = x.shape[1]
    x2 = x.reshape(T, D_MODEL)
    tgt = loss_target.reshape(T, D_MODEL)

    wg_in = _gather_w_in(w_in[0])
    w_in_full = jnp.concatenate([wg_in[j, :, :IN_SHARD] for j in range(N_DEV)], axis=1)
    wz = w_in_full[:, :ZQKV_WIDTH]
    wf = jnp.pad(w_in_full[:, ZQKV_WIDTH:GATE_OFFSET], ((0, 0), (0, LANES - N_HEADS)))
    wgt = w_in_full[:, GATE_OFFSET:]
    bfp = jnp.pad(b_forget, ((0, 0), (0, LANES - N_HEADS)))
    wsp = w_spatial[0]
    bT = b_spatial[0].T

    xn, zuv, qkv, fl, gt = _fwd_in(x2, g_mix_pre, wz, wf, wgt)
    cc, qaug, kaug = _fwd_cum(fl, bfp)
    ys = _fwd_sgu(zuv, g_sgu, b_sgu, wsp, bT)
    ya, lse, wbs, wba, wo, wup, wdn = _fwd_attn(
        qkv, qaug, kaug, (w_branch_sgu[0], w_branch_attn[0], w_out[0], w_up[0], w_down[0]))
    A, B, mg, o, h1 = _fwd_merge(ys, ya, gt, x2, wbs, wba, wo, g_mix_post)
    xn2, a, ddn, dy, loss_part, dg4 = _fwd_ffn_loss(h1, tgt, wup, wdn, g_ffn_pre, g_ffn_post)
    loss = lax.psum(jnp.sum(loss_part), ("x", "y", "c"))

    da, dh1, dg3 = _bwd_ffn(ddn, a, dy, h1, wup, wdn, g_ffn_pre)
    dw_up = _wgrad(xn2, da, "wgrad_up", tn=512, blocked=True)
    dw_down = _wgrad(a, ddn, "wgrad_down", relu2=True)
    do, dA, dB, dgl, dys, dya, qab, dab, dg2 = _bwd_merge(dh1, o, A, B, gt, ya, lse, cc, wbs, wba, wo, g_mix_post)
    dw_out = _wgrad(mg, do, "wgrad_out")
    dw_bs = _wgrad(ys, dA, "wgrad_branch_sgu")
    dw_ba = _wgrad(ya, dB, "wgrad_branch_attn")
    col_blocks = lambda g, w: g.reshape(g.shape[0], N_DEV, w).transpose(1, 0, 2)
    row_blocks = lambda g, r: g.reshape(N_DEV, r, g.shape[1])
    early_names = ["w_branch_sgu", "w_branch_attn", "w_out", "w_up", "w_down"]
    early = [col_blocks(dw_bs, 128), col_blocks(dw_ba, 128), row_blocks(dw_out, 128), dw_up, row_blocks(dw_down, 512)]
    owners = _owner_indices()
    dzuv, dwsp, dbT, dgs, dbs, *early_land1 = _bwd_sgu(zuv, dys, g_sgu, b_sgu, wsp, bT, early)
    early_parts = [_chip_partials(g, l1, owners, "chip_partials_" + nm)
                   for g, l1, nm in zip(early, early_land1, early_names)]
    dq, dk, dv, dcx, *early_land2 = _bwd_attn(qkv, dya, qab, dab, kaug, early_parts)
    dfl, dbf = _bwd_cum(dcx, fl, bfp)
    dw_z = _wgrad(xn, dzuv, "wgrad_in_z")
    dw_q = _wgrad(xn, dq, "wgrad_in_q")
    dw_k = _wgrad(xn, dk, "wgrad_in_k")
    dw_v = _wgrad(xn, dv, "wgrad_in_v")
    dw_f = _wgrad(xn, dfl, "wgrad_in_f")
    dw_g = _wgrad(xn, dgl, "wgrad_in_gate")
    dw_in = jnp.concatenate([dw_z, dw_q, dw_k, dw_v, dw_f[:, :N_HEADS], dw_g], axis=1)
    blocks_in = jnp.stack([jnp.pad(dw_in[:, IN_SHARD * j:IN_SHARD * (j + 1)], ((0, 0), (0, IN_SHARD_PAD - IN_SHARD)))
                           for j in range(N_DEV)])
    dx, dg1, land1_in = _bwd_in(dzuv, dq, dk, dv, dfl, dgl, dh1, x2, g_mix_pre, wz, wf, wgt, blocks_in)
    part_in = _chip_partials(blocks_in, land1_in, owners, "chip_partials_w_in")

    small_g = _pack_small(dg1, dbf[:, :N_HEADS], dgs, dbs, dwsp[None], dbT[:, :N_HEADS].T[None], dg2, dg3, dg4)
    small_w = _pack_small(g_mix_pre, b_forget, g_sgu, b_sgu, w_spatial, b_spatial, g_mix_post, g_ffn_pre, g_ffn_post)
    small_m = _pack_small(m_g_mix_pre, m_b_forget, m_g_sgu, m_b_sgu, m_w_spatial, m_b_spatial, m_g_mix_post,
                          m_g_ffn_pre, m_g_ffn_post)
    small_v = _pack_small(v_g_mix_pre, v_b_forget, v_g_sgu, v_b_sgu, v_w_spatial, v_b_spatial, v_g_mix_post,
                          v_g_ffn_pre, v_g_ffn_post)
    *small_out, land2_in = _allreduce_small_adamw(small_g, small_w, small_m, small_v, part_in)
    sg, sd, sm, sv = (_unpack_small(t) for t in small_out)

    big = {}
    g_in = _reduced_grad(blocks_in, land1_in, land2_in, owners, "reduced_grad_w_in")[:, :IN_SHARD]
    d_, m_, v_ = _adamw(w_in[0], g_in, m_w_in[0], v_w_in[0], "adamw_w_in")
    big["w_in"] = (g_in[None], d_[None], m_[None], v_[None])
    early_wmv = [(w_branch_sgu, m_w_branch_sgu, v_w_branch_sgu), (w_branch_attn, m_w_branch_attn, v_w_branch_attn),
                 (w_out, m_w_out, v_w_out), (w_up, m_w_up, v_w_up), (w_down, m_w_down, v_w_down)]
    for nm, (w, m, v), g, l1, l2 in zip(early_names, early_wmv, early, early_land1, early_land2):
        big[nm] = tuple(t[None] for t in _adamw_reduced(w[0], m[0], v[0], g, l1, l2, owners, "adamw_" + nm))

    order = ["g_mix_pre", "w_in", "b_forget", "g_sgu", "b_sgu", "w_spatial", "b_spatial", "w_branch_sgu",
             "w_branch_attn", "w_out", "g_mix_post", "g_ffn_pre", "w_up", "w_down", "g_ffn_post"]
    outs = [loss, dx.reshape(1, T, D_MODEL)]
    for kind, small in enumerate((sg, sd, sm, sv)):
        outs += [big[nm][kind] if nm in big else small[nm] for nm in order]
    return tuple(outs)
```

```python
import jax
import jax.numpy as jnp
from jax import lax
from jax.experimental import pallas as pl
from jax.experimental.pallas import tpu as pltpu

F32 = jnp.float32
BF16 = jnp.bfloat16
HIGHEST = lax.Precision.HIGHEST
MESH = pl.DeviceIdType.MESH

D_MODEL = 1024
SGU_WIDTH = 512
ATTN_WIDTH = 512
N_HEADS = 8
CHUNK = 128
D_FF = 4096
IN_WIDTH = 4616
N_DEV = 8
IN_SHARD = IN_WIDTH // N_DEV
IN_SHARD_PAD = 640
ZQKV_WIDTH = 2 * SGU_WIDTH + 3 * ATTN_WIDTH
GATE_OFFSET = ZQKV_WIDTH + N_HEADS
EPS = 1e-6
LANES = 128
SUBLANES = 8
VMEM_BYTES = 64 * 1024 * 1024
MIB = 1024 * 1024

ADAM_LR = 0.001
ADAM_B1 = 0.9
ADAM_B2 = 0.999
ADAM_EPS = 1e-08
ADAM_WD = 0.01
ADAM_STEP = 10

TOKEN_TILE = 256
ATTN_TILE = 512
CUM_TILE = 256
SGU_TILE = 512
WGRAD_TILE = 512
NEG = -1e30

NT_DIMS = (((1,), (1,)), ((), ()))
TN_DIMS = (((0,), (0,)), ((), ()))


def _params(vmem_mb, n_grid=1):
    return pltpu.CompilerParams(
        dimension_semantics=("arbitrary",) * n_grid,
        vmem_limit_bytes=min(vmem_mb * MIB, VMEM_BYTES - 6 * MIB),
    )


def _dot(a, b):
    return jnp.dot(a, b, preferred_element_type=F32)


def _dot_nt(a, b):
    return lax.dot_general(a, b, NT_DIMS, preferred_element_type=F32)


def _dot_tn(a, b):
    return lax.dot_general(a, b, TN_DIMS, preferred_element_type=F32)


def _const_spec(shape):
    nd = len(shape)
    return pl.BlockSpec(shape, lambda *_: (0,) * nd, pipeline_mode=pl.Buffered(1))


def _row_spec(tm, n, col=0):
    return pl.BlockSpec((tm, n), lambda i: (i, col))


def _fold8(v):
    return v.reshape(v.shape[0] // SUBLANES, SUBLANES, v.shape[1]).sum(axis=0)


def _pick(v, lane_iota, k):
    return jnp.sum(jnp.where(lane_iota == k, v, 0.0), axis=1, keepdims=True)


def _iota(shape, dim):
    return lax.broadcasted_iota(jnp.int32, shape, dim)


def _gelu(x):
    c = 0.7978845608028654
    return 0.5 * x * (1.0 + jnp.tanh(c * (x + 0.044715 * x * x * x)))


def _gelu_grad(x):
    c = 0.7978845608028654
    t = jnp.tanh(c * (x + 0.044715 * x * x * x))
    return 0.5 * (1.0 + t) + 0.5 * x * (1.0 - t * t) * (c * (1.0 + 3.0 * 0.044715 * x * x))


def _rms_stats(v):
    r = lax.rsqrt(jnp.mean(v * v, axis=1, keepdims=True) + EPS)
    return r, v * r


def _rms_bwd(dout, vhat, r, g):
    a = dout * g
    return r * (a - vhat * jnp.mean(a * vhat, axis=1, keepdims=True))


def _mesh_pos():
    return lax.axis_index("x"), lax.axis_index("y"), lax.axis_index("c")


def _dev_index(px, py, pc):
    return 4 * px + 2 * py + pc


def _other_chips(x, y):
    return [(1 - x, y), (x, 1 - y), (1 - x, 1 - y)]


class _WeightGather:
    def __init__(self, shard_shapes, kinds, stage_shapes=None):
        self.shard_shapes = list(shard_shapes)
        self.kinds = list(kinds)
        self.stage_shapes = list(stage_shapes or shard_shapes)
        self.n = len(self.kinds)

    def out_shapes(self):
        shapes = []
        for (rows, cols), kind in zip(self.stage_shapes, self.kinds):
            full = {"block": (N_DEV, rows, cols), "rows": (N_DEV * rows, cols), "cols": (rows, N_DEV * cols)}[kind]
            shapes.append(jax.ShapeDtypeStruct(full, BF16))
        return shapes

    def scratch_shapes(self):
        return ([pltpu.VMEM(s, BF16) for s in self.stage_shapes]
                + [pltpu.SemaphoreType.DMA((self.n, 7)), pltpu.SemaphoreType.DMA((self.n, 7)),
                   pltpu.SemaphoreType.DMA((self.n,))])

    def _view(self, a, ref, j):
        rows, cols = self.stage_shapes[a]
        if self.kinds[a] == "block":
            return ref.at[j]
        if self.kinds[a] == "rows":
            return ref.at[pl.ds(pl.multiple_of(j * rows, rows), rows), :]
        return ref.at[:, pl.ds(pl.multiple_of(j * cols, cols), cols)]

    def _copy(self, outs, scratch, a, k, block, to, from_stage=False):
        stage, (send_sems, recv_sems, _) = scratch[:self.n], scratch[self.n:]
        dst = self._view(a, outs[a], _dev_index(*block))
        return pltpu.make_async_remote_copy(
            src_ref=stage[a] if from_stage else dst, dst_ref=dst,
            send_sem=send_sems.at[a, k], recv_sem=recv_sems.at[a, k],
            device_id=to, device_id_type=MESH)

    def _local(self, outs, scratch, a, me):
        return pltpu.make_async_copy(scratch[a], self._view(a, outs[a], _dev_index(*me)), scratch[-1].at[a])

    def start(self, ins, outs, scratch):
        x, y, c = _mesh_pos()
        me, sibling = (x, y, c), (x, y, 1 - c)
        for a in range(self.n):
            rows, cols = self.shard_shapes[a]
            if self.stage_shapes[a] != self.shard_shapes[a]:
                scratch[a][...] = jnp.zeros(self.stage_shapes[a], BF16)
            scratch[a][0:rows, 0:cols] = ins[a][...].astype(BF16)
            self._local(outs, scratch, a, me).start()
        for a in range(self.n):
            self._copy(outs, scratch, a, 0, me, sibling, True).start()
            for j, chip in enumerate(_other_chips(x, y)):
                self._copy(outs, scratch, a, 1 + j, me, (*chip, c), True).start()

    def forward(self, outs, scratch):
        x, y, c = _mesh_pos()
        me, sibling = (x, y, c), (x, y, 1 - c)
        for a in range(self.n):
            for j, chip in enumerate(_other_chips(x, y)):
                self._copy(outs, scratch, a, 1 + j, (*chip, c), me).wait_recv()
                self._copy(outs, scratch, a, 4 + j, (*chip, c), sibling).start()

    def finish(self, outs, scratch):
        x, y, c = _mesh_pos()
        me, sibling = (x, y, c), (x, y, 1 - c)
        chips = _other_chips(x, y)
        for a in range(self.n):
            self._copy(outs, scratch, a, 0, sibling, me).wait_recv()
            for j, chip in enumerate(chips):
                self._copy(outs, scratch, a, 4 + j, (*chip, 1 - c), me).wait_recv()
        for a in range(self.n):
            self._copy(outs, scratch, a, 0, me, sibling, True).wait_send()
            for j, chip in enumerate(chips):
                self._copy(outs, scratch, a, 1 + j, me, (*chip, c), True).wait_send()
                self._copy(outs, scratch, a, 4 + j, (*chip, c), sibling).wait_send()
            self._local(outs, scratch, a, me).wait()


def _gather_w_in(w_in_local):
    g = _WeightGather([(D_MODEL, IN_SHARD)], ["block"], [(D_MODEL, IN_SHARD_PAD)])

    def body(w_ref, out_ref, *scratch):
        g.start([w_ref], [out_ref], scratch)
        g.forward([out_ref], scratch)
        g.finish([out_ref], scratch)

    return pl.pallas_call(
        body,
        out_shape=g.out_shapes()[0],
        in_specs=[pl.BlockSpec(memory_space=pltpu.VMEM)],
        out_specs=pl.BlockSpec(memory_space=pl.ANY),
        scratch_shapes=g.scratch_shapes(),
        compiler_params=pltpu.CompilerParams(vmem_limit_bytes=32 * MIB),
        name="gather_w_in",
    )(w_in_local)


class _GradExchange:
    def __init__(self, shapes):
        self.shapes = [tuple(s) for s in shapes]
        self.n = len(self.shapes)

    def land_shapes(self, stage):
        slots, dtype = (4, F32) if stage == 1 else (3, BF16)
        return [jax.ShapeDtypeStruct((slots,) + s, dtype) for s in self.shapes]

    def sem_shapes(self, stage):
        slots = 4 if stage == 1 else 3
        return [pltpu.SemaphoreType.DMA((self.n, slots)), pltpu.SemaphoreType.DMA((self.n, slots))]

    def _copy(self, stage, srcs, lands, sems, a, k):
        x, y, c = _mesh_pos()
        cx, cy = (_other_chips(x, y) + [(x, y)])[k]
        if stage == 1:
            src, to = srcs[a].at[_dev_index(cx, cy, 1 - c)], (x, y, 1 - c)
        else:
            src, to = srcs[a].at[k], (cx, cy, c)
        return pltpu.make_async_remote_copy(
            src_ref=src, dst_ref=lands[a].at[k], send_sem=sems[0].at[a, k], recv_sem=sems[1].at[a, k],
            device_id=to, device_id_type=MESH)

    def start(self, stage, srcs, lands, sems):
        for a in range(self.n):
            for k in range(4 if stage == 1 else 3):
                self._copy(stage, srcs, lands, sems, a, k).start()

    def wait(self, stage, srcs, lands, sems):
        for a in range(self.n):
            for k in range(4 if stage == 1 else 3):
                cp = self._copy(stage, srcs, lands, sems, a, k)
                cp.wait_recv()
                cp.wait_send()


def _owner_indices():
    x, y, c = _mesh_pos()
    return jnp.stack([_dev_index(cx, cy, c) for cx, cy in _other_chips(x, y) + [(x, y)]]).astype(jnp.int32)


def _chip_partials(g, land1, idx, name):
    _, rows, cols = g.shape
    tr = min(rows, 256)

    def body(idx_ref, g_ref, l_ref, o_ref):
        o_ref[...] = (g_ref[...] + l_ref[...]).astype(BF16)

    return pl.pallas_call(
        body,
        grid_spec=pltpu.PrefetchScalarGridSpec(
            num_scalar_prefetch=1, grid=(3, rows // tr),
            in_specs=[pl.BlockSpec((None, tr, cols), lambda k, r, idx: (idx[k], r, 0)),
                      pl.BlockSpec((None, tr, cols), lambda k, r, idx: (k, r, 0))],
            out_specs=pl.BlockSpec((None, tr, cols), lambda k, r, idx: (k, r, 0))),
        out_shape=jax.ShapeDtypeStruct((3, rows, cols), BF16),
        compiler_params=_params(32, 2), name=name,
    )(idx, g, land1)


def _reduced_block(g_ref, l1_ref, a_ref, b_ref, c_ref):
    return ((g_ref[...] + l1_ref[...]) + a_ref[...].astype(F32)) + b_ref[...].astype(F32) + c_ref[...].astype(F32)


def _reduced_specs(tm, cols):
    return [pl.BlockSpec((None, tm, cols), lambda i, idx: (idx[3], i, 0)),
            pl.BlockSpec((None, tm, cols), lambda i, idx: (3, i, 0)),
            pl.BlockSpec((None, tm, cols), lambda i, idx: (0, i, 0)),
            pl.BlockSpec((None, tm, cols), lambda i, idx: (1, i, 0)),
            pl.BlockSpec((None, tm, cols), lambda i, idx: (2, i, 0))]


def _reduced_grad(g, land1, land2, idx, name):
    _, rows, cols = g.shape
    tm = min(rows, 256)

    def body(idx_ref, g_ref, l1_ref, a_ref, b_ref, c_ref, o_ref):
        o_ref[...] = _reduced_block(g_ref, l1_ref, a_ref, b_ref, c_ref)

    return pl.pallas_call(
        body,
        grid_spec=pltpu.PrefetchScalarGridSpec(
            num_scalar_prefetch=1, grid=(rows // tm,), in_specs=_reduced_specs(tm, cols),
            out_specs=pl.BlockSpec((tm, cols), lambda i, idx: (i, 0))),
        out_shape=jax.ShapeDtypeStruct((rows, cols), F32),
        compiler_params=_params(32), name=name,
    )(idx, g, land1, land2, land2, land2)


def _adamw_math(w, g, m, v):
    m = ADAM_B1 * m + (1.0 - ADAM_B1) * g
    v = ADAM_B2 * v + (1.0 - ADAM_B2) * (g * g)
    m_hat = m / (1.0 - ADAM_B1 ** ADAM_STEP)
    v_hat = v / (1.0 - ADAM_B2 ** ADAM_STEP)
    delta = -ADAM_LR * (m_hat / (jnp.sqrt(v_hat) + ADAM_EPS) + ADAM_WD * w)
    return delta, m, v


def _allreduce_small_adamw(g, w, m, v, part_in):
    shape = g.shape
    ex = _GradExchange([tuple(part_in.shape[1:])])

    def body(g_ref, w_ref, m_ref, v_ref, part_ref, gs_ref, d_ref, nm_ref, nv_ref, land2_ref, sib, psum, land,
             send_sems, recv_sems, *ex_sems):
        ex.start(2, [part_ref], [land2_ref], ex_sems)
        x, y, c = _mesh_pos()
        sibling = (x, y, 1 - c)
        chips = _other_chips(x, y)
        first = pltpu.make_async_remote_copy(
            src_ref=g_ref, dst_ref=sib, send_sem=send_sems.at[0], recv_sem=recv_sems.at[0],
            device_id=sibling, device_id_type=MESH)
        first.start()
        first.wait_recv()
        psum[...] = g_ref[...] + sib[...]
        second = []
        for k, (cx, cy) in enumerate(chips):
            cp = pltpu.make_async_remote_copy(
                src_ref=psum, dst_ref=land.at[k], send_sem=send_sems.at[1 + k], recv_sem=recv_sems.at[1 + k],
                device_id=(cx, cy, c), device_id_type=MESH)
            cp.start()
            second.append(cp)
        for cp in second:
            cp.wait_recv()
        total = (psum[...] + land[0]) + (land[1] + land[2])
        gs_ref[...] = total
        delta, nm, nv = _adamw_math(w_ref[...], total, m_ref[...], v_ref[...])
        d_ref[...] = delta
        nm_ref[...] = nm
        nv_ref[...] = nv
        first.wait_send()
        for cp in second:
            cp.wait_send()
        ex.wait(2, [part_ref], [land2_ref], ex_sems)

    sd = jax.ShapeDtypeStruct(shape, F32)
    hbm_spec = pl.BlockSpec(memory_space=pl.ANY)
    return pl.pallas_call(
        body,
        out_shape=(sd, sd, sd, sd, ex.land_shapes(2)[0]),
        in_specs=[pl.BlockSpec(memory_space=pltpu.VMEM)] * 4 + [hbm_spec],
        out_specs=[pl.BlockSpec(memory_space=pltpu.VMEM)] * 4 + [hbm_spec],
        scratch_shapes=[pltpu.VMEM(shape, F32), pltpu.VMEM(shape, F32), pltpu.VMEM((3,) + shape, F32),
                        pltpu.SemaphoreType.DMA((4,)), pltpu.SemaphoreType.DMA((4,))] + ex.sem_shapes(2),
        compiler_params=pltpu.CompilerParams(vmem_limit_bytes=32 * MIB),
        name="allreduce_small_adamw",
    )(g, w, m, v, part_in)


def _adamw_reduced(w, m, v, g, land1, land2, idx, name):
    rows, cols = w.shape
    tm = min(rows, 256)

    def body(idx_ref, w_ref, m_ref, v_ref, g_ref, l1_ref, a_ref, b_ref, c_ref, go_ref, d_ref, nm_ref, nv_ref):
        gsum = _reduced_block(g_ref, l1_ref, a_ref, b_ref, c_ref)
        go_ref[...] = gsum
        delta, nm, nv = _adamw_math(w_ref[...], gsum, m_ref[...], v_ref[...])
        d_ref[...] = delta
        nm_ref[...] = nm
        nv_ref[...] = nv

    sd = jax.ShapeDtypeStruct((rows, cols), F32)
    spec = pl.BlockSpec((tm, cols), lambda i, idx: (i, 0))
    return pl.pallas_call(
        body,
        grid_spec=pltpu.PrefetchScalarGridSpec(
            num_scalar_prefetch=1, grid=(rows // tm,), in_specs=[spec] * 3 + _reduced_specs(tm, cols),
            out_specs=[spec] * 4),
        out_shape=(sd, sd, sd, sd),
        compiler_params=_params(32), name=name,
    )(idx, w, m, v, g, land1, land2, land2, land2)


def _adamw(w, g, m, v, name):
    rows, cols = w.shape
    tm = 256 if rows % 256 == 0 else rows

    def body(w_ref, g_ref, m_ref, v_ref, d_ref, nm_ref, nv_ref):
        delta, nm, nv = _adamw_math(w_ref[...], g_ref[...], m_ref[...], v_ref[...])
        d_ref[...] = delta
        nm_ref[...] = nm
        nv_ref[...] = nv

    sd = jax.ShapeDtypeStruct((rows, cols), F32)
    spec = _row_spec(tm, cols)
    return pl.pallas_call(
        body, grid=(rows // tm,), out_shape=(sd, sd, sd), in_specs=[spec] * 4, out_specs=[spec] * 3,
        compiler_params=_params(32), name=name,
    )(w, g, m, v)


def _fwd_in(x2, g1, wz, wf, wg):
    T = x2.shape[0]
    tm = TOKEN_TILE

    def body(x_ref, g_ref, wz_ref, wf_ref, wg_ref, xn_ref, zuv_ref, qkv_ref, fl_ref, gt_ref):
        x = x_ref[...]
        r, xh = _rms_stats(x)
        xn = (xh * g_ref[...]).astype(BF16)
        xn_ref[...] = xn
        zuv_ref[...] = _dot(xn, wz_ref[:, 0:1024]).astype(BF16)
        qkv_ref[:, 0:512] = (_dot(xn, wz_ref[:, 1024:1536]) * 0.125).astype(BF16)
        qkv_ref[:, 512:1536] = _dot(xn, wz_ref[:, 1536:2560]).astype(BF16)
        fl_ref[...] = _dot(xn, wf_ref[...])
        gt_ref[...] = jax.nn.sigmoid(_dot(xn, wg_ref[...])).astype(BF16)

    return pl.pallas_call(
        body, grid=(T // tm,),
        out_shape=(jax.ShapeDtypeStruct((T, D_MODEL), BF16), jax.ShapeDtypeStruct((T, 1024), BF16),
                   jax.ShapeDtypeStruct((T, 1536), BF16), jax.ShapeDtypeStruct((T, LANES), F32),
                   jax.ShapeDtypeStruct((T, 2048), BF16)),
        in_specs=[_row_spec(tm, D_MODEL), _const_spec((1, D_MODEL)), _const_spec((D_MODEL, ZQKV_WIDTH)),
                  _const_spec((D_MODEL, LANES)), _const_spec((D_MODEL, 2048))],
        out_specs=[_row_spec(tm, D_MODEL), _row_spec(tm, 1024), _row_spec(tm, 1536), _row_spec(tm, LANES),
                   _row_spec(tm, 2048)],
        compiler_params=_params(48), name="fwd_in",
    )(x2, g1, wz, wf, wg)


def _log_sigmoid(f):
    return jnp.minimum(f, 0.0) - jnp.log1p(jnp.exp(-jnp.abs(f)))


AUG_LANES = 6


def _split3(v):
    hi = v.astype(BF16)
    r1 = v - hi.astype(F32)
    mid = r1.astype(BF16)
    lo = (r1 - mid.astype(F32)).astype(BF16)
    return hi, mid, lo


def _spread(parts, k0):
    r, c = _iota((LANES, LANES), 0), _iota((LANES, LANES), 1)
    out = None
    for i, part in enumerate(parts):
        e = ((c == AUG_LANES * r + (k0 + i)) & (r < N_HEADS)).astype(BF16)
        term = _dot(part, e)
        out = term if out is None else out + term
    return out


def _aug_query(v):
    ones = (_iota(v.shape, 1) < N_HEADS).astype(BF16)
    return (_spread(_split3(v), 0) + _spread((ones, ones, ones), 3)).astype(BF16)


def _aug_key(v):
    ones = (_iota(v.shape, 1) < N_HEADS).astype(BF16)
    return (_spread((ones, ones, ones), 0) - _spread(_split3(v), 3)).astype(BF16)


def _aug_stack(t2, aug, p):
    lane = _iota(t2.shape, 1)
    low = lane < 64
    zero = jnp.zeros_like(t2)
    first = 2 * AUG_LANES * p
    a_e = jnp.where((lane >= first) & (lane < first + AUG_LANES), aug, zero)
    a_o = jnp.where((lane >= first + AUG_LANES) & (lane < first + 2 * AUG_LANES), aug, zero)
    top = jnp.concatenate([jnp.where(low, t2, zero), a_e], axis=1)
    bot = jnp.concatenate([jnp.where(low, zero, t2), a_o], axis=1)
    return jnp.concatenate([top, bot], axis=0)


def _fwd_cum(fl, bfp):
    T = fl.shape[0]
    tb = CUM_TILE

    def body(fl_ref, b_ref, cc_ref, qa_ref, ka_ref):
        tri = (_iota((tb, tb), 0) >= _iota((tb, tb), 1)).astype(F32)
        carry = jnp.zeros((1, LANES), F32)
        for i in range(T // tb):
            rows = slice(i * tb, (i + 1) * tb)
            lf = _log_sigmoid(fl_ref[rows, :] + b_ref[...])
            cs = jnp.dot(tri, lf, precision=HIGHEST, preferred_element_type=F32) + carry
            cc_ref[rows, :] = cs
            carry = cs[tb - 1:tb, :]
            qa_ref[rows, :] = _aug_query(cs)
            ka_ref[rows, :] = _aug_key(cs)

    return pl.pallas_call(
        body,
        out_shape=(jax.ShapeDtypeStruct((T, LANES), F32), jax.ShapeDtypeStruct((T, LANES), BF16),
                   jax.ShapeDtypeStruct((T, LANES), BF16)),
        compiler_params=pltpu.CompilerParams(vmem_limit_bytes=32 * MIB), name="fwd_cum",
    )(fl, bfp)


def _sgu_forward_parts(z, gs, bs):
    u = _gelu(z[:, :SGU_WIDTH])
    vv = _gelu(z[:, SGU_WIDTH:])
    vc = vv - jnp.mean(vv, axis=1, keepdims=True)
    rs = lax.rsqrt(jnp.mean(vc * vc, axis=1, keepdims=True) + EPS)
    vhat = vc * rs
    return u, vhat, rs, vhat * gs + bs


def _sgu_pair_weights(w_ref, bT, p):
    tril = _iota((CHUNK, CHUNK), 0) >= _iota((CHUNK, CHUNK), 1)
    we = jnp.where(tril, w_ref[2 * p], 0.0).astype(BF16)
    wo = jnp.where(tril, w_ref[2 * p + 1], 0.0).astype(BF16)
    lane8 = _iota(bT.shape, 1)
    low = _iota((CHUNK, LANES), 1) < 64
    b2 = jnp.where(low, _pick(bT, lane8, 2 * p), _pick(bT, lane8, 2 * p + 1))
    return we, wo, b2


def _chunks_on_lanes(v, p, nc):
    return jnp.concatenate([v[c * CHUNK:(c + 1) * CHUNK, LANES * p:LANES * (p + 1)] for c in range(nc)], axis=1)


def _sgu_mix(we, wo, b2, vcat, nc):
    low = (_iota((CHUNK, nc * LANES), 1) % LANES) < 64
    return jnp.where(low, _dot(we, vcat), _dot(wo, vcat)) + jnp.concatenate([b2] * nc, axis=1)


def _fwd_sgu(zuv, gs, bs, wsp, bT):
    T = zuv.shape[0]
    tc = SGU_TILE
    nc = tc // CHUNK

    def body(z_ref, gs_ref, bs_ref, w_ref, bT_ref, y_ref):
        u, _, _, vln = _sgu_forward_parts(z_ref[...].astype(F32), gs_ref[...], bs_ref[...])
        vb = vln.astype(BF16)
        for p in range(4):
            we, wo, b2 = _sgu_pair_weights(w_ref, bT_ref[...], p)
            s = _sgu_mix(we, wo, b2, _chunks_on_lanes(vb, p, nc), nc)
            for c in range(nc):
                rows, cols = slice(c * CHUNK, (c + 1) * CHUNK), slice(LANES * p, LANES * (p + 1))
                y_ref[rows, cols] = (u[rows, cols] * s[:, c * LANES:(c + 1) * LANES]).astype(BF16)

    return pl.pallas_call(
        body, grid=(T // tc,), out_shape=jax.ShapeDtypeStruct((T, SGU_WIDTH), BF16),
        in_specs=[_row_spec(tc, 1024), _const_spec((1, SGU_WIDTH)), _const_spec((1, SGU_WIDTH)),
                  _const_spec((8, CHUNK, CHUNK)), _const_spec((CHUNK, 8))],
        out_specs=_row_spec(tc, SGU_WIDTH),
        compiler_params=_params(40), name="fwd_sgu",
    )(zuv, gs, bs, wsp, bT)


def _fwd_attn(qkv, qaug, kaug, w_shards):
    T = qkv.shape[0]
    tq = tk = ATTN_TILE
    nq = T // tq
    gather = _WeightGather([w.shape for w in w_shards], ["cols", "cols", "rows", "cols", "rows"])
    nw = gather.n

    def body(q_ref, qa_ref, k_ref, v_ref, ka_ref, *rest):
        w_refs, (o_ref, lse_ref), wg_refs, scratch = rest[:nw], rest[nw:nw + 2], rest[nw + 2:2 * nw + 2], rest[2 * nw + 2:]
        i = pl.program_id(0)

        @pl.when(i == 0)
        def _():
            gather.start(w_refs, wg_refs, scratch)

        @pl.when(i == nq // 2)
        def _():
            gather.forward(wg_refs, scratch)

        lane = _iota((tq, LANES), 1)
        low = lane < 64
        lowk = _iota((tk, LANES), 1) < 64
        one = jnp.ones((tk, LANES), BF16)
        row = _iota((2 * tq, tk), 0) % tq
        col = _iota((2 * tq, tk), 1)
        cols = [slice(LANES * p, LANES * (p + 1)) for p in range(4)]
        qa = qa_ref[...]
        qs = [_aug_stack(q_ref[:, cols[p]], qa, p) for p in range(4)]

        def step(j, carry, masked):
            ks = pl.ds(pl.multiple_of(j * tk, tk), tk)
            ka = ka_ref[ks, :]
            new = []
            for p in range(4):
                m, acc_e, acc_o = carry[p]
                v2 = v_ref[ks, cols[p]]
                s = _dot_nt(qs[p], jnp.concatenate([k_ref[ks, cols[p]], ka], axis=1))
                if masked:
                    s = jnp.where(col <= row, s, NEG)
                mn = jnp.maximum(m, jnp.max(s, axis=1, keepdims=True))
                al = jnp.exp(m - mn)
                pm = jnp.exp(s - mn).astype(BF16)
                acc_e = al[:tq] * acc_e + _dot(pm[:tq], jnp.where(lowk, v2, one))
                acc_o = al[tq:] * acc_o + _dot(pm[tq:], jnp.where(lowk, one, v2))
                new.append((mn, acc_e, acc_o))
            return tuple(new)

        init = tuple((jnp.full((2 * tq, 1), NEG, F32), jnp.zeros((tq, LANES), F32), jnp.zeros((tq, LANES), F32))
                     for _ in range(4))
        carry = lax.fori_loop(0, i, lambda j, c: step(j, c, False), init)
        carry = step(i, carry, True)
        lse_blk = jnp.zeros((tq, LANES), F32)
        for p in range(4):
            m, acc_e, acc_o = carry[p]
            l_e = pltpu.roll(acc_e, 64, 1)
            l_o = pltpu.roll(acc_o, 64, 1)
            o_ref[:, cols[p]] = jnp.where(low, acc_e / l_e, acc_o / l_o).astype(BF16)
            lse_blk = jnp.where(lane == 2 * p, m[:tq] + jnp.log(l_e), lse_blk)
            lse_blk = jnp.where(lane == 2 * p + 1, m[tq:] + jnp.log(acc_o), lse_blk)
        lse_ref[...] = lse_blk

        @pl.when(i == nq - 1)
        def _():
            gather.finish(wg_refs, scratch)

    return pl.pallas_call(
        body, grid=(nq,),
        out_shape=[jax.ShapeDtypeStruct((T, ATTN_WIDTH), BF16), jax.ShapeDtypeStruct((T, LANES), F32)]
        + gather.out_shapes(),
        in_specs=[_row_spec(tq, 512), _row_spec(tq, LANES),
                  pl.BlockSpec((T, 512), lambda i: (0, 1), pipeline_mode=pl.Buffered(1)),
                  pl.BlockSpec((T, 512), lambda i: (0, 2), pipeline_mode=pl.Buffered(1)),
                  _const_spec((T, LANES))] + [_const_spec(w.shape) for w in w_shards],
        out_specs=[_row_spec(tq, ATTN_WIDTH), _row_spec(tq, LANES)] + [pl.BlockSpec(memory_space=pl.ANY)] * nw,
        scratch_shapes=gather.scratch_shapes(),
        compiler_params=_params(58), name="fwd_attn",
    )(qkv, qaug, qkv, qkv, kaug, *w_shards)


def _fwd_merge(ys, ya, gt, x2, wbs, wba, wo, g2):
    T = x2.shape[0]
    tm = TOKEN_TILE

    def body(ys_ref, ya_ref, gt_ref, x_ref, wbs_ref, wba_ref, wo_ref, g2_ref, a_ref, b_ref, mg_ref, o_ref, h1_ref):
        A = _dot(ys_ref[...], wbs_ref[...])
        B = _dot(ya_ref[...], wba_ref[...])
        mg = (gt_ref[:, :D_MODEL].astype(F32) * A + gt_ref[:, D_MODEL:].astype(F32) * B).astype(BF16)
        o = _dot(mg, wo_ref[...])
        r2, oh = _rms_stats(o)
        a_ref[...] = A.astype(BF16)
        b_ref[...] = B.astype(BF16)
        mg_ref[...] = mg
        o_ref[...] = o.astype(BF16)
        h1_ref[...] = x_ref[...] + oh * g2_ref[...]

    sd = jax.ShapeDtypeStruct((T, D_MODEL), BF16)
    return pl.pallas_call(
        body, grid=(T // tm,),
        out_shape=(sd, sd, sd, sd, jax.ShapeDtypeStruct((T, D_MODEL), F32)),
        in_specs=[_row_spec(tm, 512), _row_spec(tm, 512), _row_spec(tm, 2048), _row_spec(tm, D_MODEL),
                  _const_spec((512, D_MODEL)), _const_spec((512, D_MODEL)), _const_spec((D_MODEL, D_MODEL)),
                  _const_spec((1, D_MODEL))],
        out_specs=[_row_spec(tm, D_MODEL)] * 5,
        compiler_params=_params(40), name="fwd_merge",
    )(ys, ya, gt, x2, wbs, wba, wo, g2)


def _fwd_ffn_loss(h1, tgt, wup, wdn, g3, g4):
    T = h1.shape[0]
    tm = TOKEN_TILE
    nsteps = T // tm

    def body(h1_ref, tg_ref, wup_ref, wdn_ref, g3_ref, g4_ref, xn2_ref, a_ref, ddn_ref, dy_ref, loss_ref,
             dg4_ref, acc_l, acc_g):
        i = pl.program_id(0)

        @pl.when(i == 0)
        def _():
            acc_l[...] = jnp.zeros_like(acc_l)
            acc_g[...] = jnp.zeros_like(acc_g)

        h1v = h1_ref[...]
        r3, h1h = _rms_stats(h1v)
        xn2 = (h1h * g3_ref[...]).astype(BF16)
        xn2_ref[...] = xn2
        dn = jnp.zeros((tm, D_MODEL), F32)
        for j in range(D_FF // 1024):
            cols = slice(1024 * j, 1024 * (j + 1))
            a = _dot(xn2, wup_ref[:, cols])
            a_ref[:, cols] = a.astype(BF16)
            hid = jnp.square(jnp.maximum(a, 0.0)).astype(BF16)
            dn = dn + _dot(hid, wdn_ref[cols, :])
        r4, dnh = _rms_stats(dn)
        g4v = g4_ref[...]
        e = (h1v + dnh * g4v) - tg_ref[...]
        sq = e * e
        s1 = sq[:, 0:LANES]
        for j in range(1, D_MODEL // LANES):
            s1 = s1 + sq[:, LANES * j:LANES * (j + 1)]
        acc_l[...] += _fold8(s1)
        dy = e * (1.0 / D_MODEL)
        dy_ref[...] = dy
        acc_g[...] += _fold8(dy * dnh)
        ddn_ref[...] = _rms_bwd(dy, dnh, r4, g4v).astype(BF16)

        @pl.when(i == nsteps - 1)
        def _():
            loss_ref[...] = acc_l[...] * (0.5 / D_MODEL)
            dg4_ref[...] = jnp.sum(acc_g[...], axis=0, keepdims=True)

    return pl.pallas_call(
        body, grid=(nsteps,),
        out_shape=(jax.ShapeDtypeStruct((T, D_MODEL), BF16), jax.ShapeDtypeStruct((T, D_FF), BF16),
                   jax.ShapeDtypeStruct((T, D_MODEL), BF16), jax.ShapeDtypeStruct((T, D_MODEL), F32),
                   jax.ShapeDtypeStruct((SUBLANES, LANES), F32), jax.ShapeDtypeStruct((1, D_MODEL), F32)),
        in_specs=[_row_spec(tm, D_MODEL), _row_spec(tm, D_MODEL), _const_spec((D_MODEL, D_FF)),
                  _const_spec((D_FF, D_MODEL)), _const_spec((1, D_MODEL)), _const_spec((1, D_MODEL))],
        out_specs=[_row_spec(tm, D_MODEL), _row_spec(tm, D_FF), _row_spec(tm, D_MODEL), _row_spec(tm, D_MODEL),
                   pl.BlockSpec((SUBLANES, LANES), lambda i: (0, 0)), pl.BlockSpec((1, D_MODEL), lambda i: (0, 0))],
        scratch_shapes=[pltpu.VMEM((SUBLANES, LANES), F32), pltpu.VMEM((SUBLANES, D_MODEL), F32)],
        compiler_params=_params(52), name="fwd_ffn_loss",
    )(h1, tgt, wup, wdn, g3, g4)


def _bwd_ffn(ddn, a, dy, h1, wup, wdn, g3):
    T = h1.shape[0]
    tm = TOKEN_TILE
    nsteps = T // tm

    def body(ddn_ref, a_ref, dy_ref, h1_ref, wup_ref, wdn_ref, g3_ref, da_ref, dh1_ref, dg3_ref, acc_g):
        i = pl.program_id(0)

        @pl.when(i == 0)
        def _():
            acc_g[...] = jnp.zeros_like(acc_g)

        ddnv = ddn_ref[...]
        dxn2 = jnp.zeros((tm, D_MODEL), F32)
        for j in range(D_FF // 1024):
            cols = slice(1024 * j, 1024 * (j + 1))
            dhid = _dot_nt(ddnv, wdn_ref[cols, :])
            da = (dhid * (2.0 * jnp.maximum(a_ref[:, cols].astype(F32), 0.0))).astype(BF16)
            da_ref[:, cols] = da
            dxn2 = dxn2 + _dot_nt(da, wup_ref[:, cols])
        r3, h1h = _rms_stats(h1_ref[...])
        acc_g[...] += _fold8(dxn2 * h1h)
        dh1_ref[...] = dy_ref[...] + _rms_bwd(dxn2, h1h, r3, g3_ref[...])

        @pl.when(i == nsteps - 1)
        def _():
            dg3_ref[...] = jnp.sum(acc_g[...], axis=0, keepdims=True)

    return pl.pallas_call(
        body, grid=(nsteps,),
        out_shape=(jax.ShapeDtypeStruct((T, D_FF), BF16), jax.ShapeDtypeStruct((T, D_MODEL), F32),
                   jax.ShapeDtypeStruct((1, D_MODEL), F32)),
        in_specs=[_row_spec(tm, D_MODEL), _row_spec(tm, D_FF), _row_spec(tm, D_MODEL), _row_spec(tm, D_MODEL),
                  _const_spec((D_MODEL, D_FF)), _const_spec((D_FF, D_MODEL)), _const_spec((1, D_MODEL))],
        out_specs=[_row_spec(tm, D_FF), _row_spec(tm, D_MODEL), pl.BlockSpec((1, D_MODEL), lambda i: (0, 0))],
        scratch_shapes=[pltpu.VMEM((SUBLANES, D_MODEL), F32)],
        compiler_params=_params(52), name="bwd_ffn",
    )(ddn, a, dy, h1, wup, wdn, g3)


def _wgrad(xa, dy, name, relu2=False, tn=None, block_cols=None):
    T, K = xa.shape
    N = dy.shape[1]
    tn = N if tn is None else tn
    tt = WGRAD_TILE
    if block_cols:
        nb = tn // block_cols
        out_shape = jax.ShapeDtypeStruct((N // block_cols, K, block_cols), F32)
        out_spec = pl.BlockSpec((nb, K, block_cols), lambda n, t: (n, 0, 0))
    else:
        out_shape = jax.ShapeDtypeStruct((K, N), F32)
        out_spec = pl.BlockSpec((K, tn), lambda n, t: (0, n))

    def body(x_ref, dy_ref, o_ref):
        @pl.when(pl.program_id(1) == 0)
        def _():
            o_ref[...] = jnp.zeros_like(o_ref)

        xv = x_ref[...]
        if relu2:
            xv = jnp.square(jnp.maximum(xv.astype(F32), 0.0)).astype(BF16)
        if block_cols:
            for b in range(nb):
                o_ref[b] += _dot_tn(xv, dy_ref[:, block_cols * b:block_cols * (b + 1)])
        else:
            o_ref[...] += _dot_tn(xv, dy_ref[...])

    return pl.pallas_call(
        body, grid=(N // tn, T // tt), out_shape=out_shape,
        in_specs=[pl.BlockSpec((tt, K), lambda n, t: (t, 0)), pl.BlockSpec((tt, tn), lambda n, t: (t, n))],
        out_specs=out_spec,
        compiler_params=_params(52, 2), name=name,
    )(xa, dy)


def _bwd_merge(dh1, o, A, B, gt, ya, lse, cc, wbs, wba, wo, g2):
    T = dh1.shape[0]
    tm = TOKEN_TILE
    nsteps = T // tm

    def body(dh1_ref, o_ref, a_ref, b_ref, gt_ref, ya_ref, lse_ref, cc_ref, wbs_ref, wba_ref, wo_ref, g2_ref,
             do_ref, da_ref, db_ref, dgl_ref, dys_ref, dya_ref, qab_ref, dab_ref, dg2_ref, acc_g):
        i = pl.program_id(0)

        @pl.when(i == 0)
        def _():
            acc_g[...] = jnp.zeros_like(acc_g)

        dh1v = dh1_ref[...]
        r2, oh = _rms_stats(o_ref[...].astype(F32))
        acc_g[...] += _fold8(dh1v * oh)
        do = _rms_bwd(dh1v, oh, r2, g2_ref[...]).astype(BF16)
        do_ref[...] = do
        dmg = _dot_nt(do, wo_ref[...])
        ga = gt_ref[:, :D_MODEL].astype(F32)
        gb = gt_ref[:, D_MODEL:].astype(F32)
        dgl_ref[:, :D_MODEL] = (dmg * a_ref[...].astype(F32) * ga * (1.0 - ga)).astype(BF16)
        dgl_ref[:, D_MODEL:] = (dmg * b_ref[...].astype(F32) * gb * (1.0 - gb)).astype(BF16)
        dA = (dmg * ga).astype(BF16)
        dB = (dmg * gb).astype(BF16)
        da_ref[...] = dA
        db_ref[...] = dB
        dys_ref[...] = _dot_nt(dA, wbs_ref[...]).astype(BF16)
        dya = _dot_nt(dB, wba_ref[...]).astype(BF16)
        dya_ref[...] = dya
        prod = dya.astype(F32) * ya_ref[...].astype(F32)
        lane = _iota((tm, LANES), 1)
        low = lane < 64
        blk = jnp.zeros((tm, LANES), F32)
        for p in range(4):
            pp = prod[:, LANES * p:LANES * (p + 1)]
            blk = jnp.where(lane == 2 * p, jnp.sum(jnp.where(low, pp, 0.0), axis=1, keepdims=True), blk)
            blk = jnp.where(lane == 2 * p + 1, jnp.sum(jnp.where(low, 0.0, pp), axis=1, keepdims=True), blk)
        qab_ref[...] = _aug_query(cc_ref[...] - lse_ref[...])
        dab_ref[...] = _spread(_split3(-blk), 0).astype(BF16)

        @pl.when(i == nsteps - 1)
        def _():
            dg2_ref[...] = jnp.sum(acc_g[...], axis=0, keepdims=True)

    sd = jax.ShapeDtypeStruct((T, D_MODEL), BF16)
    sh = jax.ShapeDtypeStruct((T, 512), BF16)
    sa = jax.ShapeDtypeStruct((T, LANES), BF16)
    return pl.pallas_call(
        body, grid=(nsteps,),
        out_shape=(sd, sd, sd, jax.ShapeDtypeStruct((T, 2048), BF16), sh, sh, sa, sa,
                   jax.ShapeDtypeStruct((1, D_MODEL), F32)),
        in_specs=[_row_spec(tm, D_MODEL)] * 4 + [_row_spec(tm, 2048), _row_spec(tm, 512), _row_spec(tm, LANES),
                  _row_spec(tm, LANES), _const_spec((512, D_MODEL)), _const_spec((512, D_MODEL)),
                  _const_spec((D_MODEL, D_MODEL)), _const_spec((1, D_MODEL))],
        out_specs=[_row_spec(tm, D_MODEL)] * 3 + [_row_spec(tm, 2048), _row_spec(tm, 512), _row_spec(tm, 512),
                   _row_spec(tm, LANES), _row_spec(tm, LANES), pl.BlockSpec((1, D_MODEL), lambda i: (0, 0))],
        scratch_shapes=[pltpu.VMEM((SUBLANES, D_MODEL), F32)],
        compiler_params=_params(40), name="bwd_merge",
    )(dh1, o, A, B, gt, ya, lse, cc, wbs, wba, wo, g2)


def _bwd_sgu(zuv, dys, gs, bs, wsp, bT, grads):
    T = zuv.shape[0]
    tc = SGU_TILE
    nc = tc // CHUNK
    nsteps = T // tc
    ex = _GradExchange([tuple(g.shape[1:]) for g in grads])
    ng = ex.n

    def body(z_ref, dy_ref, gs_ref, bs_ref, w_ref, bT_ref, *rest):
        g_refs, (dz_ref, dw_ref, dbT_ref, dgs_ref, dbs_ref) = rest[:ng], rest[ng:ng + 5]
        land1 = rest[ng + 5:2 * ng + 5]
        acc_w, acc_b, acc_gs, acc_bs, dvln_s = rest[2 * ng + 5:2 * ng + 10]
        ex_sems = rest[2 * ng + 10:]
        i = pl.program_id(0)

        @pl.when(i == 0)
        def _():
            ex.start(1, g_refs, land1, ex_sems)
            acc_w[...] = jnp.zeros_like(acc_w)
            acc_b[...] = jnp.zeros_like(acc_b)
            acc_gs[...] = jnp.zeros_like(acc_gs)
            acc_bs[...] = jnp.zeros_like(acc_bs)

        z = z_ref[...].astype(F32)
        gsv = gs_ref[...]
        u, vhat, rs, vln = _sgu_forward_parts(z, gsv, bs_ref[...])
        vb = vln.astype(BF16)
        dy = dy_ref[...].astype(F32)
        low_w = (_iota((CHUNK, nc * LANES), 1) % LANES) < 64
        for p in range(4):
            we, wo, b2 = _sgu_pair_weights(w_ref, bT_ref[...], p)
            vcat = _chunks_on_lanes(vb, p, nc)
            s = _sgu_mix(we, wo, b2, vcat, nc)
            dyc = _chunks_on_lanes(dy, p, nc)
            ds = dyc * _chunks_on_lanes(u, p, nc)
            dsb = ds.astype(BF16)
            zero = jnp.zeros_like(dsb)
            dse = jnp.where(low_w, dsb, zero)
            dso = jnp.where(low_w, zero, dsb)
            acc_w[2 * p] += _dot_nt(dse, vcat)
            acc_w[2 * p + 1] += _dot_nt(dso, vcat)
            acc_b[p] += ds
            dvl = jnp.where(low_w, _dot_tn(we, dsb), _dot_tn(wo, dsb))
            for c in range(nc):
                rows, cols = slice(c * CHUNK, (c + 1) * CHUNK), slice(LANES * p, LANES * (p + 1))
                dvln_s[rows, cols] = dvl[:, c * LANES:(c + 1) * LANES]
                du = dy[rows, cols] * s[:, c * LANES:(c + 1) * LANES]
                dz_ref[rows, cols] = (du * _gelu_grad(z[rows, cols])).astype(BF16)
        dvln = dvln_s[...]
        acc_gs[...] += _fold8(dvln * vhat)
        acc_bs[...] += _fold8(dvln)
        al = dvln * gsv
        dvv = rs * (al - jnp.mean(al, axis=1, keepdims=True) - vhat * jnp.mean(al * vhat, axis=1, keepdims=True))
        dz_ref[:, SGU_WIDTH:] = (dvv * _gelu_grad(z[:, SGU_WIDTH:])).astype(BF16)

        @pl.when(i == nsteps - 1)
        def _():
            tril = _iota((CHUNK, CHUNK), 0) >= _iota((CHUNK, CHUNK), 1)
            lane = _iota((CHUNK, LANES), 1)
            low = lane < 64
            blk = jnp.zeros((CHUNK, LANES), F32)
            for g in range(8):
                dw_ref[g] = jnp.where(tril, acc_w[g], 0.0)
            for p in range(4):
                t = acc_b[p]
                tot = t[:, 0:LANES]
                for c in range(1, nc):
                    tot = tot + t[:, c * LANES:(c + 1) * LANES]
                blk = jnp.where(lane == 2 * p, jnp.sum(jnp.where(low, tot, 0.0), axis=1, keepdims=True), blk)
                blk = jnp.where(lane == 2 * p + 1, jnp.sum(jnp.where(low, 0.0, tot), axis=1, keepdims=True), blk)
            dbT_ref[...] = blk
            dgs_ref[...] = jnp.sum(acc_gs[...], axis=0, keepdims=True)
            dbs_ref[...] = jnp.sum(acc_bs[...], axis=0, keepdims=True)
            ex.wait(1, g_refs, land1, ex_sems)

    whole = lambda shape: pl.BlockSpec(shape, lambda i: (0,) * len(shape))
    hbm_spec = pl.BlockSpec(memory_space=pl.ANY)
    return pl.pallas_call(
        body, grid=(nsteps,),
        out_shape=[jax.ShapeDtypeStruct((T, 1024), BF16), jax.ShapeDtypeStruct((8, CHUNK, CHUNK), F32),
                   jax.ShapeDtypeStruct((CHUNK, LANES), F32), jax.ShapeDtypeStruct((1, SGU_WIDTH), F32),
                   jax.ShapeDtypeStruct((1, SGU_WIDTH), F32)] + ex.land_shapes(1),
        in_specs=[_row_spec(tc, 1024), _row_spec(tc, SGU_WIDTH), _const_spec((1, SGU_WIDTH)),
                  _const_spec((1, SGU_WIDTH)), _const_spec((8, CHUNK, CHUNK)), _const_spec((CHUNK, 8))]
        + [hbm_spec] * ng,
        out_specs=[_row_spec(tc, 1024), whole((8, CHUNK, CHUNK)), whole((CHUNK, LANES)),
                   whole((1, SGU_WIDTH)), whole((1, SGU_WIDTH))] + [hbm_spec] * ng,
        scratch_shapes=[pltpu.VMEM((8, CHUNK, CHUNK), F32), pltpu.VMEM((4, CHUNK, nc * LANES), F32),
                        pltpu.VMEM((SUBLANES, SGU_WIDTH), F32), pltpu.VMEM((SUBLANES, SGU_WIDTH), F32),
                        pltpu.VMEM((tc, SGU_WIDTH), F32)] + ex.sem_shapes(1),
        compiler_params=_params(48), name="bwd_sgu",
    )(zuv, dys, gs, bs, wsp, bT, *grads)


def _bwd_attn(qkv, dya, qab, dab, kaug, parts):
    T = qkv.shape[0]
    tq = tk = ATTN_TILE
    nq = T // tq
    nk = T // tk
    ex = _GradExchange([tuple(g.shape[1:]) for g in parts])
    nr = ex.n

    def body(q_ref, do_ref, qa_ref, da_ref, k_ref, v_ref, ka_ref, *rest):
        part_refs, (dq_ref, dk_ref, dv_ref, dcx_ref) = rest[:nr], rest[nr:nr + 4]
        land2, dq_acc, ex_sems = rest[nr + 4:2 * nr + 4], rest[2 * nr + 4], rest[2 * nr + 5:]
        p = pl.program_id(0)
        j = pl.program_id(1)

        @pl.when((p == 0) & (j == 0))
        def _():
            ex.start(2, part_refs, land2, ex_sems)

        lane = _iota((tq, LANES), 1)
        low = lane < 64
        row = _iota((2 * tq, tk), 0) % tq
        col = _iota((2 * tq, tk), 1)
        first = 2 * AUG_LANES * p

        @pl.when(j == 0)
        def _():
            dq_acc[...] = jnp.zeros_like(dq_acc)

        @pl.when((j == 0) & (p == 0))
        def _():
            dcx_ref[...] = jnp.zeros_like(dcx_ref)

        ka = ka_ref[...]
        kk = jnp.concatenate([k_ref[...], ka], axis=1)
        vv = jnp.concatenate([v_ref[...], ka], axis=1)

        def q_block(i, carry, masked):
            dk_a, dv_a = carry
            qsl = pl.ds(pl.multiple_of(i * tq, tq), tq)
            qs = _aug_stack(q_ref[qsl, :], qa_ref[qsl, :], p)
            dos = _aug_stack(do_ref[qsl, :], da_ref[qsl, :], p)
            s = _dot_nt(qs, kk)
            if masked:
                s = jnp.where(col <= row, s, NEG)
            pm = jnp.exp(s)
            ds = pm * _dot_nt(dos, vv)
            dsb = ds.astype(BF16)
            dv_a = dv_a + _dot_tn(pm.astype(BF16), dos[:, :LANES])
            dk_a = dk_a + _dot_tn(dsb, qs)
            dqx = _dot(dsb, kk)
            dq_acc[qsl, :] += jnp.where(low, dqx[:tq, :LANES], dqx[tq:, :LANES])
            dcx_ref[qsl, :] += (jnp.where(lane == first, dqx[:tq, LANES:], 0.0)
                                + jnp.where(lane == first + AUG_LANES, dqx[tq:, LANES:], 0.0))
            return dk_a, dv_a

        init = (jnp.zeros((tk, 2 * LANES), F32), jnp.zeros((tk, LANES), F32))
        carry = q_block(j, init, True)
        dk_a, dv_a = lax.fori_loop(j + 1, nq, lambda i, c: q_block(i, c, False), carry)
        dk_ref[...] = dk_a[:, :LANES].astype(BF16)
        dv_ref[...] = dv_a.astype(BF16)
        ksl = pl.ds(pl.multiple_of(j * tk, tk), tk)
        lk = _iota((tk, LANES), 1)
        dcx_ref[ksl, :] += jnp.where((lk == first + 3) | (lk == first + AUG_LANES + 3), dk_a[:, LANES:], 0.0)

        @pl.when(j == nk - 1)
        def _():
            dq_ref[...] = (dq_acc[...] * 0.125).astype(BF16)

        @pl.when((p == 3) & (j == nk - 1))
        def _():
            ex.wait(2, part_refs, land2, ex_sems)

    sh = jax.ShapeDtypeStruct((T, ATTN_WIDTH), BF16)
    full = lambda cb: pl.BlockSpec((T, LANES), lambda p, j: (0, cb + p))
    blk = lambda cb: pl.BlockSpec((tk, LANES), lambda p, j: (j, cb + p))
    hbm_spec = pl.BlockSpec(memory_space=pl.ANY)
    return pl.pallas_call(
        body, grid=(4, nk),
        out_shape=[sh, sh, sh, jax.ShapeDtypeStruct((T, LANES), F32)] + ex.land_shapes(2),
        in_specs=[full(0), full(0), _const_spec((T, LANES)), _const_spec((T, LANES)), blk(4), blk(8),
                  pl.BlockSpec((tk, LANES), lambda p, j: (j, 0))] + [hbm_spec] * nr,
        out_specs=[full(0), blk(0), blk(0), pl.BlockSpec((T, LANES), lambda p, j: (0, 0))] + [hbm_spec] * nr,
        scratch_shapes=[pltpu.VMEM((T, LANES), F32)] + ex.sem_shapes(2),
        compiler_params=_params(58, 2), name="bwd_attn",
    )(qkv, dya, qab, dab, qkv, qkv, kaug, *parts)


def _bwd_cum(dcx, fl, bfp):
    T = fl.shape[0]
    tb = CUM_TILE

    def body(dcx_ref, fl_ref, b_ref, dfl_ref, dbf_ref):
        triu = (_iota((tb, tb), 0) <= _iota((tb, tb), 1)).astype(F32)
        r, c = _iota((LANES, LANES), 0), _iota((LANES, LANES), 1)
        sel = (((r == AUG_LANES * c) & (c < N_HEADS)).astype(F32)
               - ((r == AUG_LANES * c + 3) & (c < N_HEADS)).astype(F32))
        carry = jnp.zeros((1, LANES), F32)
        dbf = jnp.zeros((1, LANES), F32)
        for i in reversed(range(T // tb)):
            colblk = jnp.dot(dcx_ref[i * tb:(i + 1) * tb, :], sel, precision=HIGHEST, preferred_element_type=F32)
            rc = jnp.dot(triu, colblk, precision=HIGHEST, preferred_element_type=F32) + carry
            carry = rc[0:1, :]
            sig = jax.nn.sigmoid(fl_ref[i * tb:(i + 1) * tb, :] + b_ref[...])
            dfl = rc * (1.0 - sig)
            dfl_ref[i * tb:(i + 1) * tb, :] = dfl.astype(BF16)
            dbf = dbf + jnp.sum(dfl, axis=0, keepdims=True)
        dbf_ref[...] = dbf

    return pl.pallas_call(
        body,
        out_shape=(jax.ShapeDtypeStruct((T, LANES), BF16), jax.ShapeDtypeStruct((1, LANES), F32)),
        compiler_params=pltpu.CompilerParams(vmem_limit_bytes=32 * MIB), name="bwd_cum",
    )(dcx, fl, bfp)


def _bwd_in(dz, dqkvf, dgl, dh1, x2, g1, wz, wf, wg, grad_in):
    T = x2.shape[0]
    tm = TOKEN_TILE
    nsteps = T // tm
    ex = _GradExchange([tuple(grad_in.shape[1:])])

    def body(dz_ref, dqkvf_ref, dgl_ref, dh1_ref, x_ref, g_ref, wz_ref, wf_ref, wg_ref,
             gin_ref, dx_ref, dg1_ref, land1_ref, acc_g, *ex_sems):
        i = pl.program_id(0)

        @pl.when(i == 0)
        def _():
            ex.start(1, [gin_ref], [land1_ref], ex_sems)
            acc_g[...] = jnp.zeros_like(acc_g)

        dxn = _dot_nt(dz_ref[...], wz_ref[:, 0:1024])
        dxn = dxn + _dot_nt(dqkvf_ref[:, 0:1536], wz_ref[:, 1024:2560])
        dxn = dxn + _dot_nt(dqkvf_ref[:, 1536:1664], wf_ref[...])
        dxn = dxn + _dot_nt(dgl_ref[...], wg_ref[...])
        r1, xh = _rms_stats(x_ref[...])
        acc_g[...] += _fold8(dxn * xh)
        dx_ref[...] = dh1_ref[...] + _rms_bwd(dxn, xh, r1, g_ref[...])

        @pl.when(i == nsteps - 1)
        def _():
            dg1_ref[...] = jnp.sum(acc_g[...], axis=0, keepdims=True)
            ex.wait(1, [gin_ref], [land1_ref], ex_sems)

    hbm_spec = pl.BlockSpec(memory_space=pl.ANY)
    return pl.pallas_call(
        body, grid=(nsteps,),
        out_shape=[jax.ShapeDtypeStruct((T, D_MODEL), F32), jax.ShapeDtypeStruct((1, D_MODEL), F32)]
        + ex.land_shapes(1),
        in_specs=[_row_spec(tm, 1024), _row_spec(tm, 1664), _row_spec(tm, 2048), _row_spec(tm, D_MODEL),
                  _row_spec(tm, D_MODEL), _const_spec((1, D_MODEL)), _const_spec((D_MODEL, ZQKV_WIDTH)),
                  _const_spec((D_MODEL, LANES)), _const_spec((D_MODEL, 2048)), hbm_spec],
        out_specs=[_row_spec(tm, D_MODEL), pl.BlockSpec((1, D_MODEL), lambda i: (0, 0)), hbm_spec],
        scratch_shapes=[pltpu.VMEM((SUBLANES, D_MODEL), F32)] + ex.sem_shapes(1),
        compiler_params=_params(48), name="bwd_in",
    )(dz, dqkvf, dgl, dh1, x2, g1, wz, wf, wg, grad_in)


def _pad_rows(v, rows):
    return jnp.pad(v, ((0, rows - v.shape[0]), (0, 0)))


def _pack_small(g_mix_pre, b_forget, g_sgu, b_sgu, w_spatial, b_spatial, g_mix_post, g_ffn_pre, g_ffn_post):
    vec = lambda v: _pad_rows(v.reshape(-1, LANES), SUBLANES)
    return jnp.concatenate([
        w_spatial.reshape(-1, LANES), vec(g_mix_pre), _pad_rows(jnp.pad(b_forget, ((0, 0), (0, LANES - N_HEADS))), SUBLANES),
        vec(g_sgu), vec(b_sgu), vec(b_spatial), vec(g_mix_post), vec(g_ffn_pre), vec(g_ffn_post)], axis=0)


def _unpack_small(p):
    nw = N_HEADS * CHUNK * CHUNK // LANES
    blk = lambda k: p[nw + SUBLANES * k: nw + SUBLANES * (k + 1)]
    return dict(
        w_spatial=p[:nw].reshape(1, N_HEADS, CHUNK, CHUNK),
        g_mix_pre=blk(0).reshape(1, D_MODEL), b_forget=blk(1)[0:1, :N_HEADS],
        g_sgu=blk(2)[:4].reshape(1, SGU_WIDTH), b_sgu=blk(3)[:4].reshape(1, SGU_WIDTH),
        b_spatial=blk(4).reshape(1, N_HEADS, CHUNK), g_mix_post=blk(5).reshape(1, D_MODEL),
        g_ffn_pre=blk(6).reshape(1, D_MODEL), g_ffn_post=blk(7).reshape(1, D_MODEL))


def kernel(x, g_mix_pre, w_in, b_forget, g_sgu, b_sgu, w_spatial, b_spatial, w_branch_sgu, w_branch_attn, w_out, g_mix_post, g_ffn_pre, w_up, w_down, g_ffn_post, loss_target, m_g_mix_pre, m_w_in, m_b_forget, m_g_sgu, m_b_sgu, m_w_spatial, m_b_spatial, m_w_branch_sgu, m_w_branch_attn, m_w_out, m_g_mix_post, m_g_ffn_pre, m_w_up, m_w_down, m_g_ffn_post, v_g_mix_pre, v_w_in, v_b_forget, v_g_sgu, v_b_sgu, v_w_spatial, v_b_spatial, v_w_branch_sgu, v_w_branch_attn, v_w_out, v_g_mix_post, v_g_ffn_pre, v_w_up, v_w_down, v_g_ffn_post):
    T = x.shape[1]
    x2 = x.reshape(T, D_MODEL)
    tgt = loss_target.reshape(T, D_MODEL)

    wg_in = _gather_w_in(w_in[0])
    w_in_full = jnp.concatenate([wg_in[j, :, :IN_SHARD] for j in range(N_DEV)], axis=1)
    wz = w_in_full[:, :ZQKV_WIDTH]
    wf = jnp.pad(w_in_full[:, ZQKV_WIDTH:GATE_OFFSET], ((0, 0), (0, LANES - N_HEADS)))
    wgt = w_in_full[:, GATE_OFFSET:]
    bfp = jnp.pad(b_forget, ((0, 0), (0, LANES - N_HEADS)))
    wsp = w_spatial[0]
    bT = b_spatial[0].T

    xn, zuv, qkv, fl, gt = _fwd_in(x2, g_mix_pre, wz, wf, wgt)
    cc, qaug, kaug = _fwd_cum(fl, bfp)
    ys = _fwd_sgu(zuv, g_sgu, b_sgu, wsp, bT)
    ya, lse, wbs, wba, wo, wup, wdn = _fwd_attn(
        qkv, qaug, kaug, (w_branch_sgu[0], w_branch_attn[0], w_out[0], w_up[0], w_down[0]))
    A, B, mg, o, h1 = _fwd_merge(ys, ya, gt, x2, wbs, wba, wo, g_mix_post)
    xn2, a, ddn, dy, loss_part, dg4 = _fwd_ffn_loss(h1, tgt, wup, wdn, g_ffn_pre, g_ffn_post)
    loss = lax.psum(jnp.sum(loss_part), ("x", "y", "c"))

    da, dh1, dg3 = _bwd_ffn(ddn, a, dy, h1, wup, wdn, g_ffn_pre)
    dw_up = _wgrad(xn2, da, "wgrad_up", tn=2048, block_cols=512)
    dw_down = _wgrad(a, ddn, "wgrad_down", relu2=True)
    do, dA, dB, dgl, dys, dya, qab, dab, dg2 = _bwd_merge(dh1, o, A, B, gt, ya, lse, cc, wbs, wba, wo, g_mix_post)
    dw_out = _wgrad(mg, do, "wgrad_out")
    dw_bs = _wgrad(ys, dA, "wgrad_branch_sgu")
    dw_ba = _wgrad(ya, dB, "wgrad_branch_attn")
    col_blocks = lambda g, w: g.reshape(g.shape[0], N_DEV, w).transpose(1, 0, 2)
    row_blocks = lambda g, r: g.reshape(N_DEV, r, g.shape[1])
    early_names = ["w_branch_sgu", "w_branch_attn", "w_out", "w_up", "w_down"]
    early = [col_blocks(dw_bs, 128), col_blocks(dw_ba, 128), row_blocks(dw_out, 128), dw_up, row_blocks(dw_down, 512)]
    owners = _owner_indices()
    dzuv, dwsp, dbT, dgs, dbs, *early_land1 = _bwd_sgu(zuv, dys, g_sgu, b_sgu, wsp, bT, early)
    early_parts = [_chip_partials(g, l1, owners, "chip_partials_" + nm)
                   for g, l1, nm in zip(early, early_land1, early_names)]
    dq, dk, dv, dcx, *early_land2 = _bwd_attn(qkv, dya, qab, dab, kaug, early_parts)
    dfl, dbf = _bwd_cum(dcx, fl, bfp)
    dqkvf = jnp.concatenate([dq, dk, dv, dfl], axis=1)
    dw_z = _wgrad(xn, dzuv, "wgrad_in_z")
    dw_qkvf = _wgrad(xn, dqkvf, "wgrad_in_qkvf")
    dw_g = _wgrad(xn, dgl, "wgrad_in_gate")
    dw_in = jnp.concatenate([dw_z, dw_qkvf[:, :3 * ATTN_WIDTH + N_HEADS], dw_g], axis=1)
    blocks_in = jnp.stack([jnp.pad(dw_in[:, IN_SHARD * j:IN_SHARD * (j + 1)], ((0, 0), (0, IN_SHARD_PAD - IN_SHARD)))
                           for j in range(N_DEV)])
    dx, dg1, land1_in = _bwd_in(dzuv, dqkvf, dgl, dh1, x2, g_mix_pre, wz, wf, wgt, blocks_in)
    part_in = _chip_partials(blocks_in, land1_in, owners, "chip_partials_w_in")

    small_g = _pack_small(dg1, dbf[:, :N_HEADS], dgs, dbs, dwsp[None], dbT[:, :N_HEADS].T[None], dg2, dg3, dg4)
    small_w = _pack_small(g_mix_pre, b_forget, g_sgu, b_sgu, w_spatial, b_spatial, g_mix_post, g_ffn_pre, g_ffn_post)
    small_m = _pack_small(m_g_mix_pre, m_b_forget, m_g_sgu, m_b_sgu, m_w_spatial, m_b_spatial, m_g_mix_post,
                          m_g_ffn_pre, m_g_ffn_post)
    small_v = _pack_small(v_g_mix_pre, v_b_forget, v_g_sgu, v_b_sgu, v_w_spatial, v_b_spatial, v_g_mix_post,
                          v_g_ffn_pre, v_g_ffn_post)
    *small_out, land2_in = _allreduce_small_adamw(small_g, small_w, small_m, small_v, part_in)
    sg, sd, sm, sv = (_unpack_small(t) for t in small_out)

    big = {}
    g_in = _reduced_grad(blocks_in, land1_in, land2_in, owners, "reduced_grad_w_in")[:, :IN_SHARD]
    d_, m_, v_ = _adamw(w_in[0], g_in, m_w_in[0], v_w_in[0], "adamw_w_in")
    big["w_in"] = (g_in[None], d_[None], m_[None], v_[None])
    early_wmv = [(w_branch_sgu, m_w_branch_sgu, v_w_branch_sgu), (w_branch_attn, m_w_branch_attn, v_w_branch_attn),
                 (w_out, m_w_out, v_w_out), (w_up, m_w_up, v_w_up), (w_down, m_w_down, v_w_down)]
    for nm, (w, m, v), g, l1, l2 in zip(early_names, early_wmv, early, early_land1, early_land2):
        big[nm] = tuple(t[None] for t in _adamw_reduced(w[0], m[0], v[0], g, l1, l2, owners, "adamw_" + nm))

    order = ["g_mix_pre", "w_in", "b_forget", "g_sgu", "b_sgu", "w_spatial", "b_spatial", "w_branch_sgu",
             "w_branch_attn", "w_out", "g_mix_post", "g_ffn_pre", "w_up", "w_down", "g_ffn_post"]
    outs = [loss, dx.reshape(1, T, D_MODEL)]
    for kind, small in enumerate((sg, sd, sm, sv)):
        outs += [big[nm][kind] if nm in big else small[nm] for nm in order]
    return tuple(outs)
```

```python
import jax
import jax.numpy as jnp
from jax import lax
from jax.experimental import pallas as pl
from jax.experimental.pallas import tpu as pltpu

F32 = jnp.float32
BF16 = jnp.bfloat16
HIGHEST = lax.Precision.HIGHEST
MESH = pl.DeviceIdType.MESH

D_MODEL = 1024
SGU_WIDTH = 512
ATTN_WIDTH = 512
N_HEADS = 8
CHUNK = 128
D_FF = 4096
IN_WIDTH = 4616
N_DEV = 8
IN_SHARD = IN_WIDTH // N_DEV
IN_SHARD_PAD = 640
ZQKV_WIDTH = 2 * SGU_WIDTH + 3 * ATTN_WIDTH
GATE_OFFSET = ZQKV_WIDTH + N_HEADS
EPS = 1e-6
LANES = 128
SUBLANES = 8
VMEM_BYTES = 64 * 1024 * 1024
MIB = 1024 * 1024

ADAM_LR = 0.001
ADAM_B1 = 0.9
ADAM_B2 = 0.999
ADAM_EPS = 1e-08
ADAM_WD = 0.01
ADAM_STEP = 10

TOKEN_TILE = 256
ATTN_TILE = 512
CUM_TILE = 256
SGU_TILE = 512
WGRAD_TILE = 1024
NEG = -1e30

NT_DIMS = (((1,), (1,)), ((), ()))
TN_DIMS = (((0,), (0,)), ((), ()))


def _params(vmem_mb, n_grid=1):
    return pltpu.CompilerParams(
        dimension_semantics=("arbitrary",) * n_grid,
        vmem_limit_bytes=min(vmem_mb * MIB, VMEM_BYTES - 6 * MIB),
    )


def _dot(a, b):
    return jnp.dot(a, b, preferred_element_type=F32)


def _dot_nt(a, b):
    return lax.dot_general(a, b, NT_DIMS, preferred_element_type=F32)


def _dot_tn(a, b):
    return lax.dot_general(a, b, TN_DIMS, preferred_element_type=F32)


def _const_spec(shape):
    nd = len(shape)
    return pl.BlockSpec(shape, lambda *_: (0,) * nd, pipeline_mode=pl.Buffered(1))


def _row_spec(tm, n, col=0):
    return pl.BlockSpec((tm, n), lambda i: (i, col))


def _fold8(v):
    return v.reshape(v.shape[0] // SUBLANES, SUBLANES, v.shape[1]).sum(axis=0)


def _pick(v, lane_iota, k):
    return jnp.sum(jnp.where(lane_iota == k, v, 0.0), axis=1, keepdims=True)


def _iota(shape, dim):
    return lax.broadcasted_iota(jnp.int32, shape, dim)


def _gelu(x):
    c = 0.7978845608028654
    return 0.5 * x * (1.0 + jnp.tanh(c * (x + 0.044715 * x * x * x)))


def _gelu_grad(x):
    c = 0.7978845608028654
    t = jnp.tanh(c * (x + 0.044715 * x * x * x))
    return 0.5 * (1.0 + t) + 0.5 * x * (1.0 - t * t) * (c * (1.0 + 3.0 * 0.044715 * x * x))


def _rms_stats(v):
    r = lax.rsqrt(jnp.mean(v * v, axis=1, keepdims=True) + EPS)
    return r, v * r


def _rms_bwd(dout, vhat, r, g):
    a = dout * g
    return r * (a - vhat * jnp.mean(a * vhat, axis=1, keepdims=True))


def _mesh_pos():
    return lax.axis_index("x"), lax.axis_index("y"), lax.axis_index("c")


def _dev_index(px, py, pc):
    return 4 * px + 2 * py + pc


def _other_chips(x, y):
    return [(1 - x, y), (x, 1 - y), (1 - x, 1 - y)]


class _WeightGather:
    def __init__(self, shard_shapes, kinds, stage_shapes=None):
        self.shard_shapes = list(shard_shapes)
        self.kinds = list(kinds)
        self.stage_shapes = list(stage_shapes or shard_shapes)
        self.n = len(self.kinds)

    def out_shapes(self):
        shapes = []
        for (rows, cols), kind in zip(self.stage_shapes, self.kinds):
            full = {"block": (N_DEV, rows, cols), "rows": (N_DEV * rows, cols), "cols": (rows, N_DEV * cols)}[kind]
            shapes.append(jax.ShapeDtypeStruct(full, BF16))
        return shapes

    def scratch_shapes(self):
        return ([pltpu.VMEM(s, BF16) for s in self.stage_shapes]
                + [pltpu.SemaphoreType.DMA((self.n, 7)), pltpu.SemaphoreType.DMA((self.n, 7)),
                   pltpu.SemaphoreType.DMA((self.n,))])

    def _view(self, a, ref, j):
        rows, cols = self.stage_shapes[a]
        if self.kinds[a] == "block":
            return ref.at[j]
        if self.kinds[a] == "rows":
            return ref.at[pl.ds(pl.multiple_of(j * rows, rows), rows), :]
        return ref.at[:, pl.ds(pl.multiple_of(j * cols, cols), cols)]

    def _copy(self, outs, scratch, a, k, block, to, from_stage=False):
        stage, (send_sems, recv_sems, _) = scratch[:self.n], scratch[self.n:]
        dst = self._view(a, outs[a], _dev_index(*block))
        return pltpu.make_async_remote_copy(
            src_ref=stage[a] if from_stage else dst, dst_ref=dst,
            send_sem=send_sems.at[a, k], recv_sem=recv_sems.at[a, k],
            device_id=to, device_id_type=MESH)

    def _local(self, outs, scratch, a, me):
        return pltpu.make_async_copy(scratch[a], self._view(a, outs[a], _dev_index(*me)), scratch[-1].at[a])

    def start(self, ins, outs, scratch):
        x, y, c = _mesh_pos()
        me, sibling = (x, y, c), (x, y, 1 - c)
        for a in range(self.n):
            rows, cols = self.shard_shapes[a]
            if self.stage_shapes[a] != self.shard_shapes[a]:
                scratch[a][...] = jnp.zeros(self.stage_shapes[a], BF16)
            scratch[a][0:rows, 0:cols] = ins[a][...].astype(BF16)
            self._local(outs, scratch, a, me).start()
        for a in range(self.n):
            self._copy(outs, scratch, a, 0, me, sibling, True).start()
            for j, chip in enumerate(_other_chips(x, y)):
                self._copy(outs, scratch, a, 1 + j, me, (*chip, c), True).start()

    def forward(self, outs, scratch):
        x, y, c = _mesh_pos()
        me, sibling = (x, y, c), (x, y, 1 - c)
        for a in range(self.n):
            for j, chip in enumerate(_other_chips(x, y)):
                self._copy(outs, scratch, a, 1 + j, (*chip, c), me).wait_recv()
                self._copy(outs, scratch, a, 4 + j, (*chip, c), sibling).start()

    def finish(self, outs, scratch):
        x, y, c = _mesh_pos()
        me, sibling = (x, y, c), (x, y, 1 - c)
        chips = _other_chips(x, y)
        for a in range(self.n):
            self._copy(outs, scratch, a, 0, sibling, me).wait_recv()
            for j, chip in enumerate(chips):
                self._copy(outs, scratch, a, 4 + j, (*chip, 1 - c), me).wait_recv()
        for a in range(self.n):
            self._copy(outs, scratch, a, 0, me, sibling, True).wait_send()
            for j, chip in enumerate(chips):
                self._copy(outs, scratch, a, 1 + j, me, (*chip, c), True).wait_send()
                self._copy(outs, scratch, a, 4 + j, (*chip, c), sibling).wait_send()
            self._local(outs, scratch, a, me).wait()


def _gather_w_in(w_in_local):
    g = _WeightGather([(D_MODEL, IN_SHARD)], ["block"], [(D_MODEL, IN_SHARD_PAD)])

    def body(w_ref, out_ref, *scratch):
        g.start([w_ref], [out_ref], scratch)
        g.forward([out_ref], scratch)
        g.finish([out_ref], scratch)

    return pl.pallas_call(
        body,
        out_shape=g.out_shapes()[0],
        in_specs=[pl.BlockSpec(memory_space=pltpu.VMEM)],
        out_specs=pl.BlockSpec(memory_space=pl.ANY),
        scratch_shapes=g.scratch_shapes(),
        compiler_params=pltpu.CompilerParams(vmem_limit_bytes=32 * MIB),
        name="gather_w_in",
    )(w_in_local)


class _GradExchange:
    def __init__(self, shapes):
        self.shapes = [tuple(s) for s in shapes]
        self.n = len(self.shapes)

    def land_shapes(self, stage):
        slots, dtype = (4, F32) if stage == 1 else (3, BF16)
        return [jax.ShapeDtypeStruct((slots,) + s, dtype) for s in self.shapes]

    def sem_shapes(self, stage):
        slots = 4 if stage == 1 else 3
        return [pltpu.SemaphoreType.DMA((self.n, slots)), pltpu.SemaphoreType.DMA((self.n, slots))]

    def _copy(self, stage, srcs, lands, sems, a, k):
        x, y, c = _mesh_pos()
        cx, cy = (_other_chips(x, y) + [(x, y)])[k]
        if stage == 1:
            src, to = srcs[a].at[_dev_index(cx, cy, 1 - c)], (x, y, 1 - c)
        else:
            src, to = srcs[a].at[k], (cx, cy, c)
        return pltpu.make_async_remote_copy(
            src_ref=src, dst_ref=lands[a].at[k], send_sem=sems[0].at[a, k], recv_sem=sems[1].at[a, k],
            device_id=to, device_id_type=MESH)

    def start(self, stage, srcs, lands, sems):
        for a in range(self.n):
            for k in range(4 if stage == 1 else 3):
                self._copy(stage, srcs, lands, sems, a, k).start()

    def wait(self, stage, srcs, lands, sems):
        for a in range(self.n):
            for k in range(4 if stage == 1 else 3):
                cp = self._copy(stage, srcs, lands, sems, a, k)
                cp.wait_recv()
                cp.wait_send()


def _owner_indices():
    x, y, c = _mesh_pos()
    return jnp.stack([_dev_index(cx, cy, c) for cx, cy in _other_chips(x, y) + [(x, y)]]).astype(jnp.int32)


def _chip_partials(g, land1, idx, name):
    _, rows, cols = g.shape
    tr = min(rows, 256)

    def body(idx_ref, g_ref, l_ref, o_ref):
        o_ref[...] = (g_ref[...] + l_ref[...]).astype(BF16)

    return pl.pallas_call(
        body,
        grid_spec=pltpu.PrefetchScalarGridSpec(
            num_scalar_prefetch=1, grid=(3, rows // tr),
            in_specs=[pl.BlockSpec((None, tr, cols), lambda k, r, idx: (idx[k], r, 0)),
                      pl.BlockSpec((None, tr, cols), lambda k, r, idx: (k, r, 0))],
            out_specs=pl.BlockSpec((None, tr, cols), lambda k, r, idx: (k, r, 0))),
        out_shape=jax.ShapeDtypeStruct((3, rows, cols), BF16),
        compiler_params=_params(32, 2), name=name,
    )(idx, g, land1)


def _reduced_block(g_ref, l1_ref, a_ref, b_ref, c_ref):
    return ((g_ref[...] + l1_ref[...]) + a_ref[...].astype(F32)) + b_ref[...].astype(F32) + c_ref[...].astype(F32)


def _reduced_specs(tm, cols):
    return [pl.BlockSpec((None, tm, cols), lambda i, idx: (idx[3], i, 0)),
            pl.BlockSpec((None, tm, cols), lambda i, idx: (3, i, 0)),
            pl.BlockSpec((None, tm, cols), lambda i, idx: (0, i, 0)),
            pl.BlockSpec((None, tm, cols), lambda i, idx: (1, i, 0)),
            pl.BlockSpec((None, tm, cols), lambda i, idx: (2, i, 0))]


def _reduced_grad(g, land1, land2, idx, name):
    _, rows, cols = g.shape
    tm = min(rows, 256)

    def body(idx_ref, g_ref, l1_ref, a_ref, b_ref, c_ref, o_ref):
        o_ref[...] = _reduced_block(g_ref, l1_ref, a_ref, b_ref, c_ref)

    return pl.pallas_call(
        body,
        grid_spec=pltpu.PrefetchScalarGridSpec(
            num_scalar_prefetch=1, grid=(rows // tm,), in_specs=_reduced_specs(tm, cols),
            out_specs=pl.BlockSpec((tm, cols), lambda i, idx: (i, 0))),
        out_shape=jax.ShapeDtypeStruct((rows, cols), F32),
        compiler_params=_params(32), name=name,
    )(idx, g, land1, land2, land2, land2)


def _adamw_math(w, g, m, v):
    m = ADAM_B1 * m + (1.0 - ADAM_B1) * g
    v = ADAM_B2 * v + (1.0 - ADAM_B2) * (g * g)
    m_hat = m / (1.0 - ADAM_B1 ** ADAM_STEP)
    v_hat = v / (1.0 - ADAM_B2 ** ADAM_STEP)
    delta = -ADAM_LR * (m_hat / (jnp.sqrt(v_hat) + ADAM_EPS) + ADAM_WD * w)
    return delta, m, v


def _allreduce_small_adamw(g, w, m, v):
    shape = g.shape

    def body(g_ref, w_ref, m_ref, v_ref, gs_ref, d_ref, nm_ref, nv_ref, sib, psum, land, send_sems, recv_sems):
        x, y, c = _mesh_pos()
        sibling = (x, y, 1 - c)
        chips = _other_chips(x, y)
        first = pltpu.make_async_remote_copy(
            src_ref=g_ref, dst_ref=sib, send_sem=send_sems.at[0], recv_sem=recv_sems.at[0],
            device_id=sibling, device_id_type=MESH)
        first.start()
        first.wait_recv()
        psum[...] = g_ref[...] + sib[...]
        second = []
        for k, (cx, cy) in enumerate(chips):
            cp = pltpu.make_async_remote_copy(
                src_ref=psum, dst_ref=land.at[k], send_sem=send_sems.at[1 + k], recv_sem=recv_sems.at[1 + k],
                device_id=(cx, cy, c), device_id_type=MESH)
            cp.start()
            second.append(cp)
        for cp in second:
            cp.wait_recv()
        total = (psum[...] + land[0]) + (land[1] + land[2])
        gs_ref[...] = total
        delta, nm, nv = _adamw_math(w_ref[...], total, m_ref[...], v_ref[...])
        d_ref[...] = delta
        nm_ref[...] = nm
        nv_ref[...] = nv
        first.wait_send()
        for cp in second:
            cp.wait_send()

    sd = jax.ShapeDtypeStruct(shape, F32)
    return pl.pallas_call(
        body,
        out_shape=(sd, sd, sd, sd),
        in_specs=[pl.BlockSpec(memory_space=pltpu.VMEM)] * 4,
        out_specs=[pl.BlockSpec(memory_space=pltpu.VMEM)] * 4,
        scratch_shapes=[pltpu.VMEM(shape, F32), pltpu.VMEM(shape, F32), pltpu.VMEM((3,) + shape, F32),
                        pltpu.SemaphoreType.DMA((4,)), pltpu.SemaphoreType.DMA((4,))],
        compiler_params=pltpu.CompilerParams(vmem_limit_bytes=32 * MIB),
        name="allreduce_small_adamw",
    )(g, w, m, v)


def _adamw_reduced(w, m, v, g, land1, land2, idx, name):
    rows, cols = w.shape
    tm = min(rows, 256)

    def body(idx_ref, w_ref, m_ref, v_ref, g_ref, l1_ref, a_ref, b_ref, c_ref, go_ref, d_ref, nm_ref, nv_ref):
        gsum = _reduced_block(g_ref, l1_ref, a_ref, b_ref, c_ref)
        go_ref[...] = gsum
        delta, nm, nv = _adamw_math(w_ref[...], gsum, m_ref[...], v_ref[...])
        d_ref[...] = delta
        nm_ref[...] = nm
        nv_ref[...] = nv

    sd = jax.ShapeDtypeStruct((rows, cols), F32)
    spec = pl.BlockSpec((tm, cols), lambda i, idx: (i, 0))
    return pl.pallas_call(
        body,
        grid_spec=pltpu.PrefetchScalarGridSpec(
            num_scalar_prefetch=1, grid=(rows // tm,), in_specs=[spec] * 3 + _reduced_specs(tm, cols),
            out_specs=[spec] * 4),
        out_shape=(sd, sd, sd, sd),
        compiler_params=_params(32), name=name,
    )(idx, w, m, v, g, land1, land2, land2, land2)


def _adamw(w, g, m, v, name):
    rows, cols = w.shape
    tm = 256 if rows % 256 == 0 else rows

    def body(w_ref, g_ref, m_ref, v_ref, d_ref, nm_ref, nv_ref):
        delta, nm, nv = _adamw_math(w_ref[...], g_ref[...], m_ref[...], v_ref[...])
        d_ref[...] = delta
        nm_ref[...] = nm
        nv_ref[...] = nv

    sd = jax.ShapeDtypeStruct((rows, cols), F32)
    spec = _row_spec(tm, cols)
    return pl.pallas_call(
        body, grid=(rows // tm,), out_shape=(sd, sd, sd), in_specs=[spec] * 4, out_specs=[spec] * 3,
        compiler_params=_params(32), name=name,
    )(w, g, m, v)


def _fwd_in(x2, g1, wz, wf, wg):
    T = x2.shape[0]
    tm = TOKEN_TILE

    def body(x_ref, g_ref, wz_ref, wf_ref, wg_ref, xn_ref, zuv_ref, qkv_ref, fl_ref, gt_ref):
        x = x_ref[...]
        r, xh = _rms_stats(x)
        xn = (xh * g_ref[...]).astype(BF16)
        xn_ref[...] = xn
        zuv_ref[...] = _dot(xn, wz_ref[:, 0:1024]).astype(BF16)
        qkv_ref[:, 0:512] = (_dot(xn, wz_ref[:, 1024:1536]) * 0.125).astype(BF16)
        qkv_ref[:, 512:1536] = _dot(xn, wz_ref[:, 1536:2560]).astype(BF16)
        fl_ref[...] = _dot(xn, wf_ref[...])
        gt_ref[...] = jax.nn.sigmoid(_dot(xn, wg_ref[...])).astype(BF16)

    return pl.pallas_call(
        body, grid=(T // tm,),
        out_shape=(jax.ShapeDtypeStruct((T, D_MODEL), BF16), jax.ShapeDtypeStruct((T, 1024), BF16),
                   jax.ShapeDtypeStruct((T, 1536), BF16), jax.ShapeDtypeStruct((T, LANES), F32),
                   jax.ShapeDtypeStruct((T, 2048), BF16)),
        in_specs=[_row_spec(tm, D_MODEL), _const_spec((1, D_MODEL)), _const_spec((D_MODEL, ZQKV_WIDTH)),
                  _const_spec((D_MODEL, LANES)), _const_spec((D_MODEL, 2048))],
        out_specs=[_row_spec(tm, D_MODEL), _row_spec(tm, 1024), _row_spec(tm, 1536), _row_spec(tm, LANES),
                   _row_spec(tm, 2048)],
        compiler_params=_params(48), name="fwd_in",
    )(x2, g1, wz, wf, wg)


def _log_sigmoid(f):
    return jnp.minimum(f, 0.0) - jnp.log1p(jnp.exp(-jnp.abs(f)))


AUG_LANES = 6


def _split3(v):
    hi = v.astype(BF16)
    r1 = v - hi.astype(F32)
    mid = r1.astype(BF16)
    lo = (r1 - mid.astype(F32)).astype(BF16)
    return hi, mid, lo


def _spread(parts, k0):
    r, c = _iota((LANES, LANES), 0), _iota((LANES, LANES), 1)
    out = None
    for i, part in enumerate(parts):
        e = ((c == AUG_LANES * r + (k0 + i)) & (r < N_HEADS)).astype(BF16)
        term = _dot(part, e)
        out = term if out is None else out + term
    return out


def _aug_query(v):
    ones = (_iota(v.shape, 1) < N_HEADS).astype(BF16)
    return (_spread(_split3(v), 0) + _spread((ones, ones, ones), 3)).astype(BF16)


def _aug_key(v):
    ones = (_iota(v.shape, 1) < N_HEADS).astype(BF16)
    return (_spread((ones, ones, ones), 0) - _spread(_split3(v), 3)).astype(BF16)


def _aug_stack(t2, aug, p):
    lane = _iota(t2.shape, 1)
    low = lane < 64
    zero = jnp.zeros_like(t2)
    first = 2 * AUG_LANES * p
    a_e = jnp.where((lane >= first) & (lane < first + AUG_LANES), aug, zero)
    a_o = jnp.where((lane >= first + AUG_LANES) & (lane < first + 2 * AUG_LANES), aug, zero)
    top = jnp.concatenate([jnp.where(low, t2, zero), a_e], axis=1)
    bot = jnp.concatenate([jnp.where(low, zero, t2), a_o], axis=1)
    return jnp.concatenate([top, bot], axis=0)


def _fwd_cum(fl, bfp):
    T = fl.shape[0]
    tb = CUM_TILE

    def body(fl_ref, b_ref, cc_ref, qa_ref, ka_ref):
        tri = (_iota((tb, tb), 0) >= _iota((tb, tb), 1)).astype(F32)
        carry = jnp.zeros((1, LANES), F32)
        for i in range(T // tb):
            rows = slice(i * tb, (i + 1) * tb)
            lf = _log_sigmoid(fl_ref[rows, :] + b_ref[...])
            cs = jnp.dot(tri, lf, precision=HIGHEST, preferred_element_type=F32) + carry
            cc_ref[rows, :] = cs
            carry = cs[tb - 1:tb, :]
            qa_ref[rows, :] = _aug_query(cs)
            ka_ref[rows, :] = _aug_key(cs)

    return pl.pallas_call(
        body,
        out_shape=(jax.ShapeDtypeStruct((T, LANES), F32), jax.ShapeDtypeStruct((T, LANES), BF16),
                   jax.ShapeDtypeStruct((T, LANES), BF16)),
        compiler_params=pltpu.CompilerParams(vmem_limit_bytes=32 * MIB), name="fwd_cum",
    )(fl, bfp)


def _sgu_forward_parts(z, gs, bs):
    u = _gelu(z[:, :SGU_WIDTH])
    vv = _gelu(z[:, SGU_WIDTH:])
    vc = vv - jnp.mean(vv, axis=1, keepdims=True)
    rs = lax.rsqrt(jnp.mean(vc * vc, axis=1, keepdims=True) + EPS)
    vhat = vc * rs
    return u, vhat, rs, vhat * gs + bs


def _sgu_pair_weights(w_ref, bT, p):
    tril = _iota((CHUNK, CHUNK), 0) >= _iota((CHUNK, CHUNK), 1)
    we = jnp.where(tril, w_ref[2 * p], 0.0).astype(BF16)
    wo = jnp.where(tril, w_ref[2 * p + 1], 0.0).astype(BF16)
    lane8 = _iota(bT.shape, 1)
    low = _iota((CHUNK, LANES), 1) < 64
    b2 = jnp.where(low, _pick(bT, lane8, 2 * p), _pick(bT, lane8, 2 * p + 1))
    return we, wo, b2


def _chunks_on_lanes(v, p, nc):
    return jnp.concatenate([v[c * CHUNK:(c + 1) * CHUNK, LANES * p:LANES * (p + 1)] for c in range(nc)], axis=1)


def _sgu_mix(we, wo, b2, vcat, nc):
    low = (_iota((CHUNK, nc * LANES), 1) % LANES) < 64
    return jnp.where(low, _dot(we, vcat), _dot(wo, vcat)) + jnp.concatenate([b2] * nc, axis=1)


def _fwd_sgu(zuv, gs, bs, wsp, bT):
    T = zuv.shape[0]
    tc = SGU_TILE
    nc = tc // CHUNK

    def body(z_ref, gs_ref, bs_ref, w_ref, bT_ref, y_ref):
        u, _, _, vln = _sgu_forward_parts(z_ref[...].astype(F32), gs_ref[...], bs_ref[...])
        vb = vln.astype(BF16)
        for p in range(4):
            we, wo, b2 = _sgu_pair_weights(w_ref, bT_ref[...], p)
            s = _sgu_mix(we, wo, b2, _chunks_on_lanes(vb, p, nc), nc)
            for c in range(nc):
                rows, cols = slice(c * CHUNK, (c + 1) * CHUNK), slice(LANES * p, LANES * (p + 1))
                y_ref[rows, cols] = (u[rows, cols] * s[:, c * LANES:(c + 1) * LANES]).astype(BF16)

    return pl.pallas_call(
        body, grid=(T // tc,), out_shape=jax.ShapeDtypeStruct((T, SGU_WIDTH), BF16),
        in_specs=[_row_spec(tc, 1024), _const_spec((1, SGU_WIDTH)), _const_spec((1, SGU_WIDTH)),
                  _const_spec((8, CHUNK, CHUNK)), _const_spec((CHUNK, 8))],
        out_specs=_row_spec(tc, SGU_WIDTH),
        compiler_params=_params(40), name="fwd_sgu",
    )(zuv, gs, bs, wsp, bT)


def _fwd_attn(qkv, qaug, kaug, w_shards):
    T = qkv.shape[0]
    tq = tk = ATTN_TILE
    nq = T // tq
    gather = _WeightGather([w.shape for w in w_shards], ["cols", "cols", "rows", "cols", "rows"])
    nw = gather.n

    def body(q_ref, qa_ref, k_ref, v_ref, ka_ref, *rest):
        w_refs, (o_ref, lse_ref), wg_refs, scratch = rest[:nw], rest[nw:nw + 2], rest[nw + 2:2 * nw + 2], rest[2 * nw + 2:]
        i = pl.program_id(0)

        @pl.when(i == 0)
        def _():
            gather.start(w_refs, wg_refs, scratch)

        @pl.when(i == nq // 2)
        def _():
            gather.forward(wg_refs, scratch)

        lane = _iota((tq, LANES), 1)
        low = lane < 64
        lowk = _iota((tk, LANES), 1) < 64
        one = jnp.ones((tk, LANES), BF16)
        row = _iota((2 * tq, tk), 0) % tq
        col = _iota((2 * tq, tk), 1)
        cols = [slice(LANES * p, LANES * (p + 1)) for p in range(4)]
        qa = qa_ref[...]
        qs = [_aug_stack(q_ref[:, cols[p]], qa, p) for p in range(4)]

        def step(j, carry, masked):
            ks = pl.ds(pl.multiple_of(j * tk, tk), tk)
            ka = ka_ref[ks, :]
            new = []
            for p in range(4):
                m, acc_e, acc_o = carry[p]
                v2 = v_ref[ks, cols[p]]
                s = _dot_nt(qs[p], jnp.concatenate([k_ref[ks, cols[p]], ka], axis=1))
                if masked:
                    s = jnp.where(col <= row, s, NEG)
                mn = jnp.maximum(m, jnp.max(s, axis=1, keepdims=True))
                al = jnp.exp(m - mn)
                pm = jnp.exp(s - mn).astype(BF16)
                acc_e = al[:tq] * acc_e + _dot(pm[:tq], jnp.where(lowk, v2, one))
                acc_o = al[tq:] * acc_o + _dot(pm[tq:], jnp.where(lowk, one, v2))
                new.append((mn, acc_e, acc_o))
            return tuple(new)

        init = tuple((jnp.full((2 * tq, 1), NEG, F32), jnp.zeros((tq, LANES), F32), jnp.zeros((tq, LANES), F32))
                     for _ in range(4))
        carry = lax.fori_loop(0, i, lambda j, c: step(j, c, False), init)
        carry = step(i, carry, True)
        lse_blk = jnp.zeros((tq, LANES), F32)
        for p in range(4):
            m, acc_e, acc_o = carry[p]
            l_e = pltpu.roll(acc_e, 64, 1)
            l_o = pltpu.roll(acc_o, 64, 1)
            o_ref[:, cols[p]] = jnp.where(low, acc_e / l_e, acc_o / l_o).astype(BF16)
            lse_blk = jnp.where(lane == 2 * p, m[:tq] + jnp.log(l_e), lse_blk)
            lse_blk = jnp.where(lane == 2 * p + 1, m[tq:] + jnp.log(acc_o), lse_blk)
        lse_ref[...] = lse_blk

        @pl.when(i == nq - 1)
        def _():
            gather.finish(wg_refs, scratch)

    return pl.pallas_call(
        body, grid=(nq,),
        out_shape=[jax.ShapeDtypeStruct((T, ATTN_WIDTH), BF16), jax.ShapeDtypeStruct((T, LANES), F32)]
        + gather.out_shapes(),
        in_specs=[_row_spec(tq, 512), _row_spec(tq, LANES),
                  pl.BlockSpec((T, 512), lambda i: (0, 1), pipeline_mode=pl.Buffered(1)),
                  pl.BlockSpec((T, 512), lambda i: (0, 2), pipeline_mode=pl.Buffered(1)),
                  _const_spec((T, LANES))] + [_const_spec(w.shape) for w in w_shards],
        out_specs=[_row_spec(tq, ATTN_WIDTH), _row_spec(tq, LANES)] + [pl.BlockSpec(memory_space=pl.ANY)] * nw,
        scratch_shapes=gather.scratch_shapes(),
        compiler_params=_params(58), name="fwd_attn",
    )(qkv, qaug, qkv, qkv, kaug, *w_shards)


def _fwd_merge(ys, ya, gt, x2, wbs, wba, wo, g2):
    T = x2.shape[0]
    tm = TOKEN_TILE

    def body(ys_ref, ya_ref, gt_ref, x_ref, wbs_ref, wba_ref, wo_ref, g2_ref, a_ref, b_ref, mg_ref, o_ref, h1_ref):
        A = _dot(ys_ref[...], wbs_ref[...])
        B = _dot(ya_ref[...], wba_ref[...])
        mg = (gt_ref[:, :D_MODEL].astype(F32) * A + gt_ref[:, D_MODEL:].astype(F32) * B).astype(BF16)
        o = _dot(mg, wo_ref[...])
        r2, oh = _rms_stats(o)
        a_ref[...] = A.astype(BF16)
        b_ref[...] = B.astype(BF16)
        mg_ref[...] = mg
        o_ref[...] = o.astype(BF16)
        h1_ref[...] = x_ref[...] + oh * g2_ref[...]

    sd = jax.ShapeDtypeStruct((T, D_MODEL), BF16)
    return pl.pallas_call(
        body, grid=(T // tm,),
        out_shape=(sd, sd, sd, sd, jax.ShapeDtypeStruct((T, D_MODEL), F32)),
        in_specs=[_row_spec(tm, 512), _row_spec(tm, 512), _row_spec(tm, 2048), _row_spec(tm, D_MODEL),
                  _const_spec((512, D_MODEL)), _const_spec((512, D_MODEL)), _const_spec((D_MODEL, D_MODEL)),
                  _const_spec((1, D_MODEL))],
        out_specs=[_row_spec(tm, D_MODEL)] * 5,
        compiler_params=_params(40), name="fwd_merge",
    )(ys, ya, gt, x2, wbs, wba, wo, g2)


def _fwd_ffn_loss(h1, tgt, wup, wdn, g3, g4):
    T = h1.shape[0]
    tm = TOKEN_TILE
    nsteps = T // tm

    def body(h1_ref, tg_ref, wup_ref, wdn_ref, g3_ref, g4_ref, xn2_ref, a_ref, ddn_ref, dy_ref, loss_ref,
             dg4_ref, acc_l, acc_g):
        i = pl.program_id(0)

        @pl.when(i == 0)
        def _():
            acc_l[...] = jnp.zeros_like(acc_l)
            acc_g[...] = jnp.zeros_like(acc_g)

        h1v = h1_ref[...]
        r3, h1h = _rms_stats(h1v)
        xn2 = (h1h * g3_ref[...]).astype(BF16)
        xn2_ref[...] = xn2
        dn = jnp.zeros((tm, D_MODEL), F32)
        for j in range(D_FF // 1024):
            cols = slice(1024 * j, 1024 * (j + 1))
            a = _dot(xn2, wup_ref[:, cols])
            a_ref[:, cols] = a.astype(BF16)
            hid = jnp.square(jnp.maximum(a, 0.0)).astype(BF16)
            dn = dn + _dot(hid, wdn_ref[cols, :])
        r4, dnh = _rms_stats(dn)
        g4v = g4_ref[...]
        e = (h1v + dnh * g4v) - tg_ref[...]
        sq = e * e
        s1 = sq[:, 0:LANES]
        for j in range(1, D_MODEL // LANES):
            s1 = s1 + sq[:, LANES * j:LANES * (j + 1)]
        acc_l[...] += _fold8(s1)
        dy = e * (1.0 / D_MODEL)
        dy_ref[...] = dy
        acc_g[...] += _fold8(dy * dnh)
        ddn_ref[...] = _rms_bwd(dy, dnh, r4, g4v).astype(BF16)

        @pl.when(i == nsteps - 1)
        def _():
            loss_ref[...] = acc_l[...] * (0.5 / D_MODEL)
            dg4_ref[...] = jnp.sum(acc_g[...], axis=0, keepdims=True)

    return pl.pallas_call(
        body, grid=(nsteps,),
        out_shape=(jax.ShapeDtypeStruct((T, D_MODEL), BF16), jax.ShapeDtypeStruct((T, D_FF), BF16),
                   jax.ShapeDtypeStruct((T, D_MODEL), BF16), jax.ShapeDtypeStruct((T, D_MODEL), F32),
                   jax.ShapeDtypeStruct((SUBLANES, LANES), F32), jax.ShapeDtypeStruct((1, D_MODEL), F32)),
        in_specs=[_row_spec(tm, D_MODEL), _row_spec(tm, D_MODEL), _const_spec((D_MODEL, D_FF)),
                  _const_spec((D_FF, D_MODEL)), _const_spec((1, D_MODEL)), _const_spec((1, D_MODEL))],
        out_specs=[_row_spec(tm, D_MODEL), _row_spec(tm, D_FF), _row_spec(tm, D_MODEL), _row_spec(tm, D_MODEL),
                   pl.BlockSpec((SUBLANES, LANES), lambda i: (0, 0)), pl.BlockSpec((1, D_MODEL), lambda i: (0, 0))],
        scratch_shapes=[pltpu.VMEM((SUBLANES, LANES), F32), pltpu.VMEM((SUBLANES, D_MODEL), F32)],
        compiler_params=_params(52), name="fwd_ffn_loss",
    )(h1, tgt, wup, wdn, g3, g4)


def _bwd_ffn(ddn, a, dy, h1, wup, wdn, g3):
    T = h1.shape[0]
    tm = TOKEN_TILE
    nsteps = T // tm

    def body(ddn_ref, a_ref, dy_ref, h1_ref, wup_ref, wdn_ref, g3_ref, da_ref, dh1_ref, dg3_ref, acc_g):
        i = pl.program_id(0)

        @pl.when(i == 0)
        def _():
            acc_g[...] = jnp.zeros_like(acc_g)

        ddnv = ddn_ref[...]
        dxn2 = jnp.zeros((tm, D_MODEL), F32)
        for j in range(D_FF // 1024):
            cols = slice(1024 * j, 1024 * (j + 1))
            dhid = _dot_nt(ddnv, wdn_ref[cols, :])
            da = (dhid * (2.0 * jnp.maximum(a_ref[:, cols].astype(F32), 0.0))).astype(BF16)
            da_ref[:, cols] = da
            dxn2 = dxn2 + _dot_nt(da, wup_ref[:, cols])
        r3, h1h = _rms_stats(h1_ref[...])
        acc_g[...] += _fold8(dxn2 * h1h)
        dh1_ref[...] = dy_ref[...] + _rms_bwd(dxn2, h1h, r3, g3_ref[...])

        @pl.when(i == nsteps - 1)
        def _():
            dg3_ref[...] = jnp.sum(acc_g[...], axis=0, keepdims=True)

    return pl.pallas_call(
        body, grid=(nsteps,),
        out_shape=(jax.ShapeDtypeStruct((T, D_FF), BF16), jax.ShapeDtypeStruct((T, D_MODEL), F32),
                   jax.ShapeDtypeStruct((1, D_MODEL), F32)),
        in_specs=[_row_spec(tm, D_MODEL), _row_spec(tm, D_FF), _row_spec(tm, D_MODEL), _row_spec(tm, D_MODEL),
                  _const_spec((D_MODEL, D_FF)), _const_spec((D_FF, D_MODEL)), _const_spec((1, D_MODEL))],
        out_specs=[_row_spec(tm, D_FF), _row_spec(tm, D_MODEL), pl.BlockSpec((1, D_MODEL), lambda i: (0, 0))],
        scratch_shapes=[pltpu.VMEM((SUBLANES, D_MODEL), F32)],
        compiler_params=_params(52), name="bwd_ffn",
    )(ddn, a, dy, h1, wup, wdn, g3)


def _wgrad(xa, dy, name, relu2=False, tn=None, block_cols=None):
    T, K = xa.shape
    N = dy.shape[1]
    tn = N if tn is None else tn
    tt = min(T, WGRAD_TILE if K <= D_MODEL else WGRAD_TILE // 2)
    if block_cols:
        nb = tn // block_cols
        out_shape = jax.ShapeDtypeStruct((N // block_cols, K, block_cols), F32)
        out_spec = pl.BlockSpec((nb, K, block_cols), lambda n, t: (n, 0, 0))
    else:
        out_shape = jax.ShapeDtypeStruct((K, N), F32)
        out_spec = pl.BlockSpec((K, tn), lambda n, t: (0, n))

    def body(x_ref, dy_ref, o_ref):
        @pl.when(pl.program_id(1) == 0)
        def _():
            o_ref[...] = jnp.zeros_like(o_ref)

        xv = x_ref[...]
        if relu2:
            xv = jnp.square(jnp.maximum(xv.astype(F32), 0.0)).astype(BF16)
        if block_cols:
            for b in range(nb):
                o_ref[b] += _dot_tn(xv, dy_ref[:, block_cols * b:block_cols * (b + 1)])
        else:
            o_ref[...] += _dot_tn(xv, dy_ref[...])

    return pl.pallas_call(
        body, grid=(N // tn, T // tt), out_shape=out_shape,
        in_specs=[pl.BlockSpec((tt, K), lambda n, t: (t, 0)), pl.BlockSpec((tt, tn), lambda n, t: (t, n))],
        out_specs=out_spec,
        compiler_params=_params(52, 2), name=name,
    )(xa, dy)


def _bwd_merge(dh1, o, A, B, gt, ya, lse, cc, wbs, wba, wo, g2):
    T = dh1.shape[0]
    tm = TOKEN_TILE
    nsteps = T // tm

    def body(dh1_ref, o_ref, a_ref, b_ref, gt_ref, ya_ref, lse_ref, cc_ref, wbs_ref, wba_ref, wo_ref, g2_ref,
             do_ref, da_ref, db_ref, dgl_ref, dys_ref, dya_ref, qab_ref, dab_ref, dg2_ref, acc_g):
        i = pl.program_id(0)

        @pl.when(i == 0)
        def _():
            acc_g[...] = jnp.zeros_like(acc_g)

        dh1v = dh1_ref[...]
        r2, oh = _rms_stats(o_ref[...].astype(F32))
        acc_g[...] += _fold8(dh1v * oh)
        do = _rms_bwd(dh1v, oh, r2, g2_ref[...]).astype(BF16)
        do_ref[...] = do
        dmg = _dot_nt(do, wo_ref[...])
        ga = gt_ref[:, :D_MODEL].astype(F32)
        gb = gt_ref[:, D_MODEL:].astype(F32)
        dgl_ref[:, :D_MODEL] = (dmg * a_ref[...].astype(F32) * ga * (1.0 - ga)).astype(BF16)
        dgl_ref[:, D_MODEL:] = (dmg * b_ref[...].astype(F32) * gb * (1.0 - gb)).astype(BF16)
        dA = (dmg * ga).astype(BF16)
        dB = (dmg * gb).astype(BF16)
        da_ref[...] = dA
        db_ref[...] = dB
        dys_ref[...] = _dot_nt(dA, wbs_ref[...]).astype(BF16)
        dya = _dot_nt(dB, wba_ref[...]).astype(BF16)
        dya_ref[...] = dya
        prod = dya.astype(F32) * ya_ref[...].astype(F32)
        lane = _iota((tm, LANES), 1)
        low = lane < 64
        blk = jnp.zeros((tm, LANES), F32)
        for p in range(4):
            pp = prod[:, LANES * p:LANES * (p + 1)]
            blk = jnp.where(lane == 2 * p, jnp.sum(jnp.where(low, pp, 0.0), axis=1, keepdims=True), blk)
            blk = jnp.where(lane == 2 * p + 1, jnp.sum(jnp.where(low, 0.0, pp), axis=1, keepdims=True), blk)
        qab_ref[...] = _aug_query(cc_ref[...] - lse_ref[...])
        dab_ref[...] = _spread(_split3(-blk), 0).astype(BF16)

        @pl.when(i == nsteps - 1)
        def _():
            dg2_ref[...] = jnp.sum(acc_g[...], axis=0, keepdims=True)

    sd = jax.ShapeDtypeStruct((T, D_MODEL), BF16)
    sh = jax.ShapeDtypeStruct((T, 512), BF16)
    sa = jax.ShapeDtypeStruct((T, LANES), BF16)
    return pl.pallas_call(
        body, grid=(nsteps,),
        out_shape=(sd, sd, sd, jax.ShapeDtypeStruct((T, 2048), BF16), sh, sh, sa, sa,
                   jax.ShapeDtypeStruct((1, D_MODEL), F32)),
        in_specs=[_row_spec(tm, D_MODEL)] * 4 + [_row_spec(tm, 2048), _row_spec(tm, 512), _row_spec(tm, LANES),
                  _row_spec(tm, LANES), _const_spec((512, D_MODEL)), _const_spec((512, D_MODEL)),
                  _const_spec((D_MODEL, D_MODEL)), _const_spec((1, D_MODEL))],
        out_specs=[_row_spec(tm, D_MODEL)] * 3 + [_row_spec(tm, 2048), _row_spec(tm, 512), _row_spec(tm, 512),
                   _row_spec(tm, LANES), _row_spec(tm, LANES), pl.BlockSpec((1, D_MODEL), lambda i: (0, 0))],
        scratch_shapes=[pltpu.VMEM((SUBLANES, D_MODEL), F32)],
        compiler_params=_params(40), name="bwd_merge",
    )(dh1, o, A, B, gt, ya, lse, cc, wbs, wba, wo, g2)


def _bwd_sgu(zuv, dys, gs, bs, wsp, bT, grads):
    T = zuv.shape[0]
    tc = SGU_TILE
    nc = tc // CHUNK
    nsteps = T // tc
    ex = _GradExchange([tuple(g.shape[1:]) for g in grads])
    ng = ex.n

    def body(z_ref, dy_ref, gs_ref, bs_ref, w_ref, bT_ref, *rest):
        g_refs, (dz_ref, dw_ref, dbT_ref, dgs_ref, dbs_ref) = rest[:ng], rest[ng:ng + 5]
        land1 = rest[ng + 5:2 * ng + 5]
        acc_w, acc_b, acc_gs, acc_bs, dvln_s = rest[2 * ng + 5:2 * ng + 10]
        ex_sems = rest[2 * ng + 10:]
        i = pl.program_id(0)

        @pl.when(i == 0)
        def _():
            ex.start(1, g_refs, land1, ex_sems)
            acc_w[...] = jnp.zeros_like(acc_w)
            acc_b[...] = jnp.zeros_like(acc_b)
            acc_gs[...] = jnp.zeros_like(acc_gs)
            acc_bs[...] = jnp.zeros_like(acc_bs)

        z = z_ref[...].astype(F32)
        gsv = gs_ref[...]
        u, vhat, rs, vln = _sgu_forward_parts(z, gsv, bs_ref[...])
        vb = vln.astype(BF16)
        dy = dy_ref[...].astype(F32)
        low_w = (_iota((CHUNK, nc * LANES), 1) % LANES) < 64
        for p in range(4):
            we, wo, b2 = _sgu_pair_weights(w_ref, bT_ref[...], p)
            vcat = _chunks_on_lanes(vb, p, nc)
            s = _sgu_mix(we, wo, b2, vcat, nc)
            dyc = _chunks_on_lanes(dy, p, nc)
            ds = dyc * _chunks_on_lanes(u, p, nc)
            dsb = ds.astype(BF16)
            zero = jnp.zeros_like(dsb)
            dse = jnp.where(low_w, dsb, zero)
            dso = jnp.where(low_w, zero, dsb)
            acc_w[2 * p] += _dot_nt(dse, vcat)
            acc_w[2 * p + 1] += _dot_nt(dso, vcat)
            acc_b[p] += ds
            dvl = jnp.where(low_w, _dot_tn(we, dsb), _dot_tn(wo, dsb))
            for c in range(nc):
                rows, cols = slice(c * CHUNK, (c + 1) * CHUNK), slice(LANES * p, LANES * (p + 1))
                dvln_s[rows, cols] = dvl[:, c * LANES:(c + 1) * LANES]
                du = dy[rows, cols] * s[:, c * LANES:(c + 1) * LANES]
                dz_ref[rows, cols] = (du * _gelu_grad(z[rows, cols])).astype(BF16)
        dvln = dvln_s[...]
        acc_gs[...] += _fold8(dvln * vhat)
        acc_bs[...] += _fold8(dvln)
        al = dvln * gsv
        dvv = rs * (al - jnp.mean(al, axis=1, keepdims=True) - vhat * jnp.mean(al * vhat, axis=1, keepdims=True))
        dz_ref[:, SGU_WIDTH:] = (dvv * _gelu_grad(z[:, SGU_WIDTH:])).astype(BF16)

        @pl.when(i == nsteps - 1)
        def _():
            tril = _iota((CHUNK, CHUNK), 0) >= _iota((CHUNK, CHUNK), 1)
            lane = _iota((CHUNK, LANES), 1)
            low = lane < 64
            blk = jnp.zeros((CHUNK, LANES), F32)
            for g in range(8):
                dw_ref[g] = jnp.where(tril, acc_w[g], 0.0)
            for p in range(4):
                t = acc_b[p]
                tot = t[:, 0:LANES]
                for c in range(1, nc):
                    tot = tot + t[:, c * LANES:(c + 1) * LANES]
                blk = jnp.where(lane == 2 * p, jnp.sum(jnp.where(low, tot, 0.0), axis=1, keepdims=True), blk)
                blk = jnp.where(lane == 2 * p + 1, jnp.sum(jnp.where(low, 0.0, tot), axis=1, keepdims=True), blk)
            dbT_ref[...] = blk
            dgs_ref[...] = jnp.sum(acc_gs[...], axis=0, keepdims=True)
            dbs_ref[...] = jnp.sum(acc_bs[...], axis=0, keepdims=True)
            ex.wait(1, g_refs, land1, ex_sems)

    whole = lambda shape: pl.BlockSpec(shape, lambda i: (0,) * len(shape))
    hbm_spec = pl.BlockSpec(memory_space=pl.ANY)
    return pl.pallas_call(
        body, grid=(nsteps,),
        out_shape=[jax.ShapeDtypeStruct((T, 1024), BF16), jax.ShapeDtypeStruct((8, CHUNK, CHUNK), F32),
                   jax.ShapeDtypeStruct((CHUNK, LANES), F32), jax.ShapeDtypeStruct((1, SGU_WIDTH), F32),
                   jax.ShapeDtypeStruct((1, SGU_WIDTH), F32)] + ex.land_shapes(1),
        in_specs=[_row_spec(tc, 1024), _row_spec(tc, SGU_WIDTH), _const_spec((1, SGU_WIDTH)),
                  _const_spec((1, SGU_WIDTH)), _const_spec((8, CHUNK, CHUNK)), _const_spec((CHUNK, 8))]
        + [hbm_spec] * ng,
        out_specs=[_row_spec(tc, 1024), whole((8, CHUNK, CHUNK)), whole((CHUNK, LANES)),
                   whole((1, SGU_WIDTH)), whole((1, SGU_WIDTH))] + [hbm_spec] * ng,
        scratch_shapes=[pltpu.VMEM((8, CHUNK, CHUNK), F32), pltpu.VMEM((4, CHUNK, nc * LANES), F32),
                        pltpu.VMEM((SUBLANES, SGU_WIDTH), F32), pltpu.VMEM((SUBLANES, SGU_WIDTH), F32),
                        pltpu.VMEM((tc, SGU_WIDTH), F32)] + ex.sem_shapes(1),
        compiler_params=_params(48), name="bwd_sgu",
    )(zuv, dys, gs, bs, wsp, bT, *grads)


def _bwd_attn(qkv, dya, qab, dab, kaug, parts):
    T = qkv.shape[0]
    tq = tk = ATTN_TILE
    nq = T // tq
    nk = T // tk
    ex = _GradExchange([tuple(g.shape[1:]) for g in parts])
    nr = ex.n

    def body(q_ref, do_ref, qa_ref, da_ref, k_ref, v_ref, ka_ref, *rest):
        part_refs, (dq_ref, dk_ref, dv_ref, dcx_ref) = rest[:nr], rest[nr:nr + 4]
        land2, dq_acc, ex_sems = rest[nr + 4:2 * nr + 4], rest[2 * nr + 4], rest[2 * nr + 5:]
        p = pl.program_id(0)
        j = pl.program_id(1)

        @pl.when((p == 0) & (j == 0))
        def _():
            ex.start(2, part_refs, land2, ex_sems)

        lane = _iota((tq, LANES), 1)
        low = lane < 64
        row = _iota((2 * tq, tk), 0) % tq
        col = _iota((2 * tq, tk), 1)
        first = 2 * AUG_LANES * p

        @pl.when(j == 0)
        def _():
            dq_acc[...] = jnp.zeros_like(dq_acc)

        @pl.when((j == 0) & (p == 0))
        def _():
            dcx_ref[...] = jnp.zeros_like(dcx_ref)

        ka = ka_ref[...]
        kk = jnp.concatenate([k_ref[...], ka], axis=1)
        vv = jnp.concatenate([v_ref[...], ka], axis=1)

        def q_block(i, carry, masked):
            dk_a, dv_a = carry
            qsl = pl.ds(pl.multiple_of(i * tq, tq), tq)
            qs = _aug_stack(q_ref[qsl, :], qa_ref[qsl, :], p)
            dos = _aug_stack(do_ref[qsl, :], da_ref[qsl, :], p)
            s = _dot_nt(qs, kk)
            if masked:
                s = jnp.where(col <= row, s, NEG)
            pm = jnp.exp(s)
            ds = pm * _dot_nt(dos, vv)
            dsb = ds.astype(BF16)
            dv_a = dv_a + _dot_tn(pm.astype(BF16), dos[:, :LANES])
            dk_a = dk_a + _dot_tn(dsb, qs)
            dqx = _dot(dsb, kk)
            dq_acc[qsl, :] += jnp.where(low, dqx[:tq, :LANES], dqx[tq:, :LANES])
            dcx_ref[qsl, :] += (jnp.where(lane == first, dqx[:tq, LANES:], 0.0)
                                + jnp.where(lane == first + AUG_LANES, dqx[tq:, LANES:], 0.0))
            return dk_a, dv_a

        init = (jnp.zeros((tk, 2 * LANES), F32), jnp.zeros((tk, LANES), F32))
        carry = q_block(j, init, True)
        dk_a, dv_a = lax.fori_loop(j + 1, nq, lambda i, c: q_block(i, c, False), carry)
        dk_ref[...] = dk_a[:, :LANES].astype(BF16)
        dv_ref[...] = dv_a.astype(BF16)
        ksl = pl.ds(pl.multiple_of(j * tk, tk), tk)
        lk = _iota((tk, LANES), 1)
        dcx_ref[ksl, :] += jnp.where((lk == first + 3) | (lk == first + AUG_LANES + 3), dk_a[:, LANES:], 0.0)

        @pl.when(j == nk - 1)
        def _():
            dq_ref[...] = (dq_acc[...] * 0.125).astype(BF16)

        @pl.when((p == 3) & (j == nk - 1))
        def _():
            ex.wait(2, part_refs, land2, ex_sems)

    sh = jax.ShapeDtypeStruct((T, ATTN_WIDTH), BF16)
    full = lambda cb: pl.BlockSpec((T, LANES), lambda p, j: (0, cb + p))
    blk = lambda cb: pl.BlockSpec((tk, LANES), lambda p, j: (j, cb + p))
    hbm_spec = pl.BlockSpec(memory_space=pl.ANY)
    return pl.pallas_call(
        body, grid=(4, nk),
        out_shape=[sh, sh, sh, jax.ShapeDtypeStruct((T, LANES), F32)] + ex.land_shapes(2),
        in_specs=[full(0), full(0), _const_spec((T, LANES)), _const_spec((T, LANES)), blk(4), blk(8),
                  pl.BlockSpec((tk, LANES), lambda p, j: (j, 0))] + [hbm_spec] * nr,
        out_specs=[full(0), blk(0), blk(0), pl.BlockSpec((T, LANES), lambda p, j: (0, 0))] + [hbm_spec] * nr,
        scratch_shapes=[pltpu.VMEM((T, LANES), F32)] + ex.sem_shapes(2),
        compiler_params=_params(58, 2), name="bwd_attn",
    )(qkv, dya, qab, dab, qkv, qkv, kaug, *parts)


def _bwd_cum(dcx, fl, bfp):
    T = fl.shape[0]
    tb = CUM_TILE

    def body(dcx_ref, fl_ref, b_ref, dfl_ref, dbf_ref):
        triu = (_iota((tb, tb), 0) <= _iota((tb, tb), 1)).astype(F32)
        r, c = _iota((LANES, LANES), 0), _iota((LANES, LANES), 1)
        sel = (((r == AUG_LANES * c) & (c < N_HEADS)).astype(F32)
               - ((r == AUG_LANES * c + 3) & (c < N_HEADS)).astype(F32))
        carry = jnp.zeros((1, LANES), F32)
        dbf = jnp.zeros((1, LANES), F32)
        for i in reversed(range(T // tb)):
            colblk = jnp.dot(dcx_ref[i * tb:(i + 1) * tb, :], sel, precision=HIGHEST, preferred_element_type=F32)
            rc = jnp.dot(triu, colblk, precision=HIGHEST, preferred_element_type=F32) + carry
            carry = rc[0:1, :]
            sig = jax.nn.sigmoid(fl_ref[i * tb:(i + 1) * tb, :] + b_ref[...])
            dfl = rc * (1.0 - sig)
            dfl_ref[i * tb:(i + 1) * tb, :] = dfl.astype(BF16)
            dbf = dbf + jnp.sum(dfl, axis=0, keepdims=True)
        dbf_ref[...] = dbf

    return pl.pallas_call(
        body,
        out_shape=(jax.ShapeDtypeStruct((T, LANES), BF16), jax.ShapeDtypeStruct((1, LANES), F32)),
        compiler_params=pltpu.CompilerParams(vmem_limit_bytes=32 * MIB), name="bwd_cum",
    )(dcx, fl, bfp)


def _bwd_in(dz, dqkvf, dgl, dh1, x2, g1, wz, wf, wg, rows, name, prev=None, stage=0, exchanged=None):
    T = x2.shape[0]
    tm = TOKEN_TILE
    first = rows[0] // tm
    nsteps = (rows[1] - rows[0]) // tm
    ex = _GradExchange([tuple(exchanged.shape[1:])]) if stage else None

    def body(dz_ref, dqkvf_ref, dgl_ref, dh1_ref, x_ref, g_ref, wz_ref, wf_ref, wg_ref, *rest):
        rest = list(rest)
        dx_prev, dg1_prev = (rest.pop(0), rest.pop(0)) if prev else (None, None)
        src_ref = rest.pop(0) if stage else None
        dx_ref, dg1_ref = rest.pop(0), rest.pop(0)
        land_ref = rest.pop(0) if stage else None
        acc_g, ex_sems = rest[0], rest[1:]
        i = pl.program_id(0)

        @pl.when(i == 0)
        def _():
            if stage:
                ex.start(stage, [src_ref], [land_ref], ex_sems)
            acc_g[...] = jnp.zeros_like(acc_g)

        dxn = _dot_nt(dz_ref[...], wz_ref[:, 0:1024])
        dxn = dxn + _dot_nt(dqkvf_ref[:, 0:1536], wz_ref[:, 1024:2560])
        dxn = dxn + _dot_nt(dqkvf_ref[:, 1536:1664], wf_ref[...])
        dxn = dxn + _dot_nt(dgl_ref[...], wg_ref[...])
        r1, xh = _rms_stats(x_ref[...])
        acc_g[...] += _fold8(dxn * xh)
        dx_ref[...] = dh1_ref[...] + _rms_bwd(dxn, xh, r1, g_ref[...])

        @pl.when(i == nsteps - 1)
        def _():
            total = jnp.sum(acc_g[...], axis=0, keepdims=True)
            dg1_ref[...] = total + dg1_prev[...] if prev else total
            if stage:
                ex.wait(stage, [src_ref], [land_ref], ex_sems)

    hbm_spec = pl.BlockSpec(memory_space=pl.ANY)
    rows_spec = lambda n: pl.BlockSpec((tm, n), lambda i: (i + first, 0))
    operands = [dz, dqkvf, dgl, dh1, x2, g1, wz, wf, wg]
    in_specs = [rows_spec(1024), rows_spec(1664), rows_spec(2048), rows_spec(D_MODEL), rows_spec(D_MODEL),
                _const_spec((1, D_MODEL)), _const_spec((D_MODEL, ZQKV_WIDTH)), _const_spec((D_MODEL, LANES)),
                _const_spec((D_MODEL, 2048))]
    aliases = {}
    if prev:
        aliases = {len(operands): 0}
        operands += list(prev)
        in_specs += [hbm_spec, _const_spec((1, D_MODEL))]
    if stage:
        operands.append(exchanged)
        in_specs.append(hbm_spec)
    return pl.pallas_call(
        body, grid=(nsteps,),
        out_shape=[jax.ShapeDtypeStruct((T, D_MODEL), F32), jax.ShapeDtypeStruct((1, D_MODEL), F32)]
        + (ex.land_shapes(stage) if stage else []),
        in_specs=in_specs,
        out_specs=[rows_spec(D_MODEL), pl.BlockSpec((1, D_MODEL), lambda i: (0, 0))] + ([hbm_spec] if stage else []),
        scratch_shapes=[pltpu.VMEM((SUBLANES, D_MODEL), F32)] + (ex.sem_shapes(stage) if stage else []),
        input_output_aliases=aliases,
        compiler_params=_params(48), name=name,
    )(*operands)


def _pad_rows(v, rows):
    return jnp.pad(v, ((0, rows - v.shape[0]), (0, 0)))


def _pack_small(g_mix_pre, b_forget, g_sgu, b_sgu, w_spatial, b_spatial, g_mix_post, g_ffn_pre, g_ffn_post):
    vec = lambda v: _pad_rows(v.reshape(-1, LANES), SUBLANES)
    return jnp.concatenate([
        w_spatial.reshape(-1, LANES), vec(g_mix_pre), _pad_rows(jnp.pad(b_forget, ((0, 0), (0, LANES - N_HEADS))), SUBLANES),
        vec(g_sgu), vec(b_sgu), vec(b_spatial), vec(g_mix_post), vec(g_ffn_pre), vec(g_ffn_post)], axis=0)


def _unpack_small(p):
    nw = N_HEADS * CHUNK * CHUNK // LANES
    blk = lambda k: p[nw + SUBLANES * k: nw + SUBLANES * (k + 1)]
    return dict(
        w_spatial=p[:nw].reshape(1, N_HEADS, CHUNK, CHUNK),
        g_mix_pre=blk(0).reshape(1, D_MODEL), b_forget=blk(1)[0:1, :N_HEADS],
        g_sgu=blk(2)[:4].reshape(1, SGU_WIDTH), b_sgu=blk(3)[:4].reshape(1, SGU_WIDTH),
        b_spatial=blk(4).reshape(1, N_HEADS, CHUNK), g_mix_post=blk(5).reshape(1, D_MODEL),
        g_ffn_pre=blk(6).reshape(1, D_MODEL), g_ffn_post=blk(7).reshape(1, D_MODEL))


def kernel(x, g_mix_pre, w_in, b_forget, g_sgu, b_sgu, w_spatial, b_spatial, w_branch_sgu, w_branch_attn, w_out, g_mix_post, g_ffn_pre, w_up, w_down, g_ffn_post, loss_target, m_g_mix_pre, m_w_in, m_b_forget, m_g_sgu, m_b_sgu, m_w_spatial, m_b_spatial, m_w_branch_sgu, m_w_branch_attn, m_w_out, m_g_mix_post, m_g_ffn_pre, m_w_up, m_w_down, m_g_ffn_post, v_g_mix_pre, v_w_in, v_b_forget, v_g_sgu, v_b_sgu, v_w_spatial, v_b_spatial, v_w_branch_sgu, v_w_branch_attn, v_w_out, v_g_mix_post, v_g_ffn_pre, v_w_up, v_w_down, v_g_ffn_post):
    T = x.shape[1]
    x2 = x.reshape(T, D_MODEL)
    tgt = loss_target.reshape(T, D_MODEL)

    wg_in = _gather_w_in(w_in[0])
    w_in_full = jnp.concatenate([wg_in[j, :, :IN_SHARD] for j in range(N_DEV)], axis=1)
    wz = w_in_full[:, :ZQKV_WIDTH]
    wf = jnp.pad(w_in_full[:, ZQKV_WIDTH:GATE_OFFSET], ((0, 0), (0, LANES - N_HEADS)))
    wgt = w_in_full[:, GATE_OFFSET:]
    bfp = jnp.pad(b_forget, ((0, 0), (0, LANES - N_HEADS)))
    wsp = w_spatial[0]
    bT = b_spatial[0].T

    xn, zuv, qkv, fl, gt = _fwd_in(x2, g_mix_pre, wz, wf, wgt)
    cc, qaug, kaug = _fwd_cum(fl, bfp)
    ys = _fwd_sgu(zuv, g_sgu, b_sgu, wsp, bT)
    ya, lse, wbs, wba, wo, wup, wdn = _fwd_attn(
        qkv, qaug, kaug, (w_branch_sgu[0], w_branch_attn[0], w_out[0], w_up[0], w_down[0]))
    A, B, mg, o, h1 = _fwd_merge(ys, ya, gt, x2, wbs, wba, wo, g_mix_post)
    xn2, a, ddn, dy, loss_part, dg4 = _fwd_ffn_loss(h1, tgt, wup, wdn, g_ffn_pre, g_ffn_post)
    loss = lax.psum(jnp.sum(loss_part), ("x", "y", "c"))

    da, dh1, dg3 = _bwd_ffn(ddn, a, dy, h1, wup, wdn, g_ffn_pre)
    dw_up = _wgrad(xn2, da, "wgrad_up", tn=2048, block_cols=512)
    dw_down = _wgrad(a, ddn, "wgrad_down", relu2=True)
    do, dA, dB, dgl, dys, dya, qab, dab, dg2 = _bwd_merge(dh1, o, A, B, gt, ya, lse, cc, wbs, wba, wo, g_mix_post)
    dw_out = _wgrad(mg, do, "wgrad_out")
    dw_bs = _wgrad(ys, dA, "wgrad_branch_sgu")
    dw_ba = _wgrad(ya, dB, "wgrad_branch_attn")
    col_blocks = lambda g, w: g.reshape(g.shape[0], N_DEV, w).transpose(1, 0, 2)
    row_blocks = lambda g, r: g.reshape(N_DEV, r, g.shape[1])
    early_names = ["w_branch_sgu", "w_branch_attn", "w_out", "w_up", "w_down"]
    early = [col_blocks(dw_bs, 128), col_blocks(dw_ba, 128), row_blocks(dw_out, 128), dw_up, row_blocks(dw_down, 512)]
    owners = _owner_indices()
    dzuv, dwsp, dbT, dgs, dbs, *early_land1 = _bwd_sgu(zuv, dys, g_sgu, b_sgu, wsp, bT, early)
    early_parts = [_chip_partials(g, l1, owners, "chip_partials_" + nm)
                   for g, l1, nm in zip(early, early_land1, early_names)]
    dq, dk, dv, dcx, *early_land2 = _bwd_attn(qkv, dya, qab, dab, kaug, early_parts)
    dfl, dbf = _bwd_cum(dcx, fl, bfp)
    dqkvf = jnp.concatenate([dq, dk, dv, dfl], axis=1)
    dw_z = _wgrad(xn, dzuv, "wgrad_in_z")
    dw_qkvf = _wgrad(xn, dqkvf, "wgrad_in_qkvf")
    dw_g = _wgrad(xn, dgl, "wgrad_in_gate")
    dw_in = jnp.concatenate([dw_z, dw_qkvf[:, :3 * ATTN_WIDTH + N_HEADS], dw_g], axis=1)
    blocks_in = jnp.stack([jnp.pad(dw_in[:, IN_SHARD * j:IN_SHARD * (j + 1)], ((0, 0), (0, IN_SHARD_PAD - IN_SHARD)))
                           for j in range(N_DEV)])
    bwd_in_args = (dzuv, dqkvf, dgl, dh1, x2, g_mix_pre, wz, wf, wgt)
    dx, dg1, land1_in = _bwd_in(*bwd_in_args, (0, T // 4), "bwd_in_a", stage=1, exchanged=blocks_in)
    part_in = _chip_partials(blocks_in, land1_in, owners, "chip_partials_w_in")
    dx, dg1, land2_in = _bwd_in(*bwd_in_args, (T // 4, 3 * T // 4), "bwd_in_b", prev=(dx, dg1), stage=2,
                                exchanged=part_in)
    dx, dg1 = _bwd_in(*bwd_in_args, (3 * T // 4, T), "bwd_in_c", prev=(dx, dg1))

    small_g = _pack_small(dg1, dbf[:, :N_HEADS], dgs, dbs, dwsp[None], dbT[:, :N_HEADS].T[None], dg2, dg3, dg4)
    small_w = _pack_small(g_mix_pre, b_forget, g_sgu, b_sgu, w_spatial, b_spatial, g_mix_post, g_ffn_pre, g_ffn_post)
    small_m = _pack_small(m_g_mix_pre, m_b_forget, m_g_sgu, m_b_sgu, m_w_spatial, m_b_spatial, m_g_mix_post,
                          m_g_ffn_pre, m_g_ffn_post)
    small_v = _pack_small(v_g_mix_pre, v_b_forget, v_g_sgu, v_b_sgu, v_w_spatial, v_b_spatial, v_g_mix_post,
                          v_g_ffn_pre, v_g_ffn_post)
    sg, sd, sm, sv = (_unpack_small(t) for t in _allreduce_small_adamw(small_g, small_w, small_m, small_v))

    big = {}
    g_in = _reduced_grad(blocks_in, land1_in, land2_in, owners, "reduced_grad_w_in")[:, :IN_SHARD]
    d_, m_, v_ = _adamw(w_in[0], g_in, m_w_in[0], v_w_in[0], "adamw_w_in")
    big["w_in"] = (g_in[None], d_[None], m_[None], v_[None])
    early_wmv = [(w_branch_sgu, m_w_branch_sgu, v_w_branch_sgu), (w_branch_attn, m_w_branch_attn, v_w_branch_attn),
                 (w_out, m_w_out, v_w_out), (w_up, m_w_up, v_w_up), (w_down, m_w_down, v_w_down)]
    for nm, (w, m, v), g, l1, l2 in zip(early_names, early_wmv, early, early_land1, early_land2):
        big[nm] = tuple(t[None] for t in _adamw_reduced(w[0], m[0], v[0], g, l1, l2, owners, "adamw_" + nm))

    order = ["g_mix_pre", "w_in", "b_forget", "g_sgu", "b_sgu", "w_spatial", "b_spatial", "w_branch_sgu",
             "w_branch_attn", "w_out", "g_mix_post", "g_ffn_pre", "w_up", "w_down", "g_ffn_post"]
    outs = [loss, dx.reshape(1, T, D_MODEL)]
    for kind, small in enumerate((sg, sd, sm, sv)):
        outs += [big[nm][kind] if nm in big else small[nm] for nm in order]
    return tuple(outs)
```

```python
import jax
import jax.numpy as jnp
from jax import lax
from jax.experimental import pallas as pl
from jax.experimental.pallas import tpu as pltpu

F32 = jnp.float32
BF16 = jnp.bfloat16
HIGHEST = lax.Precision.HIGHEST
MESH = pl.DeviceIdType.MESH

D_MODEL = 1024
SGU_WIDTH = 512
ATTN_WIDTH = 512
N_HEADS = 8
CHUNK = 128
D_FF = 4096
IN_WIDTH = 4616
N_DEV = 8
IN_SHARD = IN_WIDTH // N_DEV
IN_SHARD_PAD = 640
ZQKV_WIDTH = 2 * SGU_WIDTH + 3 * ATTN_WIDTH
GATE_OFFSET = ZQKV_WIDTH + N_HEADS
EPS = 1e-6
LANES = 128
SUBLANES = 8
VMEM_BYTES = 64 * 1024 * 1024
MIB = 1024 * 1024

ADAM_LR = 0.001
ADAM_B1 = 0.9
ADAM_B2 = 0.999
ADAM_EPS = 1e-08
ADAM_WD = 0.01
ADAM_STEP = 10

TOKEN_TILE = 256
MERGE_BWD_TILE = 512
ATTN_TILE = 512
CUM_TILE = 256
SGU_TILE = 512
WGRAD_TILE = 1024
NEG = -1e30

NT_DIMS = (((1,), (1,)), ((), ()))
TN_DIMS = (((0,), (0,)), ((), ()))


def _params(vmem_mb, n_grid=1):
    return pltpu.CompilerParams(
        dimension_semantics=("arbitrary",) * n_grid,
        vmem_limit_bytes=min(vmem_mb * MIB, VMEM_BYTES - 6 * MIB),
    )


def _dot(a, b):
    return jnp.dot(a, b, preferred_element_type=F32)


def _dot_nt(a, b):
    return lax.dot_general(a, b, NT_DIMS, preferred_element_type=F32)


def _dot_tn(a, b):
    return lax.dot_general(a, b, TN_DIMS, preferred_element_type=F32)


def _const_spec(shape):
    nd = len(shape)
    return pl.BlockSpec(shape, lambda *_: (0,) * nd, pipeline_mode=pl.Buffered(1))


def _row_spec(tm, n, col=0):
    return pl.BlockSpec((tm, n), lambda i: (i, col))


def _fold8(v):
    return v.reshape(v.shape[0] // SUBLANES, SUBLANES, v.shape[1]).sum(axis=0)


def _pick(v, lane_iota, k):
    return jnp.sum(jnp.where(lane_iota == k, v, 0.0), axis=1, keepdims=True)


def _iota(shape, dim):
    return lax.broadcasted_iota(jnp.int32, shape, dim)


def _gelu(x):
    c = 0.7978845608028654
    return 0.5 * x * (1.0 + jnp.tanh(c * (x + 0.044715 * x * x * x)))


def _gelu_grad(x):
    c = 0.7978845608028654
    t = jnp.tanh(c * (x + 0.044715 * x * x * x))
    return 0.5 * (1.0 + t) + 0.5 * x * (1.0 - t * t) * (c * (1.0 + 3.0 * 0.044715 * x * x))


def _rms_stats(v):
    r = lax.rsqrt(jnp.mean(v * v, axis=1, keepdims=True) + EPS)
    return r, v * r


def _rms_bwd(dout, vhat, r, g):
    a = dout * g
    return r * (a - vhat * jnp.mean(a * vhat, axis=1, keepdims=True))


def _mesh_pos():
    return lax.axis_index("x"), lax.axis_index("y"), lax.axis_index("c")


def _dev_index(px, py, pc):
    return 4 * px + 2 * py + pc


def _other_chips(x, y):
    return [(1 - x, y), (x, 1 - y), (1 - x, 1 - y)]


class _WeightGather:
    def __init__(self, shard_shapes, kinds, stage_shapes=None):
        self.shard_shapes = list(shard_shapes)
        self.kinds = list(kinds)
        self.stage_shapes = list(stage_shapes or shard_shapes)
        self.n = len(self.kinds)

    def out_shapes(self):
        shapes = []
        for (rows, cols), kind in zip(self.stage_shapes, self.kinds):
            full = {"block": (N_DEV, rows, cols), "rows": (N_DEV * rows, cols), "cols": (rows, N_DEV * cols)}[kind]
            shapes.append(jax.ShapeDtypeStruct(full, BF16))
        return shapes

    def scratch_shapes(self):
        return ([pltpu.VMEM(s, BF16) for s in self.stage_shapes]
                + [pltpu.SemaphoreType.DMA((self.n, 7)), pltpu.SemaphoreType.DMA((self.n, 7)),
                   pltpu.SemaphoreType.DMA((self.n,))])

    def _view(self, a, ref, j):
        rows, cols = self.stage_shapes[a]
        if self.kinds[a] == "block":
            return ref.at[j]
        if self.kinds[a] == "rows":
            return ref.at[pl.ds(pl.multiple_of(j * rows, rows), rows), :]
        return ref.at[:, pl.ds(pl.multiple_of(j * cols, cols), cols)]

    def _copy(self, outs, scratch, a, k, block, to, from_stage=False):
        stage, (send_sems, recv_sems, _) = scratch[:self.n], scratch[self.n:]
        dst = self._view(a, outs[a], _dev_index(*block))
        return pltpu.make_async_remote_copy(
            src_ref=stage[a] if from_stage else dst, dst_ref=dst,
            send_sem=send_sems.at[a, k], recv_sem=recv_sems.at[a, k],
            device_id=to, device_id_type=MESH)

    def _local(self, outs, scratch, a, me):
        return pltpu.make_async_copy(scratch[a], self._view(a, outs[a], _dev_index(*me)), scratch[-1].at[a])

    def start(self, ins, outs, scratch):
        x, y, c = _mesh_pos()
        me, sibling = (x, y, c), (x, y, 1 - c)
        for a in range(self.n):
            rows, cols = self.shard_shapes[a]
            if self.stage_shapes[a] != self.shard_shapes[a]:
                scratch[a][...] = jnp.zeros(self.stage_shapes[a], BF16)
            scratch[a][0:rows, 0:cols] = ins[a][...].astype(BF16)
            self._local(outs, scratch, a, me).start()
        for a in range(self.n):
            self._copy(outs, scratch, a, 0, me, sibling, True).start()
            for j, chip in enumerate(_other_chips(x, y)):
                self._copy(outs, scratch, a, 1 + j, me, (*chip, c), True).start()

    def forward(self, outs, scratch):
        x, y, c = _mesh_pos()
        me, sibling = (x, y, c), (x, y, 1 - c)
        for a in range(self.n):
            for j, chip in enumerate(_other_chips(x, y)):
                self._copy(outs, scratch, a, 1 + j, (*chip, c), me).wait_recv()
                self._copy(outs, scratch, a, 4 + j, (*chip, c), sibling).start()

    def finish(self, outs, scratch):
        x, y, c = _mesh_pos()
        me, sibling = (x, y, c), (x, y, 1 - c)
        chips = _other_chips(x, y)
        for a in range(self.n):
            self._copy(outs, scratch, a, 0, sibling, me).wait_recv()
            for j, chip in enumerate(chips):
                self._copy(outs, scratch, a, 4 + j, (*chip, 1 - c), me).wait_recv()
        for a in range(self.n):
            self._copy(outs, scratch, a, 0, me, sibling, True).wait_send()
            for j, chip in enumerate(chips):
                self._copy(outs, scratch, a, 1 + j, me, (*chip, c), True).wait_send()
                self._copy(outs, scratch, a, 4 + j, (*chip, c), sibling).wait_send()
            self._local(outs, scratch, a, me).wait()


def _gather_w_in(w_in_local):
    g = _WeightGather([(D_MODEL, IN_SHARD)], ["block"], [(D_MODEL, IN_SHARD_PAD)])

    def body(w_ref, out_ref, *scratch):
        g.start([w_ref], [out_ref], scratch)
        g.forward([out_ref], scratch)
        g.finish([out_ref], scratch)

    return pl.pallas_call(
        body,
        out_shape=g.out_shapes()[0],
        in_specs=[pl.BlockSpec(memory_space=pltpu.VMEM)],
        out_specs=pl.BlockSpec(memory_space=pl.ANY),
        scratch_shapes=g.scratch_shapes(),
        compiler_params=pltpu.CompilerParams(vmem_limit_bytes=32 * MIB),
        name="gather_w_in",
    )(w_in_local)


class _GradExchange:
    def __init__(self, shapes):
        self.shapes = [tuple(s) for s in shapes]
        self.n = len(self.shapes)

    def land_shapes(self, stage):
        slots, dtype = (4, F32) if stage == 1 else (3, BF16)
        return [jax.ShapeDtypeStruct((slots,) + s, dtype) for s in self.shapes]

    def sem_shapes(self, stage):
        slots = 4 if stage == 1 else 3
        return [pltpu.SemaphoreType.DMA((self.n, slots)), pltpu.SemaphoreType.DMA((self.n, slots))]

    def _copy(self, stage, srcs, lands, sems, a, k):
        x, y, c = _mesh_pos()
        cx, cy = (_other_chips(x, y) + [(x, y)])[k]
        if stage == 1:
            src, to = srcs[a].at[_dev_index(cx, cy, 1 - c)], (x, y, 1 - c)
        else:
            src, to = srcs[a].at[k], (cx, cy, c)
        return pltpu.make_async_remote_copy(
            src_ref=src, dst_ref=lands[a].at[k], send_sem=sems[0].at[a, k], recv_sem=sems[1].at[a, k],
            device_id=to, device_id_type=MESH)

    def start(self, stage, srcs, lands, sems):
        for a in range(self.n):
            for k in range(4 if stage == 1 else 3):
                self._copy(stage, srcs, lands, sems, a, k).start()

    def wait(self, stage, srcs, lands, sems):
        for a in range(self.n):
            for k in range(4 if stage == 1 else 3):
                cp = self._copy(stage, srcs, lands, sems, a, k)
                cp.wait_recv()
                cp.wait_send()


def _owner_indices():
    x, y, c = _mesh_pos()
    return jnp.stack([_dev_index(cx, cy, c) for cx, cy in _other_chips(x, y) + [(x, y)]]).astype(jnp.int32)


def _chip_partials(g, land1, idx, name):
    _, rows, cols = g.shape
    tr = min(rows, 256)

    def body(idx_ref, g_ref, l_ref, o_ref):
        o_ref[...] = (g_ref[...] + l_ref[...]).astype(BF16)

    return pl.pallas_call(
        body,
        grid_spec=pltpu.PrefetchScalarGridSpec(
            num_scalar_prefetch=1, grid=(3, rows // tr),
            in_specs=[pl.BlockSpec((None, tr, cols), lambda k, r, idx: (idx[k], r, 0)),
                      pl.BlockSpec((None, tr, cols), lambda k, r, idx: (k, r, 0))],
            out_specs=pl.BlockSpec((None, tr, cols), lambda k, r, idx: (k, r, 0))),
        out_shape=jax.ShapeDtypeStruct((3, rows, cols), BF16),
        compiler_params=_params(32, 2), name=name,
    )(idx, g, land1)


def _reduced_block(g_ref, l1_ref, a_ref, b_ref, c_ref):
    return ((g_ref[...] + l1_ref[...]) + a_ref[...].astype(F32)) + b_ref[...].astype(F32) + c_ref[...].astype(F32)


def _reduced_specs(tm, cols):
    return [pl.BlockSpec((None, tm, cols), lambda i, idx: (idx[3], i, 0)),
            pl.BlockSpec((None, tm, cols), lambda i, idx: (3, i, 0)),
            pl.BlockSpec((None, tm, cols), lambda i, idx: (0, i, 0)),
            pl.BlockSpec((None, tm, cols), lambda i, idx: (1, i, 0)),
            pl.BlockSpec((None, tm, cols), lambda i, idx: (2, i, 0))]


def _reduced_grad(g, land1, land2, idx, name):
    _, rows, cols = g.shape
    tm = min(rows, 256)

    def body(idx_ref, g_ref, l1_ref, a_ref, b_ref, c_ref, o_ref):
        o_ref[...] = _reduced_block(g_ref, l1_ref, a_ref, b_ref, c_ref)

    return pl.pallas_call(
        body,
        grid_spec=pltpu.PrefetchScalarGridSpec(
            num_scalar_prefetch=1, grid=(rows // tm,), in_specs=_reduced_specs(tm, cols),
            out_specs=pl.BlockSpec((tm, cols), lambda i, idx: (i, 0))),
        out_shape=jax.ShapeDtypeStruct((rows, cols), F32),
        compiler_params=_params(32), name=name,
    )(idx, g, land1, land2, land2, land2)


def _adamw_math(w, g, m, v):
    m = ADAM_B1 * m + (1.0 - ADAM_B1) * g
    v = ADAM_B2 * v + (1.0 - ADAM_B2) * (g * g)
    m_hat = m / (1.0 - ADAM_B1 ** ADAM_STEP)
    v_hat = v / (1.0 - ADAM_B2 ** ADAM_STEP)
    delta = -ADAM_LR * (m_hat / (jnp.sqrt(v_hat) + ADAM_EPS) + ADAM_WD * w)
    return delta, m, v


def _allreduce_small_adamw(g, w, m, v):
    shape = g.shape

    def body(g_ref, w_ref, m_ref, v_ref, gs_ref, d_ref, nm_ref, nv_ref, sib, psum, land, send_sems, recv_sems):
        x, y, c = _mesh_pos()
        sibling = (x, y, 1 - c)
        chips = _other_chips(x, y)
        first = pltpu.make_async_remote_copy(
            src_ref=g_ref, dst_ref=sib, send_sem=send_sems.at[0], recv_sem=recv_sems.at[0],
            device_id=sibling, device_id_type=MESH)
        first.start()
        first.wait_recv()
        psum[...] = g_ref[...] + sib[...]
        second = []
        for k, (cx, cy) in enumerate(chips):
            cp = pltpu.make_async_remote_copy(
                src_ref=psum, dst_ref=land.at[k], send_sem=send_sems.at[1 + k], recv_sem=recv_sems.at[1 + k],
                device_id=(cx, cy, c), device_id_type=MESH)
            cp.start()
            second.append(cp)
        for cp in second:
            cp.wait_recv()
        total = (psum[...] + land[0]) + (land[1] + land[2])
        gs_ref[...] = total
        delta, nm, nv = _adamw_math(w_ref[...], total, m_ref[...], v_ref[...])
        d_ref[...] = delta
        nm_ref[...] = nm
        nv_ref[...] = nv
        first.wait_send()
        for cp in second:
            cp.wait_send()

    sd = jax.ShapeDtypeStruct(shape, F32)
    return pl.pallas_call(
        body,
        out_shape=(sd, sd, sd, sd),
        in_specs=[pl.BlockSpec(memory_space=pltpu.VMEM)] * 4,
        out_specs=[pl.BlockSpec(memory_space=pltpu.VMEM)] * 4,
        scratch_shapes=[pltpu.VMEM(shape, F32), pltpu.VMEM(shape, F32), pltpu.VMEM((3,) + shape, F32),
                        pltpu.SemaphoreType.DMA((4,)), pltpu.SemaphoreType.DMA((4,))],
        compiler_params=pltpu.CompilerParams(vmem_limit_bytes=32 * MIB),
        name="allreduce_small_adamw",
    )(g, w, m, v)


def _adamw_reduced(w, m, v, g, land1, land2, idx, name):
    rows, cols = w.shape
    tm = min(rows, 256)

    def body(idx_ref, w_ref, m_ref, v_ref, g_ref, l1_ref, a_ref, b_ref, c_ref, go_ref, d_ref, nm_ref, nv_ref):
        gsum = _reduced_block(g_ref, l1_ref, a_ref, b_ref, c_ref)
        go_ref[...] = gsum
        delta, nm, nv = _adamw_math(w_ref[...], gsum, m_ref[...], v_ref[...])
        d_ref[...] = delta
        nm_ref[...] = nm
        nv_ref[...] = nv

    sd = jax.ShapeDtypeStruct((rows, cols), F32)
    spec = pl.BlockSpec((tm, cols), lambda i, idx: (i, 0))
    return pl.pallas_call(
        body,
        grid_spec=pltpu.PrefetchScalarGridSpec(
            num_scalar_prefetch=1, grid=(rows // tm,), in_specs=[spec] * 3 + _reduced_specs(tm, cols),
            out_specs=[spec] * 4),
        out_shape=(sd, sd, sd, sd),
        compiler_params=_params(32), name=name,
    )(idx, w, m, v, g, land1, land2, land2, land2)


def _adamw(w, g, m, v, name):
    rows, cols = w.shape
    tm = 256 if rows % 256 == 0 else rows

    def body(w_ref, g_ref, m_ref, v_ref, d_ref, nm_ref, nv_ref):
        delta, nm, nv = _adamw_math(w_ref[...], g_ref[...], m_ref[...], v_ref[...])
        d_ref[...] = delta
        nm_ref[...] = nm
        nv_ref[...] = nv

    sd = jax.ShapeDtypeStruct((rows, cols), F32)
    spec = _row_spec(tm, cols)
    return pl.pallas_call(
        body, grid=(rows // tm,), out_shape=(sd, sd, sd), in_specs=[spec] * 4, out_specs=[spec] * 3,
        compiler_params=_params(32), name=name,
    )(w, g, m, v)


def _virtual_slab(sources, v0, v_end, like):
    lane = _iota(like.shape, 1)
    out = jnp.zeros(like.shape, like.dtype)
    for v_start, v_stop, read in sources:
        a, b = max(v0, v_start), min(v0 + LANES, v_stop, v_end)
        while a < b:
            c = a - v_start
            n = min(b - a, LANES - c % LANES)
            piece = read(c // LANES)
            shift = (a - v0 - c % LANES) % LANES
            if shift:
                piece = pltpu.roll(piece, shift, 1)
            out = jnp.where((lane >= a - v0) & (lane < a - v0 + n), piece, out)
            a += n
    return out


def _assemble_w_in(wg_in):
    tm = TOKEN_TILE

    def body(src_ref, wz_ref, wf_ref, wg_ref):
        like = src_ref[0, :, 0:LANES]
        sources = [(IN_SHARD * j, IN_SHARD * (j + 1),
                    (lambda k, j=j: src_ref[j, :, LANES * k:LANES * (k + 1)])) for j in range(N_DEV)]
        for k in range(ZQKV_WIDTH // LANES):
            wz_ref[:, LANES * k:LANES * (k + 1)] = _virtual_slab(sources, LANES * k, ZQKV_WIDTH, like)
        wf_ref[...] = _virtual_slab(sources, ZQKV_WIDTH, GATE_OFFSET, like)
        for k in range(2 * D_MODEL // LANES):
            wg_ref[:, LANES * k:LANES * (k + 1)] = _virtual_slab(sources, GATE_OFFSET + LANES * k, IN_WIDTH, like)

    return pl.pallas_call(
        body, grid=(D_MODEL // tm,),
        out_shape=(jax.ShapeDtypeStruct((D_MODEL, ZQKV_WIDTH), BF16), jax.ShapeDtypeStruct((D_MODEL, LANES), BF16),
                   jax.ShapeDtypeStruct((D_MODEL, 2 * D_MODEL), BF16)),
        in_specs=[pl.BlockSpec((N_DEV, tm, IN_SHARD_PAD), lambda i: (0, i, 0))],
        out_specs=[_row_spec(tm, ZQKV_WIDTH), _row_spec(tm, LANES), _row_spec(tm, 2 * D_MODEL)],
        compiler_params=_params(32), name="assemble_w_in",
    )(wg_in)


def _block_dw_in(dw_z, dw_qkvf, dw_g):
    tm = TOKEN_TILE
    n_qkvf = 3 * ATTN_WIDTH + N_HEADS

    def body(z_ref, q_ref, g_ref, out_ref):
        like = z_ref[:, 0:LANES]
        slab = lambda ref: (lambda k: ref[:, LANES * k:LANES * (k + 1)])
        sources = [(0, 2 * SGU_WIDTH, slab(z_ref)), (2 * SGU_WIDTH, 2 * SGU_WIDTH + n_qkvf, slab(q_ref)),
                   (GATE_OFFSET, IN_WIDTH, slab(g_ref))]
        for j in range(N_DEV):
            for k in range(IN_SHARD_PAD // LANES):
                out_ref[j, :, LANES * k:LANES * (k + 1)] = _virtual_slab(
                    sources, IN_SHARD * j + LANES * k, IN_SHARD * (j + 1), like)

    return pl.pallas_call(
        body, grid=(D_MODEL // tm,),
        out_shape=jax.ShapeDtypeStruct((N_DEV, D_MODEL, IN_SHARD_PAD), F32),
        in_specs=[_row_spec(tm, 2 * SGU_WIDTH), _row_spec(tm, dw_qkvf.shape[1]), _row_spec(tm, 2 * D_MODEL)],
        out_specs=pl.BlockSpec((N_DEV, tm, IN_SHARD_PAD), lambda i: (0, i, 0)),
        compiler_params=_params(32), name="block_dw_in",
    )(dw_z, dw_qkvf, dw_g)


def _fwd_in(x2, g1, wz, wf, wg):
    T = x2.shape[0]
    tm = TOKEN_TILE

    def body(x_ref, g_ref, wz_ref, wf_ref, wg_ref, xn_ref, zuv_ref, qkv_ref, fl_ref, gt_ref):
        x = x_ref[...]
        r, xh = _rms_stats(x)
        xn = (xh * g_ref[...]).astype(BF16)
        xn_ref[...] = xn
        zuv_ref[...] = _dot(xn, wz_ref[:, 0:1024]).astype(BF16)
        qkv_ref[:, 0:512] = (_dot(xn, wz_ref[:, 1024:1536]) * 0.125).astype(BF16)
        qkv_ref[:, 512:1536] = _dot(xn, wz_ref[:, 1536:2560]).astype(BF16)
        fl_ref[...] = _dot(xn, wf_ref[...])
        gt_ref[...] = jax.nn.sigmoid(_dot(xn, wg_ref[...])).astype(BF16)

    return pl.pallas_call(
        body, grid=(T // tm,),
        out_shape=(jax.ShapeDtypeStruct((T, D_MODEL), BF16), jax.ShapeDtypeStruct((T, 1024), BF16),
                   jax.ShapeDtypeStruct((T, 1536), BF16), jax.ShapeDtypeStruct((T, LANES), F32),
                   jax.ShapeDtypeStruct((T, 2048), BF16)),
        in_specs=[_row_spec(tm, D_MODEL), _const_spec((1, D_MODEL)), _const_spec((D_MODEL, ZQKV_WIDTH)),
                  _const_spec((D_MODEL, LANES)), _const_spec((D_MODEL, 2048))],
        out_specs=[_row_spec(tm, D_MODEL), _row_spec(tm, 1024), _row_spec(tm, 1536), _row_spec(tm, LANES),
                   _row_spec(tm, 2048)],
        compiler_params=_params(48), name="fwd_in",
    )(x2, g1, wz, wf, wg)


def _log_sigmoid(f):
    return jnp.minimum(f, 0.0) - jnp.log1p(jnp.exp(-jnp.abs(f)))


AUG_LANES = 6


def _split3(v):
    hi = v.astype(BF16)
    r1 = v - hi.astype(F32)
    mid = r1.astype(BF16)
    lo = (r1 - mid.astype(F32)).astype(BF16)
    return hi, mid, lo


def _spread(parts, k0):
    r, c = _iota((LANES, LANES), 0), _iota((LANES, LANES), 1)
    out = None
    for i, part in enumerate(parts):
        e = ((c == AUG_LANES * r + (k0 + i)) & (r < N_HEADS)).astype(BF16)
        term = _dot(part, e)
        out = term if out is None else out + term
    return out


def _aug_query(v):
    ones = (_iota(v.shape, 1) < N_HEADS).astype(BF16)
    return (_spread(_split3(v), 0) + _spread((ones, ones, ones), 3)).astype(BF16)


def _aug_key(v):
    ones = (_iota(v.shape, 1) < N_HEADS).astype(BF16)
    return (_spread((ones, ones, ones), 0) - _spread(_split3(v), 3)).astype(BF16)


def _aug_stack(t2, aug, p):
    lane = _iota(t2.shape, 1)
    low = lane < 64
    zero = jnp.zeros_like(t2)
    first = 2 * AUG_LANES * p
    a_e = jnp.where((lane >= first) & (lane < first + AUG_LANES), aug, zero)
    a_o = jnp.where((lane >= first + AUG_LANES) & (lane < first + 2 * AUG_LANES), aug, zero)
    top = jnp.concatenate([jnp.where(low, t2, zero), a_e], axis=1)
    bot = jnp.concatenate([jnp.where(low, zero, t2), a_o], axis=1)
    return jnp.concatenate([top, bot], axis=0)


def _fwd_cum(fl, bfp):
    T = fl.shape[0]
    tb = CUM_TILE

    def body(fl_ref, b_ref, cc_ref, qa_ref, ka_ref):
        tri = (_iota((tb, tb), 0) >= _iota((tb, tb), 1)).astype(F32)
        carry = jnp.zeros((1, LANES), F32)
        for i in range(T // tb):
            rows = slice(i * tb, (i + 1) * tb)
            lf = _log_sigmoid(fl_ref[rows, :] + b_ref[...])
            cs = jnp.dot(tri, lf, precision=HIGHEST, preferred_element_type=F32) + carry
            cc_ref[rows, :] = cs
            carry = cs[tb - 1:tb, :]
            qa_ref[rows, :] = _aug_query(cs)
            ka_ref[rows, :] = _aug_key(cs)

    return pl.pallas_call(
        body,
        out_shape=(jax.ShapeDtypeStruct((T, LANES), F32), jax.ShapeDtypeStruct((T, LANES), BF16),
                   jax.ShapeDtypeStruct((T, LANES), BF16)),
        compiler_params=pltpu.CompilerParams(vmem_limit_bytes=32 * MIB), name="fwd_cum",
    )(fl, bfp)


def _sgu_forward_parts(z, gs, bs):
    u = _gelu(z[:, :SGU_WIDTH])
    vv = _gelu(z[:, SGU_WIDTH:])
    vc = vv - jnp.mean(vv, axis=1, keepdims=True)
    rs = lax.rsqrt(jnp.mean(vc * vc, axis=1, keepdims=True) + EPS)
    vhat = vc * rs
    return u, vhat, rs, vhat * gs + bs


def _sgu_pair_weights(w_ref, bT, p):
    tril = _iota((CHUNK, CHUNK), 0) >= _iota((CHUNK, CHUNK), 1)
    we = jnp.where(tril, w_ref[2 * p], 0.0).astype(BF16)
    wo = jnp.where(tril, w_ref[2 * p + 1], 0.0).astype(BF16)
    lane8 = _iota(bT.shape, 1)
    low = _iota((CHUNK, LANES), 1) < 64
    b2 = jnp.where(low, _pick(bT, lane8, 2 * p), _pick(bT, lane8, 2 * p + 1))
    return we, wo, b2


def _chunks_on_lanes(v, p, nc):
    return jnp.concatenate([v[c * CHUNK:(c + 1) * CHUNK, LANES * p:LANES * (p + 1)] for c in range(nc)], axis=1)


def _sgu_mix(we, wo, b2, vcat, nc):
    low = (_iota((CHUNK, nc * LANES), 1) % LANES) < 64
    return jnp.where(low, _dot(we, vcat), _dot(wo, vcat)) + jnp.concatenate([b2] * nc, axis=1)


def _fwd_sgu(zuv, gs, bs, wsp, bT):
    T = zuv.shape[0]
    tc = SGU_TILE
    nc = tc // CHUNK

    def body(z_ref, gs_ref, bs_ref, w_ref, bT_ref, y_ref):
        u, _, _, vln = _sgu_forward_parts(z_ref[...].astype(F32), gs_ref[...], bs_ref[...])
        vb = vln.astype(BF16)
        for p in range(4):
            we, wo, b2 = _sgu_pair_weights(w_ref, bT_ref[...], p)
            s = _sgu_mix(we, wo, b2, _chunks_on_lanes(vb, p, nc), nc)
            for c in range(nc):
                rows, cols = slice(c * CHUNK, (c + 1) * CHUNK), slice(LANES * p, LANES * (p + 1))
                y_ref[rows, cols] = (u[rows, cols] * s[:, c * LANES:(c + 1) * LANES]).astype(BF16)

    return pl.pallas_call(
        body, grid=(T // tc,), out_shape=jax.ShapeDtypeStruct((T, SGU_WIDTH), BF16),
        in_specs=[_row_spec(tc, 1024), _const_spec((1, SGU_WIDTH)), _const_spec((1, SGU_WIDTH)),
                  _const_spec((8, CHUNK, CHUNK)), _const_spec((CHUNK, 8))],
        out_specs=_row_spec(tc, SGU_WIDTH),
        compiler_params=_params(40), name="fwd_sgu",
    )(zuv, gs, bs, wsp, bT)


def _fwd_attn(qkv, qaug, kaug, w_shards):
    T = qkv.shape[0]
    tq = tk = ATTN_TILE
    nq = T // tq
    gather = _WeightGather([w.shape for w in w_shards], ["cols", "cols", "rows", "cols", "rows"])
    nw = gather.n

    def body(q_ref, qa_ref, k_ref, v_ref, ka_ref, *rest):
        w_refs, (o_ref, lse_ref), wg_refs, scratch = rest[:nw], rest[nw:nw + 2], rest[nw + 2:2 * nw + 2], rest[2 * nw + 2:]
        i = pl.program_id(0)

        @pl.when(i == 0)
        def _():
            gather.start(w_refs, wg_refs, scratch)

        @pl.when(i == nq // 2)
        def _():
            gather.forward(wg_refs, scratch)

        lane = _iota((tq, LANES), 1)
        low = lane < 64
        lowk = _iota((tk, LANES), 1) < 64
        one = jnp.ones((tk, LANES), BF16)
        row = _iota((2 * tq, tk), 0) % tq
        col = _iota((2 * tq, tk), 1)
        cols = [slice(LANES * p, LANES * (p + 1)) for p in range(4)]
        qa = qa_ref[...]
        qs = [_aug_stack(q_ref[:, cols[p]], qa, p) for p in range(4)]

        def step(j, carry, masked):
            ks = pl.ds(pl.multiple_of(j * tk, tk), tk)
            ka = ka_ref[ks, :]
            new = []
            for p in range(4):
                m, acc_e, acc_o = carry[p]
                v2 = v_ref[ks, cols[p]]
                s = _dot_nt(qs[p], jnp.concatenate([k_ref[ks, cols[p]], ka], axis=1))
                if masked:
                    s = jnp.where(col <= row, s, NEG)
                mn = jnp.maximum(m, jnp.max(s, axis=1, keepdims=True))
                al = jnp.exp(m - mn)
                pm = jnp.exp(s - mn).astype(BF16)
                acc_e = al[:tq] * acc_e + _dot(pm[:tq], jnp.where(lowk, v2, one))
                acc_o = al[tq:] * acc_o + _dot(pm[tq:], jnp.where(lowk, one, v2))
                new.append((mn, acc_e, acc_o))
            return tuple(new)

        init = tuple((jnp.full((2 * tq, 1), NEG, F32), jnp.zeros((tq, LANES), F32), jnp.zeros((tq, LANES), F32))
                     for _ in range(4))
        carry = lax.fori_loop(0, i, lambda j, c: step(j, c, False), init)
        carry = step(i, carry, True)
        lse_blk = jnp.zeros((tq, LANES), F32)
        for p in range(4):
            m, acc_e, acc_o = carry[p]
            l_e = pltpu.roll(acc_e, 64, 1)
            l_o = pltpu.roll(acc_o, 64, 1)
            o_ref[:, cols[p]] = jnp.where(low, acc_e / l_e, acc_o / l_o).astype(BF16)
            lse_blk = jnp.where(lane == 2 * p, m[:tq] + jnp.log(l_e), lse_blk)
            lse_blk = jnp.where(lane == 2 * p + 1, m[tq:] + jnp.log(acc_o), lse_blk)
        lse_ref[...] = lse_blk

        @pl.when(i == nq - 1)
        def _():
            gather.finish(wg_refs, scratch)

    return pl.pallas_call(
        body, grid=(nq,),
        out_shape=[jax.ShapeDtypeStruct((T, ATTN_WIDTH), BF16), jax.ShapeDtypeStruct((T, LANES), F32)]
        + gather.out_shapes(),
        in_specs=[_row_spec(tq, 512), _row_spec(tq, LANES),
                  pl.BlockSpec((T, 512), lambda i: (0, 1), pipeline_mode=pl.Buffered(1)),
                  pl.BlockSpec((T, 512), lambda i: (0, 2), pipeline_mode=pl.Buffered(1)),
                  _const_spec((T, LANES))] + [_const_spec(w.shape) for w in w_shards],
        out_specs=[_row_spec(tq, ATTN_WIDTH), _row_spec(tq, LANES)] + [pl.BlockSpec(memory_space=pl.ANY)] * nw,
        scratch_shapes=gather.scratch_shapes(),
        compiler_params=_params(58), name="fwd_attn",
    )(qkv, qaug, qkv, qkv, kaug, *w_shards)


def _fwd_merge(ys, ya, gt, x2, wbs, wba, wo, g2):
    T = x2.shape[0]
    tm = TOKEN_TILE

    def body(ys_ref, ya_ref, gt_ref, x_ref, wbs_ref, wba_ref, wo_ref, g2_ref, a_ref, b_ref, mg_ref, o_ref, h1_ref):
        A = _dot(ys_ref[...], wbs_ref[...])
        B = _dot(ya_ref[...], wba_ref[...])
        mg = (gt_ref[:, :D_MODEL].astype(F32) * A + gt_ref[:, D_MODEL:].astype(F32) * B).astype(BF16)
        o = _dot(mg, wo_ref[...])
        r2, oh = _rms_stats(o)
        a_ref[...] = A.astype(BF16)
        b_ref[...] = B.astype(BF16)
        mg_ref[...] = mg
        o_ref[...] = o.astype(BF16)
        h1_ref[...] = x_ref[...] + oh * g2_ref[...]

    sd = jax.ShapeDtypeStruct((T, D_MODEL), BF16)
    return pl.pallas_call(
        body, grid=(T // tm,),
        out_shape=(sd, sd, sd, sd, jax.ShapeDtypeStruct((T, D_MODEL), F32)),
        in_specs=[_row_spec(tm, 512), _row_spec(tm, 512), _row_spec(tm, 2048), _row_spec(tm, D_MODEL),
                  _const_spec((512, D_MODEL)), _const_spec((512, D_MODEL)), _const_spec((D_MODEL, D_MODEL)),
                  _const_spec((1, D_MODEL))],
        out_specs=[_row_spec(tm, D_MODEL)] * 5,
        compiler_params=_params(40), name="fwd_merge",
    )(ys, ya, gt, x2, wbs, wba, wo, g2)


def _fwd_ffn_loss(h1, tgt, wup, wdn, g3, g4):
    T = h1.shape[0]
    tm = TOKEN_TILE
    nsteps = T // tm

    def body(h1_ref, tg_ref, wup_ref, wdn_ref, g3_ref, g4_ref, xn2_ref, a_ref, ddn_ref, dy_ref, loss_ref,
             dg4_ref, acc_l, acc_g):
        i = pl.program_id(0)

        @pl.when(i == 0)
        def _():
            acc_l[...] = jnp.zeros_like(acc_l)
            acc_g[...] = jnp.zeros_like(acc_g)

        h1v = h1_ref[...]
        r3, h1h = _rms_stats(h1v)
        xn2 = (h1h * g3_ref[...]).astype(BF16)
        xn2_ref[...] = xn2
        dn = jnp.zeros((tm, D_MODEL), F32)
        for j in range(D_FF // 1024):
            cols = slice(1024 * j, 1024 * (j + 1))
            a = _dot(xn2, wup_ref[:, cols])
            a_ref[:, cols] = a.astype(BF16)
            hid = jnp.square(jnp.maximum(a, 0.0)).astype(BF16)
            dn = dn + _dot(hid, wdn_ref[cols, :])
        r4, dnh = _rms_stats(dn)
        g4v = g4_ref[...]
        e = (h1v + dnh * g4v) - tg_ref[...]
        sq = e * e
        s1 = sq[:, 0:LANES]
        for j in range(1, D_MODEL // LANES):
            s1 = s1 + sq[:, LANES * j:LANES * (j + 1)]
        acc_l[...] += _fold8(s1)
        dy = e * (1.0 / D_MODEL)
        dy_ref[...] = dy
        acc_g[...] += _fold8(dy * dnh)
        ddn_ref[...] = _rms_bwd(dy, dnh, r4, g4v).astype(BF16)

        @pl.when(i == nsteps - 1)
        def _():
            loss_ref[...] = acc_l[...] * (0.5 / D_MODEL)
            dg4_ref[...] = jnp.sum(acc_g[...], axis=0, keepdims=True)

    return pl.pallas_call(
        body, grid=(nsteps,),
        out_shape=(jax.ShapeDtypeStruct((T, D_MODEL), BF16), jax.ShapeDtypeStruct((T, D_FF), BF16),
                   jax.ShapeDtypeStruct((T, D_MODEL), BF16), jax.ShapeDtypeStruct((T, D_MODEL), F32),
                   jax.ShapeDtypeStruct((SUBLANES, LANES), F32), jax.ShapeDtypeStruct((1, D_MODEL), F32)),
        in_specs=[_row_spec(tm, D_MODEL), _row_spec(tm, D_MODEL), _const_spec((D_MODEL, D_FF)),
                  _const_spec((D_FF, D_MODEL)), _const_spec((1, D_MODEL)), _const_spec((1, D_MODEL))],
        out_specs=[_row_spec(tm, D_MODEL), _row_spec(tm, D_FF), _row_spec(tm, D_MODEL), _row_spec(tm, D_MODEL),
                   pl.BlockSpec((SUBLANES, LANES), lambda i: (0, 0)), pl.BlockSpec((1, D_MODEL), lambda i: (0, 0))],
        scratch_shapes=[pltpu.VMEM((SUBLANES, LANES), F32), pltpu.VMEM((SUBLANES, D_MODEL), F32)],
        compiler_params=_params(52), name="fwd_ffn_loss",
    )(h1, tgt, wup, wdn, g3, g4)


def _bwd_ffn(ddn, a, dy, h1, wup, wdn, g3):
    T = h1.shape[0]
    tm = TOKEN_TILE
    nsteps = T // tm

    def body(ddn_ref, a_ref, dy_ref, h1_ref, wup_ref, wdn_ref, g3_ref, da_ref, dh1_ref, dg3_ref, acc_g):
        i = pl.program_id(0)

        @pl.when(i == 0)
        def _():
            acc_g[...] = jnp.zeros_like(acc_g)

        ddnv = ddn_ref[...]
        dxn2 = jnp.zeros((tm, D_MODEL), F32)
        for j in range(D_FF // 1024):
            cols = slice(1024 * j, 1024 * (j + 1))
            dhid = _dot_nt(ddnv, wdn_ref[cols, :])
            da = (dhid * (2.0 * jnp.maximum(a_ref[:, cols].astype(F32), 0.0))).astype(BF16)
            da_ref[:, cols] = da
            dxn2 = dxn2 + _dot_nt(da, wup_ref[:, cols])
        r3, h1h = _rms_stats(h1_ref[...])
        acc_g[...] += _fold8(dxn2 * h1h)
        dh1_ref[...] = dy_ref[...] + _rms_bwd(dxn2, h1h, r3, g3_ref[...])

        @pl.when(i == nsteps - 1)
        def _():
            dg3_ref[...] = jnp.sum(acc_g[...], axis=0, keepdims=True)

    return pl.pallas_call(
        body, grid=(nsteps,),
        out_shape=(jax.ShapeDtypeStruct((T, D_FF), BF16), jax.ShapeDtypeStruct((T, D_MODEL), F32),
                   jax.ShapeDtypeStruct((1, D_MODEL), F32)),
        in_specs=[_row_spec(tm, D_MODEL), _row_spec(tm, D_FF), _row_spec(tm, D_MODEL), _row_spec(tm, D_MODEL),
                  _const_spec((D_MODEL, D_FF)), _const_spec((D_FF, D_MODEL)), _const_spec((1, D_MODEL))],
        out_specs=[_row_spec(tm, D_FF), _row_spec(tm, D_MODEL), pl.BlockSpec((1, D_MODEL), lambda i: (0, 0))],
        scratch_shapes=[pltpu.VMEM((SUBLANES, D_MODEL), F32)],
        compiler_params=_params(52), name="bwd_ffn",
    )(ddn, a, dy, h1, wup, wdn, g3)


def _wgrad(xa, dy, name, relu2=False, tn=None, block_cols=None):
    T, K = xa.shape
    N = dy.shape[1]
    tn = N if tn is None else tn
    tt = min(T, WGRAD_TILE if K <= D_MODEL else WGRAD_TILE // 2)
    if block_cols:
        nb = tn // block_cols
        out_shape = jax.ShapeDtypeStruct((N // block_cols, K, block_cols), F32)
        out_spec = pl.BlockSpec((nb, K, block_cols), lambda n, t: (n, 0, 0))
    else:
        out_shape = jax.ShapeDtypeStruct((K, N), F32)
        out_spec = pl.BlockSpec((K, tn), lambda n, t: (0, n))

    def body(x_ref, dy_ref, o_ref):
        @pl.when(pl.program_id(1) == 0)
        def _():
            o_ref[...] = jnp.zeros_like(o_ref)

        xv = x_ref[...]
        if relu2:
            xv = jnp.square(jnp.maximum(xv.astype(F32), 0.0)).astype(BF16)
        if block_cols:
            for b in range(nb):
                o_ref[b] += _dot_tn(xv, dy_ref[:, block_cols * b:block_cols * (b + 1)])
        else:
            o_ref[...] += _dot_tn(xv, dy_ref[...])

    return pl.pallas_call(
        body, grid=(N // tn, T // tt), out_shape=out_shape,
        in_specs=[pl.BlockSpec((tt, K), lambda n, t: (t, 0)), pl.BlockSpec((tt, tn), lambda n, t: (t, n))],
        out_specs=out_spec,
        compiler_params=_params(52, 2), name=name,
    )(xa, dy)


def _bwd_merge(dh1, o, A, B, gt, ys, ya, mg, lse, cc, wbs, wba, wo, g2):
    T = dh1.shape[0]
    tm = MERGE_BWD_TILE
    nsteps = T // tm

    def body(dh1_ref, o_ref, a_ref, b_ref, gt_ref, ys_ref, ya_ref, mg_ref, lse_ref, cc_ref, wbs_ref, wba_ref,
             wo_ref, g2_ref, dgl_ref, dys_ref, dya_ref, qab_ref, dab_ref, dg2_ref, dwbs_ref, dwba_ref, dwo_ref,
             acc_g):
        i = pl.program_id(0)

        @pl.when(i == 0)
        def _():
            acc_g[...] = jnp.zeros_like(acc_g)
            dwbs_ref[...] = jnp.zeros_like(dwbs_ref)
            dwba_ref[...] = jnp.zeros_like(dwba_ref)
            dwo_ref[...] = jnp.zeros_like(dwo_ref)

        dh1v = dh1_ref[...]
        r2, oh = _rms_stats(o_ref[...].astype(F32))
        acc_g[...] += _fold8(dh1v * oh)
        do = _rms_bwd(dh1v, oh, r2, g2_ref[...]).astype(BF16)
        dwo_ref[...] += _dot_tn(mg_ref[...], do)
        dmg = _dot_nt(do, wo_ref[...])
        ga = gt_ref[:, :D_MODEL].astype(F32)
        gb = gt_ref[:, D_MODEL:].astype(F32)
        dgl_ref[:, :D_MODEL] = (dmg * a_ref[...].astype(F32) * ga * (1.0 - ga)).astype(BF16)
        dgl_ref[:, D_MODEL:] = (dmg * b_ref[...].astype(F32) * gb * (1.0 - gb)).astype(BF16)
        dA = (dmg * ga).astype(BF16)
        dB = (dmg * gb).astype(BF16)
        dwbs_ref[...] += _dot_tn(ys_ref[...], dA)
        dwba_ref[...] += _dot_tn(ya_ref[...], dB)
        dys_ref[...] = _dot_nt(dA, wbs_ref[...]).astype(BF16)
        dya = _dot_nt(dB, wba_ref[...]).astype(BF16)
        dya_ref[...] = dya
        prod = dya.astype(F32) * ya_ref[...].astype(F32)
        lane = _iota((tm, LANES), 1)
        low = lane < 64
        blk = jnp.zeros((tm, LANES), F32)
        for p in range(4):
            pp = prod[:, LANES * p:LANES * (p + 1)]
            blk = jnp.where(lane == 2 * p, jnp.sum(jnp.where(low, pp, 0.0), axis=1, keepdims=True), blk)
            blk = jnp.where(lane == 2 * p + 1, jnp.sum(jnp.where(low, 0.0, pp), axis=1, keepdims=True), blk)
        qab_ref[...] = _aug_query(cc_ref[...] - lse_ref[...])
        dab_ref[...] = _spread(_split3(-blk), 0).astype(BF16)

        @pl.when(i == nsteps - 1)
        def _():
            dg2_ref[...] = jnp.sum(acc_g[...], axis=0, keepdims=True)

    sh = jax.ShapeDtypeStruct((T, 512), BF16)
    sa = jax.ShapeDtypeStruct((T, LANES), BF16)
    sw = jax.ShapeDtypeStruct((512, D_MODEL), F32)
    whole = lambda shape: pl.BlockSpec(shape, lambda i: (0, 0))
    return pl.pallas_call(
        body, grid=(nsteps,),
        out_shape=(jax.ShapeDtypeStruct((T, 2048), BF16), sh, sh, sa, sa, jax.ShapeDtypeStruct((1, D_MODEL), F32),
                   sw, sw, jax.ShapeDtypeStruct((D_MODEL, D_MODEL), F32)),
        in_specs=[_row_spec(tm, D_MODEL)] * 4 + [_row_spec(tm, 2048), _row_spec(tm, 512), _row_spec(tm, 512),
                  _row_spec(tm, D_MODEL), _row_spec(tm, LANES), _row_spec(tm, LANES),
                  _const_spec((512, D_MODEL)), _const_spec((512, D_MODEL)),
                  _const_spec((D_MODEL, D_MODEL)), _const_spec((1, D_MODEL))],
        out_specs=[_row_spec(tm, 2048), _row_spec(tm, 512), _row_spec(tm, 512), _row_spec(tm, LANES),
                   _row_spec(tm, LANES), whole((1, D_MODEL)), whole((512, D_MODEL)), whole((512, D_MODEL)),
                   whole((D_MODEL, D_MODEL))],
        scratch_shapes=[pltpu.VMEM((SUBLANES, D_MODEL), F32)],
        compiler_params=_params(56), name="bwd_merge",
    )(dh1, o, A, B, gt, ys, ya, mg, lse, cc, wbs, wba, wo, g2)


def _bwd_sgu(zuv, dys, gs, bs, wsp, bT, grads):
    T = zuv.shape[0]
    tc = SGU_TILE
    nc = tc // CHUNK
    nsteps = T // tc
    ex = _GradExchange([tuple(g.shape[1:]) for g in grads])
    ng = ex.n

    def body(z_ref, dy_ref, gs_ref, bs_ref, w_ref, bT_ref, *rest):
        g_refs, (dz_ref, dw_ref, dbT_ref, dgs_ref, dbs_ref) = rest[:ng], rest[ng:ng + 5]
        land1 = rest[ng + 5:2 * ng + 5]
        acc_w, acc_b, acc_gs, acc_bs, dvln_s = rest[2 * ng + 5:2 * ng + 10]
        ex_sems = rest[2 * ng + 10:]
        i = pl.program_id(0)

        @pl.when(i == 0)
        def _():
            ex.start(1, g_refs, land1, ex_sems)
            acc_w[...] = jnp.zeros_like(acc_w)
            acc_b[...] = jnp.zeros_like(acc_b)
            acc_gs[...] = jnp.zeros_like(acc_gs)
            acc_bs[...] = jnp.zeros_like(acc_bs)

        z = z_ref[...].astype(F32)
        gsv = gs_ref[...]
        u, vhat, rs, vln = _sgu_forward_parts(z, gsv, bs_ref[...])
        vb = vln.astype(BF16)
        dy = dy_ref[...].astype(F32)
        low_w = (_iota((CHUNK, nc * LANES), 1) % LANES) < 64
        for p in range(4):
            we, wo, b2 = _sgu_pair_weights(w_ref, bT_ref[...], p)
            vcat = _chunks_on_lanes(vb, p, nc)
            s = _sgu_mix(we, wo, b2, vcat, nc)
            dyc = _chunks_on_lanes(dy, p, nc)
            ds = dyc * _chunks_on_lanes(u, p, nc)
            dsb = ds.astype(BF16)
            zero = jnp.zeros_like(dsb)
            dse = jnp.where(low_w, dsb, zero)
            dso = jnp.where(low_w, zero, dsb)
            acc_w[2 * p] += _dot_nt(dse, vcat)
            acc_w[2 * p + 1] += _dot_nt(dso, vcat)
            acc_b[p] += ds
            dvl = jnp.where(low_w, _dot_tn(we, dsb), _dot_tn(wo, dsb))
            for c in range(nc):
                rows, cols = slice(c * CHUNK, (c + 1) * CHUNK), slice(LANES * p, LANES * (p + 1))
                dvln_s[rows, cols] = dvl[:, c * LANES:(c + 1) * LANES]
                du = dy[rows, cols] * s[:, c * LANES:(c + 1) * LANES]
                dz_ref[rows, cols] = (du * _gelu_grad(z[rows, cols])).astype(BF16)
        dvln = dvln_s[...]
        acc_gs[...] += _fold8(dvln * vhat)
        acc_bs[...] += _fold8(dvln)
        al = dvln * gsv
        dvv = rs * (al - jnp.mean(al, axis=1, keepdims=True) - vhat * jnp.mean(al * vhat, axis=1, keepdims=True))
        dz_ref[:, SGU_WIDTH:] = (dvv * _gelu_grad(z[:, SGU_WIDTH:])).astype(BF16)

        @pl.when(i == nsteps - 1)
        def _():
            tril = _iota((CHUNK, CHUNK), 0) >= _iota((CHUNK, CHUNK), 1)
            lane = _iota((CHUNK, LANES), 1)
            low = lane < 64
            blk = jnp.zeros((CHUNK, LANES), F32)
            for g in range(8):
                dw_ref[g] = jnp.where(tril, acc_w[g], 0.0)
            for p in range(4):
                t = acc_b[p]
                tot = t[:, 0:LANES]
                for c in range(1, nc):
                    tot = tot + t[:, c * LANES:(c + 1) * LANES]
                blk = jnp.where(lane == 2 * p, jnp.sum(jnp.where(low, tot, 0.0), axis=1, keepdims=True), blk)
                blk = jnp.where(lane == 2 * p + 1, jnp.sum(jnp.where(low, 0.0, tot), axis=1, keepdims=True), blk)
            dbT_ref[...] = blk
            dgs_ref[...] = jnp.sum(acc_gs[...], axis=0, keepdims=True)
            dbs_ref[...] = jnp.sum(acc_bs[...], axis=0, keepdims=True)
            ex.wait(1, g_refs, land1, ex_sems)

    whole = lambda shape: pl.BlockSpec(shape, lambda i: (0,) * len(shape))
    hbm_spec = pl.BlockSpec(memory_space=pl.ANY)
    return pl.pallas_call(
        body, grid=(nsteps,),
        out_shape=[jax.ShapeDtypeStruct((T, 1024), BF16), jax.ShapeDtypeStruct((8, CHUNK, CHUNK), F32),
                   jax.ShapeDtypeStruct((CHUNK, LANES), F32), jax.ShapeDtypeStruct((1, SGU_WIDTH), F32),
                   jax.ShapeDtypeStruct((1, SGU_WIDTH), F32)] + ex.land_shapes(1),
        in_specs=[_row_spec(tc, 1024), _row_spec(tc, SGU_WIDTH), _const_spec((1, SGU_WIDTH)),
                  _const_spec((1, SGU_WIDTH)), _const_spec((8, CHUNK, CHUNK)), _const_spec((CHUNK, 8))]
        + [hbm_spec] * ng,
        out_specs=[_row_spec(tc, 1024), whole((8, CHUNK, CHUNK)), whole((CHUNK, LANES)),
                   whole((1, SGU_WIDTH)), whole((1, SGU_WIDTH))] + [hbm_spec] * ng,
        scratch_shapes=[pltpu.VMEM((8, CHUNK, CHUNK), F32), pltpu.VMEM((4, CHUNK, nc * LANES), F32),
                        pltpu.VMEM((SUBLANES, SGU_WIDTH), F32), pltpu.VMEM((SUBLANES, SGU_WIDTH), F32),
                        pltpu.VMEM((tc, SGU_WIDTH), F32)] + ex.sem_shapes(1),
        compiler_params=_params(48), name="bwd_sgu",
    )(zuv, dys, gs, bs, wsp, bT, *grads)


def _bwd_attn(qkv, dya, qab, dab, kaug, parts):
    T = qkv.shape[0]
    tq = tk = ATTN_TILE
    nq = T // tq
    nk = T // tk
    ex = _GradExchange([tuple(g.shape[1:]) for g in parts])
    nr = ex.n

    def body(q_ref, do_ref, qa_ref, da_ref, k_ref, v_ref, ka_ref, *rest):
        part_refs, (dq_ref, dk_ref, dv_ref, dcx_ref) = rest[:nr], rest[nr:nr + 4]
        land2, dq_acc, ex_sems = rest[nr + 4:2 * nr + 4], rest[2 * nr + 4], rest[2 * nr + 5:]
        p = pl.program_id(0)
        j = pl.program_id(1)

        @pl.when((p == 0) & (j == 0))
        def _():
            ex.start(2, part_refs, land2, ex_sems)

        lane = _iota((tq, LANES), 1)
        low = lane < 64
        row = _iota((2 * tq, tk), 0) % tq
        col = _iota((2 * tq, tk), 1)
        first = 2 * AUG_LANES * p

        @pl.when(j == 0)
        def _():
            dq_acc[...] = jnp.zeros_like(dq_acc)

        @pl.when((j == 0) & (p == 0))
        def _():
            dcx_ref[...] = jnp.zeros_like(dcx_ref)

        ka = ka_ref[...]
        kk = jnp.concatenate([k_ref[...], ka], axis=1)
        vv = jnp.concatenate([v_ref[...], ka], axis=1)

        def q_block(i, carry, masked):
            dk_a, dv_a = carry
            qsl = pl.ds(pl.multiple_of(i * tq, tq), tq)
            qs = _aug_stack(q_ref[qsl, :], qa_ref[qsl, :], p)
            dos = _aug_stack(do_ref[qsl, :], da_ref[qsl, :], p)
            s = _dot_nt(qs, kk)
            if masked:
                s = jnp.where(col <= row, s, NEG)
            pm = jnp.exp(s)
            ds = pm * _dot_nt(dos, vv)
            dsb = ds.astype(BF16)
            dv_a = dv_a + _dot_tn(pm.astype(BF16), dos[:, :LANES])
            dk_a = dk_a + _dot_tn(dsb, qs)
            dqx = _dot(dsb, kk)
            dq_acc[qsl, :] += jnp.where(low, dqx[:tq, :LANES], dqx[tq:, :LANES])
            dcx_ref[qsl, :] += (jnp.where(lane == first, dqx[:tq, LANES:], 0.0)
                                + jnp.where(lane == first + AUG_LANES, dqx[tq:, LANES:], 0.0))
            return dk_a, dv_a

        init = (jnp.zeros((tk, 2 * LANES), F32), jnp.zeros((tk, LANES), F32))
        carry = q_block(j, init, True)
        dk_a, dv_a = lax.fori_loop(j + 1, nq, lambda i, c: q_block(i, c, False), carry)
        dk_ref[...] = dk_a[:, :LANES].astype(BF16)
        dv_ref[...] = dv_a.astype(BF16)
        ksl = pl.ds(pl.multiple_of(j * tk, tk), tk)
        lk = _iota((tk, LANES), 1)
        dcx_ref[ksl, :] += jnp.where((lk == first + 3) | (lk == first + AUG_LANES + 3), dk_a[:, LANES:], 0.0)

        @pl.when(j == nk - 1)
        def _():
            dq_ref[...] = (dq_acc[...] * 0.125).astype(BF16)

        @pl.when((p == 3) & (j == nk - 1))
        def _():
            ex.wait(2, part_refs, land2, ex_sems)

    sh = jax.ShapeDtypeStruct((T, ATTN_WIDTH), BF16)
    full = lambda cb: pl.BlockSpec((T, LANES), lambda p, j: (0, cb + p))
    blk = lambda cb: pl.BlockSpec((tk, LANES), lambda p, j: (j, cb + p))
    hbm_spec = pl.BlockSpec(memory_space=pl.ANY)
    return pl.pallas_call(
        body, grid=(4, nk),
        out_shape=[sh, sh, sh, jax.ShapeDtypeStruct((T, LANES), F32)] + ex.land_shapes(2),
        in_specs=[full(0), full(0), _const_spec((T, LANES)), _const_spec((T, LANES)), blk(4), blk(8),
                  pl.BlockSpec((tk, LANES), lambda p, j: (j, 0))] + [hbm_spec] * nr,
        out_specs=[full(0), blk(0), blk(0), pl.BlockSpec((T, LANES), lambda p, j: (0, 0))] + [hbm_spec] * nr,
        scratch_shapes=[pltpu.VMEM((T, LANES), F32)] + ex.sem_shapes(2),
        compiler_params=_params(58, 2), name="bwd_attn",
    )(qkv, dya, qab, dab, qkv, qkv, kaug, *parts)


def _bwd_cum(dcx, fl, bfp):
    T = fl.shape[0]
    tb = CUM_TILE

    def body(dcx_ref, fl_ref, b_ref, dfl_ref, dbf_ref):
        triu = (_iota((tb, tb), 0) <= _iota((tb, tb), 1)).astype(F32)
        r, c = _iota((LANES, LANES), 0), _iota((LANES, LANES), 1)
        sel = (((r == AUG_LANES * c) & (c < N_HEADS)).astype(F32)
               - ((r == AUG_LANES * c + 3) & (c < N_HEADS)).astype(F32))
        carry = jnp.zeros((1, LANES), F32)
        dbf = jnp.zeros((1, LANES), F32)
        for i in reversed(range(T // tb)):
            colblk = jnp.dot(dcx_ref[i * tb:(i + 1) * tb, :], sel, precision=HIGHEST, preferred_element_type=F32)
            rc = jnp.dot(triu, colblk, precision=HIGHEST, preferred_element_type=F32) + carry
            carry = rc[0:1, :]
            sig = jax.nn.sigmoid(fl_ref[i * tb:(i + 1) * tb, :] + b_ref[...])
            dfl = rc * (1.0 - sig)
            dfl_ref[i * tb:(i + 1) * tb, :] = dfl.astype(BF16)
            dbf = dbf + jnp.sum(dfl, axis=0, keepdims=True)
        dbf_ref[...] = dbf

    return pl.pallas_call(
        body,
        out_shape=(jax.ShapeDtypeStruct((T, LANES), BF16), jax.ShapeDtypeStruct((1, LANES), F32)),
        compiler_params=pltpu.CompilerParams(vmem_limit_bytes=32 * MIB), name="bwd_cum",
    )(dcx, fl, bfp)


def _bwd_in(dz, dqkvf, dgl, dh1, x2, g1, wz, wf, wg, rows, name, prev=None, stage=0, exchanged=None):
    T = x2.shape[0]
    tm = TOKEN_TILE
    first = rows[0] // tm
    nsteps = (rows[1] - rows[0]) // tm
    ex = _GradExchange([tuple(exchanged.shape[1:])]) if stage else None

    def body(dz_ref, dqkvf_ref, dgl_ref, dh1_ref, x_ref, g_ref, wz_ref, wf_ref, wg_ref, *rest):
        rest = list(rest)
        dx_prev, dg1_prev = (rest.pop(0), rest.pop(0)) if prev else (None, None)
        src_ref = rest.pop(0) if stage else None
        dx_ref, dg1_ref = rest.pop(0), rest.pop(0)
        land_ref = rest.pop(0) if stage else None
        acc_g, ex_sems = rest[0], rest[1:]
        i = pl.program_id(0)

        @pl.when(i == 0)
        def _():
            if stage:
                ex.start(stage, [src_ref], [land_ref], ex_sems)
            acc_g[...] = jnp.zeros_like(acc_g)

        dxn = _dot_nt(dz_ref[...], wz_ref[:, 0:1024])
        dxn = dxn + _dot_nt(dqkvf_ref[:, 0:1536], wz_ref[:, 1024:2560])
        dxn = dxn + _dot_nt(dqkvf_ref[:, 1536:1664], wf_ref[...])
        dxn = dxn + _dot_nt(dgl_ref[...], wg_ref[...])
        r1, xh = _rms_stats(x_ref[...])
        acc_g[...] += _fold8(dxn * xh)
        dx_ref[...] = dh1_ref[...] + _rms_bwd(dxn, xh, r1, g_ref[...])

        @pl.when(i == nsteps - 1)
        def _():
            total = jnp.sum(acc_g[...], axis=0, keepdims=True)
            dg1_ref[...] = total + dg1_prev[...] if prev else total
            if stage:
                ex.wait(stage, [src_ref], [land_ref], ex_sems)

    hbm_spec = pl.BlockSpec(memory_space=pl.ANY)
    rows_spec = lambda n: pl.BlockSpec((tm, n), lambda i: (i + first, 0))
    operands = [dz, dqkvf, dgl, dh1, x2, g1, wz, wf, wg]
    in_specs = [rows_spec(1024), rows_spec(1664), rows_spec(2048), rows_spec(D_MODEL), rows_spec(D_MODEL),
                _const_spec((1, D_MODEL)), _const_spec((D_MODEL, ZQKV_WIDTH)), _const_spec((D_MODEL, LANES)),
                _const_spec((D_MODEL, 2048))]
    aliases = {}
    if prev:
        aliases = {len(operands): 0}
        operands += list(prev)
        in_specs += [hbm_spec, _const_spec((1, D_MODEL))]
    if stage:
        operands.append(exchanged)
        in_specs.append(hbm_spec)
    return pl.pallas_call(
        body, grid=(nsteps,),
        out_shape=[jax.ShapeDtypeStruct((T, D_MODEL), F32), jax.ShapeDtypeStruct((1, D_MODEL), F32)]
        + (ex.land_shapes(stage) if stage else []),
        in_specs=in_specs,
        out_specs=[rows_spec(D_MODEL), pl.BlockSpec((1, D_MODEL), lambda i: (0, 0))] + ([hbm_spec] if stage else []),
        scratch_shapes=[pltpu.VMEM((SUBLANES, D_MODEL), F32)] + (ex.sem_shapes(stage) if stage else []),
        input_output_aliases=aliases,
        compiler_params=_params(48), name=name,
    )(*operands)


def _pad_rows(v, rows):
    return jnp.pad(v, ((0, rows - v.shape[0]), (0, 0)))


def _pack_small(g_mix_pre, b_forget, g_sgu, b_sgu, w_spatial, b_spatial, g_mix_post, g_ffn_pre, g_ffn_post):
    vec = lambda v: _pad_rows(v.reshape(-1, LANES), SUBLANES)
    return jnp.concatenate([
        w_spatial.reshape(-1, LANES), vec(g_mix_pre), _pad_rows(jnp.pad(b_forget, ((0, 0), (0, LANES - N_HEADS))), SUBLANES),
        vec(g_sgu), vec(b_sgu), vec(b_spatial), vec(g_mix_post), vec(g_ffn_pre), vec(g_ffn_post)], axis=0)


def _unpack_small(p):
    nw = N_HEADS * CHUNK * CHUNK // LANES
    blk = lambda k: p[nw + SUBLANES * k: nw + SUBLANES * (k + 1)]
    return dict(
        w_spatial=p[:nw].reshape(1, N_HEADS, CHUNK, CHUNK),
        g_mix_pre=blk(0).reshape(1, D_MODEL), b_forget=blk(1)[0:1, :N_HEADS],
        g_sgu=blk(2)[:4].reshape(1, SGU_WIDTH), b_sgu=blk(3)[:4].reshape(1, SGU_WIDTH),
        b_spatial=blk(4).reshape(1, N_HEADS, CHUNK), g_mix_post=blk(5).reshape(1, D_MODEL),
        g_ffn_pre=blk(6).reshape(1, D_MODEL), g_ffn_post=blk(7).reshape(1, D_MODEL))


def kernel(x, g_mix_pre, w_in, b_forget, g_sgu, b_sgu, w_spatial, b_spatial, w_branch_sgu, w_branch_attn, w_out, g_mix_post, g_ffn_pre, w_up, w_down, g_ffn_post, loss_target, m_g_mix_pre, m_w_in, m_b_forget, m_g_sgu, m_b_sgu, m_w_spatial, m_b_spatial, m_w_branch_sgu, m_w_branch_attn, m_w_out, m_g_mix_post, m_g_ffn_pre, m_w_up, m_w_down, m_g_ffn_post, v_g_mix_pre, v_w_in, v_b_forget, v_g_sgu, v_b_sgu, v_w_spatial, v_b_spatial, v_w_branch_sgu, v_w_branch_attn, v_w_out, v_g_mix_post, v_g_ffn_pre, v_w_up, v_w_down, v_g_ffn_post):
    T = x.shape[1]
    x2 = x.reshape(T, D_MODEL)
    tgt = loss_target.reshape(T, D_MODEL)

    wg_in = _gather_w_in(w_in[0])
    wz, wf, wgt = _assemble_w_in(wg_in)
    bfp = jnp.pad(b_forget, ((0, 0), (0, LANES - N_HEADS)))
    wsp = w_spatial[0]
    bT = b_spatial[0].T

    xn, zuv, qkv, fl, gt = _fwd_in(x2, g_mix_pre, wz, wf, wgt)
    cc, qaug, kaug = _fwd_cum(fl, bfp)
    ys = _fwd_sgu(zuv, g_sgu, b_sgu, wsp, bT)
    ya, lse, wbs, wba, wo, wup, wdn = _fwd_attn(
        qkv, qaug, kaug, (w_branch_sgu[0], w_branch_attn[0], w_out[0], w_up[0], w_down[0]))
    A, B, mg, o, h1 = _fwd_merge(ys, ya, gt, x2, wbs, wba, wo, g_mix_post)
    xn2, a, ddn, dy, loss_part, dg4 = _fwd_ffn_loss(h1, tgt, wup, wdn, g_ffn_pre, g_ffn_post)
    loss = lax.psum(jnp.sum(loss_part), ("x", "y", "c"))

    da, dh1, dg3 = _bwd_ffn(ddn, a, dy, h1, wup, wdn, g_ffn_pre)
    dw_up = _wgrad(xn2, da, "wgrad_up", tn=2048, block_cols=512)
    dw_down = _wgrad(a, ddn, "wgrad_down", relu2=True)
    dgl, dys, dya, qab, dab, dg2, dw_bs, dw_ba, dw_out = _bwd_merge(
        dh1, o, A, B, gt, ys, ya, mg, lse, cc, wbs, wba, wo, g_mix_post)
    col_blocks = lambda g, w: g.reshape(g.shape[0], N_DEV, w).transpose(1, 0, 2)
    row_blocks = lambda g, r: g.reshape(N_DEV, r, g.shape[1])
    early_names = ["w_branch_sgu", "w_branch_attn", "w_out", "w_up", "w_down"]
    early = [col_blocks(dw_bs, 128), col_blocks(dw_ba, 128), row_blocks(dw_out, 128), dw_up, row_blocks(dw_down, 512)]
    owners = _owner_indices()
    dzuv, dwsp, dbT, dgs, dbs, *early_land1 = _bwd_sgu(zuv, dys, g_sgu, b_sgu, wsp, bT, early)
    early_parts = [_chip_partials(g, l1, owners, "chip_partials_" + nm)
                   for g, l1, nm in zip(early, early_land1, early_names)]
    dq, dk, dv, dcx, *early_land2 = _bwd_attn(qkv, dya, qab, dab, kaug, early_parts)
    dfl, dbf = _bwd_cum(dcx, fl, bfp)
    dqkvf = jnp.concatenate([dq, dk, dv, dfl], axis=1)
    dw_z = _wgrad(xn, dzuv, "wgrad_in_z")
    dw_qkvf = _wgrad(xn, dqkvf, "wgrad_in_qkvf")
    dw_g = _wgrad(xn, dgl, "wgrad_in_gate")
    blocks_in = _block_dw_in(dw_z, dw_qkvf, dw_g)
    bwd_in_args = (dzuv, dqkvf, dgl, dh1, x2, g_mix_pre, wz, wf, wgt)
    dx, dg1, land1_in = _bwd_in(*bwd_in_args, (0, T // 4), "bwd_in_a", stage=1, exchanged=blocks_in)
    part_in = _chip_partials(blocks_in, land1_in, owners, "chip_partials_w_in")
    dx, dg1, land2_in = _bwd_in(*bwd_in_args, (T // 4, 3 * T // 4), "bwd_in_b", prev=(dx, dg1), stage=2,
                                exchanged=part_in)
    dx, dg1 = _bwd_in(*bwd_in_args, (3 * T // 4, T), "bwd_in_c", prev=(dx, dg1))

    small_g = _pack_small(dg1, dbf[:, :N_HEADS], dgs, dbs, dwsp[None], dbT[:, :N_HEADS].T[None], dg2, dg3, dg4)
    small_w = _pack_small(g_mix_pre, b_forget, g_sgu, b_sgu, w_spatial, b_spatial, g_mix_post, g_ffn_pre, g_ffn_post)
    small_m = _pack_small(m_g_mix_pre, m_b_forget, m_g_sgu, m_b_sgu, m_w_spatial, m_b_spatial, m_g_mix_post,
                          m_g_ffn_pre, m_g_ffn_post)
    small_v = _pack_small(v_g_mix_pre, v_b_forget, v_g_sgu, v_b_sgu, v_w_spatial, v_b_spatial, v_g_mix_post,
                          v_g_ffn_pre, v_g_ffn_post)
    sg, sd, sm, sv = (_unpack_small(t) for t in _allreduce_small_adamw(small_g, small_w, small_m, small_v))

    big = {}
    g_in = _reduced_grad(blocks_in, land1_in, land2_in, owners, "reduced_grad_w_in")[:, :IN_SHARD]
    d_, m_, v_ = _adamw(w_in[0], g_in, m_w_in[0], v_w_in[0], "adamw_w_in")
    big["w_in"] = (g_in[None], d_[None], m_[None], v_[None])
    early_wmv = [(w_branch_sgu, m_w_branch_sgu, v_w_branch_sgu), (w_branch_attn, m_w_branch_attn, v_w_branch_attn),
                 (w_out, m_w_out, v_w_out), (w_up, m_w_up, v_w_up), (w_down, m_w_down, v_w_down)]
    for nm, (w, m, v), g, l1, l2 in zip(early_names, early_wmv, early, early_land1, early_land2):
        big[nm] = tuple(t[None] for t in _adamw_reduced(w[0], m[0], v[0], g, l1, l2, owners, "adamw_" + nm))

    order = ["g_mix_pre", "w_in", "b_forget", "g_sgu", "b_sgu", "w_spatial", "b_spatial", "w_branch_sgu",
             "w_branch_attn", "w_out", "g_mix_post", "g_ffn_pre", "w_up", "w_down", "g_ffn_post"]
    outs = [loss, dx.reshape(1, T, D_MODEL)]
    for kind, small in enumerate((sg, sd, sm, sv)):
        outs += [big[nm][kind] if nm in big else small[nm] for nm in order]
    return tuple(outs)
```

```python
import jax
import jax.numpy as jnp
from jax import lax
from jax.experimental import pallas as pl
from jax.experimental.pallas import tpu as pltpu

F32 = jnp.float32
BF16 = jnp.bfloat16
HIGHEST = lax.Precision.HIGHEST
MESH = pl.DeviceIdType.MESH

D_MODEL = 1024
SGU_WIDTH = 512
ATTN_WIDTH = 512
N_HEADS = 8
CHUNK = 128
D_FF = 4096
IN_WIDTH = 4616
N_DEV = 8
IN_SHARD = IN_WIDTH // N_DEV
IN_SHARD_PAD = 640
ZQKV_WIDTH = 2 * SGU_WIDTH + 3 * ATTN_WIDTH
GATE_OFFSET = ZQKV_WIDTH + N_HEADS
EPS = 1e-6
LANES = 128
SUBLANES = 8
VMEM_BYTES = 64 * 1024 * 1024
MIB = 1024 * 1024

ADAM_LR = 0.001
ADAM_B1 = 0.9
ADAM_B2 = 0.999
ADAM_EPS = 1e-08
ADAM_WD = 0.01
ADAM_STEP = 10

TOKEN_TILE = 256
MERGE_BWD_TILE = 512
ATTN_TILE = 512
CUM_TILE = 256
SGU_TILE = 512
WGRAD_TILE = 1024
NEG = -1e30

NT_DIMS = (((1,), (1,)), ((), ()))
TN_DIMS = (((0,), (0,)), ((), ()))


def _params(vmem_mb, n_grid=1):
    return pltpu.CompilerParams(
        dimension_semantics=("arbitrary",) * n_grid,
        vmem_limit_bytes=min(vmem_mb * MIB, VMEM_BYTES - 6 * MIB),
    )


def _dot(a, b):
    return jnp.dot(a, b, preferred_element_type=F32)


def _dot_nt(a, b):
    return lax.dot_general(a, b, NT_DIMS, preferred_element_type=F32)


def _dot_tn(a, b):
    return lax.dot_general(a, b, TN_DIMS, preferred_element_type=F32)


def _const_spec(shape):
    nd = len(shape)
    return pl.BlockSpec(shape, lambda *_: (0,) * nd, pipeline_mode=pl.Buffered(1))


def _row_spec(tm, n, col=0):
    return pl.BlockSpec((tm, n), lambda i: (i, col))


def _fold8(v):
    return v.reshape(v.shape[0] // SUBLANES, SUBLANES, v.shape[1]).sum(axis=0)


def _pick(v, lane_iota, k):
    return jnp.sum(jnp.where(lane_iota == k, v, 0.0), axis=1, keepdims=True)


def _iota(shape, dim):
    return lax.broadcasted_iota(jnp.int32, shape, dim)


def _gelu(x):
    c = 0.7978845608028654
    return 0.5 * x * (1.0 + jnp.tanh(c * (x + 0.044715 * x * x * x)))


def _gelu_grad(x):
    c = 0.7978845608028654
    t = jnp.tanh(c * (x + 0.044715 * x * x * x))
    return 0.5 * (1.0 + t) + 0.5 * x * (1.0 - t * t) * (c * (1.0 + 3.0 * 0.044715 * x * x))


def _rms_stats(v):
    r = lax.rsqrt(jnp.mean(v * v, axis=1, keepdims=True) + EPS)
    return r, v * r


def _rms_bwd(dout, vhat, r, g):
    a = dout * g
    return r * (a - vhat * jnp.mean(a * vhat, axis=1, keepdims=True))


def _mesh_pos():
    return lax.axis_index("x"), lax.axis_index("y"), lax.axis_index("c")


def _dev_index(px, py, pc):
    return 4 * px + 2 * py + pc


def _other_chips(x, y):
    return [(1 - x, y), (x, 1 - y), (1 - x, 1 - y)]


class _WeightGather:
    def __init__(self, shard_shapes, kinds, stage_shapes=None):
        self.shard_shapes = list(shard_shapes)
        self.kinds = list(kinds)
        self.stage_shapes = list(stage_shapes or shard_shapes)
        self.n = len(self.kinds)

    def out_shapes(self):
        shapes = []
        for (rows, cols), kind in zip(self.stage_shapes, self.kinds):
            full = {"block": (N_DEV, rows, cols), "rows": (N_DEV * rows, cols), "cols": (rows, N_DEV * cols)}[kind]
            shapes.append(jax.ShapeDtypeStruct(full, BF16))
        return shapes

    def scratch_shapes(self):
        return ([pltpu.VMEM(s, BF16) for s in self.stage_shapes]
                + [pltpu.SemaphoreType.DMA((self.n, 7)), pltpu.SemaphoreType.DMA((self.n, 7)),
                   pltpu.SemaphoreType.DMA((self.n,))])

    def _view(self, a, ref, j):
        rows, cols = self.stage_shapes[a]
        if self.kinds[a] == "block":
            return ref.at[j]
        if self.kinds[a] == "rows":
            return ref.at[pl.ds(pl.multiple_of(j * rows, rows), rows), :]
        return ref.at[:, pl.ds(pl.multiple_of(j * cols, cols), cols)]

    def _copy(self, outs, scratch, a, k, block, to, from_stage=False):
        stage, (send_sems, recv_sems, _) = scratch[:self.n], scratch[self.n:]
        dst = self._view(a, outs[a], _dev_index(*block))
        return pltpu.make_async_remote_copy(
            src_ref=stage[a] if from_stage else dst, dst_ref=dst,
            send_sem=send_sems.at[a, k], recv_sem=recv_sems.at[a, k],
            device_id=to, device_id_type=MESH)

    def _local(self, outs, scratch, a, me):
        return pltpu.make_async_copy(scratch[a], self._view(a, outs[a], _dev_index(*me)), scratch[-1].at[a])

    def start(self, ins, outs, scratch):
        x, y, c = _mesh_pos()
        me, sibling = (x, y, c), (x, y, 1 - c)
        for a in range(self.n):
            rows, cols = self.shard_shapes[a]
            if self.stage_shapes[a] != self.shard_shapes[a]:
                scratch[a][...] = jnp.zeros(self.stage_shapes[a], BF16)
            scratch[a][0:rows, 0:cols] = ins[a][...].astype(BF16)
            self._local(outs, scratch, a, me).start()
        for a in range(self.n):
            self._copy(outs, scratch, a, 0, me, sibling, True).start()
            for j, chip in enumerate(_other_chips(x, y)):
                self._copy(outs, scratch, a, 1 + j, me, (*chip, c), True).start()

    def forward(self, outs, scratch):
        x, y, c = _mesh_pos()
        me, sibling = (x, y, c), (x, y, 1 - c)
        for a in range(self.n):
            for j, chip in enumerate(_other_chips(x, y)):
                self._copy(outs, scratch, a, 1 + j, (*chip, c), me).wait_recv()
                self._copy(outs, scratch, a, 4 + j, (*chip, c), sibling).start()

    def finish(self, outs, scratch):
        x, y, c = _mesh_pos()
        me, sibling = (x, y, c), (x, y, 1 - c)
        chips = _other_chips(x, y)
        for a in range(self.n):
            self._copy(outs, scratch, a, 0, sibling, me).wait_recv()
            for j, chip in enumerate(chips):
                self._copy(outs, scratch, a, 4 + j, (*chip, 1 - c), me).wait_recv()
        for a in range(self.n):
            self._copy(outs, scratch, a, 0, me, sibling, True).wait_send()
            for j, chip in enumerate(chips):
                self._copy(outs, scratch, a, 1 + j, me, (*chip, c), True).wait_send()
                self._copy(outs, scratch, a, 4 + j, (*chip, c), sibling).wait_send()
            self._local(outs, scratch, a, me).wait()


def _gather_w_in(w_in_local):
    g = _WeightGather([(D_MODEL, IN_SHARD)], ["block"], [(D_MODEL, IN_SHARD_PAD)])

    def body(w_ref, out_ref, *scratch):
        g.start([w_ref], [out_ref], scratch)
        g.forward([out_ref], scratch)
        g.finish([out_ref], scratch)

    return pl.pallas_call(
        body,
        out_shape=g.out_shapes()[0],
        in_specs=[pl.BlockSpec(memory_space=pltpu.VMEM)],
        out_specs=pl.BlockSpec(memory_space=pl.ANY),
        scratch_shapes=g.scratch_shapes(),
        compiler_params=pltpu.CompilerParams(vmem_limit_bytes=32 * MIB),
        name="gather_w_in",
    )(w_in_local)


class _GradExchange:
    def __init__(self, shapes):
        self.shapes = [tuple(s) for s in shapes]
        self.n = len(self.shapes)

    def land_shapes(self, stage):
        slots, dtype = (4, F32) if stage == 1 else (3, BF16)
        return [jax.ShapeDtypeStruct((slots,) + s, dtype) for s in self.shapes]

    def sem_shapes(self, stage):
        slots = 4 if stage == 1 else 3
        return [pltpu.SemaphoreType.DMA((self.n, slots)), pltpu.SemaphoreType.DMA((self.n, slots))]

    def _copy(self, stage, srcs, lands, sems, a, k):
        x, y, c = _mesh_pos()
        cx, cy = (_other_chips(x, y) + [(x, y)])[k]
        if stage == 1:
            src, to = srcs[a].at[_dev_index(cx, cy, 1 - c)], (x, y, 1 - c)
        else:
            src, to = srcs[a].at[k], (cx, cy, c)
        return pltpu.make_async_remote_copy(
            src_ref=src, dst_ref=lands[a].at[k], send_sem=sems[0].at[a, k], recv_sem=sems[1].at[a, k],
            device_id=to, device_id_type=MESH)

    def start(self, stage, srcs, lands, sems):
        for a in range(self.n):
            for k in range(4 if stage == 1 else 3):
                self._copy(stage, srcs, lands, sems, a, k).start()

    def wait(self, stage, srcs, lands, sems):
        for a in range(self.n):
            for k in range(4 if stage == 1 else 3):
                cp = self._copy(stage, srcs, lands, sems, a, k)
                cp.wait_recv()
                cp.wait_send()


def _owner_indices():
    x, y, c = _mesh_pos()
    return jnp.stack([_dev_index(cx, cy, c) for cx, cy in _other_chips(x, y) + [(x, y)]]).astype(jnp.int32)


def _chip_partials(g, land1, idx, name):
    _, rows, cols = g.shape
    tr = min(rows, 256)

    def body(idx_ref, g_ref, l_ref, o_ref):
        o_ref[...] = (g_ref[...] + l_ref[...]).astype(BF16)

    return pl.pallas_call(
        body,
        grid_spec=pltpu.PrefetchScalarGridSpec(
            num_scalar_prefetch=1, grid=(3, rows // tr),
            in_specs=[pl.BlockSpec((None, tr, cols), lambda k, r, idx: (idx[k], r, 0)),
                      pl.BlockSpec((None, tr, cols), lambda k, r, idx: (k, r, 0))],
            out_specs=pl.BlockSpec((None, tr, cols), lambda k, r, idx: (k, r, 0))),
        out_shape=jax.ShapeDtypeStruct((3, rows, cols), BF16),
        compiler_params=_params(32, 2), name=name,
    )(idx, g, land1)


def _reduced_block(g_ref, l1_ref, a_ref, b_ref, c_ref):
    return ((g_ref[...] + l1_ref[...]) + a_ref[...].astype(F32)) + b_ref[...].astype(F32) + c_ref[...].astype(F32)


def _reduced_specs(tm, cols):
    return [pl.BlockSpec((None, tm, cols), lambda i, idx: (idx[3], i, 0)),
            pl.BlockSpec((None, tm, cols), lambda i, idx: (3, i, 0)),
            pl.BlockSpec((None, tm, cols), lambda i, idx: (0, i, 0)),
            pl.BlockSpec((None, tm, cols), lambda i, idx: (1, i, 0)),
            pl.BlockSpec((None, tm, cols), lambda i, idx: (2, i, 0))]


def _reduced_grad(g, land1, land2, idx, name):
    _, rows, cols = g.shape
    tm = min(rows, 256)

    def body(idx_ref, g_ref, l1_ref, a_ref, b_ref, c_ref, o_ref):
        o_ref[...] = _reduced_block(g_ref, l1_ref, a_ref, b_ref, c_ref)

    return pl.pallas_call(
        body,
        grid_spec=pltpu.PrefetchScalarGridSpec(
            num_scalar_prefetch=1, grid=(rows // tm,), in_specs=_reduced_specs(tm, cols),
            out_specs=pl.BlockSpec((tm, cols), lambda i, idx: (i, 0))),
        out_shape=jax.ShapeDtypeStruct((rows, cols), F32),
        compiler_params=_params(32), name=name,
    )(idx, g, land1, land2, land2, land2)


def _adamw_math(w, g, m, v):
    m = ADAM_B1 * m + (1.0 - ADAM_B1) * g
    v = ADAM_B2 * v + (1.0 - ADAM_B2) * (g * g)
    m_hat = m / (1.0 - ADAM_B1 ** ADAM_STEP)
    v_hat = v / (1.0 - ADAM_B2 ** ADAM_STEP)
    delta = -ADAM_LR * (m_hat / (jnp.sqrt(v_hat) + ADAM_EPS) + ADAM_WD * w)
    return delta, m, v


SMALL_NAMES = ("g_mix_pre", "b_forget", "g_sgu", "b_sgu", "w_spatial", "b_spatial", "g_mix_post", "g_ffn_pre",
               "g_ffn_post")
SMALL_SLOTS = {"g_mix_pre": (0, 1, 0, 1024), "g_mix_post": (1, 1, 0, 1024), "g_ffn_pre": (2, 1, 0, 1024),
               "g_ffn_post": (3, 1, 0, 1024), "g_sgu": (4, 1, 0, 512), "b_sgu": (4, 1, 512, 512),
               "b_forget": (5, 1, 0, 128), "b_spatial": (8, 8, 0, 128)}
SMALL_TILE = (16, 1024)
SPATIAL_TILE = (N_HEADS * CHUNK, CHUNK)


def _allreduce_small(grads):
    names = list(SMALL_NAMES)

    def body(*refs):
        g = dict(zip(names, refs[:len(names)]))
        tot_a, tot_b, buf_a, buf_b, sib_a, sib_b, ps_a, ps_b, land_a, land_b, send_sems, recv_sems = refs[len(names):]
        x, y, c = _mesh_pos()
        buf_a[...] = jnp.zeros(SMALL_TILE, F32)
        for name, (r0, nr, c0, nc) in SMALL_SLOTS.items():
            val = g[name][...]
            if name == "b_spatial":
                val = jnp.transpose(val)[0:N_HEADS, :]
            buf_a[r0:r0 + nr, c0:c0 + nc] = val
        buf_b[...] = g["w_spatial"][...].reshape(SPATIAL_TILE)

        def swap(k, src, dst, to):
            return pltpu.make_async_remote_copy(src_ref=src, dst_ref=dst, send_sem=send_sems.at[k],
                                                recv_sem=recv_sems.at[k], device_id=to, device_id_type=MESH)

        first = [swap(0, buf_a, sib_a, (x, y, 1 - c)), swap(1, buf_b, sib_b, (x, y, 1 - c))]
        for cp in first:
            cp.start()
        for cp in first:
            cp.wait_recv()
        ps_a[...] = buf_a[...] + sib_a[...]
        ps_b[...] = buf_b[...] + sib_b[...]
        second = []
        for k, (cx, cy) in enumerate(_other_chips(x, y)):
            second += [swap(2 + 2 * k, ps_a, land_a.at[k], (cx, cy, c)), swap(3 + 2 * k, ps_b, land_b.at[k], (cx, cy, c))]
        for cp in second:
            cp.start()
        for cp in second:
            cp.wait_recv()
        tot_a[...] = (ps_a[...] + land_a[0]) + (land_a[1] + land_a[2])
        tot_b[...] = (ps_b[...] + land_b[0]) + (land_b[1] + land_b[2])
        for cp in first + second:
            cp.wait_send()

    vm = pl.BlockSpec(memory_space=pltpu.VMEM)
    return pl.pallas_call(
        body,
        out_shape=(jax.ShapeDtypeStruct(SMALL_TILE, F32), jax.ShapeDtypeStruct(SPATIAL_TILE, F32)),
        in_specs=[vm] * len(names), out_specs=[vm, vm],
        scratch_shapes=[pltpu.VMEM(SMALL_TILE, F32), pltpu.VMEM(SPATIAL_TILE, F32)] * 3
        + [pltpu.VMEM((3,) + SMALL_TILE, F32), pltpu.VMEM((3,) + SPATIAL_TILE, F32),
           pltpu.SemaphoreType.DMA((8,)), pltpu.SemaphoreType.DMA((8,))],
        compiler_params=pltpu.CompilerParams(vmem_limit_bytes=32 * MIB),
        name="allreduce_small",
    )(*[grads[nm] for nm in names])


def _adamw_small(tot_a, tot_b, ws, ms, vs):
    names = list(SMALL_NAMES)
    n = len(names)

    def body(a_ref, b_ref, *refs):
        w, m, v = (dict(zip(names, refs[i * n:(i + 1) * n])) for i in range(3))
        outs = [dict(zip(names, refs[(3 + i) * n:(4 + i) * n])) for i in range(4)]
        for name in names:
            if name == "w_spatial":
                g = b_ref[...].reshape(N_HEADS, CHUNK, CHUNK)
            else:
                r0, nr, c0, nc = SMALL_SLOTS[name]
                g = a_ref[r0:r0 + nr, c0:c0 + nc]
            vals = (g,) + _adamw_math(w[name][...], g, m[name][...], v[name][...])
            for out, val in zip(outs, vals):
                out[name][...] = val

    shapes = [jax.ShapeDtypeStruct(ws[nm].shape, F32) for nm in names]
    vm = pl.BlockSpec(memory_space=pltpu.VMEM)
    res = pl.pallas_call(
        body, out_shape=shapes * 4, in_specs=[vm] * (2 + 3 * n), out_specs=[vm] * (4 * n),
        compiler_params=pltpu.CompilerParams(vmem_limit_bytes=32 * MIB), name="adamw_small",
    )(tot_a, tot_b, *[d[nm] for d in (ws, ms, vs) for nm in names])
    return [dict(zip(names, res[i * n:(i + 1) * n])) for i in range(4)]


def _adamw_reduced(w, m, v, g, land1, land2, idx, name):
    rows, cols = w.shape
    tm = min(rows, 256)

    def body(idx_ref, w_ref, m_ref, v_ref, g_ref, l1_ref, a_ref, b_ref, c_ref, go_ref, d_ref, nm_ref, nv_ref):
        gsum = _reduced_block(g_ref, l1_ref, a_ref, b_ref, c_ref)
        go_ref[...] = gsum
        delta, nm, nv = _adamw_math(w_ref[...], gsum, m_ref[...], v_ref[...])
        d_ref[...] = delta
        nm_ref[...] = nm
        nv_ref[...] = nv

    sd = jax.ShapeDtypeStruct((rows, cols), F32)
    spec = pl.BlockSpec((tm, cols), lambda i, idx: (i, 0))
    return pl.pallas_call(
        body,
        grid_spec=pltpu.PrefetchScalarGridSpec(
            num_scalar_prefetch=1, grid=(rows // tm,), in_specs=[spec] * 3 + _reduced_specs(tm, cols),
            out_specs=[spec] * 4),
        out_shape=(sd, sd, sd, sd),
        compiler_params=_params(32), name=name,
    )(idx, w, m, v, g, land1, land2, land2, land2)


def _adamw(w, g, m, v, name):
    rows, cols = w.shape
    tm = 256 if rows % 256 == 0 else rows

    def body(w_ref, g_ref, m_ref, v_ref, d_ref, nm_ref, nv_ref):
        delta, nm, nv = _adamw_math(w_ref[...], g_ref[...], m_ref[...], v_ref[...])
        d_ref[...] = delta
        nm_ref[...] = nm
        nv_ref[...] = nv

    sd = jax.ShapeDtypeStruct((rows, cols), F32)
    spec = _row_spec(tm, cols)
    return pl.pallas_call(
        body, grid=(rows // tm,), out_shape=(sd, sd, sd), in_specs=[spec] * 4, out_specs=[spec] * 3,
        compiler_params=_params(32), name=name,
    )(w, g, m, v)


def _virtual_slab(sources, v0, v_end, like):
    lane = _iota(like.shape, 1)
    out = jnp.zeros(like.shape, like.dtype)
    for v_start, v_stop, read in sources:
        a, b = max(v0, v_start), min(v0 + LANES, v_stop, v_end)
        while a < b:
            c = a - v_start
            n = min(b - a, LANES - c % LANES)
            piece = read(c // LANES)
            shift = (a - v0 - c % LANES) % LANES
            if shift:
                piece = pltpu.roll(piece, shift, 1)
            out = jnp.where((lane >= a - v0) & (lane < a - v0 + n), piece, out)
            a += n
    return out


def _assemble_w_in(wg_in):
    tm = TOKEN_TILE

    def body(src_ref, wz_ref, wf_ref, wg_ref):
        like = src_ref[0, :, 0:LANES]
        sources = [(IN_SHARD * j, IN_SHARD * (j + 1),
                    (lambda k, j=j: src_ref[j, :, LANES * k:LANES * (k + 1)])) for j in range(N_DEV)]
        for k in range(ZQKV_WIDTH // LANES):
            wz_ref[:, LANES * k:LANES * (k + 1)] = _virtual_slab(sources, LANES * k, ZQKV_WIDTH, like)
        wf_ref[...] = _virtual_slab(sources, ZQKV_WIDTH, GATE_OFFSET, like)
        for k in range(2 * D_MODEL // LANES):
            wg_ref[:, LANES * k:LANES * (k + 1)] = _virtual_slab(sources, GATE_OFFSET + LANES * k, IN_WIDTH, like)

    return pl.pallas_call(
        body, grid=(D_MODEL // tm,),
        out_shape=(jax.ShapeDtypeStruct((D_MODEL, ZQKV_WIDTH), BF16), jax.ShapeDtypeStruct((D_MODEL, LANES), BF16),
                   jax.ShapeDtypeStruct((D_MODEL, 2 * D_MODEL), BF16)),
        in_specs=[pl.BlockSpec((N_DEV, tm, IN_SHARD_PAD), lambda i: (0, i, 0))],
        out_specs=[_row_spec(tm, ZQKV_WIDTH), _row_spec(tm, LANES), _row_spec(tm, 2 * D_MODEL)],
        compiler_params=_params(32), name="assemble_w_in",
    )(wg_in)


def _block_dw_in(dw_z, dw_qkvf, dw_g):
    tm = TOKEN_TILE
    n_qkvf = 3 * ATTN_WIDTH + N_HEADS

    def body(z_ref, q_ref, g_ref, out_ref):
        like = z_ref[:, 0:LANES]
        slab = lambda ref: (lambda k: ref[:, LANES * k:LANES * (k + 1)])
        sources = [(0, 2 * SGU_WIDTH, slab(z_ref)), (2 * SGU_WIDTH, 2 * SGU_WIDTH + n_qkvf, slab(q_ref)),
                   (GATE_OFFSET, IN_WIDTH, slab(g_ref))]
        for j in range(N_DEV):
            for k in range(IN_SHARD_PAD // LANES):
                out_ref[j, :, LANES * k:LANES * (k + 1)] = _virtual_slab(
                    sources, IN_SHARD * j + LANES * k, IN_SHARD * (j + 1), like)

    return pl.pallas_call(
        body, grid=(D_MODEL // tm,),
        out_shape=jax.ShapeDtypeStruct((N_DEV, D_MODEL, IN_SHARD_PAD), F32),
        in_specs=[_row_spec(tm, 2 * SGU_WIDTH), _row_spec(tm, dw_qkvf.shape[1]), _row_spec(tm, 2 * D_MODEL)],
        out_specs=pl.BlockSpec((N_DEV, tm, IN_SHARD_PAD), lambda i: (0, i, 0)),
        compiler_params=_params(32), name="block_dw_in",
    )(dw_z, dw_qkvf, dw_g)


def _fwd_in(x2, g1, wz, wf, wg):
    T = x2.shape[0]
    tm = TOKEN_TILE

    def body(x_ref, g_ref, wz_ref, wf_ref, wg_ref, xn_ref, zuv_ref, qkv_ref, fl_ref, gt_ref):
        x = x_ref[...]
        r, xh = _rms_stats(x)
        xn = (xh * g_ref[...]).astype(BF16)
        xn_ref[...] = xn
        zuv_ref[...] = _dot(xn, wz_ref[:, 0:1024]).astype(BF16)
        qkv_ref[:, 0:512] = (_dot(xn, wz_ref[:, 1024:1536]) * 0.125).astype(BF16)
        qkv_ref[:, 512:1536] = _dot(xn, wz_ref[:, 1536:2560]).astype(BF16)
        fl_ref[...] = _dot(xn, wf_ref[...])
        gt_ref[...] = jax.nn.sigmoid(_dot(xn, wg_ref[...])).astype(BF16)

    return pl.pallas_call(
        body, grid=(T // tm,),
        out_shape=(jax.ShapeDtypeStruct((T, D_MODEL), BF16), jax.ShapeDtypeStruct((T, 1024), BF16),
                   jax.ShapeDtypeStruct((T, 1536), BF16), jax.ShapeDtypeStruct((T, LANES), F32),
                   jax.ShapeDtypeStruct((T, 2048), BF16)),
        in_specs=[_row_spec(tm, D_MODEL), _const_spec((1, D_MODEL)), _const_spec((D_MODEL, ZQKV_WIDTH)),
                  _const_spec((D_MODEL, LANES)), _const_spec((D_MODEL, 2048))],
        out_specs=[_row_spec(tm, D_MODEL), _row_spec(tm, 1024), _row_spec(tm, 1536), _row_spec(tm, LANES),
                   _row_spec(tm, 2048)],
        compiler_params=_params(48), name="fwd_in",
    )(x2, g1, wz, wf, wg)


def _log_sigmoid(f):
    return jnp.minimum(f, 0.0) - jnp.log1p(jnp.exp(-jnp.abs(f)))


AUG_LANES = 6


def _split3(v):
    hi = v.astype(BF16)
    r1 = v - hi.astype(F32)
    mid = r1.astype(BF16)
    lo = (r1 - mid.astype(F32)).astype(BF16)
    return hi, mid, lo


def _spread(parts, k0):
    r, c = _iota((LANES, LANES), 0), _iota((LANES, LANES), 1)
    out = None
    for i, part in enumerate(parts):
        e = ((c == AUG_LANES * r + (k0 + i)) & (r < N_HEADS)).astype(BF16)
        term = _dot(part, e)
        out = term if out is None else out + term
    return out


def _aug_query(v):
    ones = (_iota(v.shape, 1) < N_HEADS).astype(BF16)
    return (_spread(_split3(v), 0) + _spread((ones, ones, ones), 3)).astype(BF16)


def _aug_key(v):
    ones = (_iota(v.shape, 1) < N_HEADS).astype(BF16)
    return (_spread((ones, ones, ones), 0) - _spread(_split3(v), 3)).astype(BF16)


def _aug_stack(t2, aug, p):
    lane = _iota(t2.shape, 1)
    low = lane < 64
    zero = jnp.zeros_like(t2)
    first = 2 * AUG_LANES * p
    a_e = jnp.where((lane >= first) & (lane < first + AUG_LANES), aug, zero)
    a_o = jnp.where((lane >= first + AUG_LANES) & (lane < first + 2 * AUG_LANES), aug, zero)
    top = jnp.concatenate([jnp.where(low, t2, zero), a_e], axis=1)
    bot = jnp.concatenate([jnp.where(low, zero, t2), a_o], axis=1)
    return jnp.concatenate([top, bot], axis=0)


def _fwd_cum(fl, bfp):
    T = fl.shape[0]
    tb = CUM_TILE

    def body(fl_ref, b_ref, cc_ref, qa_ref, ka_ref):
        tri = (_iota((tb, tb), 0) >= _iota((tb, tb), 1)).astype(F32)
        carry = jnp.zeros((1, LANES), F32)
        for i in range(T // tb):
            rows = slice(i * tb, (i + 1) * tb)
            lf = _log_sigmoid(fl_ref[rows, :] + b_ref[...])
            cs = jnp.dot(tri, lf, precision=HIGHEST, preferred_element_type=F32) + carry
            cc_ref[rows, :] = cs
            carry = cs[tb - 1:tb, :]
            qa_ref[rows, :] = _aug_query(cs)
            ka_ref[rows, :] = _aug_key(cs)

    return pl.pallas_call(
        body,
        out_shape=(jax.ShapeDtypeStruct((T, LANES), F32), jax.ShapeDtypeStruct((T, LANES), BF16),
                   jax.ShapeDtypeStruct((T, LANES), BF16)),
        compiler_params=pltpu.CompilerParams(vmem_limit_bytes=32 * MIB), name="fwd_cum",
    )(fl, bfp)


def _sgu_forward_parts(z, gs, bs):
    u = _gelu(z[:, :SGU_WIDTH])
    vv = _gelu(z[:, SGU_WIDTH:])
    vc = vv - jnp.mean(vv, axis=1, keepdims=True)
    rs = lax.rsqrt(jnp.mean(vc * vc, axis=1, keepdims=True) + EPS)
    vhat = vc * rs
    return u, vhat, rs, vhat * gs + bs


def _sgu_pair_weights(w_ref, bT, p):
    tril = _iota((CHUNK, CHUNK), 0) >= _iota((CHUNK, CHUNK), 1)
    we = jnp.where(tril, w_ref[2 * p], 0.0).astype(BF16)
    wo = jnp.where(tril, w_ref[2 * p + 1], 0.0).astype(BF16)
    lane8 = _iota(bT.shape, 1)
    low = _iota((CHUNK, LANES), 1) < 64
    b2 = jnp.where(low, _pick(bT, lane8, 2 * p), _pick(bT, lane8, 2 * p + 1))
    return we, wo, b2


def _chunks_on_lanes(v, p, nc):
    return jnp.concatenate([v[c * CHUNK:(c + 1) * CHUNK, LANES * p:LANES * (p + 1)] for c in range(nc)], axis=1)


def _sgu_mix(we, wo, b2, vcat, nc):
    low = (_iota((CHUNK, nc * LANES), 1) % LANES) < 64
    return jnp.where(low, _dot(we, vcat), _dot(wo, vcat)) + jnp.concatenate([b2] * nc, axis=1)


def _fwd_sgu(zuv, gs, bs, wsp, bT):
    T = zuv.shape[0]
    tc = SGU_TILE
    nc = tc // CHUNK

    def body(z_ref, gs_ref, bs_ref, w_ref, bT_ref, y_ref):
        u, _, _, vln = _sgu_forward_parts(z_ref[...].astype(F32), gs_ref[...], bs_ref[...])
        vb = vln.astype(BF16)
        for p in range(4):
            we, wo, b2 = _sgu_pair_weights(w_ref, bT_ref[...], p)
            s = _sgu_mix(we, wo, b2, _chunks_on_lanes(vb, p, nc), nc)
            for c in range(nc):
                rows, cols = slice(c * CHUNK, (c + 1) * CHUNK), slice(LANES * p, LANES * (p + 1))
                y_ref[rows, cols] = (u[rows, cols] * s[:, c * LANES:(c + 1) * LANES]).astype(BF16)

    return pl.pallas_call(
        body, grid=(T // tc,), out_shape=jax.ShapeDtypeStruct((T, SGU_WIDTH), BF16),
        in_specs=[_row_spec(tc, 1024), _const_spec((1, SGU_WIDTH)), _const_spec((1, SGU_WIDTH)),
                  _const_spec((8, CHUNK, CHUNK)), _const_spec((CHUNK, 8))],
        out_specs=_row_spec(tc, SGU_WIDTH),
        compiler_params=_params(40), name="fwd_sgu",
    )(zuv, gs, bs, wsp, bT)


def _fwd_attn(qkv, qaug, kaug, w_shards):
    T = qkv.shape[0]
    tq = tk = ATTN_TILE
    nq = T // tq
    gather = _WeightGather([w.shape for w in w_shards], ["cols", "cols", "rows", "cols", "rows"])
    nw = gather.n

    def body(q_ref, qa_ref, k_ref, v_ref, ka_ref, *rest):
        w_refs, (o_ref, lse_ref), wg_refs, scratch = rest[:nw], rest[nw:nw + 2], rest[nw + 2:2 * nw + 2], rest[2 * nw + 2:]
        i = pl.program_id(0)

        @pl.when(i == 0)
        def _():
            gather.start(w_refs, wg_refs, scratch)

        @pl.when(i == nq // 2)
        def _():
            gather.forward(wg_refs, scratch)

        lane = _iota((tq, LANES), 1)
        low = lane < 64
        lowk = _iota((tk, LANES), 1) < 64
        one = jnp.ones((tk, LANES), BF16)
        row = _iota((2 * tq, tk), 0) % tq
        col = _iota((2 * tq, tk), 1)
        cols = [slice(LANES * p, LANES * (p + 1)) for p in range(4)]
        qa = qa_ref[...]
        qs = [_aug_stack(q_ref[:, cols[p]], qa, p) for p in range(4)]

        def step(j, carry, masked):
            ks = pl.ds(pl.multiple_of(j * tk, tk), tk)
            ka = ka_ref[ks, :]
            new = []
            for p in range(4):
                m, acc_e, acc_o = carry[p]
                v2 = v_ref[ks, cols[p]]
                s = _dot_nt(qs[p], jnp.concatenate([k_ref[ks, cols[p]], ka], axis=1))
                if masked:
                    s = jnp.where(col <= row, s, NEG)
                mn = jnp.maximum(m, jnp.max(s, axis=1, keepdims=True))
                al = jnp.exp(m - mn)
                pm = jnp.exp(s - mn).astype(BF16)
                acc_e = al[:tq] * acc_e + _dot(pm[:tq], jnp.where(lowk, v2, one))
                acc_o = al[tq:] * acc_o + _dot(pm[tq:], jnp.where(lowk, one, v2))
                new.append((mn, acc_e, acc_o))
            return tuple(new)

        init = tuple((jnp.full((2 * tq, 1), NEG, F32), jnp.zeros((tq, LANES), F32), jnp.zeros((tq, LANES), F32))
                     for _ in range(4))
        carry = lax.fori_loop(0, i, lambda j, c: step(j, c, False), init)
        carry = step(i, carry, True)
        lse_blk = jnp.zeros((tq, LANES), F32)
        for p in range(4):
            m, acc_e, acc_o = carry[p]
            l_e = pltpu.roll(acc_e, 64, 1)
            l_o = pltpu.roll(acc_o, 64, 1)
            o_ref[:, cols[p]] = jnp.where(low, acc_e / l_e, acc_o / l_o).astype(BF16)
            lse_blk = jnp.where(lane == 2 * p, m[:tq] + jnp.log(l_e), lse_blk)
            lse_blk = jnp.where(lane == 2 * p + 1, m[tq:] + jnp.log(acc_o), lse_blk)
        lse_ref[...] = lse_blk

        @pl.when(i == nq - 1)
        def _():
            gather.finish(wg_refs, scratch)

    return pl.pallas_call(
        body, grid=(nq,),
        out_shape=[jax.ShapeDtypeStruct((T, ATTN_WIDTH), BF16), jax.ShapeDtypeStruct((T, LANES), F32)]
        + gather.out_shapes(),
        in_specs=[_row_spec(tq, 512), _row_spec(tq, LANES),
                  pl.BlockSpec((T, 512), lambda i: (0, 1), pipeline_mode=pl.Buffered(1)),
                  pl.BlockSpec((T, 512), lambda i: (0, 2), pipeline_mode=pl.Buffered(1)),
                  _const_spec((T, LANES))] + [_const_spec(w.shape) for w in w_shards],
        out_specs=[_row_spec(tq, ATTN_WIDTH), _row_spec(tq, LANES)] + [pl.BlockSpec(memory_space=pl.ANY)] * nw,
        scratch_shapes=gather.scratch_shapes(),
        compiler_params=_params(58), name="fwd_attn",
    )(qkv, qaug, qkv, qkv, kaug, *w_shards)


def _fwd_merge(ys, ya, gt, x2, wbs, wba, wo, g2):
    T = x2.shape[0]
    tm = TOKEN_TILE

    def body(ys_ref, ya_ref, gt_ref, x_ref, wbs_ref, wba_ref, wo_ref, g2_ref, a_ref, b_ref, mg_ref, o_ref, h1_ref):
        A = _dot(ys_ref[...], wbs_ref[...])
        B = _dot(ya_ref[...], wba_ref[...])
        mg = (gt_ref[:, :D_MODEL].astype(F32) * A + gt_ref[:, D_MODEL:].astype(F32) * B).astype(BF16)
        o = _dot(mg, wo_ref[...])
        r2, oh = _rms_stats(o)
        a_ref[...] = A.astype(BF16)
        b_ref[...] = B.astype(BF16)
        mg_ref[...] = mg
        o_ref[...] = o.astype(BF16)
        h1_ref[...] = x_ref[...] + oh * g2_ref[...]

    sd = jax.ShapeDtypeStruct((T, D_MODEL), BF16)
    return pl.pallas_call(
        body, grid=(T // tm,),
        out_shape=(sd, sd, sd, sd, jax.ShapeDtypeStruct((T, D_MODEL), F32)),
        in_specs=[_row_spec(tm, 512), _row_spec(tm, 512), _row_spec(tm, 2048), _row_spec(tm, D_MODEL),
                  _const_spec((512, D_MODEL)), _const_spec((512, D_MODEL)), _const_spec((D_MODEL, D_MODEL)),
                  _const_spec((1, D_MODEL))],
        out_specs=[_row_spec(tm, D_MODEL)] * 5,
        compiler_params=_params(40), name="fwd_merge",
    )(ys, ya, gt, x2, wbs, wba, wo, g2)


def _fwd_ffn_loss(h1, tgt, wup, wdn, g3, g4):
    T = h1.shape[0]
    tm = TOKEN_TILE
    nsteps = T // tm

    def body(h1_ref, tg_ref, wup_ref, wdn_ref, g3_ref, g4_ref, xn2_ref, a_ref, ddn_ref, dy_ref, loss_ref,
             dg4_ref, acc_l, acc_g):
        i = pl.program_id(0)

        @pl.when(i == 0)
        def _():
            acc_l[...] = jnp.zeros_like(acc_l)
            acc_g[...] = jnp.zeros_like(acc_g)

        h1v = h1_ref[...]
        r3, h1h = _rms_stats(h1v)
        xn2 = (h1h * g3_ref[...]).astype(BF16)
        xn2_ref[...] = xn2
        dn = jnp.zeros((tm, D_MODEL), F32)
        for j in range(D_FF // 1024):
            cols = slice(1024 * j, 1024 * (j + 1))
            a = _dot(xn2, wup_ref[:, cols])
            a_ref[:, cols] = a.astype(BF16)
            hid = jnp.square(jnp.maximum(a, 0.0)).astype(BF16)
            dn = dn + _dot(hid, wdn_ref[cols, :])
        r4, dnh = _rms_stats(dn)
        g4v = g4_ref[...]
        e = (h1v + dnh * g4v) - tg_ref[...]
        sq = e * e
        s1 = sq[:, 0:LANES]
        for j in range(1, D_MODEL // LANES):
            s1 = s1 + sq[:, LANES * j:LANES * (j + 1)]
        acc_l[...] += _fold8(s1)
        dy = e * (1.0 / D_MODEL)
        dy_ref[...] = dy
        acc_g[...] += _fold8(dy * dnh)
        ddn_ref[...] = _rms_bwd(dy, dnh, r4, g4v).astype(BF16)

        @pl.when(i == nsteps - 1)
        def _():
            loss_ref[...] = acc_l[...] * (0.5 / D_MODEL)
            dg4_ref[...] = jnp.sum(acc_g[...], axis=0, keepdims=True)

    return pl.pallas_call(
        body, grid=(nsteps,),
        out_shape=(jax.ShapeDtypeStruct((T, D_MODEL), BF16), jax.ShapeDtypeStruct((T, D_FF), BF16),
                   jax.ShapeDtypeStruct((T, D_MODEL), BF16), jax.ShapeDtypeStruct((T, D_MODEL), F32),
                   jax.ShapeDtypeStruct((SUBLANES, LANES), F32), jax.ShapeDtypeStruct((1, D_MODEL), F32)),
        in_specs=[_row_spec(tm, D_MODEL), _row_spec(tm, D_MODEL), _const_spec((D_MODEL, D_FF)),
                  _const_spec((D_FF, D_MODEL)), _const_spec((1, D_MODEL)), _const_spec((1, D_MODEL))],
        out_specs=[_row_spec(tm, D_MODEL), _row_spec(tm, D_FF), _row_spec(tm, D_MODEL), _row_spec(tm, D_MODEL),
                   pl.BlockSpec((SUBLANES, LANES), lambda i: (0, 0)), pl.BlockSpec((1, D_MODEL), lambda i: (0, 0))],
        scratch_shapes=[pltpu.VMEM((SUBLANES, LANES), F32), pltpu.VMEM((SUBLANES, D_MODEL), F32)],
        compiler_params=_params(52), name="fwd_ffn_loss",
    )(h1, tgt, wup, wdn, g3, g4)


def _bwd_ffn(ddn, a, dy, h1, wup, wdn, g3):
    T = h1.shape[0]
    tm = TOKEN_TILE
    nsteps = T // tm

    def body(ddn_ref, a_ref, dy_ref, h1_ref, wup_ref, wdn_ref, g3_ref, da_ref, dh1_ref, dg3_ref, acc_g):
        i = pl.program_id(0)

        @pl.when(i == 0)
        def _():
            acc_g[...] = jnp.zeros_like(acc_g)

        ddnv = ddn_ref[...]
        dxn2 = jnp.zeros((tm, D_MODEL), F32)
        for j in range(D_FF // 1024):
            cols = slice(1024 * j, 1024 * (j + 1))
            dhid = _dot_nt(ddnv, wdn_ref[cols, :])
            da = (dhid * (2.0 * jnp.maximum(a_ref[:, cols].astype(F32), 0.0))).astype(BF16)
            da_ref[:, cols] = da
            dxn2 = dxn2 + _dot_nt(da, wup_ref[:, cols])
        r3, h1h = _rms_stats(h1_ref[...])
        acc_g[...] += _fold8(dxn2 * h1h)
        dh1_ref[...] = dy_ref[...] + _rms_bwd(dxn2, h1h, r3, g3_ref[...])

        @pl.when(i == nsteps - 1)
        def _():
            dg3_ref[...] = jnp.sum(acc_g[...], axis=0, keepdims=True)

    return pl.pallas_call(
        body, grid=(nsteps,),
        out_shape=(jax.ShapeDtypeStruct((T, D_FF), BF16), jax.ShapeDtypeStruct((T, D_MODEL), F32),
                   jax.ShapeDtypeStruct((1, D_MODEL), F32)),
        in_specs=[_row_spec(tm, D_MODEL), _row_spec(tm, D_FF), _row_spec(tm, D_MODEL), _row_spec(tm, D_MODEL),
                  _const_spec((D_MODEL, D_FF)), _const_spec((D_FF, D_MODEL)), _const_spec((1, D_MODEL))],
        out_specs=[_row_spec(tm, D_FF), _row_spec(tm, D_MODEL), pl.BlockSpec((1, D_MODEL), lambda i: (0, 0))],
        scratch_shapes=[pltpu.VMEM((SUBLANES, D_MODEL), F32)],
        compiler_params=_params(52), name="bwd_ffn",
    )(ddn, a, dy, h1, wup, wdn, g3)


def _wgrad(xa, dy, name, relu2=False, tn=None, block_cols=None):
    T, K = xa.shape
    N = dy.shape[1]
    tn = N if tn is None else tn
    tt = min(T, WGRAD_TILE if K <= D_MODEL else WGRAD_TILE // 2)
    if block_cols:
        nb = tn // block_cols
        out_shape = jax.ShapeDtypeStruct((N // block_cols, K, block_cols), F32)
        out_spec = pl.BlockSpec((nb, K, block_cols), lambda n, t: (n, 0, 0))
    else:
        out_shape = jax.ShapeDtypeStruct((K, N), F32)
        out_spec = pl.BlockSpec((K, tn), lambda n, t: (0, n))

    def body(x_ref, dy_ref, o_ref):
        @pl.when(pl.program_id(1) == 0)
        def _():
            o_ref[...] = jnp.zeros_like(o_ref)

        xv = x_ref[...]
        if relu2:
            xv = jnp.square(jnp.maximum(xv.astype(F32), 0.0)).astype(BF16)
        if block_cols:
            for b in range(nb):
                o_ref[b] += _dot_tn(xv, dy_ref[:, block_cols * b:block_cols * (b + 1)])
        else:
            o_ref[...] += _dot_tn(xv, dy_ref[...])

    return pl.pallas_call(
        body, grid=(N // tn, T // tt), out_shape=out_shape,
        in_specs=[pl.BlockSpec((tt, K), lambda n, t: (t, 0)), pl.BlockSpec((tt, tn), lambda n, t: (t, n))],
        out_specs=out_spec,
        compiler_params=_params(52, 2), name=name,
    )(xa, dy)


def _bwd_merge(dh1, o, A, B, gt, ys, ya, mg, lse, cc, wbs, wba, wo, g2):
    T = dh1.shape[0]
    tm = MERGE_BWD_TILE
    nsteps = T // tm

    def body(dh1_ref, o_ref, a_ref, b_ref, gt_ref, ys_ref, ya_ref, mg_ref, lse_ref, cc_ref, wbs_ref, wba_ref,
             wo_ref, g2_ref, dgl_ref, dys_ref, dya_ref, qab_ref, dab_ref, dg2_ref, dwbs_ref, dwba_ref, dwo_ref,
             acc_g):
        i = pl.program_id(0)

        @pl.when(i == 0)
        def _():
            acc_g[...] = jnp.zeros_like(acc_g)
            dwbs_ref[...] = jnp.zeros_like(dwbs_ref)
            dwba_ref[...] = jnp.zeros_like(dwba_ref)
            dwo_ref[...] = jnp.zeros_like(dwo_ref)

        dh1v = dh1_ref[...]
        r2, oh = _rms_stats(o_ref[...].astype(F32))
        acc_g[...] += _fold8(dh1v * oh)
        do = _rms_bwd(dh1v, oh, r2, g2_ref[...]).astype(BF16)
        dwo_ref[...] += _dot_tn(mg_ref[...], do)
        dmg = _dot_nt(do, wo_ref[...])
        ga = gt_ref[:, :D_MODEL].astype(F32)
        gb = gt_ref[:, D_MODEL:].astype(F32)
        dgl_ref[:, :D_MODEL] = (dmg * a_ref[...].astype(F32) * ga * (1.0 - ga)).astype(BF16)
        dgl_ref[:, D_MODEL:] = (dmg * b_ref[...].astype(F32) * gb * (1.0 - gb)).astype(BF16)
        dA = (dmg * ga).astype(BF16)
        dB = (dmg * gb).astype(BF16)
        dwbs_ref[...] += _dot_tn(ys_ref[...], dA)
        dwba_ref[...] += _dot_tn(ya_ref[...], dB)
        dys_ref[...] = _dot_nt(dA, wbs_ref[...]).astype(BF16)
        dya = _dot_nt(dB, wba_ref[...]).astype(BF16)
        dya_ref[...] = dya
        prod = dya.astype(F32) * ya_ref[...].astype(F32)
        lane = _iota((tm, LANES), 1)
        low = lane < 64
        blk = jnp.zeros((tm, LANES), F32)
        for p in range(4):
            pp = prod[:, LANES * p:LANES * (p + 1)]
            blk = jnp.where(lane == 2 * p, jnp.sum(jnp.where(low, pp, 0.0), axis=1, keepdims=True), blk)
            blk = jnp.where(lane == 2 * p + 1, jnp.sum(jnp.where(low, 0.0, pp), axis=1, keepdims=True), blk)
        qab_ref[...] = _aug_query(cc_ref[...] - lse_ref[...])
        dab_ref[...] = _spread(_split3(-blk), 0).astype(BF16)

        @pl.when(i == nsteps - 1)
        def _():
            dg2_ref[...] = jnp.sum(acc_g[...], axis=0, keepdims=True)

    sh = jax.ShapeDtypeStruct((T, 512), BF16)
    sa = jax.ShapeDtypeStruct((T, LANES), BF16)
    sw = jax.ShapeDtypeStruct((512, D_MODEL), F32)
    whole = lambda shape: pl.BlockSpec(shape, lambda i: (0, 0))
    return pl.pallas_call(
        body, grid=(nsteps,),
        out_shape=(jax.ShapeDtypeStruct((T, 2048), BF16), sh, sh, sa, sa, jax.ShapeDtypeStruct((1, D_MODEL), F32),
                   sw, sw, jax.ShapeDtypeStruct((D_MODEL, D_MODEL), F32)),
        in_specs=[_row_spec(tm, D_MODEL)] * 4 + [_row_spec(tm, 2048), _row_spec(tm, 512), _row_spec(tm, 512),
                  _row_spec(tm, D_MODEL), _row_spec(tm, LANES), _row_spec(tm, LANES),
                  _const_spec((512, D_MODEL)), _const_spec((512, D_MODEL)),
                  _const_spec((D_MODEL, D_MODEL)), _const_spec((1, D_MODEL))],
        out_specs=[_row_spec(tm, 2048), _row_spec(tm, 512), _row_spec(tm, 512), _row_spec(tm, LANES),
                   _row_spec(tm, LANES), whole((1, D_MODEL)), whole((512, D_MODEL)), whole((512, D_MODEL)),
                   whole((D_MODEL, D_MODEL))],
        scratch_shapes=[pltpu.VMEM((SUBLANES, D_MODEL), F32)],
        compiler_params=_params(56), name="bwd_merge",
    )(dh1, o, A, B, gt, ys, ya, mg, lse, cc, wbs, wba, wo, g2)


def _bwd_sgu(zuv, dys, gs, bs, wsp, bT, grads):
    T = zuv.shape[0]
    tc = SGU_TILE
    nc = tc // CHUNK
    nsteps = T // tc
    ex = _GradExchange([tuple(g.shape[1:]) for g in grads])
    ng = ex.n

    def body(z_ref, dy_ref, gs_ref, bs_ref, w_ref, bT_ref, *rest):
        g_refs, (dz_ref, dw_ref, dbT_ref, dgs_ref, dbs_ref) = rest[:ng], rest[ng:ng + 5]
        land1 = rest[ng + 5:2 * ng + 5]
        acc_w, acc_b, acc_gs, acc_bs, dvln_s = rest[2 * ng + 5:2 * ng + 10]
        ex_sems = rest[2 * ng + 10:]
        i = pl.program_id(0)

        @pl.when(i == 0)
        def _():
            ex.start(1, g_refs, land1, ex_sems)
            acc_w[...] = jnp.zeros_like(acc_w)
            acc_b[...] = jnp.zeros_like(acc_b)
            acc_gs[...] = jnp.zeros_like(acc_gs)
            acc_bs[...] = jnp.zeros_like(acc_bs)

        z = z_ref[...].astype(F32)
        gsv = gs_ref[...]
        u, vhat, rs, vln = _sgu_forward_parts(z, gsv, bs_ref[...])
        vb = vln.astype(BF16)
        dy = dy_ref[...].astype(F32)
        low_w = (_iota((CHUNK, nc * LANES), 1) % LANES) < 64
        for p in range(4):
            we, wo, b2 = _sgu_pair_weights(w_ref, bT_ref[...], p)
            vcat = _chunks_on_lanes(vb, p, nc)
            s = _sgu_mix(we, wo, b2, vcat, nc)
            dyc = _chunks_on_lanes(dy, p, nc)
            ds = dyc * _chunks_on_lanes(u, p, nc)
            dsb = ds.astype(BF16)
            zero = jnp.zeros_like(dsb)
            dse = jnp.where(low_w, dsb, zero)
            dso = jnp.where(low_w, zero, dsb)
            acc_w[2 * p] += _dot_nt(dse, vcat)
            acc_w[2 * p + 1] += _dot_nt(dso, vcat)
            acc_b[p] += ds
            dvl = jnp.where(low_w, _dot_tn(we, dsb), _dot_tn(wo, dsb))
            for c in range(nc):
                rows, cols = slice(c * CHUNK, (c + 1) * CHUNK), slice(LANES * p, LANES * (p + 1))
                dvln_s[rows, cols] = dvl[:, c * LANES:(c + 1) * LANES]
                du = dy[rows, cols] * s[:, c * LANES:(c + 1) * LANES]
                dz_ref[rows, cols] = (du * _gelu_grad(z[rows, cols])).astype(BF16)
        dvln = dvln_s[...]
        acc_gs[...] += _fold8(dvln * vhat)
        acc_bs[...] += _fold8(dvln)
        al = dvln * gsv
        dvv = rs * (al - jnp.mean(al, axis=1, keepdims=True) - vhat * jnp.mean(al * vhat, axis=1, keepdims=True))
        dz_ref[:, SGU_WIDTH:] = (dvv * _gelu_grad(z[:, SGU_WIDTH:])).astype(BF16)

        @pl.when(i == nsteps - 1)
        def _():
            tril = _iota((CHUNK, CHUNK), 0) >= _iota((CHUNK, CHUNK), 1)
            lane = _iota((CHUNK, LANES), 1)
            low = lane < 64
            blk = jnp.zeros((CHUNK, LANES), F32)
            for g in range(8):
                dw_ref[g] = jnp.where(tril, acc_w[g], 0.0)
            for p in range(4):
                t = acc_b[p]
                tot = t[:, 0:LANES]
                for c in range(1, nc):
                    tot = tot + t[:, c * LANES:(c + 1) * LANES]
                blk = jnp.where(lane == 2 * p, jnp.sum(jnp.where(low, tot, 0.0), axis=1, keepdims=True), blk)
                blk = jnp.where(lane == 2 * p + 1, jnp.sum(jnp.where(low, 0.0, tot), axis=1, keepdims=True), blk)
            dbT_ref[...] = blk
            dgs_ref[...] = jnp.sum(acc_gs[...], axis=0, keepdims=True)
            dbs_ref[...] = jnp.sum(acc_bs[...], axis=0, keepdims=True)
            ex.wait(1, g_refs, land1, ex_sems)

    whole = lambda shape: pl.BlockSpec(shape, lambda i: (0,) * len(shape))
    hbm_spec = pl.BlockSpec(memory_space=pl.ANY)
    return pl.pallas_call(
        body, grid=(nsteps,),
        out_shape=[jax.ShapeDtypeStruct((T, 1024), BF16), jax.ShapeDtypeStruct((8, CHUNK, CHUNK), F32),
                   jax.ShapeDtypeStruct((CHUNK, LANES), F32), jax.ShapeDtypeStruct((1, SGU_WIDTH), F32),
                   jax.ShapeDtypeStruct((1, SGU_WIDTH), F32)] + ex.land_shapes(1),
        in_specs=[_row_spec(tc, 1024), _row_spec(tc, SGU_WIDTH), _const_spec((1, SGU_WIDTH)),
                  _const_spec((1, SGU_WIDTH)), _const_spec((8, CHUNK, CHUNK)), _const_spec((CHUNK, 8))]
        + [hbm_spec] * ng,
        out_specs=[_row_spec(tc, 1024), whole((8, CHUNK, CHUNK)), whole((CHUNK, LANES)),
                   whole((1, SGU_WIDTH)), whole((1, SGU_WIDTH))] + [hbm_spec] * ng,
        scratch_shapes=[pltpu.VMEM((8, CHUNK, CHUNK), F32), pltpu.VMEM((4, CHUNK, nc * LANES), F32),
                        pltpu.VMEM((SUBLANES, SGU_WIDTH), F32), pltpu.VMEM((SUBLANES, SGU_WIDTH), F32),
                        pltpu.VMEM((tc, SGU_WIDTH), F32)] + ex.sem_shapes(1),
        compiler_params=_params(48), name="bwd_sgu",
    )(zuv, dys, gs, bs, wsp, bT, *grads)


def _bwd_attn(qkv, dya, qab, dab, kaug, parts):
    T = qkv.shape[0]
    tq = tk = ATTN_TILE
    nq = T // tq
    nk = T // tk
    ex = _GradExchange([tuple(g.shape[1:]) for g in parts])
    nr = ex.n

    def body(q_ref, do_ref, qa_ref, da_ref, k_ref, v_ref, ka_ref, *rest):
        part_refs, (dq_ref, dk_ref, dv_ref, dcx_ref) = rest[:nr], rest[nr:nr + 4]
        land2, dq_acc, ex_sems = rest[nr + 4:2 * nr + 4], rest[2 * nr + 4], rest[2 * nr + 5:]
        p = pl.program_id(0)
        j = pl.program_id(1)

        @pl.when((p == 0) & (j == 0))
        def _():
            ex.start(2, part_refs, land2, ex_sems)

        lane = _iota((tq, LANES), 1)
        low = lane < 64
        row = _iota((2 * tq, tk), 0) % tq
        col = _iota((2 * tq, tk), 1)
        first = 2 * AUG_LANES * p

        @pl.when(j == 0)
        def _():
            dq_acc[...] = jnp.zeros_like(dq_acc)

        @pl.when((j == 0) & (p == 0))
        def _():
            dcx_ref[...] = jnp.zeros_like(dcx_ref)

        ka = ka_ref[...]
        kk = jnp.concatenate([k_ref[...], ka], axis=1)
        vv = jnp.concatenate([v_ref[...], ka], axis=1)

        def q_block(i, carry, masked):
            dk_a, dv_a = carry
            qsl = pl.ds(pl.multiple_of(i * tq, tq), tq)
            qs = _aug_stack(q_ref[qsl, :], qa_ref[qsl, :], p)
            dos = _aug_stack(do_ref[qsl, :], da_ref[qsl, :], p)
            s = _dot_nt(qs, kk)
            if masked:
                s = jnp.where(col <= row, s, NEG)
            pm = jnp.exp(s)
            ds = pm * _dot_nt(dos, vv)
            dsb = ds.astype(BF16)
            dv_a = dv_a + _dot_tn(pm.astype(BF16), dos[:, :LANES])
            dk_a = dk_a + _dot_tn(dsb, qs)
            dqx = _dot(dsb, kk)
            dq_acc[qsl, :] += jnp.where(low, dqx[:tq, :LANES], dqx[tq:, :LANES])
            dcx_ref[qsl, :] += (jnp.where(lane == first, dqx[:tq, LANES:], 0.0)
                                + jnp.where(lane == first + AUG_LANES, dqx[tq:, LANES:], 0.0))
            return dk_a, dv_a

        init = (jnp.zeros((tk, 2 * LANES), F32), jnp.zeros((tk, LANES), F32))
        carry = q_block(j, init, True)
        dk_a, dv_a = lax.fori_loop(j + 1, nq, lambda i, c: q_block(i, c, False), carry)
        dk_ref[...] = dk_a[:, :LANES].astype(BF16)
        dv_ref[...] = dv_a.astype(BF16)
        ksl = pl.ds(pl.multiple_of(j * tk, tk), tk)
        lk = _iota((tk, LANES), 1)
        dcx_ref[ksl, :] += jnp.where((lk == first + 3) | (lk == first + AUG_LANES + 3), dk_a[:, LANES:], 0.0)

        @pl.when(j == nk - 1)
        def _():
            dq_ref[...] = (dq_acc[...] * 0.125).astype(BF16)

        @pl.when((p == 3) & (j == nk - 1))
        def _():
            ex.wait(2, part_refs, land2, ex_sems)

    sh = jax.ShapeDtypeStruct((T, ATTN_WIDTH), BF16)
    full = lambda cb: pl.BlockSpec((T, LANES), lambda p, j: (0, cb + p))
    blk = lambda cb: pl.BlockSpec((tk, LANES), lambda p, j: (j, cb + p))
    hbm_spec = pl.BlockSpec(memory_space=pl.ANY)
    return pl.pallas_call(
        body, grid=(4, nk),
        out_shape=[sh, sh, sh, jax.ShapeDtypeStruct((T, LANES), F32)] + ex.land_shapes(2),
        in_specs=[full(0), full(0), _const_spec((T, LANES)), _const_spec((T, LANES)), blk(4), blk(8),
                  pl.BlockSpec((tk, LANES), lambda p, j: (j, 0))] + [hbm_spec] * nr,
        out_specs=[full(0), blk(0), blk(0), pl.BlockSpec((T, LANES), lambda p, j: (0, 0))] + [hbm_spec] * nr,
        scratch_shapes=[pltpu.VMEM((T, LANES), F32)] + ex.sem_shapes(2),
        compiler_params=_params(58, 2), name="bwd_attn",
    )(qkv, dya, qab, dab, qkv, qkv, kaug, *parts)


def _bwd_cum(dcx, fl, bfp):
    T = fl.shape[0]
    tb = CUM_TILE

    def body(dcx_ref, fl_ref, b_ref, dfl_ref, dbf_ref):
        triu = (_iota((tb, tb), 0) <= _iota((tb, tb), 1)).astype(F32)
        r, c = _iota((LANES, LANES), 0), _iota((LANES, LANES), 1)
        sel = (((r == AUG_LANES * c) & (c < N_HEADS)).astype(F32)
               - ((r == AUG_LANES * c + 3) & (c < N_HEADS)).astype(F32))
        carry = jnp.zeros((1, LANES), F32)
        dbf = jnp.zeros((1, LANES), F32)
        for i in reversed(range(T // tb)):
            colblk = jnp.dot(dcx_ref[i * tb:(i + 1) * tb, :], sel, precision=HIGHEST, preferred_element_type=F32)
            rc = jnp.dot(triu, colblk, precision=HIGHEST, preferred_element_type=F32) + carry
            carry = rc[0:1, :]
            sig = jax.nn.sigmoid(fl_ref[i * tb:(i + 1) * tb, :] + b_ref[...])
            dfl = rc * (1.0 - sig)
            dfl_ref[i * tb:(i + 1) * tb, :] = dfl.astype(BF16)
            dbf = dbf + jnp.sum(dfl, axis=0, keepdims=True)
        dbf_ref[...] = dbf

    return pl.pallas_call(
        body,
        out_shape=(jax.ShapeDtypeStruct((T, LANES), BF16), jax.ShapeDtypeStruct((1, LANES), F32)),
        compiler_params=pltpu.CompilerParams(vmem_limit_bytes=32 * MIB), name="bwd_cum",
    )(dcx, fl, bfp)


def _bwd_in(dz, dqkvf, dgl, dh1, x2, g1, wz, wf, wg, rows, name, prev=None, stage=0, exchanged=None):
    T = x2.shape[0]
    tm = TOKEN_TILE
    first = rows[0] // tm
    nsteps = (rows[1] - rows[0]) // tm
    ex = _GradExchange([tuple(exchanged.shape[1:])]) if stage else None

    def body(dz_ref, dqkvf_ref, dgl_ref, dh1_ref, x_ref, g_ref, wz_ref, wf_ref, wg_ref, *rest):
        rest = list(rest)
        dx_prev, dg1_prev = (rest.pop(0), rest.pop(0)) if prev else (None, None)
        src_ref = rest.pop(0) if stage else None
        dx_ref, dg1_ref = rest.pop(0), rest.pop(0)
        land_ref = rest.pop(0) if stage else None
        acc_g, ex_sems = rest[0], rest[1:]
        i = pl.program_id(0)

        @pl.when(i == 0)
        def _():
            if stage:
                ex.start(stage, [src_ref], [land_ref], ex_sems)
            acc_g[...] = jnp.zeros_like(acc_g)

        dxn = _dot_nt(dz_ref[...], wz_ref[:, 0:1024])
        dxn = dxn + _dot_nt(dqkvf_ref[:, 0:1536], wz_ref[:, 1024:2560])
        dxn = dxn + _dot_nt(dqkvf_ref[:, 1536:1664], wf_ref[...])
        dxn = dxn + _dot_nt(dgl_ref[...], wg_ref[...])
        r1, xh = _rms_stats(x_ref[...])
        acc_g[...] += _fold8(dxn * xh)
        dx_ref[...] = dh1_ref[...] + _rms_bwd(dxn, xh, r1, g_ref[...])

        @pl.when(i == nsteps - 1)
        def _():
            total = jnp.sum(acc_g[...], axis=0, keepdims=True)
            dg1_ref[...] = total + dg1_prev[...] if prev else total
            if stage:
                ex.wait(stage, [src_ref], [land_ref], ex_sems)

    hbm_spec = pl.BlockSpec(memory_space=pl.ANY)
    rows_spec = lambda n: pl.BlockSpec((tm, n), lambda i: (i + first, 0))
    operands = [dz, dqkvf, dgl, dh1, x2, g1, wz, wf, wg]
    in_specs = [rows_spec(1024), rows_spec(1664), rows_spec(2048), rows_spec(D_MODEL), rows_spec(D_MODEL),
                _const_spec((1, D_MODEL)), _const_spec((D_MODEL, ZQKV_WIDTH)), _const_spec((D_MODEL, LANES)),
                _const_spec((D_MODEL, 2048))]
    aliases = {}
    if prev:
        aliases = {len(operands): 0}
        operands += list(prev)
        in_specs += [hbm_spec, _const_spec((1, D_MODEL))]
    if stage:
        operands.append(exchanged)
        in_specs.append(hbm_spec)
    return pl.pallas_call(
        body, grid=(nsteps,),
        out_shape=[jax.ShapeDtypeStruct((T, D_MODEL), F32), jax.ShapeDtypeStruct((1, D_MODEL), F32)]
        + (ex.land_shapes(stage) if stage else []),
        in_specs=in_specs,
        out_specs=[rows_spec(D_MODEL), pl.BlockSpec((1, D_MODEL), lambda i: (0, 0))] + ([hbm_spec] if stage else []),
        scratch_shapes=[pltpu.VMEM((SUBLANES, D_MODEL), F32)] + (ex.sem_shapes(stage) if stage else []),
        input_output_aliases=aliases,
        compiler_params=_params(48), name=name,
    )(*operands)


def _small_kernel_shapes(g_mix_pre, b_forget, g_sgu, b_sgu, w_spatial, b_spatial, g_mix_post, g_ffn_pre, g_ffn_post):
    return dict(g_mix_pre=g_mix_pre, b_forget=jnp.pad(b_forget, ((0, 0), (0, LANES - N_HEADS))), g_sgu=g_sgu,
                b_sgu=b_sgu, w_spatial=w_spatial[0], b_spatial=b_spatial[0], g_mix_post=g_mix_post,
                g_ffn_pre=g_ffn_pre, g_ffn_post=g_ffn_post)


def _small_output_shapes(d):
    out = dict(d)
    out.update(b_forget=d["b_forget"][:, :N_HEADS], w_spatial=d["w_spatial"][None], b_spatial=d["b_spatial"][None])
    return out


def kernel(x, g_mix_pre, w_in, b_forget, g_sgu, b_sgu, w_spatial, b_spatial, w_branch_sgu, w_branch_attn, w_out, g_mix_post, g_ffn_pre, w_up, w_down, g_ffn_post, loss_target, m_g_mix_pre, m_w_in, m_b_forget, m_g_sgu, m_b_sgu, m_w_spatial, m_b_spatial, m_w_branch_sgu, m_w_branch_attn, m_w_out, m_g_mix_post, m_g_ffn_pre, m_w_up, m_w_down, m_g_ffn_post, v_g_mix_pre, v_w_in, v_b_forget, v_g_sgu, v_b_sgu, v_w_spatial, v_b_spatial, v_w_branch_sgu, v_w_branch_attn, v_w_out, v_g_mix_post, v_g_ffn_pre, v_w_up, v_w_down, v_g_ffn_post):
    T = x.shape[1]
    x2 = x.reshape(T, D_MODEL)
    tgt = loss_target.reshape(T, D_MODEL)

    wg_in = _gather_w_in(w_in[0])
    wz, wf, wgt = _assemble_w_in(wg_in)
    bfp = jnp.pad(b_forget, ((0, 0), (0, LANES - N_HEADS)))
    wsp = w_spatial[0]
    bT = b_spatial[0].T

    xn, zuv, qkv, fl, gt = _fwd_in(x2, g_mix_pre, wz, wf, wgt)
    cc, qaug, kaug = _fwd_cum(fl, bfp)
    ys = _fwd_sgu(zuv, g_sgu, b_sgu, wsp, bT)
    ya, lse, wbs, wba, wo, wup, wdn = _fwd_attn(
        qkv, qaug, kaug, (w_branch_sgu[0], w_branch_attn[0], w_out[0], w_up[0], w_down[0]))
    A, B, mg, o, h1 = _fwd_merge(ys, ya, gt, x2, wbs, wba, wo, g_mix_post)
    xn2, a, ddn, dy, loss_part, dg4 = _fwd_ffn_loss(h1, tgt, wup, wdn, g_ffn_pre, g_ffn_post)
    loss = lax.psum(jnp.sum(loss_part), ("x", "y", "c"))

    da, dh1, dg3 = _bwd_ffn(ddn, a, dy, h1, wup, wdn, g_ffn_pre)
    dw_up = _wgrad(xn2, da, "wgrad_up", tn=2048, block_cols=512)
    dw_down = _wgrad(a, ddn, "wgrad_down", relu2=True)
    dgl, dys, dya, qab, dab, dg2, dw_bs, dw_ba, dw_out = _bwd_merge(
        dh1, o, A, B, gt, ys, ya, mg, lse, cc, wbs, wba, wo, g_mix_post)
    col_blocks = lambda g, w: g.reshape(g.shape[0], N_DEV, w).transpose(1, 0, 2)
    row_blocks = lambda g, r: g.reshape(N_DEV, r, g.shape[1])
    early_names = ["w_branch_sgu", "w_branch_attn", "w_out", "w_up", "w_down"]
    early = [col_blocks(dw_bs, 128), col_blocks(dw_ba, 128), row_blocks(dw_out, 128), dw_up, row_blocks(dw_down, 512)]
    owners = _owner_indices()
    dzuv, dwsp, dbT, dgs, dbs, *early_land1 = _bwd_sgu(zuv, dys, g_sgu, b_sgu, wsp, bT, early)
    early_parts = [_chip_partials(g, l1, owners, "chip_partials_" + nm)
                   for g, l1, nm in zip(early, early_land1, early_names)]
    dq, dk, dv, dcx, *early_land2 = _bwd_attn(qkv, dya, qab, dab, kaug, early_parts)
    dfl, dbf = _bwd_cum(dcx, fl, bfp)
    dqkvf = jnp.concatenate([dq, dk, dv, dfl], axis=1)
    dw_z = _wgrad(xn, dzuv, "wgrad_in_z")
    dw_qkvf = _wgrad(xn, dqkvf, "wgrad_in_qkvf")
    dw_g = _wgrad(xn, dgl, "wgrad_in_gate")
    blocks_in = _block_dw_in(dw_z, dw_qkvf, dw_g)
    bwd_in_args = (dzuv, dqkvf, dgl, dh1, x2, g_mix_pre, wz, wf, wgt)
    dx, dg1, land1_in = _bwd_in(*bwd_in_args, (0, T // 4), "bwd_in_a", stage=1, exchanged=blocks_in)
    part_in = _chip_partials(blocks_in, land1_in, owners, "chip_partials_w_in")
    dx, dg1, land2_in = _bwd_in(*bwd_in_args, (T // 4, 3 * T // 4), "bwd_in_b", prev=(dx, dg1), stage=2,
                                exchanged=part_in)
    dx, dg1 = _bwd_in(*bwd_in_args, (3 * T // 4, T), "bwd_in_c", prev=(dx, dg1))

    tot_a, tot_b = _allreduce_small(dict(
        g_mix_pre=dg1, b_forget=dbf, g_sgu=dgs, b_sgu=dbs, w_spatial=dwsp, b_spatial=dbT, g_mix_post=dg2,
        g_ffn_pre=dg3, g_ffn_post=dg4))
    small_w = _small_kernel_shapes(g_mix_pre, b_forget, g_sgu, b_sgu, w_spatial, b_spatial, g_mix_post, g_ffn_pre,
                                   g_ffn_post)
    small_m = _small_kernel_shapes(m_g_mix_pre, m_b_forget, m_g_sgu, m_b_sgu, m_w_spatial, m_b_spatial, m_g_mix_post,
                                   m_g_ffn_pre, m_g_ffn_post)
    small_v = _small_kernel_shapes(v_g_mix_pre, v_b_forget, v_g_sgu, v_b_sgu, v_w_spatial, v_b_spatial, v_g_mix_post,
                                   v_g_ffn_pre, v_g_ffn_post)
    sg, sd, sm, sv = (_small_output_shapes(d) for d in _adamw_small(tot_a, tot_b, small_w, small_m, small_v))

    big = {}
    g_in = _reduced_grad(blocks_in, land1_in, land2_in, owners, "reduced_grad_w_in")[:, :IN_SHARD]
    d_, m_, v_ = _adamw(w_in[0], g_in, m_w_in[0], v_w_in[0], "adamw_w_in")
    big["w_in"] = (g_in[None], d_[None], m_[None], v_[None])
    early_wmv = [(w_branch_sgu, m_w_branch_sgu, v_w_branch_sgu), (w_branch_attn, m_w_branch_attn, v_w_branch_attn),
                 (w_out, m_w_out, v_w_out), (w_up, m_w_up, v_w_up), (w_down, m_w_down, v_w_down)]
    for nm, (w, m, v), g, l1, l2 in zip(early_names, early_wmv, early, early_land1, early_land2):
        big[nm] = tuple(t[None] for t in _adamw_reduced(w[0], m[0], v[0], g, l1, l2, owners, "adamw_" + nm))

    order = ["g_mix_pre", "w_in", "b_forget", "g_sgu", "b_sgu", "w_spatial", "b_spatial", "w_branch_sgu",
             "w_branch_attn", "w_out", "g_mix_post", "g_ffn_pre", "w_up", "w_down", "g_ffn_post"]
    outs = [loss, dx.reshape(1, T, D_MODEL)]
    for kind, small in enumerate((sg, sd, sm, sv)):
        outs += [big[nm][kind] if nm in big else small[nm] for nm in order]
    return tuple(outs)
```

```python
import jax
import jax.numpy as jnp
from jax import lax
from jax.experimental import pallas as pl
from jax.experimental.pallas import tpu as pltpu

F32 = jnp.float32
BF16 = jnp.bfloat16
HIGHEST = lax.Precision.HIGHEST
MESH = pl.DeviceIdType.MESH

D_MODEL = 1024
SGU_WIDTH = 512
ATTN_WIDTH = 512
N_HEADS = 8
CHUNK = 128
D_FF = 4096
IN_WIDTH = 4616
N_DEV = 8
IN_SHARD = IN_WIDTH // N_DEV
IN_SHARD_PAD = 640
ZQKV_WIDTH = 2 * SGU_WIDTH + 3 * ATTN_WIDTH
GATE_OFFSET = ZQKV_WIDTH + N_HEADS
EPS = 1e-6
LANES = 128
SUBLANES = 8
VMEM_BYTES = 64 * 1024 * 1024
MIB = 1024 * 1024

ADAM_LR = 0.001
ADAM_B1 = 0.9
ADAM_B2 = 0.999
ADAM_EPS = 1e-08
ADAM_WD = 0.01
ADAM_STEP = 10

TOKEN_TILE = 256
MERGE_BWD_TILE = 512
ATTN_TILE = 512
CUM_TILE = 256
SGU_TILE = 512
WGRAD_TILE = 1024
NEG = -1e30

NT_DIMS = (((1,), (1,)), ((), ()))
TN_DIMS = (((0,), (0,)), ((), ()))


def _params(vmem_mb, n_grid=1):
    return pltpu.CompilerParams(
        dimension_semantics=("arbitrary",) * n_grid,
        vmem_limit_bytes=min(vmem_mb * MIB, VMEM_BYTES - 6 * MIB),
    )


def _dot(a, b):
    return jnp.dot(a, b, preferred_element_type=F32)


def _dot_nt(a, b):
    return lax.dot_general(a, b, NT_DIMS, preferred_element_type=F32)


def _dot_tn(a, b):
    return lax.dot_general(a, b, TN_DIMS, preferred_element_type=F32)


def _const_spec(shape):
    nd = len(shape)
    return pl.BlockSpec(shape, lambda *_: (0,) * nd, pipeline_mode=pl.Buffered(1))


def _row_spec(tm, n, col=0):
    return pl.BlockSpec((tm, n), lambda i: (i, col))


def _fold8(v):
    return v.reshape(v.shape[0] // SUBLANES, SUBLANES, v.shape[1]).sum(axis=0)


def _pick(v, lane_iota, k):
    return jnp.sum(jnp.where(lane_iota == k, v, 0.0), axis=1, keepdims=True)


def _iota(shape, dim):
    return lax.broadcasted_iota(jnp.int32, shape, dim)


def _gelu(x):
    c = 0.7978845608028654
    return 0.5 * x * (1.0 + jnp.tanh(c * (x + 0.044715 * x * x * x)))


def _gelu_grad(x):
    c = 0.7978845608028654
    t = jnp.tanh(c * (x + 0.044715 * x * x * x))
    return 0.5 * (1.0 + t) + 0.5 * x * (1.0 - t * t) * (c * (1.0 + 3.0 * 0.044715 * x * x))


def _rms_stats(v):
    r = lax.rsqrt(jnp.mean(v * v, axis=1, keepdims=True) + EPS)
    return r, v * r


def _rms_bwd(dout, vhat, r, g):
    a = dout * g
    return r * (a - vhat * jnp.mean(a * vhat, axis=1, keepdims=True))


def _mesh_pos():
    return lax.axis_index("x"), lax.axis_index("y"), lax.axis_index("c")


def _dev_index(px, py, pc):
    return 4 * px + 2 * py + pc


def _other_chips(x, y):
    return [(1 - x, y), (x, 1 - y), (1 - x, 1 - y)]


class _WeightGather:
    def __init__(self, shard_shapes, kinds, stage_shapes=None):
        self.shard_shapes = list(shard_shapes)
        self.kinds = list(kinds)
        self.stage_shapes = list(stage_shapes or shard_shapes)
        self.n = len(self.kinds)

    def out_shapes(self):
        shapes = []
        for (rows, cols), kind in zip(self.stage_shapes, self.kinds):
            full = {"block": (N_DEV, rows, cols), "rows": (N_DEV * rows, cols), "cols": (rows, N_DEV * cols)}[kind]
            shapes.append(jax.ShapeDtypeStruct(full, BF16))
        return shapes

    def scratch_shapes(self):
        return ([pltpu.VMEM(s, BF16) for s in self.stage_shapes]
                + [pltpu.SemaphoreType.DMA((self.n, 7)), pltpu.SemaphoreType.DMA((self.n, 7)),
                   pltpu.SemaphoreType.DMA((self.n,))])

    def _view(self, a, ref, j):
        rows, cols = self.stage_shapes[a]
        if self.kinds[a] == "block":
            return ref.at[j]
        if self.kinds[a] == "rows":
            return ref.at[pl.ds(pl.multiple_of(j * rows, rows), rows), :]
        return ref.at[:, pl.ds(pl.multiple_of(j * cols, cols), cols)]

    def _copy(self, outs, scratch, a, k, block, to, from_stage=False):
        stage, (send_sems, recv_sems, _) = scratch[:self.n], scratch[self.n:]
        dst = self._view(a, outs[a], _dev_index(*block))
        return pltpu.make_async_remote_copy(
            src_ref=stage[a] if from_stage else dst, dst_ref=dst,
            send_sem=send_sems.at[a, k], recv_sem=recv_sems.at[a, k],
            device_id=to, device_id_type=MESH)

    def _local(self, outs, scratch, a, me):
        return pltpu.make_async_copy(scratch[a], self._view(a, outs[a], _dev_index(*me)), scratch[-1].at[a])

    def start(self, ins, outs, scratch):
        x, y, c = _mesh_pos()
        me, sibling = (x, y, c), (x, y, 1 - c)
        for a in range(self.n):
            rows, cols = self.shard_shapes[a]
            if self.stage_shapes[a] != self.shard_shapes[a]:
                scratch[a][...] = jnp.zeros(self.stage_shapes[a], BF16)
            scratch[a][0:rows, 0:cols] = ins[a][...].astype(BF16)
            self._local(outs, scratch, a, me).start()
        for a in range(self.n):
            self._copy(outs, scratch, a, 0, me, sibling, True).start()
            for j, chip in enumerate(_other_chips(x, y)):
                self._copy(outs, scratch, a, 1 + j, me, (*chip, c), True).start()

    def forward(self, outs, scratch):
        x, y, c = _mesh_pos()
        me, sibling = (x, y, c), (x, y, 1 - c)
        for a in range(self.n):
            for j, chip in enumerate(_other_chips(x, y)):
                self._copy(outs, scratch, a, 1 + j, (*chip, c), me).wait_recv()
                self._copy(outs, scratch, a, 4 + j, (*chip, c), sibling).start()

    def finish(self, outs, scratch):
        x, y, c = _mesh_pos()
        me, sibling = (x, y, c), (x, y, 1 - c)
        chips = _other_chips(x, y)
        for a in range(self.n):
            self._copy(outs, scratch, a, 0, sibling, me).wait_recv()
            for j, chip in enumerate(chips):
                self._copy(outs, scratch, a, 4 + j, (*chip, 1 - c), me).wait_recv()
        for a in range(self.n):
            self._copy(outs, scratch, a, 0, me, sibling, True).wait_send()
            for j, chip in enumerate(chips):
                self._copy(outs, scratch, a, 1 + j, me, (*chip, c), True).wait_send()
                self._copy(outs, scratch, a, 4 + j, (*chip, c), sibling).wait_send()
            self._local(outs, scratch, a, me).wait()


def _gather_w_in(w_in_local):
    g = _WeightGather([(D_MODEL, IN_SHARD)], ["block"], [(D_MODEL, IN_SHARD_PAD)])

    def body(w_ref, out_ref, *scratch):
        g.start([w_ref], [out_ref], scratch)
        g.forward([out_ref], scratch)
        g.finish([out_ref], scratch)

    return pl.pallas_call(
        body,
        out_shape=g.out_shapes()[0],
        in_specs=[pl.BlockSpec(memory_space=pltpu.VMEM)],
        out_specs=pl.BlockSpec(memory_space=pl.ANY),
        scratch_shapes=g.scratch_shapes(),
        compiler_params=pltpu.CompilerParams(vmem_limit_bytes=32 * MIB),
        name="gather_w_in",
    )(w_in_local)


class _GradExchange:
    def __init__(self, shapes):
        self.shapes = [tuple(s) for s in shapes]
        self.n = len(self.shapes)

    def land_shapes(self, stage):
        slots, dtype = (4, F32) if stage == 1 else (3, BF16)
        return [jax.ShapeDtypeStruct((slots,) + s, dtype) for s in self.shapes]

    def sem_shapes(self, stage):
        slots = 4 if stage == 1 else 3
        return [pltpu.SemaphoreType.DMA((self.n, slots)), pltpu.SemaphoreType.DMA((self.n, slots))]

    def _copy(self, stage, srcs, lands, sems, a, k):
        x, y, c = _mesh_pos()
        cx, cy = (_other_chips(x, y) + [(x, y)])[k]
        if stage == 1:
            src, to = srcs[a].at[_dev_index(cx, cy, 1 - c)], (x, y, 1 - c)
        else:
            src, to = srcs[a].at[k], (cx, cy, c)
        return pltpu.make_async_remote_copy(
            src_ref=src, dst_ref=lands[a].at[k], send_sem=sems[0].at[a, k], recv_sem=sems[1].at[a, k],
            device_id=to, device_id_type=MESH)

    def start(self, stage, srcs, lands, sems):
        for a in range(self.n):
            for k in range(4 if stage == 1 else 3):
                self._copy(stage, srcs, lands, sems, a, k).start()

    def wait(self, stage, srcs, lands, sems):
        for a in range(self.n):
            for k in range(4 if stage == 1 else 3):
                cp = self._copy(stage, srcs, lands, sems, a, k)
                cp.wait_recv()
                cp.wait_send()


def _owner_indices():
    x, y, c = _mesh_pos()
    return jnp.stack([_dev_index(cx, cy, c) for cx, cy in _other_chips(x, y) + [(x, y)]]).astype(jnp.int32)


def _chip_partials(g, land1, idx, name):
    _, rows, cols = g.shape
    tr = min(rows, 256)

    def body(idx_ref, g_ref, l_ref, o_ref):
        o_ref[...] = (g_ref[...] + l_ref[...]).astype(BF16)

    return pl.pallas_call(
        body,
        grid_spec=pltpu.PrefetchScalarGridSpec(
            num_scalar_prefetch=1, grid=(3, rows // tr),
            in_specs=[pl.BlockSpec((None, tr, cols), lambda k, r, idx: (idx[k], r, 0)),
                      pl.BlockSpec((None, tr, cols), lambda k, r, idx: (k, r, 0))],
            out_specs=pl.BlockSpec((None, tr, cols), lambda k, r, idx: (k, r, 0))),
        out_shape=jax.ShapeDtypeStruct((3, rows, cols), BF16),
        compiler_params=_params(32, 2), name=name,
    )(idx, g, land1)


def _reduced_block(g_ref, l1_ref, a_ref, b_ref, c_ref):
    return ((g_ref[...] + l1_ref[...]) + a_ref[...].astype(F32)) + b_ref[...].astype(F32) + c_ref[...].astype(F32)


def _reduced_specs(tm, cols):
    return [pl.BlockSpec((None, tm, cols), lambda i, idx: (idx[3], i, 0)),
            pl.BlockSpec((None, tm, cols), lambda i, idx: (3, i, 0)),
            pl.BlockSpec((None, tm, cols), lambda i, idx: (0, i, 0)),
            pl.BlockSpec((None, tm, cols), lambda i, idx: (1, i, 0)),
            pl.BlockSpec((None, tm, cols), lambda i, idx: (2, i, 0))]


def _reduced_grad(g, land1, land2, idx, name):
    _, rows, cols = g.shape
    tm = min(rows, 256)

    def body(idx_ref, g_ref, l1_ref, a_ref, b_ref, c_ref, o_ref):
        o_ref[...] = _reduced_block(g_ref, l1_ref, a_ref, b_ref, c_ref)

    return pl.pallas_call(
        body,
        grid_spec=pltpu.PrefetchScalarGridSpec(
            num_scalar_prefetch=1, grid=(rows // tm,), in_specs=_reduced_specs(tm, cols),
            out_specs=pl.BlockSpec((tm, cols), lambda i, idx: (i, 0))),
        out_shape=jax.ShapeDtypeStruct((rows, cols), F32),
        compiler_params=_params(32), name=name,
    )(idx, g, land1, land2, land2, land2)


def _adamw_math(w, g, m, v):
    m = ADAM_B1 * m + (1.0 - ADAM_B1) * g
    v = ADAM_B2 * v + (1.0 - ADAM_B2) * (g * g)
    m_hat = m / (1.0 - ADAM_B1 ** ADAM_STEP)
    v_hat = v / (1.0 - ADAM_B2 ** ADAM_STEP)
    delta = -ADAM_LR * (m_hat / (jnp.sqrt(v_hat) + ADAM_EPS) + ADAM_WD * w)
    return delta, m, v


SMALL_NAMES = ("g_mix_pre", "b_forget", "g_sgu", "b_sgu", "w_spatial", "b_spatial", "g_mix_post", "g_ffn_pre",
               "g_ffn_post")
SMALL_SLOTS = {"g_mix_pre": (0, 1, 0, 1024), "g_mix_post": (1, 1, 0, 1024), "g_ffn_pre": (2, 1, 0, 1024),
               "g_ffn_post": (3, 1, 0, 1024), "g_sgu": (4, 1, 0, 512), "b_sgu": (4, 1, 512, 512),
               "b_forget": (5, 1, 0, 128), "b_spatial": (8, 8, 0, 128)}
SMALL_TILE = (16, 1024)
SPATIAL_TILE = (N_HEADS * CHUNK, CHUNK)


LOSS_SLOT = (8, 8, 128, 128)


def _allreduce_small(grads, loss_part):
    names = list(SMALL_NAMES)

    def body(*refs):
        g = dict(zip(names, refs[:len(names)]))
        loss_ref = refs[len(names)]
        tot_a, tot_b, buf_a, buf_b, sib_a, sib_b, ps_a, ps_b, land_a, land_b, send_sems, recv_sems = refs[len(names) + 1:]
        x, y, c = _mesh_pos()
        buf_a[...] = jnp.zeros(SMALL_TILE, F32)
        r0, nr, c0, nc = LOSS_SLOT
        buf_a[r0:r0 + nr, c0:c0 + nc] = loss_ref[...]
        for name, (r0, nr, c0, nc) in SMALL_SLOTS.items():
            val = g[name][...]
            if name == "b_spatial":
                val = jnp.transpose(val)[0:N_HEADS, :]
            buf_a[r0:r0 + nr, c0:c0 + nc] = val
        buf_b[...] = g["w_spatial"][...].reshape(SPATIAL_TILE)

        def swap(k, src, dst, to):
            return pltpu.make_async_remote_copy(src_ref=src, dst_ref=dst, send_sem=send_sems.at[k],
                                                recv_sem=recv_sems.at[k], device_id=to, device_id_type=MESH)

        first = [swap(0, buf_a, sib_a, (x, y, 1 - c)), swap(1, buf_b, sib_b, (x, y, 1 - c))]
        for cp in first:
            cp.start()
        for cp in first:
            cp.wait_recv()
        ps_a[...] = buf_a[...] + sib_a[...]
        ps_b[...] = buf_b[...] + sib_b[...]
        second = []
        for k, (cx, cy) in enumerate(_other_chips(x, y)):
            second += [swap(2 + 2 * k, ps_a, land_a.at[k], (cx, cy, c)), swap(3 + 2 * k, ps_b, land_b.at[k], (cx, cy, c))]
        for cp in second:
            cp.start()
        for cp in second:
            cp.wait_recv()
        tot_a[...] = (ps_a[...] + land_a[0]) + (land_a[1] + land_a[2])
        tot_b[...] = (ps_b[...] + land_b[0]) + (land_b[1] + land_b[2])
        for cp in first + second:
            cp.wait_send()

    vm = pl.BlockSpec(memory_space=pltpu.VMEM)
    return pl.pallas_call(
        body,
        out_shape=(jax.ShapeDtypeStruct(SMALL_TILE, F32), jax.ShapeDtypeStruct(SPATIAL_TILE, F32)),
        in_specs=[vm] * (len(names) + 1), out_specs=[vm, vm],
        scratch_shapes=[pltpu.VMEM(SMALL_TILE, F32), pltpu.VMEM(SPATIAL_TILE, F32)] * 3
        + [pltpu.VMEM((3,) + SMALL_TILE, F32), pltpu.VMEM((3,) + SPATIAL_TILE, F32),
           pltpu.SemaphoreType.DMA((8,)), pltpu.SemaphoreType.DMA((8,))],
        compiler_params=pltpu.CompilerParams(vmem_limit_bytes=32 * MIB),
        name="allreduce_small",
    )(*[grads[nm] for nm in names], loss_part)


def _adamw_small(tot_a, tot_b, ws, ms, vs):
    names = list(SMALL_NAMES)
    n = len(names)

    def body(a_ref, b_ref, *refs):
        w, m, v = (dict(zip(names, refs[i * n:(i + 1) * n])) for i in range(3))
        outs = [dict(zip(names, refs[(3 + i) * n:(4 + i) * n])) for i in range(4)]
        for name in names:
            if name == "w_spatial":
                g = b_ref[...].reshape(N_HEADS, CHUNK, CHUNK)
            else:
                r0, nr, c0, nc = SMALL_SLOTS[name]
                g = a_ref[r0:r0 + nr, c0:c0 + nc]
            vals = (g,) + _adamw_math(w[name][...], g, m[name][...], v[name][...])
            for out, val in zip(outs, vals):
                out[name][...] = val

    shapes = [jax.ShapeDtypeStruct(ws[nm].shape, F32) for nm in names]
    vm = pl.BlockSpec(memory_space=pltpu.VMEM)
    res = pl.pallas_call(
        body, out_shape=shapes * 4, in_specs=[vm] * (2 + 3 * n), out_specs=[vm] * (4 * n),
        compiler_params=pltpu.CompilerParams(vmem_limit_bytes=32 * MIB), name="adamw_small",
    )(tot_a, tot_b, *[d[nm] for d in (ws, ms, vs) for nm in names])
    return [dict(zip(names, res[i * n:(i + 1) * n])) for i in range(4)]


def _adamw_reduced(w, m, v, g, land1, land2, idx, name):
    rows, cols = w.shape
    tm = min(rows, 256)

    def body(idx_ref, w_ref, m_ref, v_ref, g_ref, l1_ref, a_ref, b_ref, c_ref, go_ref, d_ref, nm_ref, nv_ref):
        gsum = _reduced_block(g_ref, l1_ref, a_ref, b_ref, c_ref)
        go_ref[...] = gsum
        delta, nm, nv = _adamw_math(w_ref[...], gsum, m_ref[...], v_ref[...])
        d_ref[...] = delta
        nm_ref[...] = nm
        nv_ref[...] = nv

    sd = jax.ShapeDtypeStruct((rows, cols), F32)
    spec = pl.BlockSpec((tm, cols), lambda i, idx: (i, 0))
    return pl.pallas_call(
        body,
        grid_spec=pltpu.PrefetchScalarGridSpec(
            num_scalar_prefetch=1, grid=(rows // tm,), in_specs=[spec] * 3 + _reduced_specs(tm, cols),
            out_specs=[spec] * 4),
        out_shape=(sd, sd, sd, sd),
        compiler_params=_params(32), name=name,
    )(idx, w, m, v, g, land1, land2, land2, land2)


def _adamw(w, g, m, v, name):
    rows, cols = w.shape
    tm = 256 if rows % 256 == 0 else rows

    def body(w_ref, g_ref, m_ref, v_ref, d_ref, nm_ref, nv_ref):
        delta, nm, nv = _adamw_math(w_ref[...], g_ref[...], m_ref[...], v_ref[...])
        d_ref[...] = delta
        nm_ref[...] = nm
        nv_ref[...] = nv

    sd = jax.ShapeDtypeStruct((rows, cols), F32)
    spec = _row_spec(tm, cols)
    return pl.pallas_call(
        body, grid=(rows // tm,), out_shape=(sd, sd, sd), in_specs=[spec] * 4, out_specs=[spec] * 3,
        compiler_params=_params(32), name=name,
    )(w, g, m, v)


def _virtual_slab(sources, v0, v_end, like):
    lane = _iota(like.shape, 1)
    out = jnp.zeros(like.shape, like.dtype)
    for v_start, v_stop, read in sources:
        a, b = max(v0, v_start), min(v0 + LANES, v_stop, v_end)
        while a < b:
            c = a - v_start
            n = min(b - a, LANES - c % LANES)
            piece = read(c // LANES)
            shift = (a - v0 - c % LANES) % LANES
            if shift:
                piece = pltpu.roll(piece, shift, 1)
            out = jnp.where((lane >= a - v0) & (lane < a - v0 + n), piece, out)
            a += n
    return out


def _assemble_w_in(wg_in):
    tm = TOKEN_TILE

    def body(src_ref, wz_ref, wf_ref, wg_ref):
        like = src_ref[0, :, 0:LANES]
        sources = [(IN_SHARD * j, IN_SHARD * (j + 1),
                    (lambda k, j=j: src_ref[j, :, LANES * k:LANES * (k + 1)])) for j in range(N_DEV)]
        for k in range(ZQKV_WIDTH // LANES):
            wz_ref[:, LANES * k:LANES * (k + 1)] = _virtual_slab(sources, LANES * k, ZQKV_WIDTH, like)
        wf_ref[...] = _virtual_slab(sources, ZQKV_WIDTH, GATE_OFFSET, like)
        for k in range(2 * D_MODEL // LANES):
            wg_ref[:, LANES * k:LANES * (k + 1)] = _virtual_slab(sources, GATE_OFFSET + LANES * k, IN_WIDTH, like)

    return pl.pallas_call(
        body, grid=(D_MODEL // tm,),
        out_shape=(jax.ShapeDtypeStruct((D_MODEL, ZQKV_WIDTH), BF16), jax.ShapeDtypeStruct((D_MODEL, LANES), BF16),
                   jax.ShapeDtypeStruct((D_MODEL, 2 * D_MODEL), BF16)),
        in_specs=[pl.BlockSpec((N_DEV, tm, IN_SHARD_PAD), lambda i: (0, i, 0))],
        out_specs=[_row_spec(tm, ZQKV_WIDTH), _row_spec(tm, LANES), _row_spec(tm, 2 * D_MODEL)],
        compiler_params=_params(32), name="assemble_w_in",
    )(wg_in)


def _block_dw_in(dw_z, dw_qkvf, dw_g):
    tm = TOKEN_TILE
    n_qkvf = 3 * ATTN_WIDTH + N_HEADS

    def body(z_ref, q_ref, g_ref, out_ref):
        like = z_ref[:, 0:LANES]
        slab = lambda ref: (lambda k: ref[:, LANES * k:LANES * (k + 1)])
        sources = [(0, 2 * SGU_WIDTH, slab(z_ref)), (2 * SGU_WIDTH, 2 * SGU_WIDTH + n_qkvf, slab(q_ref)),
                   (GATE_OFFSET, IN_WIDTH, slab(g_ref))]
        for j in range(N_DEV):
            for k in range(IN_SHARD_PAD // LANES):
                out_ref[j, :, LANES * k:LANES * (k + 1)] = _virtual_slab(
                    sources, IN_SHARD * j + LANES * k, IN_SHARD * (j + 1), like)

    return pl.pallas_call(
        body, grid=(D_MODEL // tm,),
        out_shape=jax.ShapeDtypeStruct((N_DEV, D_MODEL, IN_SHARD_PAD), F32),
        in_specs=[_row_spec(tm, 2 * SGU_WIDTH), _row_spec(tm, dw_qkvf.shape[1]), _row_spec(tm, 2 * D_MODEL)],
        out_specs=pl.BlockSpec((N_DEV, tm, IN_SHARD_PAD), lambda i: (0, i, 0)),
        compiler_params=_params(32), name="block_dw_in",
    )(dw_z, dw_qkvf, dw_g)


def _fwd_in(x2, g1, wz, wf, wg):
    T = x2.shape[0]
    tm = TOKEN_TILE

    def body(x_ref, g_ref, wz_ref, wf_ref, wg_ref, xn_ref, zuv_ref, qkv_ref, fl_ref, gt_ref):
        x = x_ref[...]
        r, xh = _rms_stats(x)
        xn = (xh * g_ref[...]).astype(BF16)
        xn_ref[...] = xn
        zuv_ref[...] = _dot(xn, wz_ref[:, 0:1024]).astype(BF16)
        qkv_ref[:, 0:512] = (_dot(xn, wz_ref[:, 1024:1536]) * 0.125).astype(BF16)
        qkv_ref[:, 512:1536] = _dot(xn, wz_ref[:, 1536:2560]).astype(BF16)
        fl_ref[...] = _dot(xn, wf_ref[...])
        gt_ref[...] = jax.nn.sigmoid(_dot(xn, wg_ref[...])).astype(BF16)

    return pl.pallas_call(
        body, grid=(T // tm,),
        out_shape=(jax.ShapeDtypeStruct((T, D_MODEL), BF16), jax.ShapeDtypeStruct((T, 1024), BF16),
                   jax.ShapeDtypeStruct((T, 1536), BF16), jax.ShapeDtypeStruct((T, LANES), F32),
                   jax.ShapeDtypeStruct((T, 2048), BF16)),
        in_specs=[_row_spec(tm, D_MODEL), _const_spec((1, D_MODEL)), _const_spec((D_MODEL, ZQKV_WIDTH)),
                  _const_spec((D_MODEL, LANES)), _const_spec((D_MODEL, 2048))],
        out_specs=[_row_spec(tm, D_MODEL), _row_spec(tm, 1024), _row_spec(tm, 1536), _row_spec(tm, LANES),
                   _row_spec(tm, 2048)],
        compiler_params=_params(48), name="fwd_in",
    )(x2, g1, wz, wf, wg)


def _log_sigmoid(f):
    return jnp.minimum(f, 0.0) - jnp.log1p(jnp.exp(-jnp.abs(f)))


AUG_LANES = 6


def _split3(v):
    hi = v.astype(BF16)
    r1 = v - hi.astype(F32)
    mid = r1.astype(BF16)
    lo = (r1 - mid.astype(F32)).astype(BF16)
    return hi, mid, lo


def _spread(parts, k0):
    r, c = _iota((LANES, LANES), 0), _iota((LANES, LANES), 1)
    out = None
    for i, part in enumerate(parts):
        e = ((c == AUG_LANES * r + (k0 + i)) & (r < N_HEADS)).astype(BF16)
        term = _dot(part, e)
        out = term if out is None else out + term
    return out


def _aug_query(v):
    ones = (_iota(v.shape, 1) < N_HEADS).astype(BF16)
    return (_spread(_split3(v), 0) + _spread((ones, ones, ones), 3)).astype(BF16)


def _aug_key(v):
    ones = (_iota(v.shape, 1) < N_HEADS).astype(BF16)
    return (_spread((ones, ones, ones), 0) - _spread(_split3(v), 3)).astype(BF16)


def _aug_stack(t2, aug, p):
    lane = _iota(t2.shape, 1)
    low = lane < 64
    zero = jnp.zeros_like(t2)
    first = 2 * AUG_LANES * p
    a_e = jnp.where((lane >= first) & (lane < first + AUG_LANES), aug, zero)
    a_o = jnp.where((lane >= first + AUG_LANES) & (lane < first + 2 * AUG_LANES), aug, zero)
    top = jnp.concatenate([jnp.where(low, t2, zero), a_e], axis=1)
    bot = jnp.concatenate([jnp.where(low, zero, t2), a_o], axis=1)
    return jnp.concatenate([top, bot], axis=0)


def _fwd_cum(fl, bfp):
    T = fl.shape[0]
    tb = CUM_TILE

    def body(fl_ref, b_ref, cc_ref, qa_ref, ka_ref):
        tri = (_iota((tb, tb), 0) >= _iota((tb, tb), 1)).astype(F32)
        carry = jnp.zeros((1, LANES), F32)
        for i in range(T // tb):
            rows = slice(i * tb, (i + 1) * tb)
            lf = _log_sigmoid(fl_ref[rows, :] + b_ref[...])
            cs = jnp.dot(tri, lf, precision=HIGHEST, preferred_element_type=F32) + carry
            cc_ref[rows, :] = cs
            carry = cs[tb - 1:tb, :]
            qa_ref[rows, :] = _aug_query(cs)
            ka_ref[rows, :] = _aug_key(cs)

    return pl.pallas_call(
        body,
        out_shape=(jax.ShapeDtypeStruct((T, LANES), F32), jax.ShapeDtypeStruct((T, LANES), BF16),
                   jax.ShapeDtypeStruct((T, LANES), BF16)),
        compiler_params=pltpu.CompilerParams(vmem_limit_bytes=32 * MIB), name="fwd_cum",
    )(fl, bfp)


def _sgu_forward_parts(z, gs, bs):
    u = _gelu(z[:, :SGU_WIDTH])
    vv = _gelu(z[:, SGU_WIDTH:])
    vc = vv - jnp.mean(vv, axis=1, keepdims=True)
    rs = lax.rsqrt(jnp.mean(vc * vc, axis=1, keepdims=True) + EPS)
    vhat = vc * rs
    return u, vhat, rs, vhat * gs + bs


def _sgu_pair_weights(w_ref, bT, p):
    tril = _iota((CHUNK, CHUNK), 0) >= _iota((CHUNK, CHUNK), 1)
    we = jnp.where(tril, w_ref[2 * p], 0.0).astype(BF16)
    wo = jnp.where(tril, w_ref[2 * p + 1], 0.0).astype(BF16)
    lane8 = _iota(bT.shape, 1)
    low = _iota((CHUNK, LANES), 1) < 64
    b2 = jnp.where(low, _pick(bT, lane8, 2 * p), _pick(bT, lane8, 2 * p + 1))
    return we, wo, b2


def _chunks_on_lanes(v, p, nc):
    return jnp.concatenate([v[c * CHUNK:(c + 1) * CHUNK, LANES * p:LANES * (p + 1)] for c in range(nc)], axis=1)


def _sgu_mix(we, wo, b2, vcat, nc):
    low = (_iota((CHUNK, nc * LANES), 1) % LANES) < 64
    return jnp.where(low, _dot(we, vcat), _dot(wo, vcat)) + jnp.concatenate([b2] * nc, axis=1)


def _fwd_sgu(zuv, gs, bs, wsp, bT):
    T = zuv.shape[0]
    tc = SGU_TILE
    nc = tc // CHUNK

    def body(z_ref, gs_ref, bs_ref, w_ref, bT_ref, y_ref):
        u, _, _, vln = _sgu_forward_parts(z_ref[...].astype(F32), gs_ref[...], bs_ref[...])
        vb = vln.astype(BF16)
        for p in range(4):
            we, wo, b2 = _sgu_pair_weights(w_ref, bT_ref[...], p)
            s = _sgu_mix(we, wo, b2, _chunks_on_lanes(vb, p, nc), nc)
            for c in range(nc):
                rows, cols = slice(c * CHUNK, (c + 1) * CHUNK), slice(LANES * p, LANES * (p + 1))
                y_ref[rows, cols] = (u[rows, cols] * s[:, c * LANES:(c + 1) * LANES]).astype(BF16)

    return pl.pallas_call(
        body, grid=(T // tc,), out_shape=jax.ShapeDtypeStruct((T, SGU_WIDTH), BF16),
        in_specs=[_row_spec(tc, 1024), _const_spec((1, SGU_WIDTH)), _const_spec((1, SGU_WIDTH)),
                  _const_spec((8, CHUNK, CHUNK)), _const_spec((CHUNK, 8))],
        out_specs=_row_spec(tc, SGU_WIDTH),
        compiler_params=_params(40), name="fwd_sgu",
    )(zuv, gs, bs, wsp, bT)


def _fwd_attn(qkv, qaug, kaug, w_shards):
    T = qkv.shape[0]
    tq = tk = ATTN_TILE
    nq = T // tq
    gather = _WeightGather([w.shape for w in w_shards], ["cols", "cols", "rows", "cols", "rows"])
    nw = gather.n

    def body(q_ref, qa_ref, k_ref, v_ref, ka_ref, *rest):
        w_refs, (o_ref, lse_ref), wg_refs, scratch = rest[:nw], rest[nw:nw + 2], rest[nw + 2:2 * nw + 2], rest[2 * nw + 2:]
        i = pl.program_id(0)

        @pl.when(i == 0)
        def _():
            gather.start(w_refs, wg_refs, scratch)

        @pl.when(i == nq // 2)
        def _():
            gather.forward(wg_refs, scratch)

        lane = _iota((tq, LANES), 1)
        low = lane < 64
        lowk = _iota((tk, LANES), 1) < 64
        one = jnp.ones((tk, LANES), BF16)
        row = _iota((2 * tq, tk), 0) % tq
        col = _iota((2 * tq, tk), 1)
        cols = [slice(LANES * p, LANES * (p + 1)) for p in range(4)]
        qa = qa_ref[...]
        qs = [_aug_stack(q_ref[:, cols[p]], qa, p) for p in range(4)]

        def step(j, carry, masked):
            ks = pl.ds(pl.multiple_of(j * tk, tk), tk)
            ka = ka_ref[ks, :]
            new = []
            for p in range(4):
                m, acc_e, acc_o = carry[p]
                v2 = v_ref[ks, cols[p]]
                s = _dot_nt(qs[p], jnp.concatenate([k_ref[ks, cols[p]], ka], axis=1))
                if masked:
                    s = jnp.where(col <= row, s, NEG)
                mn = jnp.maximum(m, jnp.max(s, axis=1, keepdims=True))
                al = jnp.exp(m - mn)
                pm = jnp.exp(s - mn).astype(BF16)
                acc_e = al[:tq] * acc_e + _dot(pm[:tq], jnp.where(lowk, v2, one))
                acc_o = al[tq:] * acc_o + _dot(pm[tq:], jnp.where(lowk, one, v2))
                new.append((mn, acc_e, acc_o))
            return tuple(new)

        init = tuple((jnp.full((2 * tq, 1), NEG, F32), jnp.zeros((tq, LANES), F32), jnp.zeros((tq, LANES), F32))
                     for _ in range(4))
        carry = lax.fori_loop(0, i // 2, lambda t, c: step(2 * t + 1, step(2 * t, c, False), False), init)
        carry = lax.cond(i % 2 == 1, lambda c: step(i - 1, c, False), lambda c: c, carry)
        carry = step(i, carry, True)
        lse_blk = jnp.zeros((tq, LANES), F32)
        for p in range(4):
            m, acc_e, acc_o = carry[p]
            l_e = pltpu.roll(acc_e, 64, 1)
            l_o = pltpu.roll(acc_o, 64, 1)
            o_ref[:, cols[p]] = jnp.where(low, acc_e / l_e, acc_o / l_o).astype(BF16)
            lse_blk = jnp.where(lane == 2 * p, m[:tq] + jnp.log(l_e), lse_blk)
            lse_blk = jnp.where(lane == 2 * p + 1, m[tq:] + jnp.log(acc_o), lse_blk)
        lse_ref[...] = lse_blk

        @pl.when(i == nq - 1)
        def _():
            gather.finish(wg_refs, scratch)

    return pl.pallas_call(
        body, grid=(nq,),
        out_shape=[jax.ShapeDtypeStruct((T, ATTN_WIDTH), BF16), jax.ShapeDtypeStruct((T, LANES), F32)]
        + gather.out_shapes(),
        in_specs=[_row_spec(tq, 512), _row_spec(tq, LANES),
                  pl.BlockSpec((T, 512), lambda i: (0, 1), pipeline_mode=pl.Buffered(1)),
                  pl.BlockSpec((T, 512), lambda i: (0, 2), pipeline_mode=pl.Buffered(1)),
                  _const_spec((T, LANES))] + [_const_spec(w.shape) for w in w_shards],
        out_specs=[_row_spec(tq, ATTN_WIDTH), _row_spec(tq, LANES)] + [pl.BlockSpec(memory_space=pl.ANY)] * nw,
        scratch_shapes=gather.scratch_shapes(),
        compiler_params=_params(58), name="fwd_attn",
    )(qkv, qaug, qkv, qkv, kaug, *w_shards)


def _fwd_merge(ys, ya, gt, x2, wbs, wba, wo, g2):
    T = x2.shape[0]
    tm = TOKEN_TILE

    def body(ys_ref, ya_ref, gt_ref, x_ref, wbs_ref, wba_ref, wo_ref, g2_ref, a_ref, b_ref, mg_ref, o_ref, h1_ref):
        A = _dot(ys_ref[...], wbs_ref[...])
        B = _dot(ya_ref[...], wba_ref[...])
        mg = (gt_ref[:, :D_MODEL].astype(F32) * A + gt_ref[:, D_MODEL:].astype(F32) * B).astype(BF16)
        o = _dot(mg, wo_ref[...])
        r2, oh = _rms_stats(o)
        a_ref[...] = A.astype(BF16)
        b_ref[...] = B.astype(BF16)
        mg_ref[...] = mg
        o_ref[...] = o.astype(BF16)
        h1_ref[...] = x_ref[...] + oh * g2_ref[...]

    sd = jax.ShapeDtypeStruct((T, D_MODEL), BF16)
    return pl.pallas_call(
        body, grid=(T // tm,),
        out_shape=(sd, sd, sd, sd, jax.ShapeDtypeStruct((T, D_MODEL), F32)),
        in_specs=[_row_spec(tm, 512), _row_spec(tm, 512), _row_spec(tm, 2048), _row_spec(tm, D_MODEL),
                  _const_spec((512, D_MODEL)), _const_spec((512, D_MODEL)), _const_spec((D_MODEL, D_MODEL)),
                  _const_spec((1, D_MODEL))],
        out_specs=[_row_spec(tm, D_MODEL)] * 5,
        compiler_params=_params(40), name="fwd_merge",
    )(ys, ya, gt, x2, wbs, wba, wo, g2)


def _fwd_ffn_loss(h1, tgt, wup, wdn, g3, g4):
    T = h1.shape[0]
    tm = TOKEN_TILE
    nsteps = T // tm

    def body(h1_ref, tg_ref, wup_ref, wdn_ref, g3_ref, g4_ref, xn2_ref, a_ref, ddn_ref, dy_ref, loss_ref,
             dg4_ref, acc_l, acc_g):
        i = pl.program_id(0)

        @pl.when(i == 0)
        def _():
            acc_l[...] = jnp.zeros_like(acc_l)
            acc_g[...] = jnp.zeros_like(acc_g)

        h1v = h1_ref[...]
        r3, h1h = _rms_stats(h1v)
        xn2 = (h1h * g3_ref[...]).astype(BF16)
        xn2_ref[...] = xn2
        dn = jnp.zeros((tm, D_MODEL), F32)
        for j in range(D_FF // 1024):
            cols = slice(1024 * j, 1024 * (j + 1))
            a = _dot(xn2, wup_ref[:, cols])
            a_ref[:, cols] = a.astype(BF16)
            hid = jnp.square(jnp.maximum(a, 0.0)).astype(BF16)
            dn = dn + _dot(hid, wdn_ref[cols, :])
        r4, dnh = _rms_stats(dn)
        g4v = g4_ref[...]
        e = (h1v + dnh * g4v) - tg_ref[...]
        sq = e * e
        s1 = sq[:, 0:LANES]
        for j in range(1, D_MODEL // LANES):
            s1 = s1 + sq[:, LANES * j:LANES * (j + 1)]
        acc_l[...] += _fold8(s1)
        dy = e * (1.0 / D_MODEL)
        dy_ref[...] = dy
        acc_g[...] += _fold8(dy * dnh)
        ddn_ref[...] = _rms_bwd(dy, dnh, r4, g4v).astype(BF16)

        @pl.when(i == nsteps - 1)
        def _():
            loss_ref[...] = acc_l[...] * (0.5 / D_MODEL)
            dg4_ref[...] = jnp.sum(acc_g[...], axis=0, keepdims=True)

    return pl.pallas_call(
        body, grid=(nsteps,),
        out_shape=(jax.ShapeDtypeStruct((T, D_MODEL), BF16), jax.ShapeDtypeStruct((T, D_FF), BF16),
                   jax.ShapeDtypeStruct((T, D_MODEL), BF16), jax.ShapeDtypeStruct((T, D_MODEL), F32),
                   jax.ShapeDtypeStruct((SUBLANES, LANES), F32), jax.ShapeDtypeStruct((1, D_MODEL), F32)),
        in_specs=[_row_spec(tm, D_MODEL), _row_spec(tm, D_MODEL), _const_spec((D_MODEL, D_FF)),
                  _const_spec((D_FF, D_MODEL)), _const_spec((1, D_MODEL)), _const_spec((1, D_MODEL))],
        out_specs=[_row_spec(tm, D_MODEL), _row_spec(tm, D_FF), _row_spec(tm, D_MODEL), _row_spec(tm, D_MODEL),
                   pl.BlockSpec((SUBLANES, LANES), lambda i: (0, 0)), pl.BlockSpec((1, D_MODEL), lambda i: (0, 0))],
        scratch_shapes=[pltpu.VMEM((SUBLANES, LANES), F32), pltpu.VMEM((SUBLANES, D_MODEL), F32)],
        compiler_params=_params(52), name="fwd_ffn_loss",
    )(h1, tgt, wup, wdn, g3, g4)


def _bwd_ffn(ddn, a, dy, h1, wup, wdn, g3):
    T = h1.shape[0]
    tm = TOKEN_TILE
    nsteps = T // tm

    def body(ddn_ref, a_ref, dy_ref, h1_ref, wup_ref, wdn_ref, g3_ref, da_ref, dh1_ref, dg3_ref, acc_g):
        i = pl.program_id(0)

        @pl.when(i == 0)
        def _():
            acc_g[...] = jnp.zeros_like(acc_g)

        ddnv = ddn_ref[...]
        dxn2 = jnp.zeros((tm, D_MODEL), F32)
        for j in range(D_FF // 1024):
            cols = slice(1024 * j, 1024 * (j + 1))
            dhid = _dot_nt(ddnv, wdn_ref[cols, :])
            da = (dhid * (2.0 * jnp.maximum(a_ref[:, cols].astype(F32), 0.0))).astype(BF16)
            da_ref[:, cols] = da
            dxn2 = dxn2 + _dot_nt(da, wup_ref[:, cols])
        r3, h1h = _rms_stats(h1_ref[...])
        acc_g[...] += _fold8(dxn2 * h1h)
        dh1_ref[...] = dy_ref[...] + _rms_bwd(dxn2, h1h, r3, g3_ref[...])

        @pl.when(i == nsteps - 1)
        def _():
            dg3_ref[...] = jnp.sum(acc_g[...], axis=0, keepdims=True)

    return pl.pallas_call(
        body, grid=(nsteps,),
        out_shape=(jax.ShapeDtypeStruct((T, D_FF), BF16), jax.ShapeDtypeStruct((T, D_MODEL), F32),
                   jax.ShapeDtypeStruct((1, D_MODEL), F32)),
        in_specs=[_row_spec(tm, D_MODEL), _row_spec(tm, D_FF), _row_spec(tm, D_MODEL), _row_spec(tm, D_MODEL),
                  _const_spec((D_MODEL, D_FF)), _const_spec((D_FF, D_MODEL)), _const_spec((1, D_MODEL))],
        out_specs=[_row_spec(tm, D_FF), _row_spec(tm, D_MODEL), pl.BlockSpec((1, D_MODEL), lambda i: (0, 0))],
        scratch_shapes=[pltpu.VMEM((SUBLANES, D_MODEL), F32)],
        compiler_params=_params(52), name="bwd_ffn",
    )(ddn, a, dy, h1, wup, wdn, g3)


def _wgrad(xa, dy, name, relu2=False, tn=None, block_cols=None):
    T, K = xa.shape
    N = dy.shape[1]
    tn = N if tn is None else tn
    tt = min(T, WGRAD_TILE if K <= D_MODEL else WGRAD_TILE // 2)
    if block_cols:
        nb = tn // block_cols
        out_shape = jax.ShapeDtypeStruct((N // block_cols, K, block_cols), F32)
        out_spec = pl.BlockSpec((nb, K, block_cols), lambda n, t: (n, 0, 0))
    else:
        out_shape = jax.ShapeDtypeStruct((K, N), F32)
        out_spec = pl.BlockSpec((K, tn), lambda n, t: (0, n))

    def body(x_ref, dy_ref, o_ref):
        @pl.when(pl.program_id(1) == 0)
        def _():
            o_ref[...] = jnp.zeros_like(o_ref)

        xv = x_ref[...]
        if relu2:
            xv = jnp.square(jnp.maximum(xv.astype(F32), 0.0)).astype(BF16)
        if block_cols:
            for b in range(nb):
                o_ref[b] += _dot_tn(xv, dy_ref[:, block_cols * b:block_cols * (b + 1)])
        else:
            o_ref[...] += _dot_tn(xv, dy_ref[...])

    return pl.pallas_call(
        body, grid=(N // tn, T // tt), out_shape=out_shape,
        in_specs=[pl.BlockSpec((tt, K), lambda n, t: (t, 0)), pl.BlockSpec((tt, tn), lambda n, t: (t, n))],
        out_specs=out_spec,
        compiler_params=_params(52, 2), name=name,
    )(xa, dy)


def _bwd_merge(dh1, o, A, B, gt, ys, ya, mg, lse, cc, wbs, wba, wo, g2):
    T = dh1.shape[0]
    tm = MERGE_BWD_TILE
    nsteps = T // tm

    def body(dh1_ref, o_ref, a_ref, b_ref, gt_ref, ys_ref, ya_ref, mg_ref, lse_ref, cc_ref, wbs_ref, wba_ref,
             wo_ref, g2_ref, dgl_ref, dys_ref, dya_ref, qab_ref, dab_ref, dg2_ref, dwbs_ref, dwba_ref, dwo_ref,
             acc_g):
        i = pl.program_id(0)

        @pl.when(i == 0)
        def _():
            acc_g[...] = jnp.zeros_like(acc_g)
            dwbs_ref[...] = jnp.zeros_like(dwbs_ref)
            dwba_ref[...] = jnp.zeros_like(dwba_ref)
            dwo_ref[...] = jnp.zeros_like(dwo_ref)

        dh1v = dh1_ref[...]
        r2, oh = _rms_stats(o_ref[...].astype(F32))
        acc_g[...] += _fold8(dh1v * oh)
        do = _rms_bwd(dh1v, oh, r2, g2_ref[...]).astype(BF16)
        dwo_ref[...] += _dot_tn(mg_ref[...], do)
        dmg = _dot_nt(do, wo_ref[...])
        ga = gt_ref[:, :D_MODEL].astype(F32)
        gb = gt_ref[:, D_MODEL:].astype(F32)
        dgl_ref[:, :D_MODEL] = (dmg * a_ref[...].astype(F32) * ga * (1.0 - ga)).astype(BF16)
        dgl_ref[:, D_MODEL:] = (dmg * b_ref[...].astype(F32) * gb * (1.0 - gb)).astype(BF16)
        dA = (dmg * ga).astype(BF16)
        dB = (dmg * gb).astype(BF16)
        dwbs_ref[...] += _dot_tn(ys_ref[...], dA)
        dwba_ref[...] += _dot_tn(ya_ref[...], dB)
        dys_ref[...] = _dot_nt(dA, wbs_ref[...]).astype(BF16)
        dya = _dot_nt(dB, wba_ref[...]).astype(BF16)
        dya_ref[...] = dya
        prod = dya.astype(F32) * ya_ref[...].astype(F32)
        lane = _iota((tm, LANES), 1)
        low = lane < 64
        blk = jnp.zeros((tm, LANES), F32)
        for p in range(4):
            pp = prod[:, LANES * p:LANES * (p + 1)]
            blk = jnp.where(lane == 2 * p, jnp.sum(jnp.where(low, pp, 0.0), axis=1, keepdims=True), blk)
            blk = jnp.where(lane == 2 * p + 1, jnp.sum(jnp.where(low, 0.0, pp), axis=1, keepdims=True), blk)
        qab_ref[...] = _aug_query(cc_ref[...] - lse_ref[...])
        dab_ref[...] = _spread(_split3(-blk), 0).astype(BF16)

        @pl.when(i == nsteps - 1)
        def _():
            dg2_ref[...] = jnp.sum(acc_g[...], axis=0, keepdims=True)

    sh = jax.ShapeDtypeStruct((T, 512), BF16)
    sa = jax.ShapeDtypeStruct((T, LANES), BF16)
    sw = jax.ShapeDtypeStruct((512, D_MODEL), F32)
    whole = lambda shape: pl.BlockSpec(shape, lambda i: (0, 0))
    return pl.pallas_call(
        body, grid=(nsteps,),
        out_shape=(jax.ShapeDtypeStruct((T, 2048), BF16), sh, sh, sa, sa, jax.ShapeDtypeStruct((1, D_MODEL), F32),
                   sw, sw, jax.ShapeDtypeStruct((D_MODEL, D_MODEL), F32)),
        in_specs=[_row_spec(tm, D_MODEL)] * 4 + [_row_spec(tm, 2048), _row_spec(tm, 512), _row_spec(tm, 512),
                  _row_spec(tm, D_MODEL), _row_spec(tm, LANES), _row_spec(tm, LANES),
                  _const_spec((512, D_MODEL)), _const_spec((512, D_MODEL)),
                  _const_spec((D_MODEL, D_MODEL)), _const_spec((1, D_MODEL))],
        out_specs=[_row_spec(tm, 2048), _row_spec(tm, 512), _row_spec(tm, 512), _row_spec(tm, LANES),
                   _row_spec(tm, LANES), whole((1, D_MODEL)), whole((512, D_MODEL)), whole((512, D_MODEL)),
                   whole((D_MODEL, D_MODEL))],
        scratch_shapes=[pltpu.VMEM((SUBLANES, D_MODEL), F32)],
        compiler_params=_params(56), name="bwd_merge",
    )(dh1, o, A, B, gt, ys, ya, mg, lse, cc, wbs, wba, wo, g2)


def _bwd_sgu(zuv, dys, gs, bs, wsp, bT, grads):
    T = zuv.shape[0]
    tc = SGU_TILE
    nc = tc // CHUNK
    nsteps = T // tc
    ex = _GradExchange([tuple(g.shape[1:]) for g in grads])
    ng = ex.n

    def body(z_ref, dy_ref, gs_ref, bs_ref, w_ref, bT_ref, *rest):
        g_refs, (dz_ref, dw_ref, dbT_ref, dgs_ref, dbs_ref) = rest[:ng], rest[ng:ng + 5]
        land1 = rest[ng + 5:2 * ng + 5]
        acc_w, acc_b, acc_gs, acc_bs, dvln_s = rest[2 * ng + 5:2 * ng + 10]
        ex_sems = rest[2 * ng + 10:]
        i = pl.program_id(0)

        @pl.when(i == 0)
        def _():
            ex.start(1, g_refs, land1, ex_sems)
            acc_w[...] = jnp.zeros_like(acc_w)
            acc_b[...] = jnp.zeros_like(acc_b)
            acc_gs[...] = jnp.zeros_like(acc_gs)
            acc_bs[...] = jnp.zeros_like(acc_bs)

        z = z_ref[...].astype(F32)
        gsv = gs_ref[...]
        u, vhat, rs, vln = _sgu_forward_parts(z, gsv, bs_ref[...])
        vb = vln.astype(BF16)
        dy = dy_ref[...].astype(F32)
        low_w = (_iota((CHUNK, nc * LANES), 1) % LANES) < 64
        for p in range(4):
            we, wo, b2 = _sgu_pair_weights(w_ref, bT_ref[...], p)
            vcat = _chunks_on_lanes(vb, p, nc)
            s = _sgu_mix(we, wo, b2, vcat, nc)
            dyc = _chunks_on_lanes(dy, p, nc)
            ds = dyc * _chunks_on_lanes(u, p, nc)
            dsb = ds.astype(BF16)
            zero = jnp.zeros_like(dsb)
            dse = jnp.where(low_w, dsb, zero)
            dso = jnp.where(low_w, zero, dsb)
            acc_w[2 * p] += _dot_nt(dse, vcat)
            acc_w[2 * p + 1] += _dot_nt(dso, vcat)
            acc_b[p] += ds
            dvl = jnp.where(low_w, _dot_tn(we, dsb), _dot_tn(wo, dsb))
            for c in range(nc):
                rows, cols = slice(c * CHUNK, (c + 1) * CHUNK), slice(LANES * p, LANES * (p + 1))
                dvln_s[rows, cols] = dvl[:, c * LANES:(c + 1) * LANES]
                du = dy[rows, cols] * s[:, c * LANES:(c + 1) * LANES]
                dz_ref[rows, cols] = (du * _gelu_grad(z[rows, cols])).astype(BF16)
        dvln = dvln_s[...]
        acc_gs[...] += _fold8(dvln * vhat)
        acc_bs[...] += _fold8(dvln)
        al = dvln * gsv
        dvv = rs * (al - jnp.mean(al, axis=1, keepdims=True) - vhat * jnp.mean(al * vhat, axis=1, keepdims=True))
        dz_ref[:, SGU_WIDTH:] = (dvv * _gelu_grad(z[:, SGU_WIDTH:])).astype(BF16)

        @pl.when(i == nsteps - 1)
        def _():
            tril = _iota((CHUNK, CHUNK), 0) >= _iota((CHUNK, CHUNK), 1)
            lane = _iota((CHUNK, LANES), 1)
            low = lane < 64
            blk = jnp.zeros((CHUNK, LANES), F32)
            for g in range(8):
                dw_ref[g] = jnp.where(tril, acc_w[g], 0.0)
            for p in range(4):
                t = acc_b[p]
                tot = t[:, 0:LANES]
                for c in range(1, nc):
                    tot = tot + t[:, c * LANES:(c + 1) * LANES]
                blk = jnp.where(lane == 2 * p, jnp.sum(jnp.where(low, tot, 0.0), axis=1, keepdims=True), blk)
                blk = jnp.where(lane == 2 * p + 1, jnp.sum(jnp.where(low, 0.0, tot), axis=1, keepdims=True), blk)
            dbT_ref[...] = blk
            dgs_ref[...] = jnp.sum(acc_gs[...], axis=0, keepdims=True)
            dbs_ref[...] = jnp.sum(acc_bs[...], axis=0, keepdims=True)
            ex.wait(1, g_refs, land1, ex_sems)

    whole = lambda shape: pl.BlockSpec(shape, lambda i: (0,) * len(shape))
    hbm_spec = pl.BlockSpec(memory_space=pl.ANY)
    return pl.pallas_call(
        body, grid=(nsteps,),
        out_shape=[jax.ShapeDtypeStruct((T, 1024), BF16), jax.ShapeDtypeStruct((8, CHUNK, CHUNK), F32),
                   jax.ShapeDtypeStruct((CHUNK, LANES), F32), jax.ShapeDtypeStruct((1, SGU_WIDTH), F32),
                   jax.ShapeDtypeStruct((1, SGU_WIDTH), F32)] + ex.land_shapes(1),
        in_specs=[_row_spec(tc, 1024), _row_spec(tc, SGU_WIDTH), _const_spec((1, SGU_WIDTH)),
                  _const_spec((1, SGU_WIDTH)), _const_spec((8, CHUNK, CHUNK)), _const_spec((CHUNK, 8))]
        + [hbm_spec] * ng,
        out_specs=[_row_spec(tc, 1024), whole((8, CHUNK, CHUNK)), whole((CHUNK, LANES)),
                   whole((1, SGU_WIDTH)), whole((1, SGU_WIDTH))] + [hbm_spec] * ng,
        scratch_shapes=[pltpu.VMEM((8, CHUNK, CHUNK), F32), pltpu.VMEM((4, CHUNK, nc * LANES), F32),
                        pltpu.VMEM((SUBLANES, SGU_WIDTH), F32), pltpu.VMEM((SUBLANES, SGU_WIDTH), F32),
                        pltpu.VMEM((tc, SGU_WIDTH), F32)] + ex.sem_shapes(1),
        compiler_params=_params(48), name="bwd_sgu",
    )(zuv, dys, gs, bs, wsp, bT, *grads)


def _bwd_attn(qkv, dya, qab, dab, kaug, parts):
    T = qkv.shape[0]
    tq = tk = ATTN_TILE
    nq = T // tq
    nk = T // tk
    ex = _GradExchange([tuple(g.shape[1:]) for g in parts])
    nr = ex.n

    def body(q_ref, do_ref, qa_ref, da_ref, k_ref, v_ref, ka_ref, *rest):
        part_refs, (dq_ref, dk_ref, dv_ref, dcx_ref) = rest[:nr], rest[nr:nr + 4]
        land2, dq_acc, ex_sems = rest[nr + 4:2 * nr + 4], rest[2 * nr + 4], rest[2 * nr + 5:]
        p = pl.program_id(0)
        j = pl.program_id(1)

        @pl.when((p == 0) & (j == 0))
        def _():
            ex.start(2, part_refs, land2, ex_sems)

        lane = _iota((tq, LANES), 1)
        low = lane < 64
        row = _iota((2 * tq, tk), 0) % tq
        col = _iota((2 * tq, tk), 1)
        first = 2 * AUG_LANES * p

        @pl.when(j == 0)
        def _():
            dq_acc[...] = jnp.zeros_like(dq_acc)

        @pl.when((j == 0) & (p == 0))
        def _():
            dcx_ref[...] = jnp.zeros_like(dcx_ref)

        ka = ka_ref[...]
        kk = jnp.concatenate([k_ref[...], ka], axis=1)
        vv = jnp.concatenate([v_ref[...], ka], axis=1)

        def q_block(i, carry, masked):
            dk_a, dv_a = carry
            qsl = pl.ds(pl.multiple_of(i * tq, tq), tq)
            qs = _aug_stack(q_ref[qsl, :], qa_ref[qsl, :], p)
            dos = _aug_stack(do_ref[qsl, :], da_ref[qsl, :], p)
            s = _dot_nt(qs, kk)
            if masked:
                s = jnp.where(col <= row, s, NEG)
            pm = jnp.exp(s)
            ds = pm * _dot_nt(dos, vv)
            dsb = ds.astype(BF16)
            dv_a = dv_a + _dot_tn(pm.astype(BF16), dos[:, :LANES])
            dk_a = dk_a + _dot_tn(dsb, qs)
            dqx = _dot(dsb, kk)
            dq_acc[qsl, :] += jnp.where(low, dqx[:tq, :LANES], dqx[tq:, :LANES])
            dcx_ref[qsl, :] += (jnp.where(lane == first, dqx[:tq, LANES:], 0.0)
                                + jnp.where(lane == first + AUG_LANES, dqx[tq:, LANES:], 0.0))
            return dk_a, dv_a

        init = (jnp.zeros((tk, 2 * LANES), F32), jnp.zeros((tk, LANES), F32))
        carry = q_block(j, init, True)
        n_rest = nq - 1 - j
        carry = lax.fori_loop(
            0, n_rest // 2, lambda t, c: q_block(j + 2 + 2 * t, q_block(j + 1 + 2 * t, c, False), False), carry)
        dk_a, dv_a = lax.cond(n_rest % 2 == 1, lambda c: q_block(nq - 1, c, False), lambda c: c, carry)
        dk_ref[...] = dk_a[:, :LANES].astype(BF16)
        dv_ref[...] = dv_a.astype(BF16)
        ksl = pl.ds(pl.multiple_of(j * tk, tk), tk)
        lk = _iota((tk, LANES), 1)
        dcx_ref[ksl, :] += jnp.where((lk == first + 3) | (lk == first + AUG_LANES + 3), dk_a[:, LANES:], 0.0)

        @pl.when(j == nk - 1)
        def _():
            dq_ref[...] = (dq_acc[...] * 0.125).astype(BF16)

        @pl.when((p == 3) & (j == nk - 1))
        def _():
            ex.wait(2, part_refs, land2, ex_sems)

    sh = jax.ShapeDtypeStruct((T, ATTN_WIDTH), BF16)
    full = lambda cb: pl.BlockSpec((T, LANES), lambda p, j: (0, cb + p))
    blk = lambda cb: pl.BlockSpec((tk, LANES), lambda p, j: (j, cb + p))
    hbm_spec = pl.BlockSpec(memory_space=pl.ANY)
    return pl.pallas_call(
        body, grid=(4, nk),
        out_shape=[sh, sh, sh, jax.ShapeDtypeStruct((T, LANES), F32)] + ex.land_shapes(2),
        in_specs=[full(0), full(0), _const_spec((T, LANES)), _const_spec((T, LANES)), blk(4), blk(8),
                  pl.BlockSpec((tk, LANES), lambda p, j: (j, 0))] + [hbm_spec] * nr,
        out_specs=[full(0), blk(0), blk(0), pl.BlockSpec((T, LANES), lambda p, j: (0, 0))] + [hbm_spec] * nr,
        scratch_shapes=[pltpu.VMEM((T, LANES), F32)] + ex.sem_shapes(2),
        compiler_params=_params(58, 2), name="bwd_attn",
    )(qkv, dya, qab, dab, qkv, qkv, kaug, *parts)


def _bwd_cum(dcx, fl, bfp):
    T = fl.shape[0]
    tb = CUM_TILE

    def body(dcx_ref, fl_ref, b_ref, dfl_ref, dbf_ref):
        triu = (_iota((tb, tb), 0) <= _iota((tb, tb), 1)).astype(F32)
        r, c = _iota((LANES, LANES), 0), _iota((LANES, LANES), 1)
        sel = (((r == AUG_LANES * c) & (c < N_HEADS)).astype(F32)
               - ((r == AUG_LANES * c + 3) & (c < N_HEADS)).astype(F32))
        carry = jnp.zeros((1, LANES), F32)
        dbf = jnp.zeros((1, LANES), F32)
        for i in reversed(range(T // tb)):
            colblk = jnp.dot(dcx_ref[i * tb:(i + 1) * tb, :], sel, precision=HIGHEST, preferred_element_type=F32)
            rc = jnp.dot(triu, colblk, precision=HIGHEST, preferred_element_type=F32) + carry
            carry = rc[0:1, :]
            sig = jax.nn.sigmoid(fl_ref[i * tb:(i + 1) * tb, :] + b_ref[...])
            dfl = rc * (1.0 - sig)
            dfl_ref[i * tb:(i + 1) * tb, :] = dfl.astype(BF16)
            dbf = dbf + jnp.sum(dfl, axis=0, keepdims=True)
        dbf_ref[...] = dbf

    return pl.pallas_call(
        body,
        out_shape=(jax.ShapeDtypeStruct((T, LANES), BF16), jax.ShapeDtypeStruct((1, LANES), F32)),
        compiler_params=pltpu.CompilerParams(vmem_limit_bytes=32 * MIB), name="bwd_cum",
    )(dcx, fl, bfp)


def _bwd_in(dz, dqkvf, dgl, dh1, x2, g1, wz, wf, wg, rows, name, prev=None, stage=0, exchanged=None):
    T = x2.shape[0]
    tm = TOKEN_TILE
    first = rows[0] // tm
    nsteps = (rows[1] - rows[0]) // tm
    ex = _GradExchange([tuple(exchanged.shape[1:])]) if stage else None

    def body(dz_ref, dqkvf_ref, dgl_ref, dh1_ref, x_ref, g_ref, wz_ref, wf_ref, wg_ref, *rest):
        rest = list(rest)
        dx_prev, dg1_prev = (rest.pop(0), rest.pop(0)) if prev else (None, None)
        src_ref = rest.pop(0) if stage else None
        dx_ref, dg1_ref = rest.pop(0), rest.pop(0)
        land_ref = rest.pop(0) if stage else None
        acc_g, ex_sems = rest[0], rest[1:]
        i = pl.program_id(0)

        @pl.when(i == 0)
        def _():
            if stage:
                ex.start(stage, [src_ref], [land_ref], ex_sems)
            acc_g[...] = jnp.zeros_like(acc_g)

        dxn = _dot_nt(dz_ref[...], wz_ref[:, 0:1024])
        dxn = dxn + _dot_nt(dqkvf_ref[:, 0:1536], wz_ref[:, 1024:2560])
        dxn = dxn + _dot_nt(dqkvf_ref[:, 1536:1664], wf_ref[...])
        dxn = dxn + _dot_nt(dgl_ref[...], wg_ref[...])
        r1, xh = _rms_stats(x_ref[...])
        acc_g[...] += _fold8(dxn * xh)
        dx_ref[...] = dh1_ref[...] + _rms_bwd(dxn, xh, r1, g_ref[...])

        @pl.when(i == nsteps - 1)
        def _():
            total = jnp.sum(acc_g[...], axis=0, keepdims=True)
            dg1_ref[...] = total + dg1_prev[...] if prev else total
            if stage:
                ex.wait(stage, [src_ref], [land_ref], ex_sems)

    hbm_spec = pl.BlockSpec(memory_space=pl.ANY)
    rows_spec = lambda n: pl.BlockSpec((tm, n), lambda i: (i + first, 0))
    operands = [dz, dqkvf, dgl, dh1, x2, g1, wz, wf, wg]
    in_specs = [rows_spec(1024), rows_spec(1664), rows_spec(2048), rows_spec(D_MODEL), rows_spec(D_MODEL),
                _const_spec((1, D_MODEL)), _const_spec((D_MODEL, ZQKV_WIDTH)), _const_spec((D_MODEL, LANES)),
                _const_spec((D_MODEL, 2048))]
    aliases = {}
    if prev:
        aliases = {len(operands): 0}
        operands += list(prev)
        in_specs += [hbm_spec, _const_spec((1, D_MODEL))]
    if stage:
        operands.append(exchanged)
        in_specs.append(hbm_spec)
    return pl.pallas_call(
        body, grid=(nsteps,),
        out_shape=[jax.ShapeDtypeStruct((T, D_MODEL), F32), jax.ShapeDtypeStruct((1, D_MODEL), F32)]
        + (ex.land_shapes(stage) if stage else []),
        in_specs=in_specs,
        out_specs=[rows_spec(D_MODEL), pl.BlockSpec((1, D_MODEL), lambda i: (0, 0))] + ([hbm_spec] if stage else []),
        scratch_shapes=[pltpu.VMEM((SUBLANES, D_MODEL), F32)] + (ex.sem_shapes(stage) if stage else []),
        input_output_aliases=aliases,
        compiler_params=_params(48), name=name,
    )(*operands)


def _small_kernel_shapes(g_mix_pre, b_forget, g_sgu, b_sgu, w_spatial, b_spatial, g_mix_post, g_ffn_pre, g_ffn_post):
    return dict(g_mix_pre=g_mix_pre, b_forget=jnp.pad(b_forget, ((0, 0), (0, LANES - N_HEADS))), g_sgu=g_sgu,
                b_sgu=b_sgu, w_spatial=w_spatial[0], b_spatial=b_spatial[0], g_mix_post=g_mix_post,
                g_ffn_pre=g_ffn_pre, g_ffn_post=g_ffn_post)


def _small_output_shapes(d):
    out = dict(d)
    out.update(b_forget=d["b_forget"][:, :N_HEADS], w_spatial=d["w_spatial"][None], b_spatial=d["b_spatial"][None])
    return out


def kernel(x, g_mix_pre, w_in, b_forget, g_sgu, b_sgu, w_spatial, b_spatial, w_branch_sgu, w_branch_attn, w_out, g_mix_post, g_ffn_pre, w_up, w_down, g_ffn_post, loss_target, m_g_mix_pre, m_w_in, m_b_forget, m_g_sgu, m_b_sgu, m_w_spatial, m_b_spatial, m_w_branch_sgu, m_w_branch_attn, m_w_out, m_g_mix_post, m_g_ffn_pre, m_w_up, m_w_down, m_g_ffn_post, v_g_mix_pre, v_w_in, v_b_forget, v_g_sgu, v_b_sgu, v_w_spatial, v_b_spatial, v_w_branch_sgu, v_w_branch_attn, v_w_out, v_g_mix_post, v_g_ffn_pre, v_w_up, v_w_down, v_g_ffn_post):
    T = x.shape[1]
    x2 = x.reshape(T, D_MODEL)
    tgt = loss_target.reshape(T, D_MODEL)

    wg_in = _gather_w_in(w_in[0])
    wz, wf, wgt = _assemble_w_in(wg_in)
    bfp = jnp.pad(b_forget, ((0, 0), (0, LANES - N_HEADS)))
    wsp = w_spatial[0]
    bT = b_spatial[0].T

    xn, zuv, qkv, fl, gt = _fwd_in(x2, g_mix_pre, wz, wf, wgt)
    cc, qaug, kaug = _fwd_cum(fl, bfp)
    ys = _fwd_sgu(zuv, g_sgu, b_sgu, wsp, bT)
    ya, lse, wbs, wba, wo, wup, wdn = _fwd_attn(
        qkv, qaug, kaug, (w_branch_sgu[0], w_branch_attn[0], w_out[0], w_up[0], w_down[0]))
    A, B, mg, o, h1 = _fwd_merge(ys, ya, gt, x2, wbs, wba, wo, g_mix_post)
    xn2, a, ddn, dy, loss_part, dg4 = _fwd_ffn_loss(h1, tgt, wup, wdn, g_ffn_pre, g_ffn_post)

    da, dh1, dg3 = _bwd_ffn(ddn, a, dy, h1, wup, wdn, g_ffn_pre)
    dw_up = _wgrad(xn2, da, "wgrad_up", tn=2048, block_cols=512)
    dw_down = _wgrad(a, ddn, "wgrad_down", relu2=True)
    dgl, dys, dya, qab, dab, dg2, dw_bs, dw_ba, dw_out = _bwd_merge(
        dh1, o, A, B, gt, ys, ya, mg, lse, cc, wbs, wba, wo, g_mix_post)
    col_blocks = lambda g, w: g.reshape(g.shape[0], N_DEV, w).transpose(1, 0, 2)
    row_blocks = lambda g, r: g.reshape(N_DEV, r, g.shape[1])
    early_names = ["w_branch_sgu", "w_branch_attn", "w_out", "w_up", "w_down"]
    early = [col_blocks(dw_bs, 128), col_blocks(dw_ba, 128), row_blocks(dw_out, 128), dw_up, row_blocks(dw_down, 512)]
    owners = _owner_indices()
    dzuv, dwsp, dbT, dgs, dbs, *early_land1 = _bwd_sgu(zuv, dys, g_sgu, b_sgu, wsp, bT, early)
    early_parts = [_chip_partials(g, l1, owners, "chip_partials_" + nm)
                   for g, l1, nm in zip(early, early_land1, early_names)]
    dq, dk, dv, dcx, *early_land2 = _bwd_attn(qkv, dya, qab, dab, kaug, early_parts)
    dfl, dbf = _bwd_cum(dcx, fl, bfp)
    dqkvf = jnp.concatenate([dq, dk, dv, dfl], axis=1)
    dw_z = _wgrad(xn, dzuv, "wgrad_in_z")
    dw_qkvf = _wgrad(xn, dqkvf, "wgrad_in_qkvf")
    dw_g = _wgrad(xn, dgl, "wgrad_in_gate")
    blocks_in = _block_dw_in(dw_z, dw_qkvf, dw_g)
    bwd_in_args = (dzuv, dqkvf, dgl, dh1, x2, g_mix_pre, wz, wf, wgt)
    dx, dg1, land1_in = _bwd_in(*bwd_in_args, (0, T // 4), "bwd_in_a", stage=1, exchanged=blocks_in)
    part_in = _chip_partials(blocks_in, land1_in, owners, "chip_partials_w_in")
    dx, dg1, land2_in = _bwd_in(*bwd_in_args, (T // 4, 3 * T // 4), "bwd_in_b", prev=(dx, dg1), stage=2,
                                exchanged=part_in)
    dx, dg1 = _bwd_in(*bwd_in_args, (3 * T // 4, T), "bwd_in_c", prev=(dx, dg1))

    tot_a, tot_b = _allreduce_small(dict(
        g_mix_pre=dg1, b_forget=dbf, g_sgu=dgs, b_sgu=dbs, w_spatial=dwsp, b_spatial=dbT, g_mix_post=dg2,
        g_ffn_pre=dg3, g_ffn_post=dg4), loss_part)
    r0, nr, c0, nc = LOSS_SLOT
    loss = jnp.sum(tot_a[r0:r0 + nr, c0:c0 + nc])
    small_w = _small_kernel_shapes(g_mix_pre, b_forget, g_sgu, b_sgu, w_spatial, b_spatial, g_mix_post, g_ffn_pre,
                                   g_ffn_post)
    small_m = _small_kernel_shapes(m_g_mix_pre, m_b_forget, m_g_sgu, m_b_sgu, m_w_spatial, m_b_spatial, m_g_mix_post,
                                   m_g_ffn_pre, m_g_ffn_post)
    small_v = _small_kernel_shapes(v_g_mix_pre, v_b_forget, v_g_sgu, v_b_sgu, v_w_spatial, v_b_spatial, v_g_mix_post,
                                   v_g_ffn_pre, v_g_ffn_post)
    sg, sd, sm, sv = (_small_output_shapes(d) for d in _adamw_small(tot_a, tot_b, small_w, small_m, small_v))

    big = {}
    g_in = _reduced_grad(blocks_in, land1_in, land2_in, owners, "reduced_grad_w_in")[:, :IN_SHARD]
    d_, m_, v_ = _adamw(w_in[0], g_in, m_w_in[0], v_w_in[0], "adamw_w_in")
    big["w_in"] = (g_in[None], d_[None], m_[None], v_[None])
    early_wmv = [(w_branch_sgu, m_w_branch_sgu, v_w_branch_sgu), (w_branch_attn, m_w_branch_attn, v_w_branch_attn),
                 (w_out, m_w_out, v_w_out), (w_up, m_w_up, v_w_up), (w_down, m_w_down, v_w_down)]
    for nm, (w, m, v), g, l1, l2 in zip(early_names, early_wmv, early, early_land1, early_land2):
        big[nm] = tuple(t[None] for t in _adamw_reduced(w[0], m[0], v[0], g, l1, l2, owners, "adamw_" + nm))

    order = ["g_mix_pre", "w_in", "b_forget", "g_sgu", "b_sgu", "w_spatial", "b_spatial", "w_branch_sgu",
             "w_branch_attn", "w_out", "g_mix_post", "g_ffn_pre", "w_up", "w_down", "g_ffn_post"]
    outs = [loss, dx.reshape(1, T, D_MODEL)]
    for kind, small in enumerate((sg, sd, sm, sv)):
        outs += [big[nm][kind] if nm in big else small[nm] for nm in order]
    return tuple(outs)
```

```python
import jax
import jax.numpy as jnp
from jax import lax
from jax.experimental import pallas as pl
from jax.experimental.pallas import tpu as pltpu

F32 = jnp.float32
BF16 = jnp.bfloat16
HIGHEST = lax.Precision.HIGHEST
MESH = pl.DeviceIdType.MESH

D_MODEL = 1024
SGU_WIDTH = 512
ATTN_WIDTH = 512
N_HEADS = 8
CHUNK = 128
D_FF = 4096
IN_WIDTH = 4616
N_DEV = 8
IN_SHARD = IN_WIDTH // N_DEV
IN_SHARD_PAD = 640
ZQKV_WIDTH = 2 * SGU_WIDTH + 3 * ATTN_WIDTH
GATE_OFFSET = ZQKV_WIDTH + N_HEADS
EPS = 1e-6
LANES = 128
SUBLANES = 8
VMEM_BYTES = 64 * 1024 * 1024
MIB = 1024 * 1024

ADAM_LR = 0.001
ADAM_B1 = 0.9
ADAM_B2 = 0.999
ADAM_EPS = 1e-08
ADAM_WD = 0.01
ADAM_STEP = 10

TOKEN_TILE = 256
MERGE_BWD_TILE = 512
ATTN_TILE = 512
CUM_TILE = 256
SGU_TILE = 512
WGRAD_TILE = 1024
NEG = -1e30

NT_DIMS = (((1,), (1,)), ((), ()))
TN_DIMS = (((0,), (0,)), ((), ()))


def _params(vmem_mb, n_grid=1):
    return pltpu.CompilerParams(
        dimension_semantics=("arbitrary",) * n_grid,
        vmem_limit_bytes=min(vmem_mb * MIB, VMEM_BYTES - 6 * MIB),
    )


def _dot(a, b):
    return jnp.dot(a, b, preferred_element_type=F32)


def _dot_nt(a, b):
    return lax.dot_general(a, b, NT_DIMS, preferred_element_type=F32)


def _dot_tn(a, b):
    return lax.dot_general(a, b, TN_DIMS, preferred_element_type=F32)


def _const_spec(shape):
    nd = len(shape)
    return pl.BlockSpec(shape, lambda *_: (0,) * nd, pipeline_mode=pl.Buffered(1))


def _row_spec(tm, n, col=0):
    return pl.BlockSpec((tm, n), lambda i: (i, col))


def _fold8(v):
    return v.reshape(v.shape[0] // SUBLANES, SUBLANES, v.shape[1]).sum(axis=0)


def _pick(v, lane_iota, k):
    return jnp.sum(jnp.where(lane_iota == k, v, 0.0), axis=1, keepdims=True)


def _iota(shape, dim):
    return lax.broadcasted_iota(jnp.int32, shape, dim)


def _gelu(x):
    c = 0.7978845608028654
    return 0.5 * x * (1.0 + jnp.tanh(c * (x + 0.044715 * x * x * x)))


def _gelu_grad(x):
    c = 0.7978845608028654
    t = jnp.tanh(c * (x + 0.044715 * x * x * x))
    return 0.5 * (1.0 + t) + 0.5 * x * (1.0 - t * t) * (c * (1.0 + 3.0 * 0.044715 * x * x))


def _rms_stats(v):
    r = lax.rsqrt(jnp.mean(v * v, axis=1, keepdims=True) + EPS)
    return r, v * r


def _rms_bwd(dout, vhat, r, g):
    a = dout * g
    return r * (a - vhat * jnp.mean(a * vhat, axis=1, keepdims=True))


def _mesh_pos():
    return lax.axis_index("x"), lax.axis_index("y"), lax.axis_index("c")


def _dev_index(px, py, pc):
    return 4 * px + 2 * py + pc


def _other_chips(x, y):
    return [(1 - x, y), (x, 1 - y), (1 - x, 1 - y)]


class _WeightGather:
    def __init__(self, shard_shapes, kinds, stage_shapes=None):
        self.shard_shapes = list(shard_shapes)
        self.kinds = list(kinds)
        self.stage_shapes = list(stage_shapes or shard_shapes)
        self.n = len(self.kinds)

    def out_shapes(self):
        shapes = []
        for (rows, cols), kind in zip(self.stage_shapes, self.kinds):
            full = {"block": (N_DEV, rows, cols), "rows": (N_DEV * rows, cols), "cols": (rows, N_DEV * cols)}[kind]
            shapes.append(jax.ShapeDtypeStruct(full, BF16))
        return shapes

    def scratch_shapes(self):
        return ([pltpu.VMEM(s, BF16) for s in self.stage_shapes]
                + [pltpu.SemaphoreType.DMA((self.n, 7)), pltpu.SemaphoreType.DMA((self.n, 7)),
                   pltpu.SemaphoreType.DMA((self.n,))])

    def _view(self, a, ref, j):
        rows, cols = self.stage_shapes[a]
        if self.kinds[a] == "block":
            return ref.at[j]
        if self.kinds[a] == "rows":
            return ref.at[pl.ds(pl.multiple_of(j * rows, rows), rows), :]
        return ref.at[:, pl.ds(pl.multiple_of(j * cols, cols), cols)]

    def _copy(self, outs, scratch, a, k, block, to, from_stage=False):
        stage, (send_sems, recv_sems, _) = scratch[:self.n], scratch[self.n:]
        dst = self._view(a, outs[a], _dev_index(*block))
        return pltpu.make_async_remote_copy(
            src_ref=stage[a] if from_stage else dst, dst_ref=dst,
            send_sem=send_sems.at[a, k], recv_sem=recv_sems.at[a, k],
            device_id=to, device_id_type=MESH)

    def _local(self, outs, scratch, a, me):
        return pltpu.make_async_copy(scratch[a], self._view(a, outs[a], _dev_index(*me)), scratch[-1].at[a])

    def start(self, ins, outs, scratch):
        x, y, c = _mesh_pos()
        me, sibling = (x, y, c), (x, y, 1 - c)
        for a in range(self.n):
            rows, cols = self.shard_shapes[a]
            if self.stage_shapes[a] != self.shard_shapes[a]:
                scratch[a][...] = jnp.zeros(self.stage_shapes[a], BF16)
            scratch[a][0:rows, 0:cols] = ins[a][...].astype(BF16)
            self._local(outs, scratch, a, me).start()
        for a in range(self.n):
            self._copy(outs, scratch, a, 0, me, sibling, True).start()
            for j, chip in enumerate(_other_chips(x, y)):
                self._copy(outs, scratch, a, 1 + j, me, (*chip, c), True).start()

    def forward(self, outs, scratch):
        x, y, c = _mesh_pos()
        me, sibling = (x, y, c), (x, y, 1 - c)
        for a in range(self.n):
            for j, chip in enumerate(_other_chips(x, y)):
                self._copy(outs, scratch, a, 1 + j, (*chip, c), me).wait_recv()
                self._copy(outs, scratch, a, 4 + j, (*chip, c), sibling).start()

    def finish(self, outs, scratch):
        x, y, c = _mesh_pos()
        me, sibling = (x, y, c), (x, y, 1 - c)
        chips = _other_chips(x, y)
        for a in range(self.n):
            self._copy(outs, scratch, a, 0, sibling, me).wait_recv()
            for j, chip in enumerate(chips):
                self._copy(outs, scratch, a, 4 + j, (*chip, 1 - c), me).wait_recv()
        for a in range(self.n):
            self._copy(outs, scratch, a, 0, me, sibling, True).wait_send()
            for j, chip in enumerate(chips):
                self._copy(outs, scratch, a, 1 + j, me, (*chip, c), True).wait_send()
                self._copy(outs, scratch, a, 4 + j, (*chip, c), sibling).wait_send()
            self._local(outs, scratch, a, me).wait()


def _gather_w_in(w_in_local):
    g = _WeightGather([(D_MODEL, IN_SHARD)], ["block"], [(D_MODEL, IN_SHARD_PAD)])

    def body(w_ref, out_ref, *scratch):
        g.start([w_ref], [out_ref], scratch)
        g.forward([out_ref], scratch)
        g.finish([out_ref], scratch)

    return pl.pallas_call(
        body,
        out_shape=g.out_shapes()[0],
        in_specs=[pl.BlockSpec(memory_space=pltpu.VMEM)],
        out_specs=pl.BlockSpec(memory_space=pl.ANY),
        scratch_shapes=g.scratch_shapes(),
        compiler_params=pltpu.CompilerParams(vmem_limit_bytes=32 * MIB),
        name="gather_w_in",
    )(w_in_local)


class _GradExchange:
    def __init__(self, shapes):
        self.shapes = [tuple(s) for s in shapes]
        self.n = len(self.shapes)

    def land_shapes(self, stage):
        slots, dtype = (4, F32) if stage == 1 else (3, BF16)
        return [jax.ShapeDtypeStruct((slots,) + s, dtype) for s in self.shapes]

    def sem_shapes(self, stage):
        slots = 4 if stage == 1 else 3
        return [pltpu.SemaphoreType.DMA((self.n, slots)), pltpu.SemaphoreType.DMA((self.n, slots))]

    def _copy(self, stage, srcs, lands, sems, a, k):
        x, y, c = _mesh_pos()
        cx, cy = (_other_chips(x, y) + [(x, y)])[k]
        if stage == 1:
            src, to = srcs[a].at[_dev_index(cx, cy, 1 - c)], (x, y, 1 - c)
        else:
            src, to = srcs[a].at[k], (cx, cy, c)
        return pltpu.make_async_remote_copy(
            src_ref=src, dst_ref=lands[a].at[k], send_sem=sems[0].at[a, k], recv_sem=sems[1].at[a, k],
            device_id=to, device_id_type=MESH)

    def start(self, stage, srcs, lands, sems):
        for a in range(self.n):
            for k in range(4 if stage == 1 else 3):
                self._copy(stage, srcs, lands, sems, a, k).start()

    def wait(self, stage, srcs, lands, sems):
        for a in range(self.n):
            for k in range(4 if stage == 1 else 3):
                cp = self._copy(stage, srcs, lands, sems, a, k)
                cp.wait_recv()
                cp.wait_send()


def _owner_indices():
    x, y, c = _mesh_pos()
    return jnp.stack([_dev_index(cx, cy, c) for cx, cy in _other_chips(x, y) + [(x, y)]]).astype(jnp.int32)


def _chip_partials(g, land1, idx, name):
    _, rows, cols = g.shape
    tr = min(rows, 256)

    def body(idx_ref, g_ref, l_ref, o_ref):
        o_ref[...] = (g_ref[...] + l_ref[...]).astype(BF16)

    return pl.pallas_call(
        body,
        grid_spec=pltpu.PrefetchScalarGridSpec(
            num_scalar_prefetch=1, grid=(3, rows // tr),
            in_specs=[pl.BlockSpec((None, tr, cols), lambda k, r, idx: (idx[k], r, 0)),
                      pl.BlockSpec((None, tr, cols), lambda k, r, idx: (k, r, 0))],
            out_specs=pl.BlockSpec((None, tr, cols), lambda k, r, idx: (k, r, 0))),
        out_shape=jax.ShapeDtypeStruct((3, rows, cols), BF16),
        compiler_params=_params(32, 2), name=name,
    )(idx, g, land1)


def _reduced_block(g_ref, l1_ref, a_ref, b_ref, c_ref):
    return ((g_ref[...] + l1_ref[...]) + a_ref[...].astype(F32)) + b_ref[...].astype(F32) + c_ref[...].astype(F32)


def _reduced_specs(tm, cols):
    return [pl.BlockSpec((None, tm, cols), lambda i, idx: (idx[3], i, 0)),
            pl.BlockSpec((None, tm, cols), lambda i, idx: (3, i, 0)),
            pl.BlockSpec((None, tm, cols), lambda i, idx: (0, i, 0)),
            pl.BlockSpec((None, tm, cols), lambda i, idx: (1, i, 0)),
            pl.BlockSpec((None, tm, cols), lambda i, idx: (2, i, 0))]


def _reduced_grad(g, land1, land2, idx, name):
    _, rows, cols = g.shape
    tm = min(rows, 256)

    def body(idx_ref, g_ref, l1_ref, a_ref, b_ref, c_ref, o_ref):
        o_ref[...] = _reduced_block(g_ref, l1_ref, a_ref, b_ref, c_ref)

    return pl.pallas_call(
        body,
        grid_spec=pltpu.PrefetchScalarGridSpec(
            num_scalar_prefetch=1, grid=(rows // tm,), in_specs=_reduced_specs(tm, cols),
            out_specs=pl.BlockSpec((tm, cols), lambda i, idx: (i, 0))),
        out_shape=jax.ShapeDtypeStruct((rows, cols), F32),
        compiler_params=_params(32), name=name,
    )(idx, g, land1, land2, land2, land2)


def _adamw_math(w, g, m, v):
    m = ADAM_B1 * m + (1.0 - ADAM_B1) * g
    v = ADAM_B2 * v + (1.0 - ADAM_B2) * (g * g)
    m_hat = m / (1.0 - ADAM_B1 ** ADAM_STEP)
    v_hat = v / (1.0 - ADAM_B2 ** ADAM_STEP)
    delta = -ADAM_LR * (m_hat / (jnp.sqrt(v_hat) + ADAM_EPS) + ADAM_WD * w)
    return delta, m, v


SMALL_NAMES = ("g_mix_pre", "b_forget", "g_sgu", "b_sgu", "w_spatial", "b_spatial", "g_mix_post", "g_ffn_pre",
               "g_ffn_post")
SMALL_SLOTS = {"g_mix_pre": (0, 1, 0, 1024), "g_mix_post": (1, 1, 0, 1024), "g_ffn_pre": (2, 1, 0, 1024),
               "g_ffn_post": (3, 1, 0, 1024), "g_sgu": (4, 1, 0, 512), "b_sgu": (4, 1, 512, 512),
               "b_forget": (5, 1, 0, 128), "b_spatial": (8, 8, 0, 128)}
SMALL_TILE = (16, 1024)
SPATIAL_TILE = (N_HEADS * CHUNK, CHUNK)


LOSS_SLOT = (8, 8, 128, 128)


def _allreduce_small(grads, loss_part):
    names = list(SMALL_NAMES)

    def body(*refs):
        g = dict(zip(names, refs[:len(names)]))
        loss_ref = refs[len(names)]
        tot_a, tot_b, buf_a, buf_b, sib_a, sib_b, ps_a, ps_b, land_a, land_b, send_sems, recv_sems = refs[len(names) + 1:]
        x, y, c = _mesh_pos()
        buf_a[...] = jnp.zeros(SMALL_TILE, F32)
        r0, nr, c0, nc = LOSS_SLOT
        buf_a[r0:r0 + nr, c0:c0 + nc] = loss_ref[...]
        for name, (r0, nr, c0, nc) in SMALL_SLOTS.items():
            val = g[name][...]
            if name == "b_spatial":
                val = jnp.transpose(val)[0:N_HEADS, :]
            buf_a[r0:r0 + nr, c0:c0 + nc] = val
        buf_b[...] = g["w_spatial"][...].reshape(SPATIAL_TILE)

        def swap(k, src, dst, to):
            return pltpu.make_async_remote_copy(src_ref=src, dst_ref=dst, send_sem=send_sems.at[k],
                                                recv_sem=recv_sems.at[k], device_id=to, device_id_type=MESH)

        first = [swap(0, buf_a, sib_a, (x, y, 1 - c)), swap(1, buf_b, sib_b, (x, y, 1 - c))]
        for cp in first:
            cp.start()
        for cp in first:
            cp.wait_recv()
        ps_a[...] = buf_a[...] + sib_a[...]
        ps_b[...] = buf_b[...] + sib_b[...]
        second = []
        for k, (cx, cy) in enumerate(_other_chips(x, y)):
            second += [swap(2 + 2 * k, ps_a, land_a.at[k], (cx, cy, c)), swap(3 + 2 * k, ps_b, land_b.at[k], (cx, cy, c))]
        for cp in second:
            cp.start()
        for cp in second:
            cp.wait_recv()
        tot_a[...] = (ps_a[...] + land_a[0]) + (land_a[1] + land_a[2])
        tot_b[...] = (ps_b[...] + land_b[0]) + (land_b[1] + land_b[2])
        for cp in first + second:
            cp.wait_send()

    vm = pl.BlockSpec(memory_space=pltpu.VMEM)
    return pl.pallas_call(
        body,
        out_shape=(jax.ShapeDtypeStruct(SMALL_TILE, F32), jax.ShapeDtypeStruct(SPATIAL_TILE, F32)),
        in_specs=[vm] * (len(names) + 1), out_specs=[vm, vm],
        scratch_shapes=[pltpu.VMEM(SMALL_TILE, F32), pltpu.VMEM(SPATIAL_TILE, F32)] * 3
        + [pltpu.VMEM((3,) + SMALL_TILE, F32), pltpu.VMEM((3,) + SPATIAL_TILE, F32),
           pltpu.SemaphoreType.DMA((8,)), pltpu.SemaphoreType.DMA((8,))],
        compiler_params=pltpu.CompilerParams(vmem_limit_bytes=32 * MIB),
        name="allreduce_small",
    )(*[grads[nm] for nm in names], loss_part)


def _adamw_small(tot_a, tot_b, ws, ms, vs):
    names = list(SMALL_NAMES)
    n = len(names)

    def body(a_ref, b_ref, *refs):
        w, m, v = (dict(zip(names, refs[i * n:(i + 1) * n])) for i in range(3))
        outs = [dict(zip(names, refs[(3 + i) * n:(4 + i) * n])) for i in range(4)]
        for name in names:
            if name == "w_spatial":
                g = b_ref[...].reshape(N_HEADS, CHUNK, CHUNK)
            else:
                r0, nr, c0, nc = SMALL_SLOTS[name]
                g = a_ref[r0:r0 + nr, c0:c0 + nc]
            vals = (g,) + _adamw_math(w[name][...], g, m[name][...], v[name][...])
            for out, val in zip(outs, vals):
                out[name][...] = val

    shapes = [jax.ShapeDtypeStruct(ws[nm].shape, F32) for nm in names]
    vm = pl.BlockSpec(memory_space=pltpu.VMEM)
    res = pl.pallas_call(
        body, out_shape=shapes * 4, in_specs=[vm] * (2 + 3 * n), out_specs=[vm] * (4 * n),
        compiler_params=pltpu.CompilerParams(vmem_limit_bytes=32 * MIB), name="adamw_small",
    )(tot_a, tot_b, *[d[nm] for d in (ws, ms, vs) for nm in names])
    return [dict(zip(names, res[i * n:(i + 1) * n])) for i in range(4)]


def _adamw_reduced(w, m, v, g, land1, land2, idx, name):
    rows, cols = w.shape
    tm = min(rows, 256)

    def body(idx_ref, w_ref, m_ref, v_ref, g_ref, l1_ref, a_ref, b_ref, c_ref, go_ref, d_ref, nm_ref, nv_ref):
        gsum = _reduced_block(g_ref, l1_ref, a_ref, b_ref, c_ref)
        go_ref[...] = gsum
        delta, nm, nv = _adamw_math(w_ref[...], gsum, m_ref[...], v_ref[...])
        d_ref[...] = delta
        nm_ref[...] = nm
        nv_ref[...] = nv

    sd = jax.ShapeDtypeStruct((rows, cols), F32)
    spec = pl.BlockSpec((tm, cols), lambda i, idx: (i, 0))
    return pl.pallas_call(
        body,
        grid_spec=pltpu.PrefetchScalarGridSpec(
            num_scalar_prefetch=1, grid=(rows // tm,), in_specs=[spec] * 3 + _reduced_specs(tm, cols),
            out_specs=[spec] * 4),
        out_shape=(sd, sd, sd, sd),
        compiler_params=_params(32), name=name,
    )(idx, w, m, v, g, land1, land2, land2, land2)


def _adamw(w, g, m, v, name):
    rows, cols = w.shape
    tm = 256 if rows % 256 == 0 else rows

    def body(w_ref, g_ref, m_ref, v_ref, d_ref, nm_ref, nv_ref):
        delta, nm, nv = _adamw_math(w_ref[...], g_ref[...], m_ref[...], v_ref[...])
        d_ref[...] = delta
        nm_ref[...] = nm
        nv_ref[...] = nv

    sd = jax.ShapeDtypeStruct((rows, cols), F32)
    spec = _row_spec(tm, cols)
    return pl.pallas_call(
        body, grid=(rows // tm,), out_shape=(sd, sd, sd), in_specs=[spec] * 4, out_specs=[spec] * 3,
        compiler_params=_params(32), name=name,
    )(w, g, m, v)


def _virtual_slab(sources, v0, v_end, like):
    lane = _iota(like.shape, 1)
    out = jnp.zeros(like.shape, like.dtype)
    for v_start, v_stop, read in sources:
        a, b = max(v0, v_start), min(v0 + LANES, v_stop, v_end)
        while a < b:
            c = a - v_start
            n = min(b - a, LANES - c % LANES)
            piece = read(c // LANES)
            shift = (a - v0 - c % LANES) % LANES
            if shift:
                piece = pltpu.roll(piece, shift, 1)
            out = jnp.where((lane >= a - v0) & (lane < a - v0 + n), piece, out)
            a += n
    return out


def _assemble_w_in(wg_in):
    tm = TOKEN_TILE

    def body(src_ref, wz_ref, wf_ref, wg_ref):
        like = src_ref[0, :, 0:LANES]
        sources = [(IN_SHARD * j, IN_SHARD * (j + 1),
                    (lambda k, j=j: src_ref[j, :, LANES * k:LANES * (k + 1)])) for j in range(N_DEV)]
        for k in range(ZQKV_WIDTH // LANES):
            wz_ref[:, LANES * k:LANES * (k + 1)] = _virtual_slab(sources, LANES * k, ZQKV_WIDTH, like)
        wf_ref[...] = _virtual_slab(sources, ZQKV_WIDTH, GATE_OFFSET, like)
        for k in range(2 * D_MODEL // LANES):
            wg_ref[:, LANES * k:LANES * (k + 1)] = _virtual_slab(sources, GATE_OFFSET + LANES * k, IN_WIDTH, like)

    return pl.pallas_call(
        body, grid=(D_MODEL // tm,),
        out_shape=(jax.ShapeDtypeStruct((D_MODEL, ZQKV_WIDTH), BF16), jax.ShapeDtypeStruct((D_MODEL, LANES), BF16),
                   jax.ShapeDtypeStruct((D_MODEL, 2 * D_MODEL), BF16)),
        in_specs=[pl.BlockSpec((N_DEV, tm, IN_SHARD_PAD), lambda i: (0, i, 0))],
        out_specs=[_row_spec(tm, ZQKV_WIDTH), _row_spec(tm, LANES), _row_spec(tm, 2 * D_MODEL)],
        compiler_params=_params(32), name="assemble_w_in",
    )(wg_in)


def _block_dw_in(pieces):
    tm = TOKEN_TILE
    widths = [2 * SGU_WIDTH, ATTN_WIDTH, ATTN_WIDTH, ATTN_WIDTH, N_HEADS, 2 * D_MODEL]
    starts = [sum(widths[:k]) for k in range(len(widths))]

    def body(*refs):
        in_refs, out_ref = refs[:-1], refs[-1]
        like = in_refs[0][:, 0:LANES]
        slab = lambda ref: (lambda k: ref[:, LANES * k:LANES * (k + 1)])
        sources = [(s, s + w, slab(ref)) for s, w, ref in zip(starts, widths, in_refs)]
        for j in range(N_DEV):
            for k in range(IN_SHARD_PAD // LANES):
                out_ref[j, :, LANES * k:LANES * (k + 1)] = _virtual_slab(
                    sources, IN_SHARD * j + LANES * k, IN_SHARD * (j + 1), like)

    return pl.pallas_call(
        body, grid=(D_MODEL // tm,),
        out_shape=jax.ShapeDtypeStruct((N_DEV, D_MODEL, IN_SHARD_PAD), F32),
        in_specs=[_row_spec(tm, pc.shape[1]) for pc in pieces],
        out_specs=pl.BlockSpec((N_DEV, tm, IN_SHARD_PAD), lambda i: (0, i, 0)),
        compiler_params=_params(32), name="block_dw_in",
    )(*pieces)


def _fwd_in(x2, g1, wz, wf, wg):
    T = x2.shape[0]
    tm = TOKEN_TILE

    def body(x_ref, g_ref, wz_ref, wf_ref, wg_ref, xn_ref, zuv_ref, qkv_ref, fl_ref, gt_ref):
        x = x_ref[...]
        r, xh = _rms_stats(x)
        xn = (xh * g_ref[...]).astype(BF16)
        xn_ref[...] = xn
        zuv_ref[...] = _dot(xn, wz_ref[:, 0:1024]).astype(BF16)
        qkv_ref[:, 0:512] = (_dot(xn, wz_ref[:, 1024:1536]) * 0.125).astype(BF16)
        qkv_ref[:, 512:1536] = _dot(xn, wz_ref[:, 1536:2560]).astype(BF16)
        fl_ref[...] = _dot(xn, wf_ref[...])
        gt_ref[...] = jax.nn.sigmoid(_dot(xn, wg_ref[...])).astype(BF16)

    return pl.pallas_call(
        body, grid=(T // tm,),
        out_shape=(jax.ShapeDtypeStruct((T, D_MODEL), BF16), jax.ShapeDtypeStruct((T, 1024), BF16),
                   jax.ShapeDtypeStruct((T, 1536), BF16), jax.ShapeDtypeStruct((T, LANES), F32),
                   jax.ShapeDtypeStruct((T, 2048), BF16)),
        in_specs=[_row_spec(tm, D_MODEL), _const_spec((1, D_MODEL)), _const_spec((D_MODEL, ZQKV_WIDTH)),
                  _const_spec((D_MODEL, LANES)), _const_spec((D_MODEL, 2048))],
        out_specs=[_row_spec(tm, D_MODEL), _row_spec(tm, 1024), _row_spec(tm, 1536), _row_spec(tm, LANES),
                   _row_spec(tm, 2048)],
        compiler_params=_params(48), name="fwd_in",
    )(x2, g1, wz, wf, wg)


def _log_sigmoid(f):
    return jnp.minimum(f, 0.0) - jnp.log1p(jnp.exp(-jnp.abs(f)))


AUG_LANES = 6


def _split3(v):
    hi = v.astype(BF16)
    r1 = v - hi.astype(F32)
    mid = r1.astype(BF16)
    lo = (r1 - mid.astype(F32)).astype(BF16)
    return hi, mid, lo


def _spread(parts, k0):
    r, c = _iota((LANES, LANES), 0), _iota((LANES, LANES), 1)
    out = None
    for i, part in enumerate(parts):
        e = ((c == AUG_LANES * r + (k0 + i)) & (r < N_HEADS)).astype(BF16)
        term = _dot(part, e)
        out = term if out is None else out + term
    return out


def _aug_query(v):
    ones = (_iota(v.shape, 1) < N_HEADS).astype(BF16)
    return (_spread(_split3(v), 0) + _spread((ones, ones, ones), 3)).astype(BF16)


def _aug_key(v):
    ones = (_iota(v.shape, 1) < N_HEADS).astype(BF16)
    return (_spread((ones, ones, ones), 0) - _spread(_split3(v), 3)).astype(BF16)


def _aug_stack(t2, aug, p):
    lane = _iota(t2.shape, 1)
    low = lane < 64
    zero = jnp.zeros_like(t2)
    first = 2 * AUG_LANES * p
    a_e = jnp.where((lane >= first) & (lane < first + AUG_LANES), aug, zero)
    a_o = jnp.where((lane >= first + AUG_LANES) & (lane < first + 2 * AUG_LANES), aug, zero)
    top = jnp.concatenate([jnp.where(low, t2, zero), a_e], axis=1)
    bot = jnp.concatenate([jnp.where(low, zero, t2), a_o], axis=1)
    return jnp.concatenate([top, bot], axis=0)


def _fwd_cum(fl, bfp):
    T = fl.shape[0]
    tb = CUM_TILE

    def body(fl_ref, b_ref, cc_ref, qa_ref, ka_ref):
        tri = (_iota((tb, tb), 0) >= _iota((tb, tb), 1)).astype(F32)
        carry = jnp.zeros((1, LANES), F32)
        for i in range(T // tb):
            rows = slice(i * tb, (i + 1) * tb)
            lf = _log_sigmoid(fl_ref[rows, :] + b_ref[...])
            cs = jnp.dot(tri, lf, precision=HIGHEST, preferred_element_type=F32) + carry
            cc_ref[rows, :] = cs
            carry = cs[tb - 1:tb, :]
            qa_ref[rows, :] = _aug_query(cs)
            ka_ref[rows, :] = _aug_key(cs)

    return pl.pallas_call(
        body,
        out_shape=(jax.ShapeDtypeStruct((T, LANES), F32), jax.ShapeDtypeStruct((T, LANES), BF16),
                   jax.ShapeDtypeStruct((T, LANES), BF16)),
        compiler_params=pltpu.CompilerParams(vmem_limit_bytes=32 * MIB), name="fwd_cum",
    )(fl, bfp)


def _sgu_forward_parts(z, gs, bs):
    u = _gelu(z[:, :SGU_WIDTH])
    vv = _gelu(z[:, SGU_WIDTH:])
    vc = vv - jnp.mean(vv, axis=1, keepdims=True)
    rs = lax.rsqrt(jnp.mean(vc * vc, axis=1, keepdims=True) + EPS)
    vhat = vc * rs
    return u, vhat, rs, vhat * gs + bs


def _sgu_pair_weights(w_ref, bT, p):
    tril = _iota((CHUNK, CHUNK), 0) >= _iota((CHUNK, CHUNK), 1)
    we = jnp.where(tril, w_ref[2 * p], 0.0).astype(BF16)
    wo = jnp.where(tril, w_ref[2 * p + 1], 0.0).astype(BF16)
    lane8 = _iota(bT.shape, 1)
    low = _iota((CHUNK, LANES), 1) < 64
    b2 = jnp.where(low, _pick(bT, lane8, 2 * p), _pick(bT, lane8, 2 * p + 1))
    return we, wo, b2


def _chunks_on_lanes(v, p, nc):
    return jnp.concatenate([v[c * CHUNK:(c + 1) * CHUNK, LANES * p:LANES * (p + 1)] for c in range(nc)], axis=1)


def _sgu_mix(we, wo, b2, vcat, nc):
    low = (_iota((CHUNK, nc * LANES), 1) % LANES) < 64
    return jnp.where(low, _dot(we, vcat), _dot(wo, vcat)) + jnp.concatenate([b2] * nc, axis=1)


def _fwd_sgu(zuv, gs, bs, wsp, bT):
    T = zuv.shape[0]
    tc = SGU_TILE
    nc = tc // CHUNK

    def body(z_ref, gs_ref, bs_ref, w_ref, bT_ref, y_ref):
        u, _, _, vln = _sgu_forward_parts(z_ref[...].astype(F32), gs_ref[...], bs_ref[...])
        vb = vln.astype(BF16)
        for p in range(4):
            we, wo, b2 = _sgu_pair_weights(w_ref, bT_ref[...], p)
            s = _sgu_mix(we, wo, b2, _chunks_on_lanes(vb, p, nc), nc)
            for c in range(nc):
                rows, cols = slice(c * CHUNK, (c + 1) * CHUNK), slice(LANES * p, LANES * (p + 1))
                y_ref[rows, cols] = (u[rows, cols] * s[:, c * LANES:(c + 1) * LANES]).astype(BF16)

    return pl.pallas_call(
        body, grid=(T // tc,), out_shape=jax.ShapeDtypeStruct((T, SGU_WIDTH), BF16),
        in_specs=[_row_spec(tc, 1024), _const_spec((1, SGU_WIDTH)), _const_spec((1, SGU_WIDTH)),
                  _const_spec((8, CHUNK, CHUNK)), _const_spec((CHUNK, 8))],
        out_specs=_row_spec(tc, SGU_WIDTH),
        compiler_params=_params(40), name="fwd_sgu",
    )(zuv, gs, bs, wsp, bT)


def _fwd_attn(qkv, qaug, kaug, w_shards):
    T = qkv.shape[0]
    tq = tk = ATTN_TILE
    nq = T // tq
    gather = _WeightGather([w.shape for w in w_shards], ["cols", "cols", "rows", "cols", "rows"])
    nw = gather.n

    def body(q_ref, qa_ref, k_ref, v_ref, ka_ref, *rest):
        w_refs, (o_ref, lse_ref), wg_refs, scratch = rest[:nw], rest[nw:nw + 2], rest[nw + 2:2 * nw + 2], rest[2 * nw + 2:]
        i = pl.program_id(0)

        @pl.when(i == 0)
        def _():
            gather.start(w_refs, wg_refs, scratch)

        @pl.when(i == nq // 2)
        def _():
            gather.forward(wg_refs, scratch)

        lane = _iota((tq, LANES), 1)
        low = lane < 64
        lowk = _iota((tk, LANES), 1) < 64
        one = jnp.ones((tk, LANES), BF16)
        row = _iota((2 * tq, tk), 0) % tq
        col = _iota((2 * tq, tk), 1)
        cols = [slice(LANES * p, LANES * (p + 1)) for p in range(4)]
        qa = qa_ref[...]
        qs = [_aug_stack(q_ref[:, cols[p]], qa, p) for p in range(4)]

        half = tq // 2

        def tile(j, carry, r0, c0, nc, masked):
            nr = tq - r0
            ks = pl.ds(pl.multiple_of(j * tk + c0, half), nc)
            ka = ka_ref[ks, :]
            sub = lambda t: t if r0 == 0 else jnp.concatenate([t[r0:tq], t[tq + r0:]], axis=0)
            new = []
            for p in range(4):
                m, acc_e, acc_o = carry[p]
                v2 = v_ref[ks, cols[p]]
                s = _dot_nt(sub(qs[p]), jnp.concatenate([k_ref[ks, cols[p]], ka], axis=1))
                if masked:
                    s = jnp.where(col[:2 * nr, :nc] + c0 <= row[:2 * nr, :nc] % nr + r0, s, NEG)
                m_old = sub(m)
                mn = jnp.maximum(m_old, jnp.max(s, axis=1, keepdims=True))
                al = jnp.exp(m_old - mn)
                pm = jnp.exp(s - mn).astype(BF16)
                a_e = al[:nr] * acc_e[r0:] + _dot(pm[:nr], jnp.where(lowk[:nc], v2, one[:nc]))
                a_o = al[nr:] * acc_o[r0:] + _dot(pm[nr:], jnp.where(lowk[:nc], one[:nc], v2))
                if r0:
                    mn = jnp.concatenate([m[:r0], mn[:nr], m[tq:tq + r0], mn[nr:]], axis=0)
                    a_e = jnp.concatenate([acc_e[:r0], a_e], axis=0)
                    a_o = jnp.concatenate([acc_o[:r0], a_o], axis=0)
                new.append((mn, a_e, a_o))
            return tuple(new)

        def step(j, carry):
            return tile(j, carry, 0, 0, tk, False)

        init = tuple((jnp.full((2 * tq, 1), NEG, F32), jnp.zeros((tq, LANES), F32), jnp.zeros((tq, LANES), F32))
                     for _ in range(4))
        carry = lax.fori_loop(0, i // 2, lambda t, c: step(2 * t + 1, step(2 * t, c)), init)
        carry = lax.cond(i % 2 == 1, lambda c: step(i - 1, c), lambda c: c, carry)
        carry = tile(i, carry, 0, 0, half, True)
        carry = tile(i, carry, half, half, half, True)
        lse_blk = jnp.zeros((tq, LANES), F32)
        for p in range(4):
            m, acc_e, acc_o = carry[p]
            l_e = pltpu.roll(acc_e, 64, 1)
            l_o = pltpu.roll(acc_o, 64, 1)
            o_ref[:, cols[p]] = jnp.where(low, acc_e / l_e, acc_o / l_o).astype(BF16)
            lse_blk = jnp.where(lane == 2 * p, m[:tq] + jnp.log(l_e), lse_blk)
            lse_blk = jnp.where(lane == 2 * p + 1, m[tq:] + jnp.log(acc_o), lse_blk)
        lse_ref[...] = lse_blk

        @pl.when(i == nq - 1)
        def _():
            gather.finish(wg_refs, scratch)

    return pl.pallas_call(
        body, grid=(nq,),
        out_shape=[jax.ShapeDtypeStruct((T, ATTN_WIDTH), BF16), jax.ShapeDtypeStruct((T, LANES), F32)]
        + gather.out_shapes(),
        in_specs=[_row_spec(tq, 512), _row_spec(tq, LANES),
                  pl.BlockSpec((T, 512), lambda i: (0, 1), pipeline_mode=pl.Buffered(1)),
                  pl.BlockSpec((T, 512), lambda i: (0, 2), pipeline_mode=pl.Buffered(1)),
                  _const_spec((T, LANES))] + [_const_spec(w.shape) for w in w_shards],
        out_specs=[_row_spec(tq, ATTN_WIDTH), _row_spec(tq, LANES)] + [pl.BlockSpec(memory_space=pl.ANY)] * nw,
        scratch_shapes=gather.scratch_shapes(),
        compiler_params=_params(58), name="fwd_attn",
    )(qkv, qaug, qkv, qkv, kaug, *w_shards)


def _fwd_merge(ys, ya, gt, x2, wbs, wba, wo, g2):
    T = x2.shape[0]
    tm = TOKEN_TILE

    def body(ys_ref, ya_ref, gt_ref, x_ref, wbs_ref, wba_ref, wo_ref, g2_ref, a_ref, b_ref, mg_ref, o_ref, h1_ref):
        A = _dot(ys_ref[...], wbs_ref[...])
        B = _dot(ya_ref[...], wba_ref[...])
        mg = (gt_ref[:, :D_MODEL].astype(F32) * A + gt_ref[:, D_MODEL:].astype(F32) * B).astype(BF16)
        o = _dot(mg, wo_ref[...])
        r2, oh = _rms_stats(o)
        a_ref[...] = A.astype(BF16)
        b_ref[...] = B.astype(BF16)
        mg_ref[...] = mg
        o_ref[...] = o.astype(BF16)
        h1_ref[...] = x_ref[...] + oh * g2_ref[...]

    sd = jax.ShapeDtypeStruct((T, D_MODEL), BF16)
    return pl.pallas_call(
        body, grid=(T // tm,),
        out_shape=(sd, sd, sd, sd, jax.ShapeDtypeStruct((T, D_MODEL), F32)),
        in_specs=[_row_spec(tm, 512), _row_spec(tm, 512), _row_spec(tm, 2048), _row_spec(tm, D_MODEL),
                  _const_spec((512, D_MODEL)), _const_spec((512, D_MODEL)), _const_spec((D_MODEL, D_MODEL)),
                  _const_spec((1, D_MODEL))],
        out_specs=[_row_spec(tm, D_MODEL)] * 5,
        compiler_params=_params(40), name="fwd_merge",
    )(ys, ya, gt, x2, wbs, wba, wo, g2)


def _fwd_ffn_loss(h1, tgt, wup, wdn, g3, g4):
    T = h1.shape[0]
    tm = TOKEN_TILE
    nsteps = T // tm

    def body(h1_ref, tg_ref, wup_ref, wdn_ref, g3_ref, g4_ref, xn2_ref, a_ref, ddn_ref, dy_ref, loss_ref,
             dg4_ref, acc_l, acc_g):
        i = pl.program_id(0)

        @pl.when(i == 0)
        def _():
            acc_l[...] = jnp.zeros_like(acc_l)
            acc_g[...] = jnp.zeros_like(acc_g)

        h1v = h1_ref[...]
        r3, h1h = _rms_stats(h1v)
        xn2 = (h1h * g3_ref[...]).astype(BF16)
        xn2_ref[...] = xn2
        dn = jnp.zeros((tm, D_MODEL), F32)
        for j in range(D_FF // 1024):
            cols = slice(1024 * j, 1024 * (j + 1))
            a = _dot(xn2, wup_ref[:, cols])
            a_ref[:, cols] = a.astype(BF16)
            hid = jnp.square(jnp.maximum(a, 0.0)).astype(BF16)
            dn = dn + _dot(hid, wdn_ref[cols, :])
        r4, dnh = _rms_stats(dn)
        g4v = g4_ref[...]
        e = (h1v + dnh * g4v) - tg_ref[...]
        sq = e * e
        s1 = sq[:, 0:LANES]
        for j in range(1, D_MODEL // LANES):
            s1 = s1 + sq[:, LANES * j:LANES * (j + 1)]
        acc_l[...] += _fold8(s1)
        dy = e * (1.0 / D_MODEL)
        dy_ref[...] = dy
        acc_g[...] += _fold8(dy * dnh)
        ddn_ref[...] = _rms_bwd(dy, dnh, r4, g4v).astype(BF16)

        @pl.when(i == nsteps - 1)
        def _():
            loss_ref[...] = acc_l[...] * (0.5 / D_MODEL)
            dg4_ref[...] = jnp.sum(acc_g[...], axis=0, keepdims=True)

    return pl.pallas_call(
        body, grid=(nsteps,),
        out_shape=(jax.ShapeDtypeStruct((T, D_MODEL), BF16), jax.ShapeDtypeStruct((T, D_FF), BF16),
                   jax.ShapeDtypeStruct((T, D_MODEL), BF16), jax.ShapeDtypeStruct((T, D_MODEL), F32),
                   jax.ShapeDtypeStruct((SUBLANES, LANES), F32), jax.ShapeDtypeStruct((1, D_MODEL), F32)),
        in_specs=[_row_spec(tm, D_MODEL), _row_spec(tm, D_MODEL), _const_spec((D_MODEL, D_FF)),
                  _const_spec((D_FF, D_MODEL)), _const_spec((1, D_MODEL)), _const_spec((1, D_MODEL))],
        out_specs=[_row_spec(tm, D_MODEL), _row_spec(tm, D_FF), _row_spec(tm, D_MODEL), _row_spec(tm, D_MODEL),
                   pl.BlockSpec((SUBLANES, LANES), lambda i: (0, 0)), pl.BlockSpec((1, D_MODEL), lambda i: (0, 0))],
        scratch_shapes=[pltpu.VMEM((SUBLANES, LANES), F32), pltpu.VMEM((SUBLANES, D_MODEL), F32)],
        compiler_params=_params(52), name="fwd_ffn_loss",
    )(h1, tgt, wup, wdn, g3, g4)


def _bwd_ffn(ddn, a, dy, h1, wup, wdn, g3):
    T = h1.shape[0]
    tm = TOKEN_TILE
    nsteps = T // tm

    def body(ddn_ref, a_ref, dy_ref, h1_ref, wup_ref, wdn_ref, g3_ref, da_ref, dh1_ref, dg3_ref, acc_g):
        i = pl.program_id(0)

        @pl.when(i == 0)
        def _():
            acc_g[...] = jnp.zeros_like(acc_g)

        ddnv = ddn_ref[...]
        dxn2 = jnp.zeros((tm, D_MODEL), F32)
        for j in range(D_FF // 1024):
            cols = slice(1024 * j, 1024 * (j + 1))
            dhid = _dot_nt(ddnv, wdn_ref[cols, :])
            da = (dhid * (2.0 * jnp.maximum(a_ref[:, cols].astype(F32), 0.0))).astype(BF16)
            da_ref[:, cols] = da
            dxn2 = dxn2 + _dot_nt(da, wup_ref[:, cols])
        r3, h1h = _rms_stats(h1_ref[...])
        acc_g[...] += _fold8(dxn2 * h1h)
        dh1_ref[...] = dy_ref[...] + _rms_bwd(dxn2, h1h, r3, g3_ref[...])

        @pl.when(i == nsteps - 1)
        def _():
            dg3_ref[...] = jnp.sum(acc_g[...], axis=0, keepdims=True)

    return pl.pallas_call(
        body, grid=(nsteps,),
        out_shape=(jax.ShapeDtypeStruct((T, D_FF), BF16), jax.ShapeDtypeStruct((T, D_MODEL), F32),
                   jax.ShapeDtypeStruct((1, D_MODEL), F32)),
        in_specs=[_row_spec(tm, D_MODEL), _row_spec(tm, D_FF), _row_spec(tm, D_MODEL), _row_spec(tm, D_MODEL),
                  _const_spec((D_MODEL, D_FF)), _const_spec((D_FF, D_MODEL)), _const_spec((1, D_MODEL))],
        out_specs=[_row_spec(tm, D_FF), _row_spec(tm, D_MODEL), pl.BlockSpec((1, D_MODEL), lambda i: (0, 0))],
        scratch_shapes=[pltpu.VMEM((SUBLANES, D_MODEL), F32)],
        compiler_params=_params(52), name="bwd_ffn",
    )(ddn, a, dy, h1, wup, wdn, g3)


def _wgrad(xa, dy, name, relu2=False, tn=None, block_cols=None):
    T, K = xa.shape
    N = dy.shape[1]
    tn = N if tn is None else tn
    tt = min(T, WGRAD_TILE if K <= D_MODEL else WGRAD_TILE // 2)
    if block_cols:
        nb = tn // block_cols
        out_shape = jax.ShapeDtypeStruct((N // block_cols, K, block_cols), F32)
        out_spec = pl.BlockSpec((nb, K, block_cols), lambda n, t: (n, 0, 0))
    else:
        out_shape = jax.ShapeDtypeStruct((K, N), F32)
        out_spec = pl.BlockSpec((K, tn), lambda n, t: (0, n))

    def body(x_ref, dy_ref, o_ref):
        @pl.when(pl.program_id(1) == 0)
        def _():
            o_ref[...] = jnp.zeros_like(o_ref)

        xv = x_ref[...]
        if relu2:
            xv = jnp.square(jnp.maximum(xv.astype(F32), 0.0)).astype(BF16)
        if block_cols:
            for b in range(nb):
                o_ref[b] += _dot_tn(xv, dy_ref[:, block_cols * b:block_cols * (b + 1)])
        else:
            o_ref[...] += _dot_tn(xv, dy_ref[...])

    return pl.pallas_call(
        body, grid=(N // tn, T // tt), out_shape=out_shape,
        in_specs=[pl.BlockSpec((tt, K), lambda n, t: (t, 0)), pl.BlockSpec((tt, tn), lambda n, t: (t, n))],
        out_specs=out_spec,
        compiler_params=_params(52, 2), name=name,
    )(xa, dy)


def _wgrad_multi(xa, dys, name):
    T, K = xa.shape
    tt = min(T, WGRAD_TILE)
    n = len(dys)

    def body(x_ref, *refs):
        dy_refs, o_refs = refs[:n], refs[n:]

        @pl.when(pl.program_id(0) == 0)
        def _():
            for o_ref in o_refs:
                o_ref[...] = jnp.zeros_like(o_ref)

        xv = x_ref[...]
        for dy_ref, o_ref in zip(dy_refs, o_refs):
            o_ref[...] += _dot_tn(xv, dy_ref[...])

    return pl.pallas_call(
        body, grid=(T // tt,),
        out_shape=[jax.ShapeDtypeStruct((K, dy.shape[1]), F32) for dy in dys],
        in_specs=[_row_spec(tt, K)] + [_row_spec(tt, dy.shape[1]) for dy in dys],
        out_specs=[pl.BlockSpec((K, dy.shape[1]), lambda t: (0, 0)) for dy in dys],
        compiler_params=_params(52), name=name,
    )(xa, *dys)


def _bwd_merge(dh1, o, A, B, gt, ys, ya, mg, lse, cc, wbs, wba, wo, g2):
    T = dh1.shape[0]
    tm = MERGE_BWD_TILE
    nsteps = T // tm

    def body(dh1_ref, o_ref, a_ref, b_ref, gt_ref, ys_ref, ya_ref, mg_ref, lse_ref, cc_ref, wbs_ref, wba_ref,
             wo_ref, g2_ref, dgl_ref, dys_ref, dya_ref, qab_ref, dab_ref, dg2_ref, dwbs_ref, dwba_ref, dwo_ref,
             acc_g):
        i = pl.program_id(0)

        @pl.when(i == 0)
        def _():
            acc_g[...] = jnp.zeros_like(acc_g)
            dwbs_ref[...] = jnp.zeros_like(dwbs_ref)
            dwba_ref[...] = jnp.zeros_like(dwba_ref)
            dwo_ref[...] = jnp.zeros_like(dwo_ref)

        dh1v = dh1_ref[...]
        r2, oh = _rms_stats(o_ref[...].astype(F32))
        acc_g[...] += _fold8(dh1v * oh)
        do = _rms_bwd(dh1v, oh, r2, g2_ref[...]).astype(BF16)
        dwo_ref[...] += _dot_tn(mg_ref[...], do)
        dmg = _dot_nt(do, wo_ref[...])
        ga = gt_ref[:, :D_MODEL].astype(F32)
        gb = gt_ref[:, D_MODEL:].astype(F32)
        dgl_ref[:, :D_MODEL] = (dmg * a_ref[...].astype(F32) * ga * (1.0 - ga)).astype(BF16)
        dgl_ref[:, D_MODEL:] = (dmg * b_ref[...].astype(F32) * gb * (1.0 - gb)).astype(BF16)
        dA = (dmg * ga).astype(BF16)
        dB = (dmg * gb).astype(BF16)
        dwbs_ref[...] += _dot_tn(ys_ref[...], dA)
        dwba_ref[...] += _dot_tn(ya_ref[...], dB)
        dys_ref[...] = _dot_nt(dA, wbs_ref[...]).astype(BF16)
        dya = _dot_nt(dB, wba_ref[...]).astype(BF16)
        dya_ref[...] = dya
        prod = dya.astype(F32) * ya_ref[...].astype(F32)
        lane = _iota((tm, LANES), 1)
        low = lane < 64
        blk = jnp.zeros((tm, LANES), F32)
        for p in range(4):
            pp = prod[:, LANES * p:LANES * (p + 1)]
            blk = jnp.where(lane == 2 * p, jnp.sum(jnp.where(low, pp, 0.0), axis=1, keepdims=True), blk)
            blk = jnp.where(lane == 2 * p + 1, jnp.sum(jnp.where(low, 0.0, pp), axis=1, keepdims=True), blk)
        qab_ref[...] = _aug_query(cc_ref[...] - lse_ref[...])
        dab_ref[...] = _spread(_split3(-blk), 0).astype(BF16)

        @pl.when(i == nsteps - 1)
        def _():
            dg2_ref[...] = jnp.sum(acc_g[...], axis=0, keepdims=True)

    sh = jax.ShapeDtypeStruct((T, 512), BF16)
    sa = jax.ShapeDtypeStruct((T, LANES), BF16)
    sw = jax.ShapeDtypeStruct((512, D_MODEL), F32)
    whole = lambda shape: pl.BlockSpec(shape, lambda i: (0, 0))
    return pl.pallas_call(
        body, grid=(nsteps,),
        out_shape=(jax.ShapeDtypeStruct((T, 2048), BF16), sh, sh, sa, sa, jax.ShapeDtypeStruct((1, D_MODEL), F32),
                   sw, sw, jax.ShapeDtypeStruct((D_MODEL, D_MODEL), F32)),
        in_specs=[_row_spec(tm, D_MODEL)] * 4 + [_row_spec(tm, 2048), _row_spec(tm, 512), _row_spec(tm, 512),
                  _row_spec(tm, D_MODEL), _row_spec(tm, LANES), _row_spec(tm, LANES),
                  _const_spec((512, D_MODEL)), _const_spec((512, D_MODEL)),
                  _const_spec((D_MODEL, D_MODEL)), _const_spec((1, D_MODEL))],
        out_specs=[_row_spec(tm, 2048), _row_spec(tm, 512), _row_spec(tm, 512), _row_spec(tm, LANES),
                   _row_spec(tm, LANES), whole((1, D_MODEL)), whole((512, D_MODEL)), whole((512, D_MODEL)),
                   whole((D_MODEL, D_MODEL))],
        scratch_shapes=[pltpu.VMEM((SUBLANES, D_MODEL), F32)],
        compiler_params=_params(56), name="bwd_merge",
    )(dh1, o, A, B, gt, ys, ya, mg, lse, cc, wbs, wba, wo, g2)


def _bwd_sgu(zuv, dys, gs, bs, wsp, bT, grads):
    T = zuv.shape[0]
    tc = SGU_TILE
    nc = tc // CHUNK
    nsteps = T // tc
    ex = _GradExchange([tuple(g.shape[1:]) for g in grads])
    ng = ex.n

    def body(z_ref, dy_ref, gs_ref, bs_ref, w_ref, bT_ref, *rest):
        g_refs, (dz_ref, dw_ref, dbT_ref, dgs_ref, dbs_ref) = rest[:ng], rest[ng:ng + 5]
        land1 = rest[ng + 5:2 * ng + 5]
        acc_w, acc_b, acc_gs, acc_bs, dvln_s = rest[2 * ng + 5:2 * ng + 10]
        ex_sems = rest[2 * ng + 10:]
        i = pl.program_id(0)

        @pl.when(i == 0)
        def _():
            ex.start(1, g_refs, land1, ex_sems)
            acc_w[...] = jnp.zeros_like(acc_w)
            acc_b[...] = jnp.zeros_like(acc_b)
            acc_gs[...] = jnp.zeros_like(acc_gs)
            acc_bs[...] = jnp.zeros_like(acc_bs)

        z = z_ref[...].astype(F32)
        gsv = gs_ref[...]
        u, vhat, rs, vln = _sgu_forward_parts(z, gsv, bs_ref[...])
        vb = vln.astype(BF16)
        dy = dy_ref[...].astype(F32)
        low_w = (_iota((CHUNK, nc * LANES), 1) % LANES) < 64
        for p in range(4):
            we, wo, b2 = _sgu_pair_weights(w_ref, bT_ref[...], p)
            vcat = _chunks_on_lanes(vb, p, nc)
            s = _sgu_mix(we, wo, b2, vcat, nc)
            dyc = _chunks_on_lanes(dy, p, nc)
            ds = dyc * _chunks_on_lanes(u, p, nc)
            dsb = ds.astype(BF16)
            zero = jnp.zeros_like(dsb)
            dse = jnp.where(low_w, dsb, zero)
            dso = jnp.where(low_w, zero, dsb)
            acc_w[2 * p] += _dot_nt(dse, vcat)
            acc_w[2 * p + 1] += _dot_nt(dso, vcat)
            acc_b[p] += ds
            dvl = jnp.where(low_w, _dot_tn(we, dsb), _dot_tn(wo, dsb))
            for c in range(nc):
                rows, cols = slice(c * CHUNK, (c + 1) * CHUNK), slice(LANES * p, LANES * (p + 1))
                dvln_s[rows, cols] = dvl[:, c * LANES:(c + 1) * LANES]
                du = dy[rows, cols] * s[:, c * LANES:(c + 1) * LANES]
                dz_ref[rows, cols] = (du * _gelu_grad(z[rows, cols])).astype(BF16)
        dvln = dvln_s[...]
        acc_gs[...] += _fold8(dvln * vhat)
        acc_bs[...] += _fold8(dvln)
        al = dvln * gsv
        dvv = rs * (al - jnp.mean(al, axis=1, keepdims=True) - vhat * jnp.mean(al * vhat, axis=1, keepdims=True))
        dz_ref[:, SGU_WIDTH:] = (dvv * _gelu_grad(z[:, SGU_WIDTH:])).astype(BF16)

        @pl.when(i == nsteps - 1)
        def _():
            tril = _iota((CHUNK, CHUNK), 0) >= _iota((CHUNK, CHUNK), 1)
            lane = _iota((CHUNK, LANES), 1)
            low = lane < 64
            blk = jnp.zeros((CHUNK, LANES), F32)
            for g in range(8):
                dw_ref[g] = jnp.where(tril, acc_w[g], 0.0)
            for p in range(4):
                t = acc_b[p]
                tot = t[:, 0:LANES]
                for c in range(1, nc):
                    tot = tot + t[:, c * LANES:(c + 1) * LANES]
                blk = jnp.where(lane == 2 * p, jnp.sum(jnp.where(low, tot, 0.0), axis=1, keepdims=True), blk)
                blk = jnp.where(lane == 2 * p + 1, jnp.sum(jnp.where(low, 0.0, tot), axis=1, keepdims=True), blk)
            dbT_ref[...] = blk
            dgs_ref[...] = jnp.sum(acc_gs[...], axis=0, keepdims=True)
            dbs_ref[...] = jnp.sum(acc_bs[...], axis=0, keepdims=True)
            ex.wait(1, g_refs, land1, ex_sems)

    whole = lambda shape: pl.BlockSpec(shape, lambda i: (0,) * len(shape))
    hbm_spec = pl.BlockSpec(memory_space=pl.ANY)
    return pl.pallas_call(
        body, grid=(nsteps,),
        out_shape=[jax.ShapeDtypeStruct((T, 1024), BF16), jax.ShapeDtypeStruct((8, CHUNK, CHUNK), F32),
                   jax.ShapeDtypeStruct((CHUNK, LANES), F32), jax.ShapeDtypeStruct((1, SGU_WIDTH), F32),
                   jax.ShapeDtypeStruct((1, SGU_WIDTH), F32)] + ex.land_shapes(1),
        in_specs=[_row_spec(tc, 1024), _row_spec(tc, SGU_WIDTH), _const_spec((1, SGU_WIDTH)),
                  _const_spec((1, SGU_WIDTH)), _const_spec((8, CHUNK, CHUNK)), _const_spec((CHUNK, 8))]
        + [hbm_spec] * ng,
        out_specs=[_row_spec(tc, 1024), whole((8, CHUNK, CHUNK)), whole((CHUNK, LANES)),
                   whole((1, SGU_WIDTH)), whole((1, SGU_WIDTH))] + [hbm_spec] * ng,
        scratch_shapes=[pltpu.VMEM((8, CHUNK, CHUNK), F32), pltpu.VMEM((4, CHUNK, nc * LANES), F32),
                        pltpu.VMEM((SUBLANES, SGU_WIDTH), F32), pltpu.VMEM((SUBLANES, SGU_WIDTH), F32),
                        pltpu.VMEM((tc, SGU_WIDTH), F32)] + ex.sem_shapes(1),
        compiler_params=_params(48), name="bwd_sgu",
    )(zuv, dys, gs, bs, wsp, bT, *grads)


def _bwd_attn(qkv, dya, qab, dab, kaug, parts):
    T = qkv.shape[0]
    tq = tk = ATTN_TILE
    nq = T // tq
    nk = T // tk
    ex = _GradExchange([tuple(g.shape[1:]) for g in parts])
    nr = ex.n

    def body(q_ref, do_ref, qa_ref, da_ref, k_ref, v_ref, ka_ref, *rest):
        part_refs, (dq_ref, dk_ref, dv_ref, dcx_ref) = rest[:nr], rest[nr:nr + 4]
        land2, dq_acc, ex_sems = rest[nr + 4:2 * nr + 4], rest[2 * nr + 4], rest[2 * nr + 5:]
        p = pl.program_id(0)
        j = pl.program_id(1)

        @pl.when((p == 0) & (j == 0))
        def _():
            ex.start(2, part_refs, land2, ex_sems)

        lane = _iota((tq, LANES), 1)
        low = lane < 64
        row = _iota((2 * tq, tk), 0) % tq
        col = _iota((2 * tq, tk), 1)
        first = 2 * AUG_LANES * p
        half = tq // 2

        @pl.when(j == 0)
        def _():
            dq_acc[...] = jnp.zeros_like(dq_acc)

        @pl.when((j == 0) & (p == 0))
        def _():
            dcx_ref[...] = jnp.zeros_like(dcx_ref)

        ka = ka_ref[...]
        kk = jnp.concatenate([k_ref[...], ka], axis=1)
        vv = jnp.concatenate([v_ref[...], ka], axis=1)

        def tile(i, carry, r0, c0, nc, masked):
            dk_a, dv_a = carry
            nr = tq - r0
            qsl = pl.ds(pl.multiple_of(i * tq + r0, half), nr)
            qs = _aug_stack(q_ref[qsl, :], qa_ref[qsl, :], p)
            dos = _aug_stack(do_ref[qsl, :], da_ref[qsl, :], p)
            kc, vc = kk[c0:c0 + nc], vv[c0:c0 + nc]
            s = _dot_nt(qs, kc)
            if masked:
                s = jnp.where(col[:2 * nr, :nc] + c0 <= row[:2 * nr, :nc] % nr + r0, s, NEG)
            pm = jnp.exp(s)
            ds = pm * _dot_nt(dos, vc)
            dsb = ds.astype(BF16)
            dv_u = _dot_tn(pm.astype(BF16), dos[:, :LANES])
            dk_u = _dot_tn(dsb, qs)
            if nc == tk:
                dv_a, dk_a = dv_a + dv_u, dk_a + dk_u
            else:
                pad = lambda u: jnp.concatenate(
                    [jnp.zeros((n, u.shape[1]), F32) if z else u
                     for z, n in ((True, c0), (False, nc), (True, tk - c0 - nc)) if n], axis=0)
                dv_a, dk_a = dv_a + pad(dv_u), dk_a + pad(dk_u)
            dqx = _dot(dsb, kc)
            dq_acc[qsl, :] += jnp.where(low[:nr], dqx[:nr, :LANES], dqx[nr:, :LANES])
            dcx_ref[qsl, :] += (jnp.where(lane[:nr] == first, dqx[:nr, LANES:], 0.0)
                                + jnp.where(lane[:nr] == first + AUG_LANES, dqx[nr:, LANES:], 0.0))
            return dk_a, dv_a

        def q_block(i, carry, masked):
            return tile(i, carry, 0, 0, tk, masked)

        init =(jnp.zeros((tk, 2 * LANES), F32), jnp.zeros((tk, LANES), F32))
        carry = tile(j, init, 0, 0, half, True)
        carry = tile(j, carry, half, half, half, True)
        n_rest = nq - 1 - j
        carry = lax.fori_loop(
            0, n_rest // 2, lambda t, c: q_block(j + 2 + 2 * t, q_block(j + 1 + 2 * t, c, False), False), carry)
        dk_a, dv_a = lax.cond(n_rest % 2 == 1, lambda c: q_block(nq - 1, c, False), lambda c: c, carry)
        dk_ref[...] = dk_a[:, :LANES].astype(BF16)
        dv_ref[...] = dv_a.astype(BF16)
        ksl = pl.ds(pl.multiple_of(j * tk, tk), tk)
        lk = _iota((tk, LANES), 1)
        dcx_ref[ksl, :] += jnp.where((lk == first + 3) | (lk == first + AUG_LANES + 3), dk_a[:, LANES:], 0.0)

        @pl.when(j == nk - 1)
        def _():
            dq_ref[...] = (dq_acc[...] * 0.125).astype(BF16)

        @pl.when((p == 3) & (j == nk - 1))
        def _():
            ex.wait(2, part_refs, land2, ex_sems)

    sh = jax.ShapeDtypeStruct((T, ATTN_WIDTH), BF16)
    full = lambda cb: pl.BlockSpec((T, LANES), lambda p, j: (0, cb + p))
    blk = lambda cb: pl.BlockSpec((tk, LANES), lambda p, j: (j, cb + p))
    hbm_spec = pl.BlockSpec(memory_space=pl.ANY)
    return pl.pallas_call(
        body, grid=(4, nk),
        out_shape=[sh, sh, sh, jax.ShapeDtypeStruct((T, LANES), F32)] + ex.land_shapes(2),
        in_specs=[full(0), full(0), _const_spec((T, LANES)), _const_spec((T, LANES)), blk(4), blk(8),
                  pl.BlockSpec((tk, LANES), lambda p, j: (j, 0))] + [hbm_spec] * nr,
        out_specs=[full(0), blk(0), blk(0), pl.BlockSpec((T, LANES), lambda p, j: (0, 0))] + [hbm_spec] * nr,
        scratch_shapes=[pltpu.VMEM((T, LANES), F32)] + ex.sem_shapes(2),
        compiler_params=_params(58, 2), name="bwd_attn",
    )(qkv, dya, qab, dab, qkv, qkv, kaug, *parts)


def _bwd_cum(dcx, fl, bfp):
    T = fl.shape[0]
    tb = CUM_TILE

    def body(dcx_ref, fl_ref, b_ref, dfl_ref, dbf_ref):
        triu = (_iota((tb, tb), 0) <= _iota((tb, tb), 1)).astype(F32)
        r, c = _iota((LANES, LANES), 0), _iota((LANES, LANES), 1)
        sel = (((r == AUG_LANES * c) & (c < N_HEADS)).astype(F32)
               - ((r == AUG_LANES * c + 3) & (c < N_HEADS)).astype(F32))
        carry = jnp.zeros((1, LANES), F32)
        dbf = jnp.zeros((1, LANES), F32)
        for i in reversed(range(T // tb)):
            colblk = jnp.dot(dcx_ref[i * tb:(i + 1) * tb, :], sel, precision=HIGHEST, preferred_element_type=F32)
            rc = jnp.dot(triu, colblk, precision=HIGHEST, preferred_element_type=F32) + carry
            carry = rc[0:1, :]
            sig = jax.nn.sigmoid(fl_ref[i * tb:(i + 1) * tb, :] + b_ref[...])
            dfl = rc * (1.0 - sig)
            dfl_ref[i * tb:(i + 1) * tb, :] = dfl.astype(BF16)
            dbf = dbf + jnp.sum(dfl, axis=0, keepdims=True)
        dbf_ref[...] = dbf

    return pl.pallas_call(
        body,
        out_shape=(jax.ShapeDtypeStruct((T, LANES), BF16), jax.ShapeDtypeStruct((1, LANES), F32)),
        compiler_params=pltpu.CompilerParams(vmem_limit_bytes=32 * MIB), name="bwd_cum",
    )(dcx, fl, bfp)


def _bwd_in(dz, dq, dk, dv, dfl, dgl, dh1, x2, g1, wz, wf, wg, rows, name, prev=None, stage=0, exchanged=None):
    T = x2.shape[0]
    tm = TOKEN_TILE
    first = rows[0] // tm
    nsteps = (rows[1] - rows[0]) // tm
    ex = _GradExchange([tuple(exchanged.shape[1:])]) if stage else None

    def body(dz_ref, dq_ref, dk_ref, dv_ref, dfl_ref, dgl_ref, dh1_ref, x_ref, g_ref, wz_ref, wf_ref, wg_ref, *rest):
        rest = list(rest)
        dx_prev, dg1_prev = (rest.pop(0), rest.pop(0)) if prev else (None, None)
        src_ref = rest.pop(0) if stage else None
        dx_ref, dg1_ref = rest.pop(0), rest.pop(0)
        land_ref = rest.pop(0) if stage else None
        acc_g, ex_sems = rest[0], rest[1:]
        i = pl.program_id(0)

        @pl.when(i == 0)
        def _():
            if stage:
                ex.start(stage, [src_ref], [land_ref], ex_sems)
            acc_g[...] = jnp.zeros_like(acc_g)

        dxn = _dot_nt(dz_ref[...], wz_ref[:, 0:1024])
        dxn = dxn + _dot_nt(dq_ref[...], wz_ref[:, 1024:1536])
        dxn = dxn + _dot_nt(dk_ref[...], wz_ref[:, 1536:2048])
        dxn = dxn + _dot_nt(dv_ref[...], wz_ref[:, 2048:2560])
        dxn = dxn + _dot_nt(dfl_ref[...], wf_ref[...])
        dxn = dxn + _dot_nt(dgl_ref[...], wg_ref[...])
        r1, xh = _rms_stats(x_ref[...])
        acc_g[...] += _fold8(dxn * xh)
        dx_ref[...] = dh1_ref[...] + _rms_bwd(dxn, xh, r1, g_ref[...])

        @pl.when(i == nsteps - 1)
        def _():
            total = jnp.sum(acc_g[...], axis=0, keepdims=True)
            dg1_ref[...] = total + dg1_prev[...] if prev else total
            if stage:
                ex.wait(stage, [src_ref], [land_ref], ex_sems)

    hbm_spec = pl.BlockSpec(memory_space=pl.ANY)
    rows_spec = lambda n: pl.BlockSpec((tm, n), lambda i: (i + first, 0))
    operands = [dz, dq, dk, dv, dfl, dgl, dh1, x2, g1, wz, wf, wg]
    in_specs = [rows_spec(1024), rows_spec(512), rows_spec(512), rows_spec(512), rows_spec(LANES), rows_spec(2048),
                rows_spec(D_MODEL), rows_spec(D_MODEL),
                _const_spec((1, D_MODEL)), _const_spec((D_MODEL, ZQKV_WIDTH)), _const_spec((D_MODEL, LANES)),
                _const_spec((D_MODEL, 2048))]
    aliases = {}
    if prev:
        aliases = {len(operands): 0}
        operands += list(prev)
        in_specs += [hbm_spec, _const_spec((1, D_MODEL))]
    if stage:
        operands.append(exchanged)
        in_specs.append(hbm_spec)
    return pl.pallas_call(
        body, grid=(nsteps,),
        out_shape=[jax.ShapeDtypeStruct((T, D_MODEL), F32), jax.ShapeDtypeStruct((1, D_MODEL), F32)]
        + (ex.land_shapes(stage) if stage else []),
        in_specs=in_specs,
        out_specs=[rows_spec(D_MODEL), pl.BlockSpec((1, D_MODEL), lambda i: (0, 0))] + ([hbm_spec] if stage else []),
        scratch_shapes=[pltpu.VMEM((SUBLANES, D_MODEL), F32)] + (ex.sem_shapes(stage) if stage else []),
        input_output_aliases=aliases,
        compiler_params=_params(48), name=name,
    )(*operands)


def _small_kernel_shapes(g_mix_pre, b_forget, g_sgu, b_sgu, w_spatial, b_spatial, g_mix_post, g_ffn_pre, g_ffn_post):
    return dict(g_mix_pre=g_mix_pre, b_forget=jnp.pad(b_forget, ((0, 0), (0, LANES - N_HEADS))), g_sgu=g_sgu,
                b_sgu=b_sgu, w_spatial=w_spatial[0], b_spatial=b_spatial[0], g_mix_post=g_mix_post,
                g_ffn_pre=g_ffn_pre, g_ffn_post=g_ffn_post)


def _small_output_shapes(d):
    out = dict(d)
    out.update(b_forget=d["b_forget"][:, :N_HEADS], w_spatial=d["w_spatial"][None], b_spatial=d["b_spatial"][None])
    return out


def kernel(x, g_mix_pre, w_in, b_forget, g_sgu, b_sgu, w_spatial, b_spatial, w_branch_sgu, w_branch_attn, w_out, g_mix_post, g_ffn_pre, w_up, w_down, g_ffn_post, loss_target, m_g_mix_pre, m_w_in, m_b_forget, m_g_sgu, m_b_sgu, m_w_spatial, m_b_spatial, m_w_branch_sgu, m_w_branch_attn, m_w_out, m_g_mix_post, m_g_ffn_pre, m_w_up, m_w_down, m_g_ffn_post, v_g_mix_pre, v_w_in, v_b_forget, v_g_sgu, v_b_sgu, v_w_spatial, v_b_spatial, v_w_branch_sgu, v_w_branch_attn, v_w_out, v_g_mix_post, v_g_ffn_pre, v_w_up, v_w_down, v_g_ffn_post):
    T = x.shape[1]
    x2 = x.reshape(T, D_MODEL)
    tgt = loss_target.reshape(T, D_MODEL)

    wg_in = _gather_w_in(w_in[0])
    wz, wf, wgt = _assemble_w_in(wg_in)
    bfp = jnp.pad(b_forget, ((0, 0), (0, LANES - N_HEADS)))
    wsp = w_spatial[0]
    bT = b_spatial[0].T

    xn, zuv, qkv, fl, gt = _fwd_in(x2, g_mix_pre, wz, wf, wgt)
    cc, qaug, kaug = _fwd_cum(fl, bfp)
    ys = _fwd_sgu(zuv, g_sgu, b_sgu, wsp, bT)
    ya, lse, wbs, wba, wo, wup, wdn = _fwd_attn(
        qkv, qaug, kaug, (w_branch_sgu[0], w_branch_attn[0], w_out[0], w_up[0], w_down[0]))
    A, B, mg, o, h1 = _fwd_merge(ys, ya, gt, x2, wbs, wba, wo, g_mix_post)
    xn2, a, ddn, dy, loss_part, dg4 = _fwd_ffn_loss(h1, tgt, wup, wdn, g_ffn_pre, g_ffn_post)

    da, dh1, dg3 = _bwd_ffn(ddn, a, dy, h1, wup, wdn, g_ffn_pre)
    dw_up = _wgrad(xn2, da, "wgrad_up", tn=2048, block_cols=512)
    dw_down = _wgrad(a, ddn, "wgrad_down", relu2=True)
    dgl, dys, dya, qab, dab, dg2, dw_bs, dw_ba, dw_out = _bwd_merge(
        dh1, o, A, B, gt, ys, ya, mg, lse, cc, wbs, wba, wo, g_mix_post)
    col_blocks = lambda g, w: g.reshape(g.shape[0], N_DEV, w).transpose(1, 0, 2)
    row_blocks = lambda g, r: g.reshape(N_DEV, r, g.shape[1])
    early_names = ["w_branch_sgu", "w_branch_attn", "w_out", "w_up", "w_down"]
    early = [col_blocks(dw_bs, 128), col_blocks(dw_ba, 128), row_blocks(dw_out, 128), dw_up, row_blocks(dw_down, 512)]
    owners = _owner_indices()
    dzuv, dwsp, dbT, dgs, dbs, *early_land1 = _bwd_sgu(zuv, dys, g_sgu, b_sgu, wsp, bT, early)
    early_parts = [_chip_partials(g, l1, owners, "chip_partials_" + nm)
                   for g, l1, nm in zip(early, early_land1, early_names)]
    dq, dk, dv, dcx, *early_land2 = _bwd_attn(qkv, dya, qab, dab, kaug, early_parts)
    dfl, dbf = _bwd_cum(dcx, fl, bfp)
    dw_z = _wgrad(xn, dzuv, "wgrad_in_z")
    dw_q, dw_k, dw_v, dw_f = _wgrad_multi(xn, [dq, dk, dv, dfl], "wgrad_in_qkvf")
    dw_g = _wgrad(xn, dgl, "wgrad_in_gate")
    blocks_in = _block_dw_in([dw_z, dw_q, dw_k, dw_v, dw_f, dw_g])
    bwd_in_args = (dzuv, dq, dk, dv, dfl, dgl, dh1, x2, g_mix_pre, wz, wf, wgt)
    dx, dg1, land1_in = _bwd_in(*bwd_in_args, (0, T // 4), "bwd_in_a", stage=1, exchanged=blocks_in)
    part_in = _chip_partials(blocks_in, land1_in, owners, "chip_partials_w_in")
    dx, dg1, land2_in = _bwd_in(*bwd_in_args, (T // 4, 3 * T // 4), "bwd_in_b", prev=(dx, dg1), stage=2,
                                exchanged=part_in)
    dx, dg1 = _bwd_in(*bwd_in_args, (3 * T // 4, T), "bwd_in_c", prev=(dx, dg1))

    tot_a, tot_b = _allreduce_small(dict(
        g_mix_pre=dg1, b_forget=dbf, g_sgu=dgs, b_sgu=dbs, w_spatial=dwsp, b_spatial=dbT, g_mix_post=dg2,
        g_ffn_pre=dg3, g_ffn_post=dg4), loss_part)
    r0, nr, c0, nc = LOSS_SLOT
    loss = jnp.sum(tot_a[r0:r0 + nr, c0:c0 + nc])
    small_w = _small_kernel_shapes(g_mix_pre, b_forget, g_sgu, b_sgu, w_spatial, b_spatial, g_mix_post, g_ffn_pre,
                                   g_ffn_post)
    small_m = _small_kernel_shapes(m_g_mix_pre, m_b_forget, m_g_sgu, m_b_sgu, m_w_spatial, m_b_spatial, m_g_mix_post,
                                   m_g_ffn_pre, m_g_ffn_post)
    small_v = _small_kernel_shapes(v_g_mix_pre, v_b_forget, v_g_sgu, v_b_sgu, v_w_spatial, v_b_spatial, v_g_mix_post,
                                   v_g_ffn_pre, v_g_ffn_post)
    sg, sd, sm, sv = (_small_output_shapes(d) for d in _adamw_small(tot_a, tot_b, small_w, small_m, small_v))

    big = {}
    g_in = _reduced_grad(blocks_in, land1_in, land2_in, owners, "reduced_grad_w_in")[:, :IN_SHARD]
    d_, m_, v_ = _adamw(w_in[0], g_in, m_w_in[0], v_w_in[0], "adamw_w_in")
    big["w_in"] = (g_in[None], d_[None], m_[None], v_[None])
    early_wmv = [(w_branch_sgu, m_w_branch_sgu, v_w_branch_sgu), (w_branch_attn, m_w_branch_attn, v_w_branch_attn),
                 (w_out, m_w_out, v_w_out), (w_up, m_w_up, v_w_up), (w_down, m_w_down, v_w_down)]
    for nm, (w, m, v), g, l1, l2 in zip(early_names, early_wmv, early, early_land1, early_land2):
        big[nm] = tuple(t[None] for t in _adamw_reduced(w[0], m[0], v[0], g, l1, l2, owners, "adamw_" + nm))

    order = ["g_mix_pre", "w_in", "b_forget", "g_sgu", "b_sgu", "w_spatial", "b_spatial", "w_branch_sgu",
             "w_branch_attn", "w_out", "g_mix_post", "g_ffn_pre", "w_up", "w_down", "g_ffn_post"]
    outs = [loss, dx.reshape(1, T, D_MODEL)]
    for kind, small in enumerate((sg, sd, sm, sv)):
        outs += [big[nm][kind] if nm in big else small[nm] for nm in order]
    return tuple(outs)
```

```python
import jax
import jax.numpy as jnp
from jax import lax
from jax.experimental import pallas as pl
from jax.experimental.pallas import tpu as pltpu

F32 = jnp.float32
BF16 = jnp.bfloat16
HIGHEST = lax.Precision.HIGHEST
MESH = pl.DeviceIdType.MESH

D_MODEL = 1024
SGU_WIDTH = 512
ATTN_WIDTH = 512
N_HEADS = 8
CHUNK = 128
D_FF = 4096
IN_WIDTH = 4616
N_DEV = 8
IN_SHARD = IN_WIDTH // N_DEV
IN_SHARD_PAD = 640
ZQKV_WIDTH = 2 * SGU_WIDTH + 3 * ATTN_WIDTH
GATE_OFFSET = ZQKV_WIDTH + N_HEADS
EPS = 1e-6
LANES = 128
SUBLANES = 8
VMEM_BYTES = 64 * 1024 * 1024
MIB = 1024 * 1024

ADAM_LR = 0.001
ADAM_B1 = 0.9
ADAM_B2 = 0.999
ADAM_EPS = 1e-08
ADAM_WD = 0.01
ADAM_STEP = 10

TOKEN_TILE = 256
MERGE_BWD_TILE = 512
ATTN_TILE = 512
CUM_TILE = 256
SGU_TILE = 512
WGRAD_TILE = 1024
NEG = -1e30

NT_DIMS = (((1,), (1,)), ((), ()))
TN_DIMS = (((0,), (0,)), ((), ()))


def _params(vmem_mb, n_grid=1):
    return pltpu.CompilerParams(
        dimension_semantics=("arbitrary",) * n_grid,
        vmem_limit_bytes=min(vmem_mb * MIB, VMEM_BYTES - 6 * MIB),
    )


def _dot(a, b):
    return jnp.dot(a, b, preferred_element_type=F32)


def _dot_nt(a, b):
    return lax.dot_general(a, b, NT_DIMS, preferred_element_type=F32)


def _dot_tn(a, b):
    return lax.dot_general(a, b, TN_DIMS, preferred_element_type=F32)


def _const_spec(shape):
    nd = len(shape)
    return pl.BlockSpec(shape, lambda *_: (0,) * nd, pipeline_mode=pl.Buffered(1))


def _row_spec(tm, n, col=0):
    return pl.BlockSpec((tm, n), lambda i: (i, col))


def _fold8(v):
    return v.reshape(v.shape[0] // SUBLANES, SUBLANES, v.shape[1]).sum(axis=0)


def _pick(v, lane_iota, k):
    return jnp.sum(jnp.where(lane_iota == k, v, 0.0), axis=1, keepdims=True)


def _iota(shape, dim):
    return lax.broadcasted_iota(jnp.int32, shape, dim)


def _gelu(x):
    c = 0.7978845608028654
    return 0.5 * x * (1.0 + jnp.tanh(c * (x + 0.044715 * x * x * x)))


def _gelu_grad(x):
    c = 0.7978845608028654
    t = jnp.tanh(c * (x + 0.044715 * x * x * x))
    return 0.5 * (1.0 + t) + 0.5 * x * (1.0 - t * t) * (c * (1.0 + 3.0 * 0.044715 * x * x))


def _rms_stats(v):
    r = lax.rsqrt(jnp.mean(v * v, axis=1, keepdims=True) + EPS)
    return r, v * r


def _rms_bwd(dout, vhat, r, g):
    a = dout * g
    return r * (a - vhat * jnp.mean(a * vhat, axis=1, keepdims=True))


def _mesh_pos():
    return lax.axis_index("x"), lax.axis_index("y"), lax.axis_index("c")


def _dev_index(px, py, pc):
    return 4 * px + 2 * py + pc


def _other_chips(x, y):
    return [(1 - x, y), (x, 1 - y), (1 - x, 1 - y)]


class _WeightGather:
    def __init__(self, shard_shapes, kinds, stage_shapes=None):
        self.shard_shapes = list(shard_shapes)
        self.kinds = list(kinds)
        self.stage_shapes = list(stage_shapes or shard_shapes)
        self.n = len(self.kinds)

    def out_shapes(self):
        shapes = []
        for (rows, cols), kind in zip(self.stage_shapes, self.kinds):
            full = {"block": (N_DEV, rows, cols), "rows": (N_DEV * rows, cols), "cols": (rows, N_DEV * cols)}[kind]
            shapes.append(jax.ShapeDtypeStruct(full, BF16))
        return shapes

    def scratch_shapes(self):
        return ([pltpu.VMEM(s, BF16) for s in self.stage_shapes]
                + [pltpu.SemaphoreType.DMA((self.n, 7)), pltpu.SemaphoreType.DMA((self.n, 7)),
                   pltpu.SemaphoreType.DMA((self.n,))])

    def _view(self, a, ref, j):
        rows, cols = self.stage_shapes[a]
        if self.kinds[a] == "block":
            return ref.at[j]
        if self.kinds[a] == "rows":
            return ref.at[pl.ds(pl.multiple_of(j * rows, rows), rows), :]
        return ref.at[:, pl.ds(pl.multiple_of(j * cols, cols), cols)]

    def _copy(self, outs, scratch, a, k, block, to, from_stage=False):
        stage, (send_sems, recv_sems, _) = scratch[:self.n], scratch[self.n:]
        dst = self._view(a, outs[a], _dev_index(*block))
        return pltpu.make_async_remote_copy(
            src_ref=stage[a] if from_stage else dst, dst_ref=dst,
            send_sem=send_sems.at[a, k], recv_sem=recv_sems.at[a, k],
            device_id=to, device_id_type=MESH)

    def _local(self, outs, scratch, a, me):
        return pltpu.make_async_copy(scratch[a], self._view(a, outs[a], _dev_index(*me)), scratch[-1].at[a])

    def start(self, ins, outs, scratch):
        x, y, c = _mesh_pos()
        me, sibling = (x, y, c), (x, y, 1 - c)
        for a in range(self.n):
            rows, cols = self.shard_shapes[a]
            if self.stage_shapes[a] != self.shard_shapes[a]:
                scratch[a][...] = jnp.zeros(self.stage_shapes[a], BF16)
            scratch[a][0:rows, 0:cols] = ins[a][...].astype(BF16)
            self._local(outs, scratch, a, me).start()
        for a in range(self.n):
            self._copy(outs, scratch, a, 0, me, sibling, True).start()
            for j, chip in enumerate(_other_chips(x, y)):
                self._copy(outs, scratch, a, 1 + j, me, (*chip, c), True).start()

    def forward(self, outs, scratch):
        x, y, c = _mesh_pos()
        me, sibling = (x, y, c), (x, y, 1 - c)
        for a in range(self.n):
            for j, chip in enumerate(_other_chips(x, y)):
                self._copy(outs, scratch, a, 1 + j, (*chip, c), me).wait_recv()
                self._copy(outs, scratch, a, 4 + j, (*chip, c), sibling).start()

    def finish(self, outs, scratch):
        x, y, c = _mesh_pos()
        me, sibling = (x, y, c), (x, y, 1 - c)
        chips = _other_chips(x, y)
        for a in range(self.n):
            self._copy(outs, scratch, a, 0, sibling, me).wait_recv()
            for j, chip in enumerate(chips):
                self._copy(outs, scratch, a, 4 + j, (*chip, 1 - c), me).wait_recv()
        for a in range(self.n):
            self._copy(outs, scratch, a, 0, me, sibling, True).wait_send()
            for j, chip in enumerate(chips):
                self._copy(outs, scratch, a, 1 + j, me, (*chip, c), True).wait_send()
                self._copy(outs, scratch, a, 4 + j, (*chip, c), sibling).wait_send()
            self._local(outs, scratch, a, me).wait()


def _gather_w_in(w_in_local):
    g = _WeightGather([(D_MODEL, IN_SHARD)], ["block"], [(D_MODEL, IN_SHARD_PAD)])

    def body(w_ref, out_ref, *scratch):
        g.start([w_ref], [out_ref], scratch)
        g.forward([out_ref], scratch)
        g.finish([out_ref], scratch)

    return pl.pallas_call(
        body,
        out_shape=g.out_shapes()[0],
        in_specs=[pl.BlockSpec(memory_space=pltpu.VMEM)],
        out_specs=pl.BlockSpec(memory_space=pl.ANY),
        scratch_shapes=g.scratch_shapes(),
        compiler_params=pltpu.CompilerParams(vmem_limit_bytes=32 * MIB),
        name="gather_w_in",
    )(w_in_local)


class _GradExchange:
    def __init__(self, shapes):
        self.shapes = [tuple(s) for s in shapes]
        self.n = len(self.shapes)

    def land_shapes(self, stage):
        slots, dtype = (4, F32) if stage == 1 else (3, BF16)
        return [jax.ShapeDtypeStruct((slots,) + s, dtype) for s in self.shapes]

    def sem_shapes(self, stage):
        slots = 4 if stage == 1 else 3
        return [pltpu.SemaphoreType.DMA((self.n, slots)), pltpu.SemaphoreType.DMA((self.n, slots))]

    def _copy(self, stage, srcs, lands, sems, a, k):
        x, y, c = _mesh_pos()
        cx, cy = (_other_chips(x, y) + [(x, y)])[k]
        if stage == 1:
            src, to = srcs[a].at[_dev_index(cx, cy, 1 - c)], (x, y, 1 - c)
        else:
            src, to = srcs[a].at[k], (cx, cy, c)
        return pltpu.make_async_remote_copy(
            src_ref=src, dst_ref=lands[a].at[k], send_sem=sems[0].at[a, k], recv_sem=sems[1].at[a, k],
            device_id=to, device_id_type=MESH)

    def start(self, stage, srcs, lands, sems):
        for a in range(self.n):
            for k in range(4 if stage == 1 else 3):
                self._copy(stage, srcs, lands, sems, a, k).start()

    def wait(self, stage, srcs, lands, sems):
        for a in range(self.n):
            for k in range(4 if stage == 1 else 3):
                cp = self._copy(stage, srcs, lands, sems, a, k)
                cp.wait_recv()
                cp.wait_send()


def _owner_indices():
    x, y, c = _mesh_pos()
    return jnp.stack([_dev_index(cx, cy, c) for cx, cy in _other_chips(x, y) + [(x, y)]]).astype(jnp.int32)


def _chip_partials(g, land1, idx, name):
    _, rows, cols = g.shape
    tr = min(rows, 256)

    def body(idx_ref, g_ref, l_ref, o_ref):
        o_ref[...] = (g_ref[...] + l_ref[...]).astype(BF16)

    return pl.pallas_call(
        body,
        grid_spec=pltpu.PrefetchScalarGridSpec(
            num_scalar_prefetch=1, grid=(3, rows // tr),
            in_specs=[pl.BlockSpec((None, tr, cols), lambda k, r, idx: (idx[k], r, 0)),
                      pl.BlockSpec((None, tr, cols), lambda k, r, idx: (k, r, 0))],
            out_specs=pl.BlockSpec((None, tr, cols), lambda k, r, idx: (k, r, 0))),
        out_shape=jax.ShapeDtypeStruct((3, rows, cols), BF16),
        compiler_params=_params(32, 2), name=name,
    )(idx, g, land1)


def _reduced_block(g_ref, l1_ref, a_ref, b_ref, c_ref):
    return ((g_ref[...] + l1_ref[...]) + a_ref[...].astype(F32)) + b_ref[...].astype(F32) + c_ref[...].astype(F32)


def _reduced_specs(tm, cols):
    return [pl.BlockSpec((None, tm, cols), lambda i, idx: (idx[3], i, 0)),
            pl.BlockSpec((None, tm, cols), lambda i, idx: (3, i, 0)),
            pl.BlockSpec((None, tm, cols), lambda i, idx: (0, i, 0)),
            pl.BlockSpec((None, tm, cols), lambda i, idx: (1, i, 0)),
            pl.BlockSpec((None, tm, cols), lambda i, idx: (2, i, 0))]


def _reduced_grad(g, land1, land2, idx, name):
    _, rows, cols = g.shape
    tm = min(rows, 256)

    def body(idx_ref, g_ref, l1_ref, a_ref, b_ref, c_ref, o_ref):
        o_ref[...] = _reduced_block(g_ref, l1_ref, a_ref, b_ref, c_ref)

    return pl.pallas_call(
        body,
        grid_spec=pltpu.PrefetchScalarGridSpec(
            num_scalar_prefetch=1, grid=(rows // tm,), in_specs=_reduced_specs(tm, cols),
            out_specs=pl.BlockSpec((tm, cols), lambda i, idx: (i, 0))),
        out_shape=jax.ShapeDtypeStruct((rows, cols), F32),
        compiler_params=_params(32), name=name,
    )(idx, g, land1, land2, land2, land2)


def _adamw_math(w, g, m, v):
    m = ADAM_B1 * m + (1.0 - ADAM_B1) * g
    v = ADAM_B2 * v + (1.0 - ADAM_B2) * (g * g)
    m_hat = m / (1.0 - ADAM_B1 ** ADAM_STEP)
    v_hat = v / (1.0 - ADAM_B2 ** ADAM_STEP)
    delta = -ADAM_LR * (m_hat / (jnp.sqrt(v_hat) + ADAM_EPS) + ADAM_WD * w)
    return delta, m, v


SMALL_NAMES = ("g_mix_pre", "b_forget", "g_sgu", "b_sgu", "w_spatial", "b_spatial", "g_mix_post", "g_ffn_pre",
               "g_ffn_post")
SMALL_SLOTS = {"g_mix_pre": (0, 1, 0, 1024), "g_mix_post": (1, 1, 0, 1024), "g_ffn_pre": (2, 1, 0, 1024),
               "g_ffn_post": (3, 1, 0, 1024), "g_sgu": (4, 1, 0, 512), "b_sgu": (4, 1, 512, 512),
               "b_forget": (5, 1, 0, 128), "b_spatial": (8, 8, 0, 128)}
SMALL_TILE = (16, 1024)
SPATIAL_TILE = (N_HEADS * CHUNK, CHUNK)


LOSS_SLOT = (8, 8, 128, 128)


def _allreduce_small(grads, loss_part):
    names = list(SMALL_NAMES)

    def body(*refs):
        g = dict(zip(names, refs[:len(names)]))
        loss_ref = refs[len(names)]
        tot_a, tot_b, buf_a, buf_b, sib_a, sib_b, ps_a, ps_b, land_a, land_b, send_sems, recv_sems = refs[len(names) + 1:]
        x, y, c = _mesh_pos()
        buf_a[...] = jnp.zeros(SMALL_TILE, F32)
        r0, nr, c0, nc = LOSS_SLOT
        buf_a[r0:r0 + nr, c0:c0 + nc] = loss_ref[...]
        for name, (r0, nr, c0, nc) in SMALL_SLOTS.items():
            val = g[name][...]
            if name == "b_spatial":
                val = jnp.transpose(val)[0:N_HEADS, :]
            buf_a[r0:r0 + nr, c0:c0 + nc] = val
        buf_b[...] = g["w_spatial"][...].reshape(SPATIAL_TILE)

        def swap(k, src, dst, to):
            return pltpu.make_async_remote_copy(src_ref=src, dst_ref=dst, send_sem=send_sems.at[k],
                                                recv_sem=recv_sems.at[k], device_id=to, device_id_type=MESH)

        first = [swap(0, buf_a, sib_a, (x, y, 1 - c)), swap(1, buf_b, sib_b, (x, y, 1 - c))]
        for cp in first:
            cp.start()
        for cp in first:
            cp.wait_recv()
        ps_a[...] = buf_a[...] + sib_a[...]
        ps_b[...] = buf_b[...] + sib_b[...]
        second = []
        for k, (cx, cy) in enumerate(_other_chips(x, y)):
            second += [swap(2 + 2 * k, ps_a, land_a.at[k], (cx, cy, c)), swap(3 + 2 * k, ps_b, land_b.at[k], (cx, cy, c))]
        for cp in second:
            cp.start()
        for cp in second:
            cp.wait_recv()
        tot_a[...] = (ps_a[...] + land_a[0]) + (land_a[1] + land_a[2])
        tot_b[...] = (ps_b[...] + land_b[0]) + (land_b[1] + land_b[2])
        for cp in first + second:
            cp.wait_send()

    vm = pl.BlockSpec(memory_space=pltpu.VMEM)
    return pl.pallas_call(
        body,
        out_shape=(jax.ShapeDtypeStruct(SMALL_TILE, F32), jax.ShapeDtypeStruct(SPATIAL_TILE, F32)),
        in_specs=[vm] * (len(names) + 1), out_specs=[vm, vm],
        scratch_shapes=[pltpu.VMEM(SMALL_TILE, F32), pltpu.VMEM(SPATIAL_TILE, F32)] * 3
        + [pltpu.VMEM((3,) + SMALL_TILE, F32), pltpu.VMEM((3,) + SPATIAL_TILE, F32),
           pltpu.SemaphoreType.DMA((8,)), pltpu.SemaphoreType.DMA((8,))],
        compiler_params=pltpu.CompilerParams(vmem_limit_bytes=32 * MIB),
        name="allreduce_small",
    )(*[grads[nm] for nm in names], loss_part)


def _adamw_small(tot_a, tot_b, ws, ms, vs):
    names = list(SMALL_NAMES)
    n = len(names)

    def body(a_ref, b_ref, *refs):
        w, m, v = (dict(zip(names, refs[i * n:(i + 1) * n])) for i in range(3))
        outs = [dict(zip(names, refs[(3 + i) * n:(4 + i) * n])) for i in range(4)]
        for name in names:
            if name == "w_spatial":
                g = b_ref[...].reshape(N_HEADS, CHUNK, CHUNK)
            else:
                r0, nr, c0, nc = SMALL_SLOTS[name]
                g = a_ref[r0:r0 + nr, c0:c0 + nc]
            vals = (g,) + _adamw_math(w[name][...], g, m[name][...], v[name][...])
            for out, val in zip(outs, vals):
                out[name][...] = val

    shapes = [jax.ShapeDtypeStruct(ws[nm].shape, F32) for nm in names]
    vm = pl.BlockSpec(memory_space=pltpu.VMEM)
    res = pl.pallas_call(
        body, out_shape=shapes * 4, in_specs=[vm] * (2 + 3 * n), out_specs=[vm] * (4 * n),
        compiler_params=pltpu.CompilerParams(vmem_limit_bytes=32 * MIB), name="adamw_small",
    )(tot_a, tot_b, *[d[nm] for d in (ws, ms, vs) for nm in names])
    return [dict(zip(names, res[i * n:(i + 1) * n])) for i in range(4)]


def _adamw_reduced(w, m, v, g, land1, land2, idx, name):
    rows, cols = w.shape
    tm = min(rows, 256)

    def body(idx_ref, w_ref, m_ref, v_ref, g_ref, l1_ref, a_ref, b_ref, c_ref, go_ref, d_ref, nm_ref, nv_ref):
        gsum = _reduced_block(g_ref, l1_ref, a_ref, b_ref, c_ref)
        go_ref[...] = gsum
        delta, nm, nv = _adamw_math(w_ref[...], gsum, m_ref[...], v_ref[...])
        d_ref[...] = delta
        nm_ref[...] = nm
        nv_ref[...] = nv

    sd = jax.ShapeDtypeStruct((rows, cols), F32)
    spec = pl.BlockSpec((tm, cols), lambda i, idx: (i, 0))
    return pl.pallas_call(
        body,
        grid_spec=pltpu.PrefetchScalarGridSpec(
            num_scalar_prefetch=1, grid=(rows // tm,), in_specs=[spec] * 3 + _reduced_specs(tm, cols),
            out_specs=[spec] * 4),
        out_shape=(sd, sd, sd, sd),
        compiler_params=_params(32), name=name,
    )(idx, w, m, v, g, land1, land2, land2, land2)


def _adamw(w, g, m, v, name):
    rows, cols = w.shape
    tm = 256 if rows % 256 == 0 else rows

    def body(w_ref, g_ref, m_ref, v_ref, d_ref, nm_ref, nv_ref):
        delta, nm, nv = _adamw_math(w_ref[...], g_ref[...], m_ref[...], v_ref[...])
        d_ref[...] = delta
        nm_ref[...] = nm
        nv_ref[...] = nv

    sd = jax.ShapeDtypeStruct((rows, cols), F32)
    spec = _row_spec(tm, cols)
    return pl.pallas_call(
        body, grid=(rows // tm,), out_shape=(sd, sd, sd), in_specs=[spec] * 4, out_specs=[spec] * 3,
        compiler_params=_params(32), name=name,
    )(w, g, m, v)


def _virtual_slab(sources, v0, v_end, like):
    lane = _iota(like.shape, 1)
    out = jnp.zeros(like.shape, like.dtype)
    for v_start, v_stop, read in sources:
        a, b = max(v0, v_start), min(v0 + LANES, v_stop, v_end)
        while a < b:
            c = a - v_start
            n = min(b - a, LANES - c % LANES)
            piece = read(c // LANES)
            shift = (a - v0 - c % LANES) % LANES
            if shift:
                piece = pltpu.roll(piece, shift, 1)
            out = jnp.where((lane >= a - v0) & (lane < a - v0 + n), piece, out)
            a += n
    return out


def _assemble_w_in(wg_in):
    tm = TOKEN_TILE

    def body(src_ref, wz_ref, wf_ref, wg_ref):
        like = src_ref[0, :, 0:LANES]
        sources = [(IN_SHARD * j, IN_SHARD * (j + 1),
                    (lambda k, j=j: src_ref[j, :, LANES * k:LANES * (k + 1)])) for j in range(N_DEV)]
        for k in range(ZQKV_WIDTH // LANES):
            wz_ref[:, LANES * k:LANES * (k + 1)] = _virtual_slab(sources, LANES * k, ZQKV_WIDTH, like)
        wf_ref[...] = _virtual_slab(sources, ZQKV_WIDTH, GATE_OFFSET, like)
        for k in range(2 * D_MODEL // LANES):
            wg_ref[:, LANES * k:LANES * (k + 1)] = _virtual_slab(sources, GATE_OFFSET + LANES * k, IN_WIDTH, like)

    return pl.pallas_call(
        body, grid=(D_MODEL // tm,),
        out_shape=(jax.ShapeDtypeStruct((D_MODEL, ZQKV_WIDTH), BF16), jax.ShapeDtypeStruct((D_MODEL, LANES), BF16),
                   jax.ShapeDtypeStruct((D_MODEL, 2 * D_MODEL), BF16)),
        in_specs=[pl.BlockSpec((N_DEV, tm, IN_SHARD_PAD), lambda i: (0, i, 0))],
        out_specs=[_row_spec(tm, ZQKV_WIDTH), _row_spec(tm, LANES), _row_spec(tm, 2 * D_MODEL)],
        compiler_params=_params(32), name="assemble_w_in",
    )(wg_in)


def _block_dw_in(pieces):
    tm = TOKEN_TILE
    widths = [2 * SGU_WIDTH, ATTN_WIDTH, ATTN_WIDTH, ATTN_WIDTH, N_HEADS, 2 * D_MODEL]
    starts = [sum(widths[:k]) for k in range(len(widths))]

    def body(*refs):
        in_refs, out_ref = refs[:-1], refs[-1]
        like = in_refs[0][:, 0:LANES]
        slab = lambda ref: (lambda k: ref[:, LANES * k:LANES * (k + 1)])
        sources = [(s, s + w, slab(ref)) for s, w, ref in zip(starts, widths, in_refs)]
        for j in range(N_DEV):
            for k in range(IN_SHARD_PAD // LANES):
                out_ref[j, :, LANES * k:LANES * (k + 1)] = _virtual_slab(
                    sources, IN_SHARD * j + LANES * k, IN_SHARD * (j + 1), like)

    return pl.pallas_call(
        body, grid=(D_MODEL // tm,),
        out_shape=jax.ShapeDtypeStruct((N_DEV, D_MODEL, IN_SHARD_PAD), F32),
        in_specs=[_row_spec(tm, pc.shape[1]) for pc in pieces],
        out_specs=pl.BlockSpec((N_DEV, tm, IN_SHARD_PAD), lambda i: (0, i, 0)),
        compiler_params=_params(32), name="block_dw_in",
    )(*pieces)


def _fwd_in(x2, g1, wz, wf, wg):
    T = x2.shape[0]
    tm = TOKEN_TILE

    def body(x_ref, g_ref, wz_ref, wf_ref, wg_ref, xn_ref, zuv_ref, qkv_ref, fl_ref, gt_ref):
        x = x_ref[...]
        r, xh = _rms_stats(x)
        xn = (xh * g_ref[...]).astype(BF16)
        xn_ref[...] = xn
        zuv_ref[...] = _dot(xn, wz_ref[:, 0:1024]).astype(BF16)
        qkv_ref[:, 0:512] = (_dot(xn, wz_ref[:, 1024:1536]) * 0.125).astype(BF16)
        qkv_ref[:, 512:1536] = _dot(xn, wz_ref[:, 1536:2560]).astype(BF16)
        fl_ref[...] = _dot(xn, wf_ref[...])
        gt_ref[...] = jax.nn.sigmoid(_dot(xn, wg_ref[...])).astype(BF16)

    return pl.pallas_call(
        body, grid=(T // tm,),
        out_shape=(jax.ShapeDtypeStruct((T, D_MODEL), BF16), jax.ShapeDtypeStruct((T, 1024), BF16),
                   jax.ShapeDtypeStruct((T, 1536), BF16), jax.ShapeDtypeStruct((T, LANES), F32),
                   jax.ShapeDtypeStruct((T, 2048), BF16)),
        in_specs=[_row_spec(tm, D_MODEL), _const_spec((1, D_MODEL)), _const_spec((D_MODEL, ZQKV_WIDTH)),
                  _const_spec((D_MODEL, LANES)), _const_spec((D_MODEL, 2048))],
        out_specs=[_row_spec(tm, D_MODEL), _row_spec(tm, 1024), _row_spec(tm, 1536), _row_spec(tm, LANES),
                   _row_spec(tm, 2048)],
        compiler_params=_params(48), name="fwd_in",
    )(x2, g1, wz, wf, wg)


def _log_sigmoid(f):
    return jnp.minimum(f, 0.0) - jnp.log1p(jnp.exp(-jnp.abs(f)))


AUG_LANES = 6


def _split3(v):
    hi = v.astype(BF16)
    r1 = v - hi.astype(F32)
    mid = r1.astype(BF16)
    lo = (r1 - mid.astype(F32)).astype(BF16)
    return hi, mid, lo


def _spread(parts, k0):
    r, c = _iota((LANES, LANES), 0), _iota((LANES, LANES), 1)
    out = None
    for i, part in enumerate(parts):
        e = ((c == AUG_LANES * r + (k0 + i)) & (r < N_HEADS)).astype(BF16)
        term = _dot(part, e)
        out = term if out is None else out + term
    return out


def _aug_query(v):
    ones = (_iota(v.shape, 1) < N_HEADS).astype(BF16)
    return (_spread(_split3(v), 0) + _spread((ones, ones, ones), 3)).astype(BF16)


def _aug_key(v):
    ones = (_iota(v.shape, 1) < N_HEADS).astype(BF16)
    return (_spread((ones, ones, ones), 0) - _spread(_split3(v), 3)).astype(BF16)


def _aug_stack(t2, aug, p):
    lane = _iota(t2.shape, 1)
    low = lane < 64
    zero = jnp.zeros_like(t2)
    first = 2 * AUG_LANES * p
    a_e = jnp.where((lane >= first) & (lane < first + AUG_LANES), aug, zero)
    a_o = jnp.where((lane >= first + AUG_LANES) & (lane < first + 2 * AUG_LANES), aug, zero)
    top = jnp.concatenate([jnp.where(low, t2, zero), a_e], axis=1)
    bot = jnp.concatenate([jnp.where(low, zero, t2), a_o], axis=1)
    return jnp.concatenate([top, bot], axis=0)


def _fwd_cum(fl, bfp):
    T = fl.shape[0]
    tb = CUM_TILE

    def body(fl_ref, b_ref, cc_ref, qa_ref, ka_ref):
        tri = (_iota((tb, tb), 0) >= _iota((tb, tb), 1)).astype(F32)
        carry = jnp.zeros((1, LANES), F32)
        for i in range(T // tb):
            rows = slice(i * tb, (i + 1) * tb)
            lf = _log_sigmoid(fl_ref[rows, :] + b_ref[...])
            cs = jnp.dot(tri, lf, precision=HIGHEST, preferred_element_type=F32) + carry
            cc_ref[rows, :] = cs
            carry = cs[tb - 1:tb, :]
            qa_ref[rows, :] = _aug_query(cs)
            ka_ref[rows, :] = _aug_key(cs)

    return pl.pallas_call(
        body,
        out_shape=(jax.ShapeDtypeStruct((T, LANES), F32), jax.ShapeDtypeStruct((T, LANES), BF16),
                   jax.ShapeDtypeStruct((T, LANES), BF16)),
        compiler_params=pltpu.CompilerParams(vmem_limit_bytes=32 * MIB), name="fwd_cum",
    )(fl, bfp)


def _sgu_forward_parts(z, gs, bs):
    u = _gelu(z[:, :SGU_WIDTH])
    vv = _gelu(z[:, SGU_WIDTH:])
    vc = vv - jnp.mean(vv, axis=1, keepdims=True)
    rs = lax.rsqrt(jnp.mean(vc * vc, axis=1, keepdims=True) + EPS)
    vhat = vc * rs
    return u, vhat, rs, vhat * gs + bs


def _sgu_pair_weights(w_ref, bT, p):
    tril = _iota((CHUNK, CHUNK), 0) >= _iota((CHUNK, CHUNK), 1)
    we = jnp.where(tril, w_ref[2 * p], 0.0).astype(BF16)
    wo = jnp.where(tril, w_ref[2 * p + 1], 0.0).astype(BF16)
    lane8 = _iota(bT.shape, 1)
    low = _iota((CHUNK, LANES), 1) < 64
    b2 = jnp.where(low, _pick(bT, lane8, 2 * p), _pick(bT, lane8, 2 * p + 1))
    return we, wo, b2


def _chunks_on_lanes(v, p, nc):
    return jnp.concatenate([v[c * CHUNK:(c + 1) * CHUNK, LANES * p:LANES * (p + 1)] for c in range(nc)], axis=1)


def _sgu_mix(we, wo, b2, vcat, nc):
    low = (_iota((CHUNK, nc * LANES), 1) % LANES) < 64
    return jnp.where(low, _dot(we, vcat), _dot(wo, vcat)) + jnp.concatenate([b2] * nc, axis=1)


def _fwd_sgu(zuv, gs, bs, wsp, bT):
    T = zuv.shape[0]
    tc = SGU_TILE
    nc = tc // CHUNK

    def body(z_ref, gs_ref, bs_ref, w_ref, bT_ref, y_ref):
        u, _, _, vln = _sgu_forward_parts(z_ref[...].astype(F32), gs_ref[...], bs_ref[...])
        vb = vln.astype(BF16)
        for p in range(4):
            we, wo, b2 = _sgu_pair_weights(w_ref, bT_ref[...], p)
            s = _sgu_mix(we, wo, b2, _chunks_on_lanes(vb, p, nc), nc)
            for c in range(nc):
                rows, cols = slice(c * CHUNK, (c + 1) * CHUNK), slice(LANES * p, LANES * (p + 1))
                y_ref[rows, cols] = (u[rows, cols] * s[:, c * LANES:(c + 1) * LANES]).astype(BF16)

    return pl.pallas_call(
        body, grid=(T // tc,), out_shape=jax.ShapeDtypeStruct((T, SGU_WIDTH), BF16),
        in_specs=[_row_spec(tc, 1024), _const_spec((1, SGU_WIDTH)), _const_spec((1, SGU_WIDTH)),
                  _const_spec((8, CHUNK, CHUNK)), _const_spec((CHUNK, 8))],
        out_specs=_row_spec(tc, SGU_WIDTH),
        compiler_params=_params(40), name="fwd_sgu",
    )(zuv, gs, bs, wsp, bT)


def _fwd_attn(qkv, qaug, kaug, w_shards):
    T = qkv.shape[0]
    tq = tk = ATTN_TILE
    nq = T // tq
    gather = _WeightGather([w.shape for w in w_shards], ["cols", "cols", "rows", "cols", "rows"])
    nw = gather.n

    def body(q_ref, qa_ref, k_ref, v_ref, ka_ref, *rest):
        w_refs, (o_ref, lse_ref), wg_refs, scratch = rest[:nw], rest[nw:nw + 2], rest[nw + 2:2 * nw + 2], rest[2 * nw + 2:]
        i = pl.program_id(0)

        @pl.when(i == 0)
        def _():
            gather.start(w_refs, wg_refs, scratch)

        @pl.when(i == nq // 2)
        def _():
            gather.forward(wg_refs, scratch)

        lane = _iota((tq, LANES), 1)
        low = lane < 64
        lowk = _iota((tk, LANES), 1) < 64
        one = jnp.ones((tk, LANES), BF16)
        row = _iota((2 * tq, tk), 0) % tq
        col = _iota((2 * tq, tk), 1)
        cols = [slice(LANES * p, LANES * (p + 1)) for p in range(4)]
        qa = qa_ref[...]
        qs = [_aug_stack(q_ref[:, cols[p]], qa, p) for p in range(4)]

        def step(j, carry, masked):
            ks = pl.ds(pl.multiple_of(j * tk, tk), tk)
            ka = ka_ref[ks, :]
            new = []
            for p in range(4):
                m, acc_e, acc_o = carry[p]
                v2 = v_ref[ks, cols[p]]
                s = _dot_nt(qs[p], jnp.concatenate([k_ref[ks, cols[p]], ka], axis=1))
                if masked:
                    s = jnp.where(col <= row, s, NEG)
                mn = jnp.maximum(m, jnp.max(s, axis=1, keepdims=True))
                al = jnp.exp(m - mn)
                pm = jnp.exp(s - mn).astype(BF16)
                acc_e = al[:tq] * acc_e + _dot(pm[:tq], jnp.where(lowk, v2, one))
                acc_o = al[tq:] * acc_o + _dot(pm[tq:], jnp.where(lowk, one, v2))
                new.append((mn, acc_e, acc_o))
            return tuple(new)

        init = tuple((jnp.full((2 * tq, 1), NEG, F32), jnp.zeros((tq, LANES), F32), jnp.zeros((tq, LANES), F32))
                     for _ in range(4))
        carry = lax.fori_loop(0, i // 2, lambda t, c: step(2 * t + 1, step(2 * t, c, False), False), init)
        carry = lax.cond(i % 2 == 1, lambda c: step(i - 1, c, False), lambda c: c, carry)
        carry = step(i, carry, True)
        lse_blk = jnp.zeros((tq, LANES), F32)
        for p in range(4):
            m, acc_e, acc_o = carry[p]
            l_e = pltpu.roll(acc_e, 64, 1)
            l_o = pltpu.roll(acc_o, 64, 1)
            o_ref[:, cols[p]] = jnp.where(low, acc_e / l_e, acc_o / l_o).astype(BF16)
            lse_blk = jnp.where(lane == 2 * p, m[:tq] + jnp.log(l_e), lse_blk)
            lse_blk = jnp.where(lane == 2 * p + 1, m[tq:] + jnp.log(acc_o), lse_blk)
        lse_ref[...] = lse_blk

        @pl.when(i == nq - 1)
        def _():
            gather.finish(wg_refs, scratch)

    return pl.pallas_call(
        body, grid=(nq,),
        out_shape=[jax.ShapeDtypeStruct((T, ATTN_WIDTH), BF16), jax.ShapeDtypeStruct((T, LANES), F32)]
        + gather.out_shapes(),
        in_specs=[_row_spec(tq, 512), _row_spec(tq, LANES),
                  pl.BlockSpec((T, 512), lambda i: (0, 1), pipeline_mode=pl.Buffered(1)),
                  pl.BlockSpec((T, 512), lambda i: (0, 2), pipeline_mode=pl.Buffered(1)),
                  _const_spec((T, LANES))] + [_const_spec(w.shape) for w in w_shards],
        out_specs=[_row_spec(tq, ATTN_WIDTH), _row_spec(tq, LANES)] + [pl.BlockSpec(memory_space=pl.ANY)] * nw,
        scratch_shapes=gather.scratch_shapes(),
        compiler_params=_params(58), name="fwd_attn",
    )(qkv, qaug, qkv, qkv, kaug, *w_shards)


def _fwd_merge(ys, ya, gt, x2, wbs, wba, wo, g2):
    T = x2.shape[0]
    tm = TOKEN_TILE

    def body(ys_ref, ya_ref, gt_ref, x_ref, wbs_ref, wba_ref, wo_ref, g2_ref, a_ref, b_ref, mg_ref, o_ref, h1_ref):
        A = _dot(ys_ref[...], wbs_ref[...])
        B = _dot(ya_ref[...], wba_ref[...])
        mg = (gt_ref[:, :D_MODEL].astype(F32) * A + gt_ref[:, D_MODEL:].astype(F32) * B).astype(BF16)
        o = _dot(mg, wo_ref[...])
        r2, oh = _rms_stats(o)
        a_ref[...] = A.astype(BF16)
        b_ref[...] = B.astype(BF16)
        mg_ref[...] = mg
        o_ref[...] = o.astype(BF16)
        h1_ref[...] = x_ref[...] + oh * g2_ref[...]

    sd = jax.ShapeDtypeStruct((T, D_MODEL), BF16)
    return pl.pallas_call(
        body, grid=(T // tm,),
        out_shape=(sd, sd, sd, sd, jax.ShapeDtypeStruct((T, D_MODEL), F32)),
        in_specs=[_row_spec(tm, 512), _row_spec(tm, 512), _row_spec(tm, 2048), _row_spec(tm, D_MODEL),
                  _const_spec((512, D_MODEL)), _const_spec((512, D_MODEL)), _const_spec((D_MODEL, D_MODEL)),
                  _const_spec((1, D_MODEL))],
        out_specs=[_row_spec(tm, D_MODEL)] * 5,
        compiler_params=_params(40), name="fwd_merge",
    )(ys, ya, gt, x2, wbs, wba, wo, g2)


def _fwd_ffn_loss(h1, tgt, wup, wdn, g3, g4):
    T = h1.shape[0]
    tm = TOKEN_TILE
    nsteps = T // tm

    def body(h1_ref, tg_ref, wup_ref, wdn_ref, g3_ref, g4_ref, xn2_ref, a_ref, ddn_ref, dy_ref, loss_ref,
             dg4_ref, acc_l, acc_g):
        i = pl.program_id(0)

        @pl.when(i == 0)
        def _():
            acc_l[...] = jnp.zeros_like(acc_l)
            acc_g[...] = jnp.zeros_like(acc_g)

        h1v = h1_ref[...]
        r3, h1h = _rms_stats(h1v)
        xn2 = (h1h * g3_ref[...]).astype(BF16)
        xn2_ref[...] = xn2
        dn = jnp.zeros((tm, D_MODEL), F32)
        for j in range(D_FF // 1024):
            cols = slice(1024 * j, 1024 * (j + 1))
            a = _dot(xn2, wup_ref[:, cols])
            a_ref[:, cols] = a.astype(BF16)
            hid = jnp.square(jnp.maximum(a, 0.0)).astype(BF16)
            dn = dn + _dot(hid, wdn_ref[cols, :])
        r4, dnh = _rms_stats(dn)
        g4v = g4_ref[...]
        e = (h1v + dnh * g4v) - tg_ref[...]
        sq = e * e
        s1 = sq[:, 0:LANES]
        for j in range(1, D_MODEL // LANES):
            s1 = s1 + sq[:, LANES * j:LANES * (j + 1)]
        acc_l[...] += _fold8(s1)
        dy = e * (1.0 / D_MODEL)
        dy_ref[...] = dy
        acc_g[...] += _fold8(dy * dnh)
        ddn_ref[...] = _rms_bwd(dy, dnh, r4, g4v).astype(BF16)

        @pl.when(i == nsteps - 1)
        def _():
            loss_ref[...] = acc_l[...] * (0.5 / D_MODEL)
            dg4_ref[...] = jnp.sum(acc_g[...], axis=0, keepdims=True)

    return pl.pallas_call(
        body, grid=(nsteps,),
        out_shape=(jax.ShapeDtypeStruct((T, D_MODEL), BF16), jax.ShapeDtypeStruct((T, D_FF), BF16),
                   jax.ShapeDtypeStruct((T, D_MODEL), BF16), jax.ShapeDtypeStruct((T, D_MODEL), F32),
                   jax.ShapeDtypeStruct((SUBLANES, LANES), F32), jax.ShapeDtypeStruct((1, D_MODEL), F32)),
        in_specs=[_row_spec(tm, D_MODEL), _row_spec(tm, D_MODEL), _const_spec((D_MODEL, D_FF)),
                  _const_spec((D_FF, D_MODEL)), _const_spec((1, D_MODEL)), _const_spec((1, D_MODEL))],
        out_specs=[_row_spec(tm, D_MODEL), _row_spec(tm, D_FF), _row_spec(tm, D_MODEL), _row_spec(tm, D_MODEL),
                   pl.BlockSpec((SUBLANES, LANES), lambda i: (0, 0)), pl.BlockSpec((1, D_MODEL), lambda i: (0, 0))],
        scratch_shapes=[pltpu.VMEM((SUBLANES, LANES), F32), pltpu.VMEM((SUBLANES, D_MODEL), F32)],
        compiler_params=_params(52), name="fwd_ffn_loss",
    )(h1, tgt, wup, wdn, g3, g4)


def _bwd_ffn(ddn, a, dy, h1, wup, wdn, g3):
    T = h1.shape[0]
    tm = TOKEN_TILE
    nsteps = T // tm

    def body(ddn_ref, a_ref, dy_ref, h1_ref, wup_ref, wdn_ref, g3_ref, da_ref, dh1_ref, dg3_ref, acc_g):
        i = pl.program_id(0)

        @pl.when(i == 0)
        def _():
            acc_g[...] = jnp.zeros_like(acc_g)

        ddnv = ddn_ref[...]
        dxn2 = jnp.zeros((tm, D_MODEL), F32)
        for j in range(D_FF // 1024):
            cols = slice(1024 * j, 1024 * (j + 1))
            dhid = _dot_nt(ddnv, wdn_ref[cols, :])
            da = (dhid * (2.0 * jnp.maximum(a_ref[:, cols].astype(F32), 0.0))).astype(BF16)
            da_ref[:, cols] = da
            dxn2 = dxn2 + _dot_nt(da, wup_ref[:, cols])
        r3, h1h = _rms_stats(h1_ref[...])
        acc_g[...] += _fold8(dxn2 * h1h)
        dh1_ref[...] = dy_ref[...] + _rms_bwd(dxn2, h1h, r3, g3_ref[...])

        @pl.when(i == nsteps - 1)
        def _():
            dg3_ref[...] = jnp.sum(acc_g[...], axis=0, keepdims=True)

    return pl.pallas_call(
        body, grid=(nsteps,),
        out_shape=(jax.ShapeDtypeStruct((T, D_FF), BF16), jax.ShapeDtypeStruct((T, D_MODEL), F32),
                   jax.ShapeDtypeStruct((1, D_MODEL), F32)),
        in_specs=[_row_spec(tm, D_MODEL), _row_spec(tm, D_FF), _row_spec(tm, D_MODEL), _row_spec(tm, D_MODEL),
                  _const_spec((D_MODEL, D_FF)), _const_spec((D_FF, D_MODEL)), _const_spec((1, D_MODEL))],
        out_specs=[_row_spec(tm, D_FF), _row_spec(tm, D_MODEL), pl.BlockSpec((1, D_MODEL), lambda i: (0, 0))],
        scratch_shapes=[pltpu.VMEM((SUBLANES, D_MODEL), F32)],
        compiler_params=_params(52), name="bwd_ffn",
    )(ddn, a, dy, h1, wup, wdn, g3)


def _wgrad(xa, dy, name, relu2=False, tn=None, block_cols=None):
    T, K = xa.shape
    N = dy.shape[1]
    tn = N if tn is None else tn
    tt = min(T, WGRAD_TILE if K <= D_MODEL else WGRAD_TILE // 2)
    if block_cols:
        nb = tn // block_cols
        out_shape = jax.ShapeDtypeStruct((N // block_cols, K, block_cols), F32)
        out_spec = pl.BlockSpec((nb, K, block_cols), lambda n, t: (n, 0, 0))
    else:
        out_shape = jax.ShapeDtypeStruct((K, N), F32)
        out_spec = pl.BlockSpec((K, tn), lambda n, t: (0, n))

    def body(x_ref, dy_ref, o_ref):
        @pl.when(pl.program_id(1) == 0)
        def _():
            o_ref[...] = jnp.zeros_like(o_ref)

        xv = x_ref[...]
        if relu2:
            xv = jnp.square(jnp.maximum(xv.astype(F32), 0.0)).astype(BF16)
        if block_cols:
            for b in range(nb):
                o_ref[b] += _dot_tn(xv, dy_ref[:, block_cols * b:block_cols * (b + 1)])
        else:
            o_ref[...] += _dot_tn(xv, dy_ref[...])

    return pl.pallas_call(
        body, grid=(N // tn, T // tt), out_shape=out_shape,
        in_specs=[pl.BlockSpec((tt, K), lambda n, t: (t, 0)), pl.BlockSpec((tt, tn), lambda n, t: (t, n))],
        out_specs=out_spec,
        compiler_params=_params(52, 2), name=name,
    )(xa, dy)


def _wgrad_multi(xa, dys, name):
    T, K = xa.shape
    tt = min(T, WGRAD_TILE)
    n = len(dys)

    def body(x_ref, *refs):
        dy_refs, o_refs = refs[:n], refs[n:]

        @pl.when(pl.program_id(0) == 0)
        def _():
            for o_ref in o_refs:
                o_ref[...] = jnp.zeros_like(o_ref)

        xv = x_ref[...]
        for dy_ref, o_ref in zip(dy_refs, o_refs):
            o_ref[...] += _dot_tn(xv, dy_ref[...])

    return pl.pallas_call(
        body, grid=(T // tt,),
        out_shape=[jax.ShapeDtypeStruct((K, dy.shape[1]), F32) for dy in dys],
        in_specs=[_row_spec(tt, K)] + [_row_spec(tt, dy.shape[1]) for dy in dys],
        out_specs=[pl.BlockSpec((K, dy.shape[1]), lambda t: (0, 0)) for dy in dys],
        compiler_params=_params(52), name=name,
    )(xa, *dys)


def _bwd_merge(dh1, o, A, B, gt, ys, ya, mg, lse, cc, wbs, wba, wo, g2):
    T = dh1.shape[0]
    tm = MERGE_BWD_TILE
    nsteps = T // tm

    def body(dh1_ref, o_ref, a_ref, b_ref, gt_ref, ys_ref, ya_ref, mg_ref, lse_ref, cc_ref, wbs_ref, wba_ref,
             wo_ref, g2_ref, dgl_ref, dys_ref, dya_ref, qab_ref, dab_ref, dg2_ref, dwbs_ref, dwba_ref, dwo_ref,
             acc_g):
        i = pl.program_id(0)

        @pl.when(i == 0)
        def _():
            acc_g[...] = jnp.zeros_like(acc_g)
            dwbs_ref[...] = jnp.zeros_like(dwbs_ref)
            dwba_ref[...] = jnp.zeros_like(dwba_ref)
            dwo_ref[...] = jnp.zeros_like(dwo_ref)

        dh1v = dh1_ref[...]
        r2, oh = _rms_stats(o_ref[...].astype(F32))
        acc_g[...] += _fold8(dh1v * oh)
        do = _rms_bwd(dh1v, oh, r2, g2_ref[...]).astype(BF16)
        dwo_ref[...] += _dot_tn(mg_ref[...], do)
        dmg = _dot_nt(do, wo_ref[...])
        ga = gt_ref[:, :D_MODEL].astype(F32)
        gb = gt_ref[:, D_MODEL:].astype(F32)
        dgl_ref[:, :D_MODEL] = (dmg * a_ref[...].astype(F32) * ga * (1.0 - ga)).astype(BF16)
        dgl_ref[:, D_MODEL:] = (dmg * b_ref[...].astype(F32) * gb * (1.0 - gb)).astype(BF16)
        dA = (dmg * ga).astype(BF16)
        dB = (dmg * gb).astype(BF16)
        dwbs_ref[...] += _dot_tn(ys_ref[...], dA)
        dwba_ref[...] += _dot_tn(ya_ref[...], dB)
        dys_ref[...] = _dot_nt(dA, wbs_ref[...]).astype(BF16)
        dya = _dot_nt(dB, wba_ref[...]).astype(BF16)
        dya_ref[...] = dya
        prod = dya.astype(F32) * ya_ref[...].astype(F32)
        lane = _iota((tm, LANES), 1)
        low = lane < 64
        blk = jnp.zeros((tm, LANES), F32)
        for p in range(4):
            pp = prod[:, LANES * p:LANES * (p + 1)]
            blk = jnp.where(lane == 2 * p, jnp.sum(jnp.where(low, pp, 0.0), axis=1, keepdims=True), blk)
            blk = jnp.where(lane == 2 * p + 1, jnp.sum(jnp.where(low, 0.0, pp), axis=1, keepdims=True), blk)
        qab_ref[...] = _aug_query(cc_ref[...] - lse_ref[...])
        dab_ref[...] = _spread(_split3(-blk), 0).astype(BF16)

        @pl.when(i == nsteps - 1)
        def _():
            dg2_ref[...] = jnp.sum(acc_g[...], axis=0, keepdims=True)

    sh = jax.ShapeDtypeStruct((T, 512), BF16)
    sa = jax.ShapeDtypeStruct((T, LANES), BF16)
    sw = jax.ShapeDtypeStruct((512, D_MODEL), F32)
    whole = lambda shape: pl.BlockSpec(shape, lambda i: (0, 0))
    return pl.pallas_call(
        body, grid=(nsteps,),
        out_shape=(jax.ShapeDtypeStruct((T, 2048), BF16), sh, sh, sa, sa, jax.ShapeDtypeStruct((1, D_MODEL), F32),
                   sw, sw, jax.ShapeDtypeStruct((D_MODEL, D_MODEL), F32)),
        in_specs=[_row_spec(tm, D_MODEL)] * 4 + [_row_spec(tm, 2048), _row_spec(tm, 512), _row_spec(tm, 512),
                  _row_spec(tm, D_MODEL), _row_spec(tm, LANES), _row_spec(tm, LANES),
                  _const_spec((512, D_MODEL)), _const_spec((512, D_MODEL)),
                  _const_spec((D_MODEL, D_MODEL)), _const_spec((1, D_MODEL))],
        out_specs=[_row_spec(tm, 2048), _row_spec(tm, 512), _row_spec(tm, 512), _row_spec(tm, LANES),
                   _row_spec(tm, LANES), whole((1, D_MODEL)), whole((512, D_MODEL)), whole((512, D_MODEL)),
                   whole((D_MODEL, D_MODEL))],
        scratch_shapes=[pltpu.VMEM((SUBLANES, D_MODEL), F32)],
        compiler_params=_params(56), name="bwd_merge",
    )(dh1, o, A, B, gt, ys, ya, mg, lse, cc, wbs, wba, wo, g2)


def _bwd_sgu(zuv, dys, gs, bs, wsp, bT, grads):
    T = zuv.shape[0]
    tc = SGU_TILE
    nc = tc // CHUNK
    nsteps = T // tc
    ex = _GradExchange([tuple(g.shape[1:]) for g in grads])
    ng = ex.n

    def body(z_ref, dy_ref, gs_ref, bs_ref, w_ref, bT_ref, *rest):
        g_refs, (dz_ref, dw_ref, dbT_ref, dgs_ref, dbs_ref) = rest[:ng], rest[ng:ng + 5]
        land1 = rest[ng + 5:2 * ng + 5]
        acc_w, acc_b, acc_gs, acc_bs, dvln_s = rest[2 * ng + 5:2 * ng + 10]
        ex_sems = rest[2 * ng + 10:]
        i = pl.program_id(0)

        @pl.when(i == 0)
        def _():
            ex.start(1, g_refs, land1, ex_sems)
            acc_w[...] = jnp.zeros_like(acc_w)
            acc_b[...] = jnp.zeros_like(acc_b)
            acc_gs[...] = jnp.zeros_like(acc_gs)
            acc_bs[...] = jnp.zeros_like(acc_bs)

        z = z_ref[...].astype(F32)
        gsv = gs_ref[...]
        u, vhat, rs, vln = _sgu_forward_parts(z, gsv, bs_ref[...])
        vb = vln.astype(BF16)
        dy = dy_ref[...].astype(F32)
        low_w = (_iota((CHUNK, nc * LANES), 1) % LANES) < 64
        for p in range(4):
            we, wo, b2 = _sgu_pair_weights(w_ref, bT_ref[...], p)
            vcat = _chunks_on_lanes(vb, p, nc)
            s = _sgu_mix(we, wo, b2, vcat, nc)
            dyc = _chunks_on_lanes(dy, p, nc)
            ds = dyc * _chunks_on_lanes(u, p, nc)
            dsb = ds.astype(BF16)
            zero = jnp.zeros_like(dsb)
            dse = jnp.where(low_w, dsb, zero)
            dso = jnp.where(low_w, zero, dsb)
            acc_w[2 * p] += _dot_nt(dse, vcat)
            acc_w[2 * p + 1] += _dot_nt(dso, vcat)
            acc_b[p] += ds
            dvl = jnp.where(low_w, _dot_tn(we, dsb), _dot_tn(wo, dsb))
            for c in range(nc):
                rows, cols = slice(c * CHUNK, (c + 1) * CHUNK), slice(LANES * p, LANES * (p + 1))
                dvln_s[rows, cols] = dvl[:, c * LANES:(c + 1) * LANES]
                du = dy[rows, cols] * s[:, c * LANES:(c + 1) * LANES]
                dz_ref[rows, cols] = (du * _gelu_grad(z[rows, cols])).astype(BF16)
        dvln = dvln_s[...]
        acc_gs[...] += _fold8(dvln * vhat)
        acc_bs[...] += _fold8(dvln)
        al = dvln * gsv
        dvv = rs * (al - jnp.mean(al, axis=1, keepdims=True) - vhat * jnp.mean(al * vhat, axis=1, keepdims=True))
        dz_ref[:, SGU_WIDTH:] = (dvv * _gelu_grad(z[:, SGU_WIDTH:])).astype(BF16)

        @pl.when(i == nsteps - 1)
        def _():
            tril = _iota((CHUNK, CHUNK), 0) >= _iota((CHUNK, CHUNK), 1)
            lane = _iota((CHUNK, LANES), 1)
            low = lane < 64
            blk = jnp.zeros((CHUNK, LANES), F32)
            for g in range(8):
                dw_ref[g] = jnp.where(tril, acc_w[g], 0.0)
            for p in range(4):
                t = acc_b[p]
                tot = t[:, 0:LANES]
                for c in range(1, nc):
                    tot = tot + t[:, c * LANES:(c + 1) * LANES]
                blk = jnp.where(lane == 2 * p, jnp.sum(jnp.where(low, tot, 0.0), axis=1, keepdims=True), blk)
                blk = jnp.where(lane == 2 * p + 1, jnp.sum(jnp.where(low, 0.0, tot), axis=1, keepdims=True), blk)
            dbT_ref[...] = blk
            dgs_ref[...] = jnp.sum(acc_gs[...], axis=0, keepdims=True)
            dbs_ref[...] = jnp.sum(acc_bs[...], axis=0, keepdims=True)
            ex.wait(1, g_refs, land1, ex_sems)

    whole = lambda shape: pl.BlockSpec(shape, lambda i: (0,) * len(shape))
    hbm_spec = pl.BlockSpec(memory_space=pl.ANY)
    return pl.pallas_call(
        body, grid=(nsteps,),
        out_shape=[jax.ShapeDtypeStruct((T, 1024), BF16), jax.ShapeDtypeStruct((8, CHUNK, CHUNK), F32),
                   jax.ShapeDtypeStruct((CHUNK, LANES), F32), jax.ShapeDtypeStruct((1, SGU_WIDTH), F32),
                   jax.ShapeDtypeStruct((1, SGU_WIDTH), F32)] + ex.land_shapes(1),
        in_specs=[_row_spec(tc, 1024), _row_spec(tc, SGU_WIDTH), _const_spec((1, SGU_WIDTH)),
                  _const_spec((1, SGU_WIDTH)), _const_spec((8, CHUNK, CHUNK)), _const_spec((CHUNK, 8))]
        + [hbm_spec] * ng,
        out_specs=[_row_spec(tc, 1024), whole((8, CHUNK, CHUNK)), whole((CHUNK, LANES)),
                   whole((1, SGU_WIDTH)), whole((1, SGU_WIDTH))] + [hbm_spec] * ng,
        scratch_shapes=[pltpu.VMEM((8, CHUNK, CHUNK), F32), pltpu.VMEM((4, CHUNK, nc * LANES), F32),
                        pltpu.VMEM((SUBLANES, SGU_WIDTH), F32), pltpu.VMEM((SUBLANES, SGU_WIDTH), F32),
                        pltpu.VMEM((tc, SGU_WIDTH), F32)] + ex.sem_shapes(1),
        compiler_params=_params(48), name="bwd_sgu",
    )(zuv, dys, gs, bs, wsp, bT, *grads)


def _bwd_attn(qkv, dya, qab, dab, kaug, parts):
    T = qkv.shape[0]
    tq = tk = ATTN_TILE
    nq = T // tq
    nk = T // tk
    ex = _GradExchange([tuple(g.shape[1:]) for g in parts])
    nr = ex.n

    def body(q_ref, do_ref, qa_ref, da_ref, k_ref, v_ref, ka_ref, *rest):
        part_refs, (dq_ref, dk_ref, dv_ref, dcx_ref) = rest[:nr], rest[nr:nr + 4]
        land2, dq_acc, ex_sems = rest[nr + 4:2 * nr + 4], rest[2 * nr + 4], rest[2 * nr + 5:]
        p = pl.program_id(0)
        j = pl.program_id(1)

        @pl.when((p == 0) & (j == 0))
        def _():
            ex.start(2, part_refs, land2, ex_sems)

        lane = _iota((tq, LANES), 1)
        low = lane < 64
        row = _iota((2 * tq, tk), 0) % tq
        col = _iota((2 * tq, tk), 1)
        first = 2 * AUG_LANES * p
        half = tq // 2

        @pl.when(j == 0)
        def _():
            dq_acc[...] = jnp.zeros_like(dq_acc)

        @pl.when((j == 0) & (p == 0))
        def _():
            dcx_ref[...] = jnp.zeros_like(dcx_ref)

        ka = ka_ref[...]
        kk = jnp.concatenate([k_ref[...], ka], axis=1)
        vv = jnp.concatenate([v_ref[...], ka], axis=1)

        def tile(i, carry, r0, c0, nc, masked):
            dk_a, dv_a = carry
            nr = tq - r0
            qsl = pl.ds(pl.multiple_of(i * tq + r0, half), nr)
            qs = _aug_stack(q_ref[qsl, :], qa_ref[qsl, :], p)
            dos = _aug_stack(do_ref[qsl, :], da_ref[qsl, :], p)
            kc, vc = kk[c0:c0 + nc], vv[c0:c0 + nc]
            s = _dot_nt(qs, kc)
            if masked:
                s = jnp.where(col[:2 * nr, :nc] + c0 <= row[:2 * nr, :nc] % nr + r0, s, NEG)
            pm = jnp.exp(s)
            ds = pm * _dot_nt(dos, vc)
            dsb = ds.astype(BF16)
            dv_u = _dot_tn(pm.astype(BF16), dos[:, :LANES])
            dk_u = _dot_tn(dsb, qs)
            if nc == tk:
                dv_a, dk_a = dv_a + dv_u, dk_a + dk_u
            else:
                pad = lambda u: jnp.concatenate(
                    [jnp.zeros((n, u.shape[1]), F32) if z else u
                     for z, n in ((True, c0), (False, nc), (True, tk - c0 - nc)) if n], axis=0)
                dv_a, dk_a = dv_a + pad(dv_u), dk_a + pad(dk_u)
            dqx = _dot(dsb, kc)
            dq_acc[qsl, :] += jnp.where(low[:nr], dqx[:nr, :LANES], dqx[nr:, :LANES])
            dcx_ref[qsl, :] += (jnp.where(lane[:nr] == first, dqx[:nr, LANES:], 0.0)
                                + jnp.where(lane[:nr] == first + AUG_LANES, dqx[nr:, LANES:], 0.0))
            return dk_a, dv_a

        def q_block(i, carry, masked):
            return tile(i, carry, 0, 0, tk, masked)

        init =(jnp.zeros((tk, 2 * LANES), F32), jnp.zeros((tk, LANES), F32))
        carry = tile(j, init, 0, 0, half, True)
        carry = tile(j, carry, half, half, half, True)
        n_rest = nq - 1 - j

        def trip(t, c):
            for u in range(3):
                c = q_block(j + 1 + 3 * t + u, c, False)
            return c

        carry = lax.fori_loop(0, n_rest // 3, trip, carry)
        dk_a, dv_a = lax.fori_loop(j + 1 + 3 * (n_rest // 3), nq, lambda i, c: q_block(i, c, False), carry)
        dk_ref[...] = dk_a[:, :LANES].astype(BF16)
        dv_ref[...] = dv_a.astype(BF16)
        ksl = pl.ds(pl.multiple_of(j * tk, tk), tk)
        lk = _iota((tk, LANES), 1)
        dcx_ref[ksl, :] += jnp.where((lk == first + 3) | (lk == first + AUG_LANES + 3), dk_a[:, LANES:], 0.0)

        @pl.when(j == nk - 1)
        def _():
            dq_ref[...] = (dq_acc[...] * 0.125).astype(BF16)

        @pl.when((p == 3) & (j == nk - 1))
        def _():
            ex.wait(2, part_refs, land2, ex_sems)

    sh = jax.ShapeDtypeStruct((T, ATTN_WIDTH), BF16)
    full = lambda cb: pl.BlockSpec((T, LANES), lambda p, j: (0, cb + p))
    blk = lambda cb: pl.BlockSpec((tk, LANES), lambda p, j: (j, cb + p))
    hbm_spec = pl.BlockSpec(memory_space=pl.ANY)
    return pl.pallas_call(
        body, grid=(4, nk),
        out_shape=[sh, sh, sh, jax.ShapeDtypeStruct((T, LANES), F32)] + ex.land_shapes(2),
        in_specs=[full(0), full(0), _const_spec((T, LANES)), _const_spec((T, LANES)), blk(4), blk(8),
                  pl.BlockSpec((tk, LANES), lambda p, j: (j, 0))] + [hbm_spec] * nr,
        out_specs=[full(0), blk(0), blk(0), pl.BlockSpec((T, LANES), lambda p, j: (0, 0))] + [hbm_spec] * nr,
        scratch_shapes=[pltpu.VMEM((T, LANES), F32)] + ex.sem_shapes(2),
        compiler_params=_params(58, 2), name="bwd_attn",
    )(qkv, dya, qab, dab, qkv, qkv, kaug, *parts)


def _bwd_cum(dcx, fl, bfp):
    T = fl.shape[0]
    tb = CUM_TILE

    def body(dcx_ref, fl_ref, b_ref, dfl_ref, dbf_ref):
        triu = (_iota((tb, tb), 0) <= _iota((tb, tb), 1)).astype(F32)
        r, c = _iota((LANES, LANES), 0), _iota((LANES, LANES), 1)
        sel = (((r == AUG_LANES * c) & (c < N_HEADS)).astype(F32)
               - ((r == AUG_LANES * c + 3) & (c < N_HEADS)).astype(F32))
        carry = jnp.zeros((1, LANES), F32)
        dbf = jnp.zeros((1, LANES), F32)
        for i in reversed(range(T // tb)):
            colblk = jnp.dot(dcx_ref[i * tb:(i + 1) * tb, :], sel, precision=HIGHEST, preferred_element_type=F32)
            rc = jnp.dot(triu, colblk, precision=HIGHEST, preferred_element_type=F32) + carry
            carry = rc[0:1, :]
            sig = jax.nn.sigmoid(fl_ref[i * tb:(i + 1) * tb, :] + b_ref[...])
            dfl = rc * (1.0 - sig)
            dfl_ref[i * tb:(i + 1) * tb, :] = dfl.astype(BF16)
            dbf = dbf + jnp.sum(dfl, axis=0, keepdims=True)
        dbf_ref[...] = dbf

    return pl.pallas_call(
        body,
        out_shape=(jax.ShapeDtypeStruct((T, LANES), BF16), jax.ShapeDtypeStruct((1, LANES), F32)),
        compiler_params=pltpu.CompilerParams(vmem_limit_bytes=32 * MIB), name="bwd_cum",
    )(dcx, fl, bfp)


def _bwd_in(dz, dq, dk, dv, dfl, dgl, dh1, x2, g1, wz, wf, wg, rows, name, prev=None, stage=0, exchanged=None):
    T = x2.shape[0]
    tm = TOKEN_TILE
    first = rows[0] // tm
    nsteps = (rows[1] - rows[0]) // tm
    ex = _GradExchange([tuple(exchanged.shape[1:])]) if stage else None

    def body(dz_ref, dq_ref, dk_ref, dv_ref, dfl_ref, dgl_ref, dh1_ref, x_ref, g_ref, wz_ref, wf_ref, wg_ref, *rest):
        rest = list(rest)
        dx_prev, dg1_prev = (rest.pop(0), rest.pop(0)) if prev else (None, None)
        src_ref = rest.pop(0) if stage else None
        dx_ref, dg1_ref = rest.pop(0), rest.pop(0)
        land_ref = rest.pop(0) if stage else None
        acc_g, ex_sems = rest[0], rest[1:]
        i = pl.program_id(0)

        @pl.when(i == 0)
        def _():
            if stage:
                ex.start(stage, [src_ref], [land_ref], ex_sems)
            acc_g[...] = jnp.zeros_like(acc_g)

        dxn = _dot_nt(dz_ref[...], wz_ref[:, 0:1024])
        dxn = dxn + _dot_nt(dq_ref[...], wz_ref[:, 1024:1536])
        dxn = dxn + _dot_nt(dk_ref[...], wz_ref[:, 1536:2048])
        dxn = dxn + _dot_nt(dv_ref[...], wz_ref[:, 2048:2560])
        dxn = dxn + _dot_nt(dfl_ref[...], wf_ref[...])
        dxn = dxn + _dot_nt(dgl_ref[...], wg_ref[...])
        r1, xh = _rms_stats(x_ref[...])
        acc_g[...] += _fold8(dxn * xh)
        dx_ref[...] = dh1_ref[...] + _rms_bwd(dxn, xh, r1, g_ref[...])

        @pl.when(i == nsteps - 1)
        def _():
            total = jnp.sum(acc_g[...], axis=0, keepdims=True)
            dg1_ref[...] = total + dg1_prev[...] if prev else total
            if stage:
                ex.wait(stage, [src_ref], [land_ref], ex_sems)

    hbm_spec = pl.BlockSpec(memory_space=pl.ANY)
    rows_spec = lambda n: pl.BlockSpec((tm, n), lambda i: (i + first, 0))
    operands = [dz, dq, dk, dv, dfl, dgl, dh1, x2, g1, wz, wf, wg]
    in_specs = [rows_spec(1024), rows_spec(512), rows_spec(512), rows_spec(512), rows_spec(LANES), rows_spec(2048),
                rows_spec(D_MODEL), rows_spec(D_MODEL),
                _const_spec((1, D_MODEL)), _const_spec((D_MODEL, ZQKV_WIDTH)), _const_spec((D_MODEL, LANES)),
                _const_spec((D_MODEL, 2048))]
    aliases = {}
    if prev:
        aliases = {len(operands): 0}
        operands += list(prev)
        in_specs += [hbm_spec, _const_spec((1, D_MODEL))]
    if stage:
        operands.append(exchanged)
        in_specs.append(hbm_spec)
    return pl.pallas_call(
        body, grid=(nsteps,),
        out_shape=[jax.ShapeDtypeStruct((T, D_MODEL), F32), jax.ShapeDtypeStruct((1, D_MODEL), F32)]
        + (ex.land_shapes(stage) if stage else []),
        in_specs=in_specs,
        out_specs=[rows_spec(D_MODEL), pl.BlockSpec((1, D_MODEL), lambda i: (0, 0))] + ([hbm_spec] if stage else []),
        scratch_shapes=[pltpu.VMEM((SUBLANES, D_MODEL), F32)] + (ex.sem_shapes(stage) if stage else []),
        input_output_aliases=aliases,
        compiler_params=_params(48), name=name,
    )(*operands)


def _small_kernel_shapes(g_mix_pre, b_forget, g_sgu, b_sgu, w_spatial, b_spatial, g_mix_post, g_ffn_pre, g_ffn_post):
    return dict(g_mix_pre=g_mix_pre, b_forget=jnp.pad(b_forget, ((0, 0), (0, LANES - N_HEADS))), g_sgu=g_sgu,
                b_sgu=b_sgu, w_spatial=w_spatial[0], b_spatial=b_spatial[0], g_mix_post=g_mix_post,
                g_ffn_pre=g_ffn_pre, g_ffn_post=g_ffn_post)


def _small_output_shapes(d):
    out = dict(d)
    out.update(b_forget=d["b_forget"][:, :N_HEADS], w_spatial=d["w_spatial"][None], b_spatial=d["b_spatial"][None])
    return out


def kernel(x, g_mix_pre, w_in, b_forget, g_sgu, b_sgu, w_spatial, b_spatial, w_branch_sgu, w_branch_attn, w_out, g_mix_post, g_ffn_pre, w_up, w_down, g_ffn_post, loss_target, m_g_mix_pre, m_w_in, m_b_forget, m_g_sgu, m_b_sgu, m_w_spatial, m_b_spatial, m_w_branch_sgu, m_w_branch_attn, m_w_out, m_g_mix_post, m_g_ffn_pre, m_w_up, m_w_down, m_g_ffn_post, v_g_mix_pre, v_w_in, v_b_forget, v_g_sgu, v_b_sgu, v_w_spatial, v_b_spatial, v_w_branch_sgu, v_w_branch_attn, v_w_out, v_g_mix_post, v_g_ffn_pre, v_w_up, v_w_down, v_g_ffn_post):
    T = x.shape[1]
    x2 = x.reshape(T, D_MODEL)
    tgt = loss_target.reshape(T, D_MODEL)

    wg_in = _gather_w_in(w_in[0])
    wz, wf, wgt = _assemble_w_in(wg_in)
    bfp = jnp.pad(b_forget, ((0, 0), (0, LANES - N_HEADS)))
    wsp = w_spatial[0]
    bT = b_spatial[0].T

    xn, zuv, qkv, fl, gt = _fwd_in(x2, g_mix_pre, wz, wf, wgt)
    cc, qaug, kaug = _fwd_cum(fl, bfp)
    ys = _fwd_sgu(zuv, g_sgu, b_sgu, wsp, bT)
    ya, lse, wbs, wba, wo, wup, wdn = _fwd_attn(
        qkv, qaug, kaug, (w_branch_sgu[0], w_branch_attn[0], w_out[0], w_up[0], w_down[0]))
    A, B, mg, o, h1 = _fwd_merge(ys, ya, gt, x2, wbs, wba, wo, g_mix_post)
    xn2, a, ddn, dy, loss_part, dg4 = _fwd_ffn_loss(h1, tgt, wup, wdn, g_ffn_pre, g_ffn_post)

    da, dh1, dg3 = _bwd_ffn(ddn, a, dy, h1, wup, wdn, g_ffn_pre)
    dw_up = _wgrad(xn2, da, "wgrad_up", tn=2048, block_cols=512)
    dw_down = _wgrad(a, ddn, "wgrad_down", relu2=True)
    dgl, dys, dya, qab, dab, dg2, dw_bs, dw_ba, dw_out = _bwd_merge(
        dh1, o, A, B, gt, ys, ya, mg, lse, cc, wbs, wba, wo, g_mix_post)
    col_blocks = lambda g, w: g.reshape(g.shape[0], N_DEV, w).transpose(1, 0, 2)
    row_blocks = lambda g, r: g.reshape(N_DEV, r, g.shape[1])
    early_names = ["w_branch_sgu", "w_branch_attn", "w_out", "w_up", "w_down"]
    early = [col_blocks(dw_bs, 128), col_blocks(dw_ba, 128), row_blocks(dw_out, 128), dw_up, row_blocks(dw_down, 512)]
    owners = _owner_indices()
    dzuv, dwsp, dbT, dgs, dbs, *early_land1 = _bwd_sgu(zuv, dys, g_sgu, b_sgu, wsp, bT, early)
    early_parts = [_chip_partials(g, l1, owners, "chip_partials_" + nm)
                   for g, l1, nm in zip(early, early_land1, early_names)]
    dq, dk, dv, dcx, *early_land2 = _bwd_attn(qkv, dya, qab, dab, kaug, early_parts)
    dfl, dbf = _bwd_cum(dcx, fl, bfp)
    dw_z = _wgrad(xn, dzuv, "wgrad_in_z")
    dw_q, dw_k, dw_v, dw_f = _wgrad_multi(xn, [dq, dk, dv, dfl], "wgrad_in_qkvf")
    dw_g = _wgrad(xn, dgl, "wgrad_in_gate")
    blocks_in = _block_dw_in([dw_z, dw_q, dw_k, dw_v, dw_f, dw_g])
    bwd_in_args = (dzuv, dq, dk, dv, dfl, dgl, dh1, x2, g_mix_pre, wz, wf, wgt)
    dx, dg1, land1_in = _bwd_in(*bwd_in_args, (0, T // 4), "bwd_in_a", stage=1, exchanged=blocks_in)
    part_in = _chip_partials(blocks_in, land1_in, owners, "chip_partials_w_in")
    dx, dg1, land2_in = _bwd_in(*bwd_in_args, (T // 4, 3 * T // 4), "bwd_in_b", prev=(dx, dg1), stage=2,
                                exchanged=part_in)
    dx, dg1 = _bwd_in(*bwd_in_args, (3 * T // 4, T), "bwd_in_c", prev=(dx, dg1))

    tot_a, tot_b = _allreduce_small(dict(
        g_mix_pre=dg1, b_forget=dbf, g_sgu=dgs, b_sgu=dbs, w_spatial=dwsp, b_spatial=dbT, g_mix_post=dg2,
        g_ffn_pre=dg3, g_ffn_post=dg4), loss_part)
    r0, nr, c0, nc = LOSS_SLOT
    loss = jnp.sum(tot_a[r0:r0 + nr, c0:c0 + nc])
    small_w = _small_kernel_shapes(g_mix_pre, b_forget, g_sgu, b_sgu, w_spatial, b_spatial, g_mix_post, g_ffn_pre,
                                   g_ffn_post)
    small_m = _small_kernel_shapes(m_g_mix_pre, m_b_forget, m_g_sgu, m_b_sgu, m_w_spatial, m_b_spatial, m_g_mix_post,
                                   m_g_ffn_pre, m_g_ffn_post)
    small_v = _small_kernel_shapes(v_g_mix_pre, v_b_forget, v_g_sgu, v_b_sgu, v_w_spatial, v_b_spatial, v_g_mix_post,
                                   v_g_ffn_pre, v_g_ffn_post)
    sg, sd, sm, sv = (_small_output_shapes(d) for d in _adamw_small(tot_a, tot_b, small_w, small_m, small_v))

    big = {}
    g_in = _reduced_grad(blocks_in, land1_in, land2_in, owners, "reduced_grad_w_in")[:, :IN_SHARD]
    d_, m_, v_ = _adamw(w_in[0], g_in, m_w_in[0], v_w_in[0], "adamw_w_in")
    big["w_in"] = (g_in[None], d_[None], m_[None], v_[None])
    early_wmv = [(w_branch_sgu, m_w_branch_sgu, v_w_branch_sgu), (w_branch_attn, m_w_branch_attn, v_w_branch_attn),
                 (w_out, m_w_out, v_w_out), (w_up, m_w_up, v_w_up), (w_down, m_w_down, v_w_down)]
    for nm, (w, m, v), g, l1, l2 in zip(early_names, early_wmv, early, early_land1, early_land2):
        big[nm] = tuple(t[None] for t in _adamw_reduced(w[0], m[0], v[0], g, l1, l2, owners, "adamw_" + nm))

    order = ["g_mix_pre", "w_in", "b_forget", "g_sgu", "b_sgu", "w_spatial", "b_spatial", "w_branch_sgu",
             "w_branch_attn", "w_out", "g_mix_post", "g_ffn_pre", "w_up", "w_down", "g_ffn_post"]
    outs = [loss, dx.reshape(1, T, D_MODEL)]
    for kind, small in enumerate((sg, sd, sm, sv)):
        outs += [big[nm][kind] if nm in big else small[nm] for nm in order]
    return tuple(outs)
```

```python
import jax
import jax.numpy as jnp
from jax import lax
from jax.experimental import pallas as pl
from jax.experimental.pallas import tpu as pltpu

F32 = jnp.float32
BF16 = jnp.bfloat16
HIGHEST = lax.Precision.HIGHEST
MESH = pl.DeviceIdType.MESH

D_MODEL = 1024
SGU_WIDTH = 512
ATTN_WIDTH = 512
N_HEADS = 8
CHUNK = 128
D_FF = 4096
IN_WIDTH = 4616
N_DEV = 8
IN_SHARD = IN_WIDTH // N_DEV
IN_SHARD_PAD = 640
ZQKV_WIDTH = 2 * SGU_WIDTH + 3 * ATTN_WIDTH
GATE_OFFSET = ZQKV_WIDTH + N_HEADS
EPS = 1e-6
LANES = 128
SUBLANES = 8
VMEM_BYTES = 64 * 1024 * 1024
MIB = 1024 * 1024

ADAM_LR = 0.001
ADAM_B1 = 0.9
ADAM_B2 = 0.999
ADAM_EPS = 1e-08
ADAM_WD = 0.01
ADAM_STEP = 10

TOKEN_TILE = 256
MATMUL_TILE = 512
MERGE_BWD_TILE = 512
ATTN_TILE = 512
CUM_TILE = 256
SGU_TILE = 512
WGRAD_TILE = 1024
NEG = -1e30

NT_DIMS = (((1,), (1,)), ((), ()))
TN_DIMS = (((0,), (0,)), ((), ()))


def _params(vmem_mb, n_grid=1):
    return pltpu.CompilerParams(
        dimension_semantics=("arbitrary",) * n_grid,
        vmem_limit_bytes=min(vmem_mb * MIB, VMEM_BYTES - 6 * MIB),
    )


def _dot(a, b):
    return jnp.dot(a, b, preferred_element_type=F32)


def _dot_nt(a, b):
    return lax.dot_general(a, b, NT_DIMS, preferred_element_type=F32)


def _dot_tn(a, b):
    return lax.dot_general(a, b, TN_DIMS, preferred_element_type=F32)


def _const_spec(shape):
    nd = len(shape)
    return pl.BlockSpec(shape, lambda *_: (0,) * nd, pipeline_mode=pl.Buffered(1))


def _row_spec(tm, n, col=0):
    return pl.BlockSpec((tm, n), lambda i: (i, col))


def _fold8(v):
    return v.reshape(v.shape[0] // SUBLANES, SUBLANES, v.shape[1]).sum(axis=0)


def _pick(v, lane_iota, k):
    return jnp.sum(jnp.where(lane_iota == k, v, 0.0), axis=1, keepdims=True)


def _iota(shape, dim):
    return lax.broadcasted_iota(jnp.int32, shape, dim)


def _gelu(x):
    c = 0.7978845608028654
    return 0.5 * x * (1.0 + jnp.tanh(c * (x + 0.044715 * x * x * x)))


def _gelu_grad(x):
    c = 0.7978845608028654
    t = jnp.tanh(c * (x + 0.044715 * x * x * x))
    return 0.5 * (1.0 + t) + 0.5 * x * (1.0 - t * t) * (c * (1.0 + 3.0 * 0.044715 * x * x))


def _rms_stats(v):
    r = lax.rsqrt(jnp.mean(v * v, axis=1, keepdims=True) + EPS)
    return r, v * r


def _rms_bwd(dout, vhat, r, g):
    a = dout * g
    return r * (a - vhat * jnp.mean(a * vhat, axis=1, keepdims=True))


def _mesh_pos():
    return lax.axis_index("x"), lax.axis_index("y"), lax.axis_index("c")


def _dev_index(px, py, pc):
    return 4 * px + 2 * py + pc


def _other_chips(x, y):
    return [(1 - x, y), (x, 1 - y), (1 - x, 1 - y)]


class _WeightGather:
    def __init__(self, shard_shapes, kinds, stage_shapes=None):
        self.shard_shapes = list(shard_shapes)
        self.kinds = list(kinds)
        self.stage_shapes = list(stage_shapes or shard_shapes)
        self.n = len(self.kinds)

    def out_shapes(self):
        shapes = []
        for (rows, cols), kind in zip(self.stage_shapes, self.kinds):
            full = {"block": (N_DEV, rows, cols), "rows": (N_DEV * rows, cols), "cols": (rows, N_DEV * cols)}[kind]
            shapes.append(jax.ShapeDtypeStruct(full, BF16))
        return shapes

    def scratch_shapes(self):
        return ([pltpu.VMEM(s, BF16) for s in self.stage_shapes]
                + [pltpu.SemaphoreType.DMA((self.n, 7)), pltpu.SemaphoreType.DMA((self.n, 7)),
                   pltpu.SemaphoreType.DMA((self.n,))])

    def _view(self, a, ref, j):
        rows, cols = self.stage_shapes[a]
        if self.kinds[a] == "block":
            return ref.at[j]
        if self.kinds[a] == "rows":
            return ref.at[pl.ds(pl.multiple_of(j * rows, rows), rows), :]
        return ref.at[:, pl.ds(pl.multiple_of(j * cols, cols), cols)]

    def _copy(self, outs, scratch, a, k, block, to, from_stage=False):
        stage, (send_sems, recv_sems, _) = scratch[:self.n], scratch[self.n:]
        dst = self._view(a, outs[a], _dev_index(*block))
        return pltpu.make_async_remote_copy(
            src_ref=stage[a] if from_stage else dst, dst_ref=dst,
            send_sem=send_sems.at[a, k], recv_sem=recv_sems.at[a, k],
            device_id=to, device_id_type=MESH)

    def _local(self, outs, scratch, a, me):
        return pltpu.make_async_copy(scratch[a], self._view(a, outs[a], _dev_index(*me)), scratch[-1].at[a])

    def start(self, ins, outs, scratch):
        x, y, c = _mesh_pos()
        me, sibling = (x, y, c), (x, y, 1 - c)
        for a in range(self.n):
            rows, cols = self.shard_shapes[a]
            if self.stage_shapes[a] != self.shard_shapes[a]:
                scratch[a][...] = jnp.zeros(self.stage_shapes[a], BF16)
            scratch[a][0:rows, 0:cols] = ins[a][...].astype(BF16)
            self._local(outs, scratch, a, me).start()
        for a in range(self.n):
            self._copy(outs, scratch, a, 0, me, sibling, True).start()
            for j, chip in enumerate(_other_chips(x, y)):
                self._copy(outs, scratch, a, 1 + j, me, (*chip, c), True).start()

    def forward(self, outs, scratch):
        x, y, c = _mesh_pos()
        me, sibling = (x, y, c), (x, y, 1 - c)
        for a in range(self.n):
            for j, chip in enumerate(_other_chips(x, y)):
                self._copy(outs, scratch, a, 1 + j, (*chip, c), me).wait_recv()
                self._copy(outs, scratch, a, 4 + j, (*chip, c), sibling).start()

    def finish(self, outs, scratch):
        x, y, c = _mesh_pos()
        me, sibling = (x, y, c), (x, y, 1 - c)
        chips = _other_chips(x, y)
        for a in range(self.n):
            self._copy(outs, scratch, a, 0, sibling, me).wait_recv()
            for j, chip in enumerate(chips):
                self._copy(outs, scratch, a, 4 + j, (*chip, 1 - c), me).wait_recv()
        for a in range(self.n):
            self._copy(outs, scratch, a, 0, me, sibling, True).wait_send()
            for j, chip in enumerate(chips):
                self._copy(outs, scratch, a, 1 + j, me, (*chip, c), True).wait_send()
                self._copy(outs, scratch, a, 4 + j, (*chip, c), sibling).wait_send()
            self._local(outs, scratch, a, me).wait()


def _gather_w_in(w_in_local):
    g = _WeightGather([(D_MODEL, IN_SHARD)], ["block"], [(D_MODEL, IN_SHARD_PAD)])

    def body(w_ref, out_ref, *scratch):
        g.start([w_ref], [out_ref], scratch)
        g.forward([out_ref], scratch)
        g.finish([out_ref], scratch)

    return pl.pallas_call(
        body,
        out_shape=g.out_shapes()[0],
        in_specs=[pl.BlockSpec(memory_space=pltpu.VMEM)],
        out_specs=pl.BlockSpec(memory_space=pl.ANY),
        scratch_shapes=g.scratch_shapes(),
        compiler_params=pltpu.CompilerParams(vmem_limit_bytes=32 * MIB),
        name="gather_w_in",
    )(w_in_local)


class _GradExchange:
    def __init__(self, shapes):
        self.shapes = [tuple(s) for s in shapes]
        self.n = len(self.shapes)

    def land_shapes(self, stage):
        slots, dtype = (4, F32) if stage == 1 else (3, BF16)
        return [jax.ShapeDtypeStruct((slots,) + s, dtype) for s in self.shapes]

    def sem_shapes(self, stage):
        slots = 4 if stage == 1 else 3
        return [pltpu.SemaphoreType.DMA((self.n, slots)), pltpu.SemaphoreType.DMA((self.n, slots))]

    def _copy(self, stage, srcs, lands, sems, a, k):
        x, y, c = _mesh_pos()
        cx, cy = (_other_chips(x, y) + [(x, y)])[k]
        if stage == 1:
            src, to = srcs[a].at[_dev_index(cx, cy, 1 - c)], (x, y, 1 - c)
        else:
            src, to = srcs[a].at[k], (cx, cy, c)
        return pltpu.make_async_remote_copy(
            src_ref=src, dst_ref=lands[a].at[k], send_sem=sems[0].at[a, k], recv_sem=sems[1].at[a, k],
            device_id=to, device_id_type=MESH)

    def start(self, stage, srcs, lands, sems):
        for a in range(self.n):
            for k in range(4 if stage == 1 else 3):
                self._copy(stage, srcs, lands, sems, a, k).start()

    def wait(self, stage, srcs, lands, sems):
        for a in range(self.n):
            for k in range(4 if stage == 1 else 3):
                cp = self._copy(stage, srcs, lands, sems, a, k)
                cp.wait_recv()
                cp.wait_send()


def _owner_indices():
    x, y, c = _mesh_pos()
    return jnp.stack([_dev_index(cx, cy, c) for cx, cy in _other_chips(x, y) + [(x, y)]]).astype(jnp.int32)


def _chip_partials(g, land1, idx, name):
    _, rows, cols = g.shape
    tr = min(rows, 256)

    def body(idx_ref, g_ref, l_ref, o_ref):
        o_ref[...] = (g_ref[...] + l_ref[...]).astype(BF16)

    return pl.pallas_call(
        body,
        grid_spec=pltpu.PrefetchScalarGridSpec(
            num_scalar_prefetch=1, grid=(3, rows // tr),
            in_specs=[pl.BlockSpec((None, tr, cols), lambda k, r, idx: (idx[k], r, 0)),
                      pl.BlockSpec((None, tr, cols), lambda k, r, idx: (k, r, 0))],
            out_specs=pl.BlockSpec((None, tr, cols), lambda k, r, idx: (k, r, 0))),
        out_shape=jax.ShapeDtypeStruct((3, rows, cols), BF16),
        compiler_params=_params(32, 2), name=name,
    )(idx, g, land1)


def _reduced_block(g_ref, l1_ref, a_ref, b_ref, c_ref):
    return ((g_ref[...] + l1_ref[...]) + a_ref[...].astype(F32)) + b_ref[...].astype(F32) + c_ref[...].astype(F32)


def _reduced_specs(tm, cols):
    return [pl.BlockSpec((None, tm, cols), lambda i, idx: (idx[3], i, 0)),
            pl.BlockSpec((None, tm, cols), lambda i, idx: (3, i, 0)),
            pl.BlockSpec((None, tm, cols), lambda i, idx: (0, i, 0)),
            pl.BlockSpec((None, tm, cols), lambda i, idx: (1, i, 0)),
            pl.BlockSpec((None, tm, cols), lambda i, idx: (2, i, 0))]


def _reduced_grad(g, land1, land2, idx, name):
    _, rows, cols = g.shape
    tm = min(rows, 256)

    def body(idx_ref, g_ref, l1_ref, a_ref, b_ref, c_ref, o_ref):
        o_ref[...] = _reduced_block(g_ref, l1_ref, a_ref, b_ref, c_ref)

    return pl.pallas_call(
        body,
        grid_spec=pltpu.PrefetchScalarGridSpec(
            num_scalar_prefetch=1, grid=(rows // tm,), in_specs=_reduced_specs(tm, cols),
            out_specs=pl.BlockSpec((tm, cols), lambda i, idx: (i, 0))),
        out_shape=jax.ShapeDtypeStruct((rows, cols), F32),
        compiler_params=_params(32), name=name,
    )(idx, g, land1, land2, land2, land2)


def _adamw_math(w, g, m, v):
    m = ADAM_B1 * m + (1.0 - ADAM_B1) * g
    v = ADAM_B2 * v + (1.0 - ADAM_B2) * (g * g)
    m_hat = m / (1.0 - ADAM_B1 ** ADAM_STEP)
    v_hat = v / (1.0 - ADAM_B2 ** ADAM_STEP)
    delta = -ADAM_LR * (m_hat / (jnp.sqrt(v_hat) + ADAM_EPS) + ADAM_WD * w)
    return delta, m, v


SMALL_NAMES = ("g_mix_pre", "b_forget", "g_sgu", "b_sgu", "w_spatial", "b_spatial", "g_mix_post", "g_ffn_pre",
               "g_ffn_post")
SMALL_SLOTS = {"g_mix_pre": (0, 1, 0, 1024), "g_mix_post": (1, 1, 0, 1024), "g_ffn_pre": (2, 1, 0, 1024),
               "g_ffn_post": (3, 1, 0, 1024), "g_sgu": (4, 1, 0, 512), "b_sgu": (4, 1, 512, 512),
               "b_forget": (5, 1, 0, 128), "b_spatial": (8, 8, 0, 128)}
SMALL_TILE = (16, 1024)
SPATIAL_TILE = (N_HEADS * CHUNK, CHUNK)


LOSS_SLOT = (8, 8, 128, 128)


def _allreduce_small(grads, loss_part):
    names = list(SMALL_NAMES)

    def body(*refs):
        g = dict(zip(names, refs[:len(names)]))
        loss_ref = refs[len(names)]
        tot_a, tot_b, buf_a, buf_b, sib_a, sib_b, ps_a, ps_b, land_a, land_b, send_sems, recv_sems = refs[len(names) + 1:]
        x, y, c = _mesh_pos()
        buf_a[...] = jnp.zeros(SMALL_TILE, F32)
        r0, nr, c0, nc = LOSS_SLOT
        buf_a[r0:r0 + nr, c0:c0 + nc] = loss_ref[...]
        for name, (r0, nr, c0, nc) in SMALL_SLOTS.items():
            val = g[name][...]
            if name == "b_spatial":
                val = jnp.transpose(val)[0:N_HEADS, :]
            buf_a[r0:r0 + nr, c0:c0 + nc] = val
        buf_b[...] = g["w_spatial"][...].reshape(SPATIAL_TILE)

        def swap(k, src, dst, to):
            return pltpu.make_async_remote_copy(src_ref=src, dst_ref=dst, send_sem=send_sems.at[k],
                                                recv_sem=recv_sems.at[k], device_id=to, device_id_type=MESH)

        first = [swap(0, buf_a, sib_a, (x, y, 1 - c)), swap(1, buf_b, sib_b, (x, y, 1 - c))]
        for cp in first:
            cp.start()
        for cp in first:
            cp.wait_recv()
        ps_a[...] = buf_a[...] + sib_a[...]
        ps_b[...] = buf_b[...] + sib_b[...]
        second = []
        for k, (cx, cy) in enumerate(_other_chips(x, y)):
            second += [swap(2 + 2 * k, ps_a, land_a.at[k], (cx, cy, c)), swap(3 + 2 * k, ps_b, land_b.at[k], (cx, cy, c))]
        for cp in second:
            cp.start()
        for cp in second:
            cp.wait_recv()
        tot_a[...] = (ps_a[...] + land_a[0]) + (land_a[1] + land_a[2])
        tot_b[...] = (ps_b[...] + land_b[0]) + (land_b[1] + land_b[2])
        for cp in first + second:
            cp.wait_send()

    vm = pl.BlockSpec(memory_space=pltpu.VMEM)
    return pl.pallas_call(
        body,
        out_shape=(jax.ShapeDtypeStruct(SMALL_TILE, F32), jax.ShapeDtypeStruct(SPATIAL_TILE, F32)),
        in_specs=[vm] * (len(names) + 1), out_specs=[vm, vm],
        scratch_shapes=[pltpu.VMEM(SMALL_TILE, F32), pltpu.VMEM(SPATIAL_TILE, F32)] * 3
        + [pltpu.VMEM((3,) + SMALL_TILE, F32), pltpu.VMEM((3,) + SPATIAL_TILE, F32),
           pltpu.SemaphoreType.DMA((8,)), pltpu.SemaphoreType.DMA((8,))],
        compiler_params=pltpu.CompilerParams(vmem_limit_bytes=32 * MIB),
        name="allreduce_small",
    )(*[grads[nm] for nm in names], loss_part)


def _adamw_small(tot_a, tot_b, ws, ms, vs):
    names = list(SMALL_NAMES)
    n = len(names)

    def body(a_ref, b_ref, *refs):
        w, m, v = (dict(zip(names, refs[i * n:(i + 1) * n])) for i in range(3))
        outs = [dict(zip(names, refs[(3 + i) * n:(4 + i) * n])) for i in range(4)]
        for name in names:
            if name == "w_spatial":
                g = b_ref[...].reshape(N_HEADS, CHUNK, CHUNK)
            else:
                r0, nr, c0, nc = SMALL_SLOTS[name]
                g = a_ref[r0:r0 + nr, c0:c0 + nc]
            vals = (g,) + _adamw_math(w[name][...], g, m[name][...], v[name][...])
            for out, val in zip(outs, vals):
                out[name][...] = val

    shapes = [jax.ShapeDtypeStruct(ws[nm].shape, F32) for nm in names]
    vm = pl.BlockSpec(memory_space=pltpu.VMEM)
    res = pl.pallas_call(
        body, out_shape=shapes * 4, in_specs=[vm] * (2 + 3 * n), out_specs=[vm] * (4 * n),
        compiler_params=pltpu.CompilerParams(vmem_limit_bytes=32 * MIB), name="adamw_small",
    )(tot_a, tot_b, *[d[nm] for d in (ws, ms, vs) for nm in names])
    return [dict(zip(names, res[i * n:(i + 1) * n])) for i in range(4)]


def _adamw_reduced(w, m, v, g, land1, land2, idx, name):
    rows, cols = w.shape
    tm = min(rows, 256)

    def body(idx_ref, w_ref, m_ref, v_ref, g_ref, l1_ref, a_ref, b_ref, c_ref, go_ref, d_ref, nm_ref, nv_ref):
        gsum = _reduced_block(g_ref, l1_ref, a_ref, b_ref, c_ref)
        go_ref[...] = gsum
        delta, nm, nv = _adamw_math(w_ref[...], gsum, m_ref[...], v_ref[...])
        d_ref[...] = delta
        nm_ref[...] = nm
        nv_ref[...] = nv

    sd = jax.ShapeDtypeStruct((rows, cols), F32)
    spec = pl.BlockSpec((tm, cols), lambda i, idx: (i, 0))
    return pl.pallas_call(
        body,
        grid_spec=pltpu.PrefetchScalarGridSpec(
            num_scalar_prefetch=1, grid=(rows // tm,), in_specs=[spec] * 3 + _reduced_specs(tm, cols),
            out_specs=[spec] * 4),
        out_shape=(sd, sd, sd, sd),
        compiler_params=_params(32), name=name,
    )(idx, w, m, v, g, land1, land2, land2, land2)


def _adamw(w, g, m, v, name):
    rows, cols = w.shape
    tm = 256 if rows % 256 == 0 else rows

    def body(w_ref, g_ref, m_ref, v_ref, d_ref, nm_ref, nv_ref):
        delta, nm, nv = _adamw_math(w_ref[...], g_ref[...], m_ref[...], v_ref[...])
        d_ref[...] = delta
        nm_ref[...] = nm
        nv_ref[...] = nv

    sd = jax.ShapeDtypeStruct((rows, cols), F32)
    spec = _row_spec(tm, cols)
    return pl.pallas_call(
        body, grid=(rows // tm,), out_shape=(sd, sd, sd), in_specs=[spec] * 4, out_specs=[spec] * 3,
        compiler_params=_params(32), name=name,
    )(w, g, m, v)


def _virtual_slab(sources, v0, v_end, like):
    lane = _iota(like.shape, 1)
    out = jnp.zeros(like.shape, like.dtype)
    for v_start, v_stop, read in sources:
        a, b = max(v0, v_start), min(v0 + LANES, v_stop, v_end)
        while a < b:
            c = a - v_start
            n = min(b - a, LANES - c % LANES)
            piece = read(c // LANES)
            shift = (a - v0 - c % LANES) % LANES
            if shift:
                piece = pltpu.roll(piece, shift, 1)
            out = jnp.where((lane >= a - v0) & (lane < a - v0 + n), piece, out)
            a += n
    return out


def _assemble_w_in(wg_in):
    tm = TOKEN_TILE

    def body(src_ref, wz_ref, wf_ref, wg_ref):
        like = src_ref[0, :, 0:LANES]
        sources = [(IN_SHARD * j, IN_SHARD * (j + 1),
                    (lambda k, j=j: src_ref[j, :, LANES * k:LANES * (k + 1)])) for j in range(N_DEV)]
        for k in range(ZQKV_WIDTH // LANES):
            wz_ref[:, LANES * k:LANES * (k + 1)] = _virtual_slab(sources, LANES * k, ZQKV_WIDTH, like)
        wf_ref[...] = _virtual_slab(sources, ZQKV_WIDTH, GATE_OFFSET, like)
        for k in range(2 * D_MODEL // LANES):
            wg_ref[:, LANES * k:LANES * (k + 1)] = _virtual_slab(sources, GATE_OFFSET + LANES * k, IN_WIDTH, like)

    return pl.pallas_call(
        body, grid=(D_MODEL // tm,),
        out_shape=(jax.ShapeDtypeStruct((D_MODEL, ZQKV_WIDTH), BF16), jax.ShapeDtypeStruct((D_MODEL, LANES), BF16),
                   jax.ShapeDtypeStruct((D_MODEL, 2 * D_MODEL), BF16)),
        in_specs=[pl.BlockSpec((N_DEV, tm, IN_SHARD_PAD), lambda i: (0, i, 0))],
        out_specs=[_row_spec(tm, ZQKV_WIDTH), _row_spec(tm, LANES), _row_spec(tm, 2 * D_MODEL)],
        compiler_params=_params(32), name="assemble_w_in",
    )(wg_in)


def _block_dw_in(pieces):
    tm = TOKEN_TILE
    widths = [2 * SGU_WIDTH, ATTN_WIDTH, ATTN_WIDTH, ATTN_WIDTH, N_HEADS, 2 * D_MODEL]
    starts = [sum(widths[:k]) for k in range(len(widths))]

    def body(*refs):
        in_refs, out_ref = refs[:-1], refs[-1]
        like = in_refs[0][:, 0:LANES]
        slab = lambda ref: (lambda k: ref[:, LANES * k:LANES * (k + 1)])
        sources = [(s, s + w, slab(ref)) for s, w, ref in zip(starts, widths, in_refs)]
        for j in range(N_DEV):
            for k in range(IN_SHARD_PAD // LANES):
                out_ref[j, :, LANES * k:LANES * (k + 1)] = _virtual_slab(
                    sources, IN_SHARD * j + LANES * k, IN_SHARD * (j + 1), like)

    return pl.pallas_call(
        body, grid=(D_MODEL // tm,),
        out_shape=jax.ShapeDtypeStruct((N_DEV, D_MODEL, IN_SHARD_PAD), F32),
        in_specs=[_row_spec(tm, pc.shape[1]) for pc in pieces],
        out_specs=pl.BlockSpec((N_DEV, tm, IN_SHARD_PAD), lambda i: (0, i, 0)),
        compiler_params=_params(32), name="block_dw_in",
    )(*pieces)


def _fwd_in(x2, g1, wz, wf, wg):
    T = x2.shape[0]
    tm = MATMUL_TILE

    def body(x_ref, g_ref, wz_ref, wf_ref, wg_ref, xn_ref, zuv_ref, qkv_ref, fl_ref, gt_ref):
        x = x_ref[...]
        r, xh = _rms_stats(x)
        xn = (xh * g_ref[...]).astype(BF16)
        xn_ref[...] = xn
        zuv_ref[...] = _dot(xn, wz_ref[:, 0:1024]).astype(BF16)
        qkv_ref[:, 0:512] = (_dot(xn, wz_ref[:, 1024:1536]) * 0.125).astype(BF16)
        qkv_ref[:, 512:1536] = _dot(xn, wz_ref[:, 1536:2560]).astype(BF16)
        fl_ref[...] = _dot(xn, wf_ref[...])
        gt_ref[...] = jax.nn.sigmoid(_dot(xn, wg_ref[...])).astype(BF16)

    return pl.pallas_call(
        body, grid=(T // tm,),
        out_shape=(jax.ShapeDtypeStruct((T, D_MODEL), BF16), jax.ShapeDtypeStruct((T, 1024), BF16),
                   jax.ShapeDtypeStruct((T, 1536), BF16), jax.ShapeDtypeStruct((T, LANES), F32),
                   jax.ShapeDtypeStruct((T, 2048), BF16)),
        in_specs=[_row_spec(tm, D_MODEL), _const_spec((1, D_MODEL)), _const_spec((D_MODEL, ZQKV_WIDTH)),
                  _const_spec((D_MODEL, LANES)), _const_spec((D_MODEL, 2048))],
        out_specs=[_row_spec(tm, D_MODEL), _row_spec(tm, 1024), _row_spec(tm, 1536), _row_spec(tm, LANES),
                   _row_spec(tm, 2048)],
        compiler_params=_params(48), name="fwd_in",
    )(x2, g1, wz, wf, wg)


def _log_sigmoid(f):
    return jnp.minimum(f, 0.0) - jnp.log1p(jnp.exp(-jnp.abs(f)))


AUG_LANES = 6


def _split3(v):
    hi = v.astype(BF16)
    r1 = v - hi.astype(F32)
    mid = r1.astype(BF16)
    lo = (r1 - mid.astype(F32)).astype(BF16)
    return hi, mid, lo


def _spread(parts, k0):
    r, c = _iota((LANES, LANES), 0), _iota((LANES, LANES), 1)
    out = None
    for i, part in enumerate(parts):
        e = ((c == AUG_LANES * r + (k0 + i)) & (r < N_HEADS)).astype(BF16)
        term = _dot(part, e)
        out = term if out is None else out + term
    return out


def _aug_query(v):
    ones = (_iota(v.shape, 1) < N_HEADS).astype(BF16)
    return (_spread(_split3(v), 0) + _spread((ones, ones, ones), 3)).astype(BF16)


def _aug_key(v):
    ones = (_iota(v.shape, 1) < N_HEADS).astype(BF16)
    return (_spread((ones, ones, ones), 0) - _spread(_split3(v), 3)).astype(BF16)


def _aug_stack(t2, aug, p):
    lane = _iota(t2.shape, 1)
    low = lane < 64
    zero = jnp.zeros_like(t2)
    first = 2 * AUG_LANES * p
    a_e = jnp.where((lane >= first) & (lane < first + AUG_LANES), aug, zero)
    a_o = jnp.where((lane >= first + AUG_LANES) & (lane < first + 2 * AUG_LANES), aug, zero)
    top = jnp.concatenate([jnp.where(low, t2, zero), a_e], axis=1)
    bot = jnp.concatenate([jnp.where(low, zero, t2), a_o], axis=1)
    return jnp.concatenate([top, bot], axis=0)


def _fwd_cum(fl, bfp):
    T = fl.shape[0]
    tb = CUM_TILE

    def body(fl_ref, b_ref, cc_ref, qa_ref, ka_ref):
        tri = (_iota((tb, tb), 0) >= _iota((tb, tb), 1)).astype(F32)
        carry = jnp.zeros((1, LANES), F32)
        for i in range(T // tb):
            rows = slice(i * tb, (i + 1) * tb)
            lf = _log_sigmoid(fl_ref[rows, :] + b_ref[...])
            cs = jnp.dot(tri, lf, precision=HIGHEST, preferred_element_type=F32) + carry
            cc_ref[rows, :] = cs
            carry = cs[tb - 1:tb, :]
            qa_ref[rows, :] = _aug_query(cs)
            ka_ref[rows, :] = _aug_key(cs)

    return pl.pallas_call(
        body,
        out_shape=(jax.ShapeDtypeStruct((T, LANES), F32), jax.ShapeDtypeStruct((T, LANES), BF16),
                   jax.ShapeDtypeStruct((T, LANES), BF16)),
        compiler_params=pltpu.CompilerParams(vmem_limit_bytes=32 * MIB), name="fwd_cum",
    )(fl, bfp)


def _sgu_forward_parts(z, gs, bs):
    u = _gelu(z[:, :SGU_WIDTH])
    vv = _gelu(z[:, SGU_WIDTH:])
    vc = vv - jnp.mean(vv, axis=1, keepdims=True)
    rs = lax.rsqrt(jnp.mean(vc * vc, axis=1, keepdims=True) + EPS)
    vhat = vc * rs
    return u, vhat, rs, vhat * gs + bs


def _sgu_pair_weights(w_ref, bT, p):
    tril = _iota((CHUNK, CHUNK), 0) >= _iota((CHUNK, CHUNK), 1)
    we = jnp.where(tril, w_ref[2 * p], 0.0).astype(BF16)
    wo = jnp.where(tril, w_ref[2 * p + 1], 0.0).astype(BF16)
    lane8 = _iota(bT.shape, 1)
    low = _iota((CHUNK, LANES), 1) < 64
    b2 = jnp.where(low, _pick(bT, lane8, 2 * p), _pick(bT, lane8, 2 * p + 1))
    return we, wo, b2


def _chunks_on_lanes(v, p, nc):
    return jnp.concatenate([v[c * CHUNK:(c + 1) * CHUNK, LANES * p:LANES * (p + 1)] for c in range(nc)], axis=1)


def _sgu_mix(we, wo, b2, vcat, nc):
    low = (_iota((CHUNK, nc * LANES), 1) % LANES) < 64
    return jnp.where(low, _dot(we, vcat), _dot(wo, vcat)) + jnp.concatenate([b2] * nc, axis=1)


def _fwd_sgu(zuv, gs, bs, wsp, bT):
    T = zuv.shape[0]
    tc = SGU_TILE
    nc = tc // CHUNK

    def body(z_ref, gs_ref, bs_ref, w_ref, bT_ref, y_ref):
        u, _, _, vln = _sgu_forward_parts(z_ref[...].astype(F32), gs_ref[...], bs_ref[...])
        vb = vln.astype(BF16)
        for p in range(4):
            we, wo, b2 = _sgu_pair_weights(w_ref, bT_ref[...], p)
            s = _sgu_mix(we, wo, b2, _chunks_on_lanes(vb, p, nc), nc)
            for c in range(nc):
                rows, cols = slice(c * CHUNK, (c + 1) * CHUNK), slice(LANES * p, LANES * (p + 1))
                y_ref[rows, cols] = (u[rows, cols] * s[:, c * LANES:(c + 1) * LANES]).astype(BF16)

    return pl.pallas_call(
        body, grid=(T // tc,), out_shape=jax.ShapeDtypeStruct((T, SGU_WIDTH), BF16),
        in_specs=[_row_spec(tc, 1024), _const_spec((1, SGU_WIDTH)), _const_spec((1, SGU_WIDTH)),
                  _const_spec((8, CHUNK, CHUNK)), _const_spec((CHUNK, 8))],
        out_specs=_row_spec(tc, SGU_WIDTH),
        compiler_params=_params(40), name="fwd_sgu",
    )(zuv, gs, bs, wsp, bT)


def _fwd_attn(qkv, qaug, kaug, w_shards):
    T = qkv.shape[0]
    tq = tk = ATTN_TILE
    nq = T // tq
    gather = _WeightGather([w.shape for w in w_shards], ["cols", "cols", "rows", "cols", "rows"])
    nw = gather.n

    def body(q_ref, qa_ref, k_ref, v_ref, ka_ref, *rest):
        w_refs, (o_ref, lse_ref), wg_refs, scratch = rest[:nw], rest[nw:nw + 2], rest[nw + 2:2 * nw + 2], rest[2 * nw + 2:]
        i = pl.program_id(0)

        @pl.when(i == 0)
        def _():
            gather.start(w_refs, wg_refs, scratch)

        @pl.when(i == nq // 2)
        def _():
            gather.forward(wg_refs, scratch)

        lane = _iota((tq, LANES), 1)
        low = lane < 64
        lowk = _iota((tk, LANES), 1) < 64
        one = jnp.ones((tk, LANES), BF16)
        row = _iota((2 * tq, tk), 0) % tq
        col = _iota((2 * tq, tk), 1)
        cols = [slice(LANES * p, LANES * (p + 1)) for p in range(4)]
        qa = qa_ref[...]
        qs = [_aug_stack(q_ref[:, cols[p]], qa, p) for p in range(4)]

        def step(j, carry, masked):
            ks = pl.ds(pl.multiple_of(j * tk, tk), tk)
            ka = ka_ref[ks, :]
            new = []
            for p in range(4):
                m, acc_e, acc_o = carry[p]
                v2 = v_ref[ks, cols[p]]
                s = _dot_nt(qs[p], jnp.concatenate([k_ref[ks, cols[p]], ka], axis=1))
                if masked:
                    s = jnp.where(col <= row, s, NEG)
                mn = jnp.maximum(m, jnp.max(s, axis=1, keepdims=True))
                al = jnp.exp(m - mn)
                pm = jnp.exp(s - mn).astype(BF16)
                acc_e = al[:tq] * acc_e + _dot(pm[:tq], jnp.where(lowk, v2, one))
                acc_o = al[tq:] * acc_o + _dot(pm[tq:], jnp.where(lowk, one, v2))
                new.append((mn, acc_e, acc_o))
            return tuple(new)

        init = tuple((jnp.full((2 * tq, 1), NEG, F32), jnp.zeros((tq, LANES), F32), jnp.zeros((tq, LANES), F32))
                     for _ in range(4))
        def trip(t, c):
            for u in range(3):
                c = step(3 * t + u, c, False)
            return c

        carry = lax.fori_loop(0, i // 3, trip, init)
        carry = lax.fori_loop(3 * (i // 3), i, lambda j, c: step(j, c, False), carry)
        carry = step(i, carry, True)
        lse_blk = jnp.zeros((tq, LANES), F32)
        for p in range(4):
            m, acc_e, acc_o = carry[p]
            l_e = pltpu.roll(acc_e, 64, 1)
            l_o = pltpu.roll(acc_o, 64, 1)
            o_ref[:, cols[p]] = jnp.where(low, acc_e / l_e, acc_o / l_o).astype(BF16)
            lse_blk = jnp.where(lane == 2 * p, m[:tq] + jnp.log(l_e), lse_blk)
            lse_blk = jnp.where(lane == 2 * p + 1, m[tq:] + jnp.log(acc_o), lse_blk)
        lse_ref[...] = lse_blk

        @pl.when(i == nq - 1)
        def _():
            gather.finish(wg_refs, scratch)

    return pl.pallas_call(
        body, grid=(nq,),
        out_shape=[jax.ShapeDtypeStruct((T, ATTN_WIDTH), BF16), jax.ShapeDtypeStruct((T, LANES), F32)]
        + gather.out_shapes(),
        in_specs=[_row_spec(tq, 512), _row_spec(tq, LANES),
                  pl.BlockSpec((T, 512), lambda i: (0, 1), pipeline_mode=pl.Buffered(1)),
                  pl.BlockSpec((T, 512), lambda i: (0, 2), pipeline_mode=pl.Buffered(1)),
                  _const_spec((T, LANES))] + [_const_spec(w.shape) for w in w_shards],
        out_specs=[_row_spec(tq, ATTN_WIDTH), _row_spec(tq, LANES)] + [pl.BlockSpec(memory_space=pl.ANY)] * nw,
        scratch_shapes=gather.scratch_shapes(),
        compiler_params=_params(58), name="fwd_attn",
    )(qkv, qaug, qkv, qkv, kaug, *w_shards)


def _fwd_merge(ys, ya, gt, x2, wbs, wba, wo, g2):
    T = x2.shape[0]
    tm = TOKEN_TILE

    def body(ys_ref, ya_ref, gt_ref, x_ref, wbs_ref, wba_ref, wo_ref, g2_ref, a_ref, b_ref, mg_ref, o_ref, h1_ref):
        A = _dot(ys_ref[...], wbs_ref[...])
        B = _dot(ya_ref[...], wba_ref[...])
        mg = (gt_ref[:, :D_MODEL].astype(F32) * A + gt_ref[:, D_MODEL:].astype(F32) * B).astype(BF16)
        o = _dot(mg, wo_ref[...])
        r2, oh = _rms_stats(o)
        a_ref[...] = A.astype(BF16)
        b_ref[...] = B.astype(BF16)
        mg_ref[...] = mg
        o_ref[...] = o.astype(BF16)
        h1_ref[...] = x_ref[...] + oh * g2_ref[...]

    sd = jax.ShapeDtypeStruct((T, D_MODEL), BF16)
    return pl.pallas_call(
        body, grid=(T // tm,),
        out_shape=(sd, sd, sd, sd, jax.ShapeDtypeStruct((T, D_MODEL), F32)),
        in_specs=[_row_spec(tm, 512), _row_spec(tm, 512), _row_spec(tm, 2048), _row_spec(tm, D_MODEL),
                  _const_spec((512, D_MODEL)), _const_spec((512, D_MODEL)), _const_spec((D_MODEL, D_MODEL)),
                  _const_spec((1, D_MODEL))],
        out_specs=[_row_spec(tm, D_MODEL)] * 5,
        compiler_params=_params(40), name="fwd_merge",
    )(ys, ya, gt, x2, wbs, wba, wo, g2)


def _fwd_ffn_loss(h1, tgt, wup, wdn, g3, g4):
    T = h1.shape[0]
    tm = MATMUL_TILE
    nsteps = T // tm

    def body(h1_ref, tg_ref, wup_ref, wdn_ref, g3_ref, g4_ref, xn2_ref, a_ref, ddn_ref, dy_ref, loss_ref,
             dg4_ref, acc_l, acc_g):
        i = pl.program_id(0)

        @pl.when(i == 0)
        def _():
            acc_l[...] = jnp.zeros_like(acc_l)
            acc_g[...] = jnp.zeros_like(acc_g)

        h1v = h1_ref[...]
        r3, h1h = _rms_stats(h1v)
        xn2 = (h1h * g3_ref[...]).astype(BF16)
        xn2_ref[...] = xn2
        dn = jnp.zeros((tm, D_MODEL), F32)
        for j in range(D_FF // 1024):
            cols = slice(1024 * j, 1024 * (j + 1))
            a = _dot(xn2, wup_ref[:, cols])
            a_ref[:, cols] = a.astype(BF16)
            hid = jnp.square(jnp.maximum(a, 0.0)).astype(BF16)
            dn = dn + _dot(hid, wdn_ref[cols, :])
        r4, dnh = _rms_stats(dn)
        g4v = g4_ref[...]
        e = (h1v + dnh * g4v) - tg_ref[...]
        sq = e * e
        s1 = sq[:, 0:LANES]
        for j in range(1, D_MODEL // LANES):
            s1 = s1 + sq[:, LANES * j:LANES * (j + 1)]
        acc_l[...] += _fold8(s1)
        dy = e * (1.0 / D_MODEL)
        dy_ref[...] = dy
        acc_g[...] += _fold8(dy * dnh)
        ddn_ref[...] = _rms_bwd(dy, dnh, r4, g4v).astype(BF16)

        @pl.when(i == nsteps - 1)
        def _():
            loss_ref[...] = acc_l[...] * (0.5 / D_MODEL)
            dg4_ref[...] = jnp.sum(acc_g[...], axis=0, keepdims=True)

    return pl.pallas_call(
        body, grid=(nsteps,),
        out_shape=(jax.ShapeDtypeStruct((T, D_MODEL), BF16), jax.ShapeDtypeStruct((T, D_FF), BF16),
                   jax.ShapeDtypeStruct((T, D_MODEL), BF16), jax.ShapeDtypeStruct((T, D_MODEL), F32),
                   jax.ShapeDtypeStruct((SUBLANES, LANES), F32), jax.ShapeDtypeStruct((1, D_MODEL), F32)),
        in_specs=[_row_spec(tm, D_MODEL), _row_spec(tm, D_MODEL), _const_spec((D_MODEL, D_FF)),
                  _const_spec((D_FF, D_MODEL)), _const_spec((1, D_MODEL)), _const_spec((1, D_MODEL))],
        out_specs=[_row_spec(tm, D_MODEL), _row_spec(tm, D_FF), _row_spec(tm, D_MODEL), _row_spec(tm, D_MODEL),
                   pl.BlockSpec((SUBLANES, LANES), lambda i: (0, 0)), pl.BlockSpec((1, D_MODEL), lambda i: (0, 0))],
        scratch_shapes=[pltpu.VMEM((SUBLANES, LANES), F32), pltpu.VMEM((SUBLANES, D_MODEL), F32)],
        compiler_params=_params(52), name="fwd_ffn_loss",
    )(h1, tgt, wup, wdn, g3, g4)


def _bwd_ffn(ddn, a, dy, h1, wup, wdn, g3):
    T = h1.shape[0]
    tm = MATMUL_TILE
    nsteps = T // tm

    def body(ddn_ref, a_ref, dy_ref, h1_ref, wup_ref, wdn_ref, g3_ref, da_ref, dh1_ref, dg3_ref, acc_g):
        i = pl.program_id(0)

        @pl.when(i == 0)
        def _():
            acc_g[...] = jnp.zeros_like(acc_g)

        ddnv = ddn_ref[...]
        dxn2 = jnp.zeros((tm, D_MODEL), F32)
        for j in range(D_FF // 1024):
            cols = slice(1024 * j, 1024 * (j + 1))
            dhid = _dot_nt(ddnv, wdn_ref[cols, :])
            da = (dhid * (2.0 * jnp.maximum(a_ref[:, cols].astype(F32), 0.0))).astype(BF16)
            da_ref[:, cols] = da
            dxn2 = dxn2 + _dot_nt(da, wup_ref[:, cols])
        r3, h1h = _rms_stats(h1_ref[...])
        acc_g[...] += _fold8(dxn2 * h1h)
        dh1_ref[...] = dy_ref[...] + _rms_bwd(dxn2, h1h, r3, g3_ref[...])

        @pl.when(i == nsteps - 1)
        def _():
            dg3_ref[...] = jnp.sum(acc_g[...], axis=0, keepdims=True)

    return pl.pallas_call(
        body, grid=(nsteps,),
        out_shape=(jax.ShapeDtypeStruct((T, D_FF), BF16), jax.ShapeDtypeStruct((T, D_MODEL), F32),
                   jax.ShapeDtypeStruct((1, D_MODEL), F32)),
        in_specs=[_row_spec(tm, D_MODEL), _row_spec(tm, D_FF), _row_spec(tm, D_MODEL), _row_spec(tm, D_MODEL),
                  _const_spec((D_MODEL, D_FF)), _const_spec((D_FF, D_MODEL)), _const_spec((1, D_MODEL))],
        out_specs=[_row_spec(tm, D_FF), _row_spec(tm, D_MODEL), pl.BlockSpec((1, D_MODEL), lambda i: (0, 0))],
        scratch_shapes=[pltpu.VMEM((SUBLANES, D_MODEL), F32)],
        compiler_params=_params(52), name="bwd_ffn",
    )(ddn, a, dy, h1, wup, wdn, g3)


def _wgrad(xa, dy, name, relu2=False, tn=None, block_cols=None):
    T, K = xa.shape
    N = dy.shape[1]
    tn = N if tn is None else tn
    tt = min(T, WGRAD_TILE if K <= D_MODEL else WGRAD_TILE // 2)
    if block_cols:
        nb = tn // block_cols
        out_shape = jax.ShapeDtypeStruct((N // block_cols, K, block_cols), F32)
        out_spec = pl.BlockSpec((nb, K, block_cols), lambda n, t: (n, 0, 0))
    else:
        out_shape = jax.ShapeDtypeStruct((K, N), F32)
        out_spec = pl.BlockSpec((K, tn), lambda n, t: (0, n))

    def body(x_ref, dy_ref, o_ref):
        @pl.when(pl.program_id(1) == 0)
        def _():
            o_ref[...] = jnp.zeros_like(o_ref)

        xv = x_ref[...]
        if relu2:
            xv = jnp.square(jnp.maximum(xv.astype(F32), 0.0)).astype(BF16)
        if block_cols:
            for b in range(nb):
                o_ref[b] += _dot_tn(xv, dy_ref[:, block_cols * b:block_cols * (b + 1)])
        else:
            o_ref[...] += _dot_tn(xv, dy_ref[...])

    return pl.pallas_call(
        body, grid=(N // tn, T // tt), out_shape=out_shape,
        in_specs=[pl.BlockSpec((tt, K), lambda n, t: (t, 0)), pl.BlockSpec((tt, tn), lambda n, t: (t, n))],
        out_specs=out_spec,
        compiler_params=_params(52, 2), name=name,
    )(xa, dy)


def _wgrad_multi(xa, dys, name):
    T, K = xa.shape
    tt = min(T, WGRAD_TILE)
    n = len(dys)

    def body(x_ref, *refs):
        dy_refs, o_refs = refs[:n], refs[n:]

        @pl.when(pl.program_id(0) == 0)
        def _():
            for o_ref in o_refs:
                o_ref[...] = jnp.zeros_like(o_ref)

        xv = x_ref[...]
        for dy_ref, o_ref in zip(dy_refs, o_refs):
            o_ref[...] += _dot_tn(xv, dy_ref[...])

    return pl.pallas_call(
        body, grid=(T // tt,),
        out_shape=[jax.ShapeDtypeStruct((K, dy.shape[1]), F32) for dy in dys],
        in_specs=[_row_spec(tt, K)] + [_row_spec(tt, dy.shape[1]) for dy in dys],
        out_specs=[pl.BlockSpec((K, dy.shape[1]), lambda t: (0, 0)) for dy in dys],
        compiler_params=_params(52), name=name,
    )(xa, *dys)


def _bwd_merge(dh1, o, A, B, gt, ys, ya, mg, lse, cc, wbs, wba, wo, g2):
    T = dh1.shape[0]
    tm = MERGE_BWD_TILE
    nsteps = T // tm

    def body(dh1_ref, o_ref, a_ref, b_ref, gt_ref, ys_ref, ya_ref, mg_ref, lse_ref, cc_ref, wbs_ref, wba_ref,
             wo_ref, g2_ref, dgl_ref, dys_ref, dya_ref, qab_ref, dab_ref, dg2_ref, dwbs_ref, dwba_ref, dwo_ref,
             acc_g):
        i = pl.program_id(0)

        @pl.when(i == 0)
        def _():
            acc_g[...] = jnp.zeros_like(acc_g)
            dwbs_ref[...] = jnp.zeros_like(dwbs_ref)
            dwba_ref[...] = jnp.zeros_like(dwba_ref)
            dwo_ref[...] = jnp.zeros_like(dwo_ref)

        dh1v = dh1_ref[...]
        r2, oh = _rms_stats(o_ref[...].astype(F32))
        acc_g[...] += _fold8(dh1v * oh)
        do = _rms_bwd(dh1v, oh, r2, g2_ref[...]).astype(BF16)
        dwo_ref[...] += _dot_tn(mg_ref[...], do)
        dmg = _dot_nt(do, wo_ref[...])
        ga = gt_ref[:, :D_MODEL].astype(F32)
        gb = gt_ref[:, D_MODEL:].astype(F32)
        dgl_ref[:, :D_MODEL] = (dmg * a_ref[...].astype(F32) * ga * (1.0 - ga)).astype(BF16)
        dgl_ref[:, D_MODEL:] = (dmg * b_ref[...].astype(F32) * gb * (1.0 - gb)).astype(BF16)
        dA = (dmg * ga).astype(BF16)
        dB = (dmg * gb).astype(BF16)
        dwbs_ref[...] += _dot_tn(ys_ref[...], dA)
        dwba_ref[...] += _dot_tn(ya_ref[...], dB)
        dys_ref[...] = _dot_nt(dA, wbs_ref[...]).astype(BF16)
        dya = _dot_nt(dB, wba_ref[...]).astype(BF16)
        dya_ref[...] = dya
        prod = dya.astype(F32) * ya_ref[...].astype(F32)
        lane = _iota((tm, LANES), 1)
        low = lane < 64
        blk = jnp.zeros((tm, LANES), F32)
        for p in range(4):
            pp = prod[:, LANES * p:LANES * (p + 1)]
            blk = jnp.where(lane == 2 * p, jnp.sum(jnp.where(low, pp, 0.0), axis=1, keepdims=True), blk)
            blk = jnp.where(lane == 2 * p + 1, jnp.sum(jnp.where(low, 0.0, pp), axis=1, keepdims=True), blk)
        qab_ref[...] = _aug_query(cc_ref[...] - lse_ref[...])
        dab_ref[...] = _spread(_split3(-blk), 0).astype(BF16)

        @pl.when(i == nsteps - 1)
        def _():
            dg2_ref[...] = jnp.sum(acc_g[...], axis=0, keepdims=True)

    sh = jax.ShapeDtypeStruct((T, 512), BF16)
    sa = jax.ShapeDtypeStruct((T, LANES), BF16)
    sw = jax.ShapeDtypeStruct((512, D_MODEL), F32)
    whole = lambda shape: pl.BlockSpec(shape, lambda i: (0, 0))
    return pl.pallas_call(
        body, grid=(nsteps,),
        out_shape=(jax.ShapeDtypeStruct((T, 2048), BF16), sh, sh, sa, sa, jax.ShapeDtypeStruct((1, D_MODEL), F32),
                   sw, sw, jax.ShapeDtypeStruct((D_MODEL, D_MODEL), F32)),
        in_specs=[_row_spec(tm, D_MODEL)] * 4 + [_row_spec(tm, 2048), _row_spec(tm, 512), _row_spec(tm, 512),
                  _row_spec(tm, D_MODEL), _row_spec(tm, LANES), _row_spec(tm, LANES),
                  _const_spec((512, D_MODEL)), _const_spec((512, D_MODEL)),
                  _const_spec((D_MODEL, D_MODEL)), _const_spec((1, D_MODEL))],
        out_specs=[_row_spec(tm, 2048), _row_spec(tm, 512), _row_spec(tm, 512), _row_spec(tm, LANES),
                   _row_spec(tm, LANES), whole((1, D_MODEL)), whole((512, D_MODEL)), whole((512, D_MODEL)),
                   whole((D_MODEL, D_MODEL))],
        scratch_shapes=[pltpu.VMEM((SUBLANES, D_MODEL), F32)],
        compiler_params=_params(56), name="bwd_merge",
    )(dh1, o, A, B, gt, ys, ya, mg, lse, cc, wbs, wba, wo, g2)


def _bwd_sgu(zuv, dys, gs, bs, wsp, bT, grads):
    T = zuv.shape[0]
    tc = SGU_TILE
    nc = tc // CHUNK
    nsteps = T // tc
    ex = _GradExchange([tuple(g.shape[1:]) for g in grads])
    ng = ex.n

    def body(z_ref, dy_ref, gs_ref, bs_ref, w_ref, bT_ref, *rest):
        g_refs, (dz_ref, dw_ref, dbT_ref, dgs_ref, dbs_ref) = rest[:ng], rest[ng:ng + 5]
        land1 = rest[ng + 5:2 * ng + 5]
        acc_w, acc_b, acc_gs, acc_bs, dvln_s = rest[2 * ng + 5:2 * ng + 10]
        ex_sems = rest[2 * ng + 10:]
        i = pl.program_id(0)

        @pl.when(i == 0)
        def _():
            ex.start(1, g_refs, land1, ex_sems)
            acc_w[...] = jnp.zeros_like(acc_w)
            acc_b[...] = jnp.zeros_like(acc_b)
            acc_gs[...] = jnp.zeros_like(acc_gs)
            acc_bs[...] = jnp.zeros_like(acc_bs)

        z = z_ref[...].astype(F32)
        gsv = gs_ref[...]
        u, vhat, rs, vln = _sgu_forward_parts(z, gsv, bs_ref[...])
        vb = vln.astype(BF16)
        dy = dy_ref[...].astype(F32)
        low_w = (_iota((CHUNK, nc * LANES), 1) % LANES) < 64
        for p in range(4):
            we, wo, b2 = _sgu_pair_weights(w_ref, bT_ref[...], p)
            vcat = _chunks_on_lanes(vb, p, nc)
            s = _sgu_mix(we, wo, b2, vcat, nc)
            dyc = _chunks_on_lanes(dy, p, nc)
            ds = dyc * _chunks_on_lanes(u, p, nc)
            dsb = ds.astype(BF16)
            zero = jnp.zeros_like(dsb)
            dse = jnp.where(low_w, dsb, zero)
            dso = jnp.where(low_w, zero, dsb)
            acc_w[2 * p] += _dot_nt(dse, vcat)
            acc_w[2 * p + 1] += _dot_nt(dso, vcat)
            acc_b[p] += ds
            dvl = jnp.where(low_w, _dot_tn(we, dsb), _dot_tn(wo, dsb))
            for c in range(nc):
                rows, cols = slice(c * CHUNK, (c + 1) * CHUNK), slice(LANES * p, LANES * (p + 1))
                dvln_s[rows, cols] = dvl[:, c * LANES:(c + 1) * LANES]
                du = dy[rows, cols] * s[:, c * LANES:(c + 1) * LANES]
                dz_ref[rows, cols] = (du * _gelu_grad(z[rows, cols])).astype(BF16)
        dvln = dvln_s[...]
        acc_gs[...] += _fold8(dvln * vhat)
        acc_bs[...] += _fold8(dvln)
        al = dvln * gsv
        dvv = rs * (al - jnp.mean(al, axis=1, keepdims=True) - vhat * jnp.mean(al * vhat, axis=1, keepdims=True))
        dz_ref[:, SGU_WIDTH:] = (dvv * _gelu_grad(z[:, SGU_WIDTH:])).astype(BF16)

        @pl.when(i == nsteps - 1)
        def _():
            tril = _iota((CHUNK, CHUNK), 0) >= _iota((CHUNK, CHUNK), 1)
            lane = _iota((CHUNK, LANES), 1)
            low = lane < 64
            blk = jnp.zeros((CHUNK, LANES), F32)
            for g in range(8):
                dw_ref[g] = jnp.where(tril, acc_w[g], 0.0)
            for p in range(4):
                t = acc_b[p]
                tot = t[:, 0:LANES]
                for c in range(1, nc):
                    tot = tot + t[:, c * LANES:(c + 1) * LANES]
                blk = jnp.where(lane == 2 * p, jnp.sum(jnp.where(low, tot, 0.0), axis=1, keepdims=True), blk)
                blk = jnp.where(lane == 2 * p + 1, jnp.sum(jnp.where(low, 0.0, tot), axis=1, keepdims=True), blk)
            dbT_ref[...] = blk
            dgs_ref[...] = jnp.sum(acc_gs[...], axis=0, keepdims=True)
            dbs_ref[...] = jnp.sum(acc_bs[...], axis=0, keepdims=True)
            ex.wait(1, g_refs, land1, ex_sems)

    whole = lambda shape: pl.BlockSpec(shape, lambda i: (0,) * len(shape))
    hbm_spec = pl.BlockSpec(memory_space=pl.ANY)
    return pl.pallas_call(
        body, grid=(nsteps,),
        out_shape=[jax.ShapeDtypeStruct((T, 1024), BF16), jax.ShapeDtypeStruct((8, CHUNK, CHUNK), F32),
                   jax.ShapeDtypeStruct((CHUNK, LANES), F32), jax.ShapeDtypeStruct((1, SGU_WIDTH), F32),
                   jax.ShapeDtypeStruct((1, SGU_WIDTH), F32)] + ex.land_shapes(1),
        in_specs=[_row_spec(tc, 1024), _row_spec(tc, SGU_WIDTH), _const_spec((1, SGU_WIDTH)),
                  _const_spec((1, SGU_WIDTH)), _const_spec((8, CHUNK, CHUNK)), _const_spec((CHUNK, 8))]
        + [hbm_spec] * ng,
        out_specs=[_row_spec(tc, 1024), whole((8, CHUNK, CHUNK)), whole((CHUNK, LANES)),
                   whole((1, SGU_WIDTH)), whole((1, SGU_WIDTH))] + [hbm_spec] * ng,
        scratch_shapes=[pltpu.VMEM((8, CHUNK, CHUNK), F32), pltpu.VMEM((4, CHUNK, nc * LANES), F32),
                        pltpu.VMEM((SUBLANES, SGU_WIDTH), F32), pltpu.VMEM((SUBLANES, SGU_WIDTH), F32),
                        pltpu.VMEM((tc, SGU_WIDTH), F32)] + ex.sem_shapes(1),
        compiler_params=_params(48), name="bwd_sgu",
    )(zuv, dys, gs, bs, wsp, bT, *grads)


def _bwd_attn(qkv, dya, qab, dab, kaug, parts):
    T = qkv.shape[0]
    tq = tk = ATTN_TILE
    nq = T // tq
    nk = T // tk
    ex = _GradExchange([tuple(g.shape[1:]) for g in parts])
    nr = ex.n

    def body(q_ref, do_ref, qa_ref, da_ref, k_ref, v_ref, ka_ref, *rest):
        part_refs, (dq_ref, dk_ref, dv_ref, dcx_ref) = rest[:nr], rest[nr:nr + 4]
        land2, dq_acc, ex_sems = rest[nr + 4:2 * nr + 4], rest[2 * nr + 4], rest[2 * nr + 5:]
        p = pl.program_id(0)
        j = pl.program_id(1)

        @pl.when((p == 0) & (j == 0))
        def _():
            ex.start(2, part_refs, land2, ex_sems)

        lane = _iota((tq, LANES), 1)
        low = lane < 64
        row = _iota((2 * tq, tk), 0) % tq
        col = _iota((2 * tq, tk), 1)
        first = 2 * AUG_LANES * p
        half = tq // 2

        @pl.when(j == 0)
        def _():
            dq_acc[...] = jnp.zeros_like(dq_acc)

        @pl.when((j == 0) & (p == 0))
        def _():
            dcx_ref[...] = jnp.zeros_like(dcx_ref)

        ka = ka_ref[...]
        kk = jnp.concatenate([k_ref[...], ka], axis=1)
        vv = jnp.concatenate([v_ref[...], ka], axis=1)

        def tile(i, carry, r0, c0, nc, masked):
            dk_a, dv_a = carry
            nr = tq - r0
            qsl = pl.ds(pl.multiple_of(i * tq + r0, half), nr)
            qs = _aug_stack(q_ref[qsl, :], qa_ref[qsl, :], p)
            dos = _aug_stack(do_ref[qsl, :], da_ref[qsl, :], p)
            kc, vc = kk[c0:c0 + nc], vv[c0:c0 + nc]
            s = _dot_nt(qs, kc)
            if masked:
                s = jnp.where(col[:2 * nr, :nc] + c0 <= row[:2 * nr, :nc] % nr + r0, s, NEG)
            pm = jnp.exp(s)
            ds = pm * _dot_nt(dos, vc)
            dsb = ds.astype(BF16)
            dv_u = _dot_tn(pm.astype(BF16), dos[:, :LANES])
            dk_u = _dot_tn(dsb, qs)
            if nc == tk:
                dv_a, dk_a = dv_a + dv_u, dk_a + dk_u
            else:
                pad = lambda u: jnp.concatenate(
                    [jnp.zeros((n, u.shape[1]), F32) if z else u
                     for z, n in ((True, c0), (False, nc), (True, tk - c0 - nc)) if n], axis=0)
                dv_a, dk_a = dv_a + pad(dv_u), dk_a + pad(dk_u)
            dqx = _dot(dsb, kc)
            dq_acc[qsl, :] += jnp.where(low[:nr], dqx[:nr, :LANES], dqx[nr:, :LANES])
            dcx_ref[qsl, :] += (jnp.where(lane[:nr] == first, dqx[:nr, LANES:], 0.0)
                                + jnp.where(lane[:nr] == first + AUG_LANES, dqx[nr:, LANES:], 0.0))
            return dk_a, dv_a

        def q_block(i, carry, masked):
            return tile(i, carry, 0, 0, tk, masked)

        init =(jnp.zeros((tk, 2 * LANES), F32), jnp.zeros((tk, LANES), F32))
        carry = tile(j, init, 0, 0, half, True)
        carry = tile(j, carry, half, half, half, True)
        n_rest = nq - 1 - j

        def trip(t, c):
            for u in range(3):
                c = q_block(j + 1 + 3 * t + u, c, False)
            return c

        carry = lax.fori_loop(0, n_rest // 3, trip, carry)
        dk_a, dv_a = lax.fori_loop(j + 1 + 3 * (n_rest // 3), nq, lambda i, c: q_block(i, c, False), carry)
        dk_ref[...] = dk_a[:, :LANES].astype(BF16)
        dv_ref[...] = dv_a.astype(BF16)
        ksl = pl.ds(pl.multiple_of(j * tk, tk), tk)
        lk = _iota((tk, LANES), 1)
        dcx_ref[ksl, :] += jnp.where((lk == first + 3) | (lk == first + AUG_LANES + 3), dk_a[:, LANES:], 0.0)

        @pl.when(j == nk - 1)
        def _():
            dq_ref[...] = (dq_acc[...] * 0.125).astype(BF16)

        @pl.when((p == 3) & (j == nk - 1))
        def _():
            ex.wait(2, part_refs, land2, ex_sems)

    sh = jax.ShapeDtypeStruct((T, ATTN_WIDTH), BF16)
    full = lambda cb: pl.BlockSpec((T, LANES), lambda p, j: (0, cb + p))
    blk = lambda cb: pl.BlockSpec((tk, LANES), lambda p, j: (j, cb + p))
    hbm_spec = pl.BlockSpec(memory_space=pl.ANY)
    return pl.pallas_call(
        body, grid=(4, nk),
        out_shape=[sh, sh, sh, jax.ShapeDtypeStruct((T, LANES), F32)] + ex.land_shapes(2),
        in_specs=[full(0), full(0), _const_spec((T, LANES)), _const_spec((T, LANES)), blk(4), blk(8),
                  pl.BlockSpec((tk, LANES), lambda p, j: (j, 0))] + [hbm_spec] * nr,
        out_specs=[full(0), blk(0), blk(0), pl.BlockSpec((T, LANES), lambda p, j: (0, 0))] + [hbm_spec] * nr,
        scratch_shapes=[pltpu.VMEM((T, LANES), F32)] + ex.sem_shapes(2),
        compiler_params=_params(58, 2), name="bwd_attn",
    )(qkv, dya, qab, dab, qkv, qkv, kaug, *parts)


def _bwd_cum(dcx, fl, bfp):
    T = fl.shape[0]
    tb = CUM_TILE

    def body(dcx_ref, fl_ref, b_ref, dfl_ref, dbf_ref):
        triu = (_iota((tb, tb), 0) <= _iota((tb, tb), 1)).astype(F32)
        r, c = _iota((LANES, LANES), 0), _iota((LANES, LANES), 1)
        sel = (((r == AUG_LANES * c) & (c < N_HEADS)).astype(F32)
               - ((r == AUG_LANES * c + 3) & (c < N_HEADS)).astype(F32))
        carry = jnp.zeros((1, LANES), F32)
        dbf = jnp.zeros((1, LANES), F32)
        for i in reversed(range(T // tb)):
            colblk = jnp.dot(dcx_ref[i * tb:(i + 1) * tb, :], sel, precision=HIGHEST, preferred_element_type=F32)
            rc = jnp.dot(triu, colblk, precision=HIGHEST, preferred_element_type=F32) + carry
            carry = rc[0:1, :]
            sig = jax.nn.sigmoid(fl_ref[i * tb:(i + 1) * tb, :] + b_ref[...])
            dfl = rc * (1.0 - sig)
            dfl_ref[i * tb:(i + 1) * tb, :] = dfl.astype(BF16)
            dbf = dbf + jnp.sum(dfl, axis=0, keepdims=True)
        dbf_ref[...] = dbf

    return pl.pallas_call(
        body,
        out_shape=(jax.ShapeDtypeStruct((T, LANES), BF16), jax.ShapeDtypeStruct((1, LANES), F32)),
        compiler_params=pltpu.CompilerParams(vmem_limit_bytes=32 * MIB), name="bwd_cum",
    )(dcx, fl, bfp)


def _bwd_in(dz, dq, dk, dv, dfl, dgl, dh1, x2, g1, wz, wf, wg, rows, name, prev=None, stage=0, exchanged=None):
    T = x2.shape[0]
    tm = min(MATMUL_TILE, rows[1] - rows[0])
    first = rows[0] // tm
    nsteps = (rows[1] - rows[0]) // tm
    ex = _GradExchange([tuple(exchanged.shape[1:])]) if stage else None

    def body(dz_ref, dq_ref, dk_ref, dv_ref, dfl_ref, dgl_ref, dh1_ref, x_ref, g_ref, wz_ref, wf_ref, wg_ref, *rest):
        rest = list(rest)
        dx_prev, dg1_prev = (rest.pop(0), rest.pop(0)) if prev else (None, None)
        src_ref = rest.pop(0) if stage else None
        dx_ref, dg1_ref = rest.pop(0), rest.pop(0)
        land_ref = rest.pop(0) if stage else None
        acc_g, ex_sems = rest[0], rest[1:]
        i = pl.program_id(0)

        @pl.when(i == 0)
        def _():
            if stage:
                ex.start(stage, [src_ref], [land_ref], ex_sems)
            acc_g[...] = jnp.zeros_like(acc_g)

        dxn = _dot_nt(dz_ref[...], wz_ref[:, 0:1024])
        dxn = dxn + _dot_nt(dq_ref[...], wz_ref[:, 1024:1536])
        dxn = dxn + _dot_nt(dk_ref[...], wz_ref[:, 1536:2048])
        dxn = dxn + _dot_nt(dv_ref[...], wz_ref[:, 2048:2560])
        dxn = dxn + _dot_nt(dfl_ref[...], wf_ref[...])
        dxn = dxn + _dot_nt(dgl_ref[...], wg_ref[...])
        r1, xh = _rms_stats(x_ref[...])
        acc_g[...] += _fold8(dxn * xh)
        dx_ref[...] = dh1_ref[...] + _rms_bwd(dxn, xh, r1, g_ref[...])

        @pl.when(i == nsteps - 1)
        def _():
            total = jnp.sum(acc_g[...], axis=0, keepdims=True)
            dg1_ref[...] = total + dg1_prev[...] if prev else total
            if stage:
                ex.wait(stage, [src_ref], [land_ref], ex_sems)

    hbm_spec = pl.BlockSpec(memory_space=pl.ANY)
    rows_spec = lambda n: pl.BlockSpec((tm, n), lambda i: (i + first, 0))
    operands = [dz, dq, dk, dv, dfl, dgl, dh1, x2, g1, wz, wf, wg]
    in_specs = [rows_spec(1024), rows_spec(512), rows_spec(512), rows_spec(512), rows_spec(LANES), rows_spec(2048),
                rows_spec(D_MODEL), rows_spec(D_MODEL),
                _const_spec((1, D_MODEL)), _const_spec((D_MODEL, ZQKV_WIDTH)), _const_spec((D_MODEL, LANES)),
                _const_spec((D_MODEL, 2048))]
    aliases = {}
    if prev:
        aliases = {len(operands): 0}
        operands += list(prev)
        in_specs += [hbm_spec, _const_spec((1, D_MODEL))]
    if stage:
        operands.append(exchanged)
        in_specs.append(hbm_spec)
    return pl.pallas_call(
        body, grid=(nsteps,),
        out_shape=[jax.ShapeDtypeStruct((T, D_MODEL), F32), jax.ShapeDtypeStruct((1, D_MODEL), F32)]
        + (ex.land_shapes(stage) if stage else []),
        in_specs=in_specs,
        out_specs=[rows_spec(D_MODEL), pl.BlockSpec((1, D_MODEL), lambda i: (0, 0))] + ([hbm_spec] if stage else []),
        scratch_shapes=[pltpu.VMEM((SUBLANES, D_MODEL), F32)] + (ex.sem_shapes(stage) if stage else []),
        input_output_aliases=aliases,
        compiler_params=_params(48), name=name,
    )(*operands)


def _small_kernel_shapes(g_mix_pre, b_forget, g_sgu, b_sgu, w_spatial, b_spatial, g_mix_post, g_ffn_pre, g_ffn_post):
    return dict(g_mix_pre=g_mix_pre, b_forget=jnp.pad(b_forget, ((0, 0), (0, LANES - N_HEADS))), g_sgu=g_sgu,
                b_sgu=b_sgu, w_spatial=w_spatial[0], b_spatial=b_spatial[0], g_mix_post=g_mix_post,
                g_ffn_pre=g_ffn_pre, g_ffn_post=g_ffn_post)


def _small_output_shapes(d):
    out = dict(d)
    out.update(b_forget=d["b_forget"][:, :N_HEADS], w_spatial=d["w_spatial"][None], b_spatial=d["b_spatial"][None])
    return out


def kernel(x, g_mix_pre, w_in, b_forget, g_sgu, b_sgu, w_spatial, b_spatial, w_branch_sgu, w_branch_attn, w_out, g_mix_post, g_ffn_pre, w_up, w_down, g_ffn_post, loss_target, m_g_mix_pre, m_w_in, m_b_forget, m_g_sgu, m_b_sgu, m_w_spatial, m_b_spatial, m_w_branch_sgu, m_w_branch_attn, m_w_out, m_g_mix_post, m_g_ffn_pre, m_w_up, m_w_down, m_g_ffn_post, v_g_mix_pre, v_w_in, v_b_forget, v_g_sgu, v_b_sgu, v_w_spatial, v_b_spatial, v_w_branch_sgu, v_w_branch_attn, v_w_out, v_g_mix_post, v_g_ffn_pre, v_w_up, v_w_down, v_g_ffn_post):
    T = x.shape[1]
    x2 = x.reshape(T, D_MODEL)
    tgt = loss_target.reshape(T, D_MODEL)

    wg_in = _gather_w_in(w_in[0])
    wz, wf, wgt = _assemble_w_in(wg_in)
    bfp = jnp.pad(b_forget, ((0, 0), (0, LANES - N_HEADS)))
    wsp = w_spatial[0]
    bT = b_spatial[0].T

    xn, zuv, qkv, fl, gt = _fwd_in(x2, g_mix_pre, wz, wf, wgt)
    cc, qaug, kaug = _fwd_cum(fl, bfp)
    ys = _fwd_sgu(zuv, g_sgu, b_sgu, wsp, bT)
    ya, lse, wbs, wba, wo, wup, wdn = _fwd_attn(
        qkv, qaug, kaug, (w_branch_sgu[0], w_branch_attn[0], w_out[0], w_up[0], w_down[0]))
    A, B, mg, o, h1 = _fwd_merge(ys, ya, gt, x2, wbs, wba, wo, g_mix_post)
    xn2, a, ddn, dy, loss_part, dg4 = _fwd_ffn_loss(h1, tgt, wup, wdn, g_ffn_pre, g_ffn_post)

    da, dh1, dg3 = _bwd_ffn(ddn, a, dy, h1, wup, wdn, g_ffn_pre)
    dw_up = _wgrad(xn2, da, "wgrad_up", tn=2048, block_cols=512)
    dw_down = _wgrad(a, ddn, "wgrad_down", relu2=True)
    dgl, dys, dya, qab, dab, dg2, dw_bs, dw_ba, dw_out = _bwd_merge(
        dh1, o, A, B, gt, ys, ya, mg, lse, cc, wbs, wba, wo, g_mix_post)
    col_blocks = lambda g, w: g.reshape(g.shape[0], N_DEV, w).transpose(1, 0, 2)
    row_blocks = lambda g, r: g.reshape(N_DEV, r, g.shape[1])
    early_names = ["w_branch_sgu", "w_branch_attn", "w_out", "w_up", "w_down"]
    early = [col_blocks(dw_bs, 128), col_blocks(dw_ba, 128), row_blocks(dw_out, 128), dw_up, row_blocks(dw_down, 512)]
    owners = _owner_indices()
    dzuv, dwsp, dbT, dgs, dbs, *early_land1 = _bwd_sgu(zuv, dys, g_sgu, b_sgu, wsp, bT, early)
    early_parts = [_chip_partials(g, l1, owners, "chip_partials_" + nm)
                   for g, l1, nm in zip(early, early_land1, early_names)]
    dq, dk, dv, dcx, *early_land2 = _bwd_attn(qkv, dya, qab, dab, kaug, early_parts)
    dfl, dbf = _bwd_cum(dcx, fl, bfp)
    dw_z = _wgrad(xn, dzuv, "wgrad_in_z")
    dw_q, dw_k, dw_v, dw_f = _wgrad_multi(xn, [dq, dk, dv, dfl], "wgrad_in_qkvf")
    dw_g = _wgrad(xn, dgl, "wgrad_in_gate")
    blocks_in = _block_dw_in([dw_z, dw_q, dw_k, dw_v, dw_f, dw_g])
    bwd_in_args = (dzuv, dq, dk, dv, dfl, dgl, dh1, x2, g_mix_pre, wz, wf, wgt)
    dx, dg1, land1_in = _bwd_in(*bwd_in_args, (0, T // 4), "bwd_in_a", stage=1, exchanged=blocks_in)
    part_in = _chip_partials(blocks_in, land1_in, owners, "chip_partials_w_in")
    dx, dg1, land2_in = _bwd_in(*bwd_in_args, (T // 4, 3 * T // 4), "bwd_in_b", prev=(dx, dg1), stage=2,
                                exchanged=part_in)
    dx, dg1 = _bwd_in(*bwd_in_args, (3 * T // 4, T), "bwd_in_c", prev=(dx, dg1))

    tot_a, tot_b = _allreduce_small(dict(
        g_mix_pre=dg1, b_forget=dbf, g_sgu=dgs, b_sgu=dbs, w_spatial=dwsp, b_spatial=dbT, g_mix_post=dg2,
        g_ffn_pre=dg3, g_ffn_post=dg4), loss_part)
    r0, nr, c0, nc = LOSS_SLOT
    loss = jnp.sum(tot_a[r0:r0 + nr, c0:c0 + nc])
    small_w = _small_kernel_shapes(g_mix_pre, b_forget, g_sgu, b_sgu, w_spatial, b_spatial, g_mix_post, g_ffn_pre,
                                   g_ffn_post)
    small_m = _small_kernel_shapes(m_g_mix_pre, m_b_forget, m_g_sgu, m_b_sgu, m_w_spatial, m_b_spatial, m_g_mix_post,
                                   m_g_ffn_pre, m_g_ffn_post)
    small_v = _small_kernel_shapes(v_g_mix_pre, v_b_forget, v_g_sgu, v_b_sgu, v_w_spatial, v_b_spatial, v_g_mix_post,
                                   v_g_ffn_pre, v_g_ffn_post)
    sg, sd, sm, sv = (_small_output_shapes(d) for d in _adamw_small(tot_a, tot_b, small_w, small_m, small_v))

    big = {}
    g_in = _reduced_grad(blocks_in, land1_in, land2_in, owners, "reduced_grad_w_in")[:, :IN_SHARD]
    d_, m_, v_ = _adamw(w_in[0], g_in, m_w_in[0], v_w_in[0], "adamw_w_in")
    big["w_in"] = (g_in[None], d_[None], m_[None], v_[None])
    early_wmv = [(w_branch_sgu, m_w_branch_sgu, v_w_branch_sgu), (w_branch_attn, m_w_branch_attn, v_w_branch_attn),
                 (w_out, m_w_out, v_w_out), (w_up, m_w_up, v_w_up), (w_down, m_w_down, v_w_down)]
    for nm, (w, m, v), g, l1, l2 in zip(early_names, early_wmv, early, early_land1, early_land2):
        big[nm] = tuple(t[None] for t in _adamw_reduced(w[0], m[0], v[0], g, l1, l2, owners, "adamw_" + nm))

    order = ["g_mix_pre", "w_in", "b_forget", "g_sgu", "b_sgu", "w_spatial", "b_spatial", "w_branch_sgu",
             "w_branch_attn", "w_out", "g_mix_post", "g_ffn_pre", "w_up", "w_down", "g_ffn_post"]
    outs = [loss, dx.reshape(1, T, D_MODEL)]
    for kind, small in enumerate((sg, sd, sm, sv)):
        outs += [big[nm][kind] if nm in big else small[nm] for nm in order]
    return tuple(outs)
```

```python
import math

import jax
import jax.numpy as jnp
from jax import lax
from jax.experimental import pallas as pl
from jax.experimental.pallas import tpu as pltpu

F32 = jnp.float32
BF16 = jnp.bfloat16
HIGHEST = lax.Precision.HIGHEST
MESH = pl.DeviceIdType.MESH

D_MODEL = 1024
SGU_WIDTH = 512
ATTN_WIDTH = 512
N_HEADS = 8
CHUNK = 128
D_FF = 4096
IN_WIDTH = 4616
N_DEV = 8
IN_SHARD = IN_WIDTH // N_DEV
IN_SHARD_PAD = 640
ZQKV_WIDTH = 2 * SGU_WIDTH + 3 * ATTN_WIDTH
GATE_OFFSET = ZQKV_WIDTH + N_HEADS
EPS = 1e-6
LANES = 128
SUBLANES = 8
VMEM_BYTES = 64 * 1024 * 1024
MIB = 1024 * 1024

ADAM_LR = 0.001
ADAM_B1 = 0.9
ADAM_B2 = 0.999
ADAM_EPS = 1e-08
ADAM_WD = 0.01
ADAM_STEP = 10

TOKEN_TILE = 256
MATMUL_TILE = 512
MERGE_BWD_TILE = 512
ATTN_TILE = 512
CUM_TILE = 256
SGU_TILE = 512
WGRAD_TILE = 1024
NEG = -1e30

NT_DIMS = (((1,), (1,)), ((), ()))
TN_DIMS = (((0,), (0,)), ((), ()))


def _params(vmem_mb, n_grid=1):
    return pltpu.CompilerParams(
        dimension_semantics=("arbitrary",) * n_grid,
        vmem_limit_bytes=min(vmem_mb * MIB, VMEM_BYTES - 6 * MIB),
    )


def _dot(a, b):
    return jnp.dot(a, b, preferred_element_type=F32)


def _dot_nt(a, b):
    return lax.dot_general(a, b, NT_DIMS, preferred_element_type=F32)


def _dot_tn(a, b):
    return lax.dot_general(a, b, TN_DIMS, preferred_element_type=F32)


def _const_spec(shape):
    nd = len(shape)
    return pl.BlockSpec(shape, lambda *_: (0,) * nd, pipeline_mode=pl.Buffered(1))


def _row_spec(tm, n, col=0):
    return pl.BlockSpec((tm, n), lambda i: (i, col))


def _fold8(v):
    return v.reshape(v.shape[0] // SUBLANES, SUBLANES, v.shape[1]).sum(axis=0)


def _pick(v, lane_iota, k):
    return jnp.sum(jnp.where(lane_iota == k, v, 0.0), axis=1, keepdims=True)


def _iota(shape, dim):
    return lax.broadcasted_iota(jnp.int32, shape, dim)


def _gelu(x):
    c = 0.7978845608028654
    return 0.5 * x * (1.0 + jnp.tanh(c * (x + 0.044715 * x * x * x)))


def _gelu_grad(x):
    c = 0.7978845608028654
    t = jnp.tanh(c * (x + 0.044715 * x * x * x))
    return 0.5 * (1.0 + t) + 0.5 * x * (1.0 - t * t) * (c * (1.0 + 3.0 * 0.044715 * x * x))


def _rms_stats(v):
    r = lax.rsqrt(jnp.mean(v * v, axis=1, keepdims=True) + EPS)
    return r, v * r


def _rms_bwd(dout, vhat, r, g):
    a = dout * g
    return r * (a - vhat * jnp.mean(a * vhat, axis=1, keepdims=True))


def _mesh_pos():
    return lax.axis_index("x"), lax.axis_index("y"), lax.axis_index("c")


def _dev_index(px, py, pc):
    return 4 * px + 2 * py + pc


def _other_chips(x, y):
    return [(1 - x, y), (x, 1 - y), (1 - x, 1 - y)]


class _WeightGather:
    def __init__(self, shard_shapes, kinds, stage_shapes=None):
        self.shard_shapes = list(shard_shapes)
        self.kinds = list(kinds)
        self.stage_shapes = list(stage_shapes or shard_shapes)
        self.n = len(self.kinds)

    def out_shapes(self):
        shapes = []
        for (rows, cols), kind in zip(self.stage_shapes, self.kinds):
            full = {"block": (N_DEV, rows, cols), "rows": (N_DEV * rows, cols), "cols": (rows, N_DEV * cols)}[kind]
            shapes.append(jax.ShapeDtypeStruct(full, BF16))
        return shapes

    def scratch_shapes(self):
        return ([pltpu.VMEM(s, BF16) for s in self.stage_shapes]
                + [pltpu.SemaphoreType.DMA((self.n, 7)), pltpu.SemaphoreType.DMA((self.n, 7)),
                   pltpu.SemaphoreType.DMA((self.n,))])

    def _view(self, a, ref, j):
        rows, cols = self.stage_shapes[a]
        if self.kinds[a] == "block":
            return ref.at[j]
        if self.kinds[a] == "rows":
            return ref.at[pl.ds(pl.multiple_of(j * rows, rows), rows), :]
        return ref.at[:, pl.ds(pl.multiple_of(j * cols, cols), cols)]

    def _copy(self, outs, scratch, a, k, block, to, from_stage=False):
        stage, (send_sems, recv_sems, _) = scratch[:self.n], scratch[self.n:]
        dst = self._view(a, outs[a], _dev_index(*block))
        return pltpu.make_async_remote_copy(
            src_ref=stage[a] if from_stage else dst, dst_ref=dst,
            send_sem=send_sems.at[a, k], recv_sem=recv_sems.at[a, k],
            device_id=to, device_id_type=MESH)

    def _local(self, outs, scratch, a, me):
        return pltpu.make_async_copy(scratch[a], self._view(a, outs[a], _dev_index(*me)), scratch[-1].at[a])

    def start(self, ins, outs, scratch):
        x, y, c = _mesh_pos()
        me, sibling = (x, y, c), (x, y, 1 - c)
        for a in range(self.n):
            rows, cols = self.shard_shapes[a]
            if self.stage_shapes[a] != self.shard_shapes[a]:
                scratch[a][...] = jnp.zeros(self.stage_shapes[a], BF16)
            scratch[a][0:rows, 0:cols] = ins[a][...].astype(BF16)
            self._local(outs, scratch, a, me).start()
        for a in range(self.n):
            self._copy(outs, scratch, a, 0, me, sibling, True).start()
            for j, chip in enumerate(_other_chips(x, y)):
                self._copy(outs, scratch, a, 1 + j, me, (*chip, c), True).start()

    def forward(self, outs, scratch):
        x, y, c = _mesh_pos()
        me, sibling = (x, y, c), (x, y, 1 - c)
        for a in range(self.n):
            for j, chip in enumerate(_other_chips(x, y)):
                self._copy(outs, scratch, a, 1 + j, (*chip, c), me).wait_recv()
                self._copy(outs, scratch, a, 4 + j, (*chip, c), sibling).start()

    def finish(self, outs, scratch):
        x, y, c = _mesh_pos()
        me, sibling = (x, y, c), (x, y, 1 - c)
        chips = _other_chips(x, y)
        for a in range(self.n):
            self._copy(outs, scratch, a, 0, sibling, me).wait_recv()
            for j, chip in enumerate(chips):
                self._copy(outs, scratch, a, 4 + j, (*chip, 1 - c), me).wait_recv()
        for a in range(self.n):
            self._copy(outs, scratch, a, 0, me, sibling, True).wait_send()
            for j, chip in enumerate(chips):
                self._copy(outs, scratch, a, 1 + j, me, (*chip, c), True).wait_send()
                self._copy(outs, scratch, a, 4 + j, (*chip, c), sibling).wait_send()
            self._local(outs, scratch, a, me).wait()


def _gather_w_in(w_in_local):
    g = _WeightGather([(D_MODEL, IN_SHARD)], ["block"], [(D_MODEL, IN_SHARD_PAD)])

    def body(w_ref, out_ref, *scratch):
        g.start([w_ref], [out_ref], scratch)
        g.forward([out_ref], scratch)
        g.finish([out_ref], scratch)

    return pl.pallas_call(
        body,
        out_shape=g.out_shapes()[0],
        in_specs=[pl.BlockSpec(memory_space=pltpu.VMEM)],
        out_specs=pl.BlockSpec(memory_space=pl.ANY),
        scratch_shapes=g.scratch_shapes(),
        compiler_params=pltpu.CompilerParams(vmem_limit_bytes=32 * MIB),
        name="gather_w_in",
    )(w_in_local)


class _GradExchange:
    def __init__(self, shapes):
        self.shapes = [tuple(s) for s in shapes]
        self.n = len(self.shapes)

    def land_shapes(self, stage):
        slots, dtype = (4, F32) if stage == 1 else (3, BF16)
        return [jax.ShapeDtypeStruct((slots,) + s, dtype) for s in self.shapes]

    def sem_shapes(self, stage):
        slots = 4 if stage == 1 else 3
        return [pltpu.SemaphoreType.DMA((self.n, slots)), pltpu.SemaphoreType.DMA((self.n, slots))]

    def _copy(self, stage, srcs, lands, sems, a, k):
        x, y, c = _mesh_pos()
        cx, cy = (_other_chips(x, y) + [(x, y)])[k]
        if stage == 1:
            src, to = srcs[a].at[_dev_index(cx, cy, 1 - c)], (x, y, 1 - c)
        else:
            src, to = srcs[a].at[k], (cx, cy, c)
        return pltpu.make_async_remote_copy(
            src_ref=src, dst_ref=lands[a].at[k], send_sem=sems[0].at[a, k], recv_sem=sems[1].at[a, k],
            device_id=to, device_id_type=MESH)

    def start(self, stage, srcs, lands, sems):
        for a in range(self.n):
            for k in range(4 if stage == 1 else 3):
                self._copy(stage, srcs, lands, sems, a, k).start()

    def wait(self, stage, srcs, lands, sems):
        for a in range(self.n):
            for k in range(4 if stage == 1 else 3):
                cp = self._copy(stage, srcs, lands, sems, a, k)
                cp.wait_recv()
                cp.wait_send()


def _owner_indices():
    x, y, c = _mesh_pos()
    return jnp.stack([_dev_index(cx, cy, c) for cx, cy in _other_chips(x, y) + [(x, y)]]).astype(jnp.int32)


def _chip_partials(g, land1, idx, name):
    _, rows, cols = g.shape
    tr = min(rows, 256)

    def body(idx_ref, g_ref, l_ref, o_ref):
        o_ref[...] = (g_ref[...] + l_ref[...]).astype(BF16)

    return pl.pallas_call(
        body,
        grid_spec=pltpu.PrefetchScalarGridSpec(
            num_scalar_prefetch=1, grid=(3, rows // tr),
            in_specs=[pl.BlockSpec((None, tr, cols), lambda k, r, idx: (idx[k], r, 0)),
                      pl.BlockSpec((None, tr, cols), lambda k, r, idx: (k, r, 0))],
            out_specs=pl.BlockSpec((None, tr, cols), lambda k, r, idx: (k, r, 0))),
        out_shape=jax.ShapeDtypeStruct((3, rows, cols), BF16),
        compiler_params=_params(32, 2), name=name,
    )(idx, g, land1)


def _reduced_block(g_ref, l1_ref, a_ref, b_ref, c_ref):
    return ((g_ref[...] + l1_ref[...]) + a_ref[...].astype(F32)) + b_ref[...].astype(F32) + c_ref[...].astype(F32)


def _reduced_specs(tm, cols):
    return [pl.BlockSpec((None, tm, cols), lambda i, idx: (idx[3], i, 0)),
            pl.BlockSpec((None, tm, cols), lambda i, idx: (3, i, 0)),
            pl.BlockSpec((None, tm, cols), lambda i, idx: (0, i, 0)),
            pl.BlockSpec((None, tm, cols), lambda i, idx: (1, i, 0)),
            pl.BlockSpec((None, tm, cols), lambda i, idx: (2, i, 0))]


def _reduced_grad(g, land1, land2, idx, name):
    _, rows, cols = g.shape
    tm = min(rows, 256)

    def body(idx_ref, g_ref, l1_ref, a_ref, b_ref, c_ref, o_ref):
        o_ref[...] = _reduced_block(g_ref, l1_ref, a_ref, b_ref, c_ref)

    return pl.pallas_call(
        body,
        grid_spec=pltpu.PrefetchScalarGridSpec(
            num_scalar_prefetch=1, grid=(rows // tm,), in_specs=_reduced_specs(tm, cols),
            out_specs=pl.BlockSpec((tm, cols), lambda i, idx: (i, 0))),
        out_shape=jax.ShapeDtypeStruct((rows, cols), F32),
        compiler_params=_params(32), name=name,
    )(idx, g, land1, land2, land2, land2)


def _adamw_math(w, g, m, v):
    m = ADAM_B1 * m + (1.0 - ADAM_B1) * g
    v = ADAM_B2 * v + (1.0 - ADAM_B2) * (g * g)
    m_hat = m / (1.0 - ADAM_B1 ** ADAM_STEP)
    v_hat = v / (1.0 - ADAM_B2 ** ADAM_STEP)
    delta = -ADAM_LR * (m_hat / (jnp.sqrt(v_hat) + ADAM_EPS) + ADAM_WD * w)
    return delta, m, v


SMALL_NAMES = ("g_mix_pre", "b_forget", "g_sgu", "b_sgu", "w_spatial", "b_spatial", "g_mix_post", "g_ffn_pre",
               "g_ffn_post")
SMALL_SLOTS = {"g_mix_pre": (0, 1, 0, 1024), "g_mix_post": (1, 1, 0, 1024), "g_ffn_pre": (2, 1, 0, 1024),
               "g_ffn_post": (3, 1, 0, 1024), "g_sgu": (4, 1, 0, 512), "b_sgu": (4, 1, 512, 512),
               "b_forget": (5, 1, 0, 128), "b_spatial": (8, 8, 0, 128)}
SMALL_TILE = (16, 1024)
SPATIAL_TILE = (N_HEADS * CHUNK, CHUNK)


LOSS_SLOT = (8, 8, 128, 128)


def _allreduce_small(grads, loss_part):
    names = list(SMALL_NAMES)

    def body(*refs):
        g = dict(zip(names, refs[:len(names)]))
        loss_ref = refs[len(names)]
        tot_a, tot_b, buf_a, buf_b, sib_a, sib_b, ps_a, ps_b, land_a, land_b, send_sems, recv_sems = refs[len(names) + 1:]
        x, y, c = _mesh_pos()
        buf_a[...] = jnp.zeros(SMALL_TILE, F32)
        r0, nr, c0, nc = LOSS_SLOT
        buf_a[r0:r0 + nr, c0:c0 + nc] = loss_ref[...]
        for name, (r0, nr, c0, nc) in SMALL_SLOTS.items():
            val = g[name][...]
            if name == "b_spatial":
                val = jnp.transpose(val)[0:N_HEADS, :]
            buf_a[r0:r0 + nr, c0:c0 + nc] = val
        buf_b[...] = g["w_spatial"][...].reshape(SPATIAL_TILE)

        def swap(k, src, dst, to):
            return pltpu.make_async_remote_copy(src_ref=src, dst_ref=dst, send_sem=send_sems.at[k],
                                                recv_sem=recv_sems.at[k], device_id=to, device_id_type=MESH)

        first = [swap(0, buf_a, sib_a, (x, y, 1 - c)), swap(1, buf_b, sib_b, (x, y, 1 - c))]
        for cp in first:
            cp.start()
        for cp in first:
            cp.wait_recv()
        ps_a[...] = buf_a[...] + sib_a[...]
        ps_b[...] = buf_b[...] + sib_b[...]
        second = []
        for k, (cx, cy) in enumerate(_other_chips(x, y)):
            second += [swap(2 + 2 * k, ps_a, land_a.at[k], (cx, cy, c)), swap(3 + 2 * k, ps_b, land_b.at[k], (cx, cy, c))]
        for cp in second:
            cp.start()
        for cp in second:
            cp.wait_recv()
        tot_a[...] = (ps_a[...] + land_a[0]) + (land_a[1] + land_a[2])
        tot_b[...] = (ps_b[...] + land_b[0]) + (land_b[1] + land_b[2])
        for cp in first + second:
            cp.wait_send()

    vm = pl.BlockSpec(memory_space=pltpu.VMEM)
    return pl.pallas_call(
        body,
        out_shape=(jax.ShapeDtypeStruct(SMALL_TILE, F32), jax.ShapeDtypeStruct(SPATIAL_TILE, F32)),
        in_specs=[vm] * (len(names) + 1), out_specs=[vm, vm],
        scratch_shapes=[pltpu.VMEM(SMALL_TILE, F32), pltpu.VMEM(SPATIAL_TILE, F32)] * 3
        + [pltpu.VMEM((3,) + SMALL_TILE, F32), pltpu.VMEM((3,) + SPATIAL_TILE, F32),
           pltpu.SemaphoreType.DMA((8,)), pltpu.SemaphoreType.DMA((8,))],
        compiler_params=pltpu.CompilerParams(vmem_limit_bytes=32 * MIB),
        name="allreduce_small",
    )(*[grads[nm] for nm in names], loss_part)


def _adamw_small(tot_a, tot_b, ws, ms, vs):
    names = list(SMALL_NAMES)
    n = len(names)

    def body(a_ref, b_ref, *refs):
        w, m, v = (dict(zip(names, refs[i * n:(i + 1) * n])) for i in range(3))
        outs = [dict(zip(names, refs[(3 + i) * n:(4 + i) * n])) for i in range(4)]
        for name in names:
            if name == "w_spatial":
                g = b_ref[...].reshape(N_HEADS, CHUNK, CHUNK)
            else:
                r0, nr, c0, nc = SMALL_SLOTS[name]
                g = a_ref[r0:r0 + nr, c0:c0 + nc]
            vals = (g,) + _adamw_math(w[name][...], g, m[name][...], v[name][...])
            for out, val in zip(outs, vals):
                out[name][...] = val

    shapes = [jax.ShapeDtypeStruct(ws[nm].shape, F32) for nm in names]
    vm = pl.BlockSpec(memory_space=pltpu.VMEM)
    res = pl.pallas_call(
        body, out_shape=shapes * 4, in_specs=[vm] * (2 + 3 * n), out_specs=[vm] * (4 * n),
        compiler_params=pltpu.CompilerParams(vmem_limit_bytes=32 * MIB), name="adamw_small",
    )(tot_a, tot_b, *[d[nm] for d in (ws, ms, vs) for nm in names])
    return [dict(zip(names, res[i * n:(i + 1) * n])) for i in range(4)]


def _adamw_reduced(w, m, v, g, land1, land2, idx, name):
    rows, cols = w.shape
    tm = min(rows, 256)

    def body(idx_ref, w_ref, m_ref, v_ref, g_ref, l1_ref, a_ref, b_ref, c_ref, go_ref, d_ref, nm_ref, nv_ref):
        gsum = _reduced_block(g_ref, l1_ref, a_ref, b_ref, c_ref)
        go_ref[...] = gsum
        delta, nm, nv = _adamw_math(w_ref[...], gsum, m_ref[...], v_ref[...])
        d_ref[...] = delta
        nm_ref[...] = nm
        nv_ref[...] = nv

    sd = jax.ShapeDtypeStruct((rows, cols), F32)
    spec = pl.BlockSpec((tm, cols), lambda i, idx: (i, 0))
    return pl.pallas_call(
        body,
        grid_spec=pltpu.PrefetchScalarGridSpec(
            num_scalar_prefetch=1, grid=(rows // tm,), in_specs=[spec] * 3 + _reduced_specs(tm, cols),
            out_specs=[spec] * 4),
        out_shape=(sd, sd, sd, sd),
        compiler_params=_params(32), name=name,
    )(idx, w, m, v, g, land1, land2, land2, land2)


def _adamw(w, g, m, v, name):
    rows, cols = w.shape
    tm = 256 if rows % 256 == 0 else rows

    def body(w_ref, g_ref, m_ref, v_ref, d_ref, nm_ref, nv_ref):
        delta, nm, nv = _adamw_math(w_ref[...], g_ref[...], m_ref[...], v_ref[...])
        d_ref[...] = delta
        nm_ref[...] = nm
        nv_ref[...] = nv

    sd = jax.ShapeDtypeStruct((rows, cols), F32)
    spec = _row_spec(tm, cols)
    return pl.pallas_call(
        body, grid=(rows // tm,), out_shape=(sd, sd, sd), in_specs=[spec] * 4, out_specs=[spec] * 3,
        compiler_params=_params(32), name=name,
    )(w, g, m, v)


def _virtual_slab(sources, v0, v_end, like):
    lane = _iota(like.shape, 1)
    out = jnp.zeros(like.shape, like.dtype)
    for v_start, v_stop, read in sources:
        a, b = max(v0, v_start), min(v0 + LANES, v_stop, v_end)
        while a < b:
            c = a - v_start
            n = min(b - a, LANES - c % LANES)
            piece = read(c // LANES)
            shift = (a - v0 - c % LANES) % LANES
            if shift:
                piece = pltpu.roll(piece, shift, 1)
            out = jnp.where((lane >= a - v0) & (lane < a - v0 + n), piece, out)
            a += n
    return out


def _assemble_w_in(wg_in):
    tm = TOKEN_TILE

    def body(src_ref, wz_ref, wf_ref, wg_ref):
        like = src_ref[0, :, 0:LANES]
        sources = [(IN_SHARD * j, IN_SHARD * (j + 1),
                    (lambda k, j=j: src_ref[j, :, LANES * k:LANES * (k + 1)])) for j in range(N_DEV)]
        for k in range(ZQKV_WIDTH // LANES):
            wz_ref[:, LANES * k:LANES * (k + 1)] = _virtual_slab(sources, LANES * k, ZQKV_WIDTH, like)
        wf_ref[...] = _virtual_slab(sources, ZQKV_WIDTH, GATE_OFFSET, like)
        for k in range(2 * D_MODEL // LANES):
            wg_ref[:, LANES * k:LANES * (k + 1)] = _virtual_slab(sources, GATE_OFFSET + LANES * k, IN_WIDTH, like)

    return pl.pallas_call(
        body, grid=(D_MODEL // tm,),
        out_shape=(jax.ShapeDtypeStruct((D_MODEL, ZQKV_WIDTH), BF16), jax.ShapeDtypeStruct((D_MODEL, LANES), BF16),
                   jax.ShapeDtypeStruct((D_MODEL, 2 * D_MODEL), BF16)),
        in_specs=[pl.BlockSpec((N_DEV, tm, IN_SHARD_PAD), lambda i: (0, i, 0))],
        out_specs=[_row_spec(tm, ZQKV_WIDTH), _row_spec(tm, LANES), _row_spec(tm, 2 * D_MODEL)],
        compiler_params=_params(32), name="assemble_w_in",
    )(wg_in)


def _block_dw_in(pieces):
    tm = TOKEN_TILE
    widths = [2 * SGU_WIDTH, ATTN_WIDTH, ATTN_WIDTH, ATTN_WIDTH, N_HEADS, 2 * D_MODEL]
    starts = [sum(widths[:k]) for k in range(len(widths))]

    def body(*refs):
        in_refs, out_ref = refs[:-1], refs[-1]
        like = in_refs[0][:, 0:LANES]
        slab = lambda ref: (lambda k: ref[:, LANES * k:LANES * (k + 1)])
        sources = [(s, s + w, slab(ref)) for s, w, ref in zip(starts, widths, in_refs)]
        for j in range(N_DEV):
            for k in range(IN_SHARD_PAD // LANES):
                out_ref[j, :, LANES * k:LANES * (k + 1)] = _virtual_slab(
                    sources, IN_SHARD * j + LANES * k, IN_SHARD * (j + 1), like)

    return pl.pallas_call(
        body, grid=(D_MODEL // tm,),
        out_shape=jax.ShapeDtypeStruct((N_DEV, D_MODEL, IN_SHARD_PAD), F32),
        in_specs=[_row_spec(tm, pc.shape[1]) for pc in pieces],
        out_specs=pl.BlockSpec((N_DEV, tm, IN_SHARD_PAD), lambda i: (0, i, 0)),
        compiler_params=_params(32), name="block_dw_in",
    )(*pieces)


def _fwd_in(x2, g1, wz, wf, wg):
    T = x2.shape[0]
    tm = MATMUL_TILE

    def body(x_ref, g_ref, wz_ref, wf_ref, wg_ref, xn_ref, zuv_ref, qkv_ref, fl_ref, gt_ref):
        x = x_ref[...]
        r, xh = _rms_stats(x)
        xn = (xh * g_ref[...]).astype(BF16)
        xn_ref[...] = xn
        zuv_ref[...] = _dot(xn, wz_ref[:, 0:1024]).astype(BF16)
        qkv_ref[:, 0:512] = (_dot(xn, wz_ref[:, 1024:1536]) * 0.125).astype(BF16)
        qkv_ref[:, 512:1536] = _dot(xn, wz_ref[:, 1536:2560]).astype(BF16)
        fl_ref[...] = _dot(xn, wf_ref[...])
        gt_ref[...] = jax.nn.sigmoid(_dot(xn, wg_ref[...])).astype(BF16)

    return pl.pallas_call(
        body, grid=(T // tm,),
        out_shape=(jax.ShapeDtypeStruct((T, D_MODEL), BF16), jax.ShapeDtypeStruct((T, 1024), BF16),
                   jax.ShapeDtypeStruct((T, 1536), BF16), jax.ShapeDtypeStruct((T, LANES), F32),
                   jax.ShapeDtypeStruct((T, 2048), BF16)),
        in_specs=[_row_spec(tm, D_MODEL), _const_spec((1, D_MODEL)), _const_spec((D_MODEL, ZQKV_WIDTH)),
                  _const_spec((D_MODEL, LANES)), _const_spec((D_MODEL, 2048))],
        out_specs=[_row_spec(tm, D_MODEL), _row_spec(tm, 1024), _row_spec(tm, 1536), _row_spec(tm, LANES),
                   _row_spec(tm, 2048)],
        compiler_params=_params(48), name="fwd_in",
    )(x2, g1, wz, wf, wg)


def _log_sigmoid(f):
    return jnp.minimum(f, 0.0) - jnp.log1p(jnp.exp(-jnp.abs(f)))


AUG_LANES = 6


def _split3(v):
    hi = v.astype(BF16)
    r1 = v - hi.astype(F32)
    mid = r1.astype(BF16)
    lo = (r1 - mid.astype(F32)).astype(BF16)
    return hi, mid, lo


def _spread(parts, k0):
    r, c = _iota((LANES, LANES), 0), _iota((LANES, LANES), 1)
    out = None
    for i, part in enumerate(parts):
        e = ((c == AUG_LANES * r + (k0 + i)) & (r < N_HEADS)).astype(BF16)
        term = _dot(part, e)
        out = term if out is None else out + term
    return out


def _aug_query(v):
    ones = (_iota(v.shape, 1) < N_HEADS).astype(BF16)
    return (_spread(_split3(v), 0) + _spread((ones, ones, ones), 3)).astype(BF16)


def _aug_key(v):
    ones = (_iota(v.shape, 1) < N_HEADS).astype(BF16)
    return (_spread((ones, ones, ones), 0) - _spread(_split3(v), 3)).astype(BF16)


def _aug_stack(t2, aug, p):
    lane = _iota(t2.shape, 1)
    low = lane < 64
    zero = jnp.zeros_like(t2)
    first = 2 * AUG_LANES * p
    a_e = jnp.where((lane >= first) & (lane < first + AUG_LANES), aug, zero)
    a_o = jnp.where((lane >= first + AUG_LANES) & (lane < first + 2 * AUG_LANES), aug, zero)
    top = jnp.concatenate([jnp.where(low, t2, zero), a_e], axis=1)
    bot = jnp.concatenate([jnp.where(low, zero, t2), a_o], axis=1)
    return jnp.concatenate([top, bot], axis=0)


def _fwd_cum(fl, bfp):
    T = fl.shape[0]
    tb = CUM_TILE

    def body(fl_ref, b_ref, cc_ref, qa_ref, ka_ref):
        tri = (_iota((tb, tb), 0) >= _iota((tb, tb), 1)).astype(F32)
        carry = jnp.zeros((1, LANES), F32)
        for i in range(T // tb):
            rows = slice(i * tb, (i + 1) * tb)
            lf = _log_sigmoid(fl_ref[rows, :] + b_ref[...])
            cs = jnp.dot(tri, lf, precision=HIGHEST, preferred_element_type=F32) + carry
            cc_ref[rows, :] = cs
            carry = cs[tb - 1:tb, :]
            qa_ref[rows, :] = _aug_query(cs)
            ka_ref[rows, :] = _aug_key(cs)

    return pl.pallas_call(
        body,
        out_shape=(jax.ShapeDtypeStruct((T, LANES), F32), jax.ShapeDtypeStruct((T, LANES), BF16),
                   jax.ShapeDtypeStruct((T, LANES), BF16)),
        compiler_params=pltpu.CompilerParams(vmem_limit_bytes=32 * MIB), name="fwd_cum",
    )(fl, bfp)


def _sgu_forward_parts(z, gs, bs):
    u = _gelu(z[:, :SGU_WIDTH])
    vv = _gelu(z[:, SGU_WIDTH:])
    vc = vv - jnp.mean(vv, axis=1, keepdims=True)
    rs = lax.rsqrt(jnp.mean(vc * vc, axis=1, keepdims=True) + EPS)
    vhat = vc * rs
    return u, vhat, rs, vhat * gs + bs


def _sgu_pair_weights(w_ref, bT, p):
    tril = _iota((CHUNK, CHUNK), 0) >= _iota((CHUNK, CHUNK), 1)
    we = jnp.where(tril, w_ref[2 * p], 0.0).astype(BF16)
    wo = jnp.where(tril, w_ref[2 * p + 1], 0.0).astype(BF16)
    lane8 = _iota(bT.shape, 1)
    low = _iota((CHUNK, LANES), 1) < 64
    b2 = jnp.where(low, _pick(bT, lane8, 2 * p), _pick(bT, lane8, 2 * p + 1))
    return we, wo, b2


def _chunks_on_lanes(v, p, nc):
    return jnp.concatenate([v[c * CHUNK:(c + 1) * CHUNK, LANES * p:LANES * (p + 1)] for c in range(nc)], axis=1)


def _sgu_mix(we, wo, b2, vcat, nc):
    low = (_iota((CHUNK, nc * LANES), 1) % LANES) < 64
    return jnp.where(low, _dot(we, vcat), _dot(wo, vcat)) + jnp.concatenate([b2] * nc, axis=1)


def _fwd_sgu(zuv, gs, bs, wsp, bT):
    T = zuv.shape[0]
    tc = SGU_TILE
    nc = tc // CHUNK

    def body(z_ref, gs_ref, bs_ref, w_ref, bT_ref, y_ref):
        u, _, _, vln = _sgu_forward_parts(z_ref[...].astype(F32), gs_ref[...], bs_ref[...])
        vb = vln.astype(BF16)
        for p in range(4):
            we, wo, b2 = _sgu_pair_weights(w_ref, bT_ref[...], p)
            s = _sgu_mix(we, wo, b2, _chunks_on_lanes(vb, p, nc), nc)
            for c in range(nc):
                rows, cols = slice(c * CHUNK, (c + 1) * CHUNK), slice(LANES * p, LANES * (p + 1))
                y_ref[rows, cols] = (u[rows, cols] * s[:, c * LANES:(c + 1) * LANES]).astype(BF16)

    return pl.pallas_call(
        body, grid=(T // tc,), out_shape=jax.ShapeDtypeStruct((T, SGU_WIDTH), BF16),
        in_specs=[_row_spec(tc, 1024), _const_spec((1, SGU_WIDTH)), _const_spec((1, SGU_WIDTH)),
                  _const_spec((8, CHUNK, CHUNK)), _const_spec((CHUNK, 8))],
        out_specs=_row_spec(tc, SGU_WIDTH),
        compiler_params=_params(40), name="fwd_sgu",
    )(zuv, gs, bs, wsp, bT)


def _fwd_attn(qkv, qaug, kaug, w_shards):
    T = qkv.shape[0]
    tq = tk = ATTN_TILE
    nq = T // tq
    gather = _WeightGather([w.shape for w in w_shards], ["cols", "cols", "rows", "cols", "rows"])
    nw = gather.n

    def body(q_ref, qa_ref, k_ref, v_ref, ka_ref, *rest):
        w_refs, (o_ref, lse_ref), wg_refs, scratch = rest[:nw], rest[nw:nw + 2], rest[nw + 2:2 * nw + 2], rest[2 * nw + 2:]
        i = pl.program_id(0)

        @pl.when(i == 0)
        def _():
            gather.start(w_refs, wg_refs, scratch)

        @pl.when(i == nq // 2)
        def _():
            gather.forward(wg_refs, scratch)

        lane = _iota((tq, LANES), 1)
        low = lane < 64
        lowk = _iota((tk, LANES), 1) < 64
        one = jnp.ones((tk, LANES), BF16)
        row = _iota((2 * tq, tk), 0) % tq
        col = _iota((2 * tq, tk), 1)
        cols = [slice(LANES * p, LANES * (p + 1)) for p in range(4)]
        qa = qa_ref[...]
        qs = [_aug_stack(q_ref[:, cols[p]], qa, p) for p in range(4)]

        def step(j, carry, masked):
            ks = pl.ds(pl.multiple_of(j * tk, tk), tk)
            ka = ka_ref[ks, :]
            new = []
            for p in range(4):
                m, acc_e, acc_o = carry[p]
                v2 = v_ref[ks, cols[p]]
                s = _dot_nt(qs[p], jnp.concatenate([k_ref[ks, cols[p]], ka], axis=1))
                if masked:
                    s = jnp.where(col <= row, s, NEG)
                mn = jnp.maximum(m, jnp.max(s, axis=1, keepdims=True))
                al = jnp.exp(m - mn)
                pm = jnp.exp(s - mn).astype(BF16)
                acc_e = al[:tq] * acc_e + _dot(pm[:tq], jnp.where(lowk, v2, one))
                acc_o = al[tq:] * acc_o + _dot(pm[tq:], jnp.where(lowk, one, v2))
                new.append((mn, acc_e, acc_o))
            return tuple(new)

        init = tuple((jnp.full((2 * tq, 1), NEG, F32), jnp.zeros((tq, LANES), F32), jnp.zeros((tq, LANES), F32))
                     for _ in range(4))
        def trip(t, c):
            for u in range(3):
                c = step(3 * t + u, c, False)
            return c

        carry = lax.fori_loop(0, i // 3, trip, init)
        carry = lax.fori_loop(3 * (i // 3), i, lambda j, c: step(j, c, False), carry)
        carry = step(i, carry, True)
        lse_blk = jnp.zeros((tq, LANES), F32)
        for p in range(4):
            m, acc_e, acc_o = carry[p]
            l_e = pltpu.roll(acc_e, 64, 1)
            l_o = pltpu.roll(acc_o, 64, 1)
            o_ref[:, cols[p]] = jnp.where(low, acc_e / l_e, acc_o / l_o).astype(BF16)
            lse_blk = jnp.where(lane == 2 * p, m[:tq] + jnp.log(l_e), lse_blk)
            lse_blk = jnp.where(lane == 2 * p + 1, m[tq:] + jnp.log(acc_o), lse_blk)
        lse_ref[...] = lse_blk

        @pl.when(i == nq - 1)
        def _():
            gather.finish(wg_refs, scratch)

    return pl.pallas_call(
        body, grid=(nq,),
        out_shape=[jax.ShapeDtypeStruct((T, ATTN_WIDTH), BF16), jax.ShapeDtypeStruct((T, LANES), F32)]
        + gather.out_shapes(),
        in_specs=[_row_spec(tq, 512), _row_spec(tq, LANES),
                  pl.BlockSpec((T, 512), lambda i: (0, 1), pipeline_mode=pl.Buffered(1)),
                  pl.BlockSpec((T, 512), lambda i: (0, 2), pipeline_mode=pl.Buffered(1)),
                  _const_spec((T, LANES))] + [_const_spec(w.shape) for w in w_shards],
        out_specs=[_row_spec(tq, ATTN_WIDTH), _row_spec(tq, LANES)] + [pl.BlockSpec(memory_space=pl.ANY)] * nw,
        scratch_shapes=gather.scratch_shapes(),
        compiler_params=_params(58), name="fwd_attn",
    )(qkv, qaug, qkv, qkv, kaug, *w_shards)


def _fwd_merge(ys, ya, gt, x2, wbs, wba, wo, g2):
    T = x2.shape[0]
    tm = TOKEN_TILE

    def body(ys_ref, ya_ref, gt_ref, x_ref, wbs_ref, wba_ref, wo_ref, g2_ref, a_ref, b_ref, mg_ref, o_ref, h1_ref):
        A = _dot(ys_ref[...], wbs_ref[...])
        B = _dot(ya_ref[...], wba_ref[...])
        mg = (gt_ref[:, :D_MODEL].astype(F32) * A + gt_ref[:, D_MODEL:].astype(F32) * B).astype(BF16)
        o = _dot(mg, wo_ref[...])
        r2, oh = _rms_stats(o)
        a_ref[...] = A.astype(BF16)
        b_ref[...] = B.astype(BF16)
        mg_ref[...] = mg
        o_ref[...] = o.astype(BF16)
        h1_ref[...] = x_ref[...] + oh * g2_ref[...]

    sd = jax.ShapeDtypeStruct((T, D_MODEL), BF16)
    return pl.pallas_call(
        body, grid=(T // tm,),
        out_shape=(sd, sd, sd, sd, jax.ShapeDtypeStruct((T, D_MODEL), F32)),
        in_specs=[_row_spec(tm, 512), _row_spec(tm, 512), _row_spec(tm, 2048), _row_spec(tm, D_MODEL),
                  _const_spec((512, D_MODEL)), _const_spec((512, D_MODEL)), _const_spec((D_MODEL, D_MODEL)),
                  _const_spec((1, D_MODEL))],
        out_specs=[_row_spec(tm, D_MODEL)] * 5,
        compiler_params=_params(40), name="fwd_merge",
    )(ys, ya, gt, x2, wbs, wba, wo, g2)


def _fwd_ffn_loss(h1, tgt, wup, wdn, g3, g4):
    T = h1.shape[0]
    tm = MATMUL_TILE
    nsteps = T // tm

    def body(h1_ref, tg_ref, wup_ref, wdn_ref, g3_ref, g4_ref, xn2_ref, a_ref, ddn_ref, dy_ref, loss_ref,
             dg4_ref, acc_l, acc_g):
        i = pl.program_id(0)

        @pl.when(i == 0)
        def _():
            acc_l[...] = jnp.zeros_like(acc_l)
            acc_g[...] = jnp.zeros_like(acc_g)

        h1v = h1_ref[...]
        r3, h1h = _rms_stats(h1v)
        xn2 = (h1h * g3_ref[...]).astype(BF16)
        xn2_ref[...] = xn2
        dn = jnp.zeros((tm, D_MODEL), F32)
        for j in range(D_FF // 1024):
            cols = slice(1024 * j, 1024 * (j + 1))
            a = _dot(xn2, wup_ref[:, cols])
            a_ref[:, cols] = a.astype(BF16)
            hid = jnp.square(jnp.maximum(a, 0.0)).astype(BF16)
            dn = dn + _dot(hid, wdn_ref[cols, :])
        r4, dnh = _rms_stats(dn)
        g4v = g4_ref[...]
        e = (h1v + dnh * g4v) - tg_ref[...]
        sq = e * e
        s1 = sq[:, 0:LANES]
        for j in range(1, D_MODEL // LANES):
            s1 = s1 + sq[:, LANES * j:LANES * (j + 1)]
        acc_l[...] += _fold8(s1)
        dy = e * (1.0 / D_MODEL)
        dy_ref[...] = dy
        acc_g[...] += _fold8(dy * dnh)
        ddn_ref[...] = _rms_bwd(dy, dnh, r4, g4v).astype(BF16)

        @pl.when(i == nsteps - 1)
        def _():
            loss_ref[...] = acc_l[...] * (0.5 / D_MODEL)
            dg4_ref[...] = jnp.sum(acc_g[...], axis=0, keepdims=True)

    return pl.pallas_call(
        body, grid=(nsteps,),
        out_shape=(jax.ShapeDtypeStruct((T, D_MODEL), BF16), jax.ShapeDtypeStruct((T, D_FF), BF16),
                   jax.ShapeDtypeStruct((T, D_MODEL), BF16), jax.ShapeDtypeStruct((T, D_MODEL), F32),
                   jax.ShapeDtypeStruct((SUBLANES, LANES), F32), jax.ShapeDtypeStruct((1, D_MODEL), F32)),
        in_specs=[_row_spec(tm, D_MODEL), _row_spec(tm, D_MODEL), _const_spec((D_MODEL, D_FF)),
                  _const_spec((D_FF, D_MODEL)), _const_spec((1, D_MODEL)), _const_spec((1, D_MODEL))],
        out_specs=[_row_spec(tm, D_MODEL), _row_spec(tm, D_FF), _row_spec(tm, D_MODEL), _row_spec(tm, D_MODEL),
                   pl.BlockSpec((SUBLANES, LANES), lambda i: (0, 0)), pl.BlockSpec((1, D_MODEL), lambda i: (0, 0))],
        scratch_shapes=[pltpu.VMEM((SUBLANES, LANES), F32), pltpu.VMEM((SUBLANES, D_MODEL), F32)],
        compiler_params=_params(52), name="fwd_ffn_loss",
    )(h1, tgt, wup, wdn, g3, g4)


def _bwd_ffn(ddn, a, dy, h1, wup, wdn, g3):
    T = h1.shape[0]
    tm = MATMUL_TILE
    nsteps = T // tm

    def body(ddn_ref, a_ref, dy_ref, h1_ref, wup_ref, wdn_ref, g3_ref, da_ref, dh1_ref, dg3_ref, acc_g):
        i = pl.program_id(0)

        @pl.when(i == 0)
        def _():
            acc_g[...] = jnp.zeros_like(acc_g)

        ddnv = ddn_ref[...]
        dxn2 = jnp.zeros((tm, D_MODEL), F32)
        for j in range(D_FF // 1024):
            cols = slice(1024 * j, 1024 * (j + 1))
            dhid = _dot_nt(ddnv, wdn_ref[cols, :])
            da = (dhid * (2.0 * jnp.maximum(a_ref[:, cols].astype(F32), 0.0))).astype(BF16)
            da_ref[:, cols] = da
            dxn2 = dxn2 + _dot_nt(da, wup_ref[:, cols])
        r3, h1h = _rms_stats(h1_ref[...])
        acc_g[...] += _fold8(dxn2 * h1h)
        dh1_ref[...] = dy_ref[...] + _rms_bwd(dxn2, h1h, r3, g3_ref[...])

        @pl.when(i == nsteps - 1)
        def _():
            dg3_ref[...] = jnp.sum(acc_g[...], axis=0, keepdims=True)

    return pl.pallas_call(
        body, grid=(nsteps,),
        out_shape=(jax.ShapeDtypeStruct((T, D_FF), BF16), jax.ShapeDtypeStruct((T, D_MODEL), F32),
                   jax.ShapeDtypeStruct((1, D_MODEL), F32)),
        in_specs=[_row_spec(tm, D_MODEL), _row_spec(tm, D_FF), _row_spec(tm, D_MODEL), _row_spec(tm, D_MODEL),
                  _const_spec((D_MODEL, D_FF)), _const_spec((D_FF, D_MODEL)), _const_spec((1, D_MODEL))],
        out_specs=[_row_spec(tm, D_FF), _row_spec(tm, D_MODEL), pl.BlockSpec((1, D_MODEL), lambda i: (0, 0))],
        scratch_shapes=[pltpu.VMEM((SUBLANES, D_MODEL), F32)],
        compiler_params=_params(52), name="bwd_ffn",
    )(ddn, a, dy, h1, wup, wdn, g3)


def _wgrad(xa, dy, name, relu2=False, tn=None, block_cols=None):
    T, K = xa.shape
    N = dy.shape[1]
    tn = N if tn is None else tn
    tt = min(T, WGRAD_TILE if K <= D_MODEL else WGRAD_TILE // 2)
    if block_cols:
        nb = tn // block_cols
        out_shape = jax.ShapeDtypeStruct((N // block_cols, K, block_cols), F32)
        out_spec = pl.BlockSpec((nb, K, block_cols), lambda n, t: (n, 0, 0))
    else:
        out_shape = jax.ShapeDtypeStruct((K, N), F32)
        out_spec = pl.BlockSpec((K, tn), lambda n, t: (0, n))

    def body(x_ref, dy_ref, o_ref):
        @pl.when(pl.program_id(1) == 0)
        def _():
            o_ref[...] = jnp.zeros_like(o_ref)

        xv = x_ref[...]
        if relu2:
            xv = jnp.square(jnp.maximum(xv.astype(F32), 0.0)).astype(BF16)
        if block_cols:
            for b in range(nb):
                o_ref[b] += _dot_tn(xv, dy_ref[:, block_cols * b:block_cols * (b + 1)])
        else:
            o_ref[...] += _dot_tn(xv, dy_ref[...])

    return pl.pallas_call(
        body, grid=(N // tn, T // tt), out_shape=out_shape,
        in_specs=[pl.BlockSpec((tt, K), lambda n, t: (t, 0)), pl.BlockSpec((tt, tn), lambda n, t: (t, n))],
        out_specs=out_spec,
        compiler_params=_params(52, 2), name=name,
    )(xa, dy)


def _wgrad_multi(xa, dys, name):
    T, K = xa.shape
    tt = min(T, WGRAD_TILE)
    n = len(dys)

    def body(x_ref, *refs):
        dy_refs, o_refs = refs[:n], refs[n:]

        @pl.when(pl.program_id(0) == 0)
        def _():
            for o_ref in o_refs:
                o_ref[...] = jnp.zeros_like(o_ref)

        xv = x_ref[...]
        for dy_ref, o_ref in zip(dy_refs, o_refs):
            o_ref[...] += _dot_tn(xv, dy_ref[...])

    return pl.pallas_call(
        body, grid=(T // tt,),
        out_shape=[jax.ShapeDtypeStruct((K, dy.shape[1]), F32) for dy in dys],
        in_specs=[_row_spec(tt, K)] + [_row_spec(tt, dy.shape[1]) for dy in dys],
        out_specs=[pl.BlockSpec((K, dy.shape[1]), lambda t: (0, 0)) for dy in dys],
        compiler_params=_params(52), name=name,
    )(xa, *dys)


def _bwd_merge(dh1, o, A, B, gt, ys, ya, mg, lse, cc, wbs, wba, wo, g2):
    T = dh1.shape[0]
    tm = MERGE_BWD_TILE
    nsteps = T // tm

    def body(dh1_ref, o_ref, a_ref, b_ref, gt_ref, ys_ref, ya_ref, mg_ref, lse_ref, cc_ref, wbs_ref, wba_ref,
             wo_ref, g2_ref, dgl_ref, dys_ref, dya_ref, qab_ref, dab_ref, dg2_ref, dwbs_ref, dwba_ref, dwo_ref,
             acc_g):
        i = pl.program_id(0)

        @pl.when(i == 0)
        def _():
            acc_g[...] = jnp.zeros_like(acc_g)
            dwbs_ref[...] = jnp.zeros_like(dwbs_ref)
            dwba_ref[...] = jnp.zeros_like(dwba_ref)
            dwo_ref[...] = jnp.zeros_like(dwo_ref)

        dh1v = dh1_ref[...]
        r2, oh = _rms_stats(o_ref[...].astype(F32))
        acc_g[...] += _fold8(dh1v * oh)
        do = _rms_bwd(dh1v, oh, r2, g2_ref[...]).astype(BF16)
        dwo_ref[...] += _dot_tn(mg_ref[...], do)
        dmg = _dot_nt(do, wo_ref[...])
        ga = gt_ref[:, :D_MODEL].astype(F32)
        gb = gt_ref[:, D_MODEL:].astype(F32)
        dgl_ref[:, :D_MODEL] = (dmg * a_ref[...].astype(F32) * ga * (1.0 - ga)).astype(BF16)
        dgl_ref[:, D_MODEL:] = (dmg * b_ref[...].astype(F32) * gb * (1.0 - gb)).astype(BF16)
        dA = (dmg * ga).astype(BF16)
        dB = (dmg * gb).astype(BF16)
        dwbs_ref[...] += _dot_tn(ys_ref[...], dA)
        dwba_ref[...] += _dot_tn(ya_ref[...], dB)
        dys_ref[...] = _dot_nt(dA, wbs_ref[...]).astype(BF16)
        dya = _dot_nt(dB, wba_ref[...]).astype(BF16)
        dya_ref[...] = dya
        prod = dya.astype(F32) * ya_ref[...].astype(F32)
        lane = _iota((tm, LANES), 1)
        low = lane < 64
        blk = jnp.zeros((tm, LANES), F32)
        for p in range(4):
            pp = prod[:, LANES * p:LANES * (p + 1)]
            blk = jnp.where(lane == 2 * p, jnp.sum(jnp.where(low, pp, 0.0), axis=1, keepdims=True), blk)
            blk = jnp.where(lane == 2 * p + 1, jnp.sum(jnp.where(low, 0.0, pp), axis=1, keepdims=True), blk)
        qab_ref[...] = _aug_query(cc_ref[...] - lse_ref[...])
        dab_ref[...] = _spread(_split3(-blk), 0).astype(BF16)

        @pl.when(i == nsteps - 1)
        def _():
            dg2_ref[...] = jnp.sum(acc_g[...], axis=0, keepdims=True)

    sh = jax.ShapeDtypeStruct((T, 512), BF16)
    sa = jax.ShapeDtypeStruct((T, LANES), BF16)
    sw = jax.ShapeDtypeStruct((512, D_MODEL), F32)
    whole = lambda shape: pl.BlockSpec(shape, lambda i: (0, 0))
    return pl.pallas_call(
        body, grid=(nsteps,),
        out_shape=(jax.ShapeDtypeStruct((T, 2048), BF16), sh, sh, sa, sa, jax.ShapeDtypeStruct((1, D_MODEL), F32),
                   sw, sw, jax.ShapeDtypeStruct((D_MODEL, D_MODEL), F32)),
        in_specs=[_row_spec(tm, D_MODEL)] * 4 + [_row_spec(tm, 2048), _row_spec(tm, 512), _row_spec(tm, 512),
                  _row_spec(tm, D_MODEL), _row_spec(tm, LANES), _row_spec(tm, LANES),
                  _const_spec((512, D_MODEL)), _const_spec((512, D_MODEL)),
                  _const_spec((D_MODEL, D_MODEL)), _const_spec((1, D_MODEL))],
        out_specs=[_row_spec(tm, 2048), _row_spec(tm, 512), _row_spec(tm, 512), _row_spec(tm, LANES),
                   _row_spec(tm, LANES), whole((1, D_MODEL)), whole((512, D_MODEL)), whole((512, D_MODEL)),
                   whole((D_MODEL, D_MODEL))],
        scratch_shapes=[pltpu.VMEM((SUBLANES, D_MODEL), F32)],
        compiler_params=_params(56), name="bwd_merge",
    )(dh1, o, A, B, gt, ys, ya, mg, lse, cc, wbs, wba, wo, g2)


def _bwd_sgu(zuv, dys, gs, bs, wsp, bT, grads):
    T = zuv.shape[0]
    tc = SGU_TILE
    nc = tc // CHUNK
    nsteps = T // tc
    ex = _GradExchange([tuple(g.shape[1:]) for g in grads])
    ng = ex.n

    def body(z_ref, dy_ref, gs_ref, bs_ref, w_ref, bT_ref, *rest):
        g_refs, (dz_ref, dw_ref, dbT_ref, dgs_ref, dbs_ref) = rest[:ng], rest[ng:ng + 5]
        land1 = rest[ng + 5:2 * ng + 5]
        acc_w, acc_b, acc_gs, acc_bs, dvln_s = rest[2 * ng + 5:2 * ng + 10]
        ex_sems = rest[2 * ng + 10:]
        i = pl.program_id(0)

        @pl.when(i == 0)
        def _():
            ex.start(1, g_refs, land1, ex_sems)
            acc_w[...] = jnp.zeros_like(acc_w)
            acc_b[...] = jnp.zeros_like(acc_b)
            acc_gs[...] = jnp.zeros_like(acc_gs)
            acc_bs[...] = jnp.zeros_like(acc_bs)

        z = z_ref[...].astype(F32)
        gsv = gs_ref[...]
        u, vhat, rs, vln = _sgu_forward_parts(z, gsv, bs_ref[...])
        vb = vln.astype(BF16)
        dy = dy_ref[...].astype(F32)
        low_w = (_iota((CHUNK, nc * LANES), 1) % LANES) < 64
        for p in range(4):
            we, wo, b2 = _sgu_pair_weights(w_ref, bT_ref[...], p)
            vcat = _chunks_on_lanes(vb, p, nc)
            s = _sgu_mix(we, wo, b2, vcat, nc)
            dyc = _chunks_on_lanes(dy, p, nc)
            ds = dyc * _chunks_on_lanes(u, p, nc)
            dsb = ds.astype(BF16)
            zero = jnp.zeros_like(dsb)
            dse = jnp.where(low_w, dsb, zero)
            dso = jnp.where(low_w, zero, dsb)
            acc_w[2 * p] += _dot_nt(dse, vcat)
            acc_w[2 * p + 1] += _dot_nt(dso, vcat)
            acc_b[p] += ds
            dvl = jnp.where(low_w, _dot_tn(we, dsb), _dot_tn(wo, dsb))
            for c in range(nc):
                rows, cols = slice(c * CHUNK, (c + 1) * CHUNK), slice(LANES * p, LANES * (p + 1))
                dvln_s[rows, cols] = dvl[:, c * LANES:(c + 1) * LANES]
                du = dy[rows, cols] * s[:, c * LANES:(c + 1) * LANES]
                dz_ref[rows, cols] = (du * _gelu_grad(z[rows, cols])).astype(BF16)
        dvln = dvln_s[...]
        acc_gs[...] += _fold8(dvln * vhat)
        acc_bs[...] += _fold8(dvln)
        al = dvln * gsv
        dvv = rs * (al - jnp.mean(al, axis=1, keepdims=True) - vhat * jnp.mean(al * vhat, axis=1, keepdims=True))
        dz_ref[:, SGU_WIDTH:] = (dvv * _gelu_grad(z[:, SGU_WIDTH:])).astype(BF16)

        @pl.when(i == nsteps - 1)
        def _():
            tril = _iota((CHUNK, CHUNK), 0) >= _iota((CHUNK, CHUNK), 1)
            lane = _iota((CHUNK, LANES), 1)
            low = lane < 64
            blk = jnp.zeros((CHUNK, LANES), F32)
            for g in range(8):
                dw_ref[g] = jnp.where(tril, acc_w[g], 0.0)
            for p in range(4):
                t = acc_b[p]
                tot = t[:, 0:LANES]
                for c in range(1, nc):
                    tot = tot + t[:, c * LANES:(c + 1) * LANES]
                blk = jnp.where(lane == 2 * p, jnp.sum(jnp.where(low, tot, 0.0), axis=1, keepdims=True), blk)
                blk = jnp.where(lane == 2 * p + 1, jnp.sum(jnp.where(low, 0.0, tot), axis=1, keepdims=True), blk)
            dbT_ref[...] = blk
            dgs_ref[...] = jnp.sum(acc_gs[...], axis=0, keepdims=True)
            dbs_ref[...] = jnp.sum(acc_bs[...], axis=0, keepdims=True)
            ex.wait(1, g_refs, land1, ex_sems)

    whole = lambda shape: pl.BlockSpec(shape, lambda i: (0,) * len(shape))
    hbm_spec = pl.BlockSpec(memory_space=pl.ANY)
    return pl.pallas_call(
        body, grid=(nsteps,),
        out_shape=[jax.ShapeDtypeStruct((T, 1024), BF16), jax.ShapeDtypeStruct((8, CHUNK, CHUNK), F32),
                   jax.ShapeDtypeStruct((CHUNK, LANES), F32), jax.ShapeDtypeStruct((1, SGU_WIDTH), F32),
                   jax.ShapeDtypeStruct((1, SGU_WIDTH), F32)] + ex.land_shapes(1),
        in_specs=[_row_spec(tc, 1024), _row_spec(tc, SGU_WIDTH), _const_spec((1, SGU_WIDTH)),
                  _const_spec((1, SGU_WIDTH)), _const_spec((8, CHUNK, CHUNK)), _const_spec((CHUNK, 8))]
        + [hbm_spec] * ng,
        out_specs=[_row_spec(tc, 1024), whole((8, CHUNK, CHUNK)), whole((CHUNK, LANES)),
                   whole((1, SGU_WIDTH)), whole((1, SGU_WIDTH))] + [hbm_spec] * ng,
        scratch_shapes=[pltpu.VMEM((8, CHUNK, CHUNK), F32), pltpu.VMEM((4, CHUNK, nc * LANES), F32),
                        pltpu.VMEM((SUBLANES, SGU_WIDTH), F32), pltpu.VMEM((SUBLANES, SGU_WIDTH), F32),
                        pltpu.VMEM((tc, SGU_WIDTH), F32)] + ex.sem_shapes(1),
        compiler_params=_params(48), name="bwd_sgu",
    )(zuv, dys, gs, bs, wsp, bT, *grads)


def _bwd_attn(qkv, dya, qab, dab, kaug, parts):
    T = qkv.shape[0]
    tq = tk = ATTN_TILE
    nq = T // tq
    nk = T // tk
    ex = _GradExchange([tuple(g.shape[1:]) for g in parts])
    nr = ex.n

    def body(q_ref, do_ref, qa_ref, da_ref, k_ref, v_ref, ka_ref, *rest):
        part_refs, (dq_ref, dk_ref, dv_ref, dcx_ref) = rest[:nr], rest[nr:nr + 4]
        land2, dq_acc, ex_sems = rest[nr + 4:2 * nr + 4], rest[2 * nr + 4], rest[2 * nr + 5:]
        p = pl.program_id(0)
        j = pl.program_id(1)

        @pl.when((p == 0) & (j == 0))
        def _():
            ex.start(2, part_refs, land2, ex_sems)

        lane = _iota((tq, LANES), 1)
        low = lane < 64
        row = _iota((2 * tq, tk), 0) % tq
        col = _iota((2 * tq, tk), 1)
        first = 2 * AUG_LANES * p
        half = tq // 2

        @pl.when(j == 0)
        def _():
            dq_acc[...] = jnp.zeros_like(dq_acc)

        @pl.when((j == 0) & (p == 0))
        def _():
            dcx_ref[...] = jnp.zeros_like(dcx_ref)

        ka = ka_ref[...]
        kk = jnp.concatenate([k_ref[...], ka], axis=1)
        vv = jnp.concatenate([v_ref[...], ka], axis=1)

        def tile(i, carry, r0, c0, nc, masked):
            dk_a, dv_a = carry
            nr = tq - r0
            qsl = pl.ds(pl.multiple_of(i * tq + r0, half), nr)
            qs = _aug_stack(q_ref[qsl, :], qa_ref[qsl, :], p)
            dos = _aug_stack(do_ref[qsl, :], da_ref[qsl, :], p)
            kc, vc = kk[c0:c0 + nc], vv[c0:c0 + nc]
            s = _dot_nt(qs, kc)
            if masked:
                s = jnp.where(col[:2 * nr, :nc] + c0 <= row[:2 * nr, :nc] % nr + r0, s, NEG)
            pm = jnp.exp(s)
            ds = pm * _dot_nt(dos, vc)
            dsb = ds.astype(BF16)
            dv_u = _dot_tn(pm.astype(BF16), dos[:, :LANES])
            dk_u = _dot_tn(dsb, qs)
            if nc == tk:
                dv_a, dk_a = dv_a + dv_u, dk_a + dk_u
            else:
                pad = lambda u: jnp.concatenate(
                    [jnp.zeros((n, u.shape[1]), F32) if z else u
                     for z, n in ((True, c0), (False, nc), (True, tk - c0 - nc)) if n], axis=0)
                dv_a, dk_a = dv_a + pad(dv_u), dk_a + pad(dk_u)
            dqx = _dot(dsb, kc)
            dq_acc[qsl, :] += jnp.where(low[:nr], dqx[:nr, :LANES], dqx[nr:, :LANES])
            dcx_ref[qsl, :] += (jnp.where(lane[:nr] == first, dqx[:nr, LANES:], 0.0)
                                + jnp.where(lane[:nr] == first + AUG_LANES, dqx[nr:, LANES:], 0.0))
            return dk_a, dv_a

        def q_block(i, carry, masked):
            return tile(i, carry, 0, 0, tk, masked)

        init =(jnp.zeros((tk, 2 * LANES), F32), jnp.zeros((tk, LANES), F32))
        carry = tile(j, init, 0, 0, half, True)
        carry = tile(j, carry, half, half, half, True)
        n_rest = nq - 1 - j

        def trip(t, c):
            for u in range(3):
                c = q_block(j + 1 + 3 * t + u, c, False)
            return c

        carry = lax.fori_loop(0, n_rest // 3, trip, carry)
        dk_a, dv_a = lax.fori_loop(j + 1 + 3 * (n_rest // 3), nq, lambda i, c: q_block(i, c, False), carry)
        dk_ref[...] = dk_a[:, :LANES].astype(BF16)
        dv_ref[...] = dv_a.astype(BF16)
        ksl = pl.ds(pl.multiple_of(j * tk, tk), tk)
        lk = _iota((tk, LANES), 1)
        dcx_ref[ksl, :] += jnp.where((lk == first + 3) | (lk == first + AUG_LANES + 3), dk_a[:, LANES:], 0.0)

        @pl.when(j == nk - 1)
        def _():
            dq_ref[...] = (dq_acc[...] * 0.125).astype(BF16)

        @pl.when((p == 3) & (j == nk - 1))
        def _():
            ex.wait(2, part_refs, land2, ex_sems)

    sh = jax.ShapeDtypeStruct((T, ATTN_WIDTH), BF16)
    full = lambda cb: pl.BlockSpec((T, LANES), lambda p, j: (0, cb + p))
    blk = lambda cb: pl.BlockSpec((tk, LANES), lambda p, j: (j, cb + p))
    hbm_spec = pl.BlockSpec(memory_space=pl.ANY)
    return pl.pallas_call(
        body, grid=(4, nk),
        out_shape=[sh, sh, sh, jax.ShapeDtypeStruct((T, LANES), F32)] + ex.land_shapes(2),
        in_specs=[full(0), full(0), _const_spec((T, LANES)), _const_spec((T, LANES)), blk(4), blk(8),
                  pl.BlockSpec((tk, LANES), lambda p, j: (j, 0))] + [hbm_spec] * nr,
        out_specs=[full(0), blk(0), blk(0), pl.BlockSpec((T, LANES), lambda p, j: (0, 0))] + [hbm_spec] * nr,
        scratch_shapes=[pltpu.VMEM((T, LANES), F32)] + ex.sem_shapes(2),
        compiler_params=_params(58, 2), name="bwd_attn",
    )(qkv, dya, qab, dab, qkv, qkv, kaug, *parts)


def _bwd_cum(dcx, fl, bfp):
    T = fl.shape[0]
    tb = CUM_TILE

    def body(dcx_ref, fl_ref, b_ref, dfl_ref, dbf_ref):
        triu = (_iota((tb, tb), 0) <= _iota((tb, tb), 1)).astype(F32)
        r, c = _iota((LANES, LANES), 0), _iota((LANES, LANES), 1)
        sel = (((r == AUG_LANES * c) & (c < N_HEADS)).astype(F32)
               - ((r == AUG_LANES * c + 3) & (c < N_HEADS)).astype(F32))
        carry = jnp.zeros((1, LANES), F32)
        dbf = jnp.zeros((1, LANES), F32)
        for i in reversed(range(T // tb)):
            colblk = jnp.dot(dcx_ref[i * tb:(i + 1) * tb, :], sel, precision=HIGHEST, preferred_element_type=F32)
            rc = jnp.dot(triu, colblk, precision=HIGHEST, preferred_element_type=F32) + carry
            carry = rc[0:1, :]
            sig = jax.nn.sigmoid(fl_ref[i * tb:(i + 1) * tb, :] + b_ref[...])
            dfl = rc * (1.0 - sig)
            dfl_ref[i * tb:(i + 1) * tb, :] = dfl.astype(BF16)
            dbf = dbf + jnp.sum(dfl, axis=0, keepdims=True)
        dbf_ref[...] = dbf

    return pl.pallas_call(
        body,
        out_shape=(jax.ShapeDtypeStruct((T, LANES), BF16), jax.ShapeDtypeStruct((1, LANES), F32)),
        compiler_params=pltpu.CompilerParams(vmem_limit_bytes=32 * MIB), name="bwd_cum",
    )(dcx, fl, bfp)


def _bwd_in(dz, dq, dk, dv, dfl, dgl, dh1, x2, g1, wz, wf, wg, rows, name, prev=None, stage=0, exchanged=None):
    T = x2.shape[0]
    tm = math.gcd(MATMUL_TILE, rows[0], rows[1] - rows[0])
    first = rows[0] // tm
    nsteps = (rows[1] - rows[0]) // tm
    ex = _GradExchange([tuple(exchanged.shape[1:])]) if stage else None

    def body(dz_ref, dq_ref, dk_ref, dv_ref, dfl_ref, dgl_ref, dh1_ref, x_ref, g_ref, wz_ref, wf_ref, wg_ref, *rest):
        rest = list(rest)
        dx_prev, dg1_prev = (rest.pop(0), rest.pop(0)) if prev else (None, None)
        src_ref = rest.pop(0) if stage else None
        dx_ref, dg1_ref = rest.pop(0), rest.pop(0)
        land_ref = rest.pop(0) if stage else None
        acc_g, ex_sems = rest[0], rest[1:]
        i = pl.program_id(0)

        @pl.when(i == 0)
        def _():
            if stage:
                ex.start(stage, [src_ref], [land_ref], ex_sems)
            acc_g[...] = jnp.zeros_like(acc_g)

        dxn = _dot_nt(dz_ref[...], wz_ref[:, 0:1024])
        dxn = dxn + _dot_nt(dq_ref[...], wz_ref[:, 1024:1536])
        dxn = dxn + _dot_nt(dk_ref[...], wz_ref[:, 1536:2048])
        dxn = dxn + _dot_nt(dv_ref[...], wz_ref[:, 2048:2560])
        dxn = dxn + _dot_nt(dfl_ref[...], wf_ref[...])
        dxn = dxn + _dot_nt(dgl_ref[...], wg_ref[...])
        r1, xh = _rms_stats(x_ref[...])
        acc_g[...] += _fold8(dxn * xh)
        dx_ref[...] = dh1_ref[...] + _rms_bwd(dxn, xh, r1, g_ref[...])

        @pl.when(i == nsteps - 1)
        def _():
            total = jnp.sum(acc_g[...], axis=0, keepdims=True)
            dg1_ref[...] = total + dg1_prev[...] if prev else total
            if stage:
                ex.wait(stage, [src_ref], [land_ref], ex_sems)

    hbm_spec = pl.BlockSpec(memory_space=pl.ANY)
    rows_spec = lambda n: pl.BlockSpec((tm, n), lambda i: (i + first, 0))
    operands = [dz, dq, dk, dv, dfl, dgl, dh1, x2, g1, wz, wf, wg]
    in_specs = [rows_spec(1024), rows_spec(512), rows_spec(512), rows_spec(512), rows_spec(LANES), rows_spec(2048),
                rows_spec(D_MODEL), rows_spec(D_MODEL),
                _const_spec((1, D_MODEL)), _const_spec((D_MODEL, ZQKV_WIDTH)), _const_spec((D_MODEL, LANES)),
                _const_spec((D_MODEL, 2048))]
    aliases = {}
    if prev:
        aliases = {len(operands): 0}
        operands += list(prev)
        in_specs += [hbm_spec, _const_spec((1, D_MODEL))]
    if stage:
        operands.append(exchanged)
        in_specs.append(hbm_spec)
    return pl.pallas_call(
        body, grid=(nsteps,),
        out_shape=[jax.ShapeDtypeStruct((T, D_MODEL), F32), jax.ShapeDtypeStruct((1, D_MODEL), F32)]
        + (ex.land_shapes(stage) if stage else []),
        in_specs=in_specs,
        out_specs=[rows_spec(D_MODEL), pl.BlockSpec((1, D_MODEL), lambda i: (0, 0))] + ([hbm_spec] if stage else []),
        scratch_shapes=[pltpu.VMEM((SUBLANES, D_MODEL), F32)] + (ex.sem_shapes(stage) if stage else []),
        input_output_aliases=aliases,
        compiler_params=_params(48), name=name,
    )(*operands)


def _small_kernel_shapes(g_mix_pre, b_forget, g_sgu, b_sgu, w_spatial, b_spatial, g_mix_post, g_ffn_pre, g_ffn_post):
    return dict(g_mix_pre=g_mix_pre, b_forget=jnp.pad(b_forget, ((0, 0), (0, LANES - N_HEADS))), g_sgu=g_sgu,
                b_sgu=b_sgu, w_spatial=w_spatial[0], b_spatial=b_spatial[0], g_mix_post=g_mix_post,
                g_ffn_pre=g_ffn_pre, g_ffn_post=g_ffn_post)


def _small_output_shapes(d):
    out = dict(d)
    out.update(b_forget=d["b_forget"][:, :N_HEADS], w_spatial=d["w_spatial"][None], b_spatial=d["b_spatial"][None])
    return out


def kernel(x, g_mix_pre, w_in, b_forget, g_sgu, b_sgu, w_spatial, b_spatial, w_branch_sgu, w_branch_attn, w_out, g_mix_post, g_ffn_pre, w_up, w_down, g_ffn_post, loss_target, m_g_mix_pre, m_w_in, m_b_forget, m_g_sgu, m_b_sgu, m_w_spatial, m_b_spatial, m_w_branch_sgu, m_w_branch_attn, m_w_out, m_g_mix_post, m_g_ffn_pre, m_w_up, m_w_down, m_g_ffn_post, v_g_mix_pre, v_w_in, v_b_forget, v_g_sgu, v_b_sgu, v_w_spatial, v_b_spatial, v_w_branch_sgu, v_w_branch_attn, v_w_out, v_g_mix_post, v_g_ffn_pre, v_w_up, v_w_down, v_g_ffn_post):
    T = x.shape[1]
    x2 = x.reshape(T, D_MODEL)
    tgt = loss_target.reshape(T, D_MODEL)

    wg_in = _gather_w_in(w_in[0])
    wz, wf, wgt = _assemble_w_in(wg_in)
    bfp = jnp.pad(b_forget, ((0, 0), (0, LANES - N_HEADS)))
    wsp = w_spatial[0]
    bT = b_spatial[0].T

    xn, zuv, qkv, fl, gt = _fwd_in(x2, g_mix_pre, wz, wf, wgt)
    cc, qaug, kaug = _fwd_cum(fl, bfp)
    ys = _fwd_sgu(zuv, g_sgu, b_sgu, wsp, bT)
    ya, lse, wbs, wba, wo, wup, wdn = _fwd_attn(
        qkv, qaug, kaug, (w_branch_sgu[0], w_branch_attn[0], w_out[0], w_up[0], w_down[0]))
    A, B, mg, o, h1 = _fwd_merge(ys, ya, gt, x2, wbs, wba, wo, g_mix_post)
    xn2, a, ddn, dy, loss_part, dg4 = _fwd_ffn_loss(h1, tgt, wup, wdn, g_ffn_pre, g_ffn_post)

    da, dh1, dg3 = _bwd_ffn(ddn, a, dy, h1, wup, wdn, g_ffn_pre)
    dw_up = _wgrad(xn2, da, "wgrad_up", tn=2048, block_cols=512)
    dw_down = _wgrad(a, ddn, "wgrad_down", relu2=True)
    dgl, dys, dya, qab, dab, dg2, dw_bs, dw_ba, dw_out = _bwd_merge(
        dh1, o, A, B, gt, ys, ya, mg, lse, cc, wbs, wba, wo, g_mix_post)
    col_blocks = lambda g, w: g.reshape(g.shape[0], N_DEV, w).transpose(1, 0, 2)
    row_blocks = lambda g, r: g.reshape(N_DEV, r, g.shape[1])
    early_names = ["w_branch_sgu", "w_branch_attn", "w_out", "w_up", "w_down"]
    early = [col_blocks(dw_bs, 128), col_blocks(dw_ba, 128), row_blocks(dw_out, 128), dw_up, row_blocks(dw_down, 512)]
    owners = _owner_indices()
    dzuv, dwsp, dbT, dgs, dbs, *early_land1 = _bwd_sgu(zuv, dys, g_sgu, b_sgu, wsp, bT, early)
    early_parts = [_chip_partials(g, l1, owners, "chip_partials_" + nm)
                   for g, l1, nm in zip(early, early_land1, early_names)]
    dq, dk, dv, dcx, *early_land2 = _bwd_attn(qkv, dya, qab, dab, kaug, early_parts)
    dfl, dbf = _bwd_cum(dcx, fl, bfp)
    dw_z = _wgrad(xn, dzuv, "wgrad_in_z")
    dw_q, dw_k, dw_v, dw_f = _wgrad_multi(xn, [dq, dk, dv, dfl], "wgrad_in_qkvf")
    dw_g = _wgrad(xn, dgl, "wgrad_in_gate")
    blocks_in = _block_dw_in([dw_z, dw_q, dw_k, dw_v, dw_f, dw_g])
    bwd_in_args = (dzuv, dq, dk, dv, dfl, dgl, dh1, x2, g_mix_pre, wz, wf, wgt)
    dx, dg1, land1_in = _bwd_in(*bwd_in_args, (0, T // 8), "bwd_in_a", stage=1, exchanged=blocks_in)
    part_in = _chip_partials(blocks_in, land1_in, owners, "chip_partials_w_in")
    dx, dg1, land2_in = _bwd_in(*bwd_in_args, (T // 8, 3 * T // 4), "bwd_in_b", prev=(dx, dg1), stage=2,
                                exchanged=part_in)
    dx, dg1 = _bwd_in(*bwd_in_args, (3 * T // 4, T), "bwd_in_c", prev=(dx, dg1))

    tot_a, tot_b = _allreduce_small(dict(
        g_mix_pre=dg1, b_forget=dbf, g_sgu=dgs, b_sgu=dbs, w_spatial=dwsp, b_spatial=dbT, g_mix_post=dg2,
        g_ffn_pre=dg3, g_ffn_post=dg4), loss_part)
    r0, nr, c0, nc = LOSS_SLOT
    loss = jnp.sum(tot_a[r0:r0 + nr, c0:c0 + nc])
    small_w = _small_kernel_shapes(g_mix_pre, b_forget, g_sgu, b_sgu, w_spatial, b_spatial, g_mix_post, g_ffn_pre,
                                   g_ffn_post)
    small_m = _small_kernel_shapes(m_g_mix_pre, m_b_forget, m_g_sgu, m_b_sgu, m_w_spatial, m_b_spatial, m_g_mix_post,
                                   m_g_ffn_pre, m_g_ffn_post)
    small_v = _small_kernel_shapes(v_g_mix_pre, v_b_forget, v_g_sgu, v_b_sgu, v_w_spatial, v_b_spatial, v_g_mix_post,
                                   v_g_ffn_pre, v_g_ffn_post)
    sg, sd, sm, sv = (_small_output_shapes(d) for d in _adamw_small(tot_a, tot_b, small_w, small_m, small_v))

    big = {}
    g_in = _reduced_grad(blocks_in, land1_in, land2_in, owners, "reduced_grad_w_in")[:, :IN_SHARD]
    d_, m_, v_ = _adamw(w_in[0], g_in, m_w_in[0], v_w_in[0], "adamw_w_in")
    big["w_in"] = (g_in[None], d_[None], m_[None], v_[None])
    early_wmv = [(w_branch_sgu, m_w_branch_sgu, v_w_branch_sgu), (w_branch_attn, m_w_branch_attn, v_w_branch_attn),
                 (w_out, m_w_out, v_w_out), (w_up, m_w_up, v_w_up), (w_down, m_w_down, v_w_down)]
    for nm, (w, m, v), g, l1, l2 in zip(early_names, early_wmv, early, early_land1, early_land2):
        big[nm] = tuple(t[None] for t in _adamw_reduced(w[0], m[0], v[0], g, l1, l2, owners, "adamw_" + nm))

    order = ["g_mix_pre", "w_in", "b_forget", "g_sgu", "b_sgu", "w_spatial", "b_spatial", "w_branch_sgu",
             "w_branch_attn", "w_out", "g_mix_post", "g_ffn_pre", "w_up", "w_down", "g_ffn_post"]
    outs = [loss, dx.reshape(1, T, D_MODEL)]
    for kind, small in enumerate((sg, sd, sm, sv)):
        outs += [big[nm][kind] if nm in big else small[nm] for nm in order]
    return tuple(outs)
```

```python
import math

import jax
import jax.numpy as jnp
from jax import lax
from jax.experimental import pallas as pl
from jax.experimental.pallas import tpu as pltpu

F32 = jnp.float32
BF16 = jnp.bfloat16
HIGHEST = lax.Precision.HIGHEST
MESH = pl.DeviceIdType.MESH

D_MODEL = 1024
SGU_WIDTH = 512
ATTN_WIDTH = 512
N_HEADS = 8
CHUNK = 128
D_FF = 4096
IN_WIDTH = 4616
N_DEV = 8
IN_SHARD = IN_WIDTH // N_DEV
IN_SHARD_PAD = 640
ZQKV_WIDTH = 2 * SGU_WIDTH + 3 * ATTN_WIDTH
GATE_OFFSET = ZQKV_WIDTH + N_HEADS
EPS = 1e-6
LANES = 128
SUBLANES = 8
VMEM_BYTES = 64 * 1024 * 1024
MIB = 1024 * 1024

ADAM_LR = 0.001
ADAM_B1 = 0.9
ADAM_B2 = 0.999
ADAM_EPS = 1e-08
ADAM_WD = 0.01
ADAM_STEP = 10

TOKEN_TILE = 512
MATMUL_TILE = 512
MERGE_BWD_TILE = 512
ATTN_TILE = 512
CUM_TILE = 256
SGU_TILE = 512
WGRAD_TILE = 1024
NEG = -1e30

NT_DIMS = (((1,), (1,)), ((), ()))
TN_DIMS = (((0,), (0,)), ((), ()))


def _params(vmem_mb, n_grid=1):
    return pltpu.CompilerParams(
        dimension_semantics=("arbitrary",) * n_grid,
        vmem_limit_bytes=min(vmem_mb * MIB, VMEM_BYTES - 6 * MIB),
    )


def _dot(a, b):
    return jnp.dot(a, b, preferred_element_type=F32)


def _dot_nt(a, b):
    return lax.dot_general(a, b, NT_DIMS, preferred_element_type=F32)


def _dot_tn(a, b):
    return lax.dot_general(a, b, TN_DIMS, preferred_element_type=F32)


def _const_spec(shape):
    nd = len(shape)
    return pl.BlockSpec(shape, lambda *_: (0,) * nd, pipeline_mode=pl.Buffered(1))


def _row_spec(tm, n, col=0):
    return pl.BlockSpec((tm, n), lambda i: (i, col))


def _fold8(v):
    return v.reshape(v.shape[0] // SUBLANES, SUBLANES, v.shape[1]).sum(axis=0)


def _pick(v, lane_iota, k):
    return jnp.sum(jnp.where(lane_iota == k, v, 0.0), axis=1, keepdims=True)


def _iota(shape, dim):
    return lax.broadcasted_iota(jnp.int32, shape, dim)


def _gelu(x):
    c = 0.7978845608028654
    return 0.5 * x * (1.0 + jnp.tanh(c * (x + 0.044715 * x * x * x)))


def _gelu_grad(x):
    c = 0.7978845608028654
    t = jnp.tanh(c * (x + 0.044715 * x * x * x))
    return 0.5 * (1.0 + t) + 0.5 * x * (1.0 - t * t) * (c * (1.0 + 3.0 * 0.044715 * x * x))


def _rms_stats(v):
    r = lax.rsqrt(jnp.mean(v * v, axis=1, keepdims=True) + EPS)
    return r, v * r


def _rms_bwd(dout, vhat, r, g):
    a = dout * g
    return r * (a - vhat * jnp.mean(a * vhat, axis=1, keepdims=True))


def _mesh_pos():
    return lax.axis_index("x"), lax.axis_index("y"), lax.axis_index("c")


def _dev_index(px, py, pc):
    return 4 * px + 2 * py + pc


def _other_chips(x, y):
    return [(1 - x, y), (x, 1 - y), (1 - x, 1 - y)]


class _WeightGather:
    def __init__(self, shard_shapes, kinds, stage_shapes=None):
        self.shard_shapes = list(shard_shapes)
        self.kinds = list(kinds)
        self.stage_shapes = list(stage_shapes or shard_shapes)
        self.n = len(self.kinds)

    def out_shapes(self):
        shapes = []
        for (rows, cols), kind in zip(self.stage_shapes, self.kinds):
            full = {"block": (N_DEV, rows, cols), "rows": (N_DEV * rows, cols), "cols": (rows, N_DEV * cols)}[kind]
            shapes.append(jax.ShapeDtypeStruct(full, BF16))
        return shapes

    def scratch_shapes(self):
        return ([pltpu.VMEM(s, BF16) for s in self.stage_shapes]
                + [pltpu.SemaphoreType.DMA((self.n, 7)), pltpu.SemaphoreType.DMA((self.n, 7)),
                   pltpu.SemaphoreType.DMA((self.n,))])

    def _view(self, a, ref, j):
        rows, cols = self.stage_shapes[a]
        if self.kinds[a] == "block":
            return ref.at[j]
        if self.kinds[a] == "rows":
            return ref.at[pl.ds(pl.multiple_of(j * rows, rows), rows), :]
        return ref.at[:, pl.ds(pl.multiple_of(j * cols, cols), cols)]

    def _copy(self, outs, scratch, a, k, block, to, from_stage=False):
        stage, (send_sems, recv_sems, _) = scratch[:self.n], scratch[self.n:]
        dst = self._view(a, outs[a], _dev_index(*block))
        return pltpu.make_async_remote_copy(
            src_ref=stage[a] if from_stage else dst, dst_ref=dst,
            send_sem=send_sems.at[a, k], recv_sem=recv_sems.at[a, k],
            device_id=to, device_id_type=MESH)

    def _local(self, outs, scratch, a, me):
        return pltpu.make_async_copy(scratch[a], self._view(a, outs[a], _dev_index(*me)), scratch[-1].at[a])

    def start(self, ins, outs, scratch):
        x, y, c = _mesh_pos()
        me, sibling = (x, y, c), (x, y, 1 - c)
        for a in range(self.n):
            rows, cols = self.shard_shapes[a]
            if self.stage_shapes[a] != self.shard_shapes[a]:
                scratch[a][...] = jnp.zeros(self.stage_shapes[a], BF16)
            scratch[a][0:rows, 0:cols] = ins[a][...].astype(BF16)
            self._local(outs, scratch, a, me).start()
        for a in range(self.n):
            self._copy(outs, scratch, a, 0, me, sibling, True).start()
            for j, chip in enumerate(_other_chips(x, y)):
                self._copy(outs, scratch, a, 1 + j, me, (*chip, c), True).start()

    def forward(self, outs, scratch):
        x, y, c = _mesh_pos()
        me, sibling = (x, y, c), (x, y, 1 - c)
        for a in range(self.n):
            for j, chip in enumerate(_other_chips(x, y)):
                self._copy(outs, scratch, a, 1 + j, (*chip, c), me).wait_recv()
                self._copy(outs, scratch, a, 4 + j, (*chip, c), sibling).start()

    def finish(self, outs, scratch):
        x, y, c = _mesh_pos()
        me, sibling = (x, y, c), (x, y, 1 - c)
        chips = _other_chips(x, y)
        for a in range(self.n):
            self._copy(outs, scratch, a, 0, sibling, me).wait_recv()
            for j, chip in enumerate(chips):
                self._copy(outs, scratch, a, 4 + j, (*chip, 1 - c), me).wait_recv()
        for a in range(self.n):
            self._copy(outs, scratch, a, 0, me, sibling, True).wait_send()
            for j, chip in enumerate(chips):
                self._copy(outs, scratch, a, 1 + j, me, (*chip, c), True).wait_send()
                self._copy(outs, scratch, a, 4 + j, (*chip, c), sibling).wait_send()
            self._local(outs, scratch, a, me).wait()


def _gather_w_in(w_in_local):
    g = _WeightGather([(D_MODEL, IN_SHARD)], ["block"], [(D_MODEL, IN_SHARD_PAD)])

    def body(w_ref, out_ref, *scratch):
        g.start([w_ref], [out_ref], scratch)
        g.forward([out_ref], scratch)
        g.finish([out_ref], scratch)

    return pl.pallas_call(
        body,
        out_shape=g.out_shapes()[0],
        in_specs=[pl.BlockSpec(memory_space=pltpu.VMEM)],
        out_specs=pl.BlockSpec(memory_space=pl.ANY),
        scratch_shapes=g.scratch_shapes(),
        compiler_params=pltpu.CompilerParams(vmem_limit_bytes=32 * MIB),
        name="gather_w_in",
    )(w_in_local)


class _GradExchange:
    def __init__(self, shapes):
        self.shapes = [tuple(s) for s in shapes]
        self.n = len(self.shapes)

    def land_shapes(self, stage):
        slots, dtype = (4, F32) if stage == 1 else (3, BF16)
        return [jax.ShapeDtypeStruct((slots,) + s, dtype) for s in self.shapes]

    def sem_shapes(self, stage):
        slots = 4 if stage == 1 else 3
        return [pltpu.SemaphoreType.DMA((self.n, slots)), pltpu.SemaphoreType.DMA((self.n, slots))]

    def _copy(self, stage, srcs, lands, sems, a, k):
        x, y, c = _mesh_pos()
        cx, cy = (_other_chips(x, y) + [(x, y)])[k]
        if stage == 1:
            src, to = srcs[a].at[_dev_index(cx, cy, 1 - c)], (x, y, 1 - c)
        else:
            src, to = srcs[a].at[k], (cx, cy, c)
        return pltpu.make_async_remote_copy(
            src_ref=src, dst_ref=lands[a].at[k], send_sem=sems[0].at[a, k], recv_sem=sems[1].at[a, k],
            device_id=to, device_id_type=MESH)

    def start(self, stage, srcs, lands, sems):
        for a in range(self.n):
            for k in range(4 if stage == 1 else 3):
                self._copy(stage, srcs, lands, sems, a, k).start()

    def wait(self, stage, srcs, lands, sems):
        for a in range(self.n):
            for k in range(4 if stage == 1 else 3):
                cp = self._copy(stage, srcs, lands, sems, a, k)
                cp.wait_recv()
                cp.wait_send()


def _owner_indices():
    x, y, c = _mesh_pos()
    return jnp.stack([_dev_index(cx, cy, c) for cx, cy in _other_chips(x, y) + [(x, y)]]).astype(jnp.int32)


def _chip_partials(g, land1, idx, name):
    _, rows, cols = g.shape
    tr = min(rows, 256)

    def body(idx_ref, g_ref, l_ref, o_ref):
        o_ref[...] = (g_ref[...] + l_ref[...]).astype(BF16)

    return pl.pallas_call(
        body,
        grid_spec=pltpu.PrefetchScalarGridSpec(
            num_scalar_prefetch=1, grid=(3, rows // tr),
            in_specs=[pl.BlockSpec((None, tr, cols), lambda k, r, idx: (idx[k], r, 0)),
                      pl.BlockSpec((None, tr, cols), lambda k, r, idx: (k, r, 0))],
            out_specs=pl.BlockSpec((None, tr, cols), lambda k, r, idx: (k, r, 0))),
        out_shape=jax.ShapeDtypeStruct((3, rows, cols), BF16),
        compiler_params=_params(32, 2), name=name,
    )(idx, g, land1)


def _reduced_block(g_ref, l1_ref, a_ref, b_ref, c_ref):
    return ((g_ref[...] + l1_ref[...]) + a_ref[...].astype(F32)) + b_ref[...].astype(F32) + c_ref[...].astype(F32)


def _reduced_specs(tm, cols):
    return [pl.BlockSpec((None, tm, cols), lambda i, idx: (idx[3], i, 0)),
            pl.BlockSpec((None, tm, cols), lambda i, idx: (3, i, 0)),
            pl.BlockSpec((None, tm, cols), lambda i, idx: (0, i, 0)),
            pl.BlockSpec((None, tm, cols), lambda i, idx: (1, i, 0)),
            pl.BlockSpec((None, tm, cols), lambda i, idx: (2, i, 0))]


def _reduced_grad(g, land1, land2, idx, name):
    _, rows, cols = g.shape
    tm = min(rows, 256)

    def body(idx_ref, g_ref, l1_ref, a_ref, b_ref, c_ref, o_ref):
        o_ref[...] = _reduced_block(g_ref, l1_ref, a_ref, b_ref, c_ref)

    return pl.pallas_call(
        body,
        grid_spec=pltpu.PrefetchScalarGridSpec(
            num_scalar_prefetch=1, grid=(rows // tm,), in_specs=_reduced_specs(tm, cols),
            out_specs=pl.BlockSpec((tm, cols), lambda i, idx: (i, 0))),
        out_shape=jax.ShapeDtypeStruct((rows, cols), F32),
        compiler_params=_params(32), name=name,
    )(idx, g, land1, land2, land2, land2)


def _adamw_math(w, g, m, v):
    m = ADAM_B1 * m + (1.0 - ADAM_B1) * g
    v = ADAM_B2 * v + (1.0 - ADAM_B2) * (g * g)
    m_hat = m / (1.0 - ADAM_B1 ** ADAM_STEP)
    v_hat = v / (1.0 - ADAM_B2 ** ADAM_STEP)
    delta = -ADAM_LR * (m_hat / (jnp.sqrt(v_hat) + ADAM_EPS) + ADAM_WD * w)
    return delta, m, v


SMALL_NAMES = ("g_mix_pre", "b_forget", "g_sgu", "b_sgu", "w_spatial", "b_spatial", "g_mix_post", "g_ffn_pre",
               "g_ffn_post")
SMALL_SLOTS = {"g_mix_pre": (0, 1, 0, 1024), "g_mix_post": (1, 1, 0, 1024), "g_ffn_pre": (2, 1, 0, 1024),
               "g_ffn_post": (3, 1, 0, 1024), "g_sgu": (4, 1, 0, 512), "b_sgu": (4, 1, 512, 512),
               "b_forget": (5, 1, 0, 128), "b_spatial": (8, 8, 0, 128)}
SMALL_TILE = (16, 1024)
SPATIAL_TILE = (N_HEADS * CHUNK, CHUNK)


LOSS_SLOT = (8, 8, 128, 128)


def _allreduce_small(grads, loss_part):
    names = list(SMALL_NAMES)

    def body(*refs):
        g = dict(zip(names, refs[:len(names)]))
        loss_ref = refs[len(names)]
        tot_a, tot_b, buf_a, buf_b, sib_a, sib_b, ps_a, ps_b, land_a, land_b, send_sems, recv_sems = refs[len(names) + 1:]
        x, y, c = _mesh_pos()
        buf_a[...] = jnp.zeros(SMALL_TILE, F32)
        r0, nr, c0, nc = LOSS_SLOT
        buf_a[r0:r0 + nr, c0:c0 + nc] = loss_ref[...]
        for name, (r0, nr, c0, nc) in SMALL_SLOTS.items():
            val = g[name][...]
            if name == "b_spatial":
                val = jnp.transpose(val)[0:N_HEADS, :]
            buf_a[r0:r0 + nr, c0:c0 + nc] = val
        buf_b[...] = g["w_spatial"][...].reshape(SPATIAL_TILE)

        def swap(k, src, dst, to):
            return pltpu.make_async_remote_copy(src_ref=src, dst_ref=dst, send_sem=send_sems.at[k],
                                                recv_sem=recv_sems.at[k], device_id=to, device_id_type=MESH)

        first = [swap(0, buf_a, sib_a, (x, y, 1 - c)), swap(1, buf_b, sib_b, (x, y, 1 - c))]
        for cp in first:
            cp.start()
        for cp in first:
            cp.wait_recv()
        ps_a[...] = buf_a[...] + sib_a[...]
        ps_b[...] = buf_b[...] + sib_b[...]
        second = []
        for k, (cx, cy) in enumerate(_other_chips(x, y)):
            second += [swap(2 + 2 * k, ps_a, land_a.at[k], (cx, cy, c)), swap(3 + 2 * k, ps_b, land_b.at[k], (cx, cy, c))]
        for cp in second:
            cp.start()
        for cp in second:
            cp.wait_recv()
        tot_a[...] = (ps_a[...] + land_a[0]) + (land_a[1] + land_a[2])
        tot_b[...] = (ps_b[...] + land_b[0]) + (land_b[1] + land_b[2])
        for cp in first + second:
            cp.wait_send()

    vm = pl.BlockSpec(memory_space=pltpu.VMEM)
    return pl.pallas_call(
        body,
        out_shape=(jax.ShapeDtypeStruct(SMALL_TILE, F32), jax.ShapeDtypeStruct(SPATIAL_TILE, F32)),
        in_specs=[vm] * (len(names) + 1), out_specs=[vm, vm],
        scratch_shapes=[pltpu.VMEM(SMALL_TILE, F32), pltpu.VMEM(SPATIAL_TILE, F32)] * 3
        + [pltpu.VMEM((3,) + SMALL_TILE, F32), pltpu.VMEM((3,) + SPATIAL_TILE, F32),
           pltpu.SemaphoreType.DMA((8,)), pltpu.SemaphoreType.DMA((8,))],
        compiler_params=pltpu.CompilerParams(vmem_limit_bytes=32 * MIB),
        name="allreduce_small",
    )(*[grads[nm] for nm in names], loss_part)


def _adamw_small(tot_a, tot_b, ws, ms, vs):
    names = list(SMALL_NAMES)
    n = len(names)

    def body(a_ref, b_ref, *refs):
        w, m, v = (dict(zip(names, refs[i * n:(i + 1) * n])) for i in range(3))
        outs = [dict(zip(names, refs[(3 + i) * n:(4 + i) * n])) for i in range(4)]
        for name in names:
            if name == "w_spatial":
                g = b_ref[...].reshape(N_HEADS, CHUNK, CHUNK)
            else:
                r0, nr, c0, nc = SMALL_SLOTS[name]
                g = a_ref[r0:r0 + nr, c0:c0 + nc]
            vals = (g,) + _adamw_math(w[name][...], g, m[name][...], v[name][...])
            for out, val in zip(outs, vals):
                out[name][...] = val

    shapes = [jax.ShapeDtypeStruct(ws[nm].shape, F32) for nm in names]
    vm = pl.BlockSpec(memory_space=pltpu.VMEM)
    res = pl.pallas_call(
        body, out_shape=shapes * 4, in_specs=[vm] * (2 + 3 * n), out_specs=[vm] * (4 * n),
        compiler_params=pltpu.CompilerParams(vmem_limit_bytes=32 * MIB), name="adamw_small",
    )(tot_a, tot_b, *[d[nm] for d in (ws, ms, vs) for nm in names])
    return [dict(zip(names, res[i * n:(i + 1) * n])) for i in range(4)]


def _adamw_reduced(w, m, v, g, land1, land2, idx, name):
    rows, cols = w.shape
    tm = min(rows, 256)

    def body(idx_ref, w_ref, m_ref, v_ref, g_ref, l1_ref, a_ref, b_ref, c_ref, go_ref, d_ref, nm_ref, nv_ref):
        gsum = _reduced_block(g_ref, l1_ref, a_ref, b_ref, c_ref)
        go_ref[...] = gsum
        delta, nm, nv = _adamw_math(w_ref[...], gsum, m_ref[...], v_ref[...])
        d_ref[...] = delta
        nm_ref[...] = nm
        nv_ref[...] = nv

    sd = jax.ShapeDtypeStruct((rows, cols), F32)
    spec = pl.BlockSpec((tm, cols), lambda i, idx: (i, 0))
    return pl.pallas_call(
        body,
        grid_spec=pltpu.PrefetchScalarGridSpec(
            num_scalar_prefetch=1, grid=(rows // tm,), in_specs=[spec] * 3 + _reduced_specs(tm, cols),
            out_specs=[spec] * 4),
        out_shape=(sd, sd, sd, sd),
        compiler_params=_params(32), name=name,
    )(idx, w, m, v, g, land1, land2, land2, land2)


def _adamw(w, g, m, v, name):
    rows, cols = w.shape
    tm = 256 if rows % 256 == 0 else rows

    def body(w_ref, g_ref, m_ref, v_ref, d_ref, nm_ref, nv_ref):
        delta, nm, nv = _adamw_math(w_ref[...], g_ref[...], m_ref[...], v_ref[...])
        d_ref[...] = delta
        nm_ref[...] = nm
        nv_ref[...] = nv

    sd = jax.ShapeDtypeStruct((rows, cols), F32)
    spec = _row_spec(tm, cols)
    return pl.pallas_call(
        body, grid=(rows // tm,), out_shape=(sd, sd, sd), in_specs=[spec] * 4, out_specs=[spec] * 3,
        compiler_params=_params(32), name=name,
    )(w, g, m, v)


def _virtual_slab(sources, v0, v_end, like):
    lane = _iota(like.shape, 1)
    out = jnp.zeros(like.shape, like.dtype)
    for v_start, v_stop, read in sources:
        a, b = max(v0, v_start), min(v0 + LANES, v_stop, v_end)
        while a < b:
            c = a - v_start
            n = min(b - a, LANES - c % LANES)
            piece = read(c // LANES)
            shift = (a - v0 - c % LANES) % LANES
            if shift:
                piece = pltpu.roll(piece, shift, 1)
            out = jnp.where((lane >= a - v0) & (lane < a - v0 + n), piece, out)
            a += n
    return out


def _assemble_w_in(wg_in):
    tm = TOKEN_TILE

    def body(src_ref, wz_ref, wf_ref, wg_ref):
        like = src_ref[0, :, 0:LANES]
        sources = [(IN_SHARD * j, IN_SHARD * (j + 1),
                    (lambda k, j=j: src_ref[j, :, LANES * k:LANES * (k + 1)])) for j in range(N_DEV)]
        for k in range(ZQKV_WIDTH // LANES):
            wz_ref[:, LANES * k:LANES * (k + 1)] = _virtual_slab(sources, LANES * k, ZQKV_WIDTH, like)
        wf_ref[...] = _virtual_slab(sources, ZQKV_WIDTH, GATE_OFFSET, like)
        for k in range(2 * D_MODEL // LANES):
            wg_ref[:, LANES * k:LANES * (k + 1)] = _virtual_slab(sources, GATE_OFFSET + LANES * k, IN_WIDTH, like)

    return pl.pallas_call(
        body, grid=(D_MODEL // tm,),
        out_shape=(jax.ShapeDtypeStruct((D_MODEL, ZQKV_WIDTH), BF16), jax.ShapeDtypeStruct((D_MODEL, LANES), BF16),
                   jax.ShapeDtypeStruct((D_MODEL, 2 * D_MODEL), BF16)),
        in_specs=[pl.BlockSpec((N_DEV, tm, IN_SHARD_PAD), lambda i: (0, i, 0))],
        out_specs=[_row_spec(tm, ZQKV_WIDTH), _row_spec(tm, LANES), _row_spec(tm, 2 * D_MODEL)],
        compiler_params=_params(32), name="assemble_w_in",
    )(wg_in)


def _block_dw_in(pieces):
    tm = TOKEN_TILE
    widths = [2 * SGU_WIDTH, ATTN_WIDTH, ATTN_WIDTH, ATTN_WIDTH, N_HEADS, 2 * D_MODEL]
    starts = [sum(widths[:k]) for k in range(len(widths))]

    def body(*refs):
        in_refs, out_ref = refs[:-1], refs[-1]
        like = in_refs[0][:, 0:LANES]
        slab = lambda ref: (lambda k: ref[:, LANES * k:LANES * (k + 1)])
        sources = [(s, s + w, slab(ref)) for s, w, ref in zip(starts, widths, in_refs)]
        for j in range(N_DEV):
            for k in range(IN_SHARD_PAD // LANES):
                out_ref[j, :, LANES * k:LANES * (k + 1)] = _virtual_slab(
                    sources, IN_SHARD * j + LANES * k, IN_SHARD * (j + 1), like)

    return pl.pallas_call(
        body, grid=(D_MODEL // tm,),
        out_shape=jax.ShapeDtypeStruct((N_DEV, D_MODEL, IN_SHARD_PAD), F32),
        in_specs=[_row_spec(tm, pc.shape[1]) for pc in pieces],
        out_specs=pl.BlockSpec((N_DEV, tm, IN_SHARD_PAD), lambda i: (0, i, 0)),
        compiler_params=_params(32), name="block_dw_in",
    )(*pieces)


def _fwd_in(x2, g1, wz, wf, wg):
    T = x2.shape[0]
    tm = MATMUL_TILE

    def body(x_ref, g_ref, wz_ref, wf_ref, wg_ref, xn_ref, zuv_ref, qkv_ref, fl_ref, gt_ref):
        x = x_ref[...]
        r, xh = _rms_stats(x)
        xn = (xh * g_ref[...]).astype(BF16)
        xn_ref[...] = xn
        zuv_ref[...] = _dot(xn, wz_ref[:, 0:1024]).astype(BF16)
        qkv_ref[:, 0:512] = (_dot(xn, wz_ref[:, 1024:1536]) * 0.125).astype(BF16)
        qkv_ref[:, 512:1536] = _dot(xn, wz_ref[:, 1536:2560]).astype(BF16)
        fl_ref[...] = _dot(xn, wf_ref[...])
        gt_ref[...] = jax.nn.sigmoid(_dot(xn, wg_ref[...])).astype(BF16)

    return pl.pallas_call(
        body, grid=(T // tm,),
        out_shape=(jax.ShapeDtypeStruct((T, D_MODEL), BF16), jax.ShapeDtypeStruct((T, 1024), BF16),
                   jax.ShapeDtypeStruct((T, 1536), BF16), jax.ShapeDtypeStruct((T, LANES), F32),
                   jax.ShapeDtypeStruct((T, 2048), BF16)),
        in_specs=[_row_spec(tm, D_MODEL), _const_spec((1, D_MODEL)), _const_spec((D_MODEL, ZQKV_WIDTH)),
                  _const_spec((D_MODEL, LANES)), _const_spec((D_MODEL, 2048))],
        out_specs=[_row_spec(tm, D_MODEL), _row_spec(tm, 1024), _row_spec(tm, 1536), _row_spec(tm, LANES),
                   _row_spec(tm, 2048)],
        compiler_params=_params(48), name="fwd_in",
    )(x2, g1, wz, wf, wg)


def _log_sigmoid(f):
    return jnp.minimum(f, 0.0) - jnp.log1p(jnp.exp(-jnp.abs(f)))


AUG_LANES = 6


def _split3(v):
    hi = v.astype(BF16)
    r1 = v - hi.astype(F32)
    mid = r1.astype(BF16)
    lo = (r1 - mid.astype(F32)).astype(BF16)
    return hi, mid, lo


def _spread(parts, k0):
    n = len(parts)
    r, c = _iota((n * LANES, LANES), 0), _iota((n * LANES, LANES), 1)
    e = jnp.zeros((n * LANES, LANES), BF16)
    for i in range(n):
        h = r - i * LANES
        e = jnp.where((h >= 0) & (h < N_HEADS) & (c == AUG_LANES * h + (k0 + i)), jnp.ones_like(e), e)
    return _dot(jnp.concatenate(parts, axis=1), e)


def _aug_ones(shape, k0):
    lane = _iota(shape, 1)
    head = (lane * 43) >> 8
    slot = lane - AUG_LANES * head
    return ((lane < AUG_LANES * N_HEADS) & (slot >= k0) & (slot < k0 + 3)).astype(F32)


def _aug_query(v):
    return (_spread(_split3(v), 0) + _aug_ones(v.shape, 3)).astype(BF16)


def _aug_key(v):
    return (_aug_ones(v.shape, 0) - _spread(_split3(v), 3)).astype(BF16)


def _aug_stack(t2, aug, p):
    lane = _iota(t2.shape, 1)
    low = lane < 64
    zero = jnp.zeros_like(t2)
    first = 2 * AUG_LANES * p
    a_e = jnp.where((lane >= first) & (lane < first + AUG_LANES), aug, zero)
    a_o = jnp.where((lane >= first + AUG_LANES) & (lane < first + 2 * AUG_LANES), aug, zero)
    top = jnp.concatenate([jnp.where(low, t2, zero), a_e], axis=1)
    bot = jnp.concatenate([jnp.where(low, zero, t2), a_o], axis=1)
    return jnp.concatenate([top, bot], axis=0)


def _fwd_cum(fl, bfp):
    T = fl.shape[0]
    tb = CUM_TILE

    def body(fl_ref, b_ref, cc_ref, qa_ref, ka_ref):
        tri = (_iota((tb, tb), 0) >= _iota((tb, tb), 1)).astype(F32)
        carry = jnp.zeros((1, LANES), F32)
        for i in range(T // tb):
            rows = slice(i * tb, (i + 1) * tb)
            lf = _log_sigmoid(fl_ref[rows, :] + b_ref[...])
            cs = jnp.dot(tri, lf, precision=HIGHEST, preferred_element_type=F32) + carry
            cc_ref[rows, :] = cs
            carry = cs[tb - 1:tb, :]
            qa_ref[rows, :] = _aug_query(cs)
            ka_ref[rows, :] = _aug_key(cs)

    return pl.pallas_call(
        body,
        out_shape=(jax.ShapeDtypeStruct((T, LANES), F32), jax.ShapeDtypeStruct((T, LANES), BF16),
                   jax.ShapeDtypeStruct((T, LANES), BF16)),
        compiler_params=pltpu.CompilerParams(vmem_limit_bytes=32 * MIB), name="fwd_cum",
    )(fl, bfp)


def _sgu_forward_parts(z, gs, bs):
    u = _gelu(z[:, :SGU_WIDTH])
    vv = _gelu(z[:, SGU_WIDTH:])
    vc = vv - jnp.mean(vv, axis=1, keepdims=True)
    rs = lax.rsqrt(jnp.mean(vc * vc, axis=1, keepdims=True) + EPS)
    vhat = vc * rs
    return u, vhat, rs, vhat * gs + bs


def _sgu_pair_weights(w_ref, bT, p):
    tril = _iota((CHUNK, CHUNK), 0) >= _iota((CHUNK, CHUNK), 1)
    we = jnp.where(tril, w_ref[2 * p], 0.0).astype(BF16)
    wo = jnp.where(tril, w_ref[2 * p + 1], 0.0).astype(BF16)
    lane8 = _iota(bT.shape, 1)
    low = _iota((CHUNK, LANES), 1) < 64
    b2 = jnp.where(low, _pick(bT, lane8, 2 * p), _pick(bT, lane8, 2 * p + 1))
    return we, wo, b2


def _chunks_on_lanes(v, p, nc):
    return jnp.concatenate([v[c * CHUNK:(c + 1) * CHUNK, LANES * p:LANES * (p + 1)] for c in range(nc)], axis=1)


def _sgu_mix(we, wo, b2, vcat, nc):
    low = (_iota((CHUNK, nc * LANES), 1) % LANES) < 64
    return jnp.where(low, _dot(we, vcat), _dot(wo, vcat)) + jnp.concatenate([b2] * nc, axis=1)


def _fwd_sgu(zuv, gs, bs, wsp, bT):
    T = zuv.shape[0]
    tc = SGU_TILE
    nc = tc // CHUNK

    def body(z_ref, gs_ref, bs_ref, w_ref, bT_ref, y_ref):
        u, _, _, vln = _sgu_forward_parts(z_ref[...].astype(F32), gs_ref[...], bs_ref[...])
        vb = vln.astype(BF16)
        for p in range(4):
            we, wo, b2 = _sgu_pair_weights(w_ref, bT_ref[...], p)
            s = _sgu_mix(we, wo, b2, _chunks_on_lanes(vb, p, nc), nc)
            for c in range(nc):
                rows, cols = slice(c * CHUNK, (c + 1) * CHUNK), slice(LANES * p, LANES * (p + 1))
                y_ref[rows, cols] = (u[rows, cols] * s[:, c * LANES:(c + 1) * LANES]).astype(BF16)

    return pl.pallas_call(
        body, grid=(T // tc,), out_shape=jax.ShapeDtypeStruct((T, SGU_WIDTH), BF16),
        in_specs=[_row_spec(tc, 1024), _const_spec((1, SGU_WIDTH)), _const_spec((1, SGU_WIDTH)),
                  _const_spec((8, CHUNK, CHUNK)), _const_spec((CHUNK, 8))],
        out_specs=_row_spec(tc, SGU_WIDTH),
        compiler_params=_params(40), name="fwd_sgu",
    )(zuv, gs, bs, wsp, bT)


def _fwd_attn(qkv, qaug, kaug, w_shards):
    T = qkv.shape[0]
    tq = tk = ATTN_TILE
    nq = T // tq
    gather = _WeightGather([w.shape for w in w_shards], ["cols", "cols", "rows", "cols", "rows"])
    nw = gather.n

    def body(q_ref, qa_ref, k_ref, v_ref, ka_ref, *rest):
        w_refs, (o_ref, lse_ref), wg_refs, scratch = rest[:nw], rest[nw:nw + 2], rest[nw + 2:2 * nw + 2], rest[2 * nw + 2:]
        i = pl.program_id(0)

        @pl.when(i == 0)
        def _():
            gather.start(w_refs, wg_refs, scratch)

        @pl.when(i == nq // 2)
        def _():
            gather.forward(wg_refs, scratch)

        lane = _iota((tq, LANES), 1)
        low = lane < 64
        lowk = _iota((tk, LANES), 1) < 64
        one = jnp.ones((tk, LANES), BF16)
        row = _iota((2 * tq, tk), 0) % tq
        col = _iota((2 * tq, tk), 1)
        cols = [slice(LANES * p, LANES * (p + 1)) for p in range(4)]
        qa = qa_ref[...]
        qs = [_aug_stack(q_ref[:, cols[p]], qa, p) for p in range(4)]

        def step(j, carry, masked):
            ks = pl.ds(pl.multiple_of(j * tk, tk), tk)
            ka = ka_ref[ks, :]
            new = []
            for p in range(4):
                m, acc_e, acc_o = carry[p]
                v2 = v_ref[ks, cols[p]]
                s = _dot_nt(qs[p], jnp.concatenate([k_ref[ks, cols[p]], ka], axis=1))
                if masked:
                    s = jnp.where(col <= row, s, NEG)
                mn = jnp.maximum(m, jnp.max(s, axis=1, keepdims=True))
                al = jnp.exp(m - mn)
                pm = jnp.exp(s - mn).astype(BF16)
                acc_e = al[:tq] * acc_e + _dot(pm[:tq], jnp.where(lowk, v2, one))
                acc_o = al[tq:] * acc_o + _dot(pm[tq:], jnp.where(lowk, one, v2))
                new.append((mn, acc_e, acc_o))
            return tuple(new)

        init = tuple((jnp.full((2 * tq, 1), NEG, F32), jnp.zeros((tq, LANES), F32), jnp.zeros((tq, LANES), F32))
                     for _ in range(4))
        def trip(t, c):
            for u in range(3):
                c = step(3 * t + u, c, False)
            return c

        carry = lax.fori_loop(0, i // 3, trip, init)
        carry = lax.fori_loop(3 * (i // 3), i, lambda j, c: step(j, c, False), carry)
        carry = step(i, carry, True)
        lse_blk = jnp.zeros((tq, LANES), F32)
        for p in range(4):
            m, acc_e, acc_o = carry[p]
            l_e = pltpu.roll(acc_e, 64, 1)
            l_o = pltpu.roll(acc_o, 64, 1)
            o_ref[:, cols[p]] = jnp.where(low, acc_e / l_e, acc_o / l_o).astype(BF16)
            lse_blk = jnp.where(lane == 2 * p, m[:tq] + jnp.log(l_e), lse_blk)
            lse_blk = jnp.where(lane == 2 * p + 1, m[tq:] + jnp.log(acc_o), lse_blk)
        lse_ref[...] = lse_blk

        @pl.when(i == nq - 1)
        def _():
            gather.finish(wg_refs, scratch)

    return pl.pallas_call(
        body, grid=(nq,),
        out_shape=[jax.ShapeDtypeStruct((T, ATTN_WIDTH), BF16), jax.ShapeDtypeStruct((T, LANES), F32)]
        + gather.out_shapes(),
        in_specs=[_row_spec(tq, 512), _row_spec(tq, LANES),
                  pl.BlockSpec((T, 512), lambda i: (0, 1), pipeline_mode=pl.Buffered(1)),
                  pl.BlockSpec((T, 512), lambda i: (0, 2), pipeline_mode=pl.Buffered(1)),
                  _const_spec((T, LANES))] + [_const_spec(w.shape) for w in w_shards],
        out_specs=[_row_spec(tq, ATTN_WIDTH), _row_spec(tq, LANES)] + [pl.BlockSpec(memory_space=pl.ANY)] * nw,
        scratch_shapes=gather.scratch_shapes(),
        compiler_params=_params(58), name="fwd_attn",
    )(qkv, qaug, qkv, qkv, kaug, *w_shards)


def _fwd_merge(ys, ya, gt, x2, wbs, wba, wo, g2):
    T = x2.shape[0]
    tm = TOKEN_TILE

    def body(ys_ref, ya_ref, gt_ref, x_ref, wbs_ref, wba_ref, wo_ref, g2_ref, a_ref, b_ref, mg_ref, o_ref, h1_ref):
        A = _dot(ys_ref[...], wbs_ref[...])
        B = _dot(ya_ref[...], wba_ref[...])
        mg = (gt_ref[:, :D_MODEL].astype(F32) * A + gt_ref[:, D_MODEL:].astype(F32) * B).astype(BF16)
        o = _dot(mg, wo_ref[...])
        r2, oh = _rms_stats(o)
        a_ref[...] = A.astype(BF16)
        b_ref[...] = B.astype(BF16)
        mg_ref[...] = mg
        o_ref[...] = o.astype(BF16)
        h1_ref[...] = x_ref[...] + oh * g2_ref[...]

    sd = jax.ShapeDtypeStruct((T, D_MODEL), BF16)
    return pl.pallas_call(
        body, grid=(T // tm,),
        out_shape=(sd, sd, sd, sd, jax.ShapeDtypeStruct((T, D_MODEL), F32)),
        in_specs=[_row_spec(tm, 512), _row_spec(tm, 512), _row_spec(tm, 2048), _row_spec(tm, D_MODEL),
                  _const_spec((512, D_MODEL)), _const_spec((512, D_MODEL)), _const_spec((D_MODEL, D_MODEL)),
                  _const_spec((1, D_MODEL))],
        out_specs=[_row_spec(tm, D_MODEL)] * 5,
        compiler_params=_params(40), name="fwd_merge",
    )(ys, ya, gt, x2, wbs, wba, wo, g2)


def _fwd_ffn_loss(h1, tgt, wup, wdn, g3, g4):
    T = h1.shape[0]
    tm = MATMUL_TILE
    nsteps = T // tm

    def body(h1_ref, tg_ref, wup_ref, wdn_ref, g3_ref, g4_ref, xn2_ref, a_ref, ddn_ref, dy_ref, loss_ref,
             dg4_ref, acc_l, acc_g):
        i = pl.program_id(0)

        @pl.when(i == 0)
        def _():
            acc_l[...] = jnp.zeros_like(acc_l)
            acc_g[...] = jnp.zeros_like(acc_g)

        h1v = h1_ref[...]
        r3, h1h = _rms_stats(h1v)
        xn2 = (h1h * g3_ref[...]).astype(BF16)
        xn2_ref[...] = xn2
        dn = jnp.zeros((tm, D_MODEL), F32)
        for j in range(D_FF // 1024):
            cols = slice(1024 * j, 1024 * (j + 1))
            a = _dot(xn2, wup_ref[:, cols])
            a_ref[:, cols] = a.astype(BF16)
            hid = jnp.square(jnp.maximum(a, 0.0)).astype(BF16)
            dn = dn + _dot(hid, wdn_ref[cols, :])
        r4, dnh = _rms_stats(dn)
        g4v = g4_ref[...]
        e = (h1v + dnh * g4v) - tg_ref[...]
        sq = e * e
        s1 = sq[:, 0:LANES]
        for j in range(1, D_MODEL // LANES):
            s1 = s1 + sq[:, LANES * j:LANES * (j + 1)]
        acc_l[...] += _fold8(s1)
        dy = e * (1.0 / D_MODEL)
        dy_ref[...] = dy
        acc_g[...] += _fold8(dy * dnh)
        ddn_ref[...] = _rms_bwd(dy, dnh, r4, g4v).astype(BF16)

        @pl.when(i == nsteps - 1)
        def _():
            loss_ref[...] = acc_l[...] * (0.5 / D_MODEL)
            dg4_ref[...] = jnp.sum(acc_g[...], axis=0, keepdims=True)

    return pl.pallas_call(
        body, grid=(nsteps,),
        out_shape=(jax.ShapeDtypeStruct((T, D_MODEL), BF16), jax.ShapeDtypeStruct((T, D_FF), BF16),
                   jax.ShapeDtypeStruct((T, D_MODEL), BF16), jax.ShapeDtypeStruct((T, D_MODEL), F32),
                   jax.ShapeDtypeStruct((SUBLANES, LANES), F32), jax.ShapeDtypeStruct((1, D_MODEL), F32)),
        in_specs=[_row_spec(tm, D_MODEL), _row_spec(tm, D_MODEL), _const_spec((D_MODEL, D_FF)),
                  _const_spec((D_FF, D_MODEL)), _const_spec((1, D_MODEL)), _const_spec((1, D_MODEL))],
        out_specs=[_row_spec(tm, D_MODEL), _row_spec(tm, D_FF), _row_spec(tm, D_MODEL), _row_spec(tm, D_MODEL),
                   pl.BlockSpec((SUBLANES, LANES), lambda i: (0, 0)), pl.BlockSpec((1, D_MODEL), lambda i: (0, 0))],
        scratch_shapes=[pltpu.VMEM((SUBLANES, LANES), F32), pltpu.VMEM((SUBLANES, D_MODEL), F32)],
        compiler_params=_params(52), name="fwd_ffn_loss",
    )(h1, tgt, wup, wdn, g3, g4)


def _bwd_ffn(ddn, a, dy, h1, wup, wdn, g3):
    T = h1.shape[0]
    tm = MATMUL_TILE
    nsteps = T // tm

    def body(ddn_ref, a_ref, dy_ref, h1_ref, wup_ref, wdn_ref, g3_ref, da_ref, dh1_ref, dg3_ref, acc_g):
        i = pl.program_id(0)

        @pl.when(i == 0)
        def _():
            acc_g[...] = jnp.zeros_like(acc_g)

        ddnv = ddn_ref[...]
        dxn2 = jnp.zeros((tm, D_MODEL), F32)
        for j in range(D_FF // 1024):
            cols = slice(1024 * j, 1024 * (j + 1))
            dhid = _dot_nt(ddnv, wdn_ref[cols, :])
            da = (dhid * (2.0 * jnp.maximum(a_ref[:, cols].astype(F32), 0.0))).astype(BF16)
            da_ref[:, cols] = da
            dxn2 = dxn2 + _dot_nt(da, wup_ref[:, cols])
        r3, h1h = _rms_stats(h1_ref[...])
        acc_g[...] += _fold8(dxn2 * h1h)
        dh1_ref[...] = dy_ref[...] + _rms_bwd(dxn2, h1h, r3, g3_ref[...])

        @pl.when(i == nsteps - 1)
        def _():
            dg3_ref[...] = jnp.sum(acc_g[...], axis=0, keepdims=True)

    return pl.pallas_call(
        body, grid=(nsteps,),
        out_shape=(jax.ShapeDtypeStruct((T, D_FF), BF16), jax.ShapeDtypeStruct((T, D_MODEL), F32),
                   jax.ShapeDtypeStruct((1, D_MODEL), F32)),
        in_specs=[_row_spec(tm, D_MODEL), _row_spec(tm, D_FF), _row_spec(tm, D_MODEL), _row_spec(tm, D_MODEL),
                  _const_spec((D_MODEL, D_FF)), _const_spec((D_FF, D_MODEL)), _const_spec((1, D_MODEL))],
        out_specs=[_row_spec(tm, D_FF), _row_spec(tm, D_MODEL), pl.BlockSpec((1, D_MODEL), lambda i: (0, 0))],
        scratch_shapes=[pltpu.VMEM((SUBLANES, D_MODEL), F32)],
        compiler_params=_params(52), name="bwd_ffn",
    )(ddn, a, dy, h1, wup, wdn, g3)


def _wgrad(xa, dy, name, relu2=False, tn=None, block_cols=None):
    T, K = xa.shape
    N = dy.shape[1]
    tn = N if tn is None else tn
    tt = min(T, WGRAD_TILE if K <= D_MODEL else WGRAD_TILE // 2)
    if block_cols:
        nb = tn // block_cols
        out_shape = jax.ShapeDtypeStruct((N // block_cols, K, block_cols), F32)
        out_spec = pl.BlockSpec((nb, K, block_cols), lambda n, t: (n, 0, 0))
    else:
        out_shape = jax.ShapeDtypeStruct((K, N), F32)
        out_spec = pl.BlockSpec((K, tn), lambda n, t: (0, n))

    def body(x_ref, dy_ref, o_ref):
        @pl.when(pl.program_id(1) == 0)
        def _():
            o_ref[...] = jnp.zeros_like(o_ref)

        xv = x_ref[...]
        if relu2:
            xv = jnp.square(jnp.maximum(xv.astype(F32), 0.0)).astype(BF16)
        if block_cols:
            for b in range(nb):
                o_ref[b] += _dot_tn(xv, dy_ref[:, block_cols * b:block_cols * (b + 1)])
        else:
            o_ref[...] += _dot_tn(xv, dy_ref[...])

    return pl.pallas_call(
        body, grid=(N // tn, T // tt), out_shape=out_shape,
        in_specs=[pl.BlockSpec((tt, K), lambda n, t: (t, 0)), pl.BlockSpec((tt, tn), lambda n, t: (t, n))],
        out_specs=out_spec,
        compiler_params=_params(52, 2), name=name,
    )(xa, dy)


def _wgrad_multi(xa, dys, name):
    T, K = xa.shape
    tt = min(T, WGRAD_TILE)
    n = len(dys)

    def body(x_ref, *refs):
        dy_refs, o_refs = refs[:n], refs[n:]

        @pl.when(pl.program_id(0) == 0)
        def _():
            for o_ref in o_refs:
                o_ref[...] = jnp.zeros_like(o_ref)

        xv = x_ref[...]
        for dy_ref, o_ref in zip(dy_refs, o_refs):
            o_ref[...] += _dot_tn(xv, dy_ref[...])

    return pl.pallas_call(
        body, grid=(T // tt,),
        out_shape=[jax.ShapeDtypeStruct((K, dy.shape[1]), F32) for dy in dys],
        in_specs=[_row_spec(tt, K)] + [_row_spec(tt, dy.shape[1]) for dy in dys],
        out_specs=[pl.BlockSpec((K, dy.shape[1]), lambda t: (0, 0)) for dy in dys],
        compiler_params=_params(52), name=name,
    )(xa, *dys)


def _bwd_merge(dh1, o, A, B, gt, ys, ya, mg, lse, cc, wbs, wba, wo, g2):
    T = dh1.shape[0]
    tm = MERGE_BWD_TILE
    nsteps = T // tm

    def body(dh1_ref, o_ref, a_ref, b_ref, gt_ref, ys_ref, ya_ref, mg_ref, lse_ref, cc_ref, wbs_ref, wba_ref,
             wo_ref, g2_ref, dgl_ref, dys_ref, dya_ref, qab_ref, dab_ref, dg2_ref, dwbs_ref, dwba_ref, dwo_ref,
             acc_g):
        i = pl.program_id(0)

        @pl.when(i == 0)
        def _():
            acc_g[...] = jnp.zeros_like(acc_g)
            dwbs_ref[...] = jnp.zeros_like(dwbs_ref)
            dwba_ref[...] = jnp.zeros_like(dwba_ref)
            dwo_ref[...] = jnp.zeros_like(dwo_ref)

        dh1v = dh1_ref[...]
        r2, oh = _rms_stats(o_ref[...].astype(F32))
        acc_g[...] += _fold8(dh1v * oh)
        do = _rms_bwd(dh1v, oh, r2, g2_ref[...]).astype(BF16)
        dwo_ref[...] += _dot_tn(mg_ref[...], do)
        dmg = _dot_nt(do, wo_ref[...])
        ga = gt_ref[:, :D_MODEL].astype(F32)
        gb = gt_ref[:, D_MODEL:].astype(F32)
        dgl_ref[:, :D_MODEL] = (dmg * a_ref[...].astype(F32) * ga * (1.0 - ga)).astype(BF16)
        dgl_ref[:, D_MODEL:] = (dmg * b_ref[...].astype(F32) * gb * (1.0 - gb)).astype(BF16)
        dA = (dmg * ga).astype(BF16)
        dB = (dmg * gb).astype(BF16)
        dwbs_ref[...] += _dot_tn(ys_ref[...], dA)
        dwba_ref[...] += _dot_tn(ya_ref[...], dB)
        dys_ref[...] = _dot_nt(dA, wbs_ref[...]).astype(BF16)
        dya = _dot_nt(dB, wba_ref[...]).astype(BF16)
        dya_ref[...] = dya
        prod = dya.astype(F32) * ya_ref[...].astype(F32)
        lane = _iota((tm, LANES), 1)
        low = lane < 64
        blk = jnp.zeros((tm, LANES), F32)
        for p in range(4):
            pp = prod[:, LANES * p:LANES * (p + 1)]
            blk = jnp.where(lane == 2 * p, jnp.sum(jnp.where(low, pp, 0.0), axis=1, keepdims=True), blk)
            blk = jnp.where(lane == 2 * p + 1, jnp.sum(jnp.where(low, 0.0, pp), axis=1, keepdims=True), blk)
        qab_ref[...] = _aug_query(cc_ref[...] - lse_ref[...])
        dab_ref[...] = _spread(_split3(-blk), 0).astype(BF16)

        @pl.when(i == nsteps - 1)
        def _():
            dg2_ref[...] = jnp.sum(acc_g[...], axis=0, keepdims=True)

    sh = jax.ShapeDtypeStruct((T, 512), BF16)
    sa = jax.ShapeDtypeStruct((T, LANES), BF16)
    sw = jax.ShapeDtypeStruct((512, D_MODEL), F32)
    whole = lambda shape: pl.BlockSpec(shape, lambda i: (0, 0))
    return pl.pallas_call(
        body, grid=(nsteps,),
        out_shape=(jax.ShapeDtypeStruct((T, 2048), BF16), sh, sh, sa, sa, jax.ShapeDtypeStruct((1, D_MODEL), F32),
                   sw, sw, jax.ShapeDtypeStruct((D_MODEL, D_MODEL), F32)),
        in_specs=[_row_spec(tm, D_MODEL)] * 4 + [_row_spec(tm, 2048), _row_spec(tm, 512), _row_spec(tm, 512),
                  _row_spec(tm, D_MODEL), _row_spec(tm, LANES), _row_spec(tm, LANES),
                  _const_spec((512, D_MODEL)), _const_spec((512, D_MODEL)),
                  _const_spec((D_MODEL, D_MODEL)), _const_spec((1, D_MODEL))],
        out_specs=[_row_spec(tm, 2048), _row_spec(tm, 512), _row_spec(tm, 512), _row_spec(tm, LANES),
                   _row_spec(tm, LANES), whole((1, D_MODEL)), whole((512, D_MODEL)), whole((512, D_MODEL)),
                   whole((D_MODEL, D_MODEL))],
        scratch_shapes=[pltpu.VMEM((SUBLANES, D_MODEL), F32)],
        compiler_params=_params(56), name="bwd_merge",
    )(dh1, o, A, B, gt, ys, ya, mg, lse, cc, wbs, wba, wo, g2)


def _bwd_sgu(zuv, dys, gs, bs, wsp, bT, grads):
    T = zuv.shape[0]
    tc = SGU_TILE
    nc = tc // CHUNK
    nsteps = T // tc
    ex = _GradExchange([tuple(g.shape[1:]) for g in grads])
    ng = ex.n

    def body(z_ref, dy_ref, gs_ref, bs_ref, w_ref, bT_ref, *rest):
        g_refs, (dz_ref, dw_ref, dbT_ref, dgs_ref, dbs_ref) = rest[:ng], rest[ng:ng + 5]
        land1 = rest[ng + 5:2 * ng + 5]
        acc_w, acc_b, acc_gs, acc_bs, dvln_s = rest[2 * ng + 5:2 * ng + 10]
        ex_sems = rest[2 * ng + 10:]
        i = pl.program_id(0)

        @pl.when(i == 0)
        def _():
            ex.start(1, g_refs, land1, ex_sems)
            acc_w[...] = jnp.zeros_like(acc_w)
            acc_b[...] = jnp.zeros_like(acc_b)
            acc_gs[...] = jnp.zeros_like(acc_gs)
            acc_bs[...] = jnp.zeros_like(acc_bs)

        z = z_ref[...].astype(F32)
        gsv = gs_ref[...]
        u, vhat, rs, vln = _sgu_forward_parts(z, gsv, bs_ref[...])
        vb = vln.astype(BF16)
        dy = dy_ref[...].astype(F32)
        low_w = (_iota((CHUNK, nc * LANES), 1) % LANES) < 64
        for p in range(4):
            we, wo, b2 = _sgu_pair_weights(w_ref, bT_ref[...], p)
            vcat = _chunks_on_lanes(vb, p, nc)
            s = _sgu_mix(we, wo, b2, vcat, nc)
            dyc = _chunks_on_lanes(dy, p, nc)
            ds = dyc * _chunks_on_lanes(u, p, nc)
            dsb = ds.astype(BF16)
            zero = jnp.zeros_like(dsb)
            dse = jnp.where(low_w, dsb, zero)
            dso = jnp.where(low_w, zero, dsb)
            acc_w[2 * p] += _dot_nt(dse, vcat)
            acc_w[2 * p + 1] += _dot_nt(dso, vcat)
            acc_b[p] += ds
            dvl = jnp.where(low_w, _dot_tn(we, dsb), _dot_tn(wo, dsb))
            for c in range(nc):
                rows, cols = slice(c * CHUNK, (c + 1) * CHUNK), slice(LANES * p, LANES * (p + 1))
                dvln_s[rows, cols] = dvl[:, c * LANES:(c + 1) * LANES]
                du = dy[rows, cols] * s[:, c * LANES:(c + 1) * LANES]
                dz_ref[rows, cols] = (du * _gelu_grad(z[rows, cols])).astype(BF16)
        dvln = dvln_s[...]
        acc_gs[...] += _fold8(dvln * vhat)
        acc_bs[...] += _fold8(dvln)
        al = dvln * gsv
        dvv = rs * (al - jnp.mean(al, axis=1, keepdims=True) - vhat * jnp.mean(al * vhat, axis=1, keepdims=True))
        dz_ref[:, SGU_WIDTH:] = (dvv * _gelu_grad(z[:, SGU_WIDTH:])).astype(BF16)

        @pl.when(i == nsteps - 1)
        def _():
            tril = _iota((CHUNK, CHUNK), 0) >= _iota((CHUNK, CHUNK), 1)
            lane = _iota((CHUNK, LANES), 1)
            low = lane < 64
            blk = jnp.zeros((CHUNK, LANES), F32)
            for g in range(8):
                dw_ref[g] = jnp.where(tril, acc_w[g], 0.0)
            for p in range(4):
                t = acc_b[p]
                tot = t[:, 0:LANES]
                for c in range(1, nc):
                    tot = tot + t[:, c * LANES:(c + 1) * LANES]
                blk = jnp.where(lane == 2 * p, jnp.sum(jnp.where(low, tot, 0.0), axis=1, keepdims=True), blk)
                blk = jnp.where(lane == 2 * p + 1, jnp.sum(jnp.where(low, 0.0, tot), axis=1, keepdims=True), blk)
            dbT_ref[...] = blk
            dgs_ref[...] = jnp.sum(acc_gs[...], axis=0, keepdims=True)
            dbs_ref[...] = jnp.sum(acc_bs[...], axis=0, keepdims=True)
            ex.wait(1, g_refs, land1, ex_sems)

    whole = lambda shape: pl.BlockSpec(shape, lambda i: (0,) * len(shape))
    hbm_spec = pl.BlockSpec(memory_space=pl.ANY)
    return pl.pallas_call(
        body, grid=(nsteps,),
        out_shape=[jax.ShapeDtypeStruct((T, 1024), BF16), jax.ShapeDtypeStruct((8, CHUNK, CHUNK), F32),
                   jax.ShapeDtypeStruct((CHUNK, LANES), F32), jax.ShapeDtypeStruct((1, SGU_WIDTH), F32),
                   jax.ShapeDtypeStruct((1, SGU_WIDTH), F32)] + ex.land_shapes(1),
        in_specs=[_row_spec(tc, 1024), _row_spec(tc, SGU_WIDTH), _const_spec((1, SGU_WIDTH)),
                  _const_spec((1, SGU_WIDTH)), _const_spec((8, CHUNK, CHUNK)), _const_spec((CHUNK, 8))]
        + [hbm_spec] * ng,
        out_specs=[_row_spec(tc, 1024), whole((8, CHUNK, CHUNK)), whole((CHUNK, LANES)),
                   whole((1, SGU_WIDTH)), whole((1, SGU_WIDTH))] + [hbm_spec] * ng,
        scratch_shapes=[pltpu.VMEM((8, CHUNK, CHUNK), F32), pltpu.VMEM((4, CHUNK, nc * LANES), F32),
                        pltpu.VMEM((SUBLANES, SGU_WIDTH), F32), pltpu.VMEM((SUBLANES, SGU_WIDTH), F32),
                        pltpu.VMEM((tc, SGU_WIDTH), F32)] + ex.sem_shapes(1),
        compiler_params=_params(48), name="bwd_sgu",
    )(zuv, dys, gs, bs, wsp, bT, *grads)


def _bwd_attn(qkv, dya, qab, dab, kaug, parts):
    T = qkv.shape[0]
    tq = tk = ATTN_TILE
    nq = T // tq
    nk = T // tk
    ex = _GradExchange([tuple(g.shape[1:]) for g in parts])
    nr = ex.n

    def body(q_ref, do_ref, qa_ref, da_ref, k_ref, v_ref, ka_ref, *rest):
        part_refs, (dq_ref, dk_ref, dv_ref, dcx_ref) = rest[:nr], rest[nr:nr + 4]
        land2, dq_acc, ex_sems = rest[nr + 4:2 * nr + 4], rest[2 * nr + 4], rest[2 * nr + 5:]
        p = pl.program_id(0)
        j = pl.program_id(1)

        @pl.when((p == 0) & (j == 0))
        def _():
            ex.start(2, part_refs, land2, ex_sems)

        lane = _iota((tq, LANES), 1)
        low = lane < 64
        row = _iota((2 * tq, tk), 0) % tq
        col = _iota((2 * tq, tk), 1)
        first = 2 * AUG_LANES * p
        half = tq // 2

        @pl.when(j == 0)
        def _():
            dq_acc[...] = jnp.zeros_like(dq_acc)

        @pl.when((j == 0) & (p == 0))
        def _():
            dcx_ref[...] = jnp.zeros_like(dcx_ref)

        ka = ka_ref[...]
        kk = jnp.concatenate([k_ref[...], ka], axis=1)
        vv = jnp.concatenate([v_ref[...], ka], axis=1)

        def tile(i, carry, r0, c0, nc, masked):
            dk_a, dv_a = carry
            nr = tq - r0
            qsl = pl.ds(pl.multiple_of(i * tq + r0, half), nr)
            qs = _aug_stack(q_ref[qsl, :], qa_ref[qsl, :], p)
            dos = _aug_stack(do_ref[qsl, :], da_ref[qsl, :], p)
            kc, vc = kk[c0:c0 + nc], vv[c0:c0 + nc]
            s = _dot_nt(qs, kc)
            if masked:
                s = jnp.where(col[:2 * nr, :nc] + c0 <= row[:2 * nr, :nc] % nr + r0, s, NEG)
            pm = jnp.exp(s)
            ds = pm * _dot_nt(dos, vc)
            dsb = ds.astype(BF16)
            dv_u = _dot_tn(pm.astype(BF16), dos[:, :LANES])
            dk_u = _dot_tn(dsb, qs)
            if nc == tk:
                dv_a, dk_a = dv_a + dv_u, dk_a + dk_u
            else:
                pad = lambda u: jnp.concatenate(
                    [jnp.zeros((n, u.shape[1]), F32) if z else u
                     for z, n in ((True, c0), (False, nc), (True, tk - c0 - nc)) if n], axis=0)
                dv_a, dk_a = dv_a + pad(dv_u), dk_a + pad(dk_u)
            dqx = _dot(dsb, kc)
            dq_acc[qsl, :] += jnp.where(low[:nr], dqx[:nr, :LANES], dqx[nr:, :LANES])
            dcx_ref[qsl, :] += (jnp.where(lane[:nr] == first, dqx[:nr, LANES:], 0.0)
                                + jnp.where(lane[:nr] == first + AUG_LANES, dqx[nr:, LANES:], 0.0))
            return dk_a, dv_a

        def q_block(i, carry, masked):
            return tile(i, carry, 0, 0, tk, masked)

        init =(jnp.zeros((tk, 2 * LANES), F32), jnp.zeros((tk, LANES), F32))
        carry = tile(j, init, 0, 0, half, True)
        carry = tile(j, carry, half, half, half, True)
        n_rest = nq - 1 - j

        def trip(t, c):
            for u in range(3):
                c = q_block(j + 1 + 3 * t + u, c, False)
            return c

        carry = lax.fori_loop(0, n_rest // 3, trip, carry)
        dk_a, dv_a = lax.fori_loop(j + 1 + 3 * (n_rest // 3), nq, lambda i, c: q_block(i, c, False), carry)
        dk_ref[...] = dk_a[:, :LANES].astype(BF16)
        dv_ref[...] = dv_a.astype(BF16)
        ksl = pl.ds(pl.multiple_of(j * tk, tk), tk)
        lk = _iota((tk, LANES), 1)
        dcx_ref[ksl, :] += jnp.where((lk == first + 3) | (lk == first + AUG_LANES + 3), dk_a[:, LANES:], 0.0)

        @pl.when(j == nk - 1)
        def _():
            dq_ref[...] = (dq_acc[...] * 0.125).astype(BF16)

        @pl.when((p == 3) & (j == nk - 1))
        def _():
            ex.wait(2, part_refs, land2, ex_sems)

    sh = jax.ShapeDtypeStruct((T, ATTN_WIDTH), BF16)
    full = lambda cb: pl.BlockSpec((T, LANES), lambda p, j: (0, cb + p))
    blk = lambda cb: pl.BlockSpec((tk, LANES), lambda p, j: (j, cb + p))
    hbm_spec = pl.BlockSpec(memory_space=pl.ANY)
    return pl.pallas_call(
        body, grid=(4, nk),
        out_shape=[sh, sh, sh, jax.ShapeDtypeStruct((T, LANES), F32)] + ex.land_shapes(2),
        in_specs=[full(0), full(0), _const_spec((T, LANES)), _const_spec((T, LANES)), blk(4), blk(8),
                  pl.BlockSpec((tk, LANES), lambda p, j: (j, 0))] + [hbm_spec] * nr,
        out_specs=[full(0), blk(0), blk(0), pl.BlockSpec((T, LANES), lambda p, j: (0, 0))] + [hbm_spec] * nr,
        scratch_shapes=[pltpu.VMEM((T, LANES), F32)] + ex.sem_shapes(2),
        compiler_params=_params(58, 2), name="bwd_attn",
    )(qkv, dya, qab, dab, qkv, qkv, kaug, *parts)


def _bwd_cum(dcx, fl, bfp):
    T = fl.shape[0]
    tb = CUM_TILE

    def body(dcx_ref, fl_ref, b_ref, dfl_ref, dbf_ref):
        triu = (_iota((tb, tb), 0) <= _iota((tb, tb), 1)).astype(F32)
        r, c = _iota((LANES, LANES), 0), _iota((LANES, LANES), 1)
        sel = (((r == AUG_LANES * c) & (c < N_HEADS)).astype(F32)
               - ((r == AUG_LANES * c + 3) & (c < N_HEADS)).astype(F32))
        carry = jnp.zeros((1, LANES), F32)
        dbf = jnp.zeros((1, LANES), F32)
        for i in reversed(range(T // tb)):
            colblk = jnp.dot(dcx_ref[i * tb:(i + 1) * tb, :], sel, precision=HIGHEST, preferred_element_type=F32)
            rc = jnp.dot(triu, colblk, precision=HIGHEST, preferred_element_type=F32) + carry
            carry = rc[0:1, :]
            sig = jax.nn.sigmoid(fl_ref[i * tb:(i + 1) * tb, :] + b_ref[...])
            dfl = rc * (1.0 - sig)
            dfl_ref[i * tb:(i + 1) * tb, :] = dfl.astype(BF16)
            dbf = dbf + jnp.sum(dfl, axis=0, keepdims=True)
        dbf_ref[...] = dbf

    return pl.pallas_call(
        body,
        out_shape=(jax.ShapeDtypeStruct((T, LANES), BF16), jax.ShapeDtypeStruct((1, LANES), F32)),
        compiler_params=pltpu.CompilerParams(vmem_limit_bytes=32 * MIB), name="bwd_cum",
    )(dcx, fl, bfp)


def _bwd_in(dz, dq, dk, dv, dfl, dgl, dh1, x2, g1, wz, wf, wg, rows, name, prev=None, stage=0, exchanged=None):
    T = x2.shape[0]
    tm = math.gcd(MATMUL_TILE, rows[0], rows[1] - rows[0])
    first = rows[0] // tm
    nsteps = (rows[1] - rows[0]) // tm
    ex = _GradExchange([tuple(exchanged.shape[1:])]) if stage else None

    def body(dz_ref, dq_ref, dk_ref, dv_ref, dfl_ref, dgl_ref, dh1_ref, x_ref, g_ref, wz_ref, wf_ref, wg_ref, *rest):
        rest = list(rest)
        dx_prev, dg1_prev = (rest.pop(0), rest.pop(0)) if prev else (None, None)
        src_ref = rest.pop(0) if stage else None
        dx_ref, dg1_ref = rest.pop(0), rest.pop(0)
        land_ref = rest.pop(0) if stage else None
        acc_g, ex_sems = rest[0], rest[1:]
        i = pl.program_id(0)

        @pl.when(i == 0)
        def _():
            if stage:
                ex.start(stage, [src_ref], [land_ref], ex_sems)
            acc_g[...] = jnp.zeros_like(acc_g)

        dxn = _dot_nt(dz_ref[...], wz_ref[:, 0:1024])
        dxn = dxn + _dot_nt(dq_ref[...], wz_ref[:, 1024:1536])
        dxn = dxn + _dot_nt(dk_ref[...], wz_ref[:, 1536:2048])
        dxn = dxn + _dot_nt(dv_ref[...], wz_ref[:, 2048:2560])
        dxn = dxn + _dot_nt(dfl_ref[...], wf_ref[...])
        dxn = dxn + _dot_nt(dgl_ref[...], wg_ref[...])
        r1, xh = _rms_stats(x_ref[...])
        acc_g[...] += _fold8(dxn * xh)
        dx_ref[...] = dh1_ref[...] + _rms_bwd(dxn, xh, r1, g_ref[...])

        @pl.when(i == nsteps - 1)
        def _():
            total = jnp.sum(acc_g[...], axis=0, keepdims=True)
            dg1_ref[...] = total + dg1_prev[...] if prev else total
            if stage:
                ex.wait(stage, [src_ref], [land_ref], ex_sems)

    hbm_spec = pl.BlockSpec(memory_space=pl.ANY)
    rows_spec = lambda n: pl.BlockSpec((tm, n), lambda i: (i + first, 0))
    operands = [dz, dq, dk, dv, dfl, dgl, dh1, x2, g1, wz, wf, wg]
    in_specs = [rows_spec(1024), rows_spec(512), rows_spec(512), rows_spec(512), rows_spec(LANES), rows_spec(2048),
                rows_spec(D_MODEL), rows_spec(D_MODEL),
                _const_spec((1, D_MODEL)), _const_spec((D_MODEL, ZQKV_WIDTH)), _const_spec((D_MODEL, LANES)),
                _const_spec((D_MODEL, 2048))]
    aliases = {}
    if prev:
        aliases = {len(operands): 0}
        operands += list(prev)
        in_specs += [hbm_spec, _const_spec((1, D_MODEL))]
    if stage:
        operands.append(exchanged)
        in_specs.append(hbm_spec)
    return pl.pallas_call(
        body, grid=(nsteps,),
        out_shape=[jax.ShapeDtypeStruct((T, D_MODEL), F32), jax.ShapeDtypeStruct((1, D_MODEL), F32)]
        + (ex.land_shapes(stage) if stage else []),
        in_specs=in_specs,
        out_specs=[rows_spec(D_MODEL), pl.BlockSpec((1, D_MODEL), lambda i: (0, 0))] + ([hbm_spec] if stage else []),
        scratch_shapes=[pltpu.VMEM((SUBLANES, D_MODEL), F32)] + (ex.sem_shapes(stage) if stage else []),
        input_output_aliases=aliases,
        compiler_params=_params(48), name=name,
    )(*operands)


def _small_kernel_shapes(g_mix_pre, b_forget, g_sgu, b_sgu, w_spatial, b_spatial, g_mix_post, g_ffn_pre, g_ffn_post):
    return dict(g_mix_pre=g_mix_pre, b_forget=jnp.pad(b_forget, ((0, 0), (0, LANES - N_HEADS))), g_sgu=g_sgu,
                b_sgu=b_sgu, w_spatial=w_spatial[0], b_spatial=b_spatial[0], g_mix_post=g_mix_post,
                g_ffn_pre=g_ffn_pre, g_ffn_post=g_ffn_post)


def _small_output_shapes(d):
    out = dict(d)
    out.update(b_forget=d["b_forget"][:, :N_HEADS], w_spatial=d["w_spatial"][None], b_spatial=d["b_spatial"][None])
    return out


def kernel(x, g_mix_pre, w_in, b_forget, g_sgu, b_sgu, w_spatial, b_spatial, w_branch_sgu, w_branch_attn, w_out, g_mix_post, g_ffn_pre, w_up, w_down, g_ffn_post, loss_target, m_g_mix_pre, m_w_in, m_b_forget, m_g_sgu, m_b_sgu, m_w_spatial, m_b_spatial, m_w_branch_sgu, m_w_branch_attn, m_w_out, m_g_mix_post, m_g_ffn_pre, m_w_up, m_w_down, m_g_ffn_post, v_g_mix_pre, v_w_in, v_b_forget, v_g_sgu, v_b_sgu, v_w_spatial, v_b_spatial, v_w_branch_sgu, v_w_branch_attn, v_w_out, v_g_mix_post, v_g_ffn_pre, v_w_up, v_w_down, v_g_ffn_post):
    T = x.shape[1]
    x2 = x.reshape(T, D_MODEL)
    tgt = loss_target.reshape(T, D_MODEL)

    wg_in = _gather_w_in(w_in[0])
    wz, wf, wgt = _assemble_w_in(wg_in)
    bfp = jnp.pad(b_forget, ((0, 0), (0, LANES - N_HEADS)))
    wsp = w_spatial[0]
    bT = b_spatial[0].T

    xn, zuv, qkv, fl, gt = _fwd_in(x2, g_mix_pre, wz, wf, wgt)
    cc, qaug, kaug = _fwd_cum(fl, bfp)
    ys = _fwd_sgu(zuv, g_sgu, b_sgu, wsp, bT)
    ya, lse, wbs, wba, wo, wup, wdn = _fwd_attn(
        qkv, qaug, kaug, (w_branch_sgu[0], w_branch_attn[0], w_out[0], w_up[0], w_down[0]))
    A, B, mg, o, h1 = _fwd_merge(ys, ya, gt, x2, wbs, wba, wo, g_mix_post)
    xn2, a, ddn, dy, loss_part, dg4 = _fwd_ffn_loss(h1, tgt, wup, wdn, g_ffn_pre, g_ffn_post)

    da, dh1, dg3 = _bwd_ffn(ddn, a, dy, h1, wup, wdn, g_ffn_pre)
    dw_up = _wgrad(xn2, da, "wgrad_up", tn=2048, block_cols=512)
    dw_down = _wgrad(a, ddn, "wgrad_down", relu2=True)
    dgl, dys, dya, qab, dab, dg2, dw_bs, dw_ba, dw_out = _bwd_merge(
        dh1, o, A, B, gt, ys, ya, mg, lse, cc, wbs, wba, wo, g_mix_post)
    col_blocks = lambda g, w: g.reshape(g.shape[0], N_DEV, w).transpose(1, 0, 2)
    row_blocks = lambda g, r: g.reshape(N_DEV, r, g.shape[1])
    early_names = ["w_branch_sgu", "w_branch_attn", "w_out", "w_up", "w_down"]
    early = [col_blocks(dw_bs, 128), col_blocks(dw_ba, 128), row_blocks(dw_out, 128), dw_up, row_blocks(dw_down, 512)]
    owners = _owner_indices()
    dzuv, dwsp, dbT, dgs, dbs, *early_land1 = _bwd_sgu(zuv, dys, g_sgu, b_sgu, wsp, bT, early)
    early_parts = [_chip_partials(g, l1, owners, "chip_partials_" + nm)
                   for g, l1, nm in zip(early, early_land1, early_names)]
    dq, dk, dv, dcx, *early_land2 = _bwd_attn(qkv, dya, qab, dab, kaug, early_parts)
    dfl, dbf = _bwd_cum(dcx, fl, bfp)
    dw_z = _wgrad(xn, dzuv, "wgrad_in_z")
    dw_q, dw_k, dw_v, dw_f = _wgrad_multi(xn, [dq, dk, dv, dfl], "wgrad_in_qkvf")
    dw_g = _wgrad(xn, dgl, "wgrad_in_gate")
    blocks_in = _block_dw_in([dw_z, dw_q, dw_k, dw_v, dw_f, dw_g])
    bwd_in_args = (dzuv, dq, dk, dv, dfl, dgl, dh1, x2, g_mix_pre, wz, wf, wgt)
    dx, dg1, land1_in = _bwd_in(*bwd_in_args, (0, T // 8), "bwd_in_a", stage=1, exchanged=blocks_in)
    part_in = _chip_partials(blocks_in, land1_in, owners, "chip_partials_w_in")
    dx, dg1, land2_in = _bwd_in(*bwd_in_args, (T // 8, 3 * T // 4), "bwd_in_b", prev=(dx, dg1), stage=2,
                                exchanged=part_in)
    dx, dg1 = _bwd_in(*bwd_in_args, (3 * T // 4, T), "bwd_in_c", prev=(dx, dg1))

    tot_a, tot_b = _allreduce_small(dict(
        g_mix_pre=dg1, b_forget=dbf, g_sgu=dgs, b_sgu=dbs, w_spatial=dwsp, b_spatial=dbT, g_mix_post=dg2,
        g_ffn_pre=dg3, g_ffn_post=dg4), loss_part)
    r0, nr, c0, nc = LOSS_SLOT
    loss = jnp.sum(tot_a[r0:r0 + nr, c0:c0 + nc])
    small_w = _small_kernel_shapes(g_mix_pre, b_forget, g_sgu, b_sgu, w_spatial, b_spatial, g_mix_post, g_ffn_pre,
                                   g_ffn_post)
    small_m = _small_kernel_shapes(m_g_mix_pre, m_b_forget, m_g_sgu, m_b_sgu, m_w_spatial, m_b_spatial, m_g_mix_post,
                                   m_g_ffn_pre, m_g_ffn_post)
    small_v = _small_kernel_shapes(v_g_mix_pre, v_b_forget, v_g_sgu, v_b_sgu, v_w_spatial, v_b_spatial, v_g_mix_post,
                                   v_g_ffn_pre, v_g_ffn_post)
    sg, sd, sm, sv = (_small_output_shapes(d) for d in _adamw_small(tot_a, tot_b, small_w, small_m, small_v))

    big = {}
    g_in = _reduced_grad(blocks_in, land1_in, land2_in, owners, "reduced_grad_w_in")[:, :IN_SHARD]
    d_, m_, v_ = _adamw(w_in[0], g_in, m_w_in[0], v_w_in[0], "adamw_w_in")
    big["w_in"] = (g_in[None], d_[None], m_[None], v_[None])
    early_wmv = [(w_branch_sgu, m_w_branch_sgu, v_w_branch_sgu), (w_branch_attn, m_w_branch_attn, v_w_branch_attn),
                 (w_out, m_w_out, v_w_out), (w_up, m_w_up, v_w_up), (w_down, m_w_down, v_w_down)]
    for nm, (w, m, v), g, l1, l2 in zip(early_names, early_wmv, early, early_land1, early_land2):
        big[nm] = tuple(t[None] for t in _adamw_reduced(w[0], m[0], v[0], g, l1, l2, owners, "adamw_" + nm))

    order = ["g_mix_pre", "w_in", "b_forget", "g_sgu", "b_sgu", "w_spatial", "b_spatial", "w_branch_sgu",
             "w_branch_attn", "w_out", "g_mix_post", "g_ffn_pre", "w_up", "w_down", "g_ffn_post"]
    outs = [loss, dx.reshape(1, T, D_MODEL)]
    for kind, small in enumerate((sg, sd, sm, sv)):
        outs += [big[nm][kind] if nm in big else small[nm] for nm in order]
    return tuple(outs)
```

```python
import math

import jax
import jax.numpy as jnp
from jax import lax
from jax.experimental import pallas as pl
from jax.experimental.pallas import tpu as pltpu

F32 = jnp.float32
BF16 = jnp.bfloat16
HIGHEST = lax.Precision.HIGHEST
MESH = pl.DeviceIdType.MESH

D_MODEL = 1024
SGU_WIDTH = 512
ATTN_WIDTH = 512
N_HEADS = 8
CHUNK = 128
D_FF = 4096
IN_WIDTH = 4616
N_DEV = 8
IN_SHARD = IN_WIDTH // N_DEV
IN_SHARD_PAD = 640
ZQKV_WIDTH = 2 * SGU_WIDTH + 3 * ATTN_WIDTH
GATE_OFFSET = ZQKV_WIDTH + N_HEADS
EPS = 1e-6
LANES = 128
SUBLANES = 8
VMEM_BYTES = 64 * 1024 * 1024
MIB = 1024 * 1024

ADAM_LR = 0.001
ADAM_B1 = 0.9
ADAM_B2 = 0.999
ADAM_EPS = 1e-08
ADAM_WD = 0.01
ADAM_STEP = 10

TOKEN_TILE = 512
MATMUL_TILE = 512
MERGE_FWD_TILE = 1024
MERGE_BWD_TILE = 512
ATTN_TILE = 512
CUM_TILE = 256
SGU_TILE = 1024
WGRAD_TILE = 1024
NEG = -1e30

NT_DIMS = (((1,), (1,)), ((), ()))
TN_DIMS = (((0,), (0,)), ((), ()))


def _params(vmem_mb, n_grid=1):
    return pltpu.CompilerParams(
        dimension_semantics=("arbitrary",) * n_grid,
        vmem_limit_bytes=min(vmem_mb * MIB, VMEM_BYTES - 6 * MIB),
    )


def _dot(a, b):
    return jnp.dot(a, b, preferred_element_type=F32)


def _dot_nt(a, b):
    return lax.dot_general(a, b, NT_DIMS, preferred_element_type=F32)


def _dot_tn(a, b):
    return lax.dot_general(a, b, TN_DIMS, preferred_element_type=F32)


def _const_spec(shape):
    nd = len(shape)
    return pl.BlockSpec(shape, lambda *_: (0,) * nd, pipeline_mode=pl.Buffered(1))


def _row_spec(tm, n, col=0):
    return pl.BlockSpec((tm, n), lambda i: (i, col))


def _fold8(v):
    return v.reshape(v.shape[0] // SUBLANES, SUBLANES, v.shape[1]).sum(axis=0)


def _pick(v, lane_iota, k):
    return jnp.sum(jnp.where(lane_iota == k, v, 0.0), axis=1, keepdims=True)


def _iota(shape, dim):
    return lax.broadcasted_iota(jnp.int32, shape, dim)


def _gelu(x):
    c = 0.7978845608028654
    return 0.5 * x * (1.0 + jnp.tanh(c * (x + 0.044715 * x * x * x)))


def _gelu_grad(x):
    c = 0.7978845608028654
    t = jnp.tanh(c * (x + 0.044715 * x * x * x))
    return 0.5 * (1.0 + t) + 0.5 * x * (1.0 - t * t) * (c * (1.0 + 3.0 * 0.044715 * x * x))


def _rms_stats(v):
    r = lax.rsqrt(jnp.mean(v * v, axis=1, keepdims=True) + EPS)
    return r, v * r


def _rms_bwd(dout, vhat, r, g):
    a = dout * g
    return r * (a - vhat * jnp.mean(a * vhat, axis=1, keepdims=True))


def _mesh_pos():
    return lax.axis_index("x"), lax.axis_index("y"), lax.axis_index("c")


def _dev_index(px, py, pc):
    return 4 * px + 2 * py + pc


def _other_chips(x, y):
    return [(1 - x, y), (x, 1 - y), (1 - x, 1 - y)]


class _WeightGather:
    def __init__(self, shard_shapes, kinds, stage_shapes=None):
        self.shard_shapes = list(shard_shapes)
        self.kinds = list(kinds)
        self.stage_shapes = list(stage_shapes or shard_shapes)
        self.n = len(self.kinds)

    def out_shapes(self):
        shapes = []
        for (rows, cols), kind in zip(self.stage_shapes, self.kinds):
            full = {"block": (N_DEV, rows, cols), "rows": (N_DEV * rows, cols), "cols": (rows, N_DEV * cols)}[kind]
            shapes.append(jax.ShapeDtypeStruct(full, BF16))
        return shapes

    def scratch_shapes(self):
        return ([pltpu.VMEM(s, BF16) for s in self.stage_shapes]
                + [pltpu.SemaphoreType.DMA((self.n, 7)), pltpu.SemaphoreType.DMA((self.n, 7)),
                   pltpu.SemaphoreType.DMA((self.n,))])

    def _view(self, a, ref, j):
        rows, cols = self.stage_shapes[a]
        if self.kinds[a] == "block":
            return ref.at[j]
        if self.kinds[a] == "rows":
            return ref.at[pl.ds(pl.multiple_of(j * rows, rows), rows), :]
        return ref.at[:, pl.ds(pl.multiple_of(j * cols, cols), cols)]

    def _copy(self, outs, scratch, a, k, block, to, from_stage=False):
        stage, (send_sems, recv_sems, _) = scratch[:self.n], scratch[self.n:]
        dst = self._view(a, outs[a], _dev_index(*block))
        return pltpu.make_async_remote_copy(
            src_ref=stage[a] if from_stage else dst, dst_ref=dst,
            send_sem=send_sems.at[a, k], recv_sem=recv_sems.at[a, k],
            device_id=to, device_id_type=MESH)

    def _local(self, outs, scratch, a, me):
        return pltpu.make_async_copy(scratch[a], self._view(a, outs[a], _dev_index(*me)), scratch[-1].at[a])

    def start(self, ins, outs, scratch):
        x, y, c = _mesh_pos()
        me, sibling = (x, y, c), (x, y, 1 - c)
        for a in range(self.n):
            rows, cols = self.shard_shapes[a]
            if self.stage_shapes[a] != self.shard_shapes[a]:
                scratch[a][...] = jnp.zeros(self.stage_shapes[a], BF16)
            scratch[a][0:rows, 0:cols] = ins[a][...].astype(BF16)
            self._local(outs, scratch, a, me).start()
        for a in range(self.n):
            self._copy(outs, scratch, a, 0, me, sibling, True).start()
            for j, chip in enumerate(_other_chips(x, y)):
                self._copy(outs, scratch, a, 1 + j, me, (*chip, c), True).start()

    def forward(self, outs, scratch):
        x, y, c = _mesh_pos()
        me, sibling = (x, y, c), (x, y, 1 - c)
        for a in range(self.n):
            for j, chip in enumerate(_other_chips(x, y)):
                self._copy(outs, scratch, a, 1 + j, (*chip, c), me).wait_recv()
                self._copy(outs, scratch, a, 4 + j, (*chip, c), sibling).start()

    def finish(self, outs, scratch):
        x, y, c = _mesh_pos()
        me, sibling = (x, y, c), (x, y, 1 - c)
        chips = _other_chips(x, y)
        for a in range(self.n):
            self._copy(outs, scratch, a, 0, sibling, me).wait_recv()
            for j, chip in enumerate(chips):
                self._copy(outs, scratch, a, 4 + j, (*chip, 1 - c), me).wait_recv()
        for a in range(self.n):
            self._copy(outs, scratch, a, 0, me, sibling, True).wait_send()
            for j, chip in enumerate(chips):
                self._copy(outs, scratch, a, 1 + j, me, (*chip, c), True).wait_send()
                self._copy(outs, scratch, a, 4 + j, (*chip, c), sibling).wait_send()
            self._local(outs, scratch, a, me).wait()


def _gather_w_in(w_in_local):
    g = _WeightGather([(D_MODEL, IN_SHARD)], ["block"], [(D_MODEL, IN_SHARD_PAD)])

    def body(w_ref, out_ref, *scratch):
        g.start([w_ref], [out_ref], scratch)
        g.forward([out_ref], scratch)
        g.finish([out_ref], scratch)

    return pl.pallas_call(
        body,
        out_shape=g.out_shapes()[0],
        in_specs=[pl.BlockSpec(memory_space=pltpu.VMEM)],
        out_specs=pl.BlockSpec(memory_space=pl.ANY),
        scratch_shapes=g.scratch_shapes(),
        compiler_params=pltpu.CompilerParams(vmem_limit_bytes=32 * MIB),
        name="gather_w_in",
    )(w_in_local)


class _GradExchange:
    def __init__(self, shapes):
        self.shapes = [tuple(s) for s in shapes]
        self.n = len(self.shapes)

    def land_shapes(self, stage):
        slots, dtype = (4, F32) if stage == 1 else (3, BF16)
        return [jax.ShapeDtypeStruct((slots,) + s, dtype) for s in self.shapes]

    def sem_shapes(self, stage):
        slots = 4 if stage == 1 else 3
        return [pltpu.SemaphoreType.DMA((self.n, slots)), pltpu.SemaphoreType.DMA((self.n, slots))]

    def _copy(self, stage, srcs, lands, sems, a, k):
        x, y, c = _mesh_pos()
        cx, cy = (_other_chips(x, y) + [(x, y)])[k]
        if stage == 1:
            src, to = srcs[a].at[_dev_index(cx, cy, 1 - c)], (x, y, 1 - c)
        else:
            src, to = srcs[a].at[k], (cx, cy, c)
        return pltpu.make_async_remote_copy(
            src_ref=src, dst_ref=lands[a].at[k], send_sem=sems[0].at[a, k], recv_sem=sems[1].at[a, k],
            device_id=to, device_id_type=MESH)

    def start(self, stage, srcs, lands, sems):
        for a in range(self.n):
            for k in range(4 if stage == 1 else 3):
                self._copy(stage, srcs, lands, sems, a, k).start()

    def wait(self, stage, srcs, lands, sems):
        for a in range(self.n):
            for k in range(4 if stage == 1 else 3):
                cp = self._copy(stage, srcs, lands, sems, a, k)
                cp.wait_recv()
                cp.wait_send()


def _owner_indices():
    x, y, c = _mesh_pos()
    return jnp.stack([_dev_index(cx, cy, c) for cx, cy in _other_chips(x, y) + [(x, y)]]).astype(jnp.int32)


def _chip_partials(g, land1, idx, name):
    _, rows, cols = g.shape
    tr = min(rows, 256)

    def body(idx_ref, g_ref, l_ref, o_ref):
        o_ref[...] = (g_ref[...] + l_ref[...]).astype(BF16)

    return pl.pallas_call(
        body,
        grid_spec=pltpu.PrefetchScalarGridSpec(
            num_scalar_prefetch=1, grid=(3, rows // tr),
            in_specs=[pl.BlockSpec((None, tr, cols), lambda k, r, idx: (idx[k], r, 0)),
                      pl.BlockSpec((None, tr, cols), lambda k, r, idx: (k, r, 0))],
            out_specs=pl.BlockSpec((None, tr, cols), lambda k, r, idx: (k, r, 0))),
        out_shape=jax.ShapeDtypeStruct((3, rows, cols), BF16),
        compiler_params=_params(32, 2), name=name,
    )(idx, g, land1)


def _reduced_block(g_ref, l1_ref, a_ref, b_ref, c_ref):
    return ((g_ref[...] + l1_ref[...]) + a_ref[...].astype(F32)) + b_ref[...].astype(F32) + c_ref[...].astype(F32)


def _reduced_specs(tm, cols):
    return [pl.BlockSpec((None, tm, cols), lambda i, idx: (idx[3], i, 0)),
            pl.BlockSpec((None, tm, cols), lambda i, idx: (3, i, 0)),
            pl.BlockSpec((None, tm, cols), lambda i, idx: (0, i, 0)),
            pl.BlockSpec((None, tm, cols), lambda i, idx: (1, i, 0)),
            pl.BlockSpec((None, tm, cols), lambda i, idx: (2, i, 0))]


def _reduced_grad(g, land1, land2, idx, name):
    _, rows, cols = g.shape
    tm = min(rows, 256)

    def body(idx_ref, g_ref, l1_ref, a_ref, b_ref, c_ref, o_ref):
        o_ref[...] = _reduced_block(g_ref, l1_ref, a_ref, b_ref, c_ref)

    return pl.pallas_call(
        body,
        grid_spec=pltpu.PrefetchScalarGridSpec(
            num_scalar_prefetch=1, grid=(rows // tm,), in_specs=_reduced_specs(tm, cols),
            out_specs=pl.BlockSpec((tm, cols), lambda i, idx: (i, 0))),
        out_shape=jax.ShapeDtypeStruct((rows, cols), F32),
        compiler_params=_params(32), name=name,
    )(idx, g, land1, land2, land2, land2)


def _adamw_math(w, g, m, v):
    m = ADAM_B1 * m + (1.0 - ADAM_B1) * g
    v = ADAM_B2 * v + (1.0 - ADAM_B2) * (g * g)
    m_hat = m / (1.0 - ADAM_B1 ** ADAM_STEP)
    v_hat = v / (1.0 - ADAM_B2 ** ADAM_STEP)
    delta = -ADAM_LR * (m_hat / (jnp.sqrt(v_hat) + ADAM_EPS) + ADAM_WD * w)
    return delta, m, v


SMALL_NAMES = ("g_mix_pre", "b_forget", "g_sgu", "b_sgu", "w_spatial", "b_spatial", "g_mix_post", "g_ffn_pre",
               "g_ffn_post")
SMALL_SLOTS = {"g_mix_pre": (0, 1, 0, 1024), "g_mix_post": (1, 1, 0, 1024), "g_ffn_pre": (2, 1, 0, 1024),
               "g_ffn_post": (3, 1, 0, 1024), "g_sgu": (4, 1, 0, 512), "b_sgu": (4, 1, 512, 512),
               "b_forget": (5, 1, 0, 128), "b_spatial": (8, 8, 0, 128)}
SMALL_TILE = (16, 1024)
SPATIAL_TILE = (N_HEADS * CHUNK, CHUNK)


LOSS_SLOT = (8, 8, 128, 128)


def _allreduce_small(grads, loss_part):
    names = list(SMALL_NAMES)

    def body(*refs):
        g = dict(zip(names, refs[:len(names)]))
        loss_ref = refs[len(names)]
        tot_a, tot_b, buf_a, buf_b, sib_a, sib_b, ps_a, ps_b, land_a, land_b, send_sems, recv_sems = refs[len(names) + 1:]
        x, y, c = _mesh_pos()
        buf_a[...] = jnp.zeros(SMALL_TILE, F32)
        r0, nr, c0, nc = LOSS_SLOT
        buf_a[r0:r0 + nr, c0:c0 + nc] = loss_ref[...]
        for name, (r0, nr, c0, nc) in SMALL_SLOTS.items():
            val = g[name][...]
            if name == "b_spatial":
                val = jnp.transpose(val)[0:N_HEADS, :]
            buf_a[r0:r0 + nr, c0:c0 + nc] = val
        buf_b[...] = g["w_spatial"][...].reshape(SPATIAL_TILE)

        def swap(k, src, dst, to):
            return pltpu.make_async_remote_copy(src_ref=src, dst_ref=dst, send_sem=send_sems.at[k],
                                                recv_sem=recv_sems.at[k], device_id=to, device_id_type=MESH)

        first = [swap(0, buf_a, sib_a, (x, y, 1 - c)), swap(1, buf_b, sib_b, (x, y, 1 - c))]
        for cp in first:
            cp.start()
        for cp in first:
            cp.wait_recv()
        ps_a[...] = buf_a[...] + sib_a[...]
        ps_b[...] = buf_b[...] + sib_b[...]
        second = []
        for k, (cx, cy) in enumerate(_other_chips(x, y)):
            second += [swap(2 + 2 * k, ps_a, land_a.at[k], (cx, cy, c)), swap(3 + 2 * k, ps_b, land_b.at[k], (cx, cy, c))]
        for cp in second:
            cp.start()
        for cp in second:
            cp.wait_recv()
        tot_a[...] = (ps_a[...] + land_a[0]) + (land_a[1] + land_a[2])
        tot_b[...] = (ps_b[...] + land_b[0]) + (land_b[1] + land_b[2])
        for cp in first + second:
            cp.wait_send()

    vm = pl.BlockSpec(memory_space=pltpu.VMEM)
    return pl.pallas_call(
        body,
        out_shape=(jax.ShapeDtypeStruct(SMALL_TILE, F32), jax.ShapeDtypeStruct(SPATIAL_TILE, F32)),
        in_specs=[vm] * (len(names) + 1), out_specs=[vm, vm],
        scratch_shapes=[pltpu.VMEM(SMALL_TILE, F32), pltpu.VMEM(SPATIAL_TILE, F32)] * 3
        + [pltpu.VMEM((3,) + SMALL_TILE, F32), pltpu.VMEM((3,) + SPATIAL_TILE, F32),
           pltpu.SemaphoreType.DMA((8,)), pltpu.SemaphoreType.DMA((8,))],
        compiler_params=pltpu.CompilerParams(vmem_limit_bytes=32 * MIB),
        name="allreduce_small",
    )(*[grads[nm] for nm in names], loss_part)


def _adamw_small(tot_a, tot_b, ws, ms, vs):
    names = list(SMALL_NAMES)
    n = len(names)

    def body(a_ref, b_ref, *refs):
        w, m, v = (dict(zip(names, refs[i * n:(i + 1) * n])) for i in range(3))
        outs = [dict(zip(names, refs[(3 + i) * n:(4 + i) * n])) for i in range(4)]
        for name in names:
            if name == "w_spatial":
                g = b_ref[...].reshape(N_HEADS, CHUNK, CHUNK)
            else:
                r0, nr, c0, nc = SMALL_SLOTS[name]
                g = a_ref[r0:r0 + nr, c0:c0 + nc]
            vals = (g,) + _adamw_math(w[name][...], g, m[name][...], v[name][...])
            for out, val in zip(outs, vals):
                out[name][...] = val

    shapes = [jax.ShapeDtypeStruct(ws[nm].shape, F32) for nm in names]
    vm = pl.BlockSpec(memory_space=pltpu.VMEM)
    res = pl.pallas_call(
        body, out_shape=shapes * 4, in_specs=[vm] * (2 + 3 * n), out_specs=[vm] * (4 * n),
        compiler_params=pltpu.CompilerParams(vmem_limit_bytes=32 * MIB), name="adamw_small",
    )(tot_a, tot_b, *[d[nm] for d in (ws, ms, vs) for nm in names])
    return [dict(zip(names, res[i * n:(i + 1) * n])) for i in range(4)]


def _adamw_reduced(w, m, v, g, land1, land2, idx, name):
    rows, cols = w.shape
    tm = min(rows, 256)

    def body(idx_ref, w_ref, m_ref, v_ref, g_ref, l1_ref, a_ref, b_ref, c_ref, go_ref, d_ref, nm_ref, nv_ref):
        gsum = _reduced_block(g_ref, l1_ref, a_ref, b_ref, c_ref)
        go_ref[...] = gsum
        delta, nm, nv = _adamw_math(w_ref[...], gsum, m_ref[...], v_ref[...])
        d_ref[...] = delta
        nm_ref[...] = nm
        nv_ref[...] = nv

    sd = jax.ShapeDtypeStruct((rows, cols), F32)
    spec = pl.BlockSpec((tm, cols), lambda i, idx: (i, 0))
    return pl.pallas_call(
        body,
        grid_spec=pltpu.PrefetchScalarGridSpec(
            num_scalar_prefetch=1, grid=(rows // tm,), in_specs=[spec] * 3 + _reduced_specs(tm, cols),
            out_specs=[spec] * 4),
        out_shape=(sd, sd, sd, sd),
        compiler_params=_params(32), name=name,
    )(idx, w, m, v, g, land1, land2, land2, land2)


def _adamw(w, g, m, v, name):
    rows, cols = w.shape
    tm = 256 if rows % 256 == 0 else rows

    def body(w_ref, g_ref, m_ref, v_ref, d_ref, nm_ref, nv_ref):
        delta, nm, nv = _adamw_math(w_ref[...], g_ref[...], m_ref[...], v_ref[...])
        d_ref[...] = delta
        nm_ref[...] = nm
        nv_ref[...] = nv

    sd = jax.ShapeDtypeStruct((rows, cols), F32)
    spec = _row_spec(tm, cols)
    return pl.pallas_call(
        body, grid=(rows // tm,), out_shape=(sd, sd, sd), in_specs=[spec] * 4, out_specs=[spec] * 3,
        compiler_params=_params(32), name=name,
    )(w, g, m, v)


def _virtual_slab(sources, v0, v_end, like):
    lane = _iota(like.shape, 1)
    out = jnp.zeros(like.shape, like.dtype)
    for v_start, v_stop, read in sources:
        a, b = max(v0, v_start), min(v0 + LANES, v_stop, v_end)
        while a < b:
            c = a - v_start
            n = min(b - a, LANES - c % LANES)
            piece = read(c // LANES)
            shift = (a - v0 - c % LANES) % LANES
            if shift:
                piece = pltpu.roll(piece, shift, 1)
            out = jnp.where((lane >= a - v0) & (lane < a - v0 + n), piece, out)
            a += n
    return out


def _assemble_w_in(wg_in):
    tm = TOKEN_TILE

    def body(src_ref, wz_ref, wf_ref, wg_ref):
        like = src_ref[0, :, 0:LANES]
        sources = [(IN_SHARD * j, IN_SHARD * (j + 1),
                    (lambda k, j=j: src_ref[j, :, LANES * k:LANES * (k + 1)])) for j in range(N_DEV)]
        for k in range(ZQKV_WIDTH // LANES):
            wz_ref[:, LANES * k:LANES * (k + 1)] = _virtual_slab(sources, LANES * k, ZQKV_WIDTH, like)
        wf_ref[...] = _virtual_slab(sources, ZQKV_WIDTH, GATE_OFFSET, like)
        for k in range(2 * D_MODEL // LANES):
            wg_ref[:, LANES * k:LANES * (k + 1)] = _virtual_slab(sources, GATE_OFFSET + LANES * k, IN_WIDTH, like)

    return pl.pallas_call(
        body, grid=(D_MODEL // tm,),
        out_shape=(jax.ShapeDtypeStruct((D_MODEL, ZQKV_WIDTH), BF16), jax.ShapeDtypeStruct((D_MODEL, LANES), BF16),
                   jax.ShapeDtypeStruct((D_MODEL, 2 * D_MODEL), BF16)),
        in_specs=[pl.BlockSpec((N_DEV, tm, IN_SHARD_PAD), lambda i: (0, i, 0))],
        out_specs=[_row_spec(tm, ZQKV_WIDTH), _row_spec(tm, LANES), _row_spec(tm, 2 * D_MODEL)],
        compiler_params=_params(32), name="assemble_w_in",
    )(wg_in)


def _block_dw_in(pieces):
    tm = TOKEN_TILE
    widths = [2 * SGU_WIDTH, ATTN_WIDTH, ATTN_WIDTH, ATTN_WIDTH, N_HEADS, 2 * D_MODEL]
    starts = [sum(widths[:k]) for k in range(len(widths))]

    def body(*refs):
        in_refs, out_ref = refs[:-1], refs[-1]
        like = in_refs[0][:, 0:LANES]
        slab = lambda ref: (lambda k: ref[:, LANES * k:LANES * (k + 1)])
        sources = [(s, s + w, slab(ref)) for s, w, ref in zip(starts, widths, in_refs)]
        for j in range(N_DEV):
            for k in range(IN_SHARD_PAD // LANES):
                out_ref[j, :, LANES * k:LANES * (k + 1)] = _virtual_slab(
                    sources, IN_SHARD * j + LANES * k, IN_SHARD * (j + 1), like)

    return pl.pallas_call(
        body, grid=(D_MODEL // tm,),
        out_shape=jax.ShapeDtypeStruct((N_DEV, D_MODEL, IN_SHARD_PAD), F32),
        in_specs=[_row_spec(tm, pc.shape[1]) for pc in pieces],
        out_specs=pl.BlockSpec((N_DEV, tm, IN_SHARD_PAD), lambda i: (0, i, 0)),
        compiler_params=_params(32), name="block_dw_in",
    )(*pieces)


def _fwd_in(x2, g1, wz, wf, wg):
    T = x2.shape[0]
    tm = MATMUL_TILE

    def body(x_ref, g_ref, wz_ref, wf_ref, wg_ref, xn_ref, zuv_ref, qkv_ref, fl_ref, gt_ref):
        x = x_ref[...]
        r, xh = _rms_stats(x)
        xn = (xh * g_ref[...]).astype(BF16)
        xn_ref[...] = xn
        zuv_ref[...] = _dot(xn, wz_ref[:, 0:1024]).astype(BF16)
        qkv_ref[:, 0:512] = (_dot(xn, wz_ref[:, 1024:1536]) * 0.125).astype(BF16)
        qkv_ref[:, 512:1536] = _dot(xn, wz_ref[:, 1536:2560]).astype(BF16)
        fl_ref[...] = _dot(xn, wf_ref[...])
        gt_ref[...] = jax.nn.sigmoid(_dot(xn, wg_ref[...])).astype(BF16)

    return pl.pallas_call(
        body, grid=(T // tm,),
        out_shape=(jax.ShapeDtypeStruct((T, D_MODEL), BF16), jax.ShapeDtypeStruct((T, 1024), BF16),
                   jax.ShapeDtypeStruct((T, 1536), BF16), jax.ShapeDtypeStruct((T, LANES), F32),
                   jax.ShapeDtypeStruct((T, 2048), BF16)),
        in_specs=[_row_spec(tm, D_MODEL), _const_spec((1, D_MODEL)), _const_spec((D_MODEL, ZQKV_WIDTH)),
                  _const_spec((D_MODEL, LANES)), _const_spec((D_MODEL, 2048))],
        out_specs=[_row_spec(tm, D_MODEL), _row_spec(tm, 1024), _row_spec(tm, 1536), _row_spec(tm, LANES),
                   _row_spec(tm, 2048)],
        compiler_params=_params(48), name="fwd_in",
    )(x2, g1, wz, wf, wg)


def _log_sigmoid(f):
    return jnp.minimum(f, 0.0) - jnp.log1p(jnp.exp(-jnp.abs(f)))


AUG_LANES = 6


def _split3(v):
    hi = v.astype(BF16)
    r1 = v - hi.astype(F32)
    mid = r1.astype(BF16)
    lo = (r1 - mid.astype(F32)).astype(BF16)
    return hi, mid, lo


def _spread(parts, k0):
    n = len(parts)
    r, c = _iota((n * LANES, LANES), 0), _iota((n * LANES, LANES), 1)
    e = jnp.zeros((n * LANES, LANES), BF16)
    for i in range(n):
        h = r - i * LANES
        e = jnp.where((h >= 0) & (h < N_HEADS) & (c == AUG_LANES * h + (k0 + i)), jnp.ones_like(e), e)
    return _dot(jnp.concatenate(parts, axis=1), e)


def _aug_ones(shape, k0):
    lane = _iota(shape, 1)
    head = (lane * 43) >> 8
    slot = lane - AUG_LANES * head
    return ((lane < AUG_LANES * N_HEADS) & (slot >= k0) & (slot < k0 + 3)).astype(F32)


def _aug_query(v):
    return (_spread(_split3(v), 0) + _aug_ones(v.shape, 3)).astype(BF16)


def _aug_key(v):
    return (_aug_ones(v.shape, 0) - _spread(_split3(v), 3)).astype(BF16)


def _aug_stack(t2, aug, p):
    lane = _iota(t2.shape, 1)
    low = lane < 64
    zero = jnp.zeros_like(t2)
    first = 2 * AUG_LANES * p
    a_e = jnp.where((lane >= first) & (lane < first + AUG_LANES), aug, zero)
    a_o = jnp.where((lane >= first + AUG_LANES) & (lane < first + 2 * AUG_LANES), aug, zero)
    top = jnp.concatenate([jnp.where(low, t2, zero), a_e], axis=1)
    bot = jnp.concatenate([jnp.where(low, zero, t2), a_o], axis=1)
    return jnp.concatenate([top, bot], axis=0)


def _fwd_cum(fl, bfp):
    T = fl.shape[0]
    tb = CUM_TILE

    def body(fl_ref, b_ref, cc_ref, qa_ref, ka_ref):
        tri = (_iota((tb, tb), 0) >= _iota((tb, tb), 1)).astype(F32)
        carry = jnp.zeros((1, LANES), F32)
        for i in range(T // tb):
            rows = slice(i * tb, (i + 1) * tb)
            lf = _log_sigmoid(fl_ref[rows, :] + b_ref[...])
            cs = jnp.dot(tri, lf, precision=HIGHEST, preferred_element_type=F32) + carry
            cc_ref[rows, :] = cs
            carry = cs[tb - 1:tb, :]
            qa_ref[rows, :] = _aug_query(cs)
            ka_ref[rows, :] = _aug_key(cs)

    return pl.pallas_call(
        body,
        out_shape=(jax.ShapeDtypeStruct((T, LANES), F32), jax.ShapeDtypeStruct((T, LANES), BF16),
                   jax.ShapeDtypeStruct((T, LANES), BF16)),
        compiler_params=pltpu.CompilerParams(vmem_limit_bytes=32 * MIB), name="fwd_cum",
    )(fl, bfp)


def _sgu_forward_parts(z, gs, bs):
    u = _gelu(z[:, :SGU_WIDTH])
    vv = _gelu(z[:, SGU_WIDTH:])
    vc = vv - jnp.mean(vv, axis=1, keepdims=True)
    rs = lax.rsqrt(jnp.mean(vc * vc, axis=1, keepdims=True) + EPS)
    vhat = vc * rs
    return u, vhat, rs, vhat * gs + bs


def _sgu_pair_weights(w_ref, bT, p):
    tril = _iota((CHUNK, CHUNK), 0) >= _iota((CHUNK, CHUNK), 1)
    we = jnp.where(tril, w_ref[2 * p], 0.0).astype(BF16)
    wo = jnp.where(tril, w_ref[2 * p + 1], 0.0).astype(BF16)
    lane8 = _iota(bT.shape, 1)
    low = _iota((CHUNK, LANES), 1) < 64
    b2 = jnp.where(low, _pick(bT, lane8, 2 * p), _pick(bT, lane8, 2 * p + 1))
    return we, wo, b2


def _chunks_on_lanes(v, p, nc):
    return jnp.concatenate([v[c * CHUNK:(c + 1) * CHUNK, LANES * p:LANES * (p + 1)] for c in range(nc)], axis=1)


def _sgu_mix(we, wo, b2, vcat, nc):
    low = (_iota((CHUNK, nc * LANES), 1) % LANES) < 64
    return jnp.where(low, _dot(we, vcat), _dot(wo, vcat)) + jnp.concatenate([b2] * nc, axis=1)


def _fwd_sgu(zuv, gs, bs, wsp, bT):
    T = zuv.shape[0]
    tc = SGU_TILE
    nc = tc // CHUNK

    def body(z_ref, gs_ref, bs_ref, w_ref, bT_ref, y_ref):
        u, _, _, vln = _sgu_forward_parts(z_ref[...].astype(F32), gs_ref[...], bs_ref[...])
        vb = vln.astype(BF16)
        for p in range(4):
            we, wo, b2 = _sgu_pair_weights(w_ref, bT_ref[...], p)
            s = _sgu_mix(we, wo, b2, _chunks_on_lanes(vb, p, nc), nc)
            for c in range(nc):
                rows, cols = slice(c * CHUNK, (c + 1) * CHUNK), slice(LANES * p, LANES * (p + 1))
                y_ref[rows, cols] = (u[rows, cols] * s[:, c * LANES:(c + 1) * LANES]).astype(BF16)

    return pl.pallas_call(
        body, grid=(T // tc,), out_shape=jax.ShapeDtypeStruct((T, SGU_WIDTH), BF16),
        in_specs=[_row_spec(tc, 1024), _const_spec((1, SGU_WIDTH)), _const_spec((1, SGU_WIDTH)),
                  _const_spec((8, CHUNK, CHUNK)), _const_spec((CHUNK, 8))],
        out_specs=_row_spec(tc, SGU_WIDTH),
        compiler_params=_params(40), name="fwd_sgu",
    )(zuv, gs, bs, wsp, bT)


def _fwd_attn(qkv, qaug, kaug, w_shards):
    T = qkv.shape[0]
    tq = tk = ATTN_TILE
    nq = T // tq
    gather = _WeightGather([w.shape for w in w_shards], ["cols", "cols", "rows", "cols", "rows"])
    nw = gather.n

    def body(q_ref, qa_ref, k_ref, v_ref, ka_ref, *rest):
        w_refs, (o_ref, lse_ref), wg_refs, scratch = rest[:nw], rest[nw:nw + 2], rest[nw + 2:2 * nw + 2], rest[2 * nw + 2:]
        i = pl.program_id(0)

        @pl.when(i == 0)
        def _():
            gather.start(w_refs, wg_refs, scratch)

        @pl.when(i == nq // 2)
        def _():
            gather.forward(wg_refs, scratch)

        lane = _iota((tq, LANES), 1)
        low = lane < 64
        lowk = _iota((tk, LANES), 1) < 64
        one = jnp.ones((tk, LANES), BF16)
        row = _iota((2 * tq, tk), 0) % tq
        col = _iota((2 * tq, tk), 1)
        cols = [slice(LANES * p, LANES * (p + 1)) for p in range(4)]
        qa = qa_ref[...]
        qs = [_aug_stack(q_ref[:, cols[p]], qa, p) for p in range(4)]

        def step(j, carry, masked):
            ks = pl.ds(pl.multiple_of(j * tk, tk), tk)
            ka = ka_ref[ks, :]
            new = []
            for p in range(4):
                m, acc_e, acc_o = carry[p]
                v2 = v_ref[ks, cols[p]]
                s = _dot_nt(qs[p], jnp.concatenate([k_ref[ks, cols[p]], ka], axis=1))
                if masked:
                    s = jnp.where(col <= row, s, NEG)
                mn = jnp.maximum(m, jnp.max(s, axis=1, keepdims=True))
                al = jnp.exp(m - mn)
                pm = jnp.exp(s - mn).astype(BF16)
                acc_e = al[:tq] * acc_e + _dot(pm[:tq], jnp.where(lowk, v2, one))
                acc_o = al[tq:] * acc_o + _dot(pm[tq:], jnp.where(lowk, one, v2))
                new.append((mn, acc_e, acc_o))
            return tuple(new)

        init = tuple((jnp.full((2 * tq, 1), NEG, F32), jnp.zeros((tq, LANES), F32), jnp.zeros((tq, LANES), F32))
                     for _ in range(4))
        def trip(t, c):
            for u in range(3):
                c = step(3 * t + u, c, False)
            return c

        carry = lax.fori_loop(0, i // 3, trip, init)
        carry = lax.fori_loop(3 * (i // 3), i, lambda j, c: step(j, c, False), carry)
        carry = step(i, carry, True)
        lse_blk = jnp.zeros((tq, LANES), F32)
        for p in range(4):
            m, acc_e, acc_o = carry[p]
            l_e = pltpu.roll(acc_e, 64, 1)
            l_o = pltpu.roll(acc_o, 64, 1)
            o_ref[:, cols[p]] = jnp.where(low, acc_e / l_e, acc_o / l_o).astype(BF16)
            lse_blk = jnp.where(lane == 2 * p, m[:tq] + jnp.log(l_e), lse_blk)
            lse_blk = jnp.where(lane == 2 * p + 1, m[tq:] + jnp.log(acc_o), lse_blk)
        lse_ref[...] = lse_blk

        @pl.when(i == nq - 1)
        def _():
            gather.finish(wg_refs, scratch)

    return pl.pallas_call(
        body, grid=(nq,),
        out_shape=[jax.ShapeDtypeStruct((T, ATTN_WIDTH), BF16), jax.ShapeDtypeStruct((T, LANES), F32)]
        + gather.out_shapes(),
        in_specs=[_row_spec(tq, 512), _row_spec(tq, LANES),
                  pl.BlockSpec((T, 512), lambda i: (0, 1), pipeline_mode=pl.Buffered(1)),
                  pl.BlockSpec((T, 512), lambda i: (0, 2), pipeline_mode=pl.Buffered(1)),
                  _const_spec((T, LANES))] + [_const_spec(w.shape) for w in w_shards],
        out_specs=[_row_spec(tq, ATTN_WIDTH), _row_spec(tq, LANES)] + [pl.BlockSpec(memory_space=pl.ANY)] * nw,
        scratch_shapes=gather.scratch_shapes(),
        compiler_params=_params(58), name="fwd_attn",
    )(qkv, qaug, qkv, qkv, kaug, *w_shards)


def _fwd_merge(ys, ya, gt, x2, wbs, wba, wo, g2):
    T = x2.shape[0]
    tm = min(T, MERGE_FWD_TILE)

    def body(ys_ref, ya_ref, gt_ref, x_ref, wbs_ref, wba_ref, wo_ref, g2_ref, a_ref, b_ref, mg_ref, o_ref, h1_ref):
        A = _dot(ys_ref[...], wbs_ref[...])
        B = _dot(ya_ref[...], wba_ref[...])
        mg = (gt_ref[:, :D_MODEL].astype(F32) * A + gt_ref[:, D_MODEL:].astype(F32) * B).astype(BF16)
        o = _dot(mg, wo_ref[...])
        r2, oh = _rms_stats(o)
        a_ref[...] = A.astype(BF16)
        b_ref[...] = B.astype(BF16)
        mg_ref[...] = mg
        o_ref[...] = o.astype(BF16)
        h1_ref[...] = x_ref[...] + oh * g2_ref[...]

    sd = jax.ShapeDtypeStruct((T, D_MODEL), BF16)
    return pl.pallas_call(
        body, grid=(T // tm,),
        out_shape=(sd, sd, sd, sd, jax.ShapeDtypeStruct((T, D_MODEL), F32)),
        in_specs=[_row_spec(tm, 512), _row_spec(tm, 512), _row_spec(tm, 2048), _row_spec(tm, D_MODEL),
                  _const_spec((512, D_MODEL)), _const_spec((512, D_MODEL)), _const_spec((D_MODEL, D_MODEL)),
                  _const_spec((1, D_MODEL))],
        out_specs=[_row_spec(tm, D_MODEL)] * 5,
        compiler_params=_params(58), name="fwd_merge",
    )(ys, ya, gt, x2, wbs, wba, wo, g2)


def _fwd_ffn_loss(h1, tgt, wup, wdn, g3, g4):
    T = h1.shape[0]
    tm = MATMUL_TILE
    nsteps = T // tm

    def body(h1_ref, tg_ref, wup_ref, wdn_ref, g3_ref, g4_ref, xn2_ref, a_ref, ddn_ref, dy_ref, loss_ref,
             dg4_ref, acc_l, acc_g):
        i = pl.program_id(0)

        @pl.when(i == 0)
        def _():
            acc_l[...] = jnp.zeros_like(acc_l)
            acc_g[...] = jnp.zeros_like(acc_g)

        h1v = h1_ref[...]
        r3, h1h = _rms_stats(h1v)
        xn2 = (h1h * g3_ref[...]).astype(BF16)
        xn2_ref[...] = xn2
        dn = jnp.zeros((tm, D_MODEL), F32)
        for j in range(D_FF // 1024):
            cols = slice(1024 * j, 1024 * (j + 1))
            a = _dot(xn2, wup_ref[:, cols])
            a_ref[:, cols] = a.astype(BF16)
            hid = jnp.square(jnp.maximum(a, 0.0)).astype(BF16)
            dn = dn + _dot(hid, wdn_ref[cols, :])
        r4, dnh = _rms_stats(dn)
        g4v = g4_ref[...]
        e = (h1v + dnh * g4v) - tg_ref[...]
        sq = e * e
        s1 = sq[:, 0:LANES]
        for j in range(1, D_MODEL // LANES):
            s1 = s1 + sq[:, LANES * j:LANES * (j + 1)]
        acc_l[...] += _fold8(s1)
        dy = e * (1.0 / D_MODEL)
        dy_ref[...] = dy
        acc_g[...] += _fold8(dy * dnh)
        ddn_ref[...] = _rms_bwd(dy, dnh, r4, g4v).astype(BF16)

        @pl.when(i == nsteps - 1)
        def _():
            loss_ref[...] = acc_l[...] * (0.5 / D_MODEL)
            dg4_ref[...] = jnp.sum(acc_g[...], axis=0, keepdims=True)

    return pl.pallas_call(
        body, grid=(nsteps,),
        out_shape=(jax.ShapeDtypeStruct((T, D_MODEL), BF16), jax.ShapeDtypeStruct((T, D_FF), BF16),
                   jax.ShapeDtypeStruct((T, D_MODEL), BF16), jax.ShapeDtypeStruct((T, D_MODEL), F32),
                   jax.ShapeDtypeStruct((SUBLANES, LANES), F32), jax.ShapeDtypeStruct((1, D_MODEL), F32)),
        in_specs=[_row_spec(tm, D_MODEL), _row_spec(tm, D_MODEL), _const_spec((D_MODEL, D_FF)),
                  _const_spec((D_FF, D_MODEL)), _const_spec((1, D_MODEL)), _const_spec((1, D_MODEL))],
        out_specs=[_row_spec(tm, D_MODEL), _row_spec(tm, D_FF), _row_spec(tm, D_MODEL), _row_spec(tm, D_MODEL),
                   pl.BlockSpec((SUBLANES, LANES), lambda i: (0, 0)), pl.BlockSpec((1, D_MODEL), lambda i: (0, 0))],
        scratch_shapes=[pltpu.VMEM((SUBLANES, LANES), F32), pltpu.VMEM((SUBLANES, D_MODEL), F32)],
        compiler_params=_params(52), name="fwd_ffn_loss",
    )(h1, tgt, wup, wdn, g3, g4)


def _bwd_ffn(ddn, a, dy, h1, wup, wdn, g3):
    T = h1.shape[0]
    tm = MATMUL_TILE
    nsteps = T // tm

    def body(ddn_ref, a_ref, dy_ref, h1_ref, wup_ref, wdn_ref, g3_ref, da_ref, dh1_ref, dg3_ref, acc_g):
        i = pl.program_id(0)

        @pl.when(i == 0)
        def _():
            acc_g[...] = jnp.zeros_like(acc_g)

        ddnv = ddn_ref[...]
        dxn2 = jnp.zeros((tm, D_MODEL), F32)
        for j in range(D_FF // 1024):
            cols = slice(1024 * j, 1024 * (j + 1))
            dhid = _dot_nt(ddnv, wdn_ref[cols, :])
            da = (dhid * (2.0 * jnp.maximum(a_ref[:, cols].astype(F32), 0.0))).astype(BF16)
            da_ref[:, cols] = da
            dxn2 = dxn2 + _dot_nt(da, wup_ref[:, cols])
        r3, h1h = _rms_stats(h1_ref[...])
        acc_g[...] += _fold8(dxn2 * h1h)
        dh1_ref[...] = dy_ref[...] + _rms_bwd(dxn2, h1h, r3, g3_ref[...])

        @pl.when(i == nsteps - 1)
        def _():
            dg3_ref[...] = jnp.sum(acc_g[...], axis=0, keepdims=True)

    return pl.pallas_call(
        body, grid=(nsteps,),
        out_shape=(jax.ShapeDtypeStruct((T, D_FF), BF16), jax.ShapeDtypeStruct((T, D_MODEL), F32),
                   jax.ShapeDtypeStruct((1, D_MODEL), F32)),
        in_specs=[_row_spec(tm, D_MODEL), _row_spec(tm, D_FF), _row_spec(tm, D_MODEL), _row_spec(tm, D_MODEL),
                  _const_spec((D_MODEL, D_FF)), _const_spec((D_FF, D_MODEL)), _const_spec((1, D_MODEL))],
        out_specs=[_row_spec(tm, D_FF), _row_spec(tm, D_MODEL), pl.BlockSpec((1, D_MODEL), lambda i: (0, 0))],
        scratch_shapes=[pltpu.VMEM((SUBLANES, D_MODEL), F32)],
        compiler_params=_params(52), name="bwd_ffn",
    )(ddn, a, dy, h1, wup, wdn, g3)


def _wgrad(xa, dy, name, relu2=False, tn=None, block_cols=None):
    T, K = xa.shape
    N = dy.shape[1]
    tn = N if tn is None else tn
    tt = min(T, WGRAD_TILE if K <= D_MODEL else WGRAD_TILE // 2)
    if block_cols:
        nb = tn // block_cols
        out_shape = jax.ShapeDtypeStruct((N // block_cols, K, block_cols), F32)
        out_spec = pl.BlockSpec((nb, K, block_cols), lambda n, t: (n, 0, 0))
    else:
        out_shape = jax.ShapeDtypeStruct((K, N), F32)
        out_spec = pl.BlockSpec((K, tn), lambda n, t: (0, n))

    def body(x_ref, dy_ref, o_ref):
        @pl.when(pl.program_id(1) == 0)
        def _():
            o_ref[...] = jnp.zeros_like(o_ref)

        xv = x_ref[...]
        if relu2:
            xv = jnp.square(jnp.maximum(xv.astype(F32), 0.0)).astype(BF16)
        if block_cols:
            for b in range(nb):
                o_ref[b] += _dot_tn(xv, dy_ref[:, block_cols * b:block_cols * (b + 1)])
        else:
            o_ref[...] += _dot_tn(xv, dy_ref[...])

    return pl.pallas_call(
        body, grid=(N // tn, T // tt), out_shape=out_shape,
        in_specs=[pl.BlockSpec((tt, K), lambda n, t: (t, 0)), pl.BlockSpec((tt, tn), lambda n, t: (t, n))],
        out_specs=out_spec,
        compiler_params=_params(52, 2), name=name,
    )(xa, dy)


def _wgrad_multi(xa, dys, name):
    T, K = xa.shape
    tt = min(T, WGRAD_TILE)
    n = len(dys)

    def body(x_ref, *refs):
        dy_refs, o_refs = refs[:n], refs[n:]

        @pl.when(pl.program_id(0) == 0)
        def _():
            for o_ref in o_refs:
                o_ref[...] = jnp.zeros_like(o_ref)

        xv = x_ref[...]
        for dy_ref, o_ref in zip(dy_refs, o_refs):
            o_ref[...] += _dot_tn(xv, dy_ref[...])

    return pl.pallas_call(
        body, grid=(T // tt,),
        out_shape=[jax.ShapeDtypeStruct((K, dy.shape[1]), F32) for dy in dys],
        in_specs=[_row_spec(tt, K)] + [_row_spec(tt, dy.shape[1]) for dy in dys],
        out_specs=[pl.BlockSpec((K, dy.shape[1]), lambda t: (0, 0)) for dy in dys],
        compiler_params=_params(52), name=name,
    )(xa, *dys)


def _bwd_merge(dh1, o, A, B, gt, ys, ya, mg, lse, cc, wbs, wba, wo, g2):
    T = dh1.shape[0]
    tm = MERGE_BWD_TILE
    nsteps = T // tm

    def body(dh1_ref, o_ref, a_ref, b_ref, gt_ref, ys_ref, ya_ref, mg_ref, lse_ref, cc_ref, wbs_ref, wba_ref,
             wo_ref, g2_ref, dgl_ref, dys_ref, dya_ref, qab_ref, dab_ref, dg2_ref, dwbs_ref, dwba_ref, dwo_ref,
             acc_g):
        i = pl.program_id(0)

        @pl.when(i == 0)
        def _():
            acc_g[...] = jnp.zeros_like(acc_g)
            dwbs_ref[...] = jnp.zeros_like(dwbs_ref)
            dwba_ref[...] = jnp.zeros_like(dwba_ref)
            dwo_ref[...] = jnp.zeros_like(dwo_ref)

        dh1v = dh1_ref[...]
        r2, oh = _rms_stats(o_ref[...].astype(F32))
        acc_g[...] += _fold8(dh1v * oh)
        do = _rms_bwd(dh1v, oh, r2, g2_ref[...]).astype(BF16)
        dwo_ref[...] += _dot_tn(mg_ref[...], do)
        dmg = _dot_nt(do, wo_ref[...])
        ga = gt_ref[:, :D_MODEL].astype(F32)
        gb = gt_ref[:, D_MODEL:].astype(F32)
        dgl_ref[:, :D_MODEL] = (dmg * a_ref[...].astype(F32) * ga * (1.0 - ga)).astype(BF16)
        dgl_ref[:, D_MODEL:] = (dmg * b_ref[...].astype(F32) * gb * (1.0 - gb)).astype(BF16)
        dA = (dmg * ga).astype(BF16)
        dB = (dmg * gb).astype(BF16)
        dwbs_ref[...] += _dot_tn(ys_ref[...], dA)
        dwba_ref[...] += _dot_tn(ya_ref[...], dB)
        dys_ref[...] = _dot_nt(dA, wbs_ref[...]).astype(BF16)
        dya = _dot_nt(dB, wba_ref[...]).astype(BF16)
        dya_ref[...] = dya
        prod = dya.astype(F32) * ya_ref[...].astype(F32)
        lane = _iota((tm, LANES), 1)
        low = lane < 64
        blk = jnp.zeros((tm, LANES), F32)
        for p in range(4):
            pp = prod[:, LANES * p:LANES * (p + 1)]
            blk = jnp.where(lane == 2 * p, jnp.sum(jnp.where(low, pp, 0.0), axis=1, keepdims=True), blk)
            blk = jnp.where(lane == 2 * p + 1, jnp.sum(jnp.where(low, 0.0, pp), axis=1, keepdims=True), blk)
        qab_ref[...] = _aug_query(cc_ref[...] - lse_ref[...])
        dab_ref[...] = _spread(_split3(-blk), 0).astype(BF16)

        @pl.when(i == nsteps - 1)
        def _():
            dg2_ref[...] = jnp.sum(acc_g[...], axis=0, keepdims=True)

    sh = jax.ShapeDtypeStruct((T, 512), BF16)
    sa = jax.ShapeDtypeStruct((T, LANES), BF16)
    sw = jax.ShapeDtypeStruct((512, D_MODEL), F32)
    whole = lambda shape: pl.BlockSpec(shape, lambda i: (0, 0))
    return pl.pallas_call(
        body, grid=(nsteps,),
        out_shape=(jax.ShapeDtypeStruct((T, 2048), BF16), sh, sh, sa, sa, jax.ShapeDtypeStruct((1, D_MODEL), F32),
                   sw, sw, jax.ShapeDtypeStruct((D_MODEL, D_MODEL), F32)),
        in_specs=[_row_spec(tm, D_MODEL)] * 4 + [_row_spec(tm, 2048), _row_spec(tm, 512), _row_spec(tm, 512),
                  _row_spec(tm, D_MODEL), _row_spec(tm, LANES), _row_spec(tm, LANES),
                  _const_spec((512, D_MODEL)), _const_spec((512, D_MODEL)),
                  _const_spec((D_MODEL, D_MODEL)), _const_spec((1, D_MODEL))],
        out_specs=[_row_spec(tm, 2048), _row_spec(tm, 512), _row_spec(tm, 512), _row_spec(tm, LANES),
                   _row_spec(tm, LANES), whole((1, D_MODEL)), whole((512, D_MODEL)), whole((512, D_MODEL)),
                   whole((D_MODEL, D_MODEL))],
        scratch_shapes=[pltpu.VMEM((SUBLANES, D_MODEL), F32)],
        compiler_params=_params(56), name="bwd_merge",
    )(dh1, o, A, B, gt, ys, ya, mg, lse, cc, wbs, wba, wo, g2)


def _bwd_sgu(zuv, dys, gs, bs, wsp, bT, grads):
    T = zuv.shape[0]
    tc = SGU_TILE
    nc = tc // CHUNK
    nsteps = T // tc
    ex = _GradExchange([tuple(g.shape[1:]) for g in grads])
    ng = ex.n

    def body(z_ref, dy_ref, gs_ref, bs_ref, w_ref, bT_ref, *rest):
        g_refs, (dz_ref, dw_ref, dbT_ref, dgs_ref, dbs_ref) = rest[:ng], rest[ng:ng + 5]
        land1 = rest[ng + 5:2 * ng + 5]
        acc_w, acc_b, acc_gs, acc_bs, dvln_s = rest[2 * ng + 5:2 * ng + 10]
        ex_sems = rest[2 * ng + 10:]
        i = pl.program_id(0)

        @pl.when(i == 0)
        def _():
            ex.start(1, g_refs, land1, ex_sems)
            acc_w[...] = jnp.zeros_like(acc_w)
            acc_b[...] = jnp.zeros_like(acc_b)
            acc_gs[...] = jnp.zeros_like(acc_gs)
            acc_bs[...] = jnp.zeros_like(acc_bs)

        z = z_ref[...].astype(F32)
        gsv = gs_ref[...]
        u, vhat, rs, vln = _sgu_forward_parts(z, gsv, bs_ref[...])
        vb = vln.astype(BF16)
        dy = dy_ref[...].astype(F32)
        low_w = (_iota((CHUNK, nc * LANES), 1) % LANES) < 64
        for p in range(4):
            we, wo, b2 = _sgu_pair_weights(w_ref, bT_ref[...], p)
            vcat = _chunks_on_lanes(vb, p, nc)
            s = _sgu_mix(we, wo, b2, vcat, nc)
            dyc = _chunks_on_lanes(dy, p, nc)
            ds = dyc * _chunks_on_lanes(u, p, nc)
            dsb = ds.astype(BF16)
            zero = jnp.zeros_like(dsb)
            dse = jnp.where(low_w, dsb, zero)
            dso = jnp.where(low_w, zero, dsb)
            acc_w[2 * p] += _dot_nt(dse, vcat)
            acc_w[2 * p + 1] += _dot_nt(dso, vcat)
            acc_b[p] += ds
            dvl = jnp.where(low_w, _dot_tn(we, dsb), _dot_tn(wo, dsb))
            for c in range(nc):
                rows, cols = slice(c * CHUNK, (c + 1) * CHUNK), slice(LANES * p, LANES * (p + 1))
                dvln_s[rows, cols] = dvl[:, c * LANES:(c + 1) * LANES]
                du = dy[rows, cols] * s[:, c * LANES:(c + 1) * LANES]
                dz_ref[rows, cols] = (du * _gelu_grad(z[rows, cols])).astype(BF16)
        dvln = dvln_s[...]
        acc_gs[...] += _fold8(dvln * vhat)
        acc_bs[...] += _fold8(dvln)
        al = dvln * gsv
        dvv = rs * (al - jnp.mean(al, axis=1, keepdims=True) - vhat * jnp.mean(al * vhat, axis=1, keepdims=True))
        dz_ref[:, SGU_WIDTH:] = (dvv * _gelu_grad(z[:, SGU_WIDTH:])).astype(BF16)

        @pl.when(i == nsteps - 1)
        def _():
            tril = _iota((CHUNK, CHUNK), 0) >= _iota((CHUNK, CHUNK), 1)
            lane = _iota((CHUNK, LANES), 1)
            low = lane < 64
            blk = jnp.zeros((CHUNK, LANES), F32)
            for g in range(8):
                dw_ref[g] = jnp.where(tril, acc_w[g], 0.0)
            for p in range(4):
                t = acc_b[p]
                tot = t[:, 0:LANES]
                for c in range(1, nc):
                    tot = tot + t[:, c * LANES:(c + 1) * LANES]
                blk = jnp.where(lane == 2 * p, jnp.sum(jnp.where(low, tot, 0.0), axis=1, keepdims=True), blk)
                blk = jnp.where(lane == 2 * p + 1, jnp.sum(jnp.where(low, 0.0, tot), axis=1, keepdims=True), blk)
            dbT_ref[...] = blk
            dgs_ref[...] = jnp.sum(acc_gs[...], axis=0, keepdims=True)
            dbs_ref[...] = jnp.sum(acc_bs[...], axis=0, keepdims=True)
            ex.wait(1, g_refs, land1, ex_sems)

    whole = lambda shape: pl.BlockSpec(shape, lambda i: (0,) * len(shape))
    hbm_spec = pl.BlockSpec(memory_space=pl.ANY)
    return pl.pallas_call(
        body, grid=(nsteps,),
        out_shape=[jax.ShapeDtypeStruct((T, 1024), BF16), jax.ShapeDtypeStruct((8, CHUNK, CHUNK), F32),
                   jax.ShapeDtypeStruct((CHUNK, LANES), F32), jax.ShapeDtypeStruct((1, SGU_WIDTH), F32),
                   jax.ShapeDtypeStruct((1, SGU_WIDTH), F32)] + ex.land_shapes(1),
        in_specs=[_row_spec(tc, 1024), _row_spec(tc, SGU_WIDTH), _const_spec((1, SGU_WIDTH)),
                  _const_spec((1, SGU_WIDTH)), _const_spec((8, CHUNK, CHUNK)), _const_spec((CHUNK, 8))]
        + [hbm_spec] * ng,
        out_specs=[_row_spec(tc, 1024), whole((8, CHUNK, CHUNK)), whole((CHUNK, LANES)),
                   whole((1, SGU_WIDTH)), whole((1, SGU_WIDTH))] + [hbm_spec] * ng,
        scratch_shapes=[pltpu.VMEM((8, CHUNK, CHUNK), F32), pltpu.VMEM((4, CHUNK, nc * LANES), F32),
                        pltpu.VMEM((SUBLANES, SGU_WIDTH), F32), pltpu.VMEM((SUBLANES, SGU_WIDTH), F32),
                        pltpu.VMEM((tc, SGU_WIDTH), F32)] + ex.sem_shapes(1),
        compiler_params=_params(48), name="bwd_sgu",
    )(zuv, dys, gs, bs, wsp, bT, *grads)


def _bwd_attn(qkv, dya, qab, dab, kaug, parts):
    T = qkv.shape[0]
    tq = tk = ATTN_TILE
    nq = T // tq
    nk = T // tk
    ex = _GradExchange([tuple(g.shape[1:]) for g in parts])
    nr = ex.n

    def body(q_ref, do_ref, qa_ref, da_ref, k_ref, v_ref, ka_ref, *rest):
        part_refs, (dq_ref, dk_ref, dv_ref, dcx_ref) = rest[:nr], rest[nr:nr + 4]
        land2, dq_acc, ex_sems = rest[nr + 4:2 * nr + 4], rest[2 * nr + 4], rest[2 * nr + 5:]
        p = pl.program_id(0)
        j = pl.program_id(1)

        @pl.when((p == 0) & (j == 0))
        def _():
            ex.start(2, part_refs, land2, ex_sems)

        lane = _iota((tq, LANES), 1)
        low = lane < 64
        row = _iota((2 * tq, tk), 0) % tq
        col = _iota((2 * tq, tk), 1)
        first = 2 * AUG_LANES * p
        half = tq // 2

        @pl.when(j == 0)
        def _():
            dq_acc[...] = jnp.zeros_like(dq_acc)

        @pl.when((j == 0) & (p == 0))
        def _():
            dcx_ref[...] = jnp.zeros_like(dcx_ref)

        ka = ka_ref[...]
        kk = jnp.concatenate([k_ref[...], ka], axis=1)
        vv = jnp.concatenate([v_ref[...], ka], axis=1)

        def tile(i, carry, r0, c0, nc, masked):
            dk_a, dv_a = carry
            nr = tq - r0
            qsl = pl.ds(pl.multiple_of(i * tq + r0, half), nr)
            qs = _aug_stack(q_ref[qsl, :], qa_ref[qsl, :], p)
            dos = _aug_stack(do_ref[qsl, :], da_ref[qsl, :], p)
            kc, vc = kk[c0:c0 + nc], vv[c0:c0 + nc]
            s = _dot_nt(qs, kc)
            if masked:
                s = jnp.where(col[:2 * nr, :nc] + c0 <= row[:2 * nr, :nc] % nr + r0, s, NEG)
            pm = jnp.exp(s)
            ds = pm * _dot_nt(dos, vc)
            dsb = ds.astype(BF16)
            dv_u = _dot_tn(pm.astype(BF16), dos[:, :LANES])
            dk_u = _dot_tn(dsb, qs)
            if nc == tk:
                dv_a, dk_a = dv_a + dv_u, dk_a + dk_u
            else:
                pad = lambda u: jnp.concatenate(
                    [jnp.zeros((n, u.shape[1]), F32) if z else u
                     for z, n in ((True, c0), (False, nc), (True, tk - c0 - nc)) if n], axis=0)
                dv_a, dk_a = dv_a + pad(dv_u), dk_a + pad(dk_u)
            dqx = _dot(dsb, kc)
            dq_acc[qsl, :] += jnp.where(low[:nr], dqx[:nr, :LANES], dqx[nr:, :LANES])
            dcx_ref[qsl, :] += (jnp.where(lane[:nr] == first, dqx[:nr, LANES:], 0.0)
                                + jnp.where(lane[:nr] == first + AUG_LANES, dqx[nr:, LANES:], 0.0))
            return dk_a, dv_a

        def q_block(i, carry, masked):
            return tile(i, carry, 0, 0, tk, masked)

        init =(jnp.zeros((tk, 2 * LANES), F32), jnp.zeros((tk, LANES), F32))
        carry = tile(j, init, 0, 0, half, True)
        carry = tile(j, carry, half, half, half, True)
        n_rest = nq - 1 - j

        def trip(t, c):
            for u in range(3):
                c = q_block(j + 1 + 3 * t + u, c, False)
            return c

        carry = lax.fori_loop(0, n_rest // 3, trip, carry)
        dk_a, dv_a = lax.fori_loop(j + 1 + 3 * (n_rest // 3), nq, lambda i, c: q_block(i, c, False), carry)
        dk_ref[...] = dk_a[:, :LANES].astype(BF16)
        dv_ref[...] = dv_a.astype(BF16)
        ksl = pl.ds(pl.multiple_of(j * tk, tk), tk)
        lk = _iota((tk, LANES), 1)
        dcx_ref[ksl, :] += jnp.where((lk == first + 3) | (lk == first + AUG_LANES + 3), dk_a[:, LANES:], 0.0)

        @pl.when(j == nk - 1)
        def _():
            dq_ref[...] = (dq_acc[...] * 0.125).astype(BF16)

        @pl.when((p == 3) & (j == nk - 1))
        def _():
            ex.wait(2, part_refs, land2, ex_sems)

    sh = jax.ShapeDtypeStruct((T, ATTN_WIDTH), BF16)
    full = lambda cb: pl.BlockSpec((T, LANES), lambda p, j: (0, cb + p))
    blk = lambda cb: pl.BlockSpec((tk, LANES), lambda p, j: (j, cb + p))
    hbm_spec = pl.BlockSpec(memory_space=pl.ANY)
    return pl.pallas_call(
        body, grid=(4, nk),
        out_shape=[sh, sh, sh, jax.ShapeDtypeStruct((T, LANES), F32)] + ex.land_shapes(2),
        in_specs=[full(0), full(0), _const_spec((T, LANES)), _const_spec((T, LANES)), blk(4), blk(8),
                  pl.BlockSpec((tk, LANES), lambda p, j: (j, 0))] + [hbm_spec] * nr,
        out_specs=[full(0), blk(0), blk(0), pl.BlockSpec((T, LANES), lambda p, j: (0, 0))] + [hbm_spec] * nr,
        scratch_shapes=[pltpu.VMEM((T, LANES), F32)] + ex.sem_shapes(2),
        compiler_params=_params(58, 2), name="bwd_attn",
    )(qkv, dya, qab, dab, qkv, qkv, kaug, *parts)


def _bwd_cum(dcx, fl, bfp):
    T = fl.shape[0]
    tb = CUM_TILE

    def body(dcx_ref, fl_ref, b_ref, dfl_ref, dbf_ref):
        triu = (_iota((tb, tb), 0) <= _iota((tb, tb), 1)).astype(F32)
        r, c = _iota((LANES, LANES), 0), _iota((LANES, LANES), 1)
        sel = (((r == AUG_LANES * c) & (c < N_HEADS)).astype(F32)
               - ((r == AUG_LANES * c + 3) & (c < N_HEADS)).astype(F32))
        carry = jnp.zeros((1, LANES), F32)
        dbf = jnp.zeros((1, LANES), F32)
        for i in reversed(range(T // tb)):
            colblk = jnp.dot(dcx_ref[i * tb:(i + 1) * tb, :], sel, precision=HIGHEST, preferred_element_type=F32)
            rc = jnp.dot(triu, colblk, precision=HIGHEST, preferred_element_type=F32) + carry
            carry = rc[0:1, :]
            sig = jax.nn.sigmoid(fl_ref[i * tb:(i + 1) * tb, :] + b_ref[...])
            dfl = rc * (1.0 - sig)
            dfl_ref[i * tb:(i + 1) * tb, :] = dfl.astype(BF16)
            dbf = dbf + jnp.sum(dfl, axis=0, keepdims=True)
        dbf_ref[...] = dbf

    return pl.pallas_call(
        body,
        out_shape=(jax.ShapeDtypeStruct((T, LANES), BF16), jax.ShapeDtypeStruct((1, LANES), F32)),
        compiler_params=pltpu.CompilerParams(vmem_limit_bytes=32 * MIB), name="bwd_cum",
    )(dcx, fl, bfp)


def _bwd_in(dz, dq, dk, dv, dfl, dgl, dh1, x2, g1, wz, wf, wg, rows, name, prev=None, stage=0, exchanged=None):
    T = x2.shape[0]
    tm = math.gcd(MATMUL_TILE, rows[0], rows[1] - rows[0])
    first = rows[0] // tm
    nsteps = (rows[1] - rows[0]) // tm
    ex = _GradExchange([tuple(exchanged.shape[1:])]) if stage else None

    def body(dz_ref, dq_ref, dk_ref, dv_ref, dfl_ref, dgl_ref, dh1_ref, x_ref, g_ref, wz_ref, wf_ref, wg_ref, *rest):
        rest = list(rest)
        dx_prev, dg1_prev = (rest.pop(0), rest.pop(0)) if prev else (None, None)
        src_ref = rest.pop(0) if stage else None
        dx_ref, dg1_ref = rest.pop(0), rest.pop(0)
        land_ref = rest.pop(0) if stage else None
        acc_g, ex_sems = rest[0], rest[1:]
        i = pl.program_id(0)

        @pl.when(i == 0)
        def _():
            if stage:
                ex.start(stage, [src_ref], [land_ref], ex_sems)
            acc_g[...] = jnp.zeros_like(acc_g)

        dxn = _dot_nt(dz_ref[...], wz_ref[:, 0:1024])
        dxn = dxn + _dot_nt(dq_ref[...], wz_ref[:, 1024:1536])
        dxn = dxn + _dot_nt(dk_ref[...], wz_ref[:, 1536:2048])
        dxn = dxn + _dot_nt(dv_ref[...], wz_ref[:, 2048:2560])
        dxn = dxn + _dot_nt(dfl_ref[...], wf_ref[...])
        dxn = dxn + _dot_nt(dgl_ref[...], wg_ref[...])
        r1, xh = _rms_stats(x_ref[...])
        acc_g[...] += _fold8(dxn * xh)
        dx_ref[...] = dh1_ref[...] + _rms_bwd(dxn, xh, r1, g_ref[...])

        @pl.when(i == nsteps - 1)
        def _():
            total = jnp.sum(acc_g[...], axis=0, keepdims=True)
            dg1_ref[...] = total + dg1_prev[...] if prev else total
            if stage:
                ex.wait(stage, [src_ref], [land_ref], ex_sems)

    hbm_spec = pl.BlockSpec(memory_space=pl.ANY)
    rows_spec = lambda n: pl.BlockSpec((tm, n), lambda i: (i + first, 0))
    operands = [dz, dq, dk, dv, dfl, dgl, dh1, x2, g1, wz, wf, wg]
    in_specs = [rows_spec(1024), rows_spec(512), rows_spec(512), rows_spec(512), rows_spec(LANES), rows_spec(2048),
                rows_spec(D_MODEL), rows_spec(D_MODEL),
                _const_spec((1, D_MODEL)), _const_spec((D_MODEL, ZQKV_WIDTH)), _const_spec((D_MODEL, LANES)),
                _const_spec((D_MODEL, 2048))]
    aliases = {}
    if prev:
        aliases = {len(operands): 0}
        operands += list(prev)
        in_specs += [hbm_spec, _const_spec((1, D_MODEL))]
    if stage:
        operands.append(exchanged)
        in_specs.append(hbm_spec)
    return pl.pallas_call(
        body, grid=(nsteps,),
        out_shape=[jax.ShapeDtypeStruct((T, D_MODEL), F32), jax.ShapeDtypeStruct((1, D_MODEL), F32)]
        + (ex.land_shapes(stage) if stage else []),
        in_specs=in_specs,
        out_specs=[rows_spec(D_MODEL), pl.BlockSpec((1, D_MODEL), lambda i: (0, 0))] + ([hbm_spec] if stage else []),
        scratch_shapes=[pltpu.VMEM((SUBLANES, D_MODEL), F32)] + (ex.sem_shapes(stage) if stage else []),
        input_output_aliases=aliases,
        compiler_params=_params(48), name=name,
    )(*operands)


def _small_kernel_shapes(g_mix_pre, b_forget, g_sgu, b_sgu, w_spatial, b_spatial, g_mix_post, g_ffn_pre, g_ffn_post):
    return dict(g_mix_pre=g_mix_pre, b_forget=jnp.pad(b_forget, ((0, 0), (0, LANES - N_HEADS))), g_sgu=g_sgu,
                b_sgu=b_sgu, w_spatial=w_spatial[0], b_spatial=b_spatial[0], g_mix_post=g_mix_post,
                g_ffn_pre=g_ffn_pre, g_ffn_post=g_ffn_post)


def _small_output_shapes(d):
    out = dict(d)
    out.update(b_forget=d["b_forget"][:, :N_HEADS], w_spatial=d["w_spatial"][None], b_spatial=d["b_spatial"][None])
    return out


def kernel(x, g_mix_pre, w_in, b_forget, g_sgu, b_sgu, w_spatial, b_spatial, w_branch_sgu, w_branch_attn, w_out, g_mix_post, g_ffn_pre, w_up, w_down, g_ffn_post, loss_target, m_g_mix_pre, m_w_in, m_b_forget, m_g_sgu, m_b_sgu, m_w_spatial, m_b_spatial, m_w_branch_sgu, m_w_branch_attn, m_w_out, m_g_mix_post, m_g_ffn_pre, m_w_up, m_w_down, m_g_ffn_post, v_g_mix_pre, v_w_in, v_b_forget, v_g_sgu, v_b_sgu, v_w_spatial, v_b_spatial, v_w_branch_sgu, v_w_branch_attn, v_w_out, v_g_mix_post, v_g_ffn_pre, v_w_up, v_w_down, v_g_ffn_post):
    T = x.shape[1]
    x2 = x.reshape(T, D_MODEL)
    tgt = loss_target.reshape(T, D_MODEL)

    wg_in = _gather_w_in(w_in[0])
    wz, wf, wgt = _assemble_w_in(wg_in)
    bfp = jnp.pad(b_forget, ((0, 0), (0, LANES - N_HEADS)))
    wsp = w_spatial[0]
    bT = b_spatial[0].T

    xn, zuv, qkv, fl, gt = _fwd_in(x2, g_mix_pre, wz, wf, wgt)
    cc, qaug, kaug = _fwd_cum(fl, bfp)
    ys = _fwd_sgu(zuv, g_sgu, b_sgu, wsp, bT)
    ya, lse, wbs, wba, wo, wup, wdn = _fwd_attn(
        qkv, qaug, kaug, (w_branch_sgu[0], w_branch_attn[0], w_out[0], w_up[0], w_down[0]))
    A, B, mg, o, h1 = _fwd_merge(ys, ya, gt, x2, wbs, wba, wo, g_mix_post)
    xn2, a, ddn, dy, loss_part, dg4 = _fwd_ffn_loss(h1, tgt, wup, wdn, g_ffn_pre, g_ffn_post)

    da, dh1, dg3 = _bwd_ffn(ddn, a, dy, h1, wup, wdn, g_ffn_pre)
    dw_up = _wgrad(xn2, da, "wgrad_up", tn=2048, block_cols=512)
    dw_down = _wgrad(a, ddn, "wgrad_down", relu2=True)
    dgl, dys, dya, qab, dab, dg2, dw_bs, dw_ba, dw_out = _bwd_merge(
        dh1, o, A, B, gt, ys, ya, mg, lse, cc, wbs, wba, wo, g_mix_post)
    col_blocks = lambda g, w: g.reshape(g.shape[0], N_DEV, w).transpose(1, 0, 2)
    row_blocks = lambda g, r: g.reshape(N_DEV, r, g.shape[1])
    early_names = ["w_branch_sgu", "w_branch_attn", "w_out", "w_up", "w_down"]
    early = [col_blocks(dw_bs, 128), col_blocks(dw_ba, 128), row_blocks(dw_out, 128), dw_up, row_blocks(dw_down, 512)]
    owners = _owner_indices()
    dzuv, dwsp, dbT, dgs, dbs, *early_land1 = _bwd_sgu(zuv, dys, g_sgu, b_sgu, wsp, bT, early)
    early_parts = [_chip_partials(g, l1, owners, "chip_partials_" + nm)
                   for g, l1, nm in zip(early, early_land1, early_names)]
    dq, dk, dv, dcx, *early_land2 = _bwd_attn(qkv, dya, qab, dab, kaug, early_parts)
    dfl, dbf = _bwd_cum(dcx, fl, bfp)
    dw_z = _wgrad(xn, dzuv, "wgrad_in_z")
    dw_q, dw_k, dw_v, dw_f = _wgrad_multi(xn, [dq, dk, dv, dfl], "wgrad_in_qkvf")
    dw_g = _wgrad(xn, dgl, "wgrad_in_gate")
    blocks_in = _block_dw_in([dw_z, dw_q, dw_k, dw_v, dw_f, dw_g])
    bwd_in_args = (dzuv, dq, dk, dv, dfl, dgl, dh1, x2, g_mix_pre, wz, wf, wgt)
    dx, dg1, land1_in = _bwd_in(*bwd_in_args, (0, T // 8), "bwd_in_a", stage=1, exchanged=blocks_in)
    part_in = _chip_partials(blocks_in, land1_in, owners, "chip_partials_w_in")
    dx, dg1, land2_in = _bwd_in(*bwd_in_args, (T // 8, 3 * T // 4), "bwd_in_b", prev=(dx, dg1), stage=2,
                                exchanged=part_in)
    dx, dg1 = _bwd_in(*bwd_in_args, (3 * T // 4, T), "bwd_in_c", prev=(dx, dg1))

    tot_a, tot_b = _allreduce_small(dict(
        g_mix_pre=dg1, b_forget=dbf, g_sgu=dgs, b_sgu=dbs, w_spatial=dwsp, b_spatial=dbT, g_mix_post=dg2,
        g_ffn_pre=dg3, g_ffn_post=dg4), loss_part)
    r0, nr, c0, nc = LOSS_SLOT
    loss = jnp.sum(tot_a[r0:r0 + nr, c0:c0 + nc])
    small_w = _small_kernel_shapes(g_mix_pre, b_forget, g_sgu, b_sgu, w_spatial, b_spatial, g_mix_post, g_ffn_pre,
                                   g_ffn_post)
    small_m = _small_kernel_shapes(m_g_mix_pre, m_b_forget, m_g_sgu, m_b_sgu, m_w_spatial, m_b_spatial, m_g_mix_post,
                                   m_g_ffn_pre, m_g_ffn_post)
    small_v = _small_kernel_shapes(v_g_mix_pre, v_b_forget, v_g_sgu, v_b_sgu, v_w_spatial, v_b_spatial, v_g_mix_post,
                                   v_g_ffn_pre, v_g_ffn_post)
    sg, sd, sm, sv = (_small_output_shapes(d) for d in _adamw_small(tot_a, tot_b, small_w, small_m, small_v))

    big = {}
    g_in = _reduced_grad(blocks_in, land1_in, land2_in, owners, "reduced_grad_w_in")[:, :IN_SHARD]
    d_, m_, v_ = _adamw(w_in[0], g_in, m_w_in[0], v_w_in[0], "adamw_w_in")
    big["w_in"] = (g_in[None], d_[None], m_[None], v_[None])
    early_wmv = [(w_branch_sgu, m_w_branch_sgu, v_w_branch_sgu), (w_branch_attn, m_w_branch_attn, v_w_branch_attn),
                 (w_out, m_w_out, v_w_out), (w_up, m_w_up, v_w_up), (w_down, m_w_down, v_w_down)]
    for nm, (w, m, v), g, l1, l2 in zip(early_names, early_wmv, early, early_land1, early_land2):
        big[nm] = tuple(t[None] for t in _adamw_reduced(w[0], m[0], v[0], g, l1, l2, owners, "adamw_" + nm))

    order = ["g_mix_pre", "w_in", "b_forget", "g_sgu", "b_sgu", "w_spatial", "b_spatial", "w_branch_sgu",
             "w_branch_attn", "w_out", "g_mix_post", "g_ffn_pre", "w_up", "w_down", "g_ffn_post"]
    outs = [loss, dx.reshape(1, T, D_MODEL)]
    for kind, small in enumerate((sg, sd, sm, sv)):
        outs += [big[nm][kind] if nm in big else small[nm] for nm in order]
    return tuple(outs)
```

```python
import math

import jax
import jax.numpy as jnp
from jax import lax
from jax.experimental import pallas as pl
from jax.experimental.pallas import tpu as pltpu

F32 = jnp.float32
BF16 = jnp.bfloat16
HIGHEST = lax.Precision.HIGHEST
MESH = pl.DeviceIdType.MESH

D_MODEL = 1024
SGU_WIDTH = 512
ATTN_WIDTH = 512
N_HEADS = 8
CHUNK = 128
D_FF = 4096
IN_WIDTH = 4616
N_DEV = 8
IN_SHARD = IN_WIDTH // N_DEV
IN_SHARD_PAD = 640
ZQKV_WIDTH = 2 * SGU_WIDTH + 3 * ATTN_WIDTH
GATE_OFFSET = ZQKV_WIDTH + N_HEADS
EPS = 1e-6
LANES = 128
SUBLANES = 8
VMEM_BYTES = 64 * 1024 * 1024
MIB = 1024 * 1024

ADAM_LR = 0.001
ADAM_B1 = 0.9
ADAM_B2 = 0.999
ADAM_EPS = 1e-08
ADAM_WD = 0.01
ADAM_STEP = 10

TOKEN_TILE = 512
MATMUL_TILE = 512
MERGE_FWD_TILE = 512
MERGE_BWD_TILE = 512
ATTN_TILE = 512
CUM_TILE = 256
SGU_TILE = 1024
WGRAD_TILE = 1024
NEG = -1e30

NT_DIMS = (((1,), (1,)), ((), ()))
TN_DIMS = (((0,), (0,)), ((), ()))


def _params(vmem_mb, n_grid=1):
    return pltpu.CompilerParams(
        dimension_semantics=("arbitrary",) * n_grid,
        vmem_limit_bytes=min(vmem_mb * MIB, VMEM_BYTES - 6 * MIB),
    )


def _dot(a, b):
    return jnp.dot(a, b, preferred_element_type=F32)


def _dot_nt(a, b):
    return lax.dot_general(a, b, NT_DIMS, preferred_element_type=F32)


def _dot_tn(a, b):
    return lax.dot_general(a, b, TN_DIMS, preferred_element_type=F32)


def _const_spec(shape):
    nd = len(shape)
    return pl.BlockSpec(shape, lambda *_: (0,) * nd, pipeline_mode=pl.Buffered(1))


def _row_spec(tm, n, col=0):
    return pl.BlockSpec((tm, n), lambda i: (i, col))


def _fold8(v):
    return v.reshape(v.shape[0] // SUBLANES, SUBLANES, v.shape[1]).sum(axis=0)


def _pick(v, lane_iota, k):
    return jnp.sum(jnp.where(lane_iota == k, v, 0.0), axis=1, keepdims=True)


def _iota(shape, dim):
    return lax.broadcasted_iota(jnp.int32, shape, dim)


def _gelu(x):
    c = 0.7978845608028654
    return 0.5 * x * (1.0 + jnp.tanh(c * (x + 0.044715 * x * x * x)))


def _gelu_grad(x):
    c = 0.7978845608028654
    t = jnp.tanh(c * (x + 0.044715 * x * x * x))
    return 0.5 * (1.0 + t) + 0.5 * x * (1.0 - t * t) * (c * (1.0 + 3.0 * 0.044715 * x * x))


def _rms_stats(v):
    r = lax.rsqrt(jnp.mean(v * v, axis=1, keepdims=True) + EPS)
    return r, v * r


def _rms_bwd(dout, vhat, r, g):
    a = dout * g
    return r * (a - vhat * jnp.mean(a * vhat, axis=1, keepdims=True))


def _mesh_pos():
    return lax.axis_index("x"), lax.axis_index("y"), lax.axis_index("c")


def _dev_index(px, py, pc):
    return 4 * px + 2 * py + pc


def _other_chips(x, y):
    return [(1 - x, y), (x, 1 - y), (1 - x, 1 - y)]


class _WeightGather:
    def __init__(self, shard_shapes, kinds, stage_shapes=None):
        self.shard_shapes = list(shard_shapes)
        self.kinds = list(kinds)
        self.stage_shapes = list(stage_shapes or shard_shapes)
        self.n = len(self.kinds)

    def out_shapes(self):
        shapes = []
        for (rows, cols), kind in zip(self.stage_shapes, self.kinds):
            full = {"block": (N_DEV, rows, cols), "rows": (N_DEV * rows, cols), "cols": (rows, N_DEV * cols)}[kind]
            shapes.append(jax.ShapeDtypeStruct(full, BF16))
        return shapes

    def scratch_shapes(self):
        return ([pltpu.VMEM(s, BF16) for s in self.stage_shapes]
                + [pltpu.SemaphoreType.DMA((self.n, 7)), pltpu.SemaphoreType.DMA((self.n, 7)),
                   pltpu.SemaphoreType.DMA((self.n,))])

    def _view(self, a, ref, j):
        rows, cols = self.stage_shapes[a]
        if self.kinds[a] == "block":
            return ref.at[j]
        if self.kinds[a] == "rows":
            return ref.at[pl.ds(pl.multiple_of(j * rows, rows), rows), :]
        return ref.at[:, pl.ds(pl.multiple_of(j * cols, cols), cols)]

    def _copy(self, outs, scratch, a, k, block, to, from_stage=False):
        stage, (send_sems, recv_sems, _) = scratch[:self.n], scratch[self.n:]
        dst = self._view(a, outs[a], _dev_index(*block))
        return pltpu.make_async_remote_copy(
            src_ref=stage[a] if from_stage else dst, dst_ref=dst,
            send_sem=send_sems.at[a, k], recv_sem=recv_sems.at[a, k],
            device_id=to, device_id_type=MESH)

    def _local(self, outs, scratch, a, me):
        return pltpu.make_async_copy(scratch[a], self._view(a, outs[a], _dev_index(*me)), scratch[-1].at[a])

    def start(self, ins, outs, scratch):
        x, y, c = _mesh_pos()
        me, sibling = (x, y, c), (x, y, 1 - c)
        for a in range(self.n):
            rows, cols = self.shard_shapes[a]
            if self.stage_shapes[a] != self.shard_shapes[a]:
                scratch[a][...] = jnp.zeros(self.stage_shapes[a], BF16)
            scratch[a][0:rows, 0:cols] = ins[a][...].astype(BF16)
            self._local(outs, scratch, a, me).start()
        for a in range(self.n):
            self._copy(outs, scratch, a, 0, me, sibling, True).start()
            for j, chip in enumerate(_other_chips(x, y)):
                self._copy(outs, scratch, a, 1 + j, me, (*chip, c), True).start()

    def forward(self, outs, scratch):
        x, y, c = _mesh_pos()
        me, sibling = (x, y, c), (x, y, 1 - c)
        for a in range(self.n):
            for j, chip in enumerate(_other_chips(x, y)):
                self._copy(outs, scratch, a, 1 + j, (*chip, c), me).wait_recv()
                self._copy(outs, scratch, a, 4 + j, (*chip, c), sibling).start()

    def finish(self, outs, scratch):
        x, y, c = _mesh_pos()
        me, sibling = (x, y, c), (x, y, 1 - c)
        chips = _other_chips(x, y)
        for a in range(self.n):
            self._copy(outs, scratch, a, 0, sibling, me).wait_recv()
            for j, chip in enumerate(chips):
                self._copy(outs, scratch, a, 4 + j, (*chip, 1 - c), me).wait_recv()
        for a in range(self.n):
            self._copy(outs, scratch, a, 0, me, sibling, True).wait_send()
            for j, chip in enumerate(chips):
                self._copy(outs, scratch, a, 1 + j, me, (*chip, c), True).wait_send()
                self._copy(outs, scratch, a, 4 + j, (*chip, c), sibling).wait_send()
            self._local(outs, scratch, a, me).wait()


def _gather_w_in(w_in_local):
    g = _WeightGather([(D_MODEL, IN_SHARD)], ["block"], [(D_MODEL, IN_SHARD_PAD)])

    def body(w_ref, out_ref, *scratch):
        g.start([w_ref], [out_ref], scratch)
        g.forward([out_ref], scratch)
        g.finish([out_ref], scratch)

    return pl.pallas_call(
        body,
        out_shape=g.out_shapes()[0],
        in_specs=[pl.BlockSpec(memory_space=pltpu.VMEM)],
        out_specs=pl.BlockSpec(memory_space=pl.ANY),
        scratch_shapes=g.scratch_shapes(),
        compiler_params=pltpu.CompilerParams(vmem_limit_bytes=32 * MIB),
        name="gather_w_in",
    )(w_in_local)


class _GradExchange:
    def __init__(self, shapes):
        self.shapes = [tuple(s) for s in shapes]
        self.n = len(self.shapes)

    def land_shapes(self, stage):
        slots, dtype = (4, F32) if stage == 1 else (3, BF16)
        return [jax.ShapeDtypeStruct((slots,) + s, dtype) for s in self.shapes]

    def sem_shapes(self, stage):
        slots = 4 if stage == 1 else 3
        return [pltpu.SemaphoreType.DMA((self.n, slots)), pltpu.SemaphoreType.DMA((self.n, slots))]

    def _copy(self, stage, srcs, lands, sems, a, k):
        x, y, c = _mesh_pos()
        cx, cy = (_other_chips(x, y) + [(x, y)])[k]
        if stage == 1:
            src, to = srcs[a].at[_dev_index(cx, cy, 1 - c)], (x, y, 1 - c)
        else:
            src, to = srcs[a].at[k], (cx, cy, c)
        return pltpu.make_async_remote_copy(
            src_ref=src, dst_ref=lands[a].at[k], send_sem=sems[0].at[a, k], recv_sem=sems[1].at[a, k],
            device_id=to, device_id_type=MESH)

    def start(self, stage, srcs, lands, sems):
        for a in range(self.n):
            for k in range(4 if stage == 1 else 3):
                self._copy(stage, srcs, lands, sems, a, k).start()

    def wait(self, stage, srcs, lands, sems):
        for a in range(self.n):
            for k in range(4 if stage == 1 else 3):
                cp = self._copy(stage, srcs, lands, sems, a, k)
                cp.wait_recv()
                cp.wait_send()


def _owner_indices():
    x, y, c = _mesh_pos()
    return jnp.stack([_dev_index(cx, cy, c) for cx, cy in _other_chips(x, y) + [(x, y)]]).astype(jnp.int32)


def _chip_partials(g, land1, idx, name):
    _, rows, cols = g.shape
    tr = min(rows, 256)

    def body(idx_ref, g_ref, l_ref, o_ref):
        o_ref[...] = (g_ref[...] + l_ref[...]).astype(BF16)

    return pl.pallas_call(
        body,
        grid_spec=pltpu.PrefetchScalarGridSpec(
            num_scalar_prefetch=1, grid=(3, rows // tr),
            in_specs=[pl.BlockSpec((None, tr, cols), lambda k, r, idx: (idx[k], r, 0)),
                      pl.BlockSpec((None, tr, cols), lambda k, r, idx: (k, r, 0))],
            out_specs=pl.BlockSpec((None, tr, cols), lambda k, r, idx: (k, r, 0))),
        out_shape=jax.ShapeDtypeStruct((3, rows, cols), BF16),
        compiler_params=_params(32, 2), name=name,
    )(idx, g, land1)


def _reduced_block(g_ref, l1_ref, a_ref, b_ref, c_ref):
    return ((g_ref[...] + l1_ref[...]) + a_ref[...].astype(F32)) + b_ref[...].astype(F32) + c_ref[...].astype(F32)


def _reduced_specs(tm, cols):
    return [pl.BlockSpec((None, tm, cols), lambda i, idx: (idx[3], i, 0)),
            pl.BlockSpec((None, tm, cols), lambda i, idx: (3, i, 0)),
            pl.BlockSpec((None, tm, cols), lambda i, idx: (0, i, 0)),
            pl.BlockSpec((None, tm, cols), lambda i, idx: (1, i, 0)),
            pl.BlockSpec((None, tm, cols), lambda i, idx: (2, i, 0))]


def _reduced_grad(g, land1, land2, idx, name):
    _, rows, cols = g.shape
    tm = min(rows, 256)

    def body(idx_ref, g_ref, l1_ref, a_ref, b_ref, c_ref, o_ref):
        o_ref[...] = _reduced_block(g_ref, l1_ref, a_ref, b_ref, c_ref)

    return pl.pallas_call(
        body,
        grid_spec=pltpu.PrefetchScalarGridSpec(
            num_scalar_prefetch=1, grid=(rows // tm,), in_specs=_reduced_specs(tm, cols),
            out_specs=pl.BlockSpec((tm, cols), lambda i, idx: (i, 0))),
        out_shape=jax.ShapeDtypeStruct((rows, cols), F32),
        compiler_params=_params(32), name=name,
    )(idx, g, land1, land2, land2, land2)


def _adamw_math(w, g, m, v):
    m = ADAM_B1 * m + (1.0 - ADAM_B1) * g
    v = ADAM_B2 * v + (1.0 - ADAM_B2) * (g * g)
    m_hat = m / (1.0 - ADAM_B1 ** ADAM_STEP)
    v_hat = v / (1.0 - ADAM_B2 ** ADAM_STEP)
    delta = -ADAM_LR * (m_hat / (jnp.sqrt(v_hat) + ADAM_EPS) + ADAM_WD * w)
    return delta, m, v


SMALL_NAMES = ("g_mix_pre", "b_forget", "g_sgu", "b_sgu", "w_spatial", "b_spatial", "g_mix_post", "g_ffn_pre",
               "g_ffn_post")
SMALL_SLOTS = {"g_mix_pre": (0, 1, 0, 1024), "g_mix_post": (1, 1, 0, 1024), "g_ffn_pre": (2, 1, 0, 1024),
               "g_ffn_post": (3, 1, 0, 1024), "g_sgu": (4, 1, 0, 512), "b_sgu": (4, 1, 512, 512),
               "b_forget": (5, 1, 0, 128), "b_spatial": (8, 8, 0, 128)}
SMALL_TILE = (16, 1024)
SPATIAL_TILE = (N_HEADS * CHUNK, CHUNK)


LOSS_SLOT = (8, 8, 128, 128)


def _allreduce_small(grads, loss_part):
    names = list(SMALL_NAMES)

    def body(*refs):
        g = dict(zip(names, refs[:len(names)]))
        loss_ref = refs[len(names)]
        tot_a, tot_b, buf_a, buf_b, sib_a, sib_b, ps_a, ps_b, land_a, land_b, send_sems, recv_sems = refs[len(names) + 1:]
        x, y, c = _mesh_pos()
        buf_a[...] = jnp.zeros(SMALL_TILE, F32)
        r0, nr, c0, nc = LOSS_SLOT
        buf_a[r0:r0 + nr, c0:c0 + nc] = loss_ref[...]
        for name, (r0, nr, c0, nc) in SMALL_SLOTS.items():
            val = g[name][...]
            if name == "b_spatial":
                val = jnp.transpose(val)[0:N_HEADS, :]
            buf_a[r0:r0 + nr, c0:c0 + nc] = val
        buf_b[...] = g["w_spatial"][...].reshape(SPATIAL_TILE)

        def swap(k, src, dst, to):
            return pltpu.make_async_remote_copy(src_ref=src, dst_ref=dst, send_sem=send_sems.at[k],
                                                recv_sem=recv_sems.at[k], device_id=to, device_id_type=MESH)

        first = [swap(0, buf_a, sib_a, (x, y, 1 - c)), swap(1, buf_b, sib_b, (x, y, 1 - c))]
        for cp in first:
            cp.start()
        for cp in first:
            cp.wait_recv()
        ps_a[...] = buf_a[...] + sib_a[...]
        ps_b[...] = buf_b[...] + sib_b[...]
        half = SPATIAL_TILE[0] // 2
        mine = pl.ds(pl.multiple_of(c * half, half), half)
        second = []
        for k, (cx, cy) in enumerate(_other_chips(x, y)):
            second += [swap(2 + 2 * k, ps_a, land_a.at[k], (cx, cy, c)),
                       swap(3 + 2 * k, ps_b.at[mine, :], land_b.at[k, mine, :], (cx, cy, c))]
        for cp in second:
            cp.start()
        for cp in second:
            cp.wait_recv()
        tot_a[...] = (ps_a[...] + land_a[0]) + (land_a[1] + land_a[2])
        tot_b[mine, :] = (ps_b[mine, :] + land_b[0, mine, :]) + (land_b[1, mine, :] + land_b[2, mine, :])
        third = swap(8, tot_b.at[mine, :], tot_b.at[mine, :], (x, y, 1 - c))
        third.start()
        third.wait_recv()
        for cp in first + second + [third]:
            cp.wait_send()

    vm = pl.BlockSpec(memory_space=pltpu.VMEM)
    return pl.pallas_call(
        body,
        out_shape=(jax.ShapeDtypeStruct(SMALL_TILE, F32), jax.ShapeDtypeStruct(SPATIAL_TILE, F32)),
        in_specs=[vm] * (len(names) + 1), out_specs=[vm, vm],
        scratch_shapes=[pltpu.VMEM(SMALL_TILE, F32), pltpu.VMEM(SPATIAL_TILE, F32)] * 3
        + [pltpu.VMEM((3,) + SMALL_TILE, F32), pltpu.VMEM((3,) + SPATIAL_TILE, F32),
           pltpu.SemaphoreType.DMA((9,)), pltpu.SemaphoreType.DMA((9,))],
        compiler_params=pltpu.CompilerParams(vmem_limit_bytes=32 * MIB),
        name="allreduce_small",
    )(*[grads[nm] for nm in names], loss_part)


def _adamw_small(tot_a, tot_b, ws, ms, vs):
    names = list(SMALL_NAMES)
    n = len(names)

    def body(a_ref, b_ref, *refs):
        w, m, v = (dict(zip(names, refs[i * n:(i + 1) * n])) for i in range(3))
        outs = [dict(zip(names, refs[(3 + i) * n:(4 + i) * n])) for i in range(4)]
        for name in names:
            if name == "w_spatial":
                g = b_ref[...].reshape(N_HEADS, CHUNK, CHUNK)
            else:
                r0, nr, c0, nc = SMALL_SLOTS[name]
                g = a_ref[r0:r0 + nr, c0:c0 + nc]
            vals = (g,) + _adamw_math(w[name][...], g, m[name][...], v[name][...])
            for out, val in zip(outs, vals):
                out[name][...] = val

    shapes = [jax.ShapeDtypeStruct(ws[nm].shape, F32) for nm in names]
    vm = pl.BlockSpec(memory_space=pltpu.VMEM)
    res = pl.pallas_call(
        body, out_shape=shapes * 4, in_specs=[vm] * (2 + 3 * n), out_specs=[vm] * (4 * n),
        compiler_params=pltpu.CompilerParams(vmem_limit_bytes=32 * MIB), name="adamw_small",
    )(tot_a, tot_b, *[d[nm] for d in (ws, ms, vs) for nm in names])
    return [dict(zip(names, res[i * n:(i + 1) * n])) for i in range(4)]


def _adamw_reduced(w, m, v, g, land1, land2, idx, name):
    rows, cols = w.shape
    tm = min(rows, 256)

    def body(idx_ref, w_ref, m_ref, v_ref, g_ref, l1_ref, a_ref, b_ref, c_ref, go_ref, d_ref, nm_ref, nv_ref):
        gsum = _reduced_block(g_ref, l1_ref, a_ref, b_ref, c_ref)
        go_ref[...] = gsum
        delta, nm, nv = _adamw_math(w_ref[...], gsum, m_ref[...], v_ref[...])
        d_ref[...] = delta
        nm_ref[...] = nm
        nv_ref[...] = nv

    sd = jax.ShapeDtypeStruct((rows, cols), F32)
    spec = pl.BlockSpec((tm, cols), lambda i, idx: (i, 0))
    return pl.pallas_call(
        body,
        grid_spec=pltpu.PrefetchScalarGridSpec(
            num_scalar_prefetch=1, grid=(rows // tm,), in_specs=[spec] * 3 + _reduced_specs(tm, cols),
            out_specs=[spec] * 4),
        out_shape=(sd, sd, sd, sd),
        compiler_params=_params(32), name=name,
    )(idx, w, m, v, g, land1, land2, land2, land2)


def _adamw(w, g, m, v, name):
    rows, cols = w.shape
    tm = 256 if rows % 256 == 0 else rows

    def body(w_ref, g_ref, m_ref, v_ref, d_ref, nm_ref, nv_ref):
        delta, nm, nv = _adamw_math(w_ref[...], g_ref[...], m_ref[...], v_ref[...])
        d_ref[...] = delta
        nm_ref[...] = nm
        nv_ref[...] = nv

    sd = jax.ShapeDtypeStruct((rows, cols), F32)
    spec = _row_spec(tm, cols)
    return pl.pallas_call(
        body, grid=(rows // tm,), out_shape=(sd, sd, sd), in_specs=[spec] * 4, out_specs=[spec] * 3,
        compiler_params=_params(32), name=name,
    )(w, g, m, v)


def _virtual_slab(sources, v0, v_end, like):
    lane = _iota(like.shape, 1)
    out = jnp.zeros(like.shape, like.dtype)
    for v_start, v_stop, read in sources:
        a, b = max(v0, v_start), min(v0 + LANES, v_stop, v_end)
        while a < b:
            c = a - v_start
            n = min(b - a, LANES - c % LANES)
            piece = read(c // LANES)
            shift = (a - v0 - c % LANES) % LANES
            if shift:
                piece = pltpu.roll(piece, shift, 1)
            out = jnp.where((lane >= a - v0) & (lane < a - v0 + n), piece, out)
            a += n
    return out


def _assemble_w_in(wg_in):
    tm = TOKEN_TILE

    def body(src_ref, wz_ref, wf_ref, wg_ref):
        like = src_ref[0, :, 0:LANES]
        sources = [(IN_SHARD * j, IN_SHARD * (j + 1),
                    (lambda k, j=j: src_ref[j, :, LANES * k:LANES * (k + 1)])) for j in range(N_DEV)]
        for k in range(ZQKV_WIDTH // LANES):
            wz_ref[:, LANES * k:LANES * (k + 1)] = _virtual_slab(sources, LANES * k, ZQKV_WIDTH, like)
        wf_ref[...] = _virtual_slab(sources, ZQKV_WIDTH, GATE_OFFSET, like)
        for k in range(2 * D_MODEL // LANES):
            wg_ref[:, LANES * k:LANES * (k + 1)] = _virtual_slab(sources, GATE_OFFSET + LANES * k, IN_WIDTH, like)

    return pl.pallas_call(
        body, grid=(D_MODEL // tm,),
        out_shape=(jax.ShapeDtypeStruct((D_MODEL, ZQKV_WIDTH), BF16), jax.ShapeDtypeStruct((D_MODEL, LANES), BF16),
                   jax.ShapeDtypeStruct((D_MODEL, 2 * D_MODEL), BF16)),
        in_specs=[pl.BlockSpec((N_DEV, tm, IN_SHARD_PAD), lambda i: (0, i, 0))],
        out_specs=[_row_spec(tm, ZQKV_WIDTH), _row_spec(tm, LANES), _row_spec(tm, 2 * D_MODEL)],
        compiler_params=_params(32), name="assemble_w_in",
    )(wg_in)


def _block_dw_in(pieces):
    tm = TOKEN_TILE
    widths = [2 * SGU_WIDTH, ATTN_WIDTH, ATTN_WIDTH, ATTN_WIDTH, N_HEADS, 2 * D_MODEL]
    starts = [sum(widths[:k]) for k in range(len(widths))]

    def body(*refs):
        in_refs, out_ref = refs[:-1], refs[-1]
        like = in_refs[0][:, 0:LANES]
        slab = lambda ref: (lambda k: ref[:, LANES * k:LANES * (k + 1)])
        sources = [(s, s + w, slab(ref)) for s, w, ref in zip(starts, widths, in_refs)]
        for j in range(N_DEV):
            for k in range(IN_SHARD_PAD // LANES):
                out_ref[j, :, LANES * k:LANES * (k + 1)] = _virtual_slab(
                    sources, IN_SHARD * j + LANES * k, IN_SHARD * (j + 1), like)

    return pl.pallas_call(
        body, grid=(D_MODEL // tm,),
        out_shape=jax.ShapeDtypeStruct((N_DEV, D_MODEL, IN_SHARD_PAD), F32),
        in_specs=[_row_spec(tm, pc.shape[1]) for pc in pieces],
        out_specs=pl.BlockSpec((N_DEV, tm, IN_SHARD_PAD), lambda i: (0, i, 0)),
        compiler_params=_params(32), name="block_dw_in",
    )(*pieces)


def _fwd_in(x2, g1, wz, wf, wg):
    T = x2.shape[0]
    tm = MATMUL_TILE

    def body(x_ref, g_ref, wz_ref, wf_ref, wg_ref, xn_ref, zuv_ref, qkv_ref, fl_ref, gt_ref):
        x = x_ref[...]
        r, xh = _rms_stats(x)
        xn = (xh * g_ref[...]).astype(BF16)
        xn_ref[...] = xn
        zuv_ref[...] = _dot(xn, wz_ref[:, 0:1024]).astype(BF16)
        qkv_ref[:, 0:512] = (_dot(xn, wz_ref[:, 1024:1536]) * 0.125).astype(BF16)
        qkv_ref[:, 512:1536] = _dot(xn, wz_ref[:, 1536:2560]).astype(BF16)
        fl_ref[...] = _dot(xn, wf_ref[...])
        gt_ref[...] = jax.nn.sigmoid(_dot(xn, wg_ref[...])).astype(BF16)

    return pl.pallas_call(
        body, grid=(T // tm,),
        out_shape=(jax.ShapeDtypeStruct((T, D_MODEL), BF16), jax.ShapeDtypeStruct((T, 1024), BF16),
                   jax.ShapeDtypeStruct((T, 1536), BF16), jax.ShapeDtypeStruct((T, LANES), F32),
                   jax.ShapeDtypeStruct((T, 2048), BF16)),
        in_specs=[_row_spec(tm, D_MODEL), _const_spec((1, D_MODEL)), _const_spec((D_MODEL, ZQKV_WIDTH)),
                  _const_spec((D_MODEL, LANES)), _const_spec((D_MODEL, 2048))],
        out_specs=[_row_spec(tm, D_MODEL), _row_spec(tm, 1024), _row_spec(tm, 1536), _row_spec(tm, LANES),
                   _row_spec(tm, 2048)],
        compiler_params=_params(48), name="fwd_in",
    )(x2, g1, wz, wf, wg)


def _log_sigmoid(f):
    return jnp.minimum(f, 0.0) - jnp.log1p(jnp.exp(-jnp.abs(f)))


AUG_LANES = 6


def _split3(v):
    hi = v.astype(BF16)
    r1 = v - hi.astype(F32)
    mid = r1.astype(BF16)
    lo = (r1 - mid.astype(F32)).astype(BF16)
    return hi, mid, lo


def _spread(parts, k0):
    n = len(parts)
    r, c = _iota((n * LANES, LANES), 0), _iota((n * LANES, LANES), 1)
    e = jnp.zeros((n * LANES, LANES), BF16)
    for i in range(n):
        h = r - i * LANES
        e = jnp.where((h >= 0) & (h < N_HEADS) & (c == AUG_LANES * h + (k0 + i)), jnp.ones_like(e), e)
    return _dot(jnp.concatenate(parts, axis=1), e)


def _aug_ones(shape, k0):
    lane = _iota(shape, 1)
    head = (lane * 43) >> 8
    slot = lane - AUG_LANES * head
    return ((lane < AUG_LANES * N_HEADS) & (slot >= k0) & (slot < k0 + 3)).astype(F32)


def _aug_query(v):
    return (_spread(_split3(v), 0) + _aug_ones(v.shape, 3)).astype(BF16)


def _aug_key(v):
    return (_aug_ones(v.shape, 0) - _spread(_split3(v), 3)).astype(BF16)


def _aug_stack(t2, aug, p):
    lane = _iota(t2.shape, 1)
    low = lane < 64
    zero = jnp.zeros_like(t2)
    first = 2 * AUG_LANES * p
    a_e = jnp.where((lane >= first) & (lane < first + AUG_LANES), aug, zero)
    a_o = jnp.where((lane >= first + AUG_LANES) & (lane < first + 2 * AUG_LANES), aug, zero)
    top = jnp.concatenate([jnp.where(low, t2, zero), a_e], axis=1)
    bot = jnp.concatenate([jnp.where(low, zero, t2), a_o], axis=1)
    return jnp.concatenate([top, bot], axis=0)


def _fwd_cum(fl, bfp):
    T = fl.shape[0]
    tb = CUM_TILE

    def body(fl_ref, b_ref, cc_ref, qa_ref, ka_ref):
        tri = (_iota((tb, tb), 0) >= _iota((tb, tb), 1)).astype(F32)
        carry = jnp.zeros((1, LANES), F32)
        for i in range(T // tb):
            rows = slice(i * tb, (i + 1) * tb)
            lf = _log_sigmoid(fl_ref[rows, :] + b_ref[...])
            cs = jnp.dot(tri, lf, precision=HIGHEST, preferred_element_type=F32) + carry
            cc_ref[rows, :] = cs
            carry = cs[tb - 1:tb, :]
            qa_ref[rows, :] = _aug_query(cs)
            ka_ref[rows, :] = _aug_key(cs)

    return pl.pallas_call(
        body,
        out_shape=(jax.ShapeDtypeStruct((T, LANES), F32), jax.ShapeDtypeStruct((T, LANES), BF16),
                   jax.ShapeDtypeStruct((T, LANES), BF16)),
        compiler_params=pltpu.CompilerParams(vmem_limit_bytes=32 * MIB), name="fwd_cum",
    )(fl, bfp)


def _sgu_forward_parts(z, gs, bs):
    u = _gelu(z[:, :SGU_WIDTH])
    vv = _gelu(z[:, SGU_WIDTH:])
    vc = vv - jnp.mean(vv, axis=1, keepdims=True)
    rs = lax.rsqrt(jnp.mean(vc * vc, axis=1, keepdims=True) + EPS)
    vhat = vc * rs
    return u, vhat, rs, vhat * gs + bs


def _sgu_pair_weights(w_ref, bT, p):
    tril = _iota((CHUNK, CHUNK), 0) >= _iota((CHUNK, CHUNK), 1)
    we = jnp.where(tril, w_ref[2 * p], 0.0).astype(BF16)
    wo = jnp.where(tril, w_ref[2 * p + 1], 0.0).astype(BF16)
    lane8 = _iota(bT.shape, 1)
    low = _iota((CHUNK, LANES), 1) < 64
    b2 = jnp.where(low, _pick(bT, lane8, 2 * p), _pick(bT, lane8, 2 * p + 1))
    return we, wo, b2


def _chunks_on_lanes(v, p, nc):
    return jnp.concatenate([v[c * CHUNK:(c + 1) * CHUNK, LANES * p:LANES * (p + 1)] for c in range(nc)], axis=1)


def _sgu_mix(we, wo, b2, vcat, nc):
    low = (_iota((CHUNK, nc * LANES), 1) % LANES) < 64
    return jnp.where(low, _dot(we, vcat), _dot(wo, vcat)) + jnp.concatenate([b2] * nc, axis=1)


def _fwd_sgu(zuv, gs, bs, wsp, bT):
    T = zuv.shape[0]
    tc = SGU_TILE
    nc = tc // CHUNK

    def body(z_ref, gs_ref, bs_ref, w_ref, bT_ref, y_ref):
        u, _, _, vln = _sgu_forward_parts(z_ref[...].astype(F32), gs_ref[...], bs_ref[...])
        vb = vln.astype(BF16)
        for p in range(4):
            we, wo, b2 = _sgu_pair_weights(w_ref, bT_ref[...], p)
            s = _sgu_mix(we, wo, b2, _chunks_on_lanes(vb, p, nc), nc)
            for c in range(nc):
                rows, cols = slice(c * CHUNK, (c + 1) * CHUNK), slice(LANES * p, LANES * (p + 1))
                y_ref[rows, cols] = (u[rows, cols] * s[:, c * LANES:(c + 1) * LANES]).astype(BF16)

    return pl.pallas_call(
        body, grid=(T // tc,), out_shape=jax.ShapeDtypeStruct((T, SGU_WIDTH), BF16),
        in_specs=[_row_spec(tc, 1024), _const_spec((1, SGU_WIDTH)), _const_spec((1, SGU_WIDTH)),
                  _const_spec((8, CHUNK, CHUNK)), _const_spec((CHUNK, 8))],
        out_specs=_row_spec(tc, SGU_WIDTH),
        compiler_params=_params(40), name="fwd_sgu",
    )(zuv, gs, bs, wsp, bT)


def _fwd_attn(qkv, qaug, kaug, w_shards):
    T = qkv.shape[0]
    tq = tk = ATTN_TILE
    nq = T // tq
    gather = _WeightGather([w.shape for w in w_shards], ["cols", "cols", "rows", "cols", "rows"])
    nw = gather.n

    def body(q_ref, qa_ref, k_ref, v_ref, ka_ref, *rest):
        w_refs, (o_ref, lse_ref), wg_refs, scratch = rest[:nw], rest[nw:nw + 2], rest[nw + 2:2 * nw + 2], rest[2 * nw + 2:]
        i = pl.program_id(0)

        @pl.when(i == 0)
        def _():
            gather.start(w_refs, wg_refs, scratch)

        @pl.when(i == nq // 2)
        def _():
            gather.forward(wg_refs, scratch)

        lane = _iota((tq, LANES), 1)
        low = lane < 64
        lowk = _iota((tk, LANES), 1) < 64
        one = jnp.ones((tk, LANES), BF16)
        row = _iota((2 * tq, tk), 0) % tq
        col = _iota((2 * tq, tk), 1)
        cols = [slice(LANES * p, LANES * (p + 1)) for p in range(4)]
        qa = qa_ref[...]
        qs = [_aug_stack(q_ref[:, cols[p]], qa, p) for p in range(4)]

        def step(j, carry, masked):
            ks = pl.ds(pl.multiple_of(j * tk, tk), tk)
            ka = ka_ref[ks, :]
            new = []
            for p in range(4):
                m, acc_e, acc_o = carry[p]
                v2 = v_ref[ks, cols[p]]
                s = _dot_nt(qs[p], jnp.concatenate([k_ref[ks, cols[p]], ka], axis=1))
                if masked:
                    s = jnp.where(col <= row, s, NEG)
                mn = jnp.maximum(m, jnp.max(s, axis=1, keepdims=True))
                al = jnp.exp(m - mn)
                pm = jnp.exp(s - mn).astype(BF16)
                acc_e = al[:tq] * acc_e + _dot(pm[:tq], jnp.where(lowk, v2, one))
                acc_o = al[tq:] * acc_o + _dot(pm[tq:], jnp.where(lowk, one, v2))
                new.append((mn, acc_e, acc_o))
            return tuple(new)

        init = tuple((jnp.full((2 * tq, 1), NEG, F32), jnp.zeros((tq, LANES), F32), jnp.zeros((tq, LANES), F32))
                     for _ in range(4))
        def trip(t, c):
            for u in range(3):
                c = step(3 * t + u, c, False)
            return c

        carry = lax.fori_loop(0, i // 3, trip, init)
        carry = lax.fori_loop(3 * (i // 3), i, lambda j, c: step(j, c, False), carry)
        carry = step(i, carry, True)
        lse_blk = jnp.zeros((tq, LANES), F32)
        for p in range(4):
            m, acc_e, acc_o = carry[p]
            l_e = pltpu.roll(acc_e, 64, 1)
            l_o = pltpu.roll(acc_o, 64, 1)
            o_ref[:, cols[p]] = jnp.where(low, acc_e / l_e, acc_o / l_o).astype(BF16)
            lse_blk = jnp.where(lane == 2 * p, m[:tq] + jnp.log(l_e), lse_blk)
            lse_blk = jnp.where(lane == 2 * p + 1, m[tq:] + jnp.log(acc_o), lse_blk)
        lse_ref[...] = lse_blk

        @pl.when(i == nq - 1)
        def _():
            gather.finish(wg_refs, scratch)

    return pl.pallas_call(
        body, grid=(nq,),
        out_shape=[jax.ShapeDtypeStruct((T, ATTN_WIDTH), BF16), jax.ShapeDtypeStruct((T, LANES), F32)]
        + gather.out_shapes(),
        in_specs=[_row_spec(tq, 512), _row_spec(tq, LANES),
                  pl.BlockSpec((T, 512), lambda i: (0, 1), pipeline_mode=pl.Buffered(1)),
                  pl.BlockSpec((T, 512), lambda i: (0, 2), pipeline_mode=pl.Buffered(1)),
                  _const_spec((T, LANES))] + [_const_spec(w.shape) for w in w_shards],
        out_specs=[_row_spec(tq, ATTN_WIDTH), _row_spec(tq, LANES)] + [pl.BlockSpec(memory_space=pl.ANY)] * nw,
        scratch_shapes=gather.scratch_shapes(),
        compiler_params=_params(58), name="fwd_attn",
    )(qkv, qaug, qkv, qkv, kaug, *w_shards)


def _fwd_merge(ys, ya, gt, x2, wbs, wba, wo, g2):
    T = x2.shape[0]
    tm = min(T, MERGE_FWD_TILE)

    def body(ys_ref, ya_ref, gt_ref, x_ref, wbs_ref, wba_ref, wo_ref, g2_ref, a_ref, b_ref, mg_ref, o_ref, h1_ref):
        A = _dot(ys_ref[...], wbs_ref[...])
        B = _dot(ya_ref[...], wba_ref[...])
        mg = (gt_ref[:, :D_MODEL].astype(F32) * A + gt_ref[:, D_MODEL:].astype(F32) * B).astype(BF16)
        o = _dot(mg, wo_ref[...])
        r2, oh = _rms_stats(o)
        a_ref[...] = A.astype(BF16)
        b_ref[...] = B.astype(BF16)
        mg_ref[...] = mg
        o_ref[...] = o.astype(BF16)
        h1_ref[...] = x_ref[...] + oh * g2_ref[...]

    sd = jax.ShapeDtypeStruct((T, D_MODEL), BF16)
    return pl.pallas_call(
        body, grid=(T // tm,),
        out_shape=(sd, sd, sd, sd, jax.ShapeDtypeStruct((T, D_MODEL), F32)),
        in_specs=[_row_spec(tm, 512), _row_spec(tm, 512), _row_spec(tm, 2048), _row_spec(tm, D_MODEL),
                  _const_spec((512, D_MODEL)), _const_spec((512, D_MODEL)), _const_spec((D_MODEL, D_MODEL)),
                  _const_spec((1, D_MODEL))],
        out_specs=[_row_spec(tm, D_MODEL)] * 5,
        compiler_params=_params(58), name="fwd_merge",
    )(ys, ya, gt, x2, wbs, wba, wo, g2)


def _fwd_ffn_loss(h1, tgt, wup, wdn, g3, g4):
    T = h1.shape[0]
    tm = MATMUL_TILE
    nsteps = T // tm

    def body(h1_ref, tg_ref, wup_ref, wdn_ref, g3_ref, g4_ref, xn2_ref, a_ref, ddn_ref, dy_ref, loss_ref,
             dg4_ref, acc_l, acc_g):
        i = pl.program_id(0)

        @pl.when(i == 0)
        def _():
            acc_l[...] = jnp.zeros_like(acc_l)
            acc_g[...] = jnp.zeros_like(acc_g)

        h1v = h1_ref[...]
        r3, h1h = _rms_stats(h1v)
        xn2 = (h1h * g3_ref[...]).astype(BF16)
        xn2_ref[...] = xn2
        dn = jnp.zeros((tm, D_MODEL), F32)
        for j in range(D_FF // 1024):
            cols = slice(1024 * j, 1024 * (j + 1))
            a = _dot(xn2, wup_ref[:, cols])
            a_ref[:, cols] = a.astype(BF16)
            hid = jnp.square(jnp.maximum(a, 0.0)).astype(BF16)
            dn = dn + _dot(hid, wdn_ref[cols, :])
        r4, dnh = _rms_stats(dn)
        g4v = g4_ref[...]
        e = (h1v + dnh * g4v) - tg_ref[...]
        sq = e * e
        s1 = sq[:, 0:LANES]
        for j in range(1, D_MODEL // LANES):
            s1 = s1 + sq[:, LANES * j:LANES * (j + 1)]
        acc_l[...] += _fold8(s1)
        dy = e * (1.0 / D_MODEL)
        dy_ref[...] = dy
        acc_g[...] += _fold8(dy * dnh)
        ddn_ref[...] = _rms_bwd(dy, dnh, r4, g4v).astype(BF16)

        @pl.when(i == nsteps - 1)
        def _():
            loss_ref[...] = acc_l[...] * (0.5 / D_MODEL)
            dg4_ref[...] = jnp.sum(acc_g[...], axis=0, keepdims=True)

    return pl.pallas_call(
        body, grid=(nsteps,),
        out_shape=(jax.ShapeDtypeStruct((T, D_MODEL), BF16), jax.ShapeDtypeStruct((T, D_FF), BF16),
                   jax.ShapeDtypeStruct((T, D_MODEL), BF16), jax.ShapeDtypeStruct((T, D_MODEL), F32),
                   jax.ShapeDtypeStruct((SUBLANES, LANES), F32), jax.ShapeDtypeStruct((1, D_MODEL), F32)),
        in_specs=[_row_spec(tm, D_MODEL), _row_spec(tm, D_MODEL), _const_spec((D_MODEL, D_FF)),
                  _const_spec((D_FF, D_MODEL)), _const_spec((1, D_MODEL)), _const_spec((1, D_MODEL))],
        out_specs=[_row_spec(tm, D_MODEL), _row_spec(tm, D_FF), _row_spec(tm, D_MODEL), _row_spec(tm, D_MODEL),
                   pl.BlockSpec((SUBLANES, LANES), lambda i: (0, 0)), pl.BlockSpec((1, D_MODEL), lambda i: (0, 0))],
        scratch_shapes=[pltpu.VMEM((SUBLANES, LANES), F32), pltpu.VMEM((SUBLANES, D_MODEL), F32)],
        compiler_params=_params(52), name="fwd_ffn_loss",
    )(h1, tgt, wup, wdn, g3, g4)


def _bwd_ffn(ddn, a, dy, h1, wup, wdn, g3):
    T = h1.shape[0]
    tm = MATMUL_TILE
    nsteps = T // tm

    def body(ddn_ref, a_ref, dy_ref, h1_ref, wup_ref, wdn_ref, g3_ref, da_ref, dh1_ref, dg3_ref, acc_g):
        i = pl.program_id(0)

        @pl.when(i == 0)
        def _():
            acc_g[...] = jnp.zeros_like(acc_g)

        ddnv = ddn_ref[...]
        dxn2 = jnp.zeros((tm, D_MODEL), F32)
        for j in range(D_FF // 1024):
            cols = slice(1024 * j, 1024 * (j + 1))
            dhid = _dot_nt(ddnv, wdn_ref[cols, :])
            da = (dhid * (2.0 * jnp.maximum(a_ref[:, cols].astype(F32), 0.0))).astype(BF16)
            da_ref[:, cols] = da
            dxn2 = dxn2 + _dot_nt(da, wup_ref[:, cols])
        r3, h1h = _rms_stats(h1_ref[...])
        acc_g[...] += _fold8(dxn2 * h1h)
        dh1_ref[...] = dy_ref[...] + _rms_bwd(dxn2, h1h, r3, g3_ref[...])

        @pl.when(i == nsteps - 1)
        def _():
            dg3_ref[...] = jnp.sum(acc_g[...], axis=0, keepdims=True)

    return pl.pallas_call(
        body, grid=(nsteps,),
        out_shape=(jax.ShapeDtypeStruct((T, D_FF), BF16), jax.ShapeDtypeStruct((T, D_MODEL), F32),
                   jax.ShapeDtypeStruct((1, D_MODEL), F32)),
        in_specs=[_row_spec(tm, D_MODEL), _row_spec(tm, D_FF), _row_spec(tm, D_MODEL), _row_spec(tm, D_MODEL),
                  _const_spec((D_MODEL, D_FF)), _const_spec((D_FF, D_MODEL)), _const_spec((1, D_MODEL))],
        out_specs=[_row_spec(tm, D_FF), _row_spec(tm, D_MODEL), pl.BlockSpec((1, D_MODEL), lambda i: (0, 0))],
        scratch_shapes=[pltpu.VMEM((SUBLANES, D_MODEL), F32)],
        compiler_params=_params(52), name="bwd_ffn",
    )(ddn, a, dy, h1, wup, wdn, g3)


def _wgrad(xa, dy, name, relu2=False, tn=None, block_cols=None):
    T, K = xa.shape
    N = dy.shape[1]
    tn = N if tn is None else tn
    tt = min(T, WGRAD_TILE if K <= D_MODEL else WGRAD_TILE // 2)
    if block_cols:
        nb = tn // block_cols
        out_shape = jax.ShapeDtypeStruct((N // block_cols, K, block_cols), F32)
        out_spec = pl.BlockSpec((nb, K, block_cols), lambda n, t: (n, 0, 0))
    else:
        out_shape = jax.ShapeDtypeStruct((K, N), F32)
        out_spec = pl.BlockSpec((K, tn), lambda n, t: (0, n))

    def body(x_ref, dy_ref, o_ref):
        @pl.when(pl.program_id(1) == 0)
        def _():
            o_ref[...] = jnp.zeros_like(o_ref)

        xv = x_ref[...]
        if relu2:
            xv = jnp.square(jnp.maximum(xv.astype(F32), 0.0)).astype(BF16)
        if block_cols:
            for b in range(nb):
                o_ref[b] += _dot_tn(xv, dy_ref[:, block_cols * b:block_cols * (b + 1)])
        else:
            o_ref[...] += _dot_tn(xv, dy_ref[...])

    return pl.pallas_call(
        body, grid=(N // tn, T // tt), out_shape=out_shape,
        in_specs=[pl.BlockSpec((tt, K), lambda n, t: (t, 0)), pl.BlockSpec((tt, tn), lambda n, t: (t, n))],
        out_specs=out_spec,
        compiler_params=_params(52, 2), name=name,
    )(xa, dy)


def _wgrad_multi(xa, dys, name):
    T, K = xa.shape
    tt = min(T, WGRAD_TILE)
    n = len(dys)

    def body(x_ref, *refs):
        dy_refs, o_refs = refs[:n], refs[n:]

        @pl.when(pl.program_id(0) == 0)
        def _():
            for o_ref in o_refs:
                o_ref[...] = jnp.zeros_like(o_ref)

        xv = x_ref[...]
        for dy_ref, o_ref in zip(dy_refs, o_refs):
            o_ref[...] += _dot_tn(xv, dy_ref[...])

    return pl.pallas_call(
        body, grid=(T // tt,),
        out_shape=[jax.ShapeDtypeStruct((K, dy.shape[1]), F32) for dy in dys],
        in_specs=[_row_spec(tt, K)] + [_row_spec(tt, dy.shape[1]) for dy in dys],
        out_specs=[pl.BlockSpec((K, dy.shape[1]), lambda t: (0, 0)) for dy in dys],
        compiler_params=_params(52), name=name,
    )(xa, *dys)


def _bwd_merge(dh1, o, A, B, gt, ys, ya, mg, lse, cc, wbs, wba, wo, g2):
    T = dh1.shape[0]
    tm = MERGE_BWD_TILE
    nsteps = T // tm

    def body(dh1_ref, o_ref, a_ref, b_ref, gt_ref, ys_ref, ya_ref, mg_ref, lse_ref, cc_ref, wbs_ref, wba_ref,
             wo_ref, g2_ref, dgl_ref, dys_ref, dya_ref, qab_ref, dab_ref, dg2_ref, dwbs_ref, dwba_ref, dwo_ref,
             acc_g):
        i = pl.program_id(0)

        @pl.when(i == 0)
        def _():
            acc_g[...] = jnp.zeros_like(acc_g)
            dwbs_ref[...] = jnp.zeros_like(dwbs_ref)
            dwba_ref[...] = jnp.zeros_like(dwba_ref)
            dwo_ref[...] = jnp.zeros_like(dwo_ref)

        dh1v = dh1_ref[...]
        r2, oh = _rms_stats(o_ref[...].astype(F32))
        acc_g[...] += _fold8(dh1v * oh)
        do = _rms_bwd(dh1v, oh, r2, g2_ref[...]).astype(BF16)
        dwo_ref[...] += _dot_tn(mg_ref[...], do)
        dmg = _dot_nt(do, wo_ref[...])
        ga = gt_ref[:, :D_MODEL].astype(F32)
        gb = gt_ref[:, D_MODEL:].astype(F32)
        dgl_ref[:, :D_MODEL] = (dmg * a_ref[...].astype(F32) * ga * (1.0 - ga)).astype(BF16)
        dgl_ref[:, D_MODEL:] = (dmg * b_ref[...].astype(F32) * gb * (1.0 - gb)).astype(BF16)
        dA = (dmg * ga).astype(BF16)
        dB = (dmg * gb).astype(BF16)
        dwbs_ref[...] += _dot_tn(ys_ref[...], dA)
        dwba_ref[...] += _dot_tn(ya_ref[...], dB)
        dys_ref[...] = _dot_nt(dA, wbs_ref[...]).astype(BF16)
        dya = _dot_nt(dB, wba_ref[...]).astype(BF16)
        dya_ref[...] = dya
        prod = dya.astype(F32) * ya_ref[...].astype(F32)
        lane = _iota((tm, LANES), 1)
        low = lane < 64
        blk = jnp.zeros((tm, LANES), F32)
        for p in range(4):
            pp = prod[:, LANES * p:LANES * (p + 1)]
            blk = jnp.where(lane == 2 * p, jnp.sum(jnp.where(low, pp, 0.0), axis=1, keepdims=True), blk)
            blk = jnp.where(lane == 2 * p + 1, jnp.sum(jnp.where(low, 0.0, pp), axis=1, keepdims=True), blk)
        qab_ref[...] = _aug_query(cc_ref[...] - lse_ref[...])
        dab_ref[...] = _spread(_split3(-blk), 0).astype(BF16)

        @pl.when(i == nsteps - 1)
        def _():
            dg2_ref[...] = jnp.sum(acc_g[...], axis=0, keepdims=True)

    sh = jax.ShapeDtypeStruct((T, 512), BF16)
    sa = jax.ShapeDtypeStruct((T, LANES), BF16)
    sw = jax.ShapeDtypeStruct((512, D_MODEL), F32)
    whole = lambda shape: pl.BlockSpec(shape, lambda i: (0, 0))
    return pl.pallas_call(
        body, grid=(nsteps,),
        out_shape=(jax.ShapeDtypeStruct((T, 2048), BF16), sh, sh, sa, sa, jax.ShapeDtypeStruct((1, D_MODEL), F32),
                   sw, sw, jax.ShapeDtypeStruct((D_MODEL, D_MODEL), F32)),
        in_specs=[_row_spec(tm, D_MODEL)] * 4 + [_row_spec(tm, 2048), _row_spec(tm, 512), _row_spec(tm, 512),
                  _row_spec(tm, D_MODEL), _row_spec(tm, LANES), _row_spec(tm, LANES),
                  _const_spec((512, D_MODEL)), _const_spec((512, D_MODEL)),
                  _const_spec((D_MODEL, D_MODEL)), _const_spec((1, D_MODEL))],
        out_specs=[_row_spec(tm, 2048), _row_spec(tm, 512), _row_spec(tm, 512), _row_spec(tm, LANES),
                   _row_spec(tm, LANES), whole((1, D_MODEL)), whole((512, D_MODEL)), whole((512, D_MODEL)),
                   whole((D_MODEL, D_MODEL))],
        scratch_shapes=[pltpu.VMEM((SUBLANES, D_MODEL), F32)],
        compiler_params=_params(56), name="bwd_merge",
    )(dh1, o, A, B, gt, ys, ya, mg, lse, cc, wbs, wba, wo, g2)


def _bwd_sgu(zuv, dys, gs, bs, wsp, bT, grads):
    T = zuv.shape[0]
    tc = SGU_TILE
    nc = tc // CHUNK
    nsteps = T // tc
    ex = _GradExchange([tuple(g.shape[1:]) for g in grads])
    ng = ex.n

    def body(z_ref, dy_ref, gs_ref, bs_ref, w_ref, bT_ref, *rest):
        g_refs, (dz_ref, dw_ref, dbT_ref, dgs_ref, dbs_ref) = rest[:ng], rest[ng:ng + 5]
        land1 = rest[ng + 5:2 * ng + 5]
        acc_w, acc_b, acc_gs, acc_bs, dvln_s = rest[2 * ng + 5:2 * ng + 10]
        ex_sems = rest[2 * ng + 10:]
        i = pl.program_id(0)

        @pl.when(i == 0)
        def _():
            ex.start(1, g_refs, land1, ex_sems)
            acc_w[...] = jnp.zeros_like(acc_w)
            acc_b[...] = jnp.zeros_like(acc_b)
            acc_gs[...] = jnp.zeros_like(acc_gs)
            acc_bs[...] = jnp.zeros_like(acc_bs)

        z = z_ref[...].astype(F32)
        gsv = gs_ref[...]
        u, vhat, rs, vln = _sgu_forward_parts(z, gsv, bs_ref[...])
        vb = vln.astype(BF16)
        dy = dy_ref[...].astype(F32)
        low_w = (_iota((CHUNK, nc * LANES), 1) % LANES) < 64
        for p in range(4):
            we, wo, b2 = _sgu_pair_weights(w_ref, bT_ref[...], p)
            vcat = _chunks_on_lanes(vb, p, nc)
            s = _sgu_mix(we, wo, b2, vcat, nc)
            dyc = _chunks_on_lanes(dy, p, nc)
            ds = dyc * _chunks_on_lanes(u, p, nc)
            dsb = ds.astype(BF16)
            zero = jnp.zeros_like(dsb)
            dse = jnp.where(low_w, dsb, zero)
            dso = jnp.where(low_w, zero, dsb)
            acc_w[2 * p] += _dot_nt(dse, vcat)
            acc_w[2 * p + 1] += _dot_nt(dso, vcat)
            acc_b[p] += ds
            dvl = jnp.where(low_w, _dot_tn(we, dsb), _dot_tn(wo, dsb))
            for c in range(nc):
                rows, cols = slice(c * CHUNK, (c + 1) * CHUNK), slice(LANES * p, LANES * (p + 1))
                dvln_s[rows, cols] = dvl[:, c * LANES:(c + 1) * LANES]
                du = dy[rows, cols] * s[:, c * LANES:(c + 1) * LANES]
                dz_ref[rows, cols] = (du * _gelu_grad(z[rows, cols])).astype(BF16)
        dvln = dvln_s[...]
        acc_gs[...] += _fold8(dvln * vhat)
        acc_bs[...] += _fold8(dvln)
        al = dvln * gsv
        dvv = rs * (al - jnp.mean(al, axis=1, keepdims=True) - vhat * jnp.mean(al * vhat, axis=1, keepdims=True))
        dz_ref[:, SGU_WIDTH:] = (dvv * _gelu_grad(z[:, SGU_WIDTH:])).astype(BF16)

        @pl.when(i == nsteps - 1)
        def _():
            tril = _iota((CHUNK, CHUNK), 0) >= _iota((CHUNK, CHUNK), 1)
            lane = _iota((CHUNK, LANES), 1)
            low = lane < 64
            blk = jnp.zeros((CHUNK, LANES), F32)
            for g in range(8):
                dw_ref[g] = jnp.where(tril, acc_w[g], 0.0)
            for p in range(4):
                t = acc_b[p]
                tot = t[:, 0:LANES]
                for c in range(1, nc):
                    tot = tot + t[:, c * LANES:(c + 1) * LANES]
                blk = jnp.where(lane == 2 * p, jnp.sum(jnp.where(low, tot, 0.0), axis=1, keepdims=True), blk)
                blk = jnp.where(lane == 2 * p + 1, jnp.sum(jnp.where(low, 0.0, tot), axis=1, keepdims=True), blk)
            dbT_ref[...] = blk
            dgs_ref[...] = jnp.sum(acc_gs[...], axis=0, keepdims=True)
            dbs_ref[...] = jnp.sum(acc_bs[...], axis=0, keepdims=True)
            ex.wait(1, g_refs, land1, ex_sems)

    whole = lambda shape: pl.BlockSpec(shape, lambda i: (0,) * len(shape))
    hbm_spec = pl.BlockSpec(memory_space=pl.ANY)
    return pl.pallas_call(
        body, grid=(nsteps,),
        out_shape=[jax.ShapeDtypeStruct((T, 1024), BF16), jax.ShapeDtypeStruct((8, CHUNK, CHUNK), F32),
                   jax.ShapeDtypeStruct((CHUNK, LANES), F32), jax.ShapeDtypeStruct((1, SGU_WIDTH), F32),
                   jax.ShapeDtypeStruct((1, SGU_WIDTH), F32)] + ex.land_shapes(1),
        in_specs=[_row_spec(tc, 1024), _row_spec(tc, SGU_WIDTH), _const_spec((1, SGU_WIDTH)),
                  _const_spec((1, SGU_WIDTH)), _const_spec((8, CHUNK, CHUNK)), _const_spec((CHUNK, 8))]
        + [hbm_spec] * ng,
        out_specs=[_row_spec(tc, 1024), whole((8, CHUNK, CHUNK)), whole((CHUNK, LANES)),
                   whole((1, SGU_WIDTH)), whole((1, SGU_WIDTH))] + [hbm_spec] * ng,
        scratch_shapes=[pltpu.VMEM((8, CHUNK, CHUNK), F32), pltpu.VMEM((4, CHUNK, nc * LANES), F32),
                        pltpu.VMEM((SUBLANES, SGU_WIDTH), F32), pltpu.VMEM((SUBLANES, SGU_WIDTH), F32),
                        pltpu.VMEM((tc, SGU_WIDTH), F32)] + ex.sem_shapes(1),
        compiler_params=_params(48), name="bwd_sgu",
    )(zuv, dys, gs, bs, wsp, bT, *grads)


def _bwd_attn(qkv, dya, qab, dab, kaug, parts):
    T = qkv.shape[0]
    tq = tk = ATTN_TILE
    nq = T // tq
    nk = T // tk
    ex = _GradExchange([tuple(g.shape[1:]) for g in parts])
    nr = ex.n

    def body(q_ref, do_ref, qa_ref, da_ref, k_ref, v_ref, ka_ref, *rest):
        part_refs, (dq_ref, dk_ref, dv_ref, dcx_ref) = rest[:nr], rest[nr:nr + 4]
        land2, dq_acc, ex_sems = rest[nr + 4:2 * nr + 4], rest[2 * nr + 4], rest[2 * nr + 5:]
        p = pl.program_id(0)
        j = pl.program_id(1)

        @pl.when((p == 0) & (j == 0))
        def _():
            ex.start(2, part_refs, land2, ex_sems)

        lane = _iota((tq, LANES), 1)
        low = lane < 64
        row = _iota((2 * tq, tk), 0) % tq
        col = _iota((2 * tq, tk), 1)
        first = 2 * AUG_LANES * p
        half = tq // 2

        @pl.when(j == 0)
        def _():
            dq_acc[...] = jnp.zeros_like(dq_acc)

        @pl.when((j == 0) & (p == 0))
        def _():
            dcx_ref[...] = jnp.zeros_like(dcx_ref)

        ka = ka_ref[...]
        kk = jnp.concatenate([k_ref[...], ka], axis=1)
        vv = jnp.concatenate([v_ref[...], ka], axis=1)

        def tile(i, carry, r0, c0, nc, masked):
            dk_a, dv_a = carry
            nr = tq - r0
            qsl = pl.ds(pl.multiple_of(i * tq + r0, half), nr)
            qs = _aug_stack(q_ref[qsl, :], qa_ref[qsl, :], p)
            dos = _aug_stack(do_ref[qsl, :], da_ref[qsl, :], p)
            kc, vc = kk[c0:c0 + nc], vv[c0:c0 + nc]
            s = _dot_nt(qs, kc)
            if masked:
                s = jnp.where(col[:2 * nr, :nc] + c0 <= row[:2 * nr, :nc] % nr + r0, s, NEG)
            pm = jnp.exp(s)
            ds = pm * _dot_nt(dos, vc)
            dsb = ds.astype(BF16)
            dv_u = _dot_tn(pm.astype(BF16), dos[:, :LANES])
            dk_u = _dot_tn(dsb, qs)
            if nc == tk:
                dv_a, dk_a = dv_a + dv_u, dk_a + dk_u
            else:
                pad = lambda u: jnp.concatenate(
                    [jnp.zeros((n, u.shape[1]), F32) if z else u
                     for z, n in ((True, c0), (False, nc), (True, tk - c0 - nc)) if n], axis=0)
                dv_a, dk_a = dv_a + pad(dv_u), dk_a + pad(dk_u)
            dqx = _dot(dsb, kc)
            dq_acc[qsl, :] += jnp.where(low[:nr], dqx[:nr, :LANES], dqx[nr:, :LANES])
            dcx_ref[qsl, :] += (jnp.where(lane[:nr] == first, dqx[:nr, LANES:], 0.0)
                                + jnp.where(lane[:nr] == first + AUG_LANES, dqx[nr:, LANES:], 0.0))
            return dk_a, dv_a

        def q_block(i, carry, masked):
            return tile(i, carry, 0, 0, tk, masked)

        init =(jnp.zeros((tk, 2 * LANES), F32), jnp.zeros((tk, LANES), F32))
        carry = tile(j, init, 0, 0, half, True)
        carry = tile(j, carry, half, half, half, True)
        n_rest = nq - 1 - j

        def trip(t, c):
            for u in range(3):
                c = q_block(j + 1 + 3 * t + u, c, False)
            return c

        carry = lax.fori_loop(0, n_rest // 3, trip, carry)
        dk_a, dv_a = lax.fori_loop(j + 1 + 3 * (n_rest // 3), nq, lambda i, c: q_block(i, c, False), carry)
        dk_ref[...] = dk_a[:, :LANES].astype(BF16)
        dv_ref[...] = dv_a.astype(BF16)
        ksl = pl.ds(pl.multiple_of(j * tk, tk), tk)
        lk = _iota((tk, LANES), 1)
        dcx_ref[ksl, :] += jnp.where((lk == first + 3) | (lk == first + AUG_LANES + 3), dk_a[:, LANES:], 0.0)

        @pl.when(j == nk - 1)
        def _():
            dq_ref[...] = (dq_acc[...] * 0.125).astype(BF16)

        @pl.when((p == 3) & (j == nk - 1))
        def _():
            ex.wait(2, part_refs, land2, ex_sems)

    sh = jax.ShapeDtypeStruct((T, ATTN_WIDTH), BF16)
    full = lambda cb: pl.BlockSpec((T, LANES), lambda p, j: (0, cb + p))
    blk = lambda cb: pl.BlockSpec((tk, LANES), lambda p, j: (j, cb + p))
    hbm_spec = pl.BlockSpec(memory_space=pl.ANY)
    return pl.pallas_call(
        body, grid=(4, nk),
        out_shape=[sh, sh, sh, jax.ShapeDtypeStruct((T, LANES), F32)] + ex.land_shapes(2),
        in_specs=[full(0), full(0), _const_spec((T, LANES)), _const_spec((T, LANES)), blk(4), blk(8),
                  pl.BlockSpec((tk, LANES), lambda p, j: (j, 0))] + [hbm_spec] * nr,
        out_specs=[full(0), blk(0), blk(0), pl.BlockSpec((T, LANES), lambda p, j: (0, 0))] + [hbm_spec] * nr,
        scratch_shapes=[pltpu.VMEM((T, LANES), F32)] + ex.sem_shapes(2),
        compiler_params=_params(58, 2), name="bwd_attn",
    )(qkv, dya, qab, dab, qkv, qkv, kaug, *parts)


def _bwd_cum(dcx, fl, bfp):
    T = fl.shape[0]
    tb = CUM_TILE

    def body(dcx_ref, fl_ref, b_ref, dfl_ref, dbf_ref):
        triu = (_iota((tb, tb), 0) <= _iota((tb, tb), 1)).astype(F32)
        r, c = _iota((LANES, LANES), 0), _iota((LANES, LANES), 1)
        sel = (((r == AUG_LANES * c) & (c < N_HEADS)).astype(F32)
               - ((r == AUG_LANES * c + 3) & (c < N_HEADS)).astype(F32))
        carry = jnp.zeros((1, LANES), F32)
        dbf = jnp.zeros((1, LANES), F32)
        for i in reversed(range(T // tb)):
            colblk = jnp.dot(dcx_ref[i * tb:(i + 1) * tb, :], sel, precision=HIGHEST, preferred_element_type=F32)
            rc = jnp.dot(triu, colblk, precision=HIGHEST, preferred_element_type=F32) + carry
            carry = rc[0:1, :]
            sig = jax.nn.sigmoid(fl_ref[i * tb:(i + 1) * tb, :] + b_ref[...])
            dfl = rc * (1.0 - sig)
            dfl_ref[i * tb:(i + 1) * tb, :] = dfl.astype(BF16)
            dbf = dbf + jnp.sum(dfl, axis=0, keepdims=True)
        dbf_ref[...] = dbf

    return pl.pallas_call(
        body,
        out_shape=(jax.ShapeDtypeStruct((T, LANES), BF16), jax.ShapeDtypeStruct((1, LANES), F32)),
        compiler_params=pltpu.CompilerParams(vmem_limit_bytes=32 * MIB), name="bwd_cum",
    )(dcx, fl, bfp)


def _bwd_in(dz, dq, dk, dv, dfl, dgl, dh1, x2, g1, wz, wf, wg, rows, name, prev=None, stage=0, exchanged=None):
    T = x2.shape[0]
    tm = math.gcd(MATMUL_TILE, rows[0], rows[1] - rows[0])
    first = rows[0] // tm
    nsteps = (rows[1] - rows[0]) // tm
    ex = _GradExchange([tuple(exchanged.shape[1:])]) if stage else None

    def body(dz_ref, dq_ref, dk_ref, dv_ref, dfl_ref, dgl_ref, dh1_ref, x_ref, g_ref, wz_ref, wf_ref, wg_ref, *rest):
        rest = list(rest)
        dx_prev, dg1_prev = (rest.pop(0), rest.pop(0)) if prev else (None, None)
        src_ref = rest.pop(0) if stage else None
        dx_ref, dg1_ref = rest.pop(0), rest.pop(0)
        land_ref = rest.pop(0) if stage else None
        acc_g, ex_sems = rest[0], rest[1:]
        i = pl.program_id(0)

        @pl.when(i == 0)
        def _():
            if stage:
                ex.start(stage, [src_ref], [land_ref], ex_sems)
            acc_g[...] = jnp.zeros_like(acc_g)

        dxn = _dot_nt(dz_ref[...], wz_ref[:, 0:1024])
        dxn = dxn + _dot_nt(dq_ref[...], wz_ref[:, 1024:1536])
        dxn = dxn + _dot_nt(dk_ref[...], wz_ref[:, 1536:2048])
        dxn = dxn + _dot_nt(dv_ref[...], wz_ref[:, 2048:2560])
        dxn = dxn + _dot_nt(dfl_ref[...], wf_ref[...])
        dxn = dxn + _dot_nt(dgl_ref[...], wg_ref[...])
        r1, xh = _rms_stats(x_ref[...])
        acc_g[...] += _fold8(dxn * xh)
        dx_ref[...] = dh1_ref[...] + _rms_bwd(dxn, xh, r1, g_ref[...])

        @pl.when(i == nsteps - 1)
        def _():
            total = jnp.sum(acc_g[...], axis=0, keepdims=True)
            dg1_ref[...] = total + dg1_prev[...] if prev else total
            if stage:
                ex.wait(stage, [src_ref], [land_ref], ex_sems)

    hbm_spec = pl.BlockSpec(memory_space=pl.ANY)
    rows_spec = lambda n: pl.BlockSpec((tm, n), lambda i: (i + first, 0))
    operands = [dz, dq, dk, dv, dfl, dgl, dh1, x2, g1, wz, wf, wg]
    in_specs = [rows_spec(1024), rows_spec(512), rows_spec(512), rows_spec(512), rows_spec(LANES), rows_spec(2048),
                rows_spec(D_MODEL), rows_spec(D_MODEL),
                _const_spec((1, D_MODEL)), _const_spec((D_MODEL, ZQKV_WIDTH)), _const_spec((D_MODEL, LANES)),
                _const_spec((D_MODEL, 2048))]
    aliases = {}
    if prev:
        aliases = {len(operands): 0}
        operands += list(prev)
        in_specs += [hbm_spec, _const_spec((1, D_MODEL))]
    if stage:
        operands.append(exchanged)
        in_specs.append(hbm_spec)
    return pl.pallas_call(
        body, grid=(nsteps,),
        out_shape=[jax.ShapeDtypeStruct((T, D_MODEL), F32), jax.ShapeDtypeStruct((1, D_MODEL), F32)]
        + (ex.land_shapes(stage) if stage else []),
        in_specs=in_specs,
        out_specs=[rows_spec(D_MODEL), pl.BlockSpec((1, D_MODEL), lambda i: (0, 0))] + ([hbm_spec] if stage else []),
        scratch_shapes=[pltpu.VMEM((SUBLANES, D_MODEL), F32)] + (ex.sem_shapes(stage) if stage else []),
        input_output_aliases=aliases,
        compiler_params=_params(48), name=name,
    )(*operands)


def _small_kernel_shapes(g_mix_pre, b_forget, g_sgu, b_sgu, w_spatial, b_spatial, g_mix_post, g_ffn_pre, g_ffn_post):
    return dict(g_mix_pre=g_mix_pre, b_forget=jnp.pad(b_forget, ((0, 0), (0, LANES - N_HEADS))), g_sgu=g_sgu,
                b_sgu=b_sgu, w_spatial=w_spatial[0], b_spatial=b_spatial[0], g_mix_post=g_mix_post,
                g_ffn_pre=g_ffn_pre, g_ffn_post=g_ffn_post)


def _small_output_shapes(d):
    out = dict(d)
    out.update(b_forget=d["b_forget"][:, :N_HEADS], w_spatial=d["w_spatial"][None], b_spatial=d["b_spatial"][None])
    return out


def kernel(x, g_mix_pre, w_in, b_forget, g_sgu, b_sgu, w_spatial, b_spatial, w_branch_sgu, w_branch_attn, w_out, g_mix_post, g_ffn_pre, w_up, w_down, g_ffn_post, loss_target, m_g_mix_pre, m_w_in, m_b_forget, m_g_sgu, m_b_sgu, m_w_spatial, m_b_spatial, m_w_branch_sgu, m_w_branch_attn, m_w_out, m_g_mix_post, m_g_ffn_pre, m_w_up, m_w_down, m_g_ffn_post, v_g_mix_pre, v_w_in, v_b_forget, v_g_sgu, v_b_sgu, v_w_spatial, v_b_spatial, v_w_branch_sgu, v_w_branch_attn, v_w_out, v_g_mix_post, v_g_ffn_pre, v_w_up, v_w_down, v_g_ffn_post):
    T = x.shape[1]
    x2 = x.reshape(T, D_MODEL)
    tgt = loss_target.reshape(T, D_MODEL)

    wg_in = _gather_w_in(w_in[0])
    wz, wf, wgt = _assemble_w_in(wg_in)
    bfp = jnp.pad(b_forget, ((0, 0), (0, LANES - N_HEADS)))
    wsp = w_spatial[0]
    bT = b_spatial[0].T

    xn, zuv, qkv, fl, gt = _fwd_in(x2, g_mix_pre, wz, wf, wgt)
    cc, qaug, kaug = _fwd_cum(fl, bfp)
    ys = _fwd_sgu(zuv, g_sgu, b_sgu, wsp, bT)
    ya, lse, wbs, wba, wo, wup, wdn = _fwd_attn(
        qkv, qaug, kaug, (w_branch_sgu[0], w_branch_attn[0], w_out[0], w_up[0], w_down[0]))
    A, B, mg, o, h1 = _fwd_merge(ys, ya, gt, x2, wbs, wba, wo, g_mix_post)
    xn2, a, ddn, dy, loss_part, dg4 = _fwd_ffn_loss(h1, tgt, wup, wdn, g_ffn_pre, g_ffn_post)

    da, dh1, dg3 = _bwd_ffn(ddn, a, dy, h1, wup, wdn, g_ffn_pre)
    dw_up = _wgrad(xn2, da, "wgrad_up", tn=2048, block_cols=512)
    dw_down = _wgrad(a, ddn, "wgrad_down", relu2=True)
    dgl, dys, dya, qab, dab, dg2, dw_bs, dw_ba, dw_out = _bwd_merge(
        dh1, o, A, B, gt, ys, ya, mg, lse, cc, wbs, wba, wo, g_mix_post)
    col_blocks = lambda g, w: g.reshape(g.shape[0], N_DEV, w).transpose(1, 0, 2)
    row_blocks = lambda g, r: g.reshape(N_DEV, r, g.shape[1])
    early_names = ["w_branch_sgu", "w_branch_attn", "w_out", "w_up", "w_down"]
    early = [col_blocks(dw_bs, 128), col_blocks(dw_ba, 128), row_blocks(dw_out, 128), dw_up, row_blocks(dw_down, 512)]
    owners = _owner_indices()
    dzuv, dwsp, dbT, dgs, dbs, *early_land1 = _bwd_sgu(zuv, dys, g_sgu, b_sgu, wsp, bT, early)
    early_parts = [_chip_partials(g, l1, owners, "chip_partials_" + nm)
                   for g, l1, nm in zip(early, early_land1, early_names)]
    dq, dk, dv, dcx, *early_land2 = _bwd_attn(qkv, dya, qab, dab, kaug, early_parts)
    dfl, dbf = _bwd_cum(dcx, fl, bfp)
    dw_z = _wgrad(xn, dzuv, "wgrad_in_z")
    dw_q, dw_k, dw_v, dw_f = _wgrad_multi(xn, [dq, dk, dv, dfl], "wgrad_in_qkvf")
    dw_g = _wgrad(xn, dgl, "wgrad_in_gate")
    blocks_in = _block_dw_in([dw_z, dw_q, dw_k, dw_v, dw_f, dw_g])
    bwd_in_args = (dzuv, dq, dk, dv, dfl, dgl, dh1, x2, g_mix_pre, wz, wf, wgt)
    dx, dg1, land1_in = _bwd_in(*bwd_in_args, (0, T // 8), "bwd_in_a", stage=1, exchanged=blocks_in)
    part_in = _chip_partials(blocks_in, land1_in, owners, "chip_partials_w_in")
    dx, dg1, land2_in = _bwd_in(*bwd_in_args, (T // 8, 3 * T // 4), "bwd_in_b", prev=(dx, dg1), stage=2,
                                exchanged=part_in)
    dx, dg1 = _bwd_in(*bwd_in_args, (3 * T // 4, T), "bwd_in_c", prev=(dx, dg1))

    tot_a, tot_b = _allreduce_small(dict(
        g_mix_pre=dg1, b_forget=dbf, g_sgu=dgs, b_sgu=dbs, w_spatial=dwsp, b_spatial=dbT, g_mix_post=dg2,
        g_ffn_pre=dg3, g_ffn_post=dg4), loss_part)
    r0, nr, c0, nc = LOSS_SLOT
    loss = jnp.sum(tot_a[r0:r0 + nr, c0:c0 + nc])
    small_w = _small_kernel_shapes(g_mix_pre, b_forget, g_sgu, b_sgu, w_spatial, b_spatial, g_mix_post, g_ffn_pre,
                                   g_ffn_post)
    small_m = _small_kernel_shapes(m_g_mix_pre, m_b_forget, m_g_sgu, m_b_sgu, m_w_spatial, m_b_spatial, m_g_mix_post,
                                   m_g_ffn_pre, m_g_ffn_post)
    small_v = _small_kernel_shapes(v_g_mix_pre, v_b_forget, v_g_sgu, v_b_sgu, v_w_spatial, v_b_spatial, v_g_mix_post,
                                   v_g_ffn_pre, v_g_ffn_post)
    sg, sd, sm, sv = (_small_output_shapes(d) for d in _adamw_small(tot_a, tot_b, small_w, small_m, small_v))

    big = {}
    g_in = _reduced_grad(blocks_in, land1_in, land2_in, owners, "reduced_grad_w_in")[:, :IN_SHARD]
    d_, m_, v_ = _adamw(w_in[0], g_in, m_w_in[0], v_w_in[0], "adamw_w_in")
    big["w_in"] = (g_in[None], d_[None], m_[None], v_[None])
    early_wmv = [(w_branch_sgu, m_w_branch_sgu, v_w_branch_sgu), (w_branch_attn, m_w_branch_attn, v_w_branch_attn),
                 (w_out, m_w_out, v_w_out), (w_up, m_w_up, v_w_up), (w_down, m_w_down, v_w_down)]
    for nm, (w, m, v), g, l1, l2 in zip(early_names, early_wmv, early, early_land1, early_land2):
        big[nm] = tuple(t[None] for t in _adamw_reduced(w[0], m[0], v[0], g, l1, l2, owners, "adamw_" + nm))

    order = ["g_mix_pre", "w_in", "b_forget", "g_sgu", "b_sgu", "w_spatial", "b_spatial", "w_branch_sgu",
             "w_branch_attn", "w_out", "g_mix_post", "g_ffn_pre", "w_up", "w_down", "g_ffn_post"]
    outs = [loss, dx.reshape(1, T, D_MODEL)]
    for kind, small in enumerate((sg, sd, sm, sv)):
        outs += [big[nm][kind] if nm in big else small[nm] for nm in order]
    return tuple(outs)
```

```python
import math

import jax
import jax.numpy as jnp
from jax import lax
from jax.experimental import pallas as pl
from jax.experimental.pallas import tpu as pltpu

F32 = jnp.float32
BF16 = jnp.bfloat16
HIGHEST = lax.Precision.HIGHEST
MESH = pl.DeviceIdType.MESH

D_MODEL = 1024
SGU_WIDTH = 512
ATTN_WIDTH = 512
N_HEADS = 8
CHUNK = 128
D_FF = 4096
IN_WIDTH = 4616
N_DEV = 8
IN_SHARD = IN_WIDTH // N_DEV
IN_SHARD_PAD = 640
ZQKV_WIDTH = 2 * SGU_WIDTH + 3 * ATTN_WIDTH
GATE_OFFSET = ZQKV_WIDTH + N_HEADS
EPS = 1e-6
LANES = 128
SUBLANES = 8
VMEM_BYTES = 64 * 1024 * 1024
MIB = 1024 * 1024

ADAM_LR = 0.001
ADAM_B1 = 0.9
ADAM_B2 = 0.999
ADAM_EPS = 1e-08
ADAM_WD = 0.01
ADAM_STEP = 10

TOKEN_TILE = 512
MATMUL_TILE = 512
MERGE_FWD_TILE = 512
MERGE_BWD_TILE = 512
ATTN_TILE = 512
CUM_TILE = 256
SGU_TILE = 1024
WGRAD_TILE = 1024
NEG = -1e30

NT_DIMS = (((1,), (1,)), ((), ()))
TN_DIMS = (((0,), (0,)), ((), ()))


def _params(vmem_mb, n_grid=1):
    return pltpu.CompilerParams(
        dimension_semantics=("arbitrary",) * n_grid,
        vmem_limit_bytes=min(vmem_mb * MIB, VMEM_BYTES - 6 * MIB),
    )


def _dot(a, b):
    return jnp.dot(a, b, preferred_element_type=F32)


def _dot_nt(a, b):
    return lax.dot_general(a, b, NT_DIMS, preferred_element_type=F32)


def _dot_tn(a, b):
    return lax.dot_general(a, b, TN_DIMS, preferred_element_type=F32)


def _const_spec(shape):
    nd = len(shape)
    return pl.BlockSpec(shape, lambda *_: (0,) * nd, pipeline_mode=pl.Buffered(1))


def _row_spec(tm, n, col=0):
    return pl.BlockSpec((tm, n), lambda i: (i, col))


def _fold8(v):
    return v.reshape(v.shape[0] // SUBLANES, SUBLANES, v.shape[1]).sum(axis=0)


def _pick(v, lane_iota, k):
    return jnp.sum(jnp.where(lane_iota == k, v, 0.0), axis=1, keepdims=True)


def _iota(shape, dim):
    return lax.broadcasted_iota(jnp.int32, shape, dim)


def _gelu(x):
    c = 0.7978845608028654
    return 0.5 * x * (1.0 + jnp.tanh(c * (x + 0.044715 * x * x * x)))


def _gelu_grad(x):
    c = 0.7978845608028654
    t = jnp.tanh(c * (x + 0.044715 * x * x * x))
    return 0.5 * (1.0 + t) + 0.5 * x * (1.0 - t * t) * (c * (1.0 + 3.0 * 0.044715 * x * x))


def _rms_stats(v):
    r = lax.rsqrt(jnp.mean(v * v, axis=1, keepdims=True) + EPS)
    return r, v * r


def _rms_bwd(dout, vhat, r, g):
    a = dout * g
    return r * (a - vhat * jnp.mean(a * vhat, axis=1, keepdims=True))


def _mesh_pos():
    return lax.axis_index("x"), lax.axis_index("y"), lax.axis_index("c")


def _dev_index(px, py, pc):
    return 4 * px + 2 * py + pc


def _other_chips(x, y):
    return [(1 - x, y), (x, 1 - y), (1 - x, 1 - y)]


class _WeightGather:
    def __init__(self, shard_shapes, kinds, stage_shapes=None):
        self.shard_shapes = list(shard_shapes)
        self.kinds = list(kinds)
        self.stage_shapes = list(stage_shapes or shard_shapes)
        self.n = len(self.kinds)

    def out_shapes(self):
        shapes = []
        for (rows, cols), kind in zip(self.stage_shapes, self.kinds):
            full = {"block": (N_DEV, rows, cols), "rows": (N_DEV * rows, cols), "cols": (rows, N_DEV * cols)}[kind]
            shapes.append(jax.ShapeDtypeStruct(full, BF16))
        return shapes

    def scratch_shapes(self):
        return ([pltpu.VMEM(s, BF16) for s in self.stage_shapes]
                + [pltpu.SemaphoreType.DMA((self.n, 7)), pltpu.SemaphoreType.DMA((self.n, 7)),
                   pltpu.SemaphoreType.DMA((self.n,))])

    def _view(self, a, ref, j):
        rows, cols = self.stage_shapes[a]
        if self.kinds[a] == "block":
            return ref.at[j]
        if self.kinds[a] == "rows":
            return ref.at[pl.ds(pl.multiple_of(j * rows, rows), rows), :]
        return ref.at[:, pl.ds(pl.multiple_of(j * cols, cols), cols)]

    def _copy(self, outs, scratch, a, k, block, to, from_stage=False):
        stage, (send_sems, recv_sems, _) = scratch[:self.n], scratch[self.n:]
        dst = self._view(a, outs[a], _dev_index(*block))
        return pltpu.make_async_remote_copy(
            src_ref=stage[a] if from_stage else dst, dst_ref=dst,
            send_sem=send_sems.at[a, k], recv_sem=recv_sems.at[a, k],
            device_id=to, device_id_type=MESH)

    def _local(self, outs, scratch, a, me):
        return pltpu.make_async_copy(scratch[a], self._view(a, outs[a], _dev_index(*me)), scratch[-1].at[a])

    def start(self, ins, outs, scratch):
        x, y, c = _mesh_pos()
        me, sibling = (x, y, c), (x, y, 1 - c)
        for a in range(self.n):
            rows, cols = self.shard_shapes[a]
            if self.stage_shapes[a] != self.shard_shapes[a]:
                scratch[a][...] = jnp.zeros(self.stage_shapes[a], BF16)
            scratch[a][0:rows, 0:cols] = ins[a][...].astype(BF16)
            self._local(outs, scratch, a, me).start()
        for a in range(self.n):
            self._copy(outs, scratch, a, 0, me, sibling, True).start()
            for j, chip in enumerate(_other_chips(x, y)):
                self._copy(outs, scratch, a, 1 + j, me, (*chip, c), True).start()

    def forward(self, outs, scratch):
        x, y, c = _mesh_pos()
        me, sibling = (x, y, c), (x, y, 1 - c)
        for a in range(self.n):
            for j, chip in enumerate(_other_chips(x, y)):
                self._copy(outs, scratch, a, 1 + j, (*chip, c), me).wait_recv()
                self._copy(outs, scratch, a, 4 + j, (*chip, c), sibling).start()

    def finish(self, outs, scratch):
        x, y, c = _mesh_pos()
        me, sibling = (x, y, c), (x, y, 1 - c)
        chips = _other_chips(x, y)
        for a in range(self.n):
            self._copy(outs, scratch, a, 0, sibling, me).wait_recv()
            for j, chip in enumerate(chips):
                self._copy(outs, scratch, a, 4 + j, (*chip, 1 - c), me).wait_recv()
        for a in range(self.n):
            self._copy(outs, scratch, a, 0, me, sibling, True).wait_send()
            for j, chip in enumerate(chips):
                self._copy(outs, scratch, a, 1 + j, me, (*chip, c), True).wait_send()
                self._copy(outs, scratch, a, 4 + j, (*chip, c), sibling).wait_send()
            self._local(outs, scratch, a, me).wait()


def _gather_w_in(w_in_local):
    g = _WeightGather([(D_MODEL, IN_SHARD)], ["block"], [(D_MODEL, IN_SHARD_PAD)])

    def body(w_ref, out_ref, *scratch):
        g.start([w_ref], [out_ref], scratch)
        g.forward([out_ref], scratch)
        g.finish([out_ref], scratch)

    return pl.pallas_call(
        body,
        out_shape=g.out_shapes()[0],
        in_specs=[pl.BlockSpec(memory_space=pltpu.VMEM)],
        out_specs=pl.BlockSpec(memory_space=pl.ANY),
        scratch_shapes=g.scratch_shapes(),
        compiler_params=pltpu.CompilerParams(vmem_limit_bytes=32 * MIB),
        name="gather_w_in",
    )(w_in_local)


class _GradExchange:
    def __init__(self, shapes):
        self.shapes = [tuple(s) for s in shapes]
        self.n = len(self.shapes)

    def land_shapes(self, stage):
        slots, dtype = (4, F32) if stage == 1 else (3, BF16)
        return [jax.ShapeDtypeStruct((slots,) + s, dtype) for s in self.shapes]

    def sem_shapes(self, stage):
        slots = 4 if stage == 1 else 3
        return [pltpu.SemaphoreType.DMA((self.n, slots)), pltpu.SemaphoreType.DMA((self.n, slots))]

    def _copy(self, stage, srcs, lands, sems, a, k):
        x, y, c = _mesh_pos()
        cx, cy = (_other_chips(x, y) + [(x, y)])[k]
        if stage == 1:
            src, to = srcs[a].at[_dev_index(cx, cy, 1 - c)], (x, y, 1 - c)
        else:
            src, to = srcs[a].at[k], (cx, cy, c)
        return pltpu.make_async_remote_copy(
            src_ref=src, dst_ref=lands[a].at[k], send_sem=sems[0].at[a, k], recv_sem=sems[1].at[a, k],
            device_id=to, device_id_type=MESH)

    def start(self, stage, srcs, lands, sems):
        for a in range(self.n):
            for k in range(4 if stage == 1 else 3):
                self._copy(stage, srcs, lands, sems, a, k).start()

    def wait(self, stage, srcs, lands, sems):
        for a in range(self.n):
            for k in range(4 if stage == 1 else 3):
                cp = self._copy(stage, srcs, lands, sems, a, k)
                cp.wait_recv()
                cp.wait_send()


def _owner_indices():
    x, y, c = _mesh_pos()
    return jnp.stack([_dev_index(cx, cy, c) for cx, cy in _other_chips(x, y) + [(x, y)]]).astype(jnp.int32)


def _chip_partials(g, land1, idx, name):
    _, rows, cols = g.shape
    tr = min(rows, 256)

    def body(idx_ref, g_ref, l_ref, o_ref):
        o_ref[...] = (g_ref[...] + l_ref[...]).astype(BF16)

    return pl.pallas_call(
        body,
        grid_spec=pltpu.PrefetchScalarGridSpec(
            num_scalar_prefetch=1, grid=(3, rows // tr),
            in_specs=[pl.BlockSpec((None, tr, cols), lambda k, r, idx: (idx[k], r, 0)),
                      pl.BlockSpec((None, tr, cols), lambda k, r, idx: (k, r, 0))],
            out_specs=pl.BlockSpec((None, tr, cols), lambda k, r, idx: (k, r, 0))),
        out_shape=jax.ShapeDtypeStruct((3, rows, cols), BF16),
        compiler_params=_params(32, 2), name=name,
    )(idx, g, land1)


def _reduced_block(g_ref, l1_ref, a_ref, b_ref, c_ref):
    return ((g_ref[...] + l1_ref[...]) + a_ref[...].astype(F32)) + b_ref[...].astype(F32) + c_ref[...].astype(F32)


def _reduced_specs(tm, cols):
    return [pl.BlockSpec((None, tm, cols), lambda i, idx: (idx[3], i, 0)),
            pl.BlockSpec((None, tm, cols), lambda i, idx: (3, i, 0)),
            pl.BlockSpec((None, tm, cols), lambda i, idx: (0, i, 0)),
            pl.BlockSpec((None, tm, cols), lambda i, idx: (1, i, 0)),
            pl.BlockSpec((None, tm, cols), lambda i, idx: (2, i, 0))]


def _reduced_grad(g, land1, land2, idx, name):
    _, rows, cols = g.shape
    tm = min(rows, 256)

    def body(idx_ref, g_ref, l1_ref, a_ref, b_ref, c_ref, o_ref):
        o_ref[...] = _reduced_block(g_ref, l1_ref, a_ref, b_ref, c_ref)

    return pl.pallas_call(
        body,
        grid_spec=pltpu.PrefetchScalarGridSpec(
            num_scalar_prefetch=1, grid=(rows // tm,), in_specs=_reduced_specs(tm, cols),
            out_specs=pl.BlockSpec((tm, cols), lambda i, idx: (i, 0))),
        out_shape=jax.ShapeDtypeStruct((rows, cols), F32),
        compiler_params=_params(32), name=name,
    )(idx, g, land1, land2, land2, land2)


def _adamw_math(w, g, m, v):
    m = ADAM_B1 * m + (1.0 - ADAM_B1) * g
    v = ADAM_B2 * v + (1.0 - ADAM_B2) * (g * g)
    m_hat = m / (1.0 - ADAM_B1 ** ADAM_STEP)
    v_hat = v / (1.0 - ADAM_B2 ** ADAM_STEP)
    delta = -ADAM_LR * (m_hat / (jnp.sqrt(v_hat) + ADAM_EPS) + ADAM_WD * w)
    return delta, m, v


SMALL_NAMES = ("g_mix_pre", "b_forget", "g_sgu", "b_sgu", "w_spatial", "b_spatial", "g_mix_post", "g_ffn_pre",
               "g_ffn_post")
SMALL_SLOTS = {"g_mix_pre": (0, 1, 0, 1024), "g_mix_post": (1, 1, 0, 1024), "g_ffn_pre": (2, 1, 0, 1024),
               "g_ffn_post": (3, 1, 0, 1024), "g_sgu": (4, 1, 0, 512), "b_sgu": (4, 1, 512, 512),
               "b_forget": (5, 1, 0, 128), "b_spatial": (8, 8, 0, 128)}
SMALL_TILE = (16, 1024)
SPATIAL_TILE = (N_HEADS * CHUNK, CHUNK)


LOSS_SLOT = (8, 8, 128, 128)


def _allreduce_small(grads, loss_part):
    names = list(SMALL_NAMES)

    def body(*refs):
        g = dict(zip(names, refs[:len(names)]))
        loss_ref = refs[len(names)]
        tot_a, tot_b, buf_a, buf_b, sib_a, sib_b, ps_a, ps_b, land_a, land_b, send_sems, recv_sems = refs[len(names) + 1:]
        x, y, c = _mesh_pos()
        buf_a[...] = jnp.zeros(SMALL_TILE, F32)
        r0, nr, c0, nc = LOSS_SLOT
        buf_a[r0:r0 + nr, c0:c0 + nc] = loss_ref[...]
        for name, (r0, nr, c0, nc) in SMALL_SLOTS.items():
            val = g[name][...]
            if name == "b_spatial":
                val = jnp.transpose(val)[0:N_HEADS, :]
            buf_a[r0:r0 + nr, c0:c0 + nc] = val
        buf_b[...] = g["w_spatial"][...].reshape(SPATIAL_TILE)

        def swap(k, src, dst, to):
            return pltpu.make_async_remote_copy(src_ref=src, dst_ref=dst, send_sem=send_sems.at[k],
                                                recv_sem=recv_sems.at[k], device_id=to, device_id_type=MESH)

        first = [swap(0, buf_a, sib_a, (x, y, 1 - c)), swap(1, buf_b, sib_b, (x, y, 1 - c))]
        for cp in first:
            cp.start()
        for cp in first:
            cp.wait_recv()
        ps_a[...] = buf_a[...] + sib_a[...]
        ps_b[...] = buf_b[...] + sib_b[...]
        half = SPATIAL_TILE[0] // 2
        mine = pl.ds(pl.multiple_of(c * half, half), half)
        second = []
        for k, (cx, cy) in enumerate(_other_chips(x, y)):
            second += [swap(2 + 2 * k, ps_a, land_a.at[k], (cx, cy, c)),
                       swap(3 + 2 * k, ps_b.at[mine, :], land_b.at[k, mine, :], (cx, cy, c))]
        for cp in second:
            cp.start()
        for cp in second:
            cp.wait_recv()
        tot_a[...] = (ps_a[...] + land_a[0]) + (land_a[1] + land_a[2])
        tot_b[mine, :] = (ps_b[mine, :] + land_b[0, mine, :]) + (land_b[1, mine, :] + land_b[2, mine, :])
        third = swap(8, tot_b.at[mine, :], tot_b.at[mine, :], (x, y, 1 - c))
        third.start()
        third.wait_recv()
        for cp in first + second + [third]:
            cp.wait_send()

    vm = pl.BlockSpec(memory_space=pltpu.VMEM)
    return pl.pallas_call(
        body,
        out_shape=(jax.ShapeDtypeStruct(SMALL_TILE, F32), jax.ShapeDtypeStruct(SPATIAL_TILE, F32)),
        in_specs=[vm] * (len(names) + 1), out_specs=[vm, vm],
        scratch_shapes=[pltpu.VMEM(SMALL_TILE, F32), pltpu.VMEM(SPATIAL_TILE, F32)] * 3
        + [pltpu.VMEM((3,) + SMALL_TILE, F32), pltpu.VMEM((3,) + SPATIAL_TILE, F32),
           pltpu.SemaphoreType.DMA((9,)), pltpu.SemaphoreType.DMA((9,))],
        compiler_params=pltpu.CompilerParams(vmem_limit_bytes=32 * MIB),
        name="allreduce_small",
    )(*[grads[nm] for nm in names], loss_part)


def _adamw_small(tot_a, tot_b, ws, ms, vs):
    names = list(SMALL_NAMES)
    n = len(names)

    def body(a_ref, b_ref, *refs):
        w, m, v = (dict(zip(names, refs[i * n:(i + 1) * n])) for i in range(3))
        outs = [dict(zip(names, refs[(3 + i) * n:(4 + i) * n])) for i in range(4)]
        for name in names:
            if name == "w_spatial":
                g = b_ref[...].reshape(N_HEADS, CHUNK, CHUNK)
            else:
                r0, nr, c0, nc = SMALL_SLOTS[name]
                g = a_ref[r0:r0 + nr, c0:c0 + nc]
            vals = (g,) + _adamw_math(w[name][...], g, m[name][...], v[name][...])
            for out, val in zip(outs, vals):
                out[name][...] = val

    shapes = [jax.ShapeDtypeStruct(ws[nm].shape, F32) for nm in names]
    vm = pl.BlockSpec(memory_space=pltpu.VMEM)
    res = pl.pallas_call(
        body, out_shape=shapes * 4, in_specs=[vm] * (2 + 3 * n), out_specs=[vm] * (4 * n),
        compiler_params=pltpu.CompilerParams(vmem_limit_bytes=32 * MIB), name="adamw_small",
    )(tot_a, tot_b, *[d[nm] for d in (ws, ms, vs) for nm in names])
    return [dict(zip(names, res[i * n:(i + 1) * n])) for i in range(4)]


def _adamw_reduced(w, m, v, g, land1, land2, idx, name):
    rows, cols = w.shape
    tm = min(rows, 256)

    def body(idx_ref, w_ref, m_ref, v_ref, g_ref, l1_ref, a_ref, b_ref, c_ref, go_ref, d_ref, nm_ref, nv_ref):
        gsum = _reduced_block(g_ref, l1_ref, a_ref, b_ref, c_ref)
        go_ref[...] = gsum
        delta, nm, nv = _adamw_math(w_ref[...], gsum, m_ref[...], v_ref[...])
        d_ref[...] = delta
        nm_ref[...] = nm
        nv_ref[...] = nv

    sd = jax.ShapeDtypeStruct((rows, cols), F32)
    spec = pl.BlockSpec((tm, cols), lambda i, idx: (i, 0))
    return pl.pallas_call(
        body,
        grid_spec=pltpu.PrefetchScalarGridSpec(
            num_scalar_prefetch=1, grid=(rows // tm,), in_specs=[spec] * 3 + _reduced_specs(tm, cols),
            out_specs=[spec] * 4),
        out_shape=(sd, sd, sd, sd),
        compiler_params=_params(32), name=name,
    )(idx, w, m, v, g, land1, land2, land2, land2)


def _adamw(w, g, m, v, name):
    rows, cols = w.shape
    tm = 256 if rows % 256 == 0 else rows

    def body(w_ref, g_ref, m_ref, v_ref, d_ref, nm_ref, nv_ref):
        delta, nm, nv = _adamw_math(w_ref[...], g_ref[...], m_ref[...], v_ref[...])
        d_ref[...] = delta
        nm_ref[...] = nm
        nv_ref[...] = nv

    sd = jax.ShapeDtypeStruct((rows, cols), F32)
    spec = _row_spec(tm, cols)
    return pl.pallas_call(
        body, grid=(rows // tm,), out_shape=(sd, sd, sd), in_specs=[spec] * 4, out_specs=[spec] * 3,
        compiler_params=_params(32), name=name,
    )(w, g, m, v)


def _virtual_slab(sources, v0, v_end, like):
    lane = _iota(like.shape, 1)
    out = jnp.zeros(like.shape, like.dtype)
    for v_start, v_stop, read in sources:
        a, b = max(v0, v_start), min(v0 + LANES, v_stop, v_end)
        while a < b:
            c = a - v_start
            n = min(b - a, LANES - c % LANES)
            piece = read(c // LANES)
            shift = (a - v0 - c % LANES) % LANES
            if shift:
                piece = pltpu.roll(piece, shift, 1)
            out = jnp.where((lane >= a - v0) & (lane < a - v0 + n), piece, out)
            a += n
    return out


def _assemble_w_in(wg_in):
    tm = TOKEN_TILE

    def body(src_ref, wz_ref, wf_ref, wg_ref):
        like = src_ref[0, :, 0:LANES]
        sources = [(IN_SHARD * j, IN_SHARD * (j + 1),
                    (lambda k, j=j: src_ref[j, :, LANES * k:LANES * (k + 1)])) for j in range(N_DEV)]
        for k in range(ZQKV_WIDTH // LANES):
            wz_ref[:, LANES * k:LANES * (k + 1)] = _virtual_slab(sources, LANES * k, ZQKV_WIDTH, like)
        wf_ref[...] = _virtual_slab(sources, ZQKV_WIDTH, GATE_OFFSET, like)
        for k in range(2 * D_MODEL // LANES):
            wg_ref[:, LANES * k:LANES * (k + 1)] = _virtual_slab(sources, GATE_OFFSET + LANES * k, IN_WIDTH, like)

    return pl.pallas_call(
        body, grid=(D_MODEL // tm,),
        out_shape=(jax.ShapeDtypeStruct((D_MODEL, ZQKV_WIDTH), BF16), jax.ShapeDtypeStruct((D_MODEL, LANES), BF16),
                   jax.ShapeDtypeStruct((D_MODEL, 2 * D_MODEL), BF16)),
        in_specs=[pl.BlockSpec((N_DEV, tm, IN_SHARD_PAD), lambda i: (0, i, 0))],
        out_specs=[_row_spec(tm, ZQKV_WIDTH), _row_spec(tm, LANES), _row_spec(tm, 2 * D_MODEL)],
        compiler_params=_params(32), name="assemble_w_in",
    )(wg_in)


def _block_dw_in(pieces):
    tm = TOKEN_TILE
    widths = [2 * SGU_WIDTH, ATTN_WIDTH, ATTN_WIDTH, ATTN_WIDTH, N_HEADS, 2 * D_MODEL]
    starts = [sum(widths[:k]) for k in range(len(widths))]

    def body(*refs):
        in_refs, out_ref = refs[:-1], refs[-1]
        like = in_refs[0][:, 0:LANES]
        slab = lambda ref: (lambda k: ref[:, LANES * k:LANES * (k + 1)])
        sources = [(s, s + w, slab(ref)) for s, w, ref in zip(starts, widths, in_refs)]
        for j in range(N_DEV):
            for k in range(IN_SHARD_PAD // LANES):
                out_ref[j, :, LANES * k:LANES * (k + 1)] = _virtual_slab(
                    sources, IN_SHARD * j + LANES * k, IN_SHARD * (j + 1), like)

    return pl.pallas_call(
        body, grid=(D_MODEL // tm,),
        out_shape=jax.ShapeDtypeStruct((N_DEV, D_MODEL, IN_SHARD_PAD), F32),
        in_specs=[_row_spec(tm, pc.shape[1]) for pc in pieces],
        out_specs=pl.BlockSpec((N_DEV, tm, IN_SHARD_PAD), lambda i: (0, i, 0)),
        compiler_params=_params(32), name="block_dw_in",
    )(*pieces)


def _fwd_in(x2, g1, wz, wf, wg):
    T = x2.shape[0]
    tm = MATMUL_TILE

    def body(x_ref, g_ref, wz_ref, wf_ref, wg_ref, xn_ref, zuv_ref, qkv_ref, fl_ref, gt_ref):
        x = x_ref[...]
        r, xh = _rms_stats(x)
        xn = (xh * g_ref[...]).astype(BF16)
        xn_ref[...] = xn
        zuv_ref[...] = _dot(xn, wz_ref[:, 0:1024]).astype(BF16)
        qkv_ref[:, 0:512] = (_dot(xn, wz_ref[:, 1024:1536]) * 0.125).astype(BF16)
        qkv_ref[:, 512:1536] = _dot(xn, wz_ref[:, 1536:2560]).astype(BF16)
        fl_ref[...] = _dot(xn, wf_ref[...])
        gt_ref[...] = jax.nn.sigmoid(_dot(xn, wg_ref[...])).astype(BF16)

    return pl.pallas_call(
        body, grid=(T // tm,),
        out_shape=(jax.ShapeDtypeStruct((T, D_MODEL), BF16), jax.ShapeDtypeStruct((T, 1024), BF16),
                   jax.ShapeDtypeStruct((T, 1536), BF16), jax.ShapeDtypeStruct((T, LANES), F32),
                   jax.ShapeDtypeStruct((T, 2048), BF16)),
        in_specs=[_row_spec(tm, D_MODEL), _const_spec((1, D_MODEL)), _const_spec((D_MODEL, ZQKV_WIDTH)),
                  _const_spec((D_MODEL, LANES)), _const_spec((D_MODEL, 2048))],
        out_specs=[_row_spec(tm, D_MODEL), _row_spec(tm, 1024), _row_spec(tm, 1536), _row_spec(tm, LANES),
                   _row_spec(tm, 2048)],
        compiler_params=_params(48), name="fwd_in",
    )(x2, g1, wz, wf, wg)


def _log_sigmoid(f):
    return jnp.minimum(f, 0.0) - jnp.log1p(jnp.exp(-jnp.abs(f)))


AUG_LANES = 6


def _split3(v):
    hi = v.astype(BF16)
    r1 = v - hi.astype(F32)
    mid = r1.astype(BF16)
    lo = (r1 - mid.astype(F32)).astype(BF16)
    return hi, mid, lo


def _spread(parts, k0):
    n = len(parts)
    r, c = _iota((n * LANES, LANES), 0), _iota((n * LANES, LANES), 1)
    e = jnp.zeros((n * LANES, LANES), BF16)
    for i in range(n):
        h = r - i * LANES
        e = jnp.where((h >= 0) & (h < N_HEADS) & (c == AUG_LANES * h + (k0 + i)), jnp.ones_like(e), e)
    return _dot(jnp.concatenate(parts, axis=1), e)


def _aug_ones(shape, k0):
    lane = _iota(shape, 1)
    head = (lane * 43) >> 8
    slot = lane - AUG_LANES * head
    return ((lane < AUG_LANES * N_HEADS) & (slot >= k0) & (slot < k0 + 3)).astype(F32)


def _aug_query(v):
    return (_spread(_split3(v), 0) + _aug_ones(v.shape, 3)).astype(BF16)


def _aug_key(v):
    return (_aug_ones(v.shape, 0) - _spread(_split3(v), 3)).astype(BF16)


def _aug_stack(t2, aug, p):
    lane = _iota(t2.shape, 1)
    low = lane < 64
    zero = jnp.zeros_like(t2)
    first = 2 * AUG_LANES * p
    a_e = jnp.where((lane >= first) & (lane < first + AUG_LANES), aug, zero)
    a_o = jnp.where((lane >= first + AUG_LANES) & (lane < first + 2 * AUG_LANES), aug, zero)
    top = jnp.concatenate([jnp.where(low, t2, zero), a_e], axis=1)
    bot = jnp.concatenate([jnp.where(low, zero, t2), a_o], axis=1)
    return jnp.concatenate([top, bot], axis=0)


def _fwd_cum(fl, bfp):
    T = fl.shape[0]
    tb = CUM_TILE

    def body(fl_ref, b_ref, cc_ref, qa_ref, ka_ref):
        tri = (_iota((tb, tb), 0) >= _iota((tb, tb), 1)).astype(F32)
        carry = jnp.zeros((1, LANES), F32)
        for i in range(T // tb):
            rows = slice(i * tb, (i + 1) * tb)
            lf = _log_sigmoid(fl_ref[rows, :] + b_ref[...])
            cs = jnp.dot(tri, lf, precision=HIGHEST, preferred_element_type=F32) + carry
            cc_ref[rows, :] = cs
            carry = cs[tb - 1:tb, :]
            qa_ref[rows, :] = _aug_query(cs)
            ka_ref[rows, :] = _aug_key(cs)

    return pl.pallas_call(
        body,
        out_shape=(jax.ShapeDtypeStruct((T, LANES), F32), jax.ShapeDtypeStruct((T, LANES), BF16),
                   jax.ShapeDtypeStruct((T, LANES), BF16)),
        compiler_params=pltpu.CompilerParams(vmem_limit_bytes=32 * MIB), name="fwd_cum",
    )(fl, bfp)


def _sgu_forward_parts(z, gs, bs):
    u = _gelu(z[:, :SGU_WIDTH])
    vv = _gelu(z[:, SGU_WIDTH:])
    vc = vv - jnp.mean(vv, axis=1, keepdims=True)
    rs = lax.rsqrt(jnp.mean(vc * vc, axis=1, keepdims=True) + EPS)
    vhat = vc * rs
    return u, vhat, rs, vhat * gs + bs


def _sgu_pair_weights(w_ref, bT, p):
    tril = _iota((CHUNK, CHUNK), 0) >= _iota((CHUNK, CHUNK), 1)
    we = jnp.where(tril, w_ref[2 * p], 0.0).astype(BF16)
    wo = jnp.where(tril, w_ref[2 * p + 1], 0.0).astype(BF16)
    lane8 = _iota(bT.shape, 1)
    low = _iota((CHUNK, LANES), 1) < 64
    b2 = jnp.where(low, _pick(bT, lane8, 2 * p), _pick(bT, lane8, 2 * p + 1))
    return we, wo, b2


def _chunks_on_lanes(v, p, nc):
    return jnp.concatenate([v[c * CHUNK:(c + 1) * CHUNK, LANES * p:LANES * (p + 1)] for c in range(nc)], axis=1)


def _sgu_mix(we, wo, b2, vcat, nc):
    low = (_iota((CHUNK, nc * LANES), 1) % LANES) < 64
    return jnp.where(low, _dot(we, vcat), _dot(wo, vcat)) + jnp.concatenate([b2] * nc, axis=1)


def _fwd_sgu(zuv, gs, bs, wsp, bT):
    T = zuv.shape[0]
    tc = SGU_TILE
    nc = tc // CHUNK

    def body(z_ref, gs_ref, bs_ref, w_ref, bT_ref, y_ref):
        u, _, _, vln = _sgu_forward_parts(z_ref[...].astype(F32), gs_ref[...], bs_ref[...])
        vb = vln.astype(BF16)
        for p in range(4):
            we, wo, b2 = _sgu_pair_weights(w_ref, bT_ref[...], p)
            s = _sgu_mix(we, wo, b2, _chunks_on_lanes(vb, p, nc), nc)
            for c in range(nc):
                rows, cols = slice(c * CHUNK, (c + 1) * CHUNK), slice(LANES * p, LANES * (p + 1))
                y_ref[rows, cols] = (u[rows, cols] * s[:, c * LANES:(c + 1) * LANES]).astype(BF16)

    return pl.pallas_call(
        body, grid=(T // tc,), out_shape=jax.ShapeDtypeStruct((T, SGU_WIDTH), BF16),
        in_specs=[_row_spec(tc, 1024), _const_spec((1, SGU_WIDTH)), _const_spec((1, SGU_WIDTH)),
                  _const_spec((8, CHUNK, CHUNK)), _const_spec((CHUNK, 8))],
        out_specs=_row_spec(tc, SGU_WIDTH),
        compiler_params=_params(40), name="fwd_sgu",
    )(zuv, gs, bs, wsp, bT)


def _fwd_attn(qkv, qaug, kaug, w_shards):
    T = qkv.shape[0]
    tq = tk = ATTN_TILE
    nq = T // tq
    gather = _WeightGather([w.shape for w in w_shards], ["cols", "cols", "rows", "cols", "rows"])
    nw = gather.n

    def body(q_ref, qa_ref, k_ref, v_ref, ka_ref, *rest):
        w_refs, (o_ref, lse_ref), wg_refs, scratch = rest[:nw], rest[nw:nw + 2], rest[nw + 2:2 * nw + 2], rest[2 * nw + 2:]
        i = pl.program_id(0)

        @pl.when(i == 0)
        def _():
            gather.start(w_refs, wg_refs, scratch)

        @pl.when(i == nq // 2)
        def _():
            gather.forward(wg_refs, scratch)

        lane = _iota((tq, LANES), 1)
        low = lane < 64
        lowk = _iota((tk, LANES), 1) < 64
        one = jnp.ones((tk, LANES), BF16)
        row = _iota((2 * tq, tk), 0) % tq
        col = _iota((2 * tq, tk), 1)
        cols = [slice(LANES * p, LANES * (p + 1)) for p in range(4)]
        qa = qa_ref[...]
        qs = [_aug_stack(q_ref[:, cols[p]], qa, p) for p in range(4)]

        def step(j, carry, masked):
            ks = pl.ds(pl.multiple_of(j * tk, tk), tk)
            ka = ka_ref[ks, :]
            new = []
            for p in range(4):
                m, acc_e, acc_o = carry[p]
                v2 = v_ref[ks, cols[p]]
                s = _dot_nt(qs[p], jnp.concatenate([k_ref[ks, cols[p]], ka], axis=1))
                if masked:
                    s = jnp.where(col <= row, s, NEG)
                mn = jnp.maximum(m, jnp.max(s, axis=1, keepdims=True))
                al = jnp.exp(m - mn)
                pm = jnp.exp(s - mn).astype(BF16)
                acc_e = al[:tq] * acc_e + _dot(pm[:tq], jnp.where(lowk, v2, one))
                acc_o = al[tq:] * acc_o + _dot(pm[tq:], jnp.where(lowk, one, v2))
                new.append((mn, acc_e, acc_o))
            return tuple(new)

        init = tuple((jnp.full((2 * tq, 1), NEG, F32), jnp.zeros((tq, LANES), F32), jnp.zeros((tq, LANES), F32))
                     for _ in range(4))
        def trip(t, c):
            for u in range(3):
                c = step(3 * t + u, c, False)
            return c

        carry = lax.fori_loop(0, i // 3, trip, init)
        carry = lax.fori_loop(3 * (i // 3), i, lambda j, c: step(j, c, False), carry)
        carry = step(i, carry, True)
        lse_blk = jnp.zeros((tq, LANES), F32)
        for p in range(4):
            m, acc_e, acc_o = carry[p]
            l_e = pltpu.roll(acc_e, 64, 1)
            l_o = pltpu.roll(acc_o, 64, 1)
            o_ref[:, cols[p]] = jnp.where(low, acc_e / l_e, acc_o / l_o).astype(BF16)
            lse_blk = jnp.where(lane == 2 * p, m[:tq] + jnp.log(l_e), lse_blk)
            lse_blk = jnp.where(lane == 2 * p + 1, m[tq:] + jnp.log(acc_o), lse_blk)
        lse_ref[...] = lse_blk

        @pl.when(i == nq - 1)
        def _():
            gather.finish(wg_refs, scratch)

    return pl.pallas_call(
        body, grid=(nq,),
        out_shape=[jax.ShapeDtypeStruct((T, ATTN_WIDTH), BF16), jax.ShapeDtypeStruct((T, LANES), F32)]
        + gather.out_shapes(),
        in_specs=[_row_spec(tq, 512), _row_spec(tq, LANES),
                  pl.BlockSpec((T, 512), lambda i: (0, 1), pipeline_mode=pl.Buffered(1)),
                  pl.BlockSpec((T, 512), lambda i: (0, 2), pipeline_mode=pl.Buffered(1)),
                  _const_spec((T, LANES))] + [_const_spec(w.shape) for w in w_shards],
        out_specs=[_row_spec(tq, ATTN_WIDTH), _row_spec(tq, LANES)] + [pl.BlockSpec(memory_space=pl.ANY)] * nw,
        scratch_shapes=gather.scratch_shapes(),
        compiler_params=_params(58), name="fwd_attn",
    )(qkv, qaug, qkv, qkv, kaug, *w_shards)


def _fwd_merge(ys, ya, gt, x2, wbs, wba, wo, g2):
    T = x2.shape[0]
    tm = min(T, MERGE_FWD_TILE)

    def body(ys_ref, ya_ref, gt_ref, x_ref, wbs_ref, wba_ref, wo_ref, g2_ref, a_ref, b_ref, mg_ref, o_ref, h1_ref):
        A = _dot(ys_ref[...], wbs_ref[...])
        B = _dot(ya_ref[...], wba_ref[...])
        mg = (gt_ref[:, :D_MODEL].astype(F32) * A + gt_ref[:, D_MODEL:].astype(F32) * B).astype(BF16)
        o = _dot(mg, wo_ref[...])
        r2, oh = _rms_stats(o)
        a_ref[...] = A.astype(BF16)
        b_ref[...] = B.astype(BF16)
        mg_ref[...] = mg
        o_ref[...] = o.astype(BF16)
        h1_ref[...] = x_ref[...] + oh * g2_ref[...]

    sd = jax.ShapeDtypeStruct((T, D_MODEL), BF16)
    return pl.pallas_call(
        body, grid=(T // tm,),
        out_shape=(sd, sd, sd, sd, jax.ShapeDtypeStruct((T, D_MODEL), F32)),
        in_specs=[_row_spec(tm, 512), _row_spec(tm, 512), _row_spec(tm, 2048), _row_spec(tm, D_MODEL),
                  _const_spec((512, D_MODEL)), _const_spec((512, D_MODEL)), _const_spec((D_MODEL, D_MODEL)),
                  _const_spec((1, D_MODEL))],
        out_specs=[_row_spec(tm, D_MODEL)] * 5,
        compiler_params=_params(58), name="fwd_merge",
    )(ys, ya, gt, x2, wbs, wba, wo, g2)


def _fwd_ffn_loss(h1, tgt, wup, wdn, g3, g4):
    T = h1.shape[0]
    tm = MATMUL_TILE
    nsteps = T // tm

    def body(h1_ref, tg_ref, wup_ref, wdn_ref, g3_ref, g4_ref, xn2_ref, a_ref, ddn_ref, dy_ref, loss_ref,
             dg4_ref, acc_l, acc_g):
        i = pl.program_id(0)

        @pl.when(i == 0)
        def _():
            acc_l[...] = jnp.zeros_like(acc_l)
            acc_g[...] = jnp.zeros_like(acc_g)

        h1v = h1_ref[...]
        r3, h1h = _rms_stats(h1v)
        xn2 = (h1h * g3_ref[...]).astype(BF16)
        xn2_ref[...] = xn2
        dn = jnp.zeros((tm, D_MODEL), F32)
        for j in range(D_FF // 1024):
            cols = slice(1024 * j, 1024 * (j + 1))
            a = _dot(xn2, wup_ref[:, cols])
            a_ref[:, cols] = a.astype(BF16)
            hid = jnp.square(jnp.maximum(a, 0.0)).astype(BF16)
            dn = dn + _dot(hid, wdn_ref[cols, :])
        r4, dnh = _rms_stats(dn)
        g4v = g4_ref[...]
        e = (h1v + dnh * g4v) - tg_ref[...]
        sq = e * e
        s1 = sq[:, 0:LANES]
        for j in range(1, D_MODEL // LANES):
            s1 = s1 + sq[:, LANES * j:LANES * (j + 1)]
        acc_l[...] += _fold8(s1)
        dy = e * (1.0 / D_MODEL)
        dy_ref[...] = dy
        acc_g[...] += _fold8(dy * dnh)
        ddn_ref[...] = _rms_bwd(dy, dnh, r4, g4v).astype(BF16)

        @pl.when(i == nsteps - 1)
        def _():
            loss_ref[...] = acc_l[...] * (0.5 / D_MODEL)
            dg4_ref[...] = jnp.sum(acc_g[...], axis=0, keepdims=True)

    return pl.pallas_call(
        body, grid=(nsteps,),
        out_shape=(jax.ShapeDtypeStruct((T, D_MODEL), BF16), jax.ShapeDtypeStruct((T, D_FF), BF16),
                   jax.ShapeDtypeStruct((T, D_MODEL), BF16), jax.ShapeDtypeStruct((T, D_MODEL), F32),
                   jax.ShapeDtypeStruct((SUBLANES, LANES), F32), jax.ShapeDtypeStruct((1, D_MODEL), F32)),
        in_specs=[_row_spec(tm, D_MODEL), _row_spec(tm, D_MODEL), _const_spec((D_MODEL, D_FF)),
                  _const_spec((D_FF, D_MODEL)), _const_spec((1, D_MODEL)), _const_spec((1, D_MODEL))],
        out_specs=[_row_spec(tm, D_MODEL), _row_spec(tm, D_FF), _row_spec(tm, D_MODEL), _row_spec(tm, D_MODEL),
                   pl.BlockSpec((SUBLANES, LANES), lambda i: (0, 0)), pl.BlockSpec((1, D_MODEL), lambda i: (0, 0))],
        scratch_shapes=[pltpu.VMEM((SUBLANES, LANES), F32), pltpu.VMEM((SUBLANES, D_MODEL), F32)],
        compiler_params=_params(52), name="fwd_ffn_loss",
    )(h1, tgt, wup, wdn, g3, g4)


def _bwd_ffn(ddn, a, dy, h1, wup, wdn, g3):
    T = h1.shape[0]
    tm = MATMUL_TILE
    nsteps = T // tm

    def body(ddn_ref, a_ref, dy_ref, h1_ref, wup_ref, wdn_ref, g3_ref, da_ref, dh1_ref, dg3_ref, acc_g):
        i = pl.program_id(0)

        @pl.when(i == 0)
        def _():
            acc_g[...] = jnp.zeros_like(acc_g)

        ddnv = ddn_ref[...]
        dxn2 = jnp.zeros((tm, D_MODEL), F32)
        for j in range(D_FF // 1024):
            cols = slice(1024 * j, 1024 * (j + 1))
            dhid = _dot_nt(ddnv, wdn_ref[cols, :])
            da = (dhid * (2.0 * jnp.maximum(a_ref[:, cols].astype(F32), 0.0))).astype(BF16)
            da_ref[:, cols] = da
            dxn2 = dxn2 + _dot_nt(da, wup_ref[:, cols])
        r3, h1h = _rms_stats(h1_ref[...])
        acc_g[...] += _fold8(dxn2 * h1h)
        dh1_ref[...] = dy_ref[...] + _rms_bwd(dxn2, h1h, r3, g3_ref[...])

        @pl.when(i == nsteps - 1)
        def _():
            dg3_ref[...] = jnp.sum(acc_g[...], axis=0, keepdims=True)

    return pl.pallas_call(
        body, grid=(nsteps,),
        out_shape=(jax.ShapeDtypeStruct((T, D_FF), BF16), jax.ShapeDtypeStruct((T, D_MODEL), F32),
                   jax.ShapeDtypeStruct((1, D_MODEL), F32)),
        in_specs=[_row_spec(tm, D_MODEL), _row_spec(tm, D_FF), _row_spec(tm, D_MODEL), _row_spec(tm, D_MODEL),
                  _const_spec((D_MODEL, D_FF)), _const_spec((D_FF, D_MODEL)), _const_spec((1, D_MODEL))],
        out_specs=[_row_spec(tm, D_FF), _row_spec(tm, D_MODEL), pl.BlockSpec((1, D_MODEL), lambda i: (0, 0))],
        scratch_shapes=[pltpu.VMEM((SUBLANES, D_MODEL), F32)],
        compiler_params=_params(52), name="bwd_ffn",
    )(ddn, a, dy, h1, wup, wdn, g3)


def _wgrad(xa, dy, name, relu2=False, tn=None, block_cols=None):
    T, K = xa.shape
    N = dy.shape[1]
    tn = N if tn is None else tn
    tt = min(T, WGRAD_TILE if K <= D_MODEL else WGRAD_TILE // 2)
    if block_cols:
        nb = tn // block_cols
        out_shape = jax.ShapeDtypeStruct((N // block_cols, K, block_cols), F32)
        out_spec = pl.BlockSpec((nb, K, block_cols), lambda n, t: (n, 0, 0))
    else:
        out_shape = jax.ShapeDtypeStruct((K, N), F32)
        out_spec = pl.BlockSpec((K, tn), lambda n, t: (0, n))

    def body(x_ref, dy_ref, o_ref):
        @pl.when(pl.program_id(1) == 0)
        def _():
            o_ref[...] = jnp.zeros_like(o_ref)

        xv = x_ref[...]
        if relu2:
            xv = jnp.square(jnp.maximum(xv.astype(F32), 0.0)).astype(BF16)
        if block_cols:
            for b in range(nb):
                o_ref[b] += _dot_tn(xv, dy_ref[:, block_cols * b:block_cols * (b + 1)])
        else:
            o_ref[...] += _dot_tn(xv, dy_ref[...])

    return pl.pallas_call(
        body, grid=(N // tn, T // tt), out_shape=out_shape,
        in_specs=[pl.BlockSpec((tt, K), lambda n, t: (t, 0)), pl.BlockSpec((tt, tn), lambda n, t: (t, n))],
        out_specs=out_spec,
        compiler_params=_params(52, 2), name=name,
    )(xa, dy)


def _wgrad_multi(xa, dys, name):
    T, K = xa.shape
    tt = min(T, WGRAD_TILE)
    n = len(dys)

    def body(x_ref, *refs):
        dy_refs, o_refs = refs[:n], refs[n:]

        @pl.when(pl.program_id(0) == 0)
        def _():
            for o_ref in o_refs:
                o_ref[...] = jnp.zeros_like(o_ref)

        xv = x_ref[...]
        for dy_ref, o_ref in zip(dy_refs, o_refs):
            o_ref[...] += _dot_tn(xv, dy_ref[...])

    return pl.pallas_call(
        body, grid=(T // tt,),
        out_shape=[jax.ShapeDtypeStruct((K, dy.shape[1]), F32) for dy in dys],
        in_specs=[_row_spec(tt, K)] + [_row_spec(tt, dy.shape[1]) for dy in dys],
        out_specs=[pl.BlockSpec((K, dy.shape[1]), lambda t: (0, 0)) for dy in dys],
        compiler_params=_params(52), name=name,
    )(xa, *dys)


def _bwd_merge(dh1, o, A, B, gt, ys, ya, mg, lse, cc, wbs, wba, wo, g2):
    T = dh1.shape[0]
    tm = MERGE_BWD_TILE
    nsteps = T // tm

    def body(dh1_ref, o_ref, a_ref, b_ref, gt_ref, ys_ref, ya_ref, mg_ref, lse_ref, cc_ref, wbs_ref, wba_ref,
             wo_ref, g2_ref, dgl_ref, dys_ref, dya_ref, qab_ref, dab_ref, dg2_ref, dwbs_ref, dwba_ref, dwo_ref,
             acc_g):
        i = pl.program_id(0)

        @pl.when(i == 0)
        def _():
            acc_g[...] = jnp.zeros_like(acc_g)
            dwbs_ref[...] = jnp.zeros_like(dwbs_ref)
            dwba_ref[...] = jnp.zeros_like(dwba_ref)
            dwo_ref[...] = jnp.zeros_like(dwo_ref)

        dh1v = dh1_ref[...]
        r2, oh = _rms_stats(o_ref[...].astype(F32))
        acc_g[...] += _fold8(dh1v * oh)
        do = _rms_bwd(dh1v, oh, r2, g2_ref[...]).astype(BF16)
        dwo_ref[...] += _dot_tn(mg_ref[...], do)
        dmg = _dot_nt(do, wo_ref[...])
        ga = gt_ref[:, :D_MODEL].astype(F32)
        gb = gt_ref[:, D_MODEL:].astype(F32)
        dgl_ref[:, :D_MODEL] = (dmg * a_ref[...].astype(F32) * ga * (1.0 - ga)).astype(BF16)
        dgl_ref[:, D_MODEL:] = (dmg * b_ref[...].astype(F32) * gb * (1.0 - gb)).astype(BF16)
        dA = (dmg * ga).astype(BF16)
        dB = (dmg * gb).astype(BF16)
        dwbs_ref[...] += _dot_tn(ys_ref[...], dA)
        dwba_ref[...] += _dot_tn(ya_ref[...], dB)
        dys_ref[...] = _dot_nt(dA, wbs_ref[...]).astype(BF16)
        dya = _dot_nt(dB, wba_ref[...]).astype(BF16)
        dya_ref[...] = dya
        prod = dya.astype(F32) * ya_ref[...].astype(F32)
        lane = _iota((tm, LANES), 1)
        low = lane < 64
        blk = jnp.zeros((tm, LANES), F32)
        for p in range(4):
            pp = prod[:, LANES * p:LANES * (p + 1)]
            blk = jnp.where(lane == 2 * p, jnp.sum(jnp.where(low, pp, 0.0), axis=1, keepdims=True), blk)
            blk = jnp.where(lane == 2 * p + 1, jnp.sum(jnp.where(low, 0.0, pp), axis=1, keepdims=True), blk)
        qab_ref[...] = _aug_query(cc_ref[...] - lse_ref[...])
        dab_ref[...] = _spread(_split3(-blk), 0).astype(BF16)

        @pl.when(i == nsteps - 1)
        def _():
            dg2_ref[...] = jnp.sum(acc_g[...], axis=0, keepdims=True)

    sh = jax.ShapeDtypeStruct((T, 512), BF16)
    sa = jax.ShapeDtypeStruct((T, LANES), BF16)
    sw = jax.ShapeDtypeStruct((512, D_MODEL), F32)
    whole = lambda shape: pl.BlockSpec(shape, lambda i: (0, 0))
    return pl.pallas_call(
        body, grid=(nsteps,),
        out_shape=(jax.ShapeDtypeStruct((T, 2048), BF16), sh, sh, sa, sa, jax.ShapeDtypeStruct((1, D_MODEL), F32),
                   sw, sw, jax.ShapeDtypeStruct((D_MODEL, D_MODEL), F32)),
        in_specs=[_row_spec(tm, D_MODEL)] * 4 + [_row_spec(tm, 2048), _row_spec(tm, 512), _row_spec(tm, 512),
                  _row_spec(tm, D_MODEL), _row_spec(tm, LANES), _row_spec(tm, LANES),
                  _const_spec((512, D_MODEL)), _const_spec((512, D_MODEL)),
                  _const_spec((D_MODEL, D_MODEL)), _const_spec((1, D_MODEL))],
        out_specs=[_row_spec(tm, 2048), _row_spec(tm, 512), _row_spec(tm, 512), _row_spec(tm, LANES),
                   _row_spec(tm, LANES), whole((1, D_MODEL)), whole((512, D_MODEL)), whole((512, D_MODEL)),
                   whole((D_MODEL, D_MODEL))],
        scratch_shapes=[pltpu.VMEM((SUBLANES, D_MODEL), F32)],
        compiler_params=_params(56), name="bwd_merge",
    )(dh1, o, A, B, gt, ys, ya, mg, lse, cc, wbs, wba, wo, g2)


def _bwd_sgu(zuv, dys, gs, bs, wsp, bT, grads):
    T = zuv.shape[0]
    tc = SGU_TILE
    nc = tc // CHUNK
    nsteps = T // tc
    ex = _GradExchange([tuple(g.shape[1:]) for g in grads])
    ng = ex.n

    def body(z_ref, dy_ref, gs_ref, bs_ref, w_ref, bT_ref, *rest):
        g_refs, (dz_ref, dw_ref, dbT_ref, dgs_ref, dbs_ref) = rest[:ng], rest[ng:ng + 5]
        land1 = rest[ng + 5:2 * ng + 5]
        acc_w, acc_b, acc_gs, acc_bs, dvln_s = rest[2 * ng + 5:2 * ng + 10]
        ex_sems = rest[2 * ng + 10:]
        i = pl.program_id(0)

        @pl.when(i == 0)
        def _():
            ex.start(1, g_refs, land1, ex_sems)
            acc_w[...] = jnp.zeros_like(acc_w)
            acc_b[...] = jnp.zeros_like(acc_b)
            acc_gs[...] = jnp.zeros_like(acc_gs)
            acc_bs[...] = jnp.zeros_like(acc_bs)

        z = z_ref[...].astype(F32)
        gsv = gs_ref[...]
        u, vhat, rs, vln = _sgu_forward_parts(z, gsv, bs_ref[...])
        vb = vln.astype(BF16)
        dy = dy_ref[...].astype(F32)
        low_w = (_iota((CHUNK, nc * LANES), 1) % LANES) < 64
        for p in range(4):
            we, wo, b2 = _sgu_pair_weights(w_ref, bT_ref[...], p)
            vcat = _chunks_on_lanes(vb, p, nc)
            s = _sgu_mix(we, wo, b2, vcat, nc)
            dyc = _chunks_on_lanes(dy, p, nc)
            ds = dyc * _chunks_on_lanes(u, p, nc)
            dsb = ds.astype(BF16)
            zero = jnp.zeros_like(dsb)
            dse = jnp.where(low_w, dsb, zero)
            dso = jnp.where(low_w, zero, dsb)
            acc_w[2 * p] += _dot_nt(dse, vcat)
            acc_w[2 * p + 1] += _dot_nt(dso, vcat)
            acc_b[p] += ds
            dvl = jnp.where(low_w, _dot_tn(we, dsb), _dot_tn(wo, dsb))
            for c in range(nc):
                rows, cols = slice(c * CHUNK, (c + 1) * CHUNK), slice(LANES * p, LANES * (p + 1))
                dvln_s[rows, cols] = dvl[:, c * LANES:(c + 1) * LANES]
                du = dy[rows, cols] * s[:, c * LANES:(c + 1) * LANES]
                dz_ref[rows, cols] = (du * _gelu_grad(z[rows, cols])).astype(BF16)
        dvln = dvln_s[...]
        acc_gs[...] += _fold8(dvln * vhat)
        acc_bs[...] += _fold8(dvln)
        al = dvln * gsv
        dvv = rs * (al - jnp.mean(al, axis=1, keepdims=True) - vhat * jnp.mean(al * vhat, axis=1, keepdims=True))
        dz_ref[:, SGU_WIDTH:] = (dvv * _gelu_grad(z[:, SGU_WIDTH:])).astype(BF16)

        @pl.when(i == nsteps - 1)
        def _():
            tril = _iota((CHUNK, CHUNK), 0) >= _iota((CHUNK, CHUNK), 1)
            lane = _iota((CHUNK, LANES), 1)
            low = lane < 64
            blk = jnp.zeros((CHUNK, LANES), F32)
            for g in range(8):
                dw_ref[g] = jnp.where(tril, acc_w[g], 0.0)
            for p in range(4):
                t = acc_b[p]
                tot = t[:, 0:LANES]
                for c in range(1, nc):
                    tot = tot + t[:, c * LANES:(c + 1) * LANES]
                blk = jnp.where(lane == 2 * p, jnp.sum(jnp.where(low, tot, 0.0), axis=1, keepdims=True), blk)
                blk = jnp.where(lane == 2 * p + 1, jnp.sum(jnp.where(low, 0.0, tot), axis=1, keepdims=True), blk)
            dbT_ref[...] = blk
            dgs_ref[...] = jnp.sum(acc_gs[...], axis=0, keepdims=True)
            dbs_ref[...] = jnp.sum(acc_bs[...], axis=0, keepdims=True)
            ex.wait(1, g_refs, land1, ex_sems)

    whole = lambda shape: pl.BlockSpec(shape, lambda i: (0,) * len(shape))
    hbm_spec = pl.BlockSpec(memory_space=pl.ANY)
    return pl.pallas_call(
        body, grid=(nsteps,),
        out_shape=[jax.ShapeDtypeStruct((T, 1024), BF16), jax.ShapeDtypeStruct((8, CHUNK, CHUNK), F32),
                   jax.ShapeDtypeStruct((CHUNK, LANES), F32), jax.ShapeDtypeStruct((1, SGU_WIDTH), F32),
                   jax.ShapeDtypeStruct((1, SGU_WIDTH), F32)] + ex.land_shapes(1),
        in_specs=[_row_spec(tc, 1024), _row_spec(tc, SGU_WIDTH), _const_spec((1, SGU_WIDTH)),
                  _const_spec((1, SGU_WIDTH)), _const_spec((8, CHUNK, CHUNK)), _const_spec((CHUNK, 8))]
        + [hbm_spec] * ng,
        out_specs=[_row_spec(tc, 1024), whole((8, CHUNK, CHUNK)), whole((CHUNK, LANES)),
                   whole((1, SGU_WIDTH)), whole((1, SGU_WIDTH))] + [hbm_spec] * ng,
        scratch_shapes=[pltpu.VMEM((8, CHUNK, CHUNK), F32), pltpu.VMEM((4, CHUNK, nc * LANES), F32),
                        pltpu.VMEM((SUBLANES, SGU_WIDTH), F32), pltpu.VMEM((SUBLANES, SGU_WIDTH), F32),
                        pltpu.VMEM((tc, SGU_WIDTH), F32)] + ex.sem_shapes(1),
        compiler_params=_params(48), name="bwd_sgu",
    )(zuv, dys, gs, bs, wsp, bT, *grads)


def _bwd_attn(qkv, dya, qab, dab, kaug, parts):
    T = qkv.shape[0]
    tq = tk = ATTN_TILE
    nq = T // tq
    nk = T // tk
    ex = _GradExchange([tuple(g.shape[1:]) for g in parts])
    nr = ex.n

    def body(q_ref, do_ref, qa_ref, da_ref, k_ref, v_ref, ka_ref, *rest):
        part_refs, (dq_ref, dk_ref, dv_ref, dcx_ref) = rest[:nr], rest[nr:nr + 4]
        land2, dq_acc, ex_sems = rest[nr + 4:2 * nr + 4], rest[2 * nr + 4], rest[2 * nr + 5:]
        p = pl.program_id(0)
        j = pl.program_id(1)

        @pl.when((p == 0) & (j == 0))
        def _():
            ex.start(2, part_refs, land2, ex_sems)

        lane = _iota((tq, LANES), 1)
        low = lane < 64
        row = _iota((2 * tq, tk), 0) % tq
        col = _iota((2 * tq, tk), 1)
        first = 2 * AUG_LANES * p
        half = tq // 2

        @pl.when(j == 0)
        def _():
            dq_acc[...] = jnp.zeros_like(dq_acc)

        @pl.when((j == 0) & (p == 0))
        def _():
            dcx_ref[...] = jnp.zeros_like(dcx_ref)

        ka = ka_ref[...]
        kk = jnp.concatenate([k_ref[...], ka], axis=1)
        vv = jnp.concatenate([v_ref[...], ka], axis=1)

        def tile(i, carry, r0, c0, nc, masked):
            dk_a, dv_a = carry
            nr = tq - r0
            qsl = pl.ds(pl.multiple_of(i * tq + r0, half), nr)
            qs = _aug_stack(q_ref[qsl, :], qa_ref[qsl, :], p)
            dos = _aug_stack(do_ref[qsl, :], da_ref[qsl, :], p)
            kc, vc = kk[c0:c0 + nc], vv[c0:c0 + nc]
            s = _dot_nt(qs, kc)
            if masked:
                s = jnp.where(col[:2 * nr, :nc] + c0 <= row[:2 * nr, :nc] % nr + r0, s, NEG)
            pm = jnp.exp(s)
            ds = pm * _dot_nt(dos, vc)
            dsb = ds.astype(BF16)
            dv_u = _dot_tn(pm.astype(BF16), dos[:, :LANES])
            dk_u = _dot_tn(dsb, qs)
            if nc == tk:
                dv_a, dk_a = dv_a + dv_u, dk_a + dk_u
            else:
                pad = lambda u: jnp.concatenate(
                    [jnp.zeros((n, u.shape[1]), F32) if z else u
                     for z, n in ((True, c0), (False, nc), (True, tk - c0 - nc)) if n], axis=0)
                dv_a, dk_a = dv_a + pad(dv_u), dk_a + pad(dk_u)
            dqx = _dot(dsb, kc)
            dq_acc[qsl, :] += jnp.where(low[:nr], dqx[:nr, :LANES], dqx[nr:, :LANES])
            dcx_ref[qsl, :] += (jnp.where(lane[:nr] == first, dqx[:nr, LANES:], 0.0)
                                + jnp.where(lane[:nr] == first + AUG_LANES, dqx[nr:, LANES:], 0.0))
            return dk_a, dv_a

        def q_block(i, carry, masked):
            return tile(i, carry, 0, 0, tk, masked)

        init =(jnp.zeros((tk, 2 * LANES), F32), jnp.zeros((tk, LANES), F32))
        carry = tile(j, init, 0, 0, half, True)
        carry = tile(j, carry, half, half, half, True)
        n_rest = nq - 1 - j

        def trip(t, c):
            for u in range(3):
                c = q_block(j + 1 + 3 * t + u, c, False)
            return c

        carry = lax.fori_loop(0, n_rest // 3, trip, carry)
        dk_a, dv_a = lax.fori_loop(j + 1 + 3 * (n_rest // 3), nq, lambda i, c: q_block(i, c, False), carry)
        dk_ref[...] = dk_a[:, :LANES].astype(BF16)
        dv_ref[...] = dv_a.astype(BF16)
        ksl = pl.ds(pl.multiple_of(j * tk, tk), tk)
        lk = _iota((tk, LANES), 1)
        dcx_ref[ksl, :] += jnp.where((lk == first + 3) | (lk == first + AUG_LANES + 3), dk_a[:, LANES:], 0.0)

        @pl.when(j == nk - 1)
        def _():
            dq_ref[...] = (dq_acc[...] * 0.125).astype(BF16)

        @pl.when((p == 3) & (j == nk - 1))
        def _():
            ex.wait(2, part_refs, land2, ex_sems)

    sh = jax.ShapeDtypeStruct((T, ATTN_WIDTH), BF16)
    full = lambda cb: pl.BlockSpec((T, LANES), lambda p, j: (0, cb + p))
    blk = lambda cb: pl.BlockSpec((tk, LANES), lambda p, j: (j, cb + p))
    hbm_spec = pl.BlockSpec(memory_space=pl.ANY)
    return pl.pallas_call(
        body, grid=(4, nk),
        out_shape=[sh, sh, sh, jax.ShapeDtypeStruct((T, LANES), F32)] + ex.land_shapes(2),
        in_specs=[full(0), full(0), _const_spec((T, LANES)), _const_spec((T, LANES)), blk(4), blk(8),
                  pl.BlockSpec((tk, LANES), lambda p, j: (j, 0))] + [hbm_spec] * nr,
        out_specs=[full(0), blk(0), blk(0), pl.BlockSpec((T, LANES), lambda p, j: (0, 0))] + [hbm_spec] * nr,
        scratch_shapes=[pltpu.VMEM((T, LANES), F32)] + ex.sem_shapes(2),
        compiler_params=_params(58, 2), name="bwd_attn",
    )(qkv, dya, qab, dab, qkv, qkv, kaug, *parts)


def _bwd_cum(dcx, fl, bfp):
    T = fl.shape[0]
    tb = CUM_TILE

    def body(dcx_ref, fl_ref, b_ref, dfl_ref, dbf_ref):
        triu = (_iota((tb, tb), 0) <= _iota((tb, tb), 1)).astype(F32)
        r, c = _iota((LANES, LANES), 0), _iota((LANES, LANES), 1)
        sel = (((r == AUG_LANES * c) & (c < N_HEADS)).astype(F32)
               - ((r == AUG_LANES * c + 3) & (c < N_HEADS)).astype(F32))
        carry = jnp.zeros((1, LANES), F32)
        dbf = jnp.zeros((1, LANES), F32)
        for i in reversed(range(T // tb)):
            colblk = jnp.dot(dcx_ref[i * tb:(i + 1) * tb, :], sel, precision=HIGHEST, preferred_element_type=F32)
            rc = jnp.dot(triu, colblk, precision=HIGHEST, preferred_element_type=F32) + carry
            carry = rc[0:1, :]
            sig = jax.nn.sigmoid(fl_ref[i * tb:(i + 1) * tb, :] + b_ref[...])
            dfl = rc * (1.0 - sig)
            dfl_ref[i * tb:(i + 1) * tb, :] = dfl.astype(BF16)
            dbf = dbf + jnp.sum(dfl, axis=0, keepdims=True)
        dbf_ref[...] = dbf

    return pl.pallas_call(
        body,
        out_shape=(jax.ShapeDtypeStruct((T, LANES), BF16), jax.ShapeDtypeStruct((1, LANES), F32)),
        compiler_params=pltpu.CompilerParams(vmem_limit_bytes=32 * MIB), name="bwd_cum",
    )(dcx, fl, bfp)


def _bwd_in(dz, dq, dk, dv, dfl, dgl, dh1, x2, g1, wz, wf, wg, rows, name, prev=None, stage=0, exchanged=None):
    T = x2.shape[0]
    tm = math.gcd(MATMUL_TILE, rows[0], rows[1] - rows[0])
    first = rows[0] // tm
    nsteps = (rows[1] - rows[0]) // tm
    ex = _GradExchange([tuple(exchanged.shape[1:])]) if stage else None

    def body(dz_ref, dq_ref, dk_ref, dv_ref, dfl_ref, dgl_ref, dh1_ref, x_ref, g_ref, wz_ref, wf_ref, wg_ref, *rest):
        rest = list(rest)
        dx_prev, dg1_prev = (rest.pop(0), rest.pop(0)) if prev else (None, None)
        src_ref = rest.pop(0) if stage else None
        dx_ref, dg1_ref = rest.pop(0), rest.pop(0)
        land_ref = rest.pop(0) if stage else None
        acc_g, ex_sems = rest[0], rest[1:]
        i = pl.program_id(0)

        @pl.when(i == 0)
        def _():
            if stage:
                ex.start(stage, [src_ref], [land_ref], ex_sems)
            acc_g[...] = jnp.zeros_like(acc_g)

        dxn = _dot_nt(dz_ref[...], wz_ref[:, 0:1024])
        dxn = dxn + _dot_nt(dq_ref[...], wz_ref[:, 1024:1536])
        dxn = dxn + _dot_nt(dk_ref[...], wz_ref[:, 1536:2048])
        dxn = dxn + _dot_nt(dv_ref[...], wz_ref[:, 2048:2560])
        dxn = dxn + _dot_nt(dfl_ref[...], wf_ref[...])
        dxn = dxn + _dot_nt(dgl_ref[...], wg_ref[...])
        r1, xh = _rms_stats(x_ref[...])
        acc_g[...] += _fold8(dxn * xh)
        dx_ref[...] = dh1_ref[...] + _rms_bwd(dxn, xh, r1, g_ref[...])

        @pl.when(i == nsteps - 1)
        def _():
            total = jnp.sum(acc_g[...], axis=0, keepdims=True)
            dg1_ref[...] = total + dg1_prev[...] if prev else total
            if stage:
                ex.wait(stage, [src_ref], [land_ref], ex_sems)

    hbm_spec = pl.BlockSpec(memory_space=pl.ANY)
    rows_spec = lambda n: pl.BlockSpec((tm, n), lambda i: (i + first, 0))
    operands = [dz, dq, dk, dv, dfl, dgl, dh1, x2, g1, wz, wf, wg]
    in_specs = [rows_spec(1024), rows_spec(512), rows_spec(512), rows_spec(512), rows_spec(LANES), rows_spec(2048),
                rows_spec(D_MODEL), rows_spec(D_MODEL),
                _const_spec((1, D_MODEL)), _const_spec((D_MODEL, ZQKV_WIDTH)), _const_spec((D_MODEL, LANES)),
                _const_spec((D_MODEL, 2048))]
    aliases = {}
    if prev:
        aliases = {len(operands): 0}
        operands += list(prev)
        in_specs += [hbm_spec, _const_spec((1, D_MODEL))]
    if stage:
        operands.append(exchanged)
        in_specs.append(hbm_spec)
    return pl.pallas_call(
        body, grid=(nsteps,),
        out_shape=[jax.ShapeDtypeStruct((T, D_MODEL), F32), jax.ShapeDtypeStruct((1, D_MODEL), F32)]
        + (ex.land_shapes(stage) if stage else []),
        in_specs=in_specs,
        out_specs=[rows_spec(D_MODEL), pl.BlockSpec((1, D_MODEL), lambda i: (0, 0))] + ([hbm_spec] if stage else []),
        scratch_shapes=[pltpu.VMEM((SUBLANES, D_MODEL), F32)] + (ex.sem_shapes(stage) if stage else []),
        input_output_aliases=aliases,
        compiler_params=_params(48), name=name,
    )(*operands)


def _in_hbm(v):
    return pltpu.with_memory_space_constraint(v, pltpu.HBM)


def _small_kernel_shapes(g_mix_pre, b_forget, g_sgu, b_sgu, w_spatial, b_spatial, g_mix_post, g_ffn_pre, g_ffn_post):
    return dict(g_mix_pre=g_mix_pre, b_forget=jnp.pad(b_forget, ((0, 0), (0, LANES - N_HEADS))), g_sgu=g_sgu,
                b_sgu=b_sgu, w_spatial=w_spatial[0], b_spatial=b_spatial[0], g_mix_post=g_mix_post,
                g_ffn_pre=g_ffn_pre, g_ffn_post=g_ffn_post)


def _small_output_shapes(d):
    out = dict(d)
    out.update(b_forget=d["b_forget"][:, :N_HEADS], w_spatial=d["w_spatial"][None], b_spatial=d["b_spatial"][None])
    return out


def kernel(x, g_mix_pre, w_in, b_forget, g_sgu, b_sgu, w_spatial, b_spatial, w_branch_sgu, w_branch_attn, w_out, g_mix_post, g_ffn_pre, w_up, w_down, g_ffn_post, loss_target, m_g_mix_pre, m_w_in, m_b_forget, m_g_sgu, m_b_sgu, m_w_spatial, m_b_spatial, m_w_branch_sgu, m_w_branch_attn, m_w_out, m_g_mix_post, m_g_ffn_pre, m_w_up, m_w_down, m_g_ffn_post, v_g_mix_pre, v_w_in, v_b_forget, v_g_sgu, v_b_sgu, v_w_spatial, v_b_spatial, v_w_branch_sgu, v_w_branch_attn, v_w_out, v_g_mix_post, v_g_ffn_pre, v_w_up, v_w_down, v_g_ffn_post):
    T = x.shape[1]
    x2 = x.reshape(T, D_MODEL)
    tgt = loss_target.reshape(T, D_MODEL)

    wg_in = _gather_w_in(w_in[0])
    wz, wf, wgt = _assemble_w_in(wg_in)
    bfp = jnp.pad(b_forget, ((0, 0), (0, LANES - N_HEADS)))
    wsp = w_spatial[0]
    bT = b_spatial[0].T

    xn, zuv, qkv, fl, gt = _fwd_in(x2, g_mix_pre, wz, wf, wgt)
    cc, qaug, kaug = _fwd_cum(fl, bfp)
    ys = _fwd_sgu(zuv, g_sgu, b_sgu, wsp, bT)
    ya, lse, wbs, wba, wo, wup, wdn = _fwd_attn(
        qkv, qaug, kaug, (w_branch_sgu[0], w_branch_attn[0], w_out[0], w_up[0], w_down[0]))
    A, B, mg, o, h1 = _fwd_merge(ys, ya, gt, x2, wbs, wba, wo, g_mix_post)
    xn2, a, ddn, dy, loss_part, dg4 = _fwd_ffn_loss(h1, tgt, wup, wdn, g_ffn_pre, g_ffn_post)

    da, dh1, dg3 = _bwd_ffn(ddn, a, dy, h1, wup, wdn, g_ffn_pre)
    dw_up = _wgrad(xn2, da, "wgrad_up", tn=2048, block_cols=512)
    dw_down = _wgrad(a, ddn, "wgrad_down", relu2=True)
    dgl, dys, dya, qab, dab, dg2, dw_bs, dw_ba, dw_out = _bwd_merge(
        dh1, o, A, B, gt, ys, ya, mg, lse, cc, wbs, wba, wo, g_mix_post)
    col_blocks = lambda g, w: g.reshape(g.shape[0], N_DEV, w).transpose(1, 0, 2)
    row_blocks = lambda g, r: g.reshape(N_DEV, r, g.shape[1])
    early_names = ["w_branch_sgu", "w_branch_attn", "w_out", "w_up", "w_down"]
    early = [col_blocks(dw_bs, 128), col_blocks(dw_ba, 128), row_blocks(dw_out, 128), dw_up, row_blocks(dw_down, 512)]
    owners = _owner_indices()
    dzuv, dwsp, dbT, dgs, dbs, *early_land1 = _bwd_sgu(zuv, dys, g_sgu, b_sgu, wsp, bT, early)
    early_parts = [_chip_partials(g, l1, owners, "chip_partials_" + nm)
                   for g, l1, nm in zip(early, early_land1, early_names)]
    dq, dk, dv, dcx, *early_land2 = _bwd_attn(qkv, dya, qab, dab, kaug, early_parts)
    dfl, dbf = _bwd_cum(dcx, fl, bfp)
    dw_z = _wgrad(xn, dzuv, "wgrad_in_z")
    dw_q, dw_k, dw_v, dw_f = _wgrad_multi(xn, [dq, dk, dv, dfl], "wgrad_in_qkvf")
    dw_g = _wgrad(xn, dgl, "wgrad_in_gate")
    blocks_in = _block_dw_in([dw_z, dw_q, dw_k, dw_v, dw_f, dw_g])
    bwd_in_args = (dzuv, dq, dk, dv, dfl, dgl, dh1, x2, g_mix_pre, wz, wf, wgt)
    dx, dg1, land1_in = _bwd_in(*bwd_in_args, (0, T // 8), "bwd_in_a", stage=1, exchanged=blocks_in)
    part_in = _chip_partials(blocks_in, land1_in, owners, "chip_partials_w_in")
    dx, dg1, land2_in = _bwd_in(*bwd_in_args, (T // 8, 3 * T // 4), "bwd_in_b", prev=(dx, dg1), stage=2,
                                exchanged=part_in)
    dx, dg1 = _bwd_in(*bwd_in_args, (3 * T // 4, T), "bwd_in_c", prev=(dx, dg1))

    tot_a, tot_b = _allreduce_small(dict(
        g_mix_pre=dg1, b_forget=dbf, g_sgu=dgs, b_sgu=dbs, w_spatial=dwsp, b_spatial=dbT, g_mix_post=dg2,
        g_ffn_pre=dg3, g_ffn_post=dg4), loss_part)
    r0, nr, c0, nc = LOSS_SLOT
    loss = jnp.sum(tot_a[r0:r0 + nr, c0:c0 + nc])
    small_w = _small_kernel_shapes(g_mix_pre, b_forget, g_sgu, b_sgu, w_spatial, b_spatial, g_mix_post, g_ffn_pre,
                                   g_ffn_post)
    small_m = _small_kernel_shapes(m_g_mix_pre, m_b_forget, m_g_sgu, m_b_sgu, m_w_spatial, m_b_spatial, m_g_mix_post,
                                   m_g_ffn_pre, m_g_ffn_post)
    small_v = _small_kernel_shapes(v_g_mix_pre, v_b_forget, v_g_sgu, v_b_sgu, v_w_spatial, v_b_spatial, v_g_mix_post,
                                   v_g_ffn_pre, v_g_ffn_post)
    sg, sd, sm, sv = (_small_output_shapes(d) for d in _adamw_small(tot_a, tot_b, small_w, small_m, small_v))

    big = {}
    g_in = _reduced_grad(blocks_in, land1_in, land2_in, owners, "reduced_grad_w_in")[:, :IN_SHARD]
    d_, m_, v_ = _adamw(_in_hbm(w_in[0]), g_in, _in_hbm(m_w_in[0]), _in_hbm(v_w_in[0]), "adamw_w_in")
    big["w_in"] = (g_in[None], d_[None], m_[None], v_[None])
    early_wmv = [(w_branch_sgu, m_w_branch_sgu, v_w_branch_sgu), (w_branch_attn, m_w_branch_attn, v_w_branch_attn),
                 (w_out, m_w_out, v_w_out), (w_up, m_w_up, v_w_up), (w_down, m_w_down, v_w_down)]
    for nm, (w, m, v), g, l1, l2 in zip(early_names, early_wmv, early, early_land1, early_land2):
        big[nm] = tuple(t[None] for t in _adamw_reduced(
            _in_hbm(w[0]), _in_hbm(m[0]), _in_hbm(v[0]), g, l1, l2, owners, "adamw_" + nm))

    order = ["g_mix_pre", "w_in", "b_forget", "g_sgu", "b_sgu", "w_spatial", "b_spatial", "w_branch_sgu",
             "w_branch_attn", "w_out", "g_mix_post", "g_ffn_pre", "w_up", "w_down", "g_ffn_post"]
    outs = [loss, dx.reshape(1, T, D_MODEL)]
    for kind, small in enumerate((sg, sd, sm, sv)):
        outs += [big[nm][kind] if nm in big else small[nm] for nm in order]
    return tuple(outs)
```

```python
import math

import jax
import jax.numpy as jnp
from jax import lax
from jax.experimental import pallas as pl
from jax.experimental.pallas import tpu as pltpu

F32 = jnp.float32
BF16 = jnp.bfloat16
HIGHEST = lax.Precision.HIGHEST
MESH = pl.DeviceIdType.MESH

D_MODEL = 1024
SGU_WIDTH = 512
ATTN_WIDTH = 512
N_HEADS = 8
CHUNK = 128
D_FF = 4096
IN_WIDTH = 4616
N_DEV = 8
IN_SHARD = IN_WIDTH // N_DEV
IN_SHARD_PAD = 640
ZQKV_WIDTH = 2 * SGU_WIDTH + 3 * ATTN_WIDTH
GATE_OFFSET = ZQKV_WIDTH + N_HEADS
EPS = 1e-6
LANES = 128
SUBLANES = 8
VMEM_BYTES = 64 * 1024 * 1024
MIB = 1024 * 1024

ADAM_LR = 0.001
ADAM_B1 = 0.9
ADAM_B2 = 0.999
ADAM_EPS = 1e-08
ADAM_WD = 0.01
ADAM_STEP = 10

TOKEN_TILE = 512
MATMUL_TILE = 512
MERGE_FWD_TILE = 512
MERGE_BWD_TILE = 512
ATTN_TILE = 512
CUM_TILE = 256
SGU_TILE = 1024
WGRAD_TILE = 1024
NEG = -1e30

NT_DIMS = (((1,), (1,)), ((), ()))
TN_DIMS = (((0,), (0,)), ((), ()))


def _params(vmem_mb, n_grid=1):
    return pltpu.CompilerParams(
        dimension_semantics=("arbitrary",) * n_grid,
        vmem_limit_bytes=min(vmem_mb * MIB, VMEM_BYTES - 6 * MIB),
    )


def _dot(a, b):
    return jnp.dot(a, b, preferred_element_type=F32)


def _dot_nt(a, b):
    return lax.dot_general(a, b, NT_DIMS, preferred_element_type=F32)


def _dot_tn(a, b):
    return lax.dot_general(a, b, TN_DIMS, preferred_element_type=F32)


def _const_spec(shape):
    nd = len(shape)
    return pl.BlockSpec(shape, lambda *_: (0,) * nd, pipeline_mode=pl.Buffered(1))


def _row_spec(tm, n, col=0):
    return pl.BlockSpec((tm, n), lambda i: (i, col))


def _fold8(v):
    return v.reshape(v.shape[0] // SUBLANES, SUBLANES, v.shape[1]).sum(axis=0)


def _pick(v, lane_iota, k):
    return jnp.sum(jnp.where(lane_iota == k, v, 0.0), axis=1, keepdims=True)


def _iota(shape, dim):
    return lax.broadcasted_iota(jnp.int32, shape, dim)


def _gelu(x):
    c = 0.7978845608028654
    return 0.5 * x * (1.0 + jnp.tanh(c * (x + 0.044715 * x * x * x)))


def _gelu_grad(x):
    c = 0.7978845608028654
    t = jnp.tanh(c * (x + 0.044715 * x * x * x))
    return 0.5 * (1.0 + t) + 0.5 * x * (1.0 - t * t) * (c * (1.0 + 3.0 * 0.044715 * x * x))


def _rms_stats(v):
    r = lax.rsqrt(jnp.mean(v * v, axis=1, keepdims=True) + EPS)
    return r, v * r


def _rms_bwd(dout, vhat, r, g):
    a = dout * g
    return r * (a - vhat * jnp.mean(a * vhat, axis=1, keepdims=True))


def _mesh_pos():
    return lax.axis_index("x"), lax.axis_index("y"), lax.axis_index("c")


def _dev_index(px, py, pc):
    return 4 * px + 2 * py + pc


def _other_chips(x, y):
    return [(1 - x, y), (x, 1 - y), (1 - x, 1 - y)]


class _WeightGather:
    def __init__(self, shard_shapes, kinds, stage_shapes=None):
        self.shard_shapes = list(shard_shapes)
        self.kinds = list(kinds)
        self.stage_shapes = list(stage_shapes or shard_shapes)
        self.n = len(self.kinds)

    def out_shapes(self):
        shapes = []
        for (rows, cols), kind in zip(self.stage_shapes, self.kinds):
            full = {"block": (N_DEV, rows, cols), "rows": (N_DEV * rows, cols), "cols": (rows, N_DEV * cols)}[kind]
            shapes.append(jax.ShapeDtypeStruct(full, BF16))
        return shapes

    def scratch_shapes(self):
        return ([pltpu.VMEM(s, BF16) for s in self.stage_shapes]
                + [pltpu.SemaphoreType.DMA((self.n, 7)), pltpu.SemaphoreType.DMA((self.n, 7)),
                   pltpu.SemaphoreType.DMA((self.n,))])

    def _view(self, a, ref, j):
        rows, cols = self.stage_shapes[a]
        if self.kinds[a] == "block":
            return ref.at[j]
        if self.kinds[a] == "rows":
            return ref.at[pl.ds(pl.multiple_of(j * rows, rows), rows), :]
        return ref.at[:, pl.ds(pl.multiple_of(j * cols, cols), cols)]

    def _copy(self, outs, scratch, a, k, block, to, from_stage=False):
        stage, (send_sems, recv_sems, _) = scratch[:self.n], scratch[self.n:]
        dst = self._view(a, outs[a], _dev_index(*block))
        return pltpu.make_async_remote_copy(
            src_ref=stage[a] if from_stage else dst, dst_ref=dst,
            send_sem=send_sems.at[a, k], recv_sem=recv_sems.at[a, k],
            device_id=to, device_id_type=MESH)

    def _local(self, outs, scratch, a, me):
        return pltpu.make_async_copy(scratch[a], self._view(a, outs[a], _dev_index(*me)), scratch[-1].at[a])

    def start(self, ins, outs, scratch):
        x, y, c = _mesh_pos()
        me, sibling = (x, y, c), (x, y, 1 - c)
        for a in range(self.n):
            rows, cols = self.shard_shapes[a]
            if self.stage_shapes[a] != self.shard_shapes[a]:
                scratch[a][...] = jnp.zeros(self.stage_shapes[a], BF16)
            scratch[a][0:rows, 0:cols] = ins[a][...].astype(BF16)
            self._local(outs, scratch, a, me).start()
        for a in range(self.n):
            self._copy(outs, scratch, a, 0, me, sibling, True).start()
            for j, chip in enumerate(_other_chips(x, y)):
                self._copy(outs, scratch, a, 1 + j, me, (*chip, c), True).start()

    def forward(self, outs, scratch):
        x, y, c = _mesh_pos()
        me, sibling = (x, y, c), (x, y, 1 - c)
        for a in range(self.n):
            for j, chip in enumerate(_other_chips(x, y)):
                self._copy(outs, scratch, a, 1 + j, (*chip, c), me).wait_recv()
                self._copy(outs, scratch, a, 4 + j, (*chip, c), sibling).start()

    def finish(self, outs, scratch):
        x, y, c = _mesh_pos()
        me, sibling = (x, y, c), (x, y, 1 - c)
        chips = _other_chips(x, y)
        for a in range(self.n):
            self._copy(outs, scratch, a, 0, sibling, me).wait_recv()
            for j, chip in enumerate(chips):
                self._copy(outs, scratch, a, 4 + j, (*chip, 1 - c), me).wait_recv()
        for a in range(self.n):
            self._copy(outs, scratch, a, 0, me, sibling, True).wait_send()
            for j, chip in enumerate(chips):
                self._copy(outs, scratch, a, 1 + j, me, (*chip, c), True).wait_send()
                self._copy(outs, scratch, a, 4 + j, (*chip, c), sibling).wait_send()
            self._local(outs, scratch, a, me).wait()


def _gather_w_in(w_in_local):
    g = _WeightGather([(D_MODEL, IN_SHARD)], ["block"], [(D_MODEL, IN_SHARD_PAD)])

    def body(w_ref, out_ref, *scratch):
        g.start([w_ref], [out_ref], scratch)
        g.forward([out_ref], scratch)
        g.finish([out_ref], scratch)

    return pl.pallas_call(
        body,
        out_shape=g.out_shapes()[0],
        in_specs=[pl.BlockSpec(memory_space=pltpu.VMEM)],
        out_specs=pl.BlockSpec(memory_space=pl.ANY),
        scratch_shapes=g.scratch_shapes(),
        compiler_params=pltpu.CompilerParams(vmem_limit_bytes=32 * MIB),
        name="gather_w_in",
    )(w_in_local)


class _GradExchange:
    def __init__(self, shapes):
        self.shapes = [tuple(s) for s in shapes]
        self.n = len(self.shapes)

    def land_shapes(self, stage):
        slots, dtype = (4, F32) if stage == 1 else (3, BF16)
        return [jax.ShapeDtypeStruct((slots,) + s, dtype) for s in self.shapes]

    def sem_shapes(self, stage):
        slots = 4 if stage == 1 else 3
        return [pltpu.SemaphoreType.DMA((self.n, slots)), pltpu.SemaphoreType.DMA((self.n, slots))]

    def _copy(self, stage, srcs, lands, sems, a, k):
        x, y, c = _mesh_pos()
        cx, cy = (_other_chips(x, y) + [(x, y)])[k]
        if stage == 1:
            src, to = srcs[a].at[_dev_index(cx, cy, 1 - c)], (x, y, 1 - c)
        else:
            src, to = srcs[a].at[k], (cx, cy, c)
        return pltpu.make_async_remote_copy(
            src_ref=src, dst_ref=lands[a].at[k], send_sem=sems[0].at[a, k], recv_sem=sems[1].at[a, k],
            device_id=to, device_id_type=MESH)

    def start(self, stage, srcs, lands, sems):
        for a in range(self.n):
            for k in range(4 if stage == 1 else 3):
                self._copy(stage, srcs, lands, sems, a, k).start()

    def wait(self, stage, srcs, lands, sems):
        for a in range(self.n):
            for k in range(4 if stage == 1 else 3):
                cp = self._copy(stage, srcs, lands, sems, a, k)
                cp.wait_recv()
                cp.wait_send()


def _owner_indices():
    x, y, c = _mesh_pos()
    return jnp.stack([_dev_index(cx, cy, c) for cx, cy in _other_chips(x, y) + [(x, y)]]).astype(jnp.int32)


def _chip_partials(g, land1, idx, name):
    _, rows, cols = g.shape
    tr = min(rows, 256)

    def body(idx_ref, g_ref, l_ref, o_ref):
        o_ref[...] = (g_ref[...] + l_ref[...]).astype(BF16)

    return pl.pallas_call(
        body,
        grid_spec=pltpu.PrefetchScalarGridSpec(
            num_scalar_prefetch=1, grid=(3, rows // tr),
            in_specs=[pl.BlockSpec((None, tr, cols), lambda k, r, idx: (idx[k], r, 0)),
                      pl.BlockSpec((None, tr, cols), lambda k, r, idx: (k, r, 0))],
            out_specs=pl.BlockSpec((None, tr, cols), lambda k, r, idx: (k, r, 0))),
        out_shape=jax.ShapeDtypeStruct((3, rows, cols), BF16),
        compiler_params=_params(32, 2), name=name,
    )(idx, g, land1)


def _reduced_block(g_ref, l1_ref, a_ref, b_ref, c_ref):
    return ((g_ref[...] + l1_ref[...]) + a_ref[...].astype(F32)) + b_ref[...].astype(F32) + c_ref[...].astype(F32)


def _reduced_specs(tm, cols):
    return [pl.BlockSpec((None, tm, cols), lambda i, idx: (idx[3], i, 0)),
            pl.BlockSpec((None, tm, cols), lambda i, idx: (3, i, 0)),
            pl.BlockSpec((None, tm, cols), lambda i, idx: (0, i, 0)),
            pl.BlockSpec((None, tm, cols), lambda i, idx: (1, i, 0)),
            pl.BlockSpec((None, tm, cols), lambda i, idx: (2, i, 0))]


def _reduced_grad(g, land1, land2, idx, name):
    _, rows, cols = g.shape
    tm = min(rows, 256)

    def body(idx_ref, g_ref, l1_ref, a_ref, b_ref, c_ref, o_ref):
        o_ref[...] = _reduced_block(g_ref, l1_ref, a_ref, b_ref, c_ref)

    return pl.pallas_call(
        body,
        grid_spec=pltpu.PrefetchScalarGridSpec(
            num_scalar_prefetch=1, grid=(rows // tm,), in_specs=_reduced_specs(tm, cols),
            out_specs=pl.BlockSpec((tm, cols), lambda i, idx: (i, 0))),
        out_shape=jax.ShapeDtypeStruct((rows, cols), F32),
        compiler_params=_params(32), name=name,
    )(idx, g, land1, land2, land2, land2)


def _adamw_math(w, g, m, v):
    m = ADAM_B1 * m + (1.0 - ADAM_B1) * g
    v = ADAM_B2 * v + (1.0 - ADAM_B2) * (g * g)
    m_hat = m / (1.0 - ADAM_B1 ** ADAM_STEP)
    v_hat = v / (1.0 - ADAM_B2 ** ADAM_STEP)
    delta = -ADAM_LR * (m_hat / (jnp.sqrt(v_hat) + ADAM_EPS) + ADAM_WD * w)
    return delta, m, v


SMALL_NAMES = ("g_mix_pre", "b_forget", "g_sgu", "b_sgu", "w_spatial", "b_spatial", "g_mix_post", "g_ffn_pre",
               "g_ffn_post")
SMALL_SLOTS = {"g_mix_pre": (0, 1, 0, 1024), "g_mix_post": (1, 1, 0, 1024), "g_ffn_pre": (2, 1, 0, 1024),
               "g_ffn_post": (3, 1, 0, 1024), "g_sgu": (4, 1, 0, 512), "b_sgu": (4, 1, 512, 512),
               "b_forget": (5, 1, 0, 128), "b_spatial": (8, 8, 0, 128)}
SMALL_TILE = (16, 1024)
SPATIAL_TILE = (N_HEADS * CHUNK, CHUNK)


LOSS_SLOT = (8, 8, 128, 128)


def _allreduce_small(grads, loss_part):
    names = list(SMALL_NAMES)

    def body(*refs):
        g = dict(zip(names, refs[:len(names)]))
        loss_ref = refs[len(names)]
        tot_a, tot_b, buf_a, buf_b, sib_a, sib_b, ps_a, ps_b, land_a, land_b, send_sems, recv_sems = refs[len(names) + 1:]
        x, y, c = _mesh_pos()
        buf_a[...] = jnp.zeros(SMALL_TILE, F32)
        r0, nr, c0, nc = LOSS_SLOT
        buf_a[r0:r0 + nr, c0:c0 + nc] = loss_ref[...]
        for name, (r0, nr, c0, nc) in SMALL_SLOTS.items():
            val = g[name][...]
            if name == "b_spatial":
                val = jnp.transpose(val)[0:N_HEADS, :]
            buf_a[r0:r0 + nr, c0:c0 + nc] = val
        buf_b[...] = g["w_spatial"][...].reshape(SPATIAL_TILE)

        def swap(k, src, dst, to):
            return pltpu.make_async_remote_copy(src_ref=src, dst_ref=dst, send_sem=send_sems.at[k],
                                                recv_sem=recv_sems.at[k], device_id=to, device_id_type=MESH)

        first = [swap(0, buf_a, sib_a, (x, y, 1 - c)), swap(1, buf_b, sib_b, (x, y, 1 - c))]
        for cp in first:
            cp.start()
        for cp in first:
            cp.wait_recv()
        ps_a[...] = buf_a[...] + sib_a[...]
        ps_b[...] = buf_b[...] + sib_b[...]
        half = SPATIAL_TILE[0] // 2
        mine = pl.ds(pl.multiple_of(c * half, half), half)
        second = []
        for k, (cx, cy) in enumerate(_other_chips(x, y)):
            second += [swap(2 + 2 * k, ps_a, land_a.at[k], (cx, cy, c)),
                       swap(3 + 2 * k, ps_b.at[mine, :], land_b.at[k, mine, :], (cx, cy, c))]
        for cp in second:
            cp.start()
        for cp in second:
            cp.wait_recv()
        tot_a[...] = (ps_a[...] + land_a[0]) + (land_a[1] + land_a[2])
        tot_b[mine, :] = (ps_b[mine, :] + land_b[0, mine, :]) + (land_b[1, mine, :] + land_b[2, mine, :])
        third = swap(8, tot_b.at[mine, :], tot_b.at[mine, :], (x, y, 1 - c))
        third.start()
        third.wait_recv()
        for cp in first + second + [third]:
            cp.wait_send()

    vm = pl.BlockSpec(memory_space=pltpu.VMEM)
    return pl.pallas_call(
        body,
        out_shape=(jax.ShapeDtypeStruct(SMALL_TILE, F32), jax.ShapeDtypeStruct(SPATIAL_TILE, F32)),
        in_specs=[vm] * (len(names) + 1), out_specs=[vm, vm],
        scratch_shapes=[pltpu.VMEM(SMALL_TILE, F32), pltpu.VMEM(SPATIAL_TILE, F32)] * 3
        + [pltpu.VMEM((3,) + SMALL_TILE, F32), pltpu.VMEM((3,) + SPATIAL_TILE, F32),
           pltpu.SemaphoreType.DMA((9,)), pltpu.SemaphoreType.DMA((9,))],
        compiler_params=pltpu.CompilerParams(vmem_limit_bytes=32 * MIB),
        name="allreduce_small",
    )(*[grads[nm] for nm in names], loss_part)


def _adamw_small(tot_a, tot_b, ws, ms, vs):
    names = list(SMALL_NAMES)
    n = len(names)

    def body(a_ref, b_ref, *refs):
        w, m, v = (dict(zip(names, refs[i * n:(i + 1) * n])) for i in range(3))
        outs = [dict(zip(names, refs[(3 + i) * n:(4 + i) * n])) for i in range(4)]
        for name in names:
            if name == "w_spatial":
                g = b_ref[...].reshape(N_HEADS, CHUNK, CHUNK)
            else:
                r0, nr, c0, nc = SMALL_SLOTS[name]
                g = a_ref[r0:r0 + nr, c0:c0 + nc]
            vals = (g,) + _adamw_math(w[name][...], g, m[name][...], v[name][...])
            for out, val in zip(outs, vals):
                out[name][...] = val

    shapes = [jax.ShapeDtypeStruct(ws[nm].shape, F32) for nm in names]
    vm = pl.BlockSpec(memory_space=pltpu.VMEM)
    res = pl.pallas_call(
        body, out_shape=shapes * 4, in_specs=[vm] * (2 + 3 * n), out_specs=[vm] * (4 * n),
        compiler_params=pltpu.CompilerParams(vmem_limit_bytes=32 * MIB), name="adamw_small",
    )(tot_a, tot_b, *[d[nm] for d in (ws, ms, vs) for nm in names])
    return [dict(zip(names, res[i * n:(i + 1) * n])) for i in range(4)]


def _adamw_reduced(w, m, v, g, land1, land2, idx, name):
    rows, cols = w.shape
    tm = min(rows, 256)

    def body(idx_ref, w_ref, m_ref, v_ref, g_ref, l1_ref, a_ref, b_ref, c_ref, go_ref, d_ref, nm_ref, nv_ref):
        gsum = _reduced_block(g_ref, l1_ref, a_ref, b_ref, c_ref)
        go_ref[...] = gsum
        delta, nm, nv = _adamw_math(w_ref[...], gsum, m_ref[...], v_ref[...])
        d_ref[...] = delta
        nm_ref[...] = nm
        nv_ref[...] = nv

    sd = jax.ShapeDtypeStruct((rows, cols), F32)
    spec = pl.BlockSpec((tm, cols), lambda i, idx: (i, 0))
    return pl.pallas_call(
        body,
        grid_spec=pltpu.PrefetchScalarGridSpec(
            num_scalar_prefetch=1, grid=(rows // tm,), in_specs=[spec] * 3 + _reduced_specs(tm, cols),
            out_specs=[spec] * 4),
        out_shape=(sd, sd, sd, sd),
        compiler_params=_params(32), name=name,
    )(idx, w, m, v, g, land1, land2, land2, land2)


def _adamw(w, g, m, v, name):
    rows, cols = w.shape
    tm = 256 if rows % 256 == 0 else rows

    def body(w_ref, g_ref, m_ref, v_ref, d_ref, nm_ref, nv_ref):
        delta, nm, nv = _adamw_math(w_ref[...], g_ref[...], m_ref[...], v_ref[...])
        d_ref[...] = delta
        nm_ref[...] = nm
        nv_ref[...] = nv

    sd = jax.ShapeDtypeStruct((rows, cols), F32)
    spec = _row_spec(tm, cols)
    return pl.pallas_call(
        body, grid=(rows // tm,), out_shape=(sd, sd, sd), in_specs=[spec] * 4, out_specs=[spec] * 3,
        compiler_params=_params(32), name=name,
    )(w, g, m, v)


def _virtual_slab(sources, v0, v_end, like):
    lane = _iota(like.shape, 1)
    out = jnp.zeros(like.shape, like.dtype)
    for v_start, v_stop, read in sources:
        a, b = max(v0, v_start), min(v0 + LANES, v_stop, v_end)
        while a < b:
            c = a - v_start
            n = min(b - a, LANES - c % LANES)
            piece = read(c // LANES)
            shift = (a - v0 - c % LANES) % LANES
            if shift:
                piece = pltpu.roll(piece, shift, 1)
            out = jnp.where((lane >= a - v0) & (lane < a - v0 + n), piece, out)
            a += n
    return out


def _assemble_w_in(wg_in):
    tm = TOKEN_TILE

    def body(src_ref, wz_ref, wf_ref, wg_ref):
        like = src_ref[0, :, 0:LANES]
        sources = [(IN_SHARD * j, IN_SHARD * (j + 1),
                    (lambda k, j=j: src_ref[j, :, LANES * k:LANES * (k + 1)])) for j in range(N_DEV)]
        for k in range(ZQKV_WIDTH // LANES):
            wz_ref[:, LANES * k:LANES * (k + 1)] = _virtual_slab(sources, LANES * k, ZQKV_WIDTH, like)
        wf_ref[...] = _virtual_slab(sources, ZQKV_WIDTH, GATE_OFFSET, like)
        for k in range(2 * D_MODEL // LANES):
            wg_ref[:, LANES * k:LANES * (k + 1)] = _virtual_slab(sources, GATE_OFFSET + LANES * k, IN_WIDTH, like)

    return pl.pallas_call(
        body, grid=(D_MODEL // tm,),
        out_shape=(jax.ShapeDtypeStruct((D_MODEL, ZQKV_WIDTH), BF16), jax.ShapeDtypeStruct((D_MODEL, LANES), BF16),
                   jax.ShapeDtypeStruct((D_MODEL, 2 * D_MODEL), BF16)),
        in_specs=[pl.BlockSpec((N_DEV, tm, IN_SHARD_PAD), lambda i: (0, i, 0))],
        out_specs=[_row_spec(tm, ZQKV_WIDTH), _row_spec(tm, LANES), _row_spec(tm, 2 * D_MODEL)],
        compiler_params=_params(32), name="assemble_w_in",
    )(wg_in)


def _block_dw_in(pieces):
    tm = TOKEN_TILE
    widths = [2 * SGU_WIDTH, ATTN_WIDTH, ATTN_WIDTH, ATTN_WIDTH, N_HEADS, 2 * D_MODEL]
    starts = [sum(widths[:k]) for k in range(len(widths))]

    def body(*refs):
        in_refs, out_ref = refs[:-1], refs[-1]
        like = in_refs[0][:, 0:LANES]
        slab = lambda ref: (lambda k: ref[:, LANES * k:LANES * (k + 1)])
        sources = [(s, s + w, slab(ref)) for s, w, ref in zip(starts, widths, in_refs)]
        for j in range(N_DEV):
            for k in range(IN_SHARD_PAD // LANES):
                out_ref[j, :, LANES * k:LANES * (k + 1)] = _virtual_slab(
                    sources, IN_SHARD * j + LANES * k, IN_SHARD * (j + 1), like)

    return pl.pallas_call(
        body, grid=(D_MODEL // tm,),
        out_shape=jax.ShapeDtypeStruct((N_DEV, D_MODEL, IN_SHARD_PAD), F32),
        in_specs=[_row_spec(tm, pc.shape[1]) for pc in pieces],
        out_specs=pl.BlockSpec((N_DEV, tm, IN_SHARD_PAD), lambda i: (0, i, 0)),
        compiler_params=_params(32), name="block_dw_in",
    )(*pieces)


def _fwd_in(x2, g1, wz, wf, wg):
    T = x2.shape[0]
    tm = MATMUL_TILE

    def body(x_ref, g_ref, wz_ref, wf_ref, wg_ref, xn_ref, zuv_ref, qkv_ref, fl_ref, gt_ref):
        x = x_ref[...]
        r, xh = _rms_stats(x)
        xn = (xh * g_ref[...]).astype(BF16)
        xn_ref[...] = xn
        zuv_ref[...] = _dot(xn, wz_ref[:, 0:1024]).astype(BF16)
        qkv_ref[:, 0:512] = (_dot(xn, wz_ref[:, 1024:1536]) * 0.125).astype(BF16)
        qkv_ref[:, 512:1536] = _dot(xn, wz_ref[:, 1536:2560]).astype(BF16)
        fl_ref[...] = _dot(xn, wf_ref[...])
        gt_ref[...] = jax.nn.sigmoid(_dot(xn, wg_ref[...])).astype(BF16)

    return pl.pallas_call(
        body, grid=(T // tm,),
        out_shape=(jax.ShapeDtypeStruct((T, D_MODEL), BF16), jax.ShapeDtypeStruct((T, 1024), BF16),
                   jax.ShapeDtypeStruct((T, 1536), BF16), jax.ShapeDtypeStruct((T, LANES), F32),
                   jax.ShapeDtypeStruct((T, 2048), BF16)),
        in_specs=[_row_spec(tm, D_MODEL), _const_spec((1, D_MODEL)), _const_spec((D_MODEL, ZQKV_WIDTH)),
                  _const_spec((D_MODEL, LANES)), _const_spec((D_MODEL, 2048))],
        out_specs=[_row_spec(tm, D_MODEL), _row_spec(tm, 1024), _row_spec(tm, 1536), _row_spec(tm, LANES),
                   _row_spec(tm, 2048)],
        compiler_params=_params(48), name="fwd_in",
    )(x2, g1, wz, wf, wg)


def _log_sigmoid(f):
    return jnp.minimum(f, 0.0) - jnp.log1p(jnp.exp(-jnp.abs(f)))


AUG_LANES = 6


def _split3(v):
    hi = v.astype(BF16)
    r1 = v - hi.astype(F32)
    mid = r1.astype(BF16)
    lo = (r1 - mid.astype(F32)).astype(BF16)
    return hi, mid, lo


def _spread(parts, k0):
    n = len(parts)
    r, c = _iota((n * LANES, LANES), 0), _iota((n * LANES, LANES), 1)
    e = jnp.zeros((n * LANES, LANES), BF16)
    for i in range(n):
        h = r - i * LANES
        e = jnp.where((h >= 0) & (h < N_HEADS) & (c == AUG_LANES * h + (k0 + i)), jnp.ones_like(e), e)
    return _dot(jnp.concatenate(parts, axis=1), e)


def _aug_ones(shape, k0):
    lane = _iota(shape, 1)
    head = (lane * 43) >> 8
    slot = lane - AUG_LANES * head
    return ((lane < AUG_LANES * N_HEADS) & (slot >= k0) & (slot < k0 + 3)).astype(F32)


def _aug_query(v):
    return (_spread(_split3(v), 0) + _aug_ones(v.shape, 3)).astype(BF16)


def _aug_key(v):
    return (_aug_ones(v.shape, 0) - _spread(_split3(v), 3)).astype(BF16)


def _aug_stack(t2, aug, p):
    lane = _iota(t2.shape, 1)
    low = lane < 64
    zero = jnp.zeros_like(t2)
    first = 2 * AUG_LANES * p
    a_e = jnp.where((lane >= first) & (lane < first + AUG_LANES), aug, zero)
    a_o = jnp.where((lane >= first + AUG_LANES) & (lane < first + 2 * AUG_LANES), aug, zero)
    top = jnp.concatenate([jnp.where(low, t2, zero), a_e], axis=1)
    bot = jnp.concatenate([jnp.where(low, zero, t2), a_o], axis=1)
    return jnp.concatenate([top, bot], axis=0)


def _fwd_cum(fl, bfp):
    T = fl.shape[0]
    tb = CUM_TILE

    def body(fl_ref, b_ref, cc_ref, qa_ref, ka_ref):
        tri = (_iota((tb, tb), 0) >= _iota((tb, tb), 1)).astype(F32)
        carry = jnp.zeros((1, LANES), F32)
        for i in range(T // tb):
            rows = slice(i * tb, (i + 1) * tb)
            lf = _log_sigmoid(fl_ref[rows, :] + b_ref[...])
            cs = jnp.dot(tri, lf, precision=HIGHEST, preferred_element_type=F32) + carry
            cc_ref[rows, :] = cs
            carry = cs[tb - 1:tb, :]
            qa_ref[rows, :] = _aug_query(cs)
            ka_ref[rows, :] = _aug_key(cs)

    return pl.pallas_call(
        body,
        out_shape=(jax.ShapeDtypeStruct((T, LANES), F32), jax.ShapeDtypeStruct((T, LANES), BF16),
                   jax.ShapeDtypeStruct((T, LANES), BF16)),
        compiler_params=pltpu.CompilerParams(vmem_limit_bytes=32 * MIB), name="fwd_cum",
    )(fl, bfp)


def _sgu_forward_parts(z, gs, bs):
    u = _gelu(z[:, :SGU_WIDTH])
    vv = _gelu(z[:, SGU_WIDTH:])
    vc = vv - jnp.mean(vv, axis=1, keepdims=True)
    rs = lax.rsqrt(jnp.mean(vc * vc, axis=1, keepdims=True) + EPS)
    vhat = vc * rs
    return u, vhat, rs, vhat * gs + bs


def _sgu_pair_weights(w_ref, bT, p):
    tril = _iota((CHUNK, CHUNK), 0) >= _iota((CHUNK, CHUNK), 1)
    we = jnp.where(tril, w_ref[2 * p], 0.0).astype(BF16)
    wo = jnp.where(tril, w_ref[2 * p + 1], 0.0).astype(BF16)
    lane8 = _iota(bT.shape, 1)
    low = _iota((CHUNK, LANES), 1) < 64
    b2 = jnp.where(low, _pick(bT, lane8, 2 * p), _pick(bT, lane8, 2 * p + 1))
    return we, wo, b2


def _chunks_on_lanes(v, p, nc):
    return jnp.concatenate([v[c * CHUNK:(c + 1) * CHUNK, LANES * p:LANES * (p + 1)] for c in range(nc)], axis=1)


def _sgu_mix(we, wo, b2, vcat, nc):
    low = (_iota((CHUNK, nc * LANES), 1) % LANES) < 64
    return jnp.where(low, _dot(we, vcat), _dot(wo, vcat)) + jnp.concatenate([b2] * nc, axis=1)


def _fwd_sgu(zuv, gs, bs, wsp, bT):
    T = zuv.shape[0]
    tc = SGU_TILE
    nc = tc // CHUNK

    def body(z_ref, gs_ref, bs_ref, w_ref, bT_ref, y_ref):
        u, _, _, vln = _sgu_forward_parts(z_ref[...].astype(F32), gs_ref[...], bs_ref[...])
        vb = vln.astype(BF16)
        for p in range(4):
            we, wo, b2 = _sgu_pair_weights(w_ref, bT_ref[...], p)
            s = _sgu_mix(we, wo, b2, _chunks_on_lanes(vb, p, nc), nc)
            for c in range(nc):
                rows, cols = slice(c * CHUNK, (c + 1) * CHUNK), slice(LANES * p, LANES * (p + 1))
                y_ref[rows, cols] = (u[rows, cols] * s[:, c * LANES:(c + 1) * LANES]).astype(BF16)

    return pl.pallas_call(
        body, grid=(T // tc,), out_shape=jax.ShapeDtypeStruct((T, SGU_WIDTH), BF16),
        in_specs=[_row_spec(tc, 1024), _const_spec((1, SGU_WIDTH)), _const_spec((1, SGU_WIDTH)),
                  _const_spec((8, CHUNK, CHUNK)), _const_spec((CHUNK, 8))],
        out_specs=_row_spec(tc, SGU_WIDTH),
        compiler_params=_params(40), name="fwd_sgu",
    )(zuv, gs, bs, wsp, bT)


def _fwd_attn(qkv, qaug, kaug, w_shards):
    T = qkv.shape[0]
    tq = tk = ATTN_TILE
    nq = T // tq
    gather = _WeightGather([w.shape for w in w_shards], ["cols", "cols", "rows", "cols", "rows"])
    nw = gather.n

    def body(q_ref, qa_ref, k_ref, v_ref, ka_ref, *rest):
        w_refs, (o_ref, lse_ref), wg_refs, scratch = rest[:nw], rest[nw:nw + 2], rest[nw + 2:2 * nw + 2], rest[2 * nw + 2:]
        i = pl.program_id(0)

        @pl.when(i == 0)
        def _():
            gather.start(w_refs, wg_refs, scratch)

        @pl.when(i == nq // 2)
        def _():
            gather.forward(wg_refs, scratch)

        lane = _iota((tq, LANES), 1)
        low = lane < 64
        lowk = _iota((tk, LANES), 1) < 64
        one = jnp.ones((tk, LANES), BF16)
        row = _iota((2 * tq, tk), 0) % tq
        col = _iota((2 * tq, tk), 1)
        cols = [slice(LANES * p, LANES * (p + 1)) for p in range(4)]
        qa = qa_ref[...]
        qs = [_aug_stack(q_ref[:, cols[p]], qa, p) for p in range(4)]

        def step(j, carry, masked):
            ks = pl.ds(pl.multiple_of(j * tk, tk), tk)
            ka = ka_ref[ks, :]
            new = []
            for p in range(4):
                m, acc_e, acc_o = carry[p]
                v2 = v_ref[ks, cols[p]]
                s = _dot_nt(qs[p], jnp.concatenate([k_ref[ks, cols[p]], ka], axis=1))
                if masked:
                    s = jnp.where(col <= row, s, NEG)
                mn = jnp.maximum(m, jnp.max(s, axis=1, keepdims=True))
                al = jnp.exp(m - mn)
                pm = jnp.exp(s - mn).astype(BF16)
                acc_e = al[:tq] * acc_e + _dot(pm[:tq], jnp.where(lowk, v2, one))
                acc_o = al[tq:] * acc_o + _dot(pm[tq:], jnp.where(lowk, one, v2))
                new.append((mn, acc_e, acc_o))
            return tuple(new)

        init = tuple((jnp.full((2 * tq, 1), NEG, F32), jnp.zeros((tq, LANES), F32), jnp.zeros((tq, LANES), F32))
                     for _ in range(4))
        def trip(t, c):
            for u in range(3):
                c = step(3 * t + u, c, False)
            return c

        carry = lax.fori_loop(0, i // 3, trip, init)
        carry = lax.fori_loop(3 * (i // 3), i, lambda j, c: step(j, c, False), carry)
        carry = step(i, carry, True)
        lse_blk = jnp.zeros((tq, LANES), F32)
        for p in range(4):
            m, acc_e, acc_o = carry[p]
            l_e = pltpu.roll(acc_e, 64, 1)
            l_o = pltpu.roll(acc_o, 64, 1)
            o_ref[:, cols[p]] = jnp.where(low, acc_e / l_e, acc_o / l_o).astype(BF16)
            lse_blk = jnp.where(lane == 2 * p, m[:tq] + jnp.log(l_e), lse_blk)
            lse_blk = jnp.where(lane == 2 * p + 1, m[tq:] + jnp.log(acc_o), lse_blk)
        lse_ref[...] = lse_blk

        @pl.when(i == nq - 1)
        def _():
            gather.finish(wg_refs, scratch)

    return pl.pallas_call(
        body, grid=(nq,),
        out_shape=[jax.ShapeDtypeStruct((T, ATTN_WIDTH), BF16), jax.ShapeDtypeStruct((T, LANES), F32)]
        + gather.out_shapes(),
        in_specs=[_row_spec(tq, 512), _row_spec(tq, LANES),
                  pl.BlockSpec((T, 512), lambda i: (0, 1), pipeline_mode=pl.Buffered(1)),
                  pl.BlockSpec((T, 512), lambda i: (0, 2), pipeline_mode=pl.Buffered(1)),
                  _const_spec((T, LANES))] + [_const_spec(w.shape) for w in w_shards],
        out_specs=[_row_spec(tq, ATTN_WIDTH), _row_spec(tq, LANES)] + [pl.BlockSpec(memory_space=pl.ANY)] * nw,
        scratch_shapes=gather.scratch_shapes(),
        compiler_params=_params(58), name="fwd_attn",
    )(qkv, qaug, qkv, qkv, kaug, *w_shards)


def _fwd_merge(ys, ya, gt, x2, wbs, wba, wo, g2):
    T = x2.shape[0]
    tm = min(T, MERGE_FWD_TILE)

    def body(ys_ref, ya_ref, gt_ref, x_ref, wbs_ref, wba_ref, wo_ref, g2_ref, a_ref, b_ref, mg_ref, o_ref, h1_ref):
        A = _dot(ys_ref[...], wbs_ref[...])
        B = _dot(ya_ref[...], wba_ref[...])
        mg = (gt_ref[:, :D_MODEL].astype(F32) * A + gt_ref[:, D_MODEL:].astype(F32) * B).astype(BF16)
        o = _dot(mg, wo_ref[...])
        r2, oh = _rms_stats(o)
        a_ref[...] = A.astype(BF16)
        b_ref[...] = B.astype(BF16)
        mg_ref[...] = mg
        o_ref[...] = o.astype(BF16)
        h1_ref[...] = x_ref[...] + oh * g2_ref[...]

    sd = jax.ShapeDtypeStruct((T, D_MODEL), BF16)
    return pl.pallas_call(
        body, grid=(T // tm,),
        out_shape=(sd, sd, sd, sd, jax.ShapeDtypeStruct((T, D_MODEL), F32)),
        in_specs=[_row_spec(tm, 512), _row_spec(tm, 512), _row_spec(tm, 2048), _row_spec(tm, D_MODEL),
                  _const_spec((512, D_MODEL)), _const_spec((512, D_MODEL)), _const_spec((D_MODEL, D_MODEL)),
                  _const_spec((1, D_MODEL))],
        out_specs=[_row_spec(tm, D_MODEL)] * 5,
        compiler_params=_params(58), name="fwd_merge",
    )(ys, ya, gt, x2, wbs, wba, wo, g2)


def _fwd_ffn_loss(h1, tgt, wup, wdn, g3, g4):
    T = h1.shape[0]
    tm = MATMUL_TILE
    nsteps = T // tm

    def body(h1_ref, tg_ref, wup_ref, wdn_ref, g3_ref, g4_ref, xn2_ref, a_ref, ddn_ref, dy_ref, loss_ref,
             dg4_ref, acc_l, acc_g):
        i = pl.program_id(0)

        @pl.when(i == 0)
        def _():
            acc_l[...] = jnp.zeros_like(acc_l)
            acc_g[...] = jnp.zeros_like(acc_g)

        h1v = h1_ref[...]
        r3, h1h = _rms_stats(h1v)
        xn2 = (h1h * g3_ref[...]).astype(BF16)
        xn2_ref[...] = xn2
        dn = jnp.zeros((tm, D_MODEL), F32)
        for j in range(D_FF // 1024):
            cols = slice(1024 * j, 1024 * (j + 1))
            a = _dot(xn2, wup_ref[:, cols])
            a_ref[:, cols] = a.astype(BF16)
            hid = jnp.square(jnp.maximum(a, 0.0)).astype(BF16)
            dn = dn + _dot(hid, wdn_ref[cols, :])
        r4, dnh = _rms_stats(dn)
        g4v = g4_ref[...]
        e = (h1v + dnh * g4v) - tg_ref[...]
        sq = e * e
        s1 = sq[:, 0:LANES]
        for j in range(1, D_MODEL // LANES):
            s1 = s1 + sq[:, LANES * j:LANES * (j + 1)]
        acc_l[...] += _fold8(s1)
        dy = e * (1.0 / D_MODEL)
        dy_ref[...] = dy
        acc_g[...] += _fold8(dy * dnh)
        ddn_ref[...] = _rms_bwd(dy, dnh, r4, g4v).astype(BF16)

        @pl.when(i == nsteps - 1)
        def _():
            loss_ref[...] = acc_l[...] * (0.5 / D_MODEL)
            dg4_ref[...] = jnp.sum(acc_g[...], axis=0, keepdims=True)

    return pl.pallas_call(
        body, grid=(nsteps,),
        out_shape=(jax.ShapeDtypeStruct((T, D_MODEL), BF16), jax.ShapeDtypeStruct((T, D_FF), BF16),
                   jax.ShapeDtypeStruct((T, D_MODEL), BF16), jax.ShapeDtypeStruct((T, D_MODEL), F32),
                   jax.ShapeDtypeStruct((SUBLANES, LANES), F32), jax.ShapeDtypeStruct((1, D_MODEL), F32)),
        in_specs=[_row_spec(tm, D_MODEL), _row_spec(tm, D_MODEL), _const_spec((D_MODEL, D_FF)),
                  _const_spec((D_FF, D_MODEL)), _const_spec((1, D_MODEL)), _const_spec((1, D_MODEL))],
        out_specs=[_row_spec(tm, D_MODEL), _row_spec(tm, D_FF), _row_spec(tm, D_MODEL), _row_spec(tm, D_MODEL),
                   pl.BlockSpec((SUBLANES, LANES), lambda i: (0, 0)), pl.BlockSpec((1, D_MODEL), lambda i: (0, 0))],
        scratch_shapes=[pltpu.VMEM((SUBLANES, LANES), F32), pltpu.VMEM((SUBLANES, D_MODEL), F32)],
        compiler_params=_params(52), name="fwd_ffn_loss",
    )(h1, tgt, wup, wdn, g3, g4)


def _bwd_ffn(ddn, a, dy, h1, wup, wdn, g3):
    T = h1.shape[0]
    tm = MATMUL_TILE
    nsteps = T // tm

    def body(ddn_ref, a_ref, dy_ref, h1_ref, wup_ref, wdn_ref, g3_ref, da_ref, dh1_ref, dg3_ref, acc_g):
        i = pl.program_id(0)

        @pl.when(i == 0)
        def _():
            acc_g[...] = jnp.zeros_like(acc_g)

        ddnv = ddn_ref[...]
        dxn2 = jnp.zeros((tm, D_MODEL), F32)
        for j in range(D_FF // 1024):
            cols = slice(1024 * j, 1024 * (j + 1))
            dhid = _dot_nt(ddnv, wdn_ref[cols, :])
            da = (dhid * (2.0 * jnp.maximum(a_ref[:, cols].astype(F32), 0.0))).astype(BF16)
            da_ref[:, cols] = da
            dxn2 = dxn2 + _dot_nt(da, wup_ref[:, cols])
        r3, h1h = _rms_stats(h1_ref[...])
        acc_g[...] += _fold8(dxn2 * h1h)
        dh1_ref[...] = dy_ref[...] + _rms_bwd(dxn2, h1h, r3, g3_ref[...])

        @pl.when(i == nsteps - 1)
        def _():
            dg3_ref[...] = jnp.sum(acc_g[...], axis=0, keepdims=True)

    return pl.pallas_call(
        body, grid=(nsteps,),
        out_shape=(jax.ShapeDtypeStruct((T, D_FF), BF16), jax.ShapeDtypeStruct((T, D_MODEL), F32),
                   jax.ShapeDtypeStruct((1, D_MODEL), F32)),
        in_specs=[_row_spec(tm, D_MODEL), _row_spec(tm, D_FF), _row_spec(tm, D_MODEL), _row_spec(tm, D_MODEL),
                  _const_spec((D_MODEL, D_FF)), _const_spec((D_FF, D_MODEL)), _const_spec((1, D_MODEL))],
        out_specs=[_row_spec(tm, D_FF), _row_spec(tm, D_MODEL), pl.BlockSpec((1, D_MODEL), lambda i: (0, 0))],
        scratch_shapes=[pltpu.VMEM((SUBLANES, D_MODEL), F32)],
        compiler_params=_params(52), name="bwd_ffn",
    )(ddn, a, dy, h1, wup, wdn, g3)


def _wgrad(xa, dy, name, relu2=False, tn=None, block_cols=None):
    T, K = xa.shape
    N = dy.shape[1]
    tn = N if tn is None else tn
    tt = min(T, WGRAD_TILE if K <= D_MODEL else WGRAD_TILE // 2)
    if block_cols:
        nb = tn // block_cols
        out_shape = jax.ShapeDtypeStruct((N // block_cols, K, block_cols), F32)
        out_spec = pl.BlockSpec((nb, K, block_cols), lambda n, t: (n, 0, 0))
    else:
        out_shape = jax.ShapeDtypeStruct((K, N), F32)
        out_spec = pl.BlockSpec((K, tn), lambda n, t: (0, n))

    def body(x_ref, dy_ref, o_ref):
        @pl.when(pl.program_id(1) == 0)
        def _():
            o_ref[...] = jnp.zeros_like(o_ref)

        xv = x_ref[...]
        if relu2:
            xv = jnp.square(jnp.maximum(xv.astype(F32), 0.0)).astype(BF16)
        if block_cols:
            for b in range(nb):
                o_ref[b] += _dot_tn(xv, dy_ref[:, block_cols * b:block_cols * (b + 1)])
        else:
            o_ref[...] += _dot_tn(xv, dy_ref[...])

    return pl.pallas_call(
        body, grid=(N // tn, T // tt), out_shape=out_shape,
        in_specs=[pl.BlockSpec((tt, K), lambda n, t: (t, 0)), pl.BlockSpec((tt, tn), lambda n, t: (t, n))],
        out_specs=out_spec,
        compiler_params=_params(52, 2), name=name,
    )(xa, dy)


def _wgrad_multi(xa, dys, name):
    T, K = xa.shape
    tt = min(T, WGRAD_TILE)
    n = len(dys)

    def body(x_ref, *refs):
        dy_refs, o_refs = refs[:n], refs[n:]

        @pl.when(pl.program_id(0) == 0)
        def _():
            for o_ref in o_refs:
                o_ref[...] = jnp.zeros_like(o_ref)

        xv = x_ref[...]
        for dy_ref, o_ref in zip(dy_refs, o_refs):
            o_ref[...] += _dot_tn(xv, dy_ref[...])

    return pl.pallas_call(
        body, grid=(T // tt,),
        out_shape=[jax.ShapeDtypeStruct((K, dy.shape[1]), F32) for dy in dys],
        in_specs=[_row_spec(tt, K)] + [_row_spec(tt, dy.shape[1]) for dy in dys],
        out_specs=[pl.BlockSpec((K, dy.shape[1]), lambda t: (0, 0)) for dy in dys],
        compiler_params=_params(52), name=name,
    )(xa, *dys)


def _bwd_merge(dh1, o, A, B, gt, ys, ya, mg, lse, cc, wbs, wba, wo, g2):
    T = dh1.shape[0]
    tm = MERGE_BWD_TILE
    nsteps = T // tm

    def body(dh1_ref, o_ref, a_ref, b_ref, gt_ref, ys_ref, ya_ref, mg_ref, lse_ref, cc_ref, wbs_ref, wba_ref,
             wo_ref, g2_ref, dgl_ref, dys_ref, dya_ref, qab_ref, dab_ref, dg2_ref, dwbs_ref, dwba_ref, dwo_ref,
             acc_g):
        i = pl.program_id(0)

        @pl.when(i == 0)
        def _():
            acc_g[...] = jnp.zeros_like(acc_g)
            dwbs_ref[...] = jnp.zeros_like(dwbs_ref)
            dwba_ref[...] = jnp.zeros_like(dwba_ref)
            dwo_ref[...] = jnp.zeros_like(dwo_ref)

        dh1v = dh1_ref[...]
        r2, oh = _rms_stats(o_ref[...].astype(F32))
        acc_g[...] += _fold8(dh1v * oh)
        do = _rms_bwd(dh1v, oh, r2, g2_ref[...]).astype(BF16)
        dwo_ref[...] += _dot_tn(mg_ref[...], do)
        dmg = _dot_nt(do, wo_ref[...])
        ga = gt_ref[:, :D_MODEL].astype(F32)
        gb = gt_ref[:, D_MODEL:].astype(F32)
        dgl_ref[:, :D_MODEL] = (dmg * a_ref[...].astype(F32) * ga * (1.0 - ga)).astype(BF16)
        dgl_ref[:, D_MODEL:] = (dmg * b_ref[...].astype(F32) * gb * (1.0 - gb)).astype(BF16)
        dA = (dmg * ga).astype(BF16)
        dB = (dmg * gb).astype(BF16)
        dwbs_ref[...] += _dot_tn(ys_ref[...], dA)
        dwba_ref[...] += _dot_tn(ya_ref[...], dB)
        dys_ref[...] = _dot_nt(dA, wbs_ref[...]).astype(BF16)
        dya = _dot_nt(dB, wba_ref[...]).astype(BF16)
        dya_ref[...] = dya
        prod = dya.astype(F32) * ya_ref[...].astype(F32)
        lane = _iota((tm, LANES), 1)
        low = lane < 64
        blk = jnp.zeros((tm, LANES), F32)
        for p in range(4):
            pp = prod[:, LANES * p:LANES * (p + 1)]
            blk = jnp.where(lane == 2 * p, jnp.sum(jnp.where(low, pp, 0.0), axis=1, keepdims=True), blk)
            blk = jnp.where(lane == 2 * p + 1, jnp.sum(jnp.where(low, 0.0, pp), axis=1, keepdims=True), blk)
        qab_ref[...] = _aug_query(cc_ref[...] - lse_ref[...])
        dab_ref[...] = _spread(_split3(-blk), 0).astype(BF16)

        @pl.when(i == nsteps - 1)
        def _():
            dg2_ref[...] = jnp.sum(acc_g[...], axis=0, keepdims=True)

    sh = jax.ShapeDtypeStruct((T, 512), BF16)
    sa = jax.ShapeDtypeStruct((T, LANES), BF16)
    sw = jax.ShapeDtypeStruct((512, D_MODEL), F32)
    whole = lambda shape: pl.BlockSpec(shape, lambda i: (0, 0))
    return pl.pallas_call(
        body, grid=(nsteps,),
        out_shape=(jax.ShapeDtypeStruct((T, 2048), BF16), sh, sh, sa, sa, jax.ShapeDtypeStruct((1, D_MODEL), F32),
                   sw, sw, jax.ShapeDtypeStruct((D_MODEL, D_MODEL), F32)),
        in_specs=[_row_spec(tm, D_MODEL)] * 4 + [_row_spec(tm, 2048), _row_spec(tm, 512), _row_spec(tm, 512),
                  _row_spec(tm, D_MODEL), _row_spec(tm, LANES), _row_spec(tm, LANES),
                  _const_spec((512, D_MODEL)), _const_spec((512, D_MODEL)),
                  _const_spec((D_MODEL, D_MODEL)), _const_spec((1, D_MODEL))],
        out_specs=[_row_spec(tm, 2048), _row_spec(tm, 512), _row_spec(tm, 512), _row_spec(tm, LANES),
                   _row_spec(tm, LANES), whole((1, D_MODEL)), whole((512, D_MODEL)), whole((512, D_MODEL)),
                   whole((D_MODEL, D_MODEL))],
        scratch_shapes=[pltpu.VMEM((SUBLANES, D_MODEL), F32)],
        compiler_params=_params(56), name="bwd_merge",
    )(dh1, o, A, B, gt, ys, ya, mg, lse, cc, wbs, wba, wo, g2)


def _bwd_sgu(zuv, dys, gs, bs, wsp, bT, grads):
    T = zuv.shape[0]
    tc = SGU_TILE
    nc = tc // CHUNK
    nsteps = T // tc
    ex = _GradExchange([tuple(g.shape[1:]) for g in grads])
    ng = ex.n

    def body(z_ref, dy_ref, gs_ref, bs_ref, w_ref, bT_ref, *rest):
        g_refs, (dz_ref, dw_ref, dbT_ref, dgs_ref, dbs_ref) = rest[:ng], rest[ng:ng + 5]
        land1 = rest[ng + 5:2 * ng + 5]
        acc_w, acc_b, acc_gs, acc_bs, dvln_s = rest[2 * ng + 5:2 * ng + 10]
        ex_sems = rest[2 * ng + 10:]
        i = pl.program_id(0)

        @pl.when(i == 0)
        def _():
            ex.start(1, g_refs, land1, ex_sems)
            acc_w[...] = jnp.zeros_like(acc_w)
            acc_b[...] = jnp.zeros_like(acc_b)
            acc_gs[...] = jnp.zeros_like(acc_gs)
            acc_bs[...] = jnp.zeros_like(acc_bs)

        z = z_ref[...].astype(F32)
        gsv = gs_ref[...]
        u, vhat, rs, vln = _sgu_forward_parts(z, gsv, bs_ref[...])
        vb = vln.astype(BF16)
        dy = dy_ref[...].astype(F32)
        low_w = (_iota((CHUNK, nc * LANES), 1) % LANES) < 64
        for p in range(4):
            we, wo, b2 = _sgu_pair_weights(w_ref, bT_ref[...], p)
            vcat = _chunks_on_lanes(vb, p, nc)
            s = _sgu_mix(we, wo, b2, vcat, nc)
            dyc = _chunks_on_lanes(dy, p, nc)
            ds = dyc * _chunks_on_lanes(u, p, nc)
            dsb = ds.astype(BF16)
            zero = jnp.zeros_like(dsb)
            dse = jnp.where(low_w, dsb, zero)
            dso = jnp.where(low_w, zero, dsb)
            acc_w[2 * p] += _dot_nt(dse, vcat)
            acc_w[2 * p + 1] += _dot_nt(dso, vcat)
            acc_b[p] += ds
            dvl = jnp.where(low_w, _dot_tn(we, dsb), _dot_tn(wo, dsb))
            for c in range(nc):
                rows, cols = slice(c * CHUNK, (c + 1) * CHUNK), slice(LANES * p, LANES * (p + 1))
                dvln_s[rows, cols] = dvl[:, c * LANES:(c + 1) * LANES]
                du = dy[rows, cols] * s[:, c * LANES:(c + 1) * LANES]
                dz_ref[rows, cols] = (du * _gelu_grad(z[rows, cols])).astype(BF16)
        dvln = dvln_s[...]
        acc_gs[...] += _fold8(dvln * vhat)
        acc_bs[...] += _fold8(dvln)
        al = dvln * gsv
        dvv = rs * (al - jnp.mean(al, axis=1, keepdims=True) - vhat * jnp.mean(al * vhat, axis=1, keepdims=True))
        dz_ref[:, SGU_WIDTH:] = (dvv * _gelu_grad(z[:, SGU_WIDTH:])).astype(BF16)

        @pl.when(i == nsteps - 1)
        def _():
            tril = _iota((CHUNK, CHUNK), 0) >= _iota((CHUNK, CHUNK), 1)
            lane = _iota((CHUNK, LANES), 1)
            low = lane < 64
            blk = jnp.zeros((CHUNK, LANES), F32)
            for g in range(8):
                dw_ref[g] = jnp.where(tril, acc_w[g], 0.0)
            for p in range(4):
                t = acc_b[p]
                tot = t[:, 0:LANES]
                for c in range(1, nc):
                    tot = tot + t[:, c * LANES:(c + 1) * LANES]
                blk = jnp.where(lane == 2 * p, jnp.sum(jnp.where(low, tot, 0.0), axis=1, keepdims=True), blk)
                blk = jnp.where(lane == 2 * p + 1, jnp.sum(jnp.where(low, 0.0, tot), axis=1, keepdims=True), blk)
            dbT_ref[...] = blk
            dgs_ref[...] = jnp.sum(acc_gs[...], axis=0, keepdims=True)
            dbs_ref[...] = jnp.sum(acc_bs[...], axis=0, keepdims=True)
            ex.wait(1, g_refs, land1, ex_sems)

    whole = lambda shape: pl.BlockSpec(shape, lambda i: (0,) * len(shape))
    hbm_spec = pl.BlockSpec(memory_space=pl.ANY)
    return pl.pallas_call(
        body, grid=(nsteps,),
        out_shape=[jax.ShapeDtypeStruct((T, 1024), BF16), jax.ShapeDtypeStruct((8, CHUNK, CHUNK), F32),
                   jax.ShapeDtypeStruct((CHUNK, LANES), F32), jax.ShapeDtypeStruct((1, SGU_WIDTH), F32),
                   jax.ShapeDtypeStruct((1, SGU_WIDTH), F32)] + ex.land_shapes(1),
        in_specs=[_row_spec(tc, 1024), _row_spec(tc, SGU_WIDTH), _const_spec((1, SGU_WIDTH)),
                  _const_spec((1, SGU_WIDTH)), _const_spec((8, CHUNK, CHUNK)), _const_spec((CHUNK, 8))]
        + [hbm_spec] * ng,
        out_specs=[_row_spec(tc, 1024), whole((8, CHUNK, CHUNK)), whole((CHUNK, LANES)),
                   whole((1, SGU_WIDTH)), whole((1, SGU_WIDTH))] + [hbm_spec] * ng,
        scratch_shapes=[pltpu.VMEM((8, CHUNK, CHUNK), F32), pltpu.VMEM((4, CHUNK, nc * LANES), F32),
                        pltpu.VMEM((SUBLANES, SGU_WIDTH), F32), pltpu.VMEM((SUBLANES, SGU_WIDTH), F32),
                        pltpu.VMEM((tc, SGU_WIDTH), F32)] + ex.sem_shapes(1),
        compiler_params=_params(48), name="bwd_sgu",
    )(zuv, dys, gs, bs, wsp, bT, *grads)


def _bwd_attn(qkv, dya, qab, dab, kaug, parts):
    T = qkv.shape[0]
    tq = tk = ATTN_TILE
    nq = T // tq
    nk = T // tk
    ex = _GradExchange([tuple(g.shape[1:]) for g in parts])
    nr = ex.n

    def body(q_ref, do_ref, qa_ref, da_ref, k_ref, v_ref, ka_ref, *rest):
        part_refs, (dq_ref, dk_ref, dv_ref, dcx_ref) = rest[:nr], rest[nr:nr + 4]
        land2, dq_acc, ex_sems = rest[nr + 4:2 * nr + 4], rest[2 * nr + 4], rest[2 * nr + 5:]
        p = pl.program_id(0)
        j = pl.program_id(1)

        @pl.when((p == 0) & (j == 0))
        def _():
            ex.start(2, part_refs, land2, ex_sems)

        lane = _iota((tq, LANES), 1)
        low = lane < 64
        row = _iota((2 * tq, tk), 0) % tq
        col = _iota((2 * tq, tk), 1)
        first = 2 * AUG_LANES * p
        half = tq // 2

        @pl.when(j == 0)
        def _():
            dq_acc[...] = jnp.zeros_like(dq_acc)

        @pl.when((j == 0) & (p == 0))
        def _():
            dcx_ref[...] = jnp.zeros_like(dcx_ref)

        ka = ka_ref[...]
        kk = jnp.concatenate([k_ref[...], ka], axis=1)
        vv = jnp.concatenate([v_ref[...], ka], axis=1)

        def tile(i, carry, r0, c0, nc, masked):
            dk_a, dv_a = carry
            nr = tq - r0
            qsl = pl.ds(pl.multiple_of(i * tq + r0, half), nr)
            qs = _aug_stack(q_ref[qsl, :], qa_ref[qsl, :], p)
            dos = _aug_stack(do_ref[qsl, :], da_ref[qsl, :], p)
            kc, vc = kk[c0:c0 + nc], vv[c0:c0 + nc]
            s = _dot_nt(qs, kc)
            if masked:
                s = jnp.where(col[:2 * nr, :nc] + c0 <= row[:2 * nr, :nc] % nr + r0, s, NEG)
            pm = jnp.exp(s)
            ds = pm * _dot_nt(dos, vc)
            dsb = ds.astype(BF16)
            dv_u = _dot_tn(pm.astype(BF16), dos[:, :LANES])
            dk_u = _dot_tn(dsb, qs)
            if nc == tk:
                dv_a, dk_a = dv_a + dv_u, dk_a + dk_u
            else:
                pad = lambda u: jnp.concatenate(
                    [jnp.zeros((n, u.shape[1]), F32) if z else u
                     for z, n in ((True, c0), (False, nc), (True, tk - c0 - nc)) if n], axis=0)
                dv_a, dk_a = dv_a + pad(dv_u), dk_a + pad(dk_u)
            dqx = _dot(dsb, kc)
            dq_acc[qsl, :] += jnp.where(low[:nr], dqx[:nr, :LANES], dqx[nr:, :LANES])
            dcx_ref[qsl, :] += (jnp.where(lane[:nr] == first, dqx[:nr, LANES:], 0.0)
                                + jnp.where(lane[:nr] == first + AUG_LANES, dqx[nr:, LANES:], 0.0))
            return dk_a, dv_a

        def q_block(i, carry, masked):
            return tile(i, carry, 0, 0, tk, masked)

        init =(jnp.zeros((tk, 2 * LANES), F32), jnp.zeros((tk, LANES), F32))
        carry = tile(j, init, 0, 0, half, True)
        carry = tile(j, carry, half, half, half, True)
        n_rest = nq - 1 - j

        def trip(t, c):
            for u in range(3):
                c = q_block(j + 1 + 3 * t + u, c, False)
            return c

        carry = lax.fori_loop(0, n_rest // 3, trip, carry)
        dk_a, dv_a = lax.fori_loop(j + 1 + 3 * (n_rest // 3), nq, lambda i, c: q_block(i, c, False), carry)
        dk_ref[...] = dk_a[:, :LANES].astype(BF16)
        dv_ref[...] = dv_a.astype(BF16)
        ksl = pl.ds(pl.multiple_of(j * tk, tk), tk)
        lk = _iota((tk, LANES), 1)
        dcx_ref[ksl, :] += jnp.where((lk == first + 3) | (lk == first + AUG_LANES + 3), dk_a[:, LANES:], 0.0)

        @pl.when(j == nk - 1)
        def _():
            dq_ref[...] = (dq_acc[...] * 0.125).astype(BF16)

        @pl.when((p == 3) & (j == nk - 1))
        def _():
            ex.wait(2, part_refs, land2, ex_sems)

    sh = jax.ShapeDtypeStruct((T, ATTN_WIDTH), BF16)
    full = lambda cb: pl.BlockSpec((T, LANES), lambda p, j: (0, cb + p))
    blk = lambda cb: pl.BlockSpec((tk, LANES), lambda p, j: (j, cb + p))
    hbm_spec = pl.BlockSpec(memory_space=pl.ANY)
    return pl.pallas_call(
        body, grid=(4, nk),
        out_shape=[sh, sh, sh, jax.ShapeDtypeStruct((T, LANES), F32)] + ex.land_shapes(2),
        in_specs=[full(0), full(0), _const_spec((T, LANES)), _const_spec((T, LANES)), blk(4), blk(8),
                  pl.BlockSpec((tk, LANES), lambda p, j: (j, 0))] + [hbm_spec] * nr,
        out_specs=[full(0), blk(0), blk(0), pl.BlockSpec((T, LANES), lambda p, j: (0, 0))] + [hbm_spec] * nr,
        scratch_shapes=[pltpu.VMEM((T, LANES), F32)] + ex.sem_shapes(2),
        compiler_params=_params(58, 2), name="bwd_attn",
    )(qkv, dya, qab, dab, qkv, qkv, kaug, *parts)


def _bwd_cum(dcx, fl, bfp):
    T = fl.shape[0]
    tb = CUM_TILE

    def body(dcx_ref, fl_ref, b_ref, dfl_ref, dbf_ref):
        triu = (_iota((tb, tb), 0) <= _iota((tb, tb), 1)).astype(F32)
        r, c = _iota((LANES, LANES), 0), _iota((LANES, LANES), 1)
        sel = (((r == AUG_LANES * c) & (c < N_HEADS)).astype(F32)
               - ((r == AUG_LANES * c + 3) & (c < N_HEADS)).astype(F32))
        carry = jnp.zeros((1, LANES), F32)
        dbf = jnp.zeros((1, LANES), F32)
        for i in reversed(range(T // tb)):
            colblk = jnp.dot(dcx_ref[i * tb:(i + 1) * tb, :], sel, precision=HIGHEST, preferred_element_type=F32)
            rc = jnp.dot(triu, colblk, precision=HIGHEST, preferred_element_type=F32) + carry
            carry = rc[0:1, :]
            sig = jax.nn.sigmoid(fl_ref[i * tb:(i + 1) * tb, :] + b_ref[...])
            dfl = rc * (1.0 - sig)
            dfl_ref[i * tb:(i + 1) * tb, :] = dfl.astype(BF16)
            dbf = dbf + jnp.sum(dfl, axis=0, keepdims=True)
        dbf_ref[...] = dbf

    return pl.pallas_call(
        body,
        out_shape=(jax.ShapeDtypeStruct((T, LANES), BF16), jax.ShapeDtypeStruct((1, LANES), F32)),
        compiler_params=pltpu.CompilerParams(vmem_limit_bytes=32 * MIB), name="bwd_cum",
    )(dcx, fl, bfp)


def _bwd_in(dz, dq, dk, dv, dfl, dgl, dh1, x2, g1, wz, wf, wg, rows, name, prev=None, stage=0, exchanged=None):
    T = x2.shape[0]
    tm = math.gcd(MATMUL_TILE, rows[0], rows[1] - rows[0])
    first = rows[0] // tm
    nsteps = (rows[1] - rows[0]) // tm
    ex = _GradExchange([tuple(exchanged.shape[1:])]) if stage else None

    def body(dz_ref, dq_ref, dk_ref, dv_ref, dfl_ref, dgl_ref, dh1_ref, x_ref, g_ref, wz_ref, wf_ref, wg_ref, *rest):
        rest = list(rest)
        dx_prev, dg1_prev = (rest.pop(0), rest.pop(0)) if prev else (None, None)
        src_ref = rest.pop(0) if stage else None
        dx_ref, dg1_ref = rest.pop(0), rest.pop(0)
        land_ref = rest.pop(0) if stage else None
        acc_g, ex_sems = rest[0], rest[1:]
        i = pl.program_id(0)

        @pl.when(i == 0)
        def _():
            if stage:
                ex.start(stage, [src_ref], [land_ref], ex_sems)
            acc_g[...] = jnp.zeros_like(acc_g)

        dxn = _dot_nt(dz_ref[...], wz_ref[:, 0:1024])
        dxn = dxn + _dot_nt(dq_ref[...], wz_ref[:, 1024:1536])
        dxn = dxn + _dot_nt(dk_ref[...], wz_ref[:, 1536:2048])
        dxn = dxn + _dot_nt(dv_ref[...], wz_ref[:, 2048:2560])
        dxn = dxn + _dot_nt(dfl_ref[...], wf_ref[...])
        dxn = dxn + _dot_nt(dgl_ref[...], wg_ref[...])
        r1, xh = _rms_stats(x_ref[...])
        acc_g[...] += _fold8(dxn * xh)
        dx_ref[...] = dh1_ref[...] + _rms_bwd(dxn, xh, r1, g_ref[...])

        @pl.when(i == nsteps - 1)
        def _():
            total = jnp.sum(acc_g[...], axis=0, keepdims=True)
            dg1_ref[...] = total + dg1_prev[...] if prev else total
            if stage:
                ex.wait(stage, [src_ref], [land_ref], ex_sems)

    hbm_spec = pl.BlockSpec(memory_space=pl.ANY)
    rows_spec = lambda n: pl.BlockSpec((tm, n), lambda i: (i + first, 0))
    operands = [dz, dq, dk, dv, dfl, dgl, dh1, x2, g1, wz, wf, wg]
    in_specs = [rows_spec(1024), rows_spec(512), rows_spec(512), rows_spec(512), rows_spec(LANES), rows_spec(2048),
                rows_spec(D_MODEL), rows_spec(D_MODEL),
                _const_spec((1, D_MODEL)), _const_spec((D_MODEL, ZQKV_WIDTH)), _const_spec((D_MODEL, LANES)),
                _const_spec((D_MODEL, 2048))]
    aliases = {}
    if prev:
        aliases = {len(operands): 0}
        operands += list(prev)
        in_specs += [hbm_spec, _const_spec((1, D_MODEL))]
    if stage:
        operands.append(exchanged)
        in_specs.append(hbm_spec)
    return pl.pallas_call(
        body, grid=(nsteps,),
        out_shape=[jax.ShapeDtypeStruct((T, D_MODEL), F32), jax.ShapeDtypeStruct((1, D_MODEL), F32)]
        + (ex.land_shapes(stage) if stage else []),
        in_specs=in_specs,
        out_specs=[rows_spec(D_MODEL), pl.BlockSpec((1, D_MODEL), lambda i: (0, 0))] + ([hbm_spec] if stage else []),
        scratch_shapes=[pltpu.VMEM((SUBLANES, D_MODEL), F32)] + (ex.sem_shapes(stage) if stage else []),
        input_output_aliases=aliases,
        compiler_params=_params(48), name=name,
    )(*operands)


def _in_hbm(v):
    return pltpu.with_memory_space_constraint(v, pltpu.HBM)


def _small_kernel_shapes(g_mix_pre, b_forget, g_sgu, b_sgu, w_spatial, b_spatial, g_mix_post, g_ffn_pre, g_ffn_post):
    return dict(g_mix_pre=g_mix_pre, b_forget=jnp.pad(b_forget, ((0, 0), (0, LANES - N_HEADS))), g_sgu=g_sgu,
                b_sgu=b_sgu, w_spatial=w_spatial[0], b_spatial=b_spatial[0], g_mix_post=g_mix_post,
                g_ffn_pre=g_ffn_pre, g_ffn_post=g_ffn_post)


def _small_output_shapes(d):
    out = dict(d)
    out.update(b_forget=d["b_forget"][:, :N_HEADS], w_spatial=d["w_spatial"][None], b_spatial=d["b_spatial"][None])
    return out


def kernel(x, g_mix_pre, w_in, b_forget, g_sgu, b_sgu, w_spatial, b_spatial, w_branch_sgu, w_branch_attn, w_out, g_mix_post, g_ffn_pre, w_up, w_down, g_ffn_post, loss_target, m_g_mix_pre, m_w_in, m_b_forget, m_g_sgu, m_b_sgu, m_w_spatial, m_b_spatial, m_w_branch_sgu, m_w_branch_attn, m_w_out, m_g_mix_post, m_g_ffn_pre, m_w_up, m_w_down, m_g_ffn_post, v_g_mix_pre, v_w_in, v_b_forget, v_g_sgu, v_b_sgu, v_w_spatial, v_b_spatial, v_w_branch_sgu, v_w_branch_attn, v_w_out, v_g_mix_post, v_g_ffn_pre, v_w_up, v_w_down, v_g_ffn_post):
    T = x.shape[1]
    x2 = x.reshape(T, D_MODEL)
    tgt = loss_target.reshape(T, D_MODEL)

    wg_in = _gather_w_in(w_in[0])
    wz, wf, wgt = _assemble_w_in(wg_in)
    bfp = jnp.pad(b_forget, ((0, 0), (0, LANES - N_HEADS)))
    wsp = w_spatial[0]
    bT = b_spatial[0].T

    xn, zuv, qkv, fl, gt = _fwd_in(x2, g_mix_pre, wz, wf, wgt)
    fl = _in_hbm(fl)
    cc, qaug, kaug = map(_in_hbm, _fwd_cum(fl, bfp))
    ys = _fwd_sgu(zuv, g_sgu, b_sgu, wsp, bT)
    ya, lse, wbs, wba, wo, wup, wdn = _fwd_attn(
        qkv, qaug, kaug, tuple(_in_hbm(w[0]) for w in (w_branch_sgu, w_branch_attn, w_out, w_up, w_down)))
    lse = _in_hbm(lse)
    A, B, mg, o, h1 = _fwd_merge(ys, ya, gt, x2, wbs, wba, wo, g_mix_post)
    xn2, a, ddn, dy, loss_part, dg4 = _fwd_ffn_loss(h1, tgt, wup, wdn, g_ffn_pre, g_ffn_post)

    da, dh1, dg3 = _bwd_ffn(ddn, a, dy, h1, wup, wdn, g_ffn_pre)
    dw_up = _wgrad(xn2, da, "wgrad_up", tn=2048, block_cols=512)
    dw_down = _wgrad(a, ddn, "wgrad_down", relu2=True)
    dgl, dys, dya, qab, dab, dg2, dw_bs, dw_ba, dw_out = _bwd_merge(
        dh1, o, A, B, gt, ys, ya, mg, lse, cc, wbs, wba, wo, g_mix_post)
    qab, dab = _in_hbm(qab), _in_hbm(dab)
    col_blocks = lambda g, w: g.reshape(g.shape[0], N_DEV, w).transpose(1, 0, 2)
    row_blocks = lambda g, r: g.reshape(N_DEV, r, g.shape[1])
    early_names = ["w_branch_sgu", "w_branch_attn", "w_out", "w_up", "w_down"]
    early = [col_blocks(dw_bs, 128), col_blocks(dw_ba, 128), row_blocks(dw_out, 128), dw_up, row_blocks(dw_down, 512)]
    owners = _owner_indices()
    dzuv, dwsp, dbT, dgs, dbs, *early_land1 = _bwd_sgu(zuv, dys, g_sgu, b_sgu, wsp, bT, early)
    early_parts = [_chip_partials(g, l1, owners, "chip_partials_" + nm)
                   for g, l1, nm in zip(early, early_land1, early_names)]
    dq, dk, dv, dcx, *early_land2 = _bwd_attn(qkv, dya, qab, dab, kaug, early_parts)
    dfl, dbf = _bwd_cum(_in_hbm(dcx), fl, bfp)
    dfl = _in_hbm(dfl)
    dw_z = _wgrad(xn, dzuv, "wgrad_in_z")
    dw_q, dw_k, dw_v, dw_f = _wgrad_multi(xn, [dq, dk, dv, dfl], "wgrad_in_qkvf")
    dw_g = _wgrad(xn, dgl, "wgrad_in_gate")
    blocks_in = _block_dw_in([dw_z, dw_q, dw_k, dw_v, dw_f, dw_g])
    bwd_in_args = (dzuv, dq, dk, dv, dfl, dgl, dh1, x2, g_mix_pre, wz, wf, wgt)
    dx, dg1, land1_in = _bwd_in(*bwd_in_args, (0, T // 8), "bwd_in_a", stage=1, exchanged=blocks_in)
    part_in = _chip_partials(blocks_in, land1_in, owners, "chip_partials_w_in")
    dx, dg1, land2_in = _bwd_in(*bwd_in_args, (T // 8, 3 * T // 4), "bwd_in_b", prev=(dx, dg1), stage=2,
                                exchanged=part_in)
    dx, dg1 = _bwd_in(*bwd_in_args, (3 * T // 4, T), "bwd_in_c", prev=(dx, dg1))

    tot_a, tot_b = _allreduce_small(dict(
        g_mix_pre=dg1, b_forget=dbf, g_sgu=dgs, b_sgu=dbs, w_spatial=dwsp, b_spatial=dbT, g_mix_post=dg2,
        g_ffn_pre=dg3, g_ffn_post=dg4), loss_part)
    r0, nr, c0, nc = LOSS_SLOT
    loss = jnp.sum(tot_a[r0:r0 + nr, c0:c0 + nc])
    small_w = _small_kernel_shapes(g_mix_pre, b_forget, g_sgu, b_sgu, w_spatial, b_spatial, g_mix_post, g_ffn_pre,
                                   g_ffn_post)
    small_m = _small_kernel_shapes(m_g_mix_pre, m_b_forget, m_g_sgu, m_b_sgu, m_w_spatial, m_b_spatial, m_g_mix_post,
                                   m_g_ffn_pre, m_g_ffn_post)
    small_v = _small_kernel_shapes(v_g_mix_pre, v_b_forget, v_g_sgu, v_b_sgu, v_w_spatial, v_b_spatial, v_g_mix_post,
                                   v_g_ffn_pre, v_g_ffn_post)
    sg, sd, sm, sv = (_small_output_shapes(d) for d in _adamw_small(tot_a, tot_b, small_w, small_m, small_v))

    big = {}
    g_in = _reduced_grad(blocks_in, land1_in, land2_in, owners, "reduced_grad_w_in")[:, :IN_SHARD]
    d_, m_, v_ = _adamw(_in_hbm(w_in[0]), g_in, _in_hbm(m_w_in[0]), _in_hbm(v_w_in[0]), "adamw_w_in")
    big["w_in"] = (g_in[None], d_[None], m_[None], v_[None])
    early_wmv = [(w_branch_sgu, m_w_branch_sgu, v_w_branch_sgu), (w_branch_attn, m_w_branch_attn, v_w_branch_attn),
                 (w_out, m_w_out, v_w_out), (w_up, m_w_up, v_w_up), (w_down, m_w_down, v_w_down)]
    for nm, (w, m, v), g, l1, l2 in zip(early_names, early_wmv, early, early_land1, early_land2):
        big[nm] = tuple(t[None] for t in _adamw_reduced(
            _in_hbm(w[0]), _in_hbm(m[0]), _in_hbm(v[0]), g, l1, l2, owners, "adamw_" + nm))

    order = ["g_mix_pre", "w_in", "b_forget", "g_sgu", "b_sgu", "w_spatial", "b_spatial", "w_branch_sgu",
             "w_branch_attn", "w_out", "g_mix_post", "g_ffn_pre", "w_up", "w_down", "g_ffn_post"]
    outs = [loss, dx.reshape(1, T, D_MODEL)]
    for kind, small in enumerate((sg, sd, sm, sv)):
        outs += [big[nm][kind] if nm in big else small[nm] for nm in order]
    return tuple(outs)
```
